```python
import math
import jax, jax.numpy as jnp
from jax import lax
import numpy as np

D_MODEL = 1024
BATCH = 8
SEQ = 2048
DEPTH = 1
DEC_BATCH = 128
DEC_SEQ = 1
PAST_LEN = 16384
PAGE_SIZE = 128

N_META = 16
D_MIX = 2 * D_MODEL
SSD_WIDTH = D_MODEL // 1
SSD_HEAD_DIM = 64
SSD_HEADS = SSD_WIDTH // SSD_HEAD_DIM
SSD_GROUPS = 2
SSD_HPG = SSD_HEADS // SSD_GROUPS
SSD_STATE = 128
SSD_CONV = 4
SSD_CHUNK = 128
SSD_CONV_DIM = SSD_WIDTH + 2 * SSD_GROUPS * SSD_STATE
S5_WIDTH = D_MIX - SSD_WIDTH
S5_GROUP_CH = 16
S5_GROUPS = S5_WIDTH // S5_GROUP_CH
S5_STATE = 64
IN_COLS = SSD_WIDTH + SSD_CONV_DIM + SSD_HEADS + S5_WIDTH
MOE_GROUPS = 4
MOE_EPG = 8
MOE_EXPERTS = MOE_GROUPS * MOE_EPG
MOE_TOP_K = 2
MOE_D_FF = 512
EPS = 1e-6

kernel_name = 'hymba_ssd_s5_hiermoe_step'


def _rmsnorm(x, g):
    xf = x.astype(jnp.float32)
    y = xf * lax.rsqrt(jnp.mean(xf * xf, axis=-1, keepdims=True) + EPS)
    return y.astype(x.dtype) * g


def _causal_conv(xbc, prefix, w, b):
    L = xbc.shape[1]
    xp = jnp.concatenate([prefix.astype(xbc.dtype), xbc], axis=1)
    out = b
    for k in range(SSD_CONV):
        out = out + xp[:, k:k + L] * w[k]
    return jax.nn.silu(out), xp[:, L:]


def _ssd_chunked(x, dt, A, B, C, D, h0, chunk):
    bsz, L = x.shape[0], x.shape[1]
    nc = L // chunk
    xc = x.reshape(bsz, nc, chunk, SSD_GROUPS, SSD_HPG, SSD_HEAD_DIM)
    dtc = dt.reshape(bsz, nc, chunk, SSD_GROUPS, SSD_HPG)
    Bc = B.reshape(bsz, nc, chunk, SSD_GROUPS, SSD_STATE)
    Cc = C.reshape(bsz, nc, chunk, SSD_GROUPS, SSD_STATE)
    Ag = A.reshape(SSD_GROUPS, SSD_HPG)
    dA = (dtc * Ag).astype(jnp.float32)
    cs = jnp.cumsum(dA, axis=2)
    seg = cs[:, :, :, None] - cs[:, :, None, :]
    mask = jnp.tril(jnp.ones((chunk, chunk), dtype=bool))[:, :, None, None]
    Lmat = jnp.exp(jnp.where(mask, seg, -jnp.inf))
    cb = jnp.einsum('bclgn,bcsgn->bclsg', Cc, Bc)
    M = cb[..., None] * Lmat * dtc[:, :, None]
    y_diag = jnp.einsum('bclsgk,bcsgkp->bclgkp', M, xc)
    decay_end = jnp.exp(cs[:, :, -1:] - cs)
    states = jnp.einsum('bclgn,bclgk,bclgkp->bcgkpn', Bc, decay_end * dtc, xc).astype(jnp.float32)
    chunk_decay = jnp.exp(cs[:, :, -1])

    def step(h, inp):
        s, d = inp
        return d[..., None, None] * h + s, h

    h_init = h0.reshape(bsz, SSD_GROUPS, SSD_HPG, SSD_HEAD_DIM, SSD_STATE).astype(jnp.float32)
    h_last, h_prev = lax.scan(step, h_init, (jnp.moveaxis(states, 1, 0), jnp.moveaxis(chunk_decay, 1, 0)))
    h_prev = jnp.moveaxis(h_prev, 0, 1)
    y_off = jnp.einsum('bclgn,bcgkpn,bclgk->bclgkp', Cc, h_prev, jnp.exp(cs))
    y = y_diag + y_off + D.reshape(SSD_GROUPS, SSD_HPG)[..., None] * xc
    y = y.reshape(bsz, L, SSD_HEADS, SSD_HEAD_DIM).astype(x.dtype)
    return y, h_last.reshape(bsz, SSD_HEADS, SSD_HEAD_DIM, SSD_STATE).astype(x.dtype)


def _s5_combine(e1, e2):
    a1r, a1i, b1r, b1i = e1
    a2r, a2i, b2r, b2i = e2
    ar = a1r * a2r - a1i * a2i
    ai = a1r * a2i + a1i * a2r
    br = a2r * b1r - a2i * b1i + b2r
    bi = a2r * b1i + a2i * b1r + b2i
    return (ar, ai, br, bi)


def _s5_group(u, h0_re, h0_im, a_re, a_im, log_dt, b_re, b_im, c_re, c_im, d, w_glu, b_glu, gain):
    bsz, L = u.shape[0], u.shape[1]
    ug = u.reshape(bsz, L, S5_GROUPS, S5_GROUP_CH).transpose(1, 0, 2, 3)
    dt = jnp.exp(log_dt)[:, None]
    mag = jnp.exp(a_re * dt)
    ab_re = mag * jnp.cos(a_im * dt)
    ab_im = mag * jnp.sin(a_im * dt)
    den = a_re * a_re + a_im * a_im
    nr = ab_re - 1.0
    q_re = (nr * a_re + ab_im * a_im) / den
    q_im = (ab_im * a_re - nr * a_im) / den
    bb_re = q_re[..., None] * b_re - q_im[..., None] * b_im
    bb_im = q_re[..., None] * b_im + q_im[..., None] * b_re
    bu_re = jnp.einsum('lbgc,gpc->lbgp', ug, bb_re)
    bu_im = jnp.einsum('lbgc,gpc->lbgp', ug, bb_im)
    a_seq_re = jnp.broadcast_to(ab_re[None, None], (L, 1, S5_GROUPS, S5_STATE))
    a_seq_im = jnp.broadcast_to(ab_im[None, None], (L, 1, S5_GROUPS, S5_STATE))
    acr, aci, hr, hi = lax.associative_scan(_s5_combine, (a_seq_re, a_seq_im, bu_re, bu_im), axis=0)
    hr = hr + acr * h0_re[None] - aci * h0_im[None]
    hi = hi + acr * h0_im[None] + aci * h0_re[None]
    y = jnp.einsum('lbgp,gcp->lbgc', hr, c_re) - jnp.einsum('lbgp,gcp->lbgc', hi, c_im) + d * ug
    y = y.transpose(1, 0, 2, 3).reshape(bsz, L, S5_WIDTH)
    y = jax.nn.gelu(y)
    y = y * jax.nn.sigmoid(y @ w_glu + b_glu)
    return _rmsnorm(y, gain), hr[-1], hi[-1]


def _hier_moe(x, rc_w, rc_b, rf_w, rf_b, w_gate, w_up, w_down):
    bsz, L, dm = x.shape
    t = x.reshape(-1, dm)
    lc = (t @ rc_w + rc_b).astype(jnp.float32)
    pc = jax.nn.softmax(lc, axis=-1)
    g_sel = jnp.argmax(lc, axis=-1)
    p_sel = jnp.take_along_axis(pc, g_sel[:, None], axis=-1)
    lf = jnp.einsum('td,gde->tge', t, rf_w) + rf_b
    lf_sel = jnp.take_along_axis(lf, g_sel[:, None, None], axis=1)[:, 0].astype(jnp.float32)
    pf = jax.nn.softmax(lf_sel, axis=-1)
    vals, idx = lax.top_k(pf, MOE_TOP_K)
    vals = vals / jnp.sum(vals, axis=-1, keepdims=True)
    eid = g_sel[:, None] * MOE_EPG + idx
    gates = p_sel * jnp.einsum('tk,tke->te', vals, jax.nn.one_hot(eid, MOE_EXPERTS, dtype=jnp.float32))
    gates = gates.astype(t.dtype)
    out = jnp.zeros_like(t)
    for e in range(MOE_EXPERTS):
        h = jax.nn.silu(t @ w_gate[e]) * (t @ w_up[e])
        out = out + gates[:, e:e + 1] * (h @ w_down[e])
    return out.reshape(bsz, L, dm)


def _layer(x, conv_prefix, ssm_h0, s5_h0_re, s5_h0_im, pad_front, chunk,
           norm_mix, w_in, conv_w, conv_b, dt_bias, a_log, d_ssd, ssd_norm,
           s5_a_re, s5_a_im, s5_log_dt, s5_b_re, s5_b_im, s5_c_re, s5_c_im, s5_d,
           w_glu, b_glu, s5_norm, w_out, norm_ffn, rc_w, rc_b, rf_w, rf_b,
           w_gate, w_up, w_down):
    bsz, L, _ = x.shape
    xn = _rmsnorm(x, norm_mix)
    proj = xn @ w_in
    z, xbc, dt_raw, u = jnp.split(
        proj, [SSD_WIDTH, SSD_WIDTH + SSD_CONV_DIM, SSD_WIDTH + SSD_CONV_DIM + SSD_HEADS], axis=-1)
    xbc, new_conv = _causal_conv(xbc, conv_prefix, conv_w, conv_b)
    xs, bm, cm = jnp.split(xbc, [SSD_WIDTH, SSD_WIDTH + SSD_GROUPS * SSD_STATE], axis=-1)
    dt = jax.nn.softplus(dt_raw + dt_bias)
    xs = xs.reshape(bsz, L, SSD_HEADS, SSD_HEAD_DIM)
    bm = bm.reshape(bsz, L, SSD_GROUPS, SSD_STATE)
    cm = cm.reshape(bsz, L, SSD_GROUPS, SSD_STATE)
    if pad_front > 0:
        padf = lambda a: jnp.pad(a, [(0, 0), (pad_front, 0)] + [(0, 0)] * (a.ndim - 2))
        xs, dt, bm, cm = padf(xs), padf(dt), padf(bm), padf(cm)
    y, new_ssm = _ssd_chunked(xs, dt, -jnp.exp(a_log), bm, cm, d_ssd, ssm_h0, chunk)
    y = y[:, pad_front:].reshape(bsz, L, SSD_WIDTH)
    y_ssd = _rmsnorm(y * jax.nn.silu(z), ssd_norm)
    y_s5, new_re, new_im = _s5_group(u, s5_h0_re, s5_h0_im, s5_a_re, s5_a_im, s5_log_dt,
                                     s5_b_re, s5_b_im, s5_c_re, s5_c_im, s5_d, w_glu, b_glu, s5_norm)
    x = x + jnp.concatenate([y_ssd, y_s5], axis=-1) @ w_out
    x = x + _hier_moe(_rmsnorm(x, norm_ffn), rc_w, rc_b, rf_w, rf_b, w_gate, w_up, w_down)
    return x, new_conv, new_ssm, new_re, new_im


def setup_inputs(seed: int = 0) -> dict:
    key = jax.random.key(seed)
    ks = iter(jax.random.split(key, 48))
    f32 = jnp.float32
    nrm = lambda shape, scale: jax.random.normal(next(ks), shape, f32) * scale
    P_ = DEPTH
    dt0 = jnp.exp(jax.random.uniform(next(ks), (P_, SSD_HEADS), f32, math.log(1e-3), math.log(1e-1)))
    dt_bias = dt0 + jnp.log(-jnp.expm1(-dt0))
    a_log = jnp.log(jax.random.uniform(next(ks), (P_, SSD_HEADS), f32, 1.0, 16.0))
    s5_a_re = -0.5 + nrm((P_, S5_GROUPS, S5_STATE), 0.01)
    s5_a_im = math.pi * jnp.arange(S5_STATE, dtype=f32)[None, None] + nrm((P_, S5_GROUPS, S5_STATE), 0.01)
    s5_log_dt = jax.random.uniform(next(ks), (P_, S5_GROUPS), f32, math.log(1e-3), math.log(1e-1))
    return {
        'x_prompt': nrm((BATCH, SEQ, D_MODEL), 1.0),
        'x_sample': nrm((DEC_BATCH, DEC_SEQ, D_MODEL), 1.0),
        'state_ssd_conv': nrm((DEPTH, DEC_BATCH, SSD_CONV - 1, SSD_CONV_DIM), 1.0),
        'state_ssd_ssm': nrm((DEPTH, DEC_BATCH, SSD_HEADS, SSD_HEAD_DIM, SSD_STATE), 0.5),
        'state_s5_re': nrm((DEPTH, DEC_BATCH, S5_GROUPS, S5_STATE), 1.0),
        'state_s5_im': nrm((DEPTH, DEC_BATCH, S5_GROUPS, S5_STATE), 1.0),
        'meta_tokens': nrm((N_META, D_MODEL), 1.0),
        'norm_mix': 1.0 + nrm((P_, D_MODEL), 0.02),
        'w_in': nrm((P_, D_MODEL, IN_COLS), D_MODEL ** -0.5),
        'conv_w': nrm((P_, SSD_CONV, SSD_CONV_DIM), 0.5),
        'conv_b': nrm((P_, SSD_CONV_DIM), 0.01),
        'dt_bias': dt_bias,
        'a_log': a_log,
        'd_ssd': 1.0 + nrm((P_, SSD_HEADS), 0.02),
        'ssd_norm': 1.0 + nrm((P_, SSD_WIDTH), 0.02),
        's5_a_re': s5_a_re,
        's5_a_im': s5_a_im,
        's5_log_dt': s5_log_dt,
        's5_b_re': nrm((P_, S5_GROUPS, S5_STATE, S5_GROUP_CH), (2 * S5_GROUP_CH) ** -0.5),
        's5_b_im': nrm((P_, S5_GROUPS, S5_STATE, S5_GROUP_CH), (2 * S5_GROUP_CH) ** -0.5),
        's5_c_re': nrm((P_, S5_GROUPS, S5_GROUP_CH, S5_STATE), (2 * S5_STATE) ** -0.5),
        's5_c_im': nrm((P_, S5_GROUPS, S5_GROUP_CH, S5_STATE), (2 * S5_STATE) ** -0.5),
        's5_d': nrm((P_, S5_GROUPS, S5_GROUP_CH), 0.5),
        'w_glu': nrm((P_, S5_WIDTH, S5_WIDTH), S5_WIDTH ** -0.5),
        'b_glu': nrm((P_, S5_WIDTH), 0.01),
        's5_norm': 1.0 + nrm((P_, S5_WIDTH), 0.02),
        'w_out': nrm((P_, D_MIX, D_MODEL), D_MIX ** -0.5),
        'norm_ffn': 1.0 + nrm((P_, D_MODEL), 0.02),
        'router_coarse_w': nrm((P_, D_MODEL, MOE_GROUPS), D_MODEL ** -0.5),
        'router_coarse_b': nrm((P_, MOE_GROUPS), 0.01),
        'router_fine_w': nrm((P_, MOE_GROUPS, D_MODEL, MOE_EPG), D_MODEL ** -0.5),
        'router_fine_b': nrm((P_, MOE_GROUPS, MOE_EPG), 0.01),
        'w_gate': nrm((P_, MOE_EXPERTS, D_MODEL, MOE_D_FF), D_MODEL ** -0.5),
        'w_up': nrm((P_, MOE_EXPERTS, D_MODEL, MOE_D_FF), D_MODEL ** -0.5),
        'w_down': nrm((P_, MOE_EXPERTS, MOE_D_FF, D_MODEL), MOE_D_FF ** -0.5),
        'norm_final': 1.0 + nrm((D_MODEL,), 0.02),
    }


def reference(x_prompt, x_sample, state_ssd_conv, state_ssd_ssm, state_s5_re, state_s5_im,
              meta_tokens, norm_mix, w_in, conv_w, conv_b, dt_bias, a_log, d_ssd, ssd_norm,
              s5_a_re, s5_a_im, s5_log_dt, s5_b_re, s5_b_im, s5_c_re, s5_c_im, s5_d,
              w_glu, b_glu, s5_norm, w_out, norm_ffn, router_coarse_w, router_coarse_b,
              router_fine_w, router_fine_b, w_gate, w_up, w_down, norm_final):
    bp = x_prompt.shape[0]
    meta = jnp.broadcast_to(meta_tokens.astype(x_prompt.dtype)[None], (bp, N_META, D_MODEL))
    xp = jnp.concatenate([meta, x_prompt], axis=1)
    pad_front = (-xp.shape[1]) % SSD_CHUNK
    xs = x_sample
    bs = xs.shape[0]
    conv_p, ssm_p, re_p, im_p = [], [], [], []
    conv_s, ssm_s, re_s, im_s = [], [], [], []
    for l in range(DEPTH):
        lp = (norm_mix[l], w_in[l], conv_w[l], conv_b[l], dt_bias[l], a_log[l], d_ssd[l], ssd_norm[l],
              s5_a_re[l], s5_a_im[l], s5_log_dt[l], s5_b_re[l], s5_b_im[l], s5_c_re[l], s5_c_im[l], s5_d[l],
              w_glu[l], b_glu[l], s5_norm[l], w_out[l], norm_ffn[l], router_coarse_w[l], router_coarse_b[l],
              router_fine_w[l], router_fine_b[l], w_gate[l], w_up[l], w_down[l])
        xp, c1, h1, r1, i1 = _layer(
            xp,
            jnp.zeros((bp, SSD_CONV - 1, SSD_CONV_DIM), xp.dtype),
            jnp.zeros((bp, SSD_HEADS, SSD_HEAD_DIM, SSD_STATE), xp.dtype),
            jnp.zeros((bp, S5_GROUPS, S5_STATE), xp.dtype),
            jnp.zeros((bp, S5_GROUPS, S5_STATE), xp.dtype),
            pad_front, SSD_CHUNK, *lp)
        xs, c2, h2, r2, i2 = _layer(
            xs, state_ssd_conv[l], state_ssd_ssm[l], state_s5_re[l], state_s5_im[l],
            0, xs.shape[1], *lp)
        conv_p.append(c1); ssm_p.append(h1); re_p.append(r1); im_p.append(i1)
        conv_s.append(c2); ssm_s.append(h2); re_s.append(r2); im_s.append(i2)
    y_prompt = _rmsnorm(xp, norm_final)[:, N_META:]
    y_sample = _rmsnorm(xs, norm_final)
    new_conv_prompt = jnp.stack(conv_p, 0)
    new_ssm_prompt = jnp.stack(ssm_p, 0)
    new_s5_re_prompt = jnp.stack(re_p, 0)
    new_s5_im_prompt = jnp.stack(im_p, 0)
    new_conv_sample = jnp.stack(conv_s, 0)
    new_ssm_sample = jnp.stack(ssm_s, 0)
    new_s5_re_sample = jnp.stack(re_s, 0)
    new_s5_im_sample = jnp.stack(im_s, 0)
    return (y_prompt, y_sample, new_conv_prompt, new_ssm_prompt, new_s5_re_prompt, new_s5_im_prompt,
            new_conv_sample, new_ssm_sample, new_s5_re_sample, new_s5_im_sample)
```

```python
import functools

import jax
import jax.numpy as jnp
from jax import lax
from jax.experimental import pallas as pl
from jax.experimental.pallas import tpu as pltpu

F32, BF16 = jnp.float32, jnp.bfloat16

D_MODEL = 1024
N_META = 16
SSD_WIDTH = 1024
SSD_HEAD_DIM = 64
SSD_HEADS = 16
SSD_GROUPS = 2
SSD_HPG = SSD_HEADS // SSD_GROUPS
SSD_STATE = 128
SSD_CONV = 4
SSD_CHUNK = 128
SSD_CONV_DIM = SSD_WIDTH + 2 * SSD_GROUPS * SSD_STATE
S5_WIDTH = 1024
S5_GROUP_CH = 16
S5_GROUPS = 64
S5_STATE = 64
S5_LANES = S5_GROUPS * S5_STATE
MOE_GROUPS = 4
MOE_EPG = 8
MOE_EXPERTS = MOE_GROUPS * MOE_EPG
MOE_D_FF = 512
EPS = 1e-6

LANES = 128
SUBLANES = 8
VMEM_LIMIT = 56 * 1024 * 1024

S5_TIME_TILE = 32
S5_SCAN_LANES = 512
MOE_TILE = 256
TOK_TILE = 512


def _dot(a, b):
    return jnp.dot(a, b, preferred_element_type=F32)


def _rms(x, g):
    return x * lax.rsqrt(jnp.mean(x * x, axis=-1, keepdims=True) + EPS) * g


def _softplus(x):
    return jnp.maximum(x, 0.0) + jnp.log1p(jnp.exp(-jnp.abs(x)))


def _split3(x):
    hi = x.astype(BF16)
    r = x - hi.astype(F32)
    mid = r.astype(BF16)
    lo = (r - mid.astype(F32)).astype(BF16)
    return hi, mid, lo


def _dot3(x, w):
    hi, mid, lo = _split3(x)
    return _dot(hi, w) + _dot(mid, w) + _dot(lo, w)


def _dot3_left(w, x):
    hi, mid, lo = _split3(x)
    return _dot(w, hi) + _dot(w, mid) + _dot(w, lo)


def _full_spec(a):
    nd = a.ndim
    return pl.BlockSpec(a.shape, lambda *_: (0,) * nd)


def _in_proj_body(x_ref, g_ref, wz_ref, wx_ref, wdt_ref, wu_ref, z_ref, xbc_ref, dt_ref, u_ref):
    xb = _rms(x_ref[...], g_ref[...]).astype(BF16)
    z_ref[...] = _dot(xb, wz_ref[...]).astype(z_ref.dtype)
    xbc_ref[...] = _dot(xb, wx_ref[...]).astype(xbc_ref.dtype)
    dt_ref[...] = _dot(xb, wdt_ref[...])
    u_ref[...] = _dot(xb, wu_ref[...]).astype(u_ref.dtype)


def _in_proj(x2d, g, wz, wx, wdt, wu, tm, act_dtype):
    rows = x2d.shape[0]
    row = lambda w: pl.BlockSpec((tm, w), lambda i: (i, 0))
    return pl.pallas_call(
        _in_proj_body,
        grid=(rows // tm,),
        in_specs=[row(D_MODEL), _full_spec(g), _full_spec(wz), _full_spec(wx), _full_spec(wdt), _full_spec(wu)],
        out_specs=[row(SSD_WIDTH), row(SSD_CONV_DIM), row(LANES), row(S5_WIDTH)],
        out_shape=[jax.ShapeDtypeStruct((rows, SSD_WIDTH), act_dtype),
                   jax.ShapeDtypeStruct((rows, SSD_CONV_DIM), act_dtype),
                   jax.ShapeDtypeStruct((rows, LANES), F32),
                   jax.ShapeDtypeStruct((rows, S5_WIDTH), act_dtype)],
        compiler_params=pltpu.CompilerParams(dimension_semantics=("parallel",), vmem_limit_bytes=VMEM_LIMIT),
        name="in_proj",
    )(x2d, g, wz, wx, wdt, wu)


def _ssd_body(mask_rows, xbc_ref, dt_ref, z_ref, cinit_ref, hinit_ref, cw_ref, cb_ref, dtb_ref, alog_ref,
              dexp_ref, nrm_ref, eexp_ref, y_ref, ctail_ref, st_ref, hto_ref, xwin, hT):
    c = pl.program_id(1)
    L = SSD_CHUNK

    @pl.when(c == 0)
    def _init():
        xwin[0:SUBLANES, :] = cinit_ref[0]
        hT[...] = hinit_ref[0]

    xwin[SUBLANES:SUBLANES + L, :] = xbc_ref[0].astype(F32)
    acc = cb_ref[...]
    for k in range(SSD_CONV):
        off = SUBLANES - (SSD_CONV - 1) + k
        acc = acc + xwin[off:off + L, :] * cw_ref[k:k + 1, :]
    tail = xwin[L:L + SUBLANES, :]
    xwin[0:SUBLANES, :] = tail
    ctail_ref[0] = tail

    xact = acc * jax.nn.sigmoid(acc)
    dt = _softplus(dt_ref[0] + dtb_ref[...])
    if mask_rows:
        valid = lax.broadcasted_iota(jnp.int32, (L, 1), 0) >= mask_rows
        xact = jnp.where(valid, xact, 0.0)
        dt = jnp.where(valid, dt, 0.0)

    a_neg = -jnp.exp(alog_ref[...])
    dA = dt * a_neg
    row_i = lax.broadcasted_iota(jnp.int32, (L, L), 0)
    col_i = lax.broadcasted_iota(jnp.int32, (L, L), 1)
    causal = row_i >= col_i
    tril = causal.astype(BF16)
    cs = _dot3_left(tril, dA)
    csT = cs.T
    dtT = dt.T
    ecs = jnp.exp(cs)
    wdec = jnp.exp(cs[L - 1:L, :] - cs) * dt
    eexp = eexp_ref[...]
    ecs_e = _dot3(ecs, eexp)
    wdec_e = _dot3(wdec, eexp)
    lane = lax.broadcasted_iota(jnp.int32, (L, LANES), 1)
    first_half = lane < SSD_HEAD_DIM

    gw = SSD_HPG * SSD_HEAD_DIM
    y_groups = []
    for g in range(SSD_GROUPS):
        b_g = xact[:, SSD_WIDTH + g * SSD_STATE: SSD_WIDTH + (g + 1) * SSD_STATE]
        c_g = xact[:, SSD_WIDTH + (SSD_GROUPS + g) * SSD_STATE: SSD_WIDTH + (SSD_GROUPS + g + 1) * SSD_STATE]
        b_b = b_g.astype(BF16)
        c_b = c_g.astype(BF16)
        cb = lax.dot_general(c_b, b_b, (((1,), (1,)), ((), ())), preferred_element_type=F32)
        xs_g = xact[:, g * gw:(g + 1) * gw]
        h_prev = hT[g]
        y_off = _dot(c_b, h_prev.astype(BF16)) * ecs_e[:, g * gw:(g + 1) * gw]
        xdec = (xs_g * wdec_e[:, g * gw:(g + 1) * gw]).astype(BF16)
        hT[g] = h_prev * ecs_e[L - 1:L, g * gw:(g + 1) * gw] + _dot(b_g.T.astype(BF16), xdec)
        pieces = []
        for j in range(SSD_HPG // 2):
            xs_pair = xs_g[:, j * LANES:(j + 1) * LANES]
            halves = (jnp.where(first_half, xs_pair, 0.0).astype(BF16),
                      jnp.where(first_half, 0.0, xs_pair).astype(BF16))
            yd = None
            for t in range(2):
                h = g * SSD_HPG + 2 * j + t
                seg = cs[:, h:h + 1] - csT[h:h + 1, :]
                lmat = jnp.exp(jnp.where(causal, seg, -jnp.inf))
                m = (cb * lmat * dtT[h:h + 1, :]).astype(BF16)
                part = _dot(m, halves[t])
                yd = part if yd is None else yd + part
            pieces.append(yd)
        y_groups.append(jnp.concatenate(pieces, axis=-1) + y_off + dexp_ref[:, g * gw:(g + 1) * gw] * xs_g)
    y = jnp.concatenate(y_groups, axis=-1)
    z = z_ref[0].astype(F32)
    y_ref[0] = _rms(y * (z * jax.nn.sigmoid(z)), nrm_ref[...]).astype(y_ref.dtype)

    @pl.when(c == pl.num_programs(1) - 1)
    def _emit():
        hto_ref[0] = hT[...]
        for g in range(SSD_GROUPS):
            t = hT[g].T
            for k in range(SSD_HPG):
                st_ref[0, g * SSD_HPG + k] = t[k * SSD_HEAD_DIM:(k + 1) * SSD_HEAD_DIM, :]


def _ssd_chunked(xbc, dt, z, cinit, hinit, cw, cb, dtb, alog, dexp, nrm, eexp, mask_rows):
    bsz, seq, _ = xbc.shape
    nc = seq // SSD_CHUNK
    gw = SSD_HPG * SSD_HEAD_DIM
    blk = lambda w: pl.BlockSpec((1, SSD_CHUNK, w), lambda b, c: (b, c, 0))
    return pl.pallas_call(
        functools.partial(_ssd_body, mask_rows),
        grid=(bsz, nc),
        in_specs=[blk(SSD_CONV_DIM), blk(LANES), blk(SSD_WIDTH),
                  pl.BlockSpec((1, SUBLANES, SSD_CONV_DIM), lambda b, c: (0, 0, 0)),
                  pl.BlockSpec((1, SSD_GROUPS, SSD_STATE, gw), lambda b, c: (0, 0, 0, 0)),
                  _full_spec(cw), _full_spec(cb), _full_spec(dtb), _full_spec(alog),
                  _full_spec(dexp), _full_spec(nrm), _full_spec(eexp)],
        out_specs=[blk(SSD_WIDTH),
                   pl.BlockSpec((1, SUBLANES, SSD_CONV_DIM), lambda b, c: (b, 0, 0)),
                   pl.BlockSpec((1, SSD_HEADS, SSD_HEAD_DIM, SSD_STATE), lambda b, c: (b, 0, 0, 0)),
                   pl.BlockSpec((1, SSD_GROUPS, SSD_STATE, gw), lambda b, c: (b, 0, 0, 0))],
        out_shape=[jax.ShapeDtypeStruct((bsz, seq, SSD_WIDTH), BF16),
                   jax.ShapeDtypeStruct((bsz, SUBLANES, SSD_CONV_DIM), F32),
                   jax.ShapeDtypeStruct((bsz, SSD_HEADS, SSD_HEAD_DIM, SSD_STATE), F32),
                   jax.ShapeDtypeStruct((bsz, SSD_GROUPS, SSD_STATE, gw), F32)],
        scratch_shapes=[pltpu.VMEM((SUBLANES + SSD_CHUNK, SSD_CONV_DIM), F32),
                        pltpu.VMEM((SSD_GROUPS, SSD_STATE, gw), F32)],
        compiler_params=pltpu.CompilerParams(dimension_semantics=("parallel", "arbitrary"),
                                             vmem_limit_bytes=VMEM_LIMIT),
        name="ssd_chunked",
    )(xbc, dt, z, cinit, hinit, cw, cb, dtb, alog, dexp, nrm, eexp)


def _ssd_step_prep_body(xbc_ref, c0_ref, c1_ref, c2_ref, dt_ref, cw_ref, cb_ref, dtb_ref, alog_ref, eexp_ref,
                        xdt_ref, dec_ref, bc_ref, xs_ref):
    acc = cb_ref[...]
    for k, r in enumerate((c0_ref, c1_ref, c2_ref, xbc_ref)):
        acc = acc + r[...] * cw_ref[k:k + 1, :]
    xact = acc * jax.nn.sigmoid(acc)
    xs = xact[:, :SSD_WIDTH]
    dt = _softplus(dt_ref[...] + dtb_ref[...])
    dec = jnp.exp(dt * -jnp.exp(alog_ref[...]))
    eexp = eexp_ref[...]
    xdt_ref[...] = xs * _dot3(dt, eexp)
    dec_ref[...] = _dot3(dec, eexp)
    bc_ref[...] = xact[:, SSD_WIDTH:]
    xs_ref[...] = xs


def _ssd_step_prep(xbc, c0, c1, c2, dt, cw, cb, dtb, alog, eexp):
    n = xbc.shape[0]
    args = (xbc, c0, c1, c2, dt, cw, cb, dtb, alog, eexp)
    return pl.pallas_call(
        _ssd_step_prep_body,
        in_specs=[_full_spec(a) for a in args],
        out_specs=[pl.BlockSpec((n, SSD_WIDTH), lambda: (0, 0)), pl.BlockSpec((n, SSD_WIDTH), lambda: (0, 0)),
                   pl.BlockSpec((n, 2 * SSD_GROUPS * SSD_STATE), lambda: (0, 0)),
                   pl.BlockSpec((n, SSD_WIDTH), lambda: (0, 0))],
        out_shape=[jax.ShapeDtypeStruct((n, SSD_WIDTH), F32), jax.ShapeDtypeStruct((n, SSD_WIDTH), F32),
                   jax.ShapeDtypeStruct((n, 2 * SSD_GROUPS * SSD_STATE), F32),
                   jax.ShapeDtypeStruct((n, SSD_WIDTH), F32)],
        compiler_params=pltpu.CompilerParams(vmem_limit_bytes=VMEM_LIMIT),
        name="ssd_step_prep",
    )(*args)


def _ssd_step_body(st_ref, xdt_ref, dec_ref, bc_ref, so_ref, yt_ref):
    for i in range(SUBLANES):
        for h in range(SSD_HEADS):
            g = h // SSD_HPG
            rows = slice(h * SSD_HEAD_DIM, (h + 1) * SSD_HEAD_DIM)
            xcol = xdt_ref[0, rows, i:i + 1]
            dcol = dec_ref[0, rows, i:i + 1]
            brow = bc_ref[i:i + 1, g * SSD_STATE:(g + 1) * SSD_STATE]
            crow = bc_ref[i:i + 1, (SSD_GROUPS + g) * SSD_STATE:(SSD_GROUPS + g + 1) * SSD_STATE]
            new = dcol * st_ref[i, h] + xcol * brow
            so_ref[i, h] = new
            yt_ref[0, rows, i:i + 1] = jnp.sum(new * crow, axis=-1, keepdims=True)


def _ssd_step(state, xdt_t, dec_t, bc):
    n = state.shape[0]
    st_spec = pl.BlockSpec((SUBLANES, SSD_HEADS, SSD_HEAD_DIM, SSD_STATE), lambda i: (i, 0, 0, 0))
    col_spec = pl.BlockSpec((1, SSD_WIDTH, SUBLANES), lambda i: (i, 0, 0))
    return pl.pallas_call(
        _ssd_step_body,
        grid=(n // SUBLANES,),
        in_specs=[st_spec, col_spec, col_spec, pl.BlockSpec((SUBLANES, bc.shape[1]), lambda i: (i, 0))],
        out_specs=[st_spec, col_spec],
        out_shape=[jax.ShapeDtypeStruct(state.shape, F32),
                   jax.ShapeDtypeStruct((n // SUBLANES, SSD_WIDTH, SUBLANES), F32)],
        compiler_params=pltpu.CompilerParams(dimension_semantics=("parallel",), vmem_limit_bytes=VMEM_LIMIT),
        name="ssd_step",
    )(state, xdt_t, dec_t, bc)


def _s5_project_in(u_b16, wb_ref, store):
    kw = 16 * S5_GROUP_CH
    nw = 16 * S5_STATE
    for j in range(S5_WIDTH // kw):
        r = _dot(u_b16[:, j * kw:(j + 1) * kw], wb_ref[j])
        store(j, r[:, :nw], r[:, nw:])


def _s5_tail(hre_of, him_of, u_f32, wcr_ref, wci_ref, d_ref, wglu_ref, bglu_ref, nrm_ref):
    cols = []
    for j in range(wcr_ref.shape[0]):
        cols.append(_dot(hre_of(j).astype(BF16), wcr_ref[j]) + _dot(him_of(j).astype(BF16), wci_ref[j]))
    y = jnp.concatenate(cols, axis=-1) + d_ref[...] * u_f32
    y = jax.nn.gelu(y)
    y = y * jax.nn.sigmoid(_dot(y.astype(BF16), wglu_ref[...]) + bglu_ref[...])
    return _rms(y, nrm_ref[...])


def _s5_seq_body(u_ref, um_ref, wb_ref, abr_ref, abi_ref, wcr_ref, wci_ref, d_ref, wglu_ref, bglu_ref, nrm_ref,
                 y_ref, sre_ref, sim_ref, bu, h):
    j = pl.program_id(0)
    bsz, lc = u_ref.shape[0], u_ref.shape[1]
    nw = 16 * S5_STATE

    def scan(nsteps):
        for k in range(S5_LANES // S5_SCAN_LANES):
            sl_r = pl.ds(k * S5_SCAN_LANES, S5_SCAN_LANES)
            sl_i = pl.ds(S5_LANES + k * S5_SCAN_LANES, S5_SCAN_LANES)
            ar = abr_ref[:, sl_r]
            ai = abi_ref[:, sl_r]

            def step(l, carry):
                hr, hi = carry
                nr = ar * hr - ai * hi + bu[l, :, sl_r]
                ni = ar * hi + ai * hr + bu[l, :, sl_i]
                bu[l, :, sl_r] = nr
                bu[l, :, sl_i] = ni
                return nr, ni

            hr, hi = lax.fori_loop(0, nsteps, step, (h[:, sl_r], h[:, sl_i]))
            h[:, sl_r] = hr
            h[:, sl_i] = hi

    @pl.when(j == 0)
    def _meta():
        h[...] = jnp.zeros_like(h)

        def store(jj, re, im):
            for b in range(bsz):
                bu[0:N_META, b, jj * nw:(jj + 1) * nw] = re
                bu[0:N_META, b, S5_LANES + jj * nw:S5_LANES + (jj + 1) * nw] = im

        _s5_project_in(um_ref[...], wb_ref, store)
        scan(N_META)

    u2 = u_ref[...].reshape(bsz * lc, S5_WIDTH)

    def store(jj, re, im):
        for b in range(bsz):
            bu[:, b, jj * nw:(jj + 1) * nw] = re[b * lc:(b + 1) * lc, :]
            bu[:, b, S5_LANES + jj * nw:S5_LANES + (jj + 1) * nw] = im[b * lc:(b + 1) * lc, :]

    _s5_project_in(u2, wb_ref, store)
    scan(lc)

    def slab(base):
        return lambda jj: jnp.concatenate(
            [bu[:, b, base + jj * nw:base + (jj + 1) * nw] for b in range(bsz)], axis=0)

    y = _s5_tail(slab(0), slab(S5_LANES), u2.astype(F32), wcr_ref, wci_ref, d_ref, wglu_ref, bglu_ref, nrm_ref)
    y_ref[...] = y.astype(y_ref.dtype).reshape(bsz, lc, S5_WIDTH)

    @pl.when(j == pl.num_programs(0) - 1)
    def _emit():
        sre_ref[...] = h[:, 0:S5_LANES]
        sim_ref[...] = h[:, S5_LANES:]


def _s5_seq(u, um, wb, abr, abi, wcr, wci, d, wglu, bglu, nrm):
    bsz, seq, _ = u.shape
    lc = S5_TIME_TILE
    consts = (um, wb, abr, abi, wcr, wci, d, wglu, bglu, nrm)
    blk = pl.BlockSpec((bsz, lc, S5_WIDTH), lambda j: (0, j, 0))
    st = pl.BlockSpec((bsz, S5_LANES), lambda j: (0, 0))
    return pl.pallas_call(
        _s5_seq_body,
        grid=(seq // lc,),
        in_specs=[blk] + [_full_spec(a) for a in consts],
        out_specs=[blk, st, st],
        out_shape=[jax.ShapeDtypeStruct((bsz, seq, S5_WIDTH), BF16),
                   jax.ShapeDtypeStruct((bsz, S5_LANES), F32), jax.ShapeDtypeStruct((bsz, S5_LANES), F32)],
        scratch_shapes=[pltpu.VMEM((lc, bsz, 2 * S5_LANES), F32), pltpu.VMEM((bsz, 2 * S5_LANES), F32)],
        compiler_params=pltpu.CompilerParams(dimension_semantics=("arbitrary",), vmem_limit_bytes=VMEM_LIMIT),
        name="s5_seq",
    )(u, *consts)


def _sample_post_body(yc_ref, xs_ref, z_ref, dexp_ref, snrm_ref, u_ref, hr_ref, hi_ref, wb_ref, abr_ref, abi_ref,
                      wcr_ref, wci_ref, d_ref, wglu_ref, bglu_ref, nrm_ref,
                      yssd_ref, ys5_ref, nre_ref, nim_ref):
    z = z_ref[...]
    y = yc_ref[...] + dexp_ref[...] * xs_ref[...]
    yssd_ref[...] = _rms(y * (z * jax.nn.sigmoid(z)), snrm_ref[...]).astype(yssd_ref.dtype)

    u = u_ref[...]
    nw = 16 * S5_STATE
    ar, ai = abr_ref[...], abi_ref[...]

    def store(jj, re, im):
        sl = slice(jj * nw, (jj + 1) * nw)
        h0r, h0i = hr_ref[:, sl], hi_ref[:, sl]
        nre_ref[:, sl] = ar[:, sl] * h0r - ai[:, sl] * h0i + re
        nim_ref[:, sl] = ar[:, sl] * h0i + ai[:, sl] * h0r + im

    _s5_project_in(u.astype(BF16), wb_ref, store)
    slab = lambda ref: (lambda jj: ref[:, jj * nw:(jj + 1) * nw])
    y5 = _s5_tail(slab(nre_ref), slab(nim_ref), u, wcr_ref, wci_ref, d_ref, wglu_ref, bglu_ref, nrm_ref)
    ys5_ref[...] = y5.astype(ys5_ref.dtype)


def _sample_post(yc, xs, z, dexp, snrm, u, h0r, h0i, wb, abr1, abi1, wcr, wci, d, wglu, bglu, nrm):
    n = yc.shape[0]
    args = (yc, xs, z, dexp, snrm, u, h0r, h0i, wb, abr1, abi1, wcr, wci, d, wglu, bglu, nrm)
    spec = lambda w: pl.BlockSpec((n, w), lambda: (0, 0))
    return pl.pallas_call(
        _sample_post_body,
        in_specs=[_full_spec(a) for a in args],
        out_specs=[spec(SSD_WIDTH), spec(S5_WIDTH), spec(S5_LANES), spec(S5_LANES)],
        out_shape=[jax.ShapeDtypeStruct((n, SSD_WIDTH), BF16), jax.ShapeDtypeStruct((n, S5_WIDTH), BF16),
                   jax.ShapeDtypeStruct((n, S5_LANES), F32), jax.ShapeDtypeStruct((n, S5_LANES), F32)],
        compiler_params=pltpu.CompilerParams(vmem_limit_bytes=VMEM_LIMIT),
        name="sample_post",
    )(*args)


def _mix_route_body(n_blocks, *refs):
    i = pl.program_id(0)

    @pl.when(i < n_blocks)
    def _compute():
        _mix_route_compute(*refs)

    @pl.when(i >= n_blocks)
    def _fill():
        for ref in refs[-3:]:
            ref[...] = jnp.zeros_like(ref)


def _mix_route_compute(x_ref, ys_ref, y5_ref, wa_ref, wb_ref, nf_ref, wrh_ref, wrl_ref, br_ref, *rest):
    x1_ref, xn_ref, rt_ref = rest[-3:]
    x1 = x_ref[...] + _dot(ys_ref[...], wa_ref[...]) + _dot(y5_ref[...], wb_ref[...])
    x1_ref[...] = x1
    xn = _rms(x1, nf_ref[...])
    xn_ref[...] = xn

    xh = xn.astype(BF16)
    xl = (xn - xh.astype(F32)).astype(BF16)
    logits = _dot(xh, wrh_ref[...]) + _dot(xl, wrh_ref[...]) + _dot(xh, wrl_ref[...]) + br_ref[...]
    lane = lax.broadcasted_iota(jnp.int32, logits.shape, 1).astype(F32)
    neg = -jnp.inf
    big = float(LANES)

    def first_max(v):
        m = jnp.max(v, axis=-1, keepdims=True)
        return m, jnp.min(jnp.where(v == m, lane, big), axis=-1, keepdims=True)

    coarse = lane < MOE_GROUPS
    mc, gsel = first_max(jnp.where(coarse, logits, neg))
    psel = 1.0 / jnp.sum(jnp.where(coarse, jnp.exp(logits - mc), 0.0), axis=-1, keepdims=True)
    lo = MOE_GROUPS + MOE_EPG * gsel
    lf = jnp.where((lane >= lo) & (lane < lo + MOE_EPG), logits, neg)
    m1, i1 = first_max(lf)
    m2, i2 = first_max(jnp.where(lane == i1, neg, lf))
    e2 = jnp.exp(m2 - m1)
    g1 = psel / (1.0 + e2)
    g2 = psel * e2 / (1.0 + e2)
    rt_ref[...] = jnp.where(lane == 0.0, i1 - MOE_GROUPS,
                            jnp.where(lane == 1.0, i2 - MOE_GROUPS,
                                      jnp.where(lane == 2.0, g1, jnp.where(lane == 3.0, g2, 0.0))))


def _mix_route(x, ys, y5, wa, wb, nf, wrh, wrl, br, tm, total_rows, row_block_offset, bufs):
    n_blocks = x.shape[0] // tm
    fill_tail = bufs is None and n_blocks * tm < total_rows
    row = lambda w: pl.BlockSpec((tm, w), lambda i: (jnp.minimum(i, n_blocks - 1), 0))
    out_row = lambda w: pl.BlockSpec((tm, w), lambda i: (i + row_block_offset, 0))
    consts = (wa, wb, nf, wrh, wrl, br)
    in_specs = [row(D_MODEL), row(SSD_WIDTH), row(S5_WIDTH)] + [_full_spec(a) for a in consts]
    args = [x, ys, y5, *consts]
    aliases = {}
    if bufs is not None:
        for k, b in enumerate(bufs):
            in_specs.append(pl.BlockSpec(memory_space=pl.ANY))
            aliases[len(args)] = k
            args.append(b)
    return pl.pallas_call(
        functools.partial(_mix_route_body, n_blocks),
        grid=(n_blocks + int(fill_tail),),
        in_specs=in_specs,
        out_specs=[out_row(D_MODEL), out_row(D_MODEL), out_row(LANES)],
        out_shape=[jax.ShapeDtypeStruct((total_rows, D_MODEL), F32), jax.ShapeDtypeStruct((total_rows, D_MODEL), F32),
                   jax.ShapeDtypeStruct((total_rows, LANES), F32)],
        input_output_aliases=aliases,
        compiler_params=pltpu.CompilerParams(dimension_semantics=("parallel",), vmem_limit_bytes=VMEM_LIMIT),
        name="mix_route",
    )(*args)


def _gather_rows(idx_ref, base, n, src_hbm, dst, sems):
    def copy(r):
        return pltpu.make_async_copy(src_hbm.at[pl.ds(idx_ref[base + r], 1)], dst.at[pl.ds(r, 1)], sems.at[r])

    def start(r, carry):
        copy(r).start()
        return carry

    def wait(r, carry):
        copy(r).wait()
        return carry

    lax.fori_loop(0, n, start, 0)
    lax.fori_loop(0, n, wait, 0)


def _moe_ffn_body(te_ref, src_ref, nused_ref, xn_hbm, wg_ref, wu_ref, wd_ref, y_ref, xbuf, wgb, wub, wdb, sems):
    i = pl.program_id(0)

    @pl.when(i >= nused_ref[0])
    def _unused_tile():
        y_ref[...] = jnp.zeros_like(y_ref)

    @pl.when(i < nused_ref[0])
    def _tile():
        @pl.when((i == 0) | (te_ref[i] != te_ref[jnp.maximum(i - 1, 0)]))
        def _cast_weights():
            wgb[...] = wg_ref[0].astype(BF16)
            wub[...] = wu_ref[0].astype(BF16)
            wdb[...] = wd_ref[0].astype(BF16)

        _gather_rows(src_ref, i * MOE_TILE, MOE_TILE, xn_hbm, xbuf, sems)
        x = xbuf[...].astype(BF16)
        gate = _dot(x, wgb[...])
        hmid = (gate * jax.nn.sigmoid(gate)) * _dot(x, wub[...])
        y_ref[...] = _dot(hmid.astype(BF16), wdb[...])


def _moe_ffn(tile_expert, src_tok, n_used, xn, w_gate, w_up, w_down, n_tiles):
    wspec = lambda s: pl.BlockSpec((1,) + s, lambda i, te, src, nu: (te[i], 0, 0))
    return pl.pallas_call(
        _moe_ffn_body,
        grid_spec=pltpu.PrefetchScalarGridSpec(
            num_scalar_prefetch=3,
            grid=(n_tiles,),
            in_specs=[pl.BlockSpec(memory_space=pl.ANY),
                      wspec((D_MODEL, MOE_D_FF)), wspec((D_MODEL, MOE_D_FF)), wspec((MOE_D_FF, D_MODEL))],
            out_specs=pl.BlockSpec((MOE_TILE, D_MODEL), lambda i, te, src, nu: (i, 0)),
            scratch_shapes=[pltpu.VMEM((MOE_TILE, D_MODEL), F32),
                            pltpu.VMEM((D_MODEL, MOE_D_FF), BF16), pltpu.VMEM((D_MODEL, MOE_D_FF), BF16),
                            pltpu.VMEM((MOE_D_FF, D_MODEL), BF16),
                            pltpu.SemaphoreType.DMA((MOE_TILE,))]),
        out_shape=jax.ShapeDtypeStruct((n_tiles * MOE_TILE, D_MODEL), F32),
        compiler_params=pltpu.CompilerParams(dimension_semantics=("arbitrary",), vmem_limit_bytes=VMEM_LIMIT),
        name="moe_ffn",
    )(tile_expert, src_tok, n_used, xn, w_gate, w_up, w_down)


def _combine_body(tm, row_block_offset, pos_ref, x1_ref, rt_ref, ys_hbm, nf_ref, out_ref, ybuf, sems):
    i = pl.program_id(0)
    rows = pl.num_programs(0) * tm
    for k in range(2):
        _gather_rows(pos_ref, k * rows + i * tm, tm, ys_hbm, ybuf.at[k], sems)
    rt = rt_ref[...]
    x2 = x1_ref[...] + rt[:, 2:3] * ybuf[0] + rt[:, 3:4] * ybuf[1]
    out_ref[...] = _rms(x2, nf_ref[...])


def _combine(pos, x1, rt, ysorted, nf, tm, rows, row_block_offset):
    row = lambda w: pl.BlockSpec((tm, w), lambda i, p: (i + row_block_offset, 0))
    return pl.pallas_call(
        functools.partial(_combine_body, tm, row_block_offset),
        grid_spec=pltpu.PrefetchScalarGridSpec(
            num_scalar_prefetch=1,
            grid=(rows // tm,),
            in_specs=[row(D_MODEL), row(LANES), pl.BlockSpec(memory_space=pl.ANY),
                      pl.BlockSpec((1, D_MODEL), lambda i, p: (0, 0))],
            out_specs=pl.BlockSpec((tm, D_MODEL), lambda i, p: (i, 0)),
            scratch_shapes=[pltpu.VMEM((2, tm, D_MODEL), F32), pltpu.SemaphoreType.DMA((tm,))]),
        out_shape=jax.ShapeDtypeStruct((rows, D_MODEL), F32),
        compiler_params=pltpu.CompilerParams(dimension_semantics=("arbitrary",), vmem_limit_bytes=VMEM_LIMIT),
        name="moe_combine",
    )(pos, x1, rt, ysorted, nf)


def _route_tables(eid, n_tiles):
    n_tok = eid.shape[0]
    flat = eid.reshape(-1)
    n = flat.shape[0]
    order = jnp.argsort(flat, stable=True).astype(jnp.int32)
    sorted_e = flat[order]
    counts = jnp.sum((flat[:, None] == jnp.arange(MOE_EXPERTS, dtype=jnp.int32)[None, :]).astype(jnp.int32), axis=0)
    start = jnp.cumsum(counts) - counts
    tiles_per = (counts + MOE_TILE - 1) // MOE_TILE
    tile_end = jnp.cumsum(tiles_per)
    pstart = (tile_end - tiles_per) * MOE_TILE
    dest = pstart[sorted_e] + (jnp.arange(n, dtype=jnp.int32) - start[sorted_e])
    src_tok = jnp.zeros((n_tiles * MOE_TILE,), jnp.int32).at[dest].set(order // 2)
    pos = jnp.zeros((n,), jnp.int32).at[order].set(dest).reshape(n_tok, 2)
    n_used = tile_end[-1]
    tile_ids = jnp.minimum(jnp.arange(n_tiles, dtype=jnp.int32), n_used - 1)
    tile_expert = jnp.searchsorted(tile_end, tile_ids, side="right").astype(jnp.int32)
    return src_tok, pos, tile_expert, n_used.reshape(1).astype(jnp.int32)


def _s5_tables(a_re, a_im, log_dt, b_re, b_im, c_re, c_im):
    dt = jnp.exp(log_dt)[:, None]
    mag = jnp.exp(a_re * dt)
    ab_re = mag * jnp.cos(a_im * dt)
    ab_im = mag * jnp.sin(a_im * dt)
    den = a_re * a_re + a_im * a_im
    nr = ab_re - 1.0
    q_re = (nr * a_re + ab_im * a_im) / den
    q_im = (ab_im * a_re - nr * a_im) / den
    bb_re = q_re[..., None] * b_re - q_im[..., None] * b_im
    bb_im = q_re[..., None] * b_im + q_im[..., None] * b_re
    eye = jnp.eye(16, dtype=F32)
    nblk = S5_GROUPS // 16

    def in_map(bb):
        w = jnp.einsum("jgpc,gh->jgchp", bb.reshape(nblk, 16, S5_STATE, S5_GROUP_CH), eye)
        return w.reshape(nblk, 16 * S5_GROUP_CH, 16 * S5_STATE)

    def out_map(cc):
        w = jnp.einsum("jgcp,gh->jgphc", cc.reshape(nblk, 16, S5_GROUP_CH, S5_STATE), eye)
        return w.reshape(nblk, 16 * S5_STATE, 16 * S5_GROUP_CH)

    wb = jnp.concatenate([in_map(bb_re), in_map(bb_im)], axis=-1).astype(BF16)
    return (wb, ab_re.reshape(1, S5_LANES), ab_im.reshape(1, S5_LANES),
            out_map(c_re).astype(BF16), out_map(-c_im).astype(BF16))


def kernel(x_prompt, x_sample, state_ssd_conv, state_ssd_ssm, state_s5_re, state_s5_im, meta_tokens, norm_mix, w_in, conv_w, conv_b, dt_bias, a_log, d_ssd, ssd_norm, s5_a_re, s5_a_im, s5_log_dt, s5_b_re, s5_b_im, s5_c_re, s5_c_im, s5_d, w_glu, b_glu, s5_norm, w_out, norm_ffn, router_coarse_w, router_coarse_b, router_fine_w, router_fine_b, w_gate, w_up, w_down, norm_final):
    bp, seq, _ = x_prompt.shape
    bs = x_sample.shape[0]
    n_prompt = bp * seq
    n_tok = n_prompt + bs
    row2 = lambda v: v.reshape(1, -1)
    pad_heads = lambda v: jnp.pad(v, (0, LANES - SSD_HEADS)).reshape(1, LANES)

    w = w_in[0]
    o1, o2, o3 = SSD_WIDTH, SSD_WIDTH + SSD_CONV_DIM, SSD_WIDTH + SSD_CONV_DIM + SSD_HEADS
    wz, wx, wu = w[:, :o1].astype(BF16), w[:, o1:o2].astype(BF16), w[:, o3:].astype(BF16)
    wdt = jnp.pad(w[:, o2:o3], ((0, 0), (0, LANES - SSD_HEADS))).astype(BF16)
    g_mix = row2(norm_mix[0])
    cw, cb = conv_w[0], row2(conv_b[0])
    dtb, alog = pad_heads(dt_bias[0]), pad_heads(a_log[0])
    dexp = row2(jnp.repeat(d_ssd[0], SSD_HEAD_DIM))
    snrm = row2(ssd_norm[0])
    eexp = (jnp.arange(LANES)[:, None] == (jnp.arange(SSD_WIDTH) // SSD_HEAD_DIM)[None, :]).astype(BF16)
    wb5, ab_re, ab_im, wcr, wci = _s5_tables(s5_a_re[0], s5_a_im[0], s5_log_dt[0], s5_b_re[0], s5_b_im[0],
                                             s5_c_re[0], s5_c_im[0])
    d5, wglu, bglu, nrm5 = row2(s5_d[0]), w_glu[0].astype(BF16), row2(b_glu[0]), row2(s5_norm[0])
    wo_a, wo_b = w_out[0][:SSD_WIDTH].astype(BF16), w_out[0][SSD_WIDTH:].astype(BF16)
    w_r = jnp.concatenate([router_coarse_w[0], router_fine_w[0].transpose(1, 0, 2).reshape(D_MODEL, MOE_EXPERTS)], axis=1)
    w_r = jnp.pad(w_r, ((0, 0), (0, LANES - w_r.shape[1])))
    wrh = w_r.astype(BF16)
    wrl = (w_r - wrh.astype(F32)).astype(BF16)
    b_r = jnp.concatenate([router_coarse_b[0], router_fine_b[0].reshape(-1)])
    b_r = jnp.pad(b_r, (0, LANES - b_r.shape[0])).reshape(1, LANES)

    zp, xbcp, dtp, up = _in_proj(x_prompt.reshape(n_prompt, D_MODEL), g_mix, wz, wx, wdt, wu, TOK_TILE, BF16)
    xsm = jnp.concatenate([x_sample.reshape(bs, D_MODEL), meta_tokens], axis=0)
    zs, xbcs, dts, us = _in_proj(xsm, g_mix, wz, wx, wdt, wu, xsm.shape[0], F32)

    front = SSD_CHUNK - N_META
    padf = lambda a: jnp.pad(a[bs:], ((front, 0), (0, 0)))[None]
    gw = SSD_HPG * SSD_HEAD_DIM
    ssd_consts = (cw, cb, dtb, alog, dexp, snrm, eexp)
    _, ctail_m, _, ht_m = _ssd_chunked(
        padf(xbcs), padf(dts), jnp.zeros((1, SSD_CHUNK, SSD_WIDTH), F32),
        jnp.zeros((1, SUBLANES, SSD_CONV_DIM), F32), jnp.zeros((1, SSD_GROUPS, SSD_STATE, gw), F32),
        *ssd_consts, mask_rows=front)
    y_ssd_p, ctail_p, ssm_p, _ = _ssd_chunked(
        xbcp.reshape(bp, seq, SSD_CONV_DIM), dtp.reshape(bp, seq, LANES), zp.reshape(bp, seq, SSD_WIDTH),
        ctail_m, ht_m, *ssd_consts, mask_rows=0)

    abr8, abi8 = jnp.broadcast_to(ab_re, (bp, S5_LANES)), jnp.broadcast_to(ab_im, (bp, S5_LANES))
    y_s5_p, s5re_p, s5im_p = _s5_seq(up.reshape(bp, seq, S5_WIDTH), us[bs:].astype(BF16), wb5, abr8, abi8,
                                     wcr, wci, d5, wglu, bglu, nrm5)

    cst = state_ssd_conv[0]
    xdt, dec, bc, xs_s = _ssd_step_prep(xbcs[:bs], cst[:, 0], cst[:, 1], cst[:, 2], dts[:bs], cw, cb, dtb, alog, eexp)
    cols = lambda a: a.reshape(bs // SUBLANES, SUBLANES, SSD_WIDTH).transpose(0, 2, 1)
    ssm_s, y_t = _ssd_step(state_ssd_ssm[0], cols(xdt), cols(dec), bc)
    y_core = y_t.transpose(0, 2, 1).reshape(bs, SSD_WIDTH)
    y_ssd_s, y_s5_s, s5re_s, s5im_s = _sample_post(
        y_core, xs_s, zs[:bs], dexp, snrm, us[:bs], state_s5_re[0].reshape(bs, S5_LANES),
        state_s5_im[0].reshape(bs, S5_LANES), wb5, ab_re, ab_im, wcr, wci, d5, wglu, bglu, nrm5)

    route_consts = (wo_a, wo_b, row2(norm_ffn[0]), wrh, wrl, b_r)
    bufs = _mix_route(x_prompt.reshape(n_prompt, D_MODEL), y_ssd_p.reshape(n_prompt, SSD_WIDTH),
                      y_s5_p.reshape(n_prompt, S5_WIDTH), *route_consts, TOK_TILE, n_tok, 0, None)
    x1, xn, rt = _mix_route(x_sample.reshape(bs, D_MODEL), y_ssd_s, y_s5_s, *route_consts, bs, n_tok,
                            n_prompt // bs, bufs)

    n_tiles = -(-2 * n_tok // MOE_TILE) + MOE_EXPERTS
    eid = jnp.clip(rt[:, 0:2].astype(jnp.int32), 0, MOE_EXPERTS - 1)
    src_tok, pos, tile_expert, n_used = _route_tables(eid, n_tiles)
    ysorted = _moe_ffn(tile_expert, src_tok, n_used, xn, w_gate[0], w_up[0], w_down[0], n_tiles)
    nfin = row2(norm_final)
    ctile = 256
    y_p = _combine(pos[:n_prompt].T.reshape(-1), x1, rt, ysorted, nfin, ctile, n_prompt, 0)
    y_s = _combine(pos[n_prompt:].T.reshape(-1), x1, rt, ysorted, nfin, bs, bs, n_prompt // bs)

    s5_state = lambda a, b: a.reshape(1, b, S5_GROUPS, S5_STATE)
    new_conv_s = jnp.stack([cst[:, 1], cst[:, 2], xbcs[:bs]], axis=1)[None]
    return (y_p.reshape(bp, seq, D_MODEL), y_s.reshape(bs, 1, D_MODEL),
            ctail_p[:, SUBLANES - (SSD_CONV - 1):][None], ssm_p[None], s5_state(s5re_p, bp), s5_state(s5im_p, bp),
            new_conv_s, ssm_s[None], s5_state(s5re_s, bs), s5_state(s5im_s, bs))
```

```python
import functools

import jax
import jax.numpy as jnp
from jax import lax
from jax.experimental import pallas as pl
from jax.experimental.pallas import tpu as pltpu

F32, BF16 = jnp.float32, jnp.bfloat16

D_MODEL = 1024
N_META = 16
SSD_WIDTH = 1024
SSD_HEAD_DIM = 64
SSD_HEADS = 16
SSD_GROUPS = 2
SSD_HPG = SSD_HEADS // SSD_GROUPS
SSD_STATE = 128
SSD_CONV = 4
SSD_CHUNK = 128
SSD_CONV_DIM = SSD_WIDTH + 2 * SSD_GROUPS * SSD_STATE
S5_WIDTH = 1024
S5_GROUP_CH = 16
S5_GROUPS = 64
S5_STATE = 64
S5_LANES = S5_GROUPS * S5_STATE
MOE_GROUPS = 4
MOE_EPG = 8
MOE_EXPERTS = MOE_GROUPS * MOE_EPG
MOE_D_FF = 512
EPS = 1e-6

LANES = 128
SUBLANES = 8
VMEM_LIMIT = 56 * 1024 * 1024

S5_TIME_TILE = 32
S5_SCAN_LANES = 512
MOE_TILE = 256
SLAB_ROWS = D_MODEL // LANES
DISPATCH_BATCH = 256
DMA_UNROLL = 8
TOK_TILE = 512


def _dot(a, b):
    return jnp.dot(a, b, preferred_element_type=F32)


def _rms(x, g):
    return x * lax.rsqrt(jnp.mean(x * x, axis=-1, keepdims=True) + EPS) * g


def _softplus(x):
    return jnp.maximum(x, 0.0) + jnp.log1p(jnp.exp(-jnp.abs(x)))


def _split3(x):
    hi = x.astype(BF16)
    r = x - hi.astype(F32)
    mid = r.astype(BF16)
    lo = (r - mid.astype(F32)).astype(BF16)
    return hi, mid, lo


def _dot3(x, w):
    hi, mid, lo = _split3(x)
    return _dot(hi, w) + _dot(mid, w) + _dot(lo, w)


def _dot3_left(w, x):
    hi, mid, lo = _split3(x)
    return _dot(w, hi) + _dot(w, mid) + _dot(w, lo)


def _full_spec(a):
    nd = a.ndim
    return pl.BlockSpec(a.shape, lambda *_: (0,) * nd)


def _in_proj_body(x_ref, g_ref, wz_ref, wx_ref, wdt_ref, wu_ref, z_ref, xbc_ref, dt_ref, u_ref):
    xb = _rms(x_ref[...], g_ref[...]).astype(BF16)
    z_ref[...] = _dot(xb, wz_ref[...]).astype(z_ref.dtype)
    xbc_ref[...] = _dot(xb, wx_ref[...]).astype(xbc_ref.dtype)
    dt_ref[...] = _dot(xb, wdt_ref[...])
    u_ref[...] = _dot(xb, wu_ref[...]).astype(u_ref.dtype)


def _in_proj(x2d, g, wz, wx, wdt, wu, tm, act_dtype):
    rows = x2d.shape[0]
    row = lambda w: pl.BlockSpec((tm, w), lambda i: (i, 0))
    return pl.pallas_call(
        _in_proj_body,
        grid=(rows // tm,),
        in_specs=[row(D_MODEL), _full_spec(g), _full_spec(wz), _full_spec(wx), _full_spec(wdt), _full_spec(wu)],
        out_specs=[row(SSD_WIDTH), row(SSD_CONV_DIM), row(LANES), row(S5_WIDTH)],
        out_shape=[jax.ShapeDtypeStruct((rows, SSD_WIDTH), act_dtype),
                   jax.ShapeDtypeStruct((rows, SSD_CONV_DIM), act_dtype),
                   jax.ShapeDtypeStruct((rows, LANES), F32),
                   jax.ShapeDtypeStruct((rows, S5_WIDTH), act_dtype)],
        compiler_params=pltpu.CompilerParams(dimension_semantics=("parallel",), vmem_limit_bytes=VMEM_LIMIT),
        name="in_proj",
    )(x2d, g, wz, wx, wdt, wu)


def _ssd_body(mask_rows, xbc_ref, dt_ref, z_ref, cinit_ref, hinit_ref, cw_ref, cb_ref, dtb_ref, alog_ref,
              dexp_ref, nrm_ref, eexp_ref, y_ref, ctail_ref, st_ref, hto_ref, xwin, hT):
    c = pl.program_id(1)
    L = SSD_CHUNK

    @pl.when(c == 0)
    def _init():
        xwin[0:SUBLANES, :] = cinit_ref[0]
        hT[...] = hinit_ref[0]

    xwin[SUBLANES:SUBLANES + L, :] = xbc_ref[0].astype(F32)
    acc = cb_ref[...]
    for k in range(SSD_CONV):
        off = SUBLANES - (SSD_CONV - 1) + k
        acc = acc + xwin[off:off + L, :] * cw_ref[k:k + 1, :]
    tail = xwin[L:L + SUBLANES, :]
    xwin[0:SUBLANES, :] = tail
    ctail_ref[0] = tail

    xact = acc * jax.nn.sigmoid(acc)
    dt = _softplus(dt_ref[0] + dtb_ref[...])
    if mask_rows:
        valid = lax.broadcasted_iota(jnp.int32, (L, 1), 0) >= mask_rows
        xact = jnp.where(valid, xact, 0.0)
        dt = jnp.where(valid, dt, 0.0)

    a_neg = -jnp.exp(alog_ref[...])
    dA = dt * a_neg
    row_i = lax.broadcasted_iota(jnp.int32, (L, L), 0)
    col_i = lax.broadcasted_iota(jnp.int32, (L, L), 1)
    causal = row_i >= col_i
    tril = causal.astype(BF16)
    cs = _dot3_left(tril, dA)
    csT = cs.T
    dtT = dt.T
    ecs = jnp.exp(cs)
    wdec = jnp.exp(cs[L - 1:L, :] - cs) * dt
    eexp = eexp_ref[...]
    ecs_e = _dot3(ecs, eexp)
    wdec_e = _dot3(wdec, eexp)
    lane = lax.broadcasted_iota(jnp.int32, (L, LANES), 1)
    first_half = lane < SSD_HEAD_DIM

    gw = SSD_HPG * SSD_HEAD_DIM
    y_groups = []
    for g in range(SSD_GROUPS):
        b_g = xact[:, SSD_WIDTH + g * SSD_STATE: SSD_WIDTH + (g + 1) * SSD_STATE]
        c_g = xact[:, SSD_WIDTH + (SSD_GROUPS + g) * SSD_STATE: SSD_WIDTH + (SSD_GROUPS + g + 1) * SSD_STATE]
        b_b = b_g.astype(BF16)
        c_b = c_g.astype(BF16)
        cb = lax.dot_general(c_b, b_b, (((1,), (1,)), ((), ())), preferred_element_type=F32)
        xs_g = xact[:, g * gw:(g + 1) * gw]
        h_prev = hT[g]
        y_off = _dot(c_b, h_prev.astype(BF16)) * ecs_e[:, g * gw:(g + 1) * gw]
        xdec = (xs_g * wdec_e[:, g * gw:(g + 1) * gw]).astype(BF16)
        hT[g] = h_prev * ecs_e[L - 1:L, g * gw:(g + 1) * gw] + _dot(b_g.T.astype(BF16), xdec)
        pieces = []
        for j in range(SSD_HPG // 2):
            xs_pair = xs_g[:, j * LANES:(j + 1) * LANES]
            halves = (jnp.where(first_half, xs_pair, 0.0).astype(BF16),
                      jnp.where(first_half, 0.0, xs_pair).astype(BF16))
            yd = None
            for t in range(2):
                h = g * SSD_HPG + 2 * j + t
                seg = cs[:, h:h + 1] - csT[h:h + 1, :]
                lmat = jnp.exp(jnp.where(causal, seg, -jnp.inf))
                m = (cb * lmat * dtT[h:h + 1, :]).astype(BF16)
                part = _dot(m, halves[t])
                yd = part if yd is None else yd + part
            pieces.append(yd)
        y_groups.append(jnp.concatenate(pieces, axis=-1) + y_off + dexp_ref[:, g * gw:(g + 1) * gw] * xs_g)
    y = jnp.concatenate(y_groups, axis=-1)
    z = z_ref[0].astype(F32)
    y_ref[0] = _rms(y * (z * jax.nn.sigmoid(z)), nrm_ref[...]).astype(y_ref.dtype)

    @pl.when(c == pl.num_programs(1) - 1)
    def _emit():
        hto_ref[0] = hT[...]
        for g in range(SSD_GROUPS):
            t = hT[g].T
            for k in range(SSD_HPG):
                st_ref[0, g * SSD_HPG + k] = t[k * SSD_HEAD_DIM:(k + 1) * SSD_HEAD_DIM, :]


def _ssd_chunked(xbc, dt, z, cinit, hinit, cw, cb, dtb, alog, dexp, nrm, eexp, mask_rows):
    bsz, seq, _ = xbc.shape
    nc = seq // SSD_CHUNK
    gw = SSD_HPG * SSD_HEAD_DIM
    blk = lambda w: pl.BlockSpec((1, SSD_CHUNK, w), lambda b, c: (b, c, 0))
    return pl.pallas_call(
        functools.partial(_ssd_body, mask_rows),
        grid=(bsz, nc),
        in_specs=[blk(SSD_CONV_DIM), blk(LANES), blk(SSD_WIDTH),
                  pl.BlockSpec((1, SUBLANES, SSD_CONV_DIM), lambda b, c: (0, 0, 0)),
                  pl.BlockSpec((1, SSD_GROUPS, SSD_STATE, gw), lambda b, c: (0, 0, 0, 0)),
                  _full_spec(cw), _full_spec(cb), _full_spec(dtb), _full_spec(alog),
                  _full_spec(dexp), _full_spec(nrm), _full_spec(eexp)],
        out_specs=[blk(SSD_WIDTH),
                   pl.BlockSpec((1, SUBLANES, SSD_CONV_DIM), lambda b, c: (b, 0, 0)),
                   pl.BlockSpec((1, SSD_HEADS, SSD_HEAD_DIM, SSD_STATE), lambda b, c: (b, 0, 0, 0)),
                   pl.BlockSpec((1, SSD_GROUPS, SSD_STATE, gw), lambda b, c: (b, 0, 0, 0))],
        out_shape=[jax.ShapeDtypeStruct((bsz, seq, SSD_WIDTH), BF16),
                   jax.ShapeDtypeStruct((bsz, SUBLANES, SSD_CONV_DIM), F32),
                   jax.ShapeDtypeStruct((bsz, SSD_HEADS, SSD_HEAD_DIM, SSD_STATE), F32),
                   jax.ShapeDtypeStruct((bsz, SSD_GROUPS, SSD_STATE, gw), F32)],
        scratch_shapes=[pltpu.VMEM((SUBLANES + SSD_CHUNK, SSD_CONV_DIM), F32),
                        pltpu.VMEM((SSD_GROUPS, SSD_STATE, gw), F32)],
        compiler_params=pltpu.CompilerParams(dimension_semantics=("parallel", "arbitrary"),
                                             vmem_limit_bytes=VMEM_LIMIT),
        name="ssd_chunked",
    )(xbc, dt, z, cinit, hinit, cw, cb, dtb, alog, dexp, nrm, eexp)


def _ssd_step_prep_body(xbc_ref, c0_ref, c1_ref, c2_ref, dt_ref, cw_ref, cb_ref, dtb_ref, alog_ref,
                        xt_ref, dt_out_ref, dec_ref, bc_ref, xs_ref):
    acc = cb_ref[...]
    for k, r in enumerate((c0_ref, c1_ref, c2_ref, xbc_ref)):
        acc = acc + r[...] * cw_ref[k:k + 1, :]
    xact = acc * jax.nn.sigmoid(acc)
    xs = xact[:, :SSD_WIDTH]
    dt = _softplus(dt_ref[...] + dtb_ref[...])
    dt_out_ref[...] = dt
    dec_ref[...] = jnp.exp(dt * -jnp.exp(alog_ref[...]))
    bc_ref[...] = xact[:, SSD_WIDTH:]
    xs_ref[...] = xs
    xt_ref[...] = xs.T.astype(xt_ref.dtype)


def _ssd_step_prep(xbc, c0, c1, c2, dt, cw, cb, dtb, alog):
    n = xbc.shape[0]
    args = (xbc, c0, c1, c2, dt, cw, cb, dtb, alog)
    spec = lambda r, w: pl.BlockSpec((r, w), lambda: (0, 0))
    return pl.pallas_call(
        _ssd_step_prep_body,
        in_specs=[_full_spec(a) for a in args],
        out_specs=[spec(SSD_WIDTH, n), spec(n, LANES), spec(n, LANES), spec(n, 2 * SSD_GROUPS * SSD_STATE),
                   spec(n, SSD_WIDTH)],
        out_shape=[jax.ShapeDtypeStruct((SSD_WIDTH, n), BF16), jax.ShapeDtypeStruct((n, LANES), F32),
                   jax.ShapeDtypeStruct((n, LANES), F32),
                   jax.ShapeDtypeStruct((n, 2 * SSD_GROUPS * SSD_STATE), F32),
                   jax.ShapeDtypeStruct((n, SSD_WIDTH), F32)],
        compiler_params=pltpu.CompilerParams(vmem_limit_bytes=VMEM_LIMIT),
        name="ssd_step_prep",
    )(*args)


def _ssd_step_body(dt_ref, dec_ref, st_ref, xt_ref, bc_ref, so_ref, y_ref):
    n = xt_ref.shape[1]
    gw = SSD_HPG * SSD_HEAD_DIM
    blk = pl.program_id(0)
    seq_id = lax.broadcasted_iota(jnp.int32, (n, SSD_STATE), 0)
    sub_id = lax.broadcasted_iota(jnp.int32, (SUBLANES, gw), 0)
    base = pl.multiple_of(blk * SUBLANES, SUBLANES)
    y_acc = [jnp.zeros((SUBLANES, gw), F32) for _ in range(SSD_GROUPS)]
    for i in range(SUBLANES):
        s = blk * SUBLANES + i
        for g in range(SSD_GROUPS):
            b_all = bc_ref[:, g * SSD_STATE:(g + 1) * SSD_STATE]
            rhs = jnp.where(seq_id == s, b_all, 0.0).astype(BF16)
            outer = _dot(xt_ref[g * gw:(g + 1) * gw, :], rhs)
            news = []
            for k in range(SSD_HPG):
                h = g * SSD_HPG + k
                new = (dec_ref[s * SSD_HEADS + h] * st_ref[i, h]
                       + dt_ref[s * SSD_HEADS + h] * outer[k * SSD_HEAD_DIM:(k + 1) * SSD_HEAD_DIM, :])
                so_ref[i, h] = new
                news.append(new)
            new_g = jnp.concatenate(news, axis=0).astype(BF16)
            c_lo = (SSD_GROUPS + g) * SSD_STATE
            c_blk = bc_ref[pl.ds(base, SUBLANES), c_lo:c_lo + SSD_STATE].astype(BF16)
            r = lax.dot_general(c_blk, new_g, (((1,), (1,)), ((), ())), preferred_element_type=F32)
            y_acc[g] = y_acc[g] + jnp.where(sub_id == i, r, 0.0)
    y_ref[...] = jnp.concatenate(y_acc, axis=-1)


def _ssd_step(dt_flat, dec_flat, state, xt, bc):
    n = state.shape[0]
    st_spec = pl.BlockSpec((SUBLANES, SSD_HEADS, SSD_HEAD_DIM, SSD_STATE), lambda i, *_: (i, 0, 0, 0))
    return pl.pallas_call(
        _ssd_step_body,
        grid_spec=pltpu.PrefetchScalarGridSpec(
            num_scalar_prefetch=2,
            grid=(n // SUBLANES,),
            in_specs=[st_spec, pl.BlockSpec(xt.shape, lambda i, *_: (0, 0)),
                      pl.BlockSpec(bc.shape, lambda i, *_: (0, 0))],
            out_specs=[st_spec, pl.BlockSpec((SUBLANES, SSD_WIDTH), lambda i, *_: (i, 0))]),
        out_shape=[jax.ShapeDtypeStruct(state.shape, F32), jax.ShapeDtypeStruct((n, SSD_WIDTH), F32)],
        compiler_params=pltpu.CompilerParams(dimension_semantics=("parallel",), vmem_limit_bytes=VMEM_LIMIT),
        name="ssd_step",
    )(dt_flat, dec_flat, state, xt, bc)


def _s5_project_in(u_b16, wb_ref, store):
    kw = 16 * S5_GROUP_CH
    nw = 16 * S5_STATE
    for j in range(S5_WIDTH // kw):
        r = _dot(u_b16[:, j * kw:(j + 1) * kw], wb_ref[j])
        store(j, r[:, :nw], r[:, nw:])


def _s5_tail(hre_of, him_of, u_f32, wcr_ref, wci_ref, d_ref, wglu_ref, bglu_ref, nrm_ref):
    cols = []
    for j in range(wcr_ref.shape[0]):
        cols.append(_dot(hre_of(j).astype(BF16), wcr_ref[j]) + _dot(him_of(j).astype(BF16), wci_ref[j]))
    y = jnp.concatenate(cols, axis=-1) + d_ref[...] * u_f32
    y = jax.nn.gelu(y)
    y = y * jax.nn.sigmoid(_dot(y.astype(BF16), wglu_ref[...]) + bglu_ref[...])
    return _rms(y, nrm_ref[...])


def _s5_seq_body(u_ref, um_ref, wb_ref, abr_ref, abi_ref, wcr_ref, wci_ref, d_ref, wglu_ref, bglu_ref, nrm_ref,
                 y_ref, sre_ref, sim_ref, bu, h):
    j = pl.program_id(0)
    bsz, lc = u_ref.shape[0], u_ref.shape[1]
    nw = 16 * S5_STATE

    def scan(nsteps):
        for k in range(S5_LANES // S5_SCAN_LANES):
            sl_r = pl.ds(k * S5_SCAN_LANES, S5_SCAN_LANES)
            sl_i = pl.ds(S5_LANES + k * S5_SCAN_LANES, S5_SCAN_LANES)
            ar = abr_ref[:, sl_r]
            ai = abi_ref[:, sl_r]

            def step(l, carry):
                hr, hi = carry
                nr = ar * hr - ai * hi + bu[l, :, sl_r]
                ni = ar * hi + ai * hr + bu[l, :, sl_i]
                bu[l, :, sl_r] = nr
                bu[l, :, sl_i] = ni
                return nr, ni

            hr, hi = lax.fori_loop(0, nsteps, step, (h[:, sl_r], h[:, sl_i]))
            h[:, sl_r] = hr
            h[:, sl_i] = hi

    @pl.when(j == 0)
    def _meta():
        h[...] = jnp.zeros_like(h)

        def store(jj, re, im):
            for b in range(bsz):
                bu[0:N_META, b, jj * nw:(jj + 1) * nw] = re
                bu[0:N_META, b, S5_LANES + jj * nw:S5_LANES + (jj + 1) * nw] = im

        _s5_project_in(um_ref[...], wb_ref, store)
        scan(N_META)

    u2 = u_ref[...].reshape(bsz * lc, S5_WIDTH)

    def store(jj, re, im):
        for b in range(bsz):
            bu[:, b, jj * nw:(jj + 1) * nw] = re[b * lc:(b + 1) * lc, :]
            bu[:, b, S5_LANES + jj * nw:S5_LANES + (jj + 1) * nw] = im[b * lc:(b + 1) * lc, :]

    _s5_project_in(u2, wb_ref, store)
    scan(lc)

    def slab(base):
        return lambda jj: jnp.concatenate(
            [bu[:, b, base + jj * nw:base + (jj + 1) * nw] for b in range(bsz)], axis=0)

    y = _s5_tail(slab(0), slab(S5_LANES), u2.astype(F32), wcr_ref, wci_ref, d_ref, wglu_ref, bglu_ref, nrm_ref)
    y_ref[...] = y.astype(y_ref.dtype).reshape(bsz, lc, S5_WIDTH)

    @pl.when(j == pl.num_programs(0) - 1)
    def _emit():
        sre_ref[...] = h[:, 0:S5_LANES]
        sim_ref[...] = h[:, S5_LANES:]


def _s5_seq(u, um, wb, abr, abi, wcr, wci, d, wglu, bglu, nrm):
    bsz, seq, _ = u.shape
    lc = S5_TIME_TILE
    consts = (um, wb, abr, abi, wcr, wci, d, wglu, bglu, nrm)
    blk = pl.BlockSpec((bsz, lc, S5_WIDTH), lambda j: (0, j, 0))
    st = pl.BlockSpec((bsz, S5_LANES), lambda j: (0, 0))
    return pl.pallas_call(
        _s5_seq_body,
        grid=(seq // lc,),
        in_specs=[blk] + [_full_spec(a) for a in consts],
        out_specs=[blk, st, st],
        out_shape=[jax.ShapeDtypeStruct((bsz, seq, S5_WIDTH), BF16),
                   jax.ShapeDtypeStruct((bsz, S5_LANES), F32), jax.ShapeDtypeStruct((bsz, S5_LANES), F32)],
        scratch_shapes=[pltpu.VMEM((lc, bsz, 2 * S5_LANES), F32), pltpu.VMEM((bsz, 2 * S5_LANES), F32)],
        compiler_params=pltpu.CompilerParams(dimension_semantics=("arbitrary",), vmem_limit_bytes=VMEM_LIMIT),
        name="s5_seq",
    )(u, *consts)


def _sample_post_body(yc_ref, xs_ref, z_ref, dexp_ref, snrm_ref, u_ref, hr_ref, hi_ref, wb_ref, abr_ref, abi_ref,
                      wcr_ref, wci_ref, d_ref, wglu_ref, bglu_ref, nrm_ref,
                      yssd_ref, ys5_ref, nre_ref, nim_ref):
    z = z_ref[...]
    y = yc_ref[...] + dexp_ref[...] * xs_ref[...]
    yssd_ref[...] = _rms(y * (z * jax.nn.sigmoid(z)), snrm_ref[...]).astype(yssd_ref.dtype)

    u = u_ref[...]
    nw = 16 * S5_STATE
    ar, ai = abr_ref[...], abi_ref[...]

    def store(jj, re, im):
        sl = slice(jj * nw, (jj + 1) * nw)
        h0r, h0i = hr_ref[:, sl], hi_ref[:, sl]
        nre_ref[:, sl] = ar[:, sl] * h0r - ai[:, sl] * h0i + re
        nim_ref[:, sl] = ar[:, sl] * h0i + ai[:, sl] * h0r + im

    _s5_project_in(u.astype(BF16), wb_ref, store)
    slab = lambda ref: (lambda jj: ref[:, jj * nw:(jj + 1) * nw])
    y5 = _s5_tail(slab(nre_ref), slab(nim_ref), u, wcr_ref, wci_ref, d_ref, wglu_ref, bglu_ref, nrm_ref)
    ys5_ref[...] = y5.astype(ys5_ref.dtype)


def _sample_post(yc, xs, z, dexp, snrm, u, h0r, h0i, wb, abr1, abi1, wcr, wci, d, wglu, bglu, nrm):
    n = yc.shape[0]
    args = (yc, xs, z, dexp, snrm, u, h0r, h0i, wb, abr1, abi1, wcr, wci, d, wglu, bglu, nrm)
    spec = lambda w: pl.BlockSpec((n, w), lambda: (0, 0))
    return pl.pallas_call(
        _sample_post_body,
        in_specs=[_full_spec(a) for a in args],
        out_specs=[spec(SSD_WIDTH), spec(S5_WIDTH), spec(S5_LANES), spec(S5_LANES)],
        out_shape=[jax.ShapeDtypeStruct((n, SSD_WIDTH), BF16), jax.ShapeDtypeStruct((n, S5_WIDTH), BF16),
                   jax.ShapeDtypeStruct((n, S5_LANES), F32), jax.ShapeDtypeStruct((n, S5_LANES), F32)],
        compiler_params=pltpu.CompilerParams(vmem_limit_bytes=VMEM_LIMIT),
        name="sample_post",
    )(*args)


def _mix_route_body(n_blocks, cin_ref, *refs):
    x1_ref, xn_ref, rt_ref, cnt_ref, carry = refs[-5:]
    i = pl.program_id(0)

    @pl.when(i == 0)
    def _init():
        carry[...] = cin_ref[...]

    @pl.when(i < n_blocks)
    def _compute():
        _mix_route_compute(*refs)

    @pl.when(i >= n_blocks)
    def _fill():
        for ref in (x1_ref, xn_ref, rt_ref):
            ref[...] = jnp.zeros_like(ref)

    cnt_ref[...] = carry[...]


def _mix_route_compute(x_ref, ys_ref, y5_ref, wa_ref, wb_ref, nf_ref, wrh_ref, wrl_ref, br_ref, *rest):
    x1_ref, xn_ref, rt_ref, _, carry = rest[-5:]
    x1 = x_ref[...] + _dot(ys_ref[...], wa_ref[...]) + _dot(y5_ref[...], wb_ref[...])
    x1_ref[...] = x1
    xn = _rms(x1, nf_ref[...])
    for j in range(SLAB_ROWS):
        xn_ref[:, j, :] = xn[:, j * LANES:(j + 1) * LANES]

    xh = xn.astype(BF16)
    xl = (xn - xh.astype(F32)).astype(BF16)
    logits = _dot(xh, wrh_ref[...]) + _dot(xl, wrh_ref[...]) + _dot(xh, wrl_ref[...]) + br_ref[...]
    tm = logits.shape[0]
    lane = lax.broadcasted_iota(jnp.int32, logits.shape, 1).astype(F32)
    neg = -jnp.inf
    big = float(LANES)

    def first_max(v):
        m = jnp.max(v, axis=-1, keepdims=True)
        return m, jnp.min(jnp.where(v == m, lane, big), axis=-1, keepdims=True)

    coarse = lane < MOE_GROUPS
    mc, gsel = first_max(jnp.where(coarse, logits, neg))
    psel = 1.0 / jnp.sum(jnp.where(coarse, jnp.exp(logits - mc), 0.0), axis=-1, keepdims=True)
    lo = MOE_GROUPS + MOE_EPG * gsel
    lf = jnp.where((lane >= lo) & (lane < lo + MOE_EPG), logits, neg)
    m1, i1 = first_max(lf)
    m2, i2 = first_max(jnp.where(lane == i1, neg, lf))
    e2 = jnp.exp(m2 - m1)
    g1 = psel / (1.0 + e2)
    g2 = psel * e2 / (1.0 + e2)
    e_a, e_b = i1 - MOE_GROUPS, i2 - MOE_GROUPS

    pick_a, pick_b = lane == e_a, lane == e_b
    picks = jnp.where(pick_a | pick_b, 1.0, 0.0)
    earlier = lax.broadcasted_iota(jnp.int32, (tm, tm), 0) > lax.broadcasted_iota(jnp.int32, (tm, tm), 1)
    prior = _dot(earlier.astype(BF16), picks.astype(BF16)) + carry[...]
    rank_a = jnp.sum(jnp.where(pick_a, prior, 0.0), axis=-1, keepdims=True)
    rank_b = jnp.sum(jnp.where(pick_b, prior, 0.0), axis=-1, keepdims=True)
    carry[...] = prior[tm - 1:tm, :] + picks[tm - 1:tm, :]

    out = jnp.zeros_like(logits)
    for k, v in enumerate((e_a, e_b, g1, g2, rank_a, rank_b)):
        out = jnp.where(lane == float(k), v, out)
    rt_ref[...] = out


def _mix_route(counts_in, x, ys, y5, wa, wb, nf, wrh, wrl, br, tm, total_rows, row_block_offset, bufs):
    n_blocks = x.shape[0] // tm
    fill_tail = bufs is None and n_blocks * tm < total_rows
    row = lambda w: pl.BlockSpec((tm, w), lambda i: (jnp.minimum(i, n_blocks - 1), 0))
    out_row = lambda w: pl.BlockSpec((tm, w), lambda i: (i + row_block_offset, 0))
    consts = (wa, wb, nf, wrh, wrl, br)
    in_specs = ([_full_spec(counts_in), row(D_MODEL), row(SSD_WIDTH), row(S5_WIDTH)]
                + [_full_spec(a) for a in consts])
    args = [counts_in, x, ys, y5, *consts]
    aliases = {}
    if bufs is not None:
        for k, b in enumerate(bufs):
            in_specs.append(pl.BlockSpec(memory_space=pl.ANY))
            aliases[len(args)] = k
            args.append(b)
    return pl.pallas_call(
        functools.partial(_mix_route_body, n_blocks),
        grid=(n_blocks + int(fill_tail),),
        in_specs=in_specs,
        out_specs=[out_row(D_MODEL),
                   pl.BlockSpec((tm, SLAB_ROWS, LANES), lambda i: (i + row_block_offset, 0, 0)),
                   out_row(LANES), pl.BlockSpec((1, LANES), lambda i: (0, 0))],
        out_shape=[jax.ShapeDtypeStruct((total_rows, D_MODEL), F32),
                   jax.ShapeDtypeStruct((total_rows, SLAB_ROWS, LANES), F32),
                   jax.ShapeDtypeStruct((total_rows, LANES), F32), jax.ShapeDtypeStruct((1, LANES), F32)],
        scratch_shapes=[pltpu.VMEM((1, LANES), F32)],
        input_output_aliases=aliases,
        compiler_params=pltpu.CompilerParams(dimension_semantics=("arbitrary",), vmem_limit_bytes=VMEM_LIMIT),
        name="mix_route",
    )(*args)


def _from_slabs(ref):
    return jnp.concatenate([ref[:, j, :] for j in range(SLAB_ROWS)], axis=-1)


def _dispatch_body(pos_ref, tz_ref, xn_hbm, xs_hbm, zbuf, zsems, sems):
    n_tiles = tz_ref.shape[0]
    batch = DISPATCH_BATCH
    i = pl.program_id(0)
    last = pl.num_programs(0) - 1

    @pl.when(i == 0)
    def _zero_fill():
        zbuf[...] = jnp.zeros_like(zbuf)

        def zero_copy(t):
            return pltpu.make_async_copy(zbuf, xs_hbm.at[pl.ds(t * MOE_TILE, MOE_TILE)], zsems.at[t])

        def zero_start(t, carry):
            @pl.when(tz_ref[t] != 0)
            def _():
                zero_copy(t).start()
            return carry

        def zero_wait(t, carry):
            @pl.when(tz_ref[t] != 0)
            def _():
                zero_copy(t).wait()
            return carry

        lax.fori_loop(0, n_tiles, zero_start, 0)
        lax.fori_loop(0, n_tiles, zero_wait, 0)

    def copy(step, r):
        a = step * batch + r
        return pltpu.make_async_copy(xn_hbm.at[pl.ds(a >> 1, 1)], xs_hbm.at[pl.ds(pos_ref[a], 1)],
                                     sems.at[step % 2, r])

    def start(r, carry):
        copy(i, r).start()
        return carry

    def wait_of(step):
        def wait(r, carry):
            copy(step, r).wait()
            return carry
        return wait

    lax.fori_loop(0, batch, start, 0, unroll=DMA_UNROLL)

    @pl.when(i > 0)
    def _wait_previous():
        lax.fori_loop(0, batch, wait_of(i - 1), 0, unroll=DMA_UNROLL)

    @pl.when(i == last)
    def _wait_own():
        lax.fori_loop(0, batch, wait_of(i), 0, unroll=DMA_UNROLL)


def _dispatch(pos_flat, tile_zero, xn, n_tiles):
    return pl.pallas_call(
        _dispatch_body,
        grid_spec=pltpu.PrefetchScalarGridSpec(
            num_scalar_prefetch=2,
            grid=(pos_flat.shape[0] // DISPATCH_BATCH,),
            in_specs=[pl.BlockSpec(memory_space=pl.ANY)],
            out_specs=pl.BlockSpec(memory_space=pl.ANY),
            scratch_shapes=[pltpu.VMEM((MOE_TILE, SLAB_ROWS, LANES), F32),
                            pltpu.SemaphoreType.DMA((n_tiles,)),
                            pltpu.SemaphoreType.DMA((2, DISPATCH_BATCH))]),
        out_shape=jax.ShapeDtypeStruct((n_tiles * MOE_TILE, SLAB_ROWS, LANES), F32),
        compiler_params=pltpu.CompilerParams(dimension_semantics=("arbitrary",), vmem_limit_bytes=VMEM_LIMIT),
        name="moe_dispatch",
    )(pos_flat, tile_zero, xn)


def _moe_ffn_body(te_ref, nused_ref, x_ref, wg_ref, wu_ref, wd_ref, y_ref, wgb, wub, wdb):
    i = pl.program_id(0)

    @pl.when(i >= nused_ref[0])
    def _unused_tile():
        y_ref[...] = jnp.zeros_like(y_ref)

    @pl.when(i < nused_ref[0])
    def _tile():
        @pl.when((i == 0) | (te_ref[i] != te_ref[jnp.maximum(i - 1, 0)]))
        def _cast_weights():
            wgb[...] = wg_ref[0].astype(BF16)
            wub[...] = wu_ref[0].astype(BF16)
            wdb[...] = wd_ref[0].astype(BF16)

        x = _from_slabs(x_ref).astype(BF16)
        gate = _dot(x, wgb[...])
        hmid = (gate * jax.nn.sigmoid(gate)) * _dot(x, wub[...])
        y = _dot(hmid.astype(BF16), wdb[...])
        for j in range(SLAB_ROWS):
            y_ref[:, j, :] = y[:, j * LANES:(j + 1) * LANES]


def _moe_ffn(tile_expert, n_used, xsorted, w_gate, w_up, w_down, n_tiles):
    wspec = lambda s: pl.BlockSpec((1,) + s, lambda i, te, nu: (te[i], 0, 0))
    slab = lambda imap: pl.BlockSpec((MOE_TILE, SLAB_ROWS, LANES), imap)
    return pl.pallas_call(
        _moe_ffn_body,
        grid_spec=pltpu.PrefetchScalarGridSpec(
            num_scalar_prefetch=2,
            grid=(n_tiles,),
            in_specs=[slab(lambda i, te, nu: (jnp.minimum(i, nu[0] - 1), 0, 0)),
                      wspec((D_MODEL, MOE_D_FF)), wspec((D_MODEL, MOE_D_FF)), wspec((MOE_D_FF, D_MODEL))],
            out_specs=slab(lambda i, te, nu: (i, 0, 0)),
            scratch_shapes=[pltpu.VMEM((D_MODEL, MOE_D_FF), BF16), pltpu.VMEM((D_MODEL, MOE_D_FF), BF16),
                            pltpu.VMEM((MOE_D_FF, D_MODEL), BF16)]),
        out_shape=jax.ShapeDtypeStruct((n_tiles * MOE_TILE, SLAB_ROWS, LANES), F32),
        compiler_params=pltpu.CompilerParams(dimension_semantics=("arbitrary",), vmem_limit_bytes=VMEM_LIMIT),
        name="moe_ffn",
    )(tile_expert, n_used, xsorted, w_gate, w_up, w_down)


def _combine_body(tm, pos_ref, x1_ref, rt_ref, ys_hbm, nf_ref, out_ref, ybuf, sems):
    i = pl.program_id(0)
    rows = pl.num_programs(0) * tm

    def copy(k, r):
        return pltpu.make_async_copy(ys_hbm.at[pl.ds(pos_ref[k * rows + i * tm + r], 1)],
                                     ybuf.at[k, pl.ds(r, 1)], sems.at[k * tm + r])

    def start(r, carry):
        copy(0, r).start()
        copy(1, r).start()
        return carry

    def wait(r, carry):
        copy(0, r).wait()
        copy(1, r).wait()
        return carry

    lax.fori_loop(0, tm, start, 0, unroll=DMA_UNROLL)
    lax.fori_loop(0, tm, wait, 0, unroll=DMA_UNROLL)
    rt = rt_ref[...]
    x1 = x1_ref[...]
    x2 = jnp.concatenate(
        [x1[:, j * LANES:(j + 1) * LANES] + rt[:, 2:3] * ybuf[0, :, j, :] + rt[:, 3:4] * ybuf[1, :, j, :]
         for j in range(SLAB_ROWS)], axis=-1)
    out_ref[...] = _rms(x2, nf_ref[...])


def _combine(pos, x1, rt, ysorted, nf, tm, rows, row_block_offset):
    row = lambda w: pl.BlockSpec((tm, w), lambda i, p: (i + row_block_offset, 0))
    return pl.pallas_call(
        functools.partial(_combine_body, tm),
        grid_spec=pltpu.PrefetchScalarGridSpec(
            num_scalar_prefetch=1,
            grid=(rows // tm,),
            in_specs=[row(D_MODEL), row(LANES), pl.BlockSpec(memory_space=pl.ANY),
                      pl.BlockSpec((1, D_MODEL), lambda i, p: (0, 0))],
            out_specs=pl.BlockSpec((tm, D_MODEL), lambda i, p: (i, 0)),
            scratch_shapes=[pltpu.VMEM((2, tm, SLAB_ROWS, LANES), F32), pltpu.SemaphoreType.DMA((2 * tm,))]),
        out_shape=jax.ShapeDtypeStruct((rows, D_MODEL), F32),
        compiler_params=pltpu.CompilerParams(dimension_semantics=("arbitrary",), vmem_limit_bytes=VMEM_LIMIT),
        name="moe_combine",
    )(pos, x1, rt, ysorted, nf)


def _route_tables(counts, eid, rank, n_tiles):
    experts = jnp.arange(MOE_EXPERTS, dtype=jnp.int32)
    tiles_per = (counts + MOE_TILE - 1) // MOE_TILE
    tile_end = jnp.cumsum(tiles_per)
    pstart = (tile_end - tiles_per) * MOE_TILE
    pos = jnp.sum(jnp.where(eid[..., None] == experts, pstart, 0), axis=-1) + rank
    n_used = tile_end[-1]
    tiles = jnp.arange(n_tiles, dtype=jnp.int32)
    tile_expert = jnp.sum((tile_end[None, :] <= jnp.minimum(tiles, n_used - 1)[:, None]).astype(jnp.int32), axis=1)
    ragged = counts % MOE_TILE != 0
    tile_zero = (tiles >= n_used) | jnp.any((tiles[:, None] == tile_end[None, :] - 1) & ragged[None, :], axis=1)
    return pos, tile_expert, tile_zero.astype(jnp.int32), n_used.reshape(1).astype(jnp.int32)


def _s5_tables(a_re, a_im, log_dt, b_re, b_im, c_re, c_im):
    dt = jnp.exp(log_dt)[:, None]
    mag = jnp.exp(a_re * dt)
    ab_re = mag * jnp.cos(a_im * dt)
    ab_im = mag * jnp.sin(a_im * dt)
    den = a_re * a_re + a_im * a_im
    nr = ab_re - 1.0
    q_re = (nr * a_re + ab_im * a_im) / den
    q_im = (ab_im * a_re - nr * a_im) / den
    bb_re = q_re[..., None] * b_re - q_im[..., None] * b_im
    bb_im = q_re[..., None] * b_im + q_im[..., None] * b_re
    eye = jnp.eye(16, dtype=F32)
    nblk = S5_GROUPS // 16

    def in_map(bb):
        w = jnp.einsum("jgpc,gh->jgchp", bb.reshape(nblk, 16, S5_STATE, S5_GROUP_CH), eye)
        return w.reshape(nblk, 16 * S5_GROUP_CH, 16 * S5_STATE)

    def out_map(cc):
        w = jnp.einsum("jgcp,gh->jgphc", cc.reshape(nblk, 16, S5_GROUP_CH, S5_STATE), eye)
        return w.reshape(nblk, 16 * S5_STATE, 16 * S5_GROUP_CH)

    wb = jnp.concatenate([in_map(bb_re), in_map(bb_im)], axis=-1).astype(BF16)
    return (wb, ab_re.reshape(1, S5_LANES), ab_im.reshape(1, S5_LANES),
            out_map(c_re).astype(BF16), out_map(-c_im).astype(BF16))


def kernel(x_prompt, x_sample, state_ssd_conv, state_ssd_ssm, state_s5_re, state_s5_im, meta_tokens, norm_mix, w_in, conv_w, conv_b, dt_bias, a_log, d_ssd, ssd_norm, s5_a_re, s5_a_im, s5_log_dt, s5_b_re, s5_b_im, s5_c_re, s5_c_im, s5_d, w_glu, b_glu, s5_norm, w_out, norm_ffn, router_coarse_w, router_coarse_b, router_fine_w, router_fine_b, w_gate, w_up, w_down, norm_final):
    bp, seq, _ = x_prompt.shape
    bs = x_sample.shape[0]
    n_prompt = bp * seq
    n_tok = n_prompt + bs
    row2 = lambda v: v.reshape(1, -1)
    pad_heads = lambda v: jnp.pad(v, (0, LANES - SSD_HEADS)).reshape(1, LANES)

    w = w_in[0]
    o1, o2, o3 = SSD_WIDTH, SSD_WIDTH + SSD_CONV_DIM, SSD_WIDTH + SSD_CONV_DIM + SSD_HEADS
    wz, wx, wu = w[:, :o1].astype(BF16), w[:, o1:o2].astype(BF16), w[:, o3:].astype(BF16)
    wdt = jnp.pad(w[:, o2:o3], ((0, 0), (0, LANES - SSD_HEADS))).astype(BF16)
    g_mix = row2(norm_mix[0])
    cw, cb = conv_w[0], row2(conv_b[0])
    dtb, alog = pad_heads(dt_bias[0]), pad_heads(a_log[0])
    dexp = row2(jnp.repeat(d_ssd[0], SSD_HEAD_DIM))
    snrm = row2(ssd_norm[0])
    eexp = (jnp.arange(LANES)[:, None] == (jnp.arange(SSD_WIDTH) // SSD_HEAD_DIM)[None, :]).astype(BF16)
    wb5, ab_re, ab_im, wcr, wci = _s5_tables(s5_a_re[0], s5_a_im[0], s5_log_dt[0], s5_b_re[0], s5_b_im[0],
                                             s5_c_re[0], s5_c_im[0])
    d5, wglu, bglu, nrm5 = row2(s5_d[0]), w_glu[0].astype(BF16), row2(b_glu[0]), row2(s5_norm[0])
    wo_a, wo_b = w_out[0][:SSD_WIDTH].astype(BF16), w_out[0][SSD_WIDTH:].astype(BF16)
    w_r = jnp.concatenate([router_coarse_w[0], router_fine_w[0].transpose(1, 0, 2).reshape(D_MODEL, MOE_EXPERTS)], axis=1)
    w_r = jnp.pad(w_r, ((0, 0), (0, LANES - w_r.shape[1])))
    wrh = w_r.astype(BF16)
    wrl = (w_r - wrh.astype(F32)).astype(BF16)
    b_r = jnp.concatenate([router_coarse_b[0], router_fine_b[0].reshape(-1)])
    b_r = jnp.pad(b_r, (0, LANES - b_r.shape[0])).reshape(1, LANES)

    zp, xbcp, dtp, up = _in_proj(x_prompt.reshape(n_prompt, D_MODEL), g_mix, wz, wx, wdt, wu, TOK_TILE, BF16)
    xsm = jnp.concatenate([x_sample.reshape(bs, D_MODEL), meta_tokens], axis=0)
    zs, xbcs, dts, us = _in_proj(xsm, g_mix, wz, wx, wdt, wu, xsm.shape[0], F32)

    front = SSD_CHUNK - N_META
    padf = lambda a: jnp.pad(a[bs:], ((front, 0), (0, 0)))[None]
    gw = SSD_HPG * SSD_HEAD_DIM
    ssd_consts = (cw, cb, dtb, alog, dexp, snrm, eexp)
    _, ctail_m, _, ht_m = _ssd_chunked(
        padf(xbcs), padf(dts), jnp.zeros((1, SSD_CHUNK, SSD_WIDTH), F32),
        jnp.zeros((1, SUBLANES, SSD_CONV_DIM), F32), jnp.zeros((1, SSD_GROUPS, SSD_STATE, gw), F32),
        *ssd_consts, mask_rows=front)
    y_ssd_p, ctail_p, ssm_p, _ = _ssd_chunked(
        xbcp.reshape(bp, seq, SSD_CONV_DIM), dtp.reshape(bp, seq, LANES), zp.reshape(bp, seq, SSD_WIDTH),
        ctail_m, ht_m, *ssd_consts, mask_rows=0)

    abr8, abi8 = jnp.broadcast_to(ab_re, (bp, S5_LANES)), jnp.broadcast_to(ab_im, (bp, S5_LANES))
    y_s5_p, s5re_p, s5im_p = _s5_seq(up.reshape(bp, seq, S5_WIDTH), us[bs:].astype(BF16), wb5, abr8, abi8,
                                     wcr, wci, d5, wglu, bglu, nrm5)

    cst = state_ssd_conv[0]
    xt_s, dt_s, dec_s, bc, xs_s = _ssd_step_prep(xbcs[:bs], cst[:, 0], cst[:, 1], cst[:, 2], dts[:bs],
                                                 cw, cb, dtb, alog)
    ssm_s, y_core = _ssd_step(dt_s[:, :SSD_HEADS].reshape(-1), dec_s[:, :SSD_HEADS].reshape(-1),
                              state_ssd_ssm[0], xt_s, bc)
    y_ssd_s, y_s5_s, s5re_s, s5im_s = _sample_post(
        y_core, xs_s, zs[:bs], dexp, snrm, us[:bs], state_s5_re[0].reshape(bs, S5_LANES),
        state_s5_im[0].reshape(bs, S5_LANES), wb5, ab_re, ab_im, wcr, wci, d5, wglu, bglu, nrm5)

    route_consts = (wo_a, wo_b, row2(norm_ffn[0]), wrh, wrl, b_r)
    *bufs, counts_p = _mix_route(jnp.zeros((1, LANES), F32), x_prompt.reshape(n_prompt, D_MODEL),
                                 y_ssd_p.reshape(n_prompt, SSD_WIDTH), y_s5_p.reshape(n_prompt, S5_WIDTH),
                                 *route_consts, TOK_TILE, n_tok, 0, None)
    x1, xn, rt, counts = _mix_route(counts_p, x_sample.reshape(bs, D_MODEL), y_ssd_s, y_s5_s, *route_consts,
                                    bs, n_tok, n_prompt // bs, bufs)

    n_tiles = -(-2 * n_tok // MOE_TILE) + MOE_EXPERTS
    eid = jnp.clip(rt[:, 0:2].astype(jnp.int32), 0, MOE_EXPERTS - 1)
    pos, tile_expert, tile_zero, n_used = _route_tables(counts[0, :MOE_EXPERTS].astype(jnp.int32), eid,
                                                        rt[:, 4:6].astype(jnp.int32), n_tiles)
    xsorted = _dispatch(pos.reshape(-1), tile_zero, xn, n_tiles)
    ysorted = _moe_ffn(tile_expert, n_used, xsorted, w_gate[0], w_up[0], w_down[0], n_tiles)
    nfin = row2(norm_final)
    y_p = _combine(pos[:n_prompt].T.reshape(-1), x1, rt, ysorted, nfin, MOE_TILE, n_prompt, 0)
    y_s = _combine(pos[n_prompt:].T.reshape(-1), x1, rt, ysorted, nfin, bs, bs, n_prompt // bs)

    s5_state = lambda a, b: a.reshape(1, b, S5_GROUPS, S5_STATE)
    new_conv_s = jnp.stack([cst[:, 1], cst[:, 2], xbcs[:bs]], axis=1)[None]
    return (y_p.reshape(bp, seq, D_MODEL), y_s.reshape(bs, 1, D_MODEL),
            ctail_p[:, SUBLANES - (SSD_CONV - 1):][None], ssm_p[None], s5_state(s5re_p, bp), s5_state(s5im_p, bp),
            new_conv_s, ssm_s[None], s5_state(s5re_s, bs), s5_state(s5im_s, bs))
```

```python
import functools

import jax
import jax.numpy as jnp
from jax import lax
from jax.experimental import pallas as pl
from jax.experimental.pallas import tpu as pltpu

F32, BF16 = jnp.float32, jnp.bfloat16

D_MODEL = 1024
N_META = 16
SSD_WIDTH = 1024
SSD_HEAD_DIM = 64
SSD_HEADS = 16
SSD_GROUPS = 2
SSD_HPG = SSD_HEADS // SSD_GROUPS
SSD_STATE = 128
SSD_CONV = 4
SSD_CHUNK = 128
SSD_CONV_DIM = SSD_WIDTH + 2 * SSD_GROUPS * SSD_STATE
S5_WIDTH = 1024
S5_GROUP_CH = 16
S5_GROUPS = 64
S5_STATE = 64
S5_LANES = S5_GROUPS * S5_STATE
MOE_GROUPS = 4
MOE_EPG = 8
MOE_EXPERTS = MOE_GROUPS * MOE_EPG
MOE_D_FF = 512
EPS = 1e-6

LANES = 128
SUBLANES = 8
VMEM_LIMIT = 56 * 1024 * 1024

S5_TIME_TILE = 32
S5_SCAN_LANES = 512
MOE_TILE = 256
SLAB_ROWS = D_MODEL // LANES
DISPATCH_BATCH = 256
DMA_UNROLL = 8
TOK_TILE = 512


def _dot(a, b):
    return jnp.dot(a, b, preferred_element_type=F32)


def _rms(x, g):
    return x * lax.rsqrt(jnp.mean(x * x, axis=-1, keepdims=True) + EPS) * g


def _softplus(x):
    return jnp.maximum(x, 0.0) + jnp.log1p(jnp.exp(-jnp.abs(x)))


def _split3(x):
    hi = x.astype(BF16)
    r = x - hi.astype(F32)
    mid = r.astype(BF16)
    lo = (r - mid.astype(F32)).astype(BF16)
    return hi, mid, lo


def _dot3(x, w):
    hi, mid, lo = _split3(x)
    return _dot(hi, w) + _dot(mid, w) + _dot(lo, w)


def _dot3_left(w, x):
    hi, mid, lo = _split3(x)
    return _dot(w, hi) + _dot(w, mid) + _dot(w, lo)


def _full_spec(a):
    nd = a.ndim
    return pl.BlockSpec(a.shape, lambda *_: (0,) * nd)


def _in_proj_body(x_ref, g_ref, wz_ref, wx_ref, wdt_ref, wu_ref, z_ref, xbc_ref, dt_ref, u_ref):
    xb = _rms(x_ref[...], g_ref[...]).astype(BF16)
    z_ref[...] = _dot(xb, wz_ref[...]).astype(z_ref.dtype)
    xbc_ref[...] = _dot(xb, wx_ref[...]).astype(xbc_ref.dtype)
    dt_ref[...] = _dot(xb, wdt_ref[...])
    u_ref[...] = _dot(xb, wu_ref[...]).astype(u_ref.dtype)


def _in_proj(x2d, g, wz, wx, wdt, wu, tm, act_dtype):
    rows = x2d.shape[0]
    row = lambda w: pl.BlockSpec((tm, w), lambda i: (i, 0))
    return pl.pallas_call(
        _in_proj_body,
        grid=(rows // tm,),
        in_specs=[row(D_MODEL), _full_spec(g), _full_spec(wz), _full_spec(wx), _full_spec(wdt), _full_spec(wu)],
        out_specs=[row(SSD_WIDTH), row(SSD_CONV_DIM), row(LANES), row(S5_WIDTH)],
        out_shape=[jax.ShapeDtypeStruct((rows, SSD_WIDTH), act_dtype),
                   jax.ShapeDtypeStruct((rows, SSD_CONV_DIM), act_dtype),
                   jax.ShapeDtypeStruct((rows, LANES), F32),
                   jax.ShapeDtypeStruct((rows, S5_WIDTH), act_dtype)],
        compiler_params=pltpu.CompilerParams(dimension_semantics=("parallel",), vmem_limit_bytes=VMEM_LIMIT),
        name="in_proj",
    )(x2d, g, wz, wx, wdt, wu)


def _ssd_body(mask_rows, xbc_ref, dt_ref, z_ref, cinit_ref, hinit_ref, cw_ref, cb_ref, dtb_ref, alog_ref,
              dexp_ref, nrm_ref, eexp_ref, y_ref, ctail_ref, st_ref, hto_ref, xwin, hT):
    c = pl.program_id(1)
    L = SSD_CHUNK

    @pl.when(c == 0)
    def _init():
        xwin[0:SUBLANES, :] = cinit_ref[0]
        hT[...] = hinit_ref[0]

    xwin[SUBLANES:SUBLANES + L, :] = xbc_ref[0].astype(F32)
    acc = cb_ref[...]
    for k in range(SSD_CONV):
        off = SUBLANES - (SSD_CONV - 1) + k
        acc = acc + xwin[off:off + L, :] * cw_ref[k:k + 1, :]
    tail = xwin[L:L + SUBLANES, :]
    xwin[0:SUBLANES, :] = tail
    ctail_ref[0] = tail

    xact = acc * jax.nn.sigmoid(acc)
    dt = _softplus(dt_ref[0] + dtb_ref[...])
    if mask_rows:
        valid = lax.broadcasted_iota(jnp.int32, (L, 1), 0) >= mask_rows
        xact = jnp.where(valid, xact, 0.0)
        dt = jnp.where(valid, dt, 0.0)

    a_neg = -jnp.exp(alog_ref[...])
    dA = dt * a_neg
    row_i = lax.broadcasted_iota(jnp.int32, (L, L), 0)
    col_i = lax.broadcasted_iota(jnp.int32, (L, L), 1)
    causal = row_i >= col_i
    tril = causal.astype(BF16)
    cs = _dot3_left(tril, dA)
    csT = cs.T
    dtT = dt.T
    ecs = jnp.exp(cs)
    wdec = jnp.exp(cs[L - 1:L, :] - cs) * dt
    eexp = eexp_ref[...]
    ecs_e = _dot3(ecs, eexp)
    wdec_e = _dot3(wdec, eexp)
    lane = lax.broadcasted_iota(jnp.int32, (L, LANES), 1)
    first_half = lane < SSD_HEAD_DIM

    gw = SSD_HPG * SSD_HEAD_DIM
    y_groups = []
    for g in range(SSD_GROUPS):
        b_g = xact[:, SSD_WIDTH + g * SSD_STATE: SSD_WIDTH + (g + 1) * SSD_STATE]
        c_g = xact[:, SSD_WIDTH + (SSD_GROUPS + g) * SSD_STATE: SSD_WIDTH + (SSD_GROUPS + g + 1) * SSD_STATE]
        b_b = b_g.astype(BF16)
        c_b = c_g.astype(BF16)
        cb = lax.dot_general(c_b, b_b, (((1,), (1,)), ((), ())), preferred_element_type=F32)
        xs_g = xact[:, g * gw:(g + 1) * gw]
        h_prev = hT[g]
        y_off = _dot(c_b, h_prev.astype(BF16)) * ecs_e[:, g * gw:(g + 1) * gw]
        xdec = (xs_g * wdec_e[:, g * gw:(g + 1) * gw]).astype(BF16)
        hT[g] = h_prev * ecs_e[L - 1:L, g * gw:(g + 1) * gw] + _dot(b_g.T.astype(BF16), xdec)
        pieces = []
        for j in range(SSD_HPG // 2):
            xs_pair = xs_g[:, j * LANES:(j + 1) * LANES]
            halves = (jnp.where(first_half, xs_pair, 0.0).astype(BF16),
                      jnp.where(first_half, 0.0, xs_pair).astype(BF16))
            yd = None
            for t in range(2):
                h = g * SSD_HPG + 2 * j + t
                seg = cs[:, h:h + 1] - csT[h:h + 1, :]
                lmat = jnp.exp(jnp.where(causal, seg, -jnp.inf))
                m = (cb * lmat * dtT[h:h + 1, :]).astype(BF16)
                part = _dot(m, halves[t])
                yd = part if yd is None else yd + part
            pieces.append(yd)
        y_groups.append(jnp.concatenate(pieces, axis=-1) + y_off + dexp_ref[:, g * gw:(g + 1) * gw] * xs_g)
    y = jnp.concatenate(y_groups, axis=-1)
    z = z_ref[0].astype(F32)
    y_ref[0] = _rms(y * (z * jax.nn.sigmoid(z)), nrm_ref[...]).astype(y_ref.dtype)

    @pl.when(c == pl.num_programs(1) - 1)
    def _emit():
        hto_ref[0] = hT[...]
        for g in range(SSD_GROUPS):
            t = hT[g].T
            for k in range(SSD_HPG):
                st_ref[0, g * SSD_HPG + k] = t[k * SSD_HEAD_DIM:(k + 1) * SSD_HEAD_DIM, :]


def _ssd_chunked(xbc, dt, z, cinit, hinit, cw, cb, dtb, alog, dexp, nrm, eexp, mask_rows):
    bsz, seq, _ = xbc.shape
    nc = seq // SSD_CHUNK
    gw = SSD_HPG * SSD_HEAD_DIM
    blk = lambda w: pl.BlockSpec((1, SSD_CHUNK, w), lambda b, c: (b, c, 0))
    return pl.pallas_call(
        functools.partial(_ssd_body, mask_rows),
        grid=(bsz, nc),
        in_specs=[blk(SSD_CONV_DIM), blk(LANES), blk(SSD_WIDTH),
                  pl.BlockSpec((1, SUBLANES, SSD_CONV_DIM), lambda b, c: (0, 0, 0)),
                  pl.BlockSpec((1, SSD_GROUPS, SSD_STATE, gw), lambda b, c: (0, 0, 0, 0)),
                  _full_spec(cw), _full_spec(cb), _full_spec(dtb), _full_spec(alog),
                  _full_spec(dexp), _full_spec(nrm), _full_spec(eexp)],
        out_specs=[blk(SSD_WIDTH),
                   pl.BlockSpec((1, SUBLANES, SSD_CONV_DIM), lambda b, c: (b, 0, 0)),
                   pl.BlockSpec((1, SSD_HEADS, SSD_HEAD_DIM, SSD_STATE), lambda b, c: (b, 0, 0, 0)),
                   pl.BlockSpec((1, SSD_GROUPS, SSD_STATE, gw), lambda b, c: (b, 0, 0, 0))],
        out_shape=[jax.ShapeDtypeStruct((bsz, seq, SSD_WIDTH), BF16),
                   jax.ShapeDtypeStruct((bsz, SUBLANES, SSD_CONV_DIM), F32),
                   jax.ShapeDtypeStruct((bsz, SSD_HEADS, SSD_HEAD_DIM, SSD_STATE), F32),
                   jax.ShapeDtypeStruct((bsz, SSD_GROUPS, SSD_STATE, gw), F32)],
        scratch_shapes=[pltpu.VMEM((SUBLANES + SSD_CHUNK, SSD_CONV_DIM), F32),
                        pltpu.VMEM((SSD_GROUPS, SSD_STATE, gw), F32)],
        compiler_params=pltpu.CompilerParams(dimension_semantics=("parallel", "arbitrary"),
                                             vmem_limit_bytes=VMEM_LIMIT),
        name="ssd_chunked",
    )(xbc, dt, z, cinit, hinit, cw, cb, dtb, alog, dexp, nrm, eexp)


def _ssd_step_prep_body(xbc_ref, c0_ref, c1_ref, c2_ref, dt_ref, cw_ref, cb_ref, dtb_ref, alog_ref,
                        xt_ref, dt_out_ref, dec_ref, bc_ref, xs_ref):
    acc = cb_ref[...]
    for k, r in enumerate((c0_ref, c1_ref, c2_ref, xbc_ref)):
        acc = acc + r[...] * cw_ref[k:k + 1, :]
    xact = acc * jax.nn.sigmoid(acc)
    xs = xact[:, :SSD_WIDTH]
    dt = _softplus(dt_ref[...] + dtb_ref[...])
    dt_out_ref[...] = dt
    dec_ref[...] = jnp.exp(dt * -jnp.exp(alog_ref[...]))
    bc_ref[...] = xact[:, SSD_WIDTH:]
    xs_ref[...] = xs
    xt_ref[...] = xs.T.astype(xt_ref.dtype)


def _ssd_step_prep(xbc, c0, c1, c2, dt, cw, cb, dtb, alog):
    n = xbc.shape[0]
    args = (xbc, c0, c1, c2, dt, cw, cb, dtb, alog)
    spec = lambda r, w: pl.BlockSpec((r, w), lambda: (0, 0))
    return pl.pallas_call(
        _ssd_step_prep_body,
        in_specs=[_full_spec(a) for a in args],
        out_specs=[spec(SSD_WIDTH, n), spec(n, LANES), spec(n, LANES), spec(n, 2 * SSD_GROUPS * SSD_STATE),
                   spec(n, SSD_WIDTH)],
        out_shape=[jax.ShapeDtypeStruct((SSD_WIDTH, n), BF16), jax.ShapeDtypeStruct((n, LANES), F32),
                   jax.ShapeDtypeStruct((n, LANES), F32),
                   jax.ShapeDtypeStruct((n, 2 * SSD_GROUPS * SSD_STATE), F32),
                   jax.ShapeDtypeStruct((n, SSD_WIDTH), F32)],
        compiler_params=pltpu.CompilerParams(vmem_limit_bytes=VMEM_LIMIT),
        name="ssd_step_prep",
    )(*args)


def _ssd_step_body(dt_ref, dec_ref, st_ref, xt_ref, bc_ref, so_ref, y_ref):
    n = xt_ref.shape[1]
    gw = SSD_HPG * SSD_HEAD_DIM
    blk = pl.program_id(0)
    seq_id = lax.broadcasted_iota(jnp.int32, (n, SSD_STATE), 0)
    sub_id = lax.broadcasted_iota(jnp.int32, (SUBLANES, gw), 0)
    base = pl.multiple_of(blk * SUBLANES, SUBLANES)
    y_acc = [jnp.zeros((SUBLANES, gw), F32) for _ in range(SSD_GROUPS)]
    for i in range(SUBLANES):
        s = blk * SUBLANES + i
        for g in range(SSD_GROUPS):
            b_all = bc_ref[:, g * SSD_STATE:(g + 1) * SSD_STATE]
            rhs = jnp.where(seq_id == s, b_all, 0.0).astype(BF16)
            outer = _dot(xt_ref[g * gw:(g + 1) * gw, :], rhs)
            news = []
            for k in range(SSD_HPG):
                h = g * SSD_HPG + k
                new = (dec_ref[s * SSD_HEADS + h] * st_ref[i, h]
                       + dt_ref[s * SSD_HEADS + h] * outer[k * SSD_HEAD_DIM:(k + 1) * SSD_HEAD_DIM, :])
                so_ref[i, h] = new
                news.append(new)
            new_g = jnp.concatenate(news, axis=0).astype(BF16)
            c_lo = (SSD_GROUPS + g) * SSD_STATE
            c_blk = bc_ref[pl.ds(base, SUBLANES), c_lo:c_lo + SSD_STATE].astype(BF16)
            r = lax.dot_general(c_blk, new_g, (((1,), (1,)), ((), ())), preferred_element_type=F32)
            y_acc[g] = y_acc[g] + jnp.where(sub_id == i, r, 0.0)
    y_ref[...] = jnp.concatenate(y_acc, axis=-1)


def _ssd_step(dt_flat, dec_flat, state, xt, bc):
    n = state.shape[0]
    st_spec = pl.BlockSpec((SUBLANES, SSD_HEADS, SSD_HEAD_DIM, SSD_STATE), lambda i, *_: (i, 0, 0, 0))
    return pl.pallas_call(
        _ssd_step_body,
        grid_spec=pltpu.PrefetchScalarGridSpec(
            num_scalar_prefetch=2,
            grid=(n // SUBLANES,),
            in_specs=[st_spec, pl.BlockSpec(xt.shape, lambda i, *_: (0, 0)),
                      pl.BlockSpec(bc.shape, lambda i, *_: (0, 0))],
            out_specs=[st_spec, pl.BlockSpec((SUBLANES, SSD_WIDTH), lambda i, *_: (i, 0))]),
        out_shape=[jax.ShapeDtypeStruct(state.shape, F32), jax.ShapeDtypeStruct((n, SSD_WIDTH), F32)],
        compiler_params=pltpu.CompilerParams(dimension_semantics=("parallel",), vmem_limit_bytes=VMEM_LIMIT),
        name="ssd_step",
    )(dt_flat, dec_flat, state, xt, bc)


def _s5_project_in(u_b16, wb_ref, store):
    kw = 16 * S5_GROUP_CH
    nw = 16 * S5_STATE
    for j in range(S5_WIDTH // kw):
        r = _dot(u_b16[:, j * kw:(j + 1) * kw], wb_ref[j])
        store(j, r[:, :nw], r[:, nw:])


def _s5_tail(hre_of, him_of, u_f32, wcr_ref, wci_ref, d_ref, wglu_ref, bglu_ref, nrm_ref):
    cols = []
    for j in range(wcr_ref.shape[0]):
        cols.append(_dot(hre_of(j).astype(BF16), wcr_ref[j]) + _dot(him_of(j).astype(BF16), wci_ref[j]))
    y = jnp.concatenate(cols, axis=-1) + d_ref[...] * u_f32
    y = jax.nn.gelu(y)
    y = y * jax.nn.sigmoid(_dot(y.astype(BF16), wglu_ref[...]) + bglu_ref[...])
    return _rms(y, nrm_ref[...])


def _s5_seq_body(u_ref, um_ref, wb_ref, abr_ref, abi_ref, wcr_ref, wci_ref, d_ref, wglu_ref, bglu_ref, nrm_ref,
                 y_ref, sre_ref, sim_ref, bu, h):
    j = pl.program_id(0)
    bsz, lc = u_ref.shape[0], u_ref.shape[1]
    nw = 16 * S5_STATE

    def scan(nsteps):
        for k in range(S5_LANES // S5_SCAN_LANES):
            sl_r = pl.ds(k * S5_SCAN_LANES, S5_SCAN_LANES)
            sl_i = pl.ds(S5_LANES + k * S5_SCAN_LANES, S5_SCAN_LANES)
            ar = abr_ref[:, sl_r]
            ai = abi_ref[:, sl_r]

            def step(l, carry):
                hr, hi = carry
                nr = ar * hr - ai * hi + bu[l, :, sl_r]
                ni = ar * hi + ai * hr + bu[l, :, sl_i]
                bu[l, :, sl_r] = nr
                bu[l, :, sl_i] = ni
                return nr, ni

            hr, hi = lax.fori_loop(0, nsteps, step, (h[:, sl_r], h[:, sl_i]))
            h[:, sl_r] = hr
            h[:, sl_i] = hi

    @pl.when(j == 0)
    def _meta():
        h[...] = jnp.zeros_like(h)

        def store(jj, re, im):
            for b in range(bsz):
                bu[0:N_META, b, jj * nw:(jj + 1) * nw] = re
                bu[0:N_META, b, S5_LANES + jj * nw:S5_LANES + (jj + 1) * nw] = im

        _s5_project_in(um_ref[...], wb_ref, store)
        scan(N_META)

    u2 = u_ref[...].reshape(bsz * lc, S5_WIDTH)

    def store(jj, re, im):
        for b in range(bsz):
            bu[:, b, jj * nw:(jj + 1) * nw] = re[b * lc:(b + 1) * lc, :]
            bu[:, b, S5_LANES + jj * nw:S5_LANES + (jj + 1) * nw] = im[b * lc:(b + 1) * lc, :]

    _s5_project_in(u2, wb_ref, store)
    scan(lc)

    def slab(base):
        return lambda jj: jnp.concatenate(
            [bu[:, b, base + jj * nw:base + (jj + 1) * nw] for b in range(bsz)], axis=0)

    y = _s5_tail(slab(0), slab(S5_LANES), u2.astype(F32), wcr_ref, wci_ref, d_ref, wglu_ref, bglu_ref, nrm_ref)
    y_ref[...] = y.astype(y_ref.dtype).reshape(bsz, lc, S5_WIDTH)

    @pl.when(j == pl.num_programs(0) - 1)
    def _emit():
        sre_ref[...] = h[:, 0:S5_LANES]
        sim_ref[...] = h[:, S5_LANES:]


def _s5_seq(u, um, wb, abr, abi, wcr, wci, d, wglu, bglu, nrm):
    bsz, seq, _ = u.shape
    lc = S5_TIME_TILE
    consts = (um, wb, abr, abi, wcr, wci, d, wglu, bglu, nrm)
    blk = pl.BlockSpec((bsz, lc, S5_WIDTH), lambda j: (0, j, 0))
    st = pl.BlockSpec((bsz, S5_LANES), lambda j: (0, 0))
    return pl.pallas_call(
        _s5_seq_body,
        grid=(seq // lc,),
        in_specs=[blk] + [_full_spec(a) for a in consts],
        out_specs=[blk, st, st],
        out_shape=[jax.ShapeDtypeStruct((bsz, seq, S5_WIDTH), BF16),
                   jax.ShapeDtypeStruct((bsz, S5_LANES), F32), jax.ShapeDtypeStruct((bsz, S5_LANES), F32)],
        scratch_shapes=[pltpu.VMEM((lc, bsz, 2 * S5_LANES), F32), pltpu.VMEM((bsz, 2 * S5_LANES), F32)],
        compiler_params=pltpu.CompilerParams(dimension_semantics=("arbitrary",), vmem_limit_bytes=VMEM_LIMIT),
        name="s5_seq",
    )(u, *consts)


def _sample_post_body(yc_ref, xs_ref, z_ref, dexp_ref, snrm_ref, u_ref, hr_ref, hi_ref, wb_ref, abr_ref, abi_ref,
                      wcr_ref, wci_ref, d_ref, wglu_ref, bglu_ref, nrm_ref,
                      yssd_ref, ys5_ref, nre_ref, nim_ref):
    z = z_ref[...]
    y = yc_ref[...] + dexp_ref[...] * xs_ref[...]
    yssd_ref[...] = _rms(y * (z * jax.nn.sigmoid(z)), snrm_ref[...]).astype(yssd_ref.dtype)

    u = u_ref[...]
    nw = 16 * S5_STATE
    ar, ai = abr_ref[...], abi_ref[...]

    def store(jj, re, im):
        sl = slice(jj * nw, (jj + 1) * nw)
        h0r, h0i = hr_ref[:, sl], hi_ref[:, sl]
        nre_ref[:, sl] = ar[:, sl] * h0r - ai[:, sl] * h0i + re
        nim_ref[:, sl] = ar[:, sl] * h0i + ai[:, sl] * h0r + im

    _s5_project_in(u.astype(BF16), wb_ref, store)
    slab = lambda ref: (lambda jj: ref[:, jj * nw:(jj + 1) * nw])
    y5 = _s5_tail(slab(nre_ref), slab(nim_ref), u, wcr_ref, wci_ref, d_ref, wglu_ref, bglu_ref, nrm_ref)
    ys5_ref[...] = y5.astype(ys5_ref.dtype)


def _sample_post(yc, xs, z, dexp, snrm, u, h0r, h0i, wb, abr1, abi1, wcr, wci, d, wglu, bglu, nrm):
    n = yc.shape[0]
    args = (yc, xs, z, dexp, snrm, u, h0r, h0i, wb, abr1, abi1, wcr, wci, d, wglu, bglu, nrm)
    spec = lambda w: pl.BlockSpec((n, w), lambda: (0, 0))
    return pl.pallas_call(
        _sample_post_body,
        in_specs=[_full_spec(a) for a in args],
        out_specs=[spec(SSD_WIDTH), spec(S5_WIDTH), spec(S5_LANES), spec(S5_LANES)],
        out_shape=[jax.ShapeDtypeStruct((n, SSD_WIDTH), BF16), jax.ShapeDtypeStruct((n, S5_WIDTH), BF16),
                   jax.ShapeDtypeStruct((n, S5_LANES), F32), jax.ShapeDtypeStruct((n, S5_LANES), F32)],
        compiler_params=pltpu.CompilerParams(vmem_limit_bytes=VMEM_LIMIT),
        name="sample_post",
    )(*args)


def _mix_route_body(n_blocks, cin_ref, *refs):
    x1_ref, xn_ref, rt_ref, cnt_ref, carry = refs[-5:]
    i = pl.program_id(0)

    @pl.when(i == 0)
    def _init():
        carry[...] = cin_ref[...]

    @pl.when(i < n_blocks)
    def _compute():
        _mix_route_compute(*refs)

    @pl.when(i >= n_blocks)
    def _fill():
        for ref in (x1_ref, xn_ref, rt_ref):
            ref[...] = jnp.zeros_like(ref)

    cnt_ref[...] = carry[...]


def _mix_route_compute(x_ref, ys_ref, y5_ref, wa_ref, wb_ref, nf_ref, wrh_ref, wrl_ref, br_ref, *rest):
    x1_ref, xn_ref, rt_ref, _, carry = rest[-5:]
    x1 = x_ref[...] + _dot(ys_ref[...], wa_ref[...]) + _dot(y5_ref[...], wb_ref[...])
    x1_ref[...] = x1
    xn = _rms(x1, nf_ref[...])
    for j in range(SLAB_ROWS):
        xn_ref[:, j, :] = xn[:, j * LANES:(j + 1) * LANES]

    xh = xn.astype(BF16)
    xl = (xn - xh.astype(F32)).astype(BF16)
    logits = _dot(xh, wrh_ref[...]) + _dot(xl, wrh_ref[...]) + _dot(xh, wrl_ref[...]) + br_ref[...]
    tm = logits.shape[0]
    lane = lax.broadcasted_iota(jnp.int32, logits.shape, 1).astype(F32)
    neg = -jnp.inf
    big = float(LANES)

    def first_max(v):
        m = jnp.max(v, axis=-1, keepdims=True)
        return m, jnp.min(jnp.where(v == m, lane, big), axis=-1, keepdims=True)

    coarse = lane < MOE_GROUPS
    mc, gsel = first_max(jnp.where(coarse, logits, neg))
    psel = 1.0 / jnp.sum(jnp.where(coarse, jnp.exp(logits - mc), 0.0), axis=-1, keepdims=True)
    lo = MOE_GROUPS + MOE_EPG * gsel
    lf = jnp.where((lane >= lo) & (lane < lo + MOE_EPG), logits, neg)
    m1, i1 = first_max(lf)
    m2, i2 = first_max(jnp.where(lane == i1, neg, lf))
    e2 = jnp.exp(m2 - m1)
    g1 = psel / (1.0 + e2)
    g2 = psel * e2 / (1.0 + e2)
    e_a, e_b = i1 - MOE_GROUPS, i2 - MOE_GROUPS

    pick_a, pick_b = lane == e_a, lane == e_b
    picks = jnp.where(pick_a | pick_b, 1.0, 0.0)
    earlier = lax.broadcasted_iota(jnp.int32, (tm, tm), 0) > lax.broadcasted_iota(jnp.int32, (tm, tm), 1)
    prior = _dot(earlier.astype(BF16), picks.astype(BF16)) + carry[...]
    rank_a = jnp.sum(jnp.where(pick_a, prior, 0.0), axis=-1, keepdims=True)
    rank_b = jnp.sum(jnp.where(pick_b, prior, 0.0), axis=-1, keepdims=True)
    carry[...] = prior[tm - 1:tm, :] + picks[tm - 1:tm, :]

    out = jnp.zeros_like(logits)
    for k, v in enumerate((e_a, e_b, g1, g2, rank_a, rank_b)):
        out = jnp.where(lane == float(k), v, out)
    rt_ref[...] = out


def _mix_route(counts_in, x, ys, y5, wa, wb, nf, wrh, wrl, br, tm, total_rows, row_block_offset, bufs):
    n_blocks = x.shape[0] // tm
    fill_tail = bufs is None and n_blocks * tm < total_rows
    row = lambda w: pl.BlockSpec((tm, w), lambda i: (jnp.minimum(i, n_blocks - 1), 0))
    out_row = lambda w: pl.BlockSpec((tm, w), lambda i: (i + row_block_offset, 0))
    consts = (wa, wb, nf, wrh, wrl, br)
    in_specs = ([_full_spec(counts_in), row(D_MODEL), row(SSD_WIDTH), row(S5_WIDTH)]
                + [_full_spec(a) for a in consts])
    args = [counts_in, x, ys, y5, *consts]
    aliases = {}
    if bufs is not None:
        for k, b in enumerate(bufs):
            in_specs.append(pl.BlockSpec(memory_space=pl.ANY))
            aliases[len(args)] = k
            args.append(b)
    return pl.pallas_call(
        functools.partial(_mix_route_body, n_blocks),
        grid=(n_blocks + int(fill_tail),),
        in_specs=in_specs,
        out_specs=[out_row(D_MODEL),
                   pl.BlockSpec((tm, SLAB_ROWS, LANES), lambda i: (i + row_block_offset, 0, 0)),
                   out_row(LANES), pl.BlockSpec((1, LANES), lambda i: (0, 0))],
        out_shape=[jax.ShapeDtypeStruct((total_rows, D_MODEL), F32),
                   jax.ShapeDtypeStruct((total_rows, SLAB_ROWS, LANES), F32),
                   jax.ShapeDtypeStruct((total_rows, LANES), F32), jax.ShapeDtypeStruct((1, LANES), F32)],
        scratch_shapes=[pltpu.VMEM((1, LANES), F32)],
        input_output_aliases=aliases,
        compiler_params=pltpu.CompilerParams(dimension_semantics=("arbitrary",), vmem_limit_bytes=VMEM_LIMIT),
        name="mix_route",
    )(*args)


def _from_slabs(ref):
    return jnp.concatenate([ref[:, j, :] for j in range(SLAB_ROWS)], axis=-1)


def _dispatch_body(pos_ref, tz_ref, xn_ref, xs_hbm, zbuf, xbuf, zsems, sems):
    n_tiles = tz_ref.shape[0]
    batch = DISPATCH_BATCH
    i = pl.program_id(0)
    last = pl.num_programs(0) - 1

    @pl.when(i == 0)
    def _zero_fill():
        zbuf[...] = jnp.zeros_like(zbuf)

        def zero_copy(t):
            return pltpu.make_async_copy(zbuf, xs_hbm.at[pl.ds(t * MOE_TILE, MOE_TILE)], zsems.at[t])

        def zero_start(t, carry):
            @pl.when(tz_ref[t] != 0)
            def _():
                zero_copy(t).start()
            return carry

        def zero_wait(t, carry):
            @pl.when(tz_ref[t] != 0)
            def _():
                zero_copy(t).wait()
            return carry

        lax.fori_loop(0, n_tiles, zero_start, 0)
        lax.fori_loop(0, n_tiles, zero_wait, 0)

    xbuf[i % 2] = xn_ref[...]

    def copy(step, r):
        return pltpu.make_async_copy(xbuf.at[step % 2, pl.ds(r >> 1, 1)],
                                     xs_hbm.at[pl.ds(pos_ref[step * batch + r], 1)], sems.at[step % 2, r])

    def start(r, carry):
        copy(i, r).start()
        return carry

    def wait_of(step):
        def wait(r, carry):
            copy(step, r).wait()
            return carry
        return wait

    lax.fori_loop(0, batch, start, 0, unroll=DMA_UNROLL)

    @pl.when(i > 0)
    def _wait_previous():
        lax.fori_loop(0, batch, wait_of(i - 1), 0, unroll=DMA_UNROLL)

    @pl.when(i == last)
    def _wait_own():
        lax.fori_loop(0, batch, wait_of(i), 0, unroll=DMA_UNROLL)


def _dispatch(pos_flat, tile_zero, xn, n_tiles):
    return pl.pallas_call(
        _dispatch_body,
        grid_spec=pltpu.PrefetchScalarGridSpec(
            num_scalar_prefetch=2,
            grid=(pos_flat.shape[0] // DISPATCH_BATCH,),
            in_specs=[pl.BlockSpec((DISPATCH_BATCH // 2, SLAB_ROWS, LANES), lambda i, *_: (i, 0, 0))],
            out_specs=pl.BlockSpec(memory_space=pl.ANY),
            scratch_shapes=[pltpu.VMEM((MOE_TILE, SLAB_ROWS, LANES), F32),
                            pltpu.VMEM((2, DISPATCH_BATCH // 2, SLAB_ROWS, LANES), F32),
                            pltpu.SemaphoreType.DMA((n_tiles,)),
                            pltpu.SemaphoreType.DMA((2, DISPATCH_BATCH))]),
        out_shape=jax.ShapeDtypeStruct((n_tiles * MOE_TILE, SLAB_ROWS, LANES), F32),
        compiler_params=pltpu.CompilerParams(dimension_semantics=("arbitrary",), vmem_limit_bytes=VMEM_LIMIT),
        name="moe_dispatch",
    )(pos_flat, tile_zero, xn)


def _moe_ffn_body(te_ref, nused_ref, x_ref, wg_ref, wu_ref, wd_ref, y_ref, wgb, wub, wdb):
    i = pl.program_id(0)

    @pl.when(i >= nused_ref[0])
    def _unused_tile():
        y_ref[...] = jnp.zeros_like(y_ref)

    @pl.when(i < nused_ref[0])
    def _tile():
        @pl.when((i == 0) | (te_ref[i] != te_ref[jnp.maximum(i - 1, 0)]))
        def _cast_weights():
            wgb[...] = wg_ref[0].astype(BF16)
            wub[...] = wu_ref[0].astype(BF16)
            wdb[...] = wd_ref[0].astype(BF16)

        x = _from_slabs(x_ref).astype(BF16)
        gate = _dot(x, wgb[...])
        hmid = (gate * jax.nn.sigmoid(gate)) * _dot(x, wub[...])
        y = _dot(hmid.astype(BF16), wdb[...])
        for j in range(SLAB_ROWS):
            y_ref[:, j, :] = y[:, j * LANES:(j + 1) * LANES]


def _moe_ffn(tile_expert, n_used, xsorted, w_gate, w_up, w_down, n_tiles):
    wspec = lambda s: pl.BlockSpec((1,) + s, lambda i, te, nu: (te[i], 0, 0))
    slab = lambda imap: pl.BlockSpec((MOE_TILE, SLAB_ROWS, LANES), imap)
    return pl.pallas_call(
        _moe_ffn_body,
        grid_spec=pltpu.PrefetchScalarGridSpec(
            num_scalar_prefetch=2,
            grid=(n_tiles,),
            in_specs=[slab(lambda i, te, nu: (jnp.clip(i, 0, jnp.maximum(nu[0] - 1, 0)), 0, 0)),
                      wspec((D_MODEL, MOE_D_FF)), wspec((D_MODEL, MOE_D_FF)), wspec((MOE_D_FF, D_MODEL))],
            out_specs=slab(lambda i, te, nu: (i, 0, 0)),
            scratch_shapes=[pltpu.VMEM((D_MODEL, MOE_D_FF), BF16), pltpu.VMEM((D_MODEL, MOE_D_FF), BF16),
                            pltpu.VMEM((MOE_D_FF, D_MODEL), BF16)]),
        out_shape=jax.ShapeDtypeStruct((n_tiles * MOE_TILE, SLAB_ROWS, LANES), F32),
        compiler_params=pltpu.CompilerParams(dimension_semantics=("arbitrary",), vmem_limit_bytes=VMEM_LIMIT),
        name="moe_ffn",
    )(tile_expert, n_used, xsorted, w_gate, w_up, w_down)


def _combine_body(tm, pos_ref, x1_ref, rt_ref, ys_hbm, nf_ref, out_ref, ybuf, sems):
    i = pl.program_id(0)
    rows = pl.num_programs(0) * tm

    def copy(k, r):
        return pltpu.make_async_copy(ys_hbm.at[pl.ds(pos_ref[k * rows + i * tm + r], 1)],
                                     ybuf.at[k, pl.ds(r, 1)], sems.at[k * tm + r])

    def start(r, carry):
        copy(0, r).start()
        copy(1, r).start()
        return carry

    def wait(r, carry):
        copy(0, r).wait()
        copy(1, r).wait()
        return carry

    lax.fori_loop(0, tm, start, 0, unroll=DMA_UNROLL)
    lax.fori_loop(0, tm, wait, 0, unroll=DMA_UNROLL)
    rt = rt_ref[...]
    x1 = x1_ref[...]
    x2 = jnp.concatenate(
        [x1[:, j * LANES:(j + 1) * LANES] + rt[:, 2:3] * ybuf[0, :, j, :] + rt[:, 3:4] * ybuf[1, :, j, :]
         for j in range(SLAB_ROWS)], axis=-1)
    out_ref[...] = _rms(x2, nf_ref[...])


def _combine(pos, x1, rt, ysorted, nf, tm, rows, row_block_offset):
    row = lambda w: pl.BlockSpec((tm, w), lambda i, p: (i + row_block_offset, 0))
    return pl.pallas_call(
        functools.partial(_combine_body, tm),
        grid_spec=pltpu.PrefetchScalarGridSpec(
            num_scalar_prefetch=1,
            grid=(rows // tm,),
            in_specs=[row(D_MODEL), row(LANES), pl.BlockSpec(memory_space=pl.ANY),
                      pl.BlockSpec((1, D_MODEL), lambda i, p: (0, 0))],
            out_specs=pl.BlockSpec((tm, D_MODEL), lambda i, p: (i, 0)),
            scratch_shapes=[pltpu.VMEM((2, tm, SLAB_ROWS, LANES), F32), pltpu.SemaphoreType.DMA((2 * tm,))]),
        out_shape=jax.ShapeDtypeStruct((rows, D_MODEL), F32),
        compiler_params=pltpu.CompilerParams(dimension_semantics=("arbitrary",), vmem_limit_bytes=VMEM_LIMIT),
        name="moe_combine",
    )(pos, x1, rt, ysorted, nf)


def _route_tables(counts, eid, rank, n_tiles):
    experts = jnp.arange(MOE_EXPERTS, dtype=jnp.int32)
    tiles_per = (counts + MOE_TILE - 1) // MOE_TILE
    tile_end = jnp.cumsum(tiles_per)
    pstart = (tile_end - tiles_per) * MOE_TILE
    pos = jnp.sum(jnp.where(eid[..., None] == experts, pstart, 0), axis=-1) + rank
    n_used = tile_end[-1]
    tiles = jnp.arange(n_tiles, dtype=jnp.int32)
    tile_expert = jnp.sum((tile_end[None, :] <= jnp.minimum(tiles, n_used - 1)[:, None]).astype(jnp.int32), axis=1)
    ragged = counts % MOE_TILE != 0
    tile_zero = (tiles >= n_used) | jnp.any((tiles[:, None] == tile_end[None, :] - 1) & ragged[None, :], axis=1)
    return pos, tile_expert, tile_zero.astype(jnp.int32), n_used.reshape(1).astype(jnp.int32)


def _s5_tables(a_re, a_im, log_dt, b_re, b_im, c_re, c_im):
    dt = jnp.exp(log_dt)[:, None]
    mag = jnp.exp(a_re * dt)
    ab_re = mag * jnp.cos(a_im * dt)
    ab_im = mag * jnp.sin(a_im * dt)
    den = a_re * a_re + a_im * a_im
    nr = ab_re - 1.0
    q_re = (nr * a_re + ab_im * a_im) / den
    q_im = (ab_im * a_re - nr * a_im) / den
    bb_re = q_re[..., None] * b_re - q_im[..., None] * b_im
    bb_im = q_re[..., None] * b_im + q_im[..., None] * b_re
    eye = jnp.eye(16, dtype=F32)
    nblk = S5_GROUPS // 16

    def in_map(bb):
        w = jnp.einsum("jgpc,gh->jgchp", bb.reshape(nblk, 16, S5_STATE, S5_GROUP_CH), eye)
        return w.reshape(nblk, 16 * S5_GROUP_CH, 16 * S5_STATE)

    def out_map(cc):
        w = jnp.einsum("jgcp,gh->jgphc", cc.reshape(nblk, 16, S5_GROUP_CH, S5_STATE), eye)
        return w.reshape(nblk, 16 * S5_STATE, 16 * S5_GROUP_CH)

    wb = jnp.concatenate([in_map(bb_re), in_map(bb_im)], axis=-1).astype(BF16)
    return (wb, ab_re.reshape(1, S5_LANES), ab_im.reshape(1, S5_LANES),
            out_map(c_re).astype(BF16), out_map(-c_im).astype(BF16))


def kernel(x_prompt, x_sample, state_ssd_conv, state_ssd_ssm, state_s5_re, state_s5_im, meta_tokens, norm_mix, w_in, conv_w, conv_b, dt_bias, a_log, d_ssd, ssd_norm, s5_a_re, s5_a_im, s5_log_dt, s5_b_re, s5_b_im, s5_c_re, s5_c_im, s5_d, w_glu, b_glu, s5_norm, w_out, norm_ffn, router_coarse_w, router_coarse_b, router_fine_w, router_fine_b, w_gate, w_up, w_down, norm_final):
    bp, seq, _ = x_prompt.shape
    bs = x_sample.shape[0]
    n_prompt = bp * seq
    n_tok = n_prompt + bs
    row2 = lambda v: v.reshape(1, -1)
    pad_heads = lambda v: jnp.pad(v, (0, LANES - SSD_HEADS)).reshape(1, LANES)

    w = w_in[0]
    o1, o2, o3 = SSD_WIDTH, SSD_WIDTH + SSD_CONV_DIM, SSD_WIDTH + SSD_CONV_DIM + SSD_HEADS
    wz, wx, wu = w[:, :o1].astype(BF16), w[:, o1:o2].astype(BF16), w[:, o3:].astype(BF16)
    wdt = jnp.pad(w[:, o2:o3], ((0, 0), (0, LANES - SSD_HEADS))).astype(BF16)
    g_mix = row2(norm_mix[0])
    cw, cb = conv_w[0], row2(conv_b[0])
    dtb, alog = pad_heads(dt_bias[0]), pad_heads(a_log[0])
    dexp = row2(jnp.repeat(d_ssd[0], SSD_HEAD_DIM))
    snrm = row2(ssd_norm[0])
    eexp = (jnp.arange(LANES)[:, None] == (jnp.arange(SSD_WIDTH) // SSD_HEAD_DIM)[None, :]).astype(BF16)
    wb5, ab_re, ab_im, wcr, wci = _s5_tables(s5_a_re[0], s5_a_im[0], s5_log_dt[0], s5_b_re[0], s5_b_im[0],
                                             s5_c_re[0], s5_c_im[0])
    d5, wglu, bglu, nrm5 = row2(s5_d[0]), w_glu[0].astype(BF16), row2(b_glu[0]), row2(s5_norm[0])
    wo_a, wo_b = w_out[0][:SSD_WIDTH].astype(BF16), w_out[0][SSD_WIDTH:].astype(BF16)
    w_r = jnp.concatenate([router_coarse_w[0], router_fine_w[0].transpose(1, 0, 2).reshape(D_MODEL, MOE_EXPERTS)], axis=1)
    w_r = jnp.pad(w_r, ((0, 0), (0, LANES - w_r.shape[1])))
    wrh = w_r.astype(BF16)
    wrl = (w_r - wrh.astype(F32)).astype(BF16)
    b_r = jnp.concatenate([router_coarse_b[0], router_fine_b[0].reshape(-1)])
    b_r = jnp.pad(b_r, (0, LANES - b_r.shape[0])).reshape(1, LANES)

    zp, xbcp, dtp, up = _in_proj(x_prompt.reshape(n_prompt, D_MODEL), g_mix, wz, wx, wdt, wu, TOK_TILE, BF16)
    xsm = jnp.concatenate([x_sample.reshape(bs, D_MODEL), meta_tokens], axis=0)
    zs, xbcs, dts, us = _in_proj(xsm, g_mix, wz, wx, wdt, wu, xsm.shape[0], F32)

    front = SSD_CHUNK - N_META
    padf = lambda a: jnp.pad(a[bs:], ((front, 0), (0, 0)))[None]
    gw = SSD_HPG * SSD_HEAD_DIM
    ssd_consts = (cw, cb, dtb, alog, dexp, snrm, eexp)
    _, ctail_m, _, ht_m = _ssd_chunked(
        padf(xbcs), padf(dts), jnp.zeros((1, SSD_CHUNK, SSD_WIDTH), F32),
        jnp.zeros((1, SUBLANES, SSD_CONV_DIM), F32), jnp.zeros((1, SSD_GROUPS, SSD_STATE, gw), F32),
        *ssd_consts, mask_rows=front)
    y_ssd_p, ctail_p, ssm_p, _ = _ssd_chunked(
        xbcp.reshape(bp, seq, SSD_CONV_DIM), dtp.reshape(bp, seq, LANES), zp.reshape(bp, seq, SSD_WIDTH),
        ctail_m, ht_m, *ssd_consts, mask_rows=0)

    abr8, abi8 = jnp.broadcast_to(ab_re, (bp, S5_LANES)), jnp.broadcast_to(ab_im, (bp, S5_LANES))
    y_s5_p, s5re_p, s5im_p = _s5_seq(up.reshape(bp, seq, S5_WIDTH), us[bs:].astype(BF16), wb5, abr8, abi8,
                                     wcr, wci, d5, wglu, bglu, nrm5)

    cst = state_ssd_conv[0]
    xt_s, dt_s, dec_s, bc, xs_s = _ssd_step_prep(xbcs[:bs], cst[:, 0], cst[:, 1], cst[:, 2], dts[:bs],
                                                 cw, cb, dtb, alog)
    ssm_s, y_core = _ssd_step(dt_s[:, :SSD_HEADS].reshape(-1), dec_s[:, :SSD_HEADS].reshape(-1),
                              state_ssd_ssm[0], xt_s, bc)
    y_ssd_s, y_s5_s, s5re_s, s5im_s = _sample_post(
        y_core, xs_s, zs[:bs], dexp, snrm, us[:bs], state_s5_re[0].reshape(bs, S5_LANES),
        state_s5_im[0].reshape(bs, S5_LANES), wb5, ab_re, ab_im, wcr, wci, d5, wglu, bglu, nrm5)

    route_consts = (wo_a, wo_b, row2(norm_ffn[0]), wrh, wrl, b_r)
    *bufs, counts_p = _mix_route(jnp.zeros((1, LANES), F32), x_prompt.reshape(n_prompt, D_MODEL),
                                 y_ssd_p.reshape(n_prompt, SSD_WIDTH), y_s5_p.reshape(n_prompt, S5_WIDTH),
                                 *route_consts, TOK_TILE, n_tok, 0, None)
    x1, xn, rt, counts = _mix_route(counts_p, x_sample.reshape(bs, D_MODEL), y_ssd_s, y_s5_s, *route_consts,
                                    bs, n_tok, n_prompt // bs, bufs)

    n_tiles = -(-2 * n_tok // MOE_TILE) + MOE_EXPERTS
    eid = jnp.clip(rt[:, 0:2].astype(jnp.int32), 0, MOE_EXPERTS - 1)
    pos, tile_expert, tile_zero, n_used = _route_tables(counts[0, :MOE_EXPERTS].astype(jnp.int32), eid,
                                                        rt[:, 4:6].astype(jnp.int32), n_tiles)
    xsorted = _dispatch(pos.reshape(-1), tile_zero, xn, n_tiles)
    ysorted = _moe_ffn(tile_expert, n_used, xsorted, w_gate[0], w_up[0], w_down[0], n_tiles)
    nfin = row2(norm_final)
    y_p = _combine(pos[:n_prompt].T.reshape(-1), x1, rt, ysorted, nfin, MOE_TILE, n_prompt, 0)
    y_s = _combine(pos[n_prompt:].T.reshape(-1), x1, rt, ysorted, nfin, bs, bs, n_prompt // bs)

    s5_state = lambda a, b: a.reshape(1, b, S5_GROUPS, S5_STATE)
    new_conv_s = jnp.stack([cst[:, 1], cst[:, 2], xbcs[:bs]], axis=1)[None]
    return (y_p.reshape(bp, seq, D_MODEL), y_s.reshape(bs, 1, D_MODEL),
            ctail_p[:, SUBLANES - (SSD_CONV - 1):][None], ssm_p[None], s5_state(s5re_p, bp), s5_state(s5im_p, bp),
            new_conv_s, ssm_s[None], s5_state(s5re_s, bs), s5_state(s5im_s, bs))
```

```python
import functools

import jax
import jax.numpy as jnp
from jax import lax
from jax.experimental import pallas as pl
from jax.experimental.pallas import tpu as pltpu

F32, BF16 = jnp.float32, jnp.bfloat16

D_MODEL = 1024
N_META = 16
SSD_WIDTH = 1024
SSD_HEAD_DIM = 64
SSD_HEADS = 16
SSD_GROUPS = 2
SSD_HPG = SSD_HEADS // SSD_GROUPS
SSD_STATE = 128
SSD_CONV = 4
SSD_CHUNK = 128
SSD_CONV_DIM = SSD_WIDTH + 2 * SSD_GROUPS * SSD_STATE
S5_WIDTH = 1024
S5_GROUP_CH = 16
S5_GROUPS = 64
S5_STATE = 64
S5_LANES = S5_GROUPS * S5_STATE
MOE_GROUPS = 4
MOE_EPG = 8
MOE_EXPERTS = MOE_GROUPS * MOE_EPG
MOE_D_FF = 512
EPS = 1e-6

LANES = 128
SUBLANES = 8
VMEM_LIMIT = 56 * 1024 * 1024

S5_TIME_TILE = 32
S5_SCAN_LANES = 512
MOE_TILE = 256
SLAB_ROWS = D_MODEL // LANES
DISPATCH_BATCH = 256
DMA_UNROLL = 8
TOK_TILE = 512


def _dot(a, b):
    return jnp.dot(a, b, preferred_element_type=F32)


def _rms(x, g):
    return x * lax.rsqrt(jnp.mean(x * x, axis=-1, keepdims=True) + EPS) * g


def _softplus(x):
    return jnp.maximum(x, 0.0) + jnp.log1p(jnp.exp(-jnp.abs(x)))


def _split3(x):
    hi = x.astype(BF16)
    r = x - hi.astype(F32)
    mid = r.astype(BF16)
    lo = (r - mid.astype(F32)).astype(BF16)
    return hi, mid, lo


def _dot3(x, w):
    hi, mid, lo = _split3(x)
    return _dot(hi, w) + _dot(mid, w) + _dot(lo, w)


def _dot3_left(w, x):
    hi, mid, lo = _split3(x)
    return _dot(w, hi) + _dot(w, mid) + _dot(w, lo)


def _full_spec(a):
    nd = a.ndim
    return pl.BlockSpec(a.shape, lambda *_: (0,) * nd)


def _in_proj_body(x_ref, g_ref, wz_ref, wx_ref, wdt_ref, wu_ref, z_ref, xbc_ref, dt_ref, u_ref):
    xb = _rms(x_ref[...], g_ref[...]).astype(BF16)
    z_ref[...] = _dot(xb, wz_ref[...]).astype(z_ref.dtype)
    xbc_ref[...] = _dot(xb, wx_ref[...]).astype(xbc_ref.dtype)
    dt_ref[...] = _dot(xb, wdt_ref[...])
    u_ref[...] = _dot(xb, wu_ref[...]).astype(u_ref.dtype)


def _in_proj(x2d, g, wz, wx, wdt, wu, tm, act_dtype, u_dtype):
    rows = x2d.shape[0]
    row = lambda w: pl.BlockSpec((tm, w), lambda i: (i, 0))
    return pl.pallas_call(
        _in_proj_body,
        grid=(rows // tm,),
        in_specs=[row(D_MODEL), _full_spec(g), _full_spec(wz), _full_spec(wx), _full_spec(wdt), _full_spec(wu)],
        out_specs=[row(SSD_WIDTH), row(SSD_CONV_DIM), row(LANES), row(S5_WIDTH)],
        out_shape=[jax.ShapeDtypeStruct((rows, SSD_WIDTH), act_dtype),
                   jax.ShapeDtypeStruct((rows, SSD_CONV_DIM), act_dtype),
                   jax.ShapeDtypeStruct((rows, LANES), F32),
                   jax.ShapeDtypeStruct((rows, S5_WIDTH), u_dtype)],
        compiler_params=pltpu.CompilerParams(dimension_semantics=("parallel",), vmem_limit_bytes=VMEM_LIMIT),
        name="in_proj",
    )(x2d, g, wz, wx, wdt, wu)


def _ssd_body(mask_rows, xbc_ref, dt_ref, z_ref, cinit_ref, hinit_ref, cw_ref, cb_ref, dtb_ref, alog_ref,
              dexp_ref, nrm_ref, eexp_ref, y_ref, ctail_ref, st_ref, hto_ref, xwin, hT):
    c = pl.program_id(1)
    L = SSD_CHUNK

    @pl.when(c == 0)
    def _init():
        xwin[0:SUBLANES, :] = cinit_ref[0]
        hT[...] = hinit_ref[0]

    xwin[SUBLANES:SUBLANES + L, :] = xbc_ref[0].astype(F32)
    acc = cb_ref[...]
    for k in range(SSD_CONV):
        off = SUBLANES - (SSD_CONV - 1) + k
        acc = acc + xwin[off:off + L, :] * cw_ref[k:k + 1, :]
    tail = xwin[L:L + SUBLANES, :]
    xwin[0:SUBLANES, :] = tail
    ctail_ref[0] = tail

    xact = acc * jax.nn.sigmoid(acc)
    dt = _softplus(dt_ref[0] + dtb_ref[...])
    if mask_rows:
        valid = lax.broadcasted_iota(jnp.int32, (L, 1), 0) >= mask_rows
        xact = jnp.where(valid, xact, 0.0)
        dt = jnp.where(valid, dt, 0.0)

    a_neg = -jnp.exp(alog_ref[...])
    dA = dt * a_neg
    row_i = lax.broadcasted_iota(jnp.int32, (L, L), 0)
    col_i = lax.broadcasted_iota(jnp.int32, (L, L), 1)
    causal = row_i >= col_i
    tril = causal.astype(BF16)
    cs = _dot3_left(tril, dA)
    csT = cs.T
    dtT = dt.T
    ecs = jnp.exp(cs)
    wdec = jnp.exp(cs[L - 1:L, :] - cs) * dt
    eexp = eexp_ref[...]
    ecs_e = _dot3(ecs, eexp)
    wdec_e = _dot3(wdec, eexp)
    lane = lax.broadcasted_iota(jnp.int32, (L, LANES), 1)
    first_half = lane < SSD_HEAD_DIM

    gw = SSD_HPG * SSD_HEAD_DIM
    y_groups = []
    for g in range(SSD_GROUPS):
        b_g = xact[:, SSD_WIDTH + g * SSD_STATE: SSD_WIDTH + (g + 1) * SSD_STATE]
        c_g = xact[:, SSD_WIDTH + (SSD_GROUPS + g) * SSD_STATE: SSD_WIDTH + (SSD_GROUPS + g + 1) * SSD_STATE]
        b_b = b_g.astype(BF16)
        c_b = c_g.astype(BF16)
        cb = lax.dot_general(c_b, b_b, (((1,), (1,)), ((), ())), preferred_element_type=F32)
        xs_g = xact[:, g * gw:(g + 1) * gw]
        h_prev = hT[g]
        y_off = _dot(c_b, h_prev.astype(BF16)) * ecs_e[:, g * gw:(g + 1) * gw]
        xdec = (xs_g * wdec_e[:, g * gw:(g + 1) * gw]).astype(BF16)
        hT[g] = h_prev * ecs_e[L - 1:L, g * gw:(g + 1) * gw] + _dot(b_g.T.astype(BF16), xdec)
        pieces = []
        for j in range(SSD_HPG // 2):
            xs_pair = xs_g[:, j * LANES:(j + 1) * LANES]
            halves = (jnp.where(first_half, xs_pair, 0.0).astype(BF16),
                      jnp.where(first_half, 0.0, xs_pair).astype(BF16))
            yd = None
            for t in range(2):
                h = g * SSD_HPG + 2 * j + t
                seg = cs[:, h:h + 1] - csT[h:h + 1, :]
                lmat = jnp.exp(jnp.where(causal, seg, -jnp.inf))
                m = (cb * lmat * dtT[h:h + 1, :]).astype(BF16)
                part = _dot(m, halves[t])
                yd = part if yd is None else yd + part
            pieces.append(yd)
        y_groups.append(jnp.concatenate(pieces, axis=-1) + y_off + dexp_ref[:, g * gw:(g + 1) * gw] * xs_g)
    y = jnp.concatenate(y_groups, axis=-1)
    z = z_ref[0].astype(F32)
    y_ref[0] = _rms(y * (z * jax.nn.sigmoid(z)), nrm_ref[...]).astype(y_ref.dtype)

    @pl.when(c == pl.num_programs(1) - 1)
    def _emit():
        hto_ref[0] = hT[...]
        for g in range(SSD_GROUPS):
            t = hT[g].T
            for k in range(SSD_HPG):
                st_ref[0, g * SSD_HPG + k] = t[k * SSD_HEAD_DIM:(k + 1) * SSD_HEAD_DIM, :]


def _ssd_chunked(xbc, dt, z, cinit, hinit, cw, cb, dtb, alog, dexp, nrm, eexp, mask_rows):
    bsz, seq, _ = xbc.shape
    nc = seq // SSD_CHUNK
    gw = SSD_HPG * SSD_HEAD_DIM
    blk = lambda w: pl.BlockSpec((1, SSD_CHUNK, w), lambda b, c: (b, c, 0))
    return pl.pallas_call(
        functools.partial(_ssd_body, mask_rows),
        grid=(bsz, nc),
        in_specs=[blk(SSD_CONV_DIM), blk(LANES), blk(SSD_WIDTH),
                  pl.BlockSpec((1, SUBLANES, SSD_CONV_DIM), lambda b, c: (0, 0, 0)),
                  pl.BlockSpec((1, SSD_GROUPS, SSD_STATE, gw), lambda b, c: (0, 0, 0, 0)),
                  _full_spec(cw), _full_spec(cb), _full_spec(dtb), _full_spec(alog),
                  _full_spec(dexp), _full_spec(nrm), _full_spec(eexp)],
        out_specs=[blk(SSD_WIDTH),
                   pl.BlockSpec((1, SUBLANES, SSD_CONV_DIM), lambda b, c: (b, 0, 0)),
                   pl.BlockSpec((1, SSD_HEADS, SSD_HEAD_DIM, SSD_STATE), lambda b, c: (b, 0, 0, 0)),
                   pl.BlockSpec((1, SSD_GROUPS, SSD_STATE, gw), lambda b, c: (b, 0, 0, 0))],
        out_shape=[jax.ShapeDtypeStruct((bsz, seq, SSD_WIDTH), BF16),
                   jax.ShapeDtypeStruct((bsz, SUBLANES, SSD_CONV_DIM), F32),
                   jax.ShapeDtypeStruct((bsz, SSD_HEADS, SSD_HEAD_DIM, SSD_STATE), F32),
                   jax.ShapeDtypeStruct((bsz, SSD_GROUPS, SSD_STATE, gw), F32)],
        scratch_shapes=[pltpu.VMEM((SUBLANES + SSD_CHUNK, SSD_CONV_DIM), F32),
                        pltpu.VMEM((SSD_GROUPS, SSD_STATE, gw), F32)],
        compiler_params=pltpu.CompilerParams(dimension_semantics=("parallel", "arbitrary"),
                                             vmem_limit_bytes=VMEM_LIMIT),
        name="ssd_chunked",
    )(xbc, dt, z, cinit, hinit, cw, cb, dtb, alog, dexp, nrm, eexp)


def _ssd_step_prep_body(xbc_ref, c0_ref, c1_ref, c2_ref, dt_ref, cw_ref, cb_ref, dtb_ref, alog_ref,
                        xt_ref, dt_out_ref, dec_ref, bc_ref, xs_ref):
    acc = cb_ref[...]
    for k, r in enumerate((c0_ref, c1_ref, c2_ref, xbc_ref)):
        acc = acc + r[...] * cw_ref[k:k + 1, :]
    xact = acc * jax.nn.sigmoid(acc)
    xs = xact[:, :SSD_WIDTH]
    dt = _softplus(dt_ref[...] + dtb_ref[...])
    dt_out_ref[...] = dt
    dec_ref[...] = jnp.exp(dt * -jnp.exp(alog_ref[...]))
    bc_ref[...] = xact[:, SSD_WIDTH:]
    xs_ref[...] = xs
    xt_ref[...] = xs.T.astype(xt_ref.dtype)


def _ssd_step_prep(xbc, c0, c1, c2, dt, cw, cb, dtb, alog):
    n = xbc.shape[0]
    args = (xbc, c0, c1, c2, dt, cw, cb, dtb, alog)
    spec = lambda r, w: pl.BlockSpec((r, w), lambda: (0, 0))
    return pl.pallas_call(
        _ssd_step_prep_body,
        in_specs=[_full_spec(a) for a in args],
        out_specs=[spec(SSD_WIDTH, n), spec(n, LANES), spec(n, LANES), spec(n, 2 * SSD_GROUPS * SSD_STATE),
                   spec(n, SSD_WIDTH)],
        out_shape=[jax.ShapeDtypeStruct((SSD_WIDTH, n), BF16), jax.ShapeDtypeStruct((n, LANES), F32),
                   jax.ShapeDtypeStruct((n, LANES), F32),
                   jax.ShapeDtypeStruct((n, 2 * SSD_GROUPS * SSD_STATE), F32),
                   jax.ShapeDtypeStruct((n, SSD_WIDTH), F32)],
        compiler_params=pltpu.CompilerParams(vmem_limit_bytes=VMEM_LIMIT),
        name="ssd_step_prep",
    )(*args)


def _ssd_step_body(dt_ref, dec_ref, st_ref, xt_ref, bc_ref, so_ref, y_ref):
    n = xt_ref.shape[1]
    gw = SSD_HPG * SSD_HEAD_DIM
    blk = pl.program_id(0)
    seq_id = lax.broadcasted_iota(jnp.int32, (n, SSD_STATE), 0)
    sub_id = lax.broadcasted_iota(jnp.int32, (SUBLANES, gw), 0)
    base = pl.multiple_of(blk * SUBLANES, SUBLANES)
    y_acc = [jnp.zeros((SUBLANES, gw), F32) for _ in range(SSD_GROUPS)]
    for i in range(SUBLANES):
        s = blk * SUBLANES + i
        for g in range(SSD_GROUPS):
            b_all = bc_ref[:, g * SSD_STATE:(g + 1) * SSD_STATE]
            rhs = jnp.where(seq_id == s, b_all, 0.0).astype(BF16)
            outer = _dot(xt_ref[g * gw:(g + 1) * gw, :], rhs)
            news = []
            for k in range(SSD_HPG):
                h = g * SSD_HPG + k
                new = (dec_ref[s * SSD_HEADS + h] * st_ref[i, h]
                       + dt_ref[s * SSD_HEADS + h] * outer[k * SSD_HEAD_DIM:(k + 1) * SSD_HEAD_DIM, :])
                so_ref[i, h] = new
                news.append(new)
            new_g = jnp.concatenate(news, axis=0).astype(BF16)
            c_lo = (SSD_GROUPS + g) * SSD_STATE
            c_blk = bc_ref[pl.ds(base, SUBLANES), c_lo:c_lo + SSD_STATE].astype(BF16)
            r = lax.dot_general(c_blk, new_g, (((1,), (1,)), ((), ())), preferred_element_type=F32)
            y_acc[g] = y_acc[g] + jnp.where(sub_id == i, r, 0.0)
    y_ref[...] = jnp.concatenate(y_acc, axis=-1)


def _ssd_step(dt_flat, dec_flat, state, xt, bc):
    n = state.shape[0]
    st_spec = pl.BlockSpec((SUBLANES, SSD_HEADS, SSD_HEAD_DIM, SSD_STATE), lambda i, *_: (i, 0, 0, 0))
    return pl.pallas_call(
        _ssd_step_body,
        grid_spec=pltpu.PrefetchScalarGridSpec(
            num_scalar_prefetch=2,
            grid=(n // SUBLANES,),
            in_specs=[st_spec, pl.BlockSpec(xt.shape, lambda i, *_: (0, 0)),
                      pl.BlockSpec(bc.shape, lambda i, *_: (0, 0))],
            out_specs=[st_spec, pl.BlockSpec((SUBLANES, SSD_WIDTH), lambda i, *_: (i, 0))]),
        out_shape=[jax.ShapeDtypeStruct(state.shape, F32), jax.ShapeDtypeStruct((n, SSD_WIDTH), F32)],
        compiler_params=pltpu.CompilerParams(dimension_semantics=("parallel",), vmem_limit_bytes=VMEM_LIMIT),
        name="ssd_step",
    )(dt_flat, dec_flat, state, xt, bc)


def _s5_project_in(u_b16, wb_ref, store):
    kw = 16 * S5_GROUP_CH
    nw = 16 * S5_STATE
    for j in range(S5_WIDTH // kw):
        r = _dot(u_b16[:, j * kw:(j + 1) * kw], wb_ref[j])
        store(j, r[:, :nw], r[:, nw:])


def _s5_tail(hre_of, him_of, u_f32, wcr_ref, wci_ref, d_ref, wglu_ref, bglu_ref, nrm_ref):
    cols = []
    for j in range(wcr_ref.shape[0]):
        cols.append(_dot(hre_of(j).astype(BF16), wcr_ref[j]) + _dot(him_of(j).astype(BF16), wci_ref[j]))
    y = jnp.concatenate(cols, axis=-1) + d_ref[...] * u_f32
    y = jax.nn.gelu(y)
    y = y * jax.nn.sigmoid(_dot(y.astype(BF16), wglu_ref[...]) + bglu_ref[...])
    return _rms(y, nrm_ref[...])


def _s5_seq_body(u_hbm, um_ref, wb_ref, abr_ref, abi_ref, wcr_ref, wci_ref, d_ref, wglu_ref, bglu_ref, nrm_ref,
                 y_hbm, sre_ref, sim_ref, ubuf, ybuf, bu, h, in_sems, out_sems):
    j = pl.program_id(0)
    last = pl.num_programs(0) - 1
    lc, bsz = ubuf.shape[1], ubuf.shape[2]
    rows = lc * bsz
    nw = 16 * S5_STATE

    def in_copy(step, b):
        return pltpu.make_async_copy(u_hbm.at[b, pl.ds(step * lc, lc), :], ubuf.at[step % 2, :, b, :],
                                     in_sems.at[step % 2, b])

    def out_copy(step, b):
        return pltpu.make_async_copy(ybuf.at[step % 2, :, b, :], y_hbm.at[b, pl.ds(step * lc, lc), :],
                                     out_sems.at[step % 2, b])

    def project_in(u_b16, nrows):
        def store(jj, re, im):
            bu[0:nrows, jj * nw:(jj + 1) * nw] = re
            bu[0:nrows, S5_LANES + jj * nw:S5_LANES + (jj + 1) * nw] = im
        _s5_project_in(u_b16, wb_ref, store)

    def scan(nsteps):
        for k in range(S5_LANES // S5_SCAN_LANES):
            sl_r = pl.ds(k * S5_SCAN_LANES, S5_SCAN_LANES)
            sl_i = pl.ds(S5_LANES + k * S5_SCAN_LANES, S5_SCAN_LANES)
            ar = abr_ref[:, sl_r]
            ai = abi_ref[:, sl_r]

            def step(l, carry):
                hr, hi = carry
                slab = pl.ds(pl.multiple_of(l * bsz, bsz), bsz)
                nr = ar * hr - ai * hi + bu[slab, sl_r]
                ni = ar * hi + ai * hr + bu[slab, sl_i]
                bu[slab, sl_r] = nr
                bu[slab, sl_i] = ni
                return nr, ni

            hr, hi = lax.fori_loop(0, nsteps, step, (h[:, sl_r], h[:, sl_i]))
            h[:, sl_r] = hr
            h[:, sl_i] = hi

    @pl.when(j == 0)
    def _first():
        for b in range(bsz):
            in_copy(0, b).start()
        h[...] = jnp.zeros_like(h)
        project_in(um_ref[...], N_META * bsz)
        scan(N_META)

    @pl.when(j < last)
    def _prefetch():
        for b in range(bsz):
            in_copy(j + 1, b).start()

    for b in range(bsz):
        in_copy(j, b).wait()
    u2 = ubuf[j % 2].reshape(rows, S5_WIDTH)
    project_in(u2.astype(BF16), rows)
    scan(lc)
    y = _s5_tail(lambda jj: bu[:, jj * nw:(jj + 1) * nw], lambda jj: bu[:, S5_LANES + jj * nw:S5_LANES + (jj + 1) * nw],
                 u2, wcr_ref, wci_ref, d_ref, wglu_ref, bglu_ref, nrm_ref)
    ybuf[j % 2] = y.reshape(lc, bsz, S5_WIDTH)
    for b in range(bsz):
        out_copy(j, b).start()

    @pl.when(j > 0)
    def _wait_previous_out():
        for b in range(bsz):
            out_copy(j - 1, b).wait()

    @pl.when(j == last)
    def _emit():
        for b in range(bsz):
            out_copy(j, b).wait()
        sre_ref[...] = h[:, 0:S5_LANES]
        sim_ref[...] = h[:, S5_LANES:]


def _s5_seq(u, um, wb, abr, abi, wcr, wci, d, wglu, bglu, nrm):
    bsz, seq, _ = u.shape
    lc = S5_TIME_TILE
    consts = (um, wb, abr, abi, wcr, wci, d, wglu, bglu, nrm)
    st = pl.BlockSpec((bsz, S5_LANES), lambda j: (0, 0))
    return pl.pallas_call(
        _s5_seq_body,
        grid=(seq // lc,),
        in_specs=[pl.BlockSpec(memory_space=pl.ANY)] + [_full_spec(a) for a in consts],
        out_specs=[pl.BlockSpec(memory_space=pl.ANY), st, st],
        out_shape=[jax.ShapeDtypeStruct((bsz, seq, S5_WIDTH), F32),
                   jax.ShapeDtypeStruct((bsz, S5_LANES), F32), jax.ShapeDtypeStruct((bsz, S5_LANES), F32)],
        scratch_shapes=[pltpu.VMEM((2, lc, bsz, S5_WIDTH), F32), pltpu.VMEM((2, lc, bsz, S5_WIDTH), F32),
                        pltpu.VMEM((lc * bsz, 2 * S5_LANES), F32), pltpu.VMEM((bsz, 2 * S5_LANES), F32),
                        pltpu.SemaphoreType.DMA((2, bsz)), pltpu.SemaphoreType.DMA((2, bsz))],
        compiler_params=pltpu.CompilerParams(dimension_semantics=("arbitrary",), vmem_limit_bytes=VMEM_LIMIT),
        name="s5_seq",
    )(u, *consts)


def _sample_post_body(yc_ref, xs_ref, z_ref, dexp_ref, snrm_ref, u_ref, hr_ref, hi_ref, wb_ref, abr_ref, abi_ref,
                      wcr_ref, wci_ref, d_ref, wglu_ref, bglu_ref, nrm_ref,
                      yssd_ref, ys5_ref, nre_ref, nim_ref):
    z = z_ref[...]
    y = yc_ref[...] + dexp_ref[...] * xs_ref[...]
    yssd_ref[...] = _rms(y * (z * jax.nn.sigmoid(z)), snrm_ref[...]).astype(yssd_ref.dtype)

    u = u_ref[...]
    nw = 16 * S5_STATE
    ar, ai = abr_ref[...], abi_ref[...]

    def store(jj, re, im):
        sl = slice(jj * nw, (jj + 1) * nw)
        h0r, h0i = hr_ref[:, sl], hi_ref[:, sl]
        nre_ref[:, sl] = ar[:, sl] * h0r - ai[:, sl] * h0i + re
        nim_ref[:, sl] = ar[:, sl] * h0i + ai[:, sl] * h0r + im

    _s5_project_in(u.astype(BF16), wb_ref, store)
    slab = lambda ref: (lambda jj: ref[:, jj * nw:(jj + 1) * nw])
    y5 = _s5_tail(slab(nre_ref), slab(nim_ref), u, wcr_ref, wci_ref, d_ref, wglu_ref, bglu_ref, nrm_ref)
    ys5_ref[...] = y5.astype(ys5_ref.dtype)


def _sample_post(yc, xs, z, dexp, snrm, u, h0r, h0i, wb, abr1, abi1, wcr, wci, d, wglu, bglu, nrm):
    n = yc.shape[0]
    args = (yc, xs, z, dexp, snrm, u, h0r, h0i, wb, abr1, abi1, wcr, wci, d, wglu, bglu, nrm)
    spec = lambda w: pl.BlockSpec((n, w), lambda: (0, 0))
    return pl.pallas_call(
        _sample_post_body,
        in_specs=[_full_spec(a) for a in args],
        out_specs=[spec(SSD_WIDTH), spec(S5_WIDTH), spec(S5_LANES), spec(S5_LANES)],
        out_shape=[jax.ShapeDtypeStruct((n, SSD_WIDTH), BF16), jax.ShapeDtypeStruct((n, S5_WIDTH), BF16),
                   jax.ShapeDtypeStruct((n, S5_LANES), F32), jax.ShapeDtypeStruct((n, S5_LANES), F32)],
        compiler_params=pltpu.CompilerParams(vmem_limit_bytes=VMEM_LIMIT),
        name="sample_post",
    )(*args)


def _mix_route_body(n_blocks, xp_ref, ysp_ref, y5p_ref, xs_ref, yss_ref, y5s_ref, *refs):
    cnt_ref, carry = refs[-2:]
    i = pl.program_id(0)

    @pl.when(i == 0)
    def _init():
        carry[...] = jnp.zeros_like(carry)

    @pl.when(i < n_blocks)
    def _prompt_rows():
        _mix_route_compute(xp_ref, ysp_ref, y5p_ref, *refs)

    @pl.when(i == n_blocks)
    def _sample_rows():
        _mix_route_compute(xs_ref, yss_ref, y5s_ref, *refs)

    cnt_ref[...] = carry[...]


def _mix_route_compute(x_ref, ys_ref, y5_ref, wa_ref, wb_ref, nf_ref, wrh_ref, wrl_ref, br_ref,
                       x1_ref, xn_ref, rt_ref, _, carry):
    rows = x_ref.shape[0]
    x1 = x_ref[...] + _dot(ys_ref[...], wa_ref[...]) + _dot(y5_ref[...].astype(BF16), wb_ref[...])
    x1_ref[0:rows, :] = x1
    xn = _rms(x1, nf_ref[...])
    for j in range(SLAB_ROWS):
        xn_ref[0:rows, j, :] = xn[:, j * LANES:(j + 1) * LANES]

    xh = xn.astype(BF16)
    xl = (xn - xh.astype(F32)).astype(BF16)
    logits = _dot(xh, wrh_ref[...]) + _dot(xl, wrh_ref[...]) + _dot(xh, wrl_ref[...]) + br_ref[...]
    tm = logits.shape[0]
    lane = lax.broadcasted_iota(jnp.int32, logits.shape, 1).astype(F32)
    neg = -jnp.inf
    big = float(LANES)

    def first_max(v):
        m = jnp.max(v, axis=-1, keepdims=True)
        return m, jnp.min(jnp.where(v == m, lane, big), axis=-1, keepdims=True)

    coarse = lane < MOE_GROUPS
    mc, gsel = first_max(jnp.where(coarse, logits, neg))
    psel = 1.0 / jnp.sum(jnp.where(coarse, jnp.exp(logits - mc), 0.0), axis=-1, keepdims=True)
    lo = MOE_GROUPS + MOE_EPG * gsel
    lf = jnp.where((lane >= lo) & (lane < lo + MOE_EPG), logits, neg)
    m1, i1 = first_max(lf)
    m2, i2 = first_max(jnp.where(lane == i1, neg, lf))
    e2 = jnp.exp(m2 - m1)
    g1 = psel / (1.0 + e2)
    g2 = psel * e2 / (1.0 + e2)
    e_a, e_b = i1 - MOE_GROUPS, i2 - MOE_GROUPS

    pick_a, pick_b = lane == e_a, lane == e_b
    picks = jnp.where(pick_a | pick_b, 1.0, 0.0)
    earlier = lax.broadcasted_iota(jnp.int32, (tm, tm), 0) > lax.broadcasted_iota(jnp.int32, (tm, tm), 1)
    prior = _dot(earlier.astype(BF16), picks.astype(BF16)) + carry[...]
    rank_a = jnp.sum(jnp.where(pick_a, prior, 0.0), axis=-1, keepdims=True)
    rank_b = jnp.sum(jnp.where(pick_b, prior, 0.0), axis=-1, keepdims=True)
    carry[...] = prior[tm - 1:tm, :] + picks[tm - 1:tm, :]

    out = jnp.zeros_like(logits)
    for k, v in enumerate((e_a, e_b, g1, g2, rank_a, rank_b)):
        out = jnp.where(lane == float(k), v, out)
    rt_ref[0:rows, :] = out


def _mix_route(prompt, sample, consts, tm):
    n_prompt, n_sample = prompt[0].shape[0], sample[0].shape[0]
    assert n_prompt % tm == 0 and n_sample <= tm
    n_blocks = n_prompt // tm
    total_rows = n_prompt + n_sample
    row = lambda w: pl.BlockSpec((tm, w), lambda i: (jnp.minimum(i, n_blocks - 1), 0))
    out_row = lambda w: pl.BlockSpec((tm, w), lambda i: (i, 0))
    return pl.pallas_call(
        functools.partial(_mix_route_body, n_blocks),
        grid=(n_blocks + 1,),
        in_specs=([row(D_MODEL), row(SSD_WIDTH), row(S5_WIDTH)] + [_full_spec(a) for a in sample]
                  + [_full_spec(a) for a in consts]),
        out_specs=[out_row(D_MODEL), pl.BlockSpec((tm, SLAB_ROWS, LANES), lambda i: (i, 0, 0)),
                   out_row(LANES), pl.BlockSpec((1, LANES), lambda i: (0, 0))],
        out_shape=[jax.ShapeDtypeStruct((total_rows, D_MODEL), F32),
                   jax.ShapeDtypeStruct((total_rows, SLAB_ROWS, LANES), F32),
                   jax.ShapeDtypeStruct((total_rows, LANES), F32), jax.ShapeDtypeStruct((1, LANES), F32)],
        scratch_shapes=[pltpu.VMEM((1, LANES), F32)],
        compiler_params=pltpu.CompilerParams(dimension_semantics=("arbitrary",), vmem_limit_bytes=VMEM_LIMIT),
        name="mix_route",
    )(*prompt, *sample, *consts)


def _from_slabs(ref):
    return jnp.concatenate([ref[:, j, :] for j in range(SLAB_ROWS)], axis=-1)


def _dispatch_body(pos_ref, tz_ref, xn_ref, xs_hbm, zbuf, xbuf, zsems, sems):
    n_tiles = tz_ref.shape[0]
    batch = DISPATCH_BATCH
    i = pl.program_id(0)
    last = pl.num_programs(0) - 1

    @pl.when(i == 0)
    def _zero_fill():
        zbuf[...] = jnp.zeros_like(zbuf)

        def zero_copy(t):
            return pltpu.make_async_copy(zbuf, xs_hbm.at[pl.ds(t * MOE_TILE, MOE_TILE)], zsems.at[t])

        def zero_start(t, carry):
            @pl.when(tz_ref[t] != 0)
            def _():
                zero_copy(t).start()
            return carry

        def zero_wait(t, carry):
            @pl.when(tz_ref[t] != 0)
            def _():
                zero_copy(t).wait()
            return carry

        lax.fori_loop(0, n_tiles, zero_start, 0)
        lax.fori_loop(0, n_tiles, zero_wait, 0)

    xbuf[i % 2] = xn_ref[...]

    def copy(step, r):
        return pltpu.make_async_copy(xbuf.at[step % 2, pl.ds(r >> 1, 1)],
                                     xs_hbm.at[pl.ds(pos_ref[step * batch + r], 1)], sems.at[step % 2, r])

    def start(r, carry):
        copy(i, r).start()
        return carry

    def wait_of(step):
        def wait(r, carry):
            copy(step, r).wait()
            return carry
        return wait

    lax.fori_loop(0, batch, start, 0, unroll=DMA_UNROLL)

    @pl.when(i > 0)
    def _wait_previous():
        lax.fori_loop(0, batch, wait_of(i - 1), 0, unroll=DMA_UNROLL)

    @pl.when(i == last)
    def _wait_own():
        lax.fori_loop(0, batch, wait_of(i), 0, unroll=DMA_UNROLL)


def _dispatch(pos_flat, tile_zero, xn, n_tiles):
    return pl.pallas_call(
        _dispatch_body,
        grid_spec=pltpu.PrefetchScalarGridSpec(
            num_scalar_prefetch=2,
            grid=(pos_flat.shape[0] // DISPATCH_BATCH,),
            in_specs=[pl.BlockSpec((DISPATCH_BATCH // 2, SLAB_ROWS, LANES), lambda i, *_: (i, 0, 0))],
            out_specs=pl.BlockSpec(memory_space=pl.ANY),
            scratch_shapes=[pltpu.VMEM((MOE_TILE, SLAB_ROWS, LANES), F32),
                            pltpu.VMEM((2, DISPATCH_BATCH // 2, SLAB_ROWS, LANES), F32),
                            pltpu.SemaphoreType.DMA((n_tiles,)),
                            pltpu.SemaphoreType.DMA((2, DISPATCH_BATCH))]),
        out_shape=jax.ShapeDtypeStruct((n_tiles * MOE_TILE, SLAB_ROWS, LANES), F32),
        compiler_params=pltpu.CompilerParams(dimension_semantics=("arbitrary",), vmem_limit_bytes=VMEM_LIMIT),
        name="moe_dispatch",
    )(pos_flat, tile_zero, xn)


def _moe_ffn_body(te_ref, nused_ref, x_ref, wg_ref, wu_ref, wd_ref, y_ref, wgb, wub, wdb):
    i = pl.program_id(0)

    @pl.when(i >= nused_ref[0])
    def _unused_tile():
        y_ref[...] = jnp.zeros_like(y_ref)

    @pl.when(i < nused_ref[0])
    def _tile():
        @pl.when((i == 0) | (te_ref[i] != te_ref[jnp.maximum(i - 1, 0)]))
        def _cast_weights():
            wgb[...] = wg_ref[0].astype(BF16)
            wub[...] = wu_ref[0].astype(BF16)
            wdb[...] = wd_ref[0].astype(BF16)

        x = _from_slabs(x_ref).astype(BF16)
        gate = _dot(x, wgb[...])
        hmid = (gate * jax.nn.sigmoid(gate)) * _dot(x, wub[...])
        y = _dot(hmid.astype(BF16), wdb[...])
        for j in range(SLAB_ROWS):
            y_ref[:, j, :] = y[:, j * LANES:(j + 1) * LANES]


def _moe_ffn(tile_expert, n_used, xsorted, w_gate, w_up, w_down, n_tiles):
    wspec = lambda s: pl.BlockSpec((1,) + s, lambda i, te, nu: (te[i], 0, 0))
    slab = lambda imap: pl.BlockSpec((MOE_TILE, SLAB_ROWS, LANES), imap)
    return pl.pallas_call(
        _moe_ffn_body,
        grid_spec=pltpu.PrefetchScalarGridSpec(
            num_scalar_prefetch=2,
            grid=(n_tiles,),
            in_specs=[slab(lambda i, te, nu: (jnp.clip(i, 0, jnp.maximum(nu[0] - 1, 0)), 0, 0)),
                      wspec((D_MODEL, MOE_D_FF)), wspec((D_MODEL, MOE_D_FF)), wspec((MOE_D_FF, D_MODEL))],
            out_specs=slab(lambda i, te, nu: (i, 0, 0)),
            scratch_shapes=[pltpu.VMEM((D_MODEL, MOE_D_FF), BF16), pltpu.VMEM((D_MODEL, MOE_D_FF), BF16),
                            pltpu.VMEM((MOE_D_FF, D_MODEL), BF16)]),
        out_shape=jax.ShapeDtypeStruct((n_tiles * MOE_TILE, SLAB_ROWS, LANES), F32),
        compiler_params=pltpu.CompilerParams(dimension_semantics=("arbitrary",), vmem_limit_bytes=VMEM_LIMIT),
        name="moe_ffn",
    )(tile_expert, n_used, xsorted, w_gate, w_up, w_down)


def _combine_body(tm, pos_ref, x1_ref, rt_ref, ys_hbm, nf_ref, out_ref, ybuf, sems):
    i = pl.program_id(0)
    rows = pl.num_programs(0) * tm

    def copy(k, r):
        return pltpu.make_async_copy(ys_hbm.at[pl.ds(pos_ref[k * rows + i * tm + r], 1)],
                                     ybuf.at[k, pl.ds(r, 1)], sems.at[k * tm + r])

    def start(r, carry):
        copy(0, r).start()
        copy(1, r).start()
        return carry

    def wait(r, carry):
        copy(0, r).wait()
        copy(1, r).wait()
        return carry

    lax.fori_loop(0, tm, start, 0, unroll=DMA_UNROLL)
    lax.fori_loop(0, tm, wait, 0, unroll=DMA_UNROLL)
    rt = rt_ref[...]
    x1 = x1_ref[...]
    x2 = jnp.concatenate(
        [x1[:, j * LANES:(j + 1) * LANES] + rt[:, 2:3] * ybuf[0, :, j, :] + rt[:, 3:4] * ybuf[1, :, j, :]
         for j in range(SLAB_ROWS)], axis=-1)
    out_ref[...] = _rms(x2, nf_ref[...])


def _combine(pos, x1, rt, ysorted, nf, tm, rows, row_block_offset):
    row = lambda w: pl.BlockSpec((tm, w), lambda i, p: (i + row_block_offset, 0))
    return pl.pallas_call(
        functools.partial(_combine_body, tm),
        grid_spec=pltpu.PrefetchScalarGridSpec(
            num_scalar_prefetch=1,
            grid=(rows // tm,),
            in_specs=[row(D_MODEL), row(LANES), pl.BlockSpec(memory_space=pl.ANY),
                      pl.BlockSpec((1, D_MODEL), lambda i, p: (0, 0))],
            out_specs=pl.BlockSpec((tm, D_MODEL), lambda i, p: (i, 0)),
            scratch_shapes=[pltpu.VMEM((2, tm, SLAB_ROWS, LANES), F32), pltpu.SemaphoreType.DMA((2 * tm,))]),
        out_shape=jax.ShapeDtypeStruct((rows, D_MODEL), F32),
        compiler_params=pltpu.CompilerParams(dimension_semantics=("arbitrary",), vmem_limit_bytes=VMEM_LIMIT),
        name="moe_combine",
    )(pos, x1, rt, ysorted, nf)


def _route_tables(counts, eid, rank, n_tiles):
    experts = jnp.arange(MOE_EXPERTS, dtype=jnp.int32)
    tiles_per = (counts + MOE_TILE - 1) // MOE_TILE
    tile_end = jnp.cumsum(tiles_per)
    pstart = (tile_end - tiles_per) * MOE_TILE
    pos = jnp.sum(jnp.where(eid[..., None] == experts, pstart, 0), axis=-1) + rank
    n_used = tile_end[-1]
    tiles = jnp.arange(n_tiles, dtype=jnp.int32)
    tile_expert = jnp.sum((tile_end[None, :] <= jnp.minimum(tiles, n_used - 1)[:, None]).astype(jnp.int32), axis=1)
    ragged = counts % MOE_TILE != 0
    tile_zero = (tiles >= n_used) | jnp.any((tiles[:, None] == tile_end[None, :] - 1) & ragged[None, :], axis=1)
    return pos, tile_expert, tile_zero.astype(jnp.int32), n_used.reshape(1).astype(jnp.int32)


def _s5_tables(a_re, a_im, log_dt, b_re, b_im, c_re, c_im):
    dt = jnp.exp(log_dt)[:, None]
    mag = jnp.exp(a_re * dt)
    ab_re = mag * jnp.cos(a_im * dt)
    ab_im = mag * jnp.sin(a_im * dt)
    den = a_re * a_re + a_im * a_im
    nr = ab_re - 1.0
    q_re = (nr * a_re + ab_im * a_im) / den
    q_im = (ab_im * a_re - nr * a_im) / den
    bb_re = q_re[..., None] * b_re - q_im[..., None] * b_im
    bb_im = q_re[..., None] * b_im + q_im[..., None] * b_re
    eye = jnp.eye(16, dtype=F32)
    nblk = S5_GROUPS // 16

    def in_map(bb):
        w = jnp.einsum("jgpc,gh->jgchp", bb.reshape(nblk, 16, S5_STATE, S5_GROUP_CH), eye)
        return w.reshape(nblk, 16 * S5_GROUP_CH, 16 * S5_STATE)

    def out_map(cc):
        w = jnp.einsum("jgcp,gh->jgphc", cc.reshape(nblk, 16, S5_GROUP_CH, S5_STATE), eye)
        return w.reshape(nblk, 16 * S5_STATE, 16 * S5_GROUP_CH)

    wb = jnp.concatenate([in_map(bb_re), in_map(bb_im)], axis=-1).astype(BF16)
    return (wb, ab_re.reshape(1, S5_LANES), ab_im.reshape(1, S5_LANES),
            out_map(c_re).astype(BF16), out_map(-c_im).astype(BF16))


def kernel(x_prompt, x_sample, state_ssd_conv, state_ssd_ssm, state_s5_re, state_s5_im, meta_tokens, norm_mix, w_in, conv_w, conv_b, dt_bias, a_log, d_ssd, ssd_norm, s5_a_re, s5_a_im, s5_log_dt, s5_b_re, s5_b_im, s5_c_re, s5_c_im, s5_d, w_glu, b_glu, s5_norm, w_out, norm_ffn, router_coarse_w, router_coarse_b, router_fine_w, router_fine_b, w_gate, w_up, w_down, norm_final):
    bp, seq, _ = x_prompt.shape
    bs = x_sample.shape[0]
    n_prompt = bp * seq
    n_tok = n_prompt + bs
    row2 = lambda v: v.reshape(1, -1)
    pad_heads = lambda v: jnp.pad(v, (0, LANES - SSD_HEADS)).reshape(1, LANES)

    w = w_in[0]
    o1, o2, o3 = SSD_WIDTH, SSD_WIDTH + SSD_CONV_DIM, SSD_WIDTH + SSD_CONV_DIM + SSD_HEADS
    wz, wx, wu = w[:, :o1].astype(BF16), w[:, o1:o2].astype(BF16), w[:, o3:].astype(BF16)
    wdt = jnp.pad(w[:, o2:o3], ((0, 0), (0, LANES - SSD_HEADS))).astype(BF16)
    g_mix = row2(norm_mix[0])
    cw, cb = conv_w[0], row2(conv_b[0])
    dtb, alog = pad_heads(dt_bias[0]), pad_heads(a_log[0])
    dexp = row2(jnp.repeat(d_ssd[0], SSD_HEAD_DIM))
    snrm = row2(ssd_norm[0])
    eexp = (jnp.arange(LANES)[:, None] == (jnp.arange(SSD_WIDTH) // SSD_HEAD_DIM)[None, :]).astype(BF16)
    wb5, ab_re, ab_im, wcr, wci = _s5_tables(s5_a_re[0], s5_a_im[0], s5_log_dt[0], s5_b_re[0], s5_b_im[0],
                                             s5_c_re[0], s5_c_im[0])
    d5, wglu, bglu, nrm5 = row2(s5_d[0]), w_glu[0].astype(BF16), row2(b_glu[0]), row2(s5_norm[0])
    wo_a, wo_b = w_out[0][:SSD_WIDTH].astype(BF16), w_out[0][SSD_WIDTH:].astype(BF16)
    w_r = jnp.concatenate([router_coarse_w[0], router_fine_w[0].transpose(1, 0, 2).reshape(D_MODEL, MOE_EXPERTS)], axis=1)
    w_r = jnp.pad(w_r, ((0, 0), (0, LANES - w_r.shape[1])))
    wrh = w_r.astype(BF16)
    wrl = (w_r - wrh.astype(F32)).astype(BF16)
    b_r = jnp.concatenate([router_coarse_b[0], router_fine_b[0].reshape(-1)])
    b_r = jnp.pad(b_r, (0, LANES - b_r.shape[0])).reshape(1, LANES)

    zp, xbcp, dtp, up = _in_proj(x_prompt.reshape(n_prompt, D_MODEL), g_mix, wz, wx, wdt, wu, TOK_TILE, BF16, F32)
    xsm = jnp.concatenate([x_sample.reshape(bs, D_MODEL), meta_tokens], axis=0)
    zs, xbcs, dts, us = _in_proj(xsm, g_mix, wz, wx, wdt, wu, xsm.shape[0], F32, F32)

    front = SSD_CHUNK - N_META
    padf = lambda a: jnp.pad(a[bs:], ((front, 0), (0, 0)))[None]
    gw = SSD_HPG * SSD_HEAD_DIM
    ssd_consts = (cw, cb, dtb, alog, dexp, snrm, eexp)
    _, ctail_m, _, ht_m = _ssd_chunked(
        padf(xbcs), padf(dts), jnp.zeros((1, SSD_CHUNK, SSD_WIDTH), F32),
        jnp.zeros((1, SUBLANES, SSD_CONV_DIM), F32), jnp.zeros((1, SSD_GROUPS, SSD_STATE, gw), F32),
        *ssd_consts, mask_rows=front)
    y_ssd_p, ctail_p, ssm_p, _ = _ssd_chunked(
        xbcp.reshape(bp, seq, SSD_CONV_DIM), dtp.reshape(bp, seq, LANES), zp.reshape(bp, seq, SSD_WIDTH),
        ctail_m, ht_m, *ssd_consts, mask_rows=0)

    abr8, abi8 = jnp.broadcast_to(ab_re, (bp, S5_LANES)), jnp.broadcast_to(ab_im, (bp, S5_LANES))
    um8 = jnp.repeat(us[bs:], bp, axis=0).astype(BF16)
    y_s5_p, s5re_p, s5im_p = _s5_seq(up.reshape(bp, seq, S5_WIDTH), um8, wb5, abr8, abi8,
                                     wcr, wci, d5, wglu, bglu, nrm5)

    cst = state_ssd_conv[0]
    xt_s, dt_s, dec_s, bc, xs_s = _ssd_step_prep(xbcs[:bs], cst[:, 0], cst[:, 1], cst[:, 2], dts[:bs],
                                                 cw, cb, dtb, alog)
    ssm_s, y_core = _ssd_step(dt_s[:, :SSD_HEADS].reshape(-1), dec_s[:, :SSD_HEADS].reshape(-1),
                              state_ssd_ssm[0], xt_s, bc)
    y_ssd_s, y_s5_s, s5re_s, s5im_s = _sample_post(
        y_core, xs_s, zs[:bs], dexp, snrm, us[:bs], state_s5_re[0].reshape(bs, S5_LANES),
        state_s5_im[0].reshape(bs, S5_LANES), wb5, ab_re, ab_im, wcr, wci, d5, wglu, bglu, nrm5)

    route_consts = (wo_a, wo_b, row2(norm_ffn[0]), wrh, wrl, b_r)
    x1, xn, rt, counts = _mix_route(
        (x_prompt.reshape(n_prompt, D_MODEL), y_ssd_p.reshape(n_prompt, SSD_WIDTH), y_s5_p.reshape(n_prompt, S5_WIDTH)),
        (x_sample.reshape(bs, D_MODEL), y_ssd_s, y_s5_s), route_consts, TOK_TILE)

    n_tiles = -(-2 * n_tok // MOE_TILE) + MOE_EXPERTS
    eid = jnp.clip(rt[:, 0:2].astype(jnp.int32), 0, MOE_EXPERTS - 1)
    pos, tile_expert, tile_zero, n_used = _route_tables(counts[0, :MOE_EXPERTS].astype(jnp.int32), eid,
                                                        rt[:, 4:6].astype(jnp.int32), n_tiles)
    xsorted = _dispatch(pos.reshape(-1), tile_zero, xn, n_tiles)
    ysorted = _moe_ffn(tile_expert, n_used, xsorted, w_gate[0], w_up[0], w_down[0], n_tiles)
    nfin = row2(norm_final)
    y_p = _combine(pos[:n_prompt].T.reshape(-1), x1, rt, ysorted, nfin, MOE_TILE, n_prompt, 0)
    y_s = _combine(pos[n_prompt:].T.reshape(-1), x1, rt, ysorted, nfin, bs, bs, n_prompt // bs)

    s5_state = lambda a, b: a.reshape(1, b, S5_GROUPS, S5_STATE)
    new_conv_s = jnp.stack([cst[:, 1], cst[:, 2], xbcs[:bs]], axis=1)[None]
    return (y_p.reshape(bp, seq, D_MODEL), y_s.reshape(bs, 1, D_MODEL),
            ctail_p[:, SUBLANES - (SSD_CONV - 1):][None], ssm_p[None], s5_state(s5re_p, bp), s5_state(s5im_p, bp),
            new_conv_s, ssm_s[None], s5_state(s5re_s, bs), s5_state(s5im_s, bs))
```

```python
import functools

import jax
import jax.numpy as jnp
from jax import lax
from jax.experimental import pallas as pl
from jax.experimental.pallas import tpu as pltpu

F32, BF16 = jnp.float32, jnp.bfloat16

D_MODEL = 1024
N_META = 16
SSD_WIDTH = 1024
SSD_HEAD_DIM = 64
SSD_HEADS = 16
SSD_GROUPS = 2
SSD_HPG = SSD_HEADS // SSD_GROUPS
SSD_STATE = 128
SSD_CONV = 4
SSD_CHUNK = 128
SSD_CONV_DIM = SSD_WIDTH + 2 * SSD_GROUPS * SSD_STATE
S5_WIDTH = 1024
S5_GROUP_CH = 16
S5_GROUPS = 64
S5_STATE = 64
S5_LANES = S5_GROUPS * S5_STATE
MOE_GROUPS = 4
MOE_EPG = 8
MOE_EXPERTS = MOE_GROUPS * MOE_EPG
MOE_D_FF = 512
EPS = 1e-6

LANES = 128
SUBLANES = 8
VMEM_LIMIT = 56 * 1024 * 1024

S5_TIME_TILE = 32
S5_SCAN_LANES = 512
MOE_TILE = 256
SLAB_ROWS = D_MODEL // LANES
INVERT_BATCH = 256
DMA_UNROLL = 8
TOK_TILE = 512


def _dot(a, b):
    return jnp.dot(a, b, preferred_element_type=F32)


def _rms(x, g):
    return x * lax.rsqrt(jnp.mean(x * x, axis=-1, keepdims=True) + EPS) * g


def _softplus(x):
    return jnp.maximum(x, 0.0) + jnp.log1p(jnp.exp(-jnp.abs(x)))


def _split3(x):
    hi = x.astype(BF16)
    r = x - hi.astype(F32)
    mid = r.astype(BF16)
    lo = (r - mid.astype(F32)).astype(BF16)
    return hi, mid, lo


def _dot3(x, w):
    hi, mid, lo = _split3(x)
    return _dot(hi, w) + _dot(mid, w) + _dot(lo, w)


def _dot3_left(w, x):
    hi, mid, lo = _split3(x)
    return _dot(w, hi) + _dot(w, mid) + _dot(w, lo)


def _full_spec(a):
    nd = a.ndim
    return pl.BlockSpec(a.shape, lambda *_: (0,) * nd)


def _in_proj_body(x_ref, g_ref, wz_ref, wx_ref, wdt_ref, wu_ref, z_ref, xbc_ref, dt_ref, u_ref):
    xb = _rms(x_ref[...], g_ref[...]).astype(BF16)
    z_ref[...] = _dot(xb, wz_ref[...]).astype(z_ref.dtype)
    xbc_ref[...] = _dot(xb, wx_ref[...]).astype(xbc_ref.dtype)
    dt_ref[...] = _dot(xb, wdt_ref[...])
    u_ref[...] = _dot(xb, wu_ref[...]).astype(u_ref.dtype)


def _in_proj(x2d, g, wz, wx, wdt, wu, tm, act_dtype, u_dtype):
    rows = x2d.shape[0]
    row = lambda w: pl.BlockSpec((tm, w), lambda i: (i, 0))
    return pl.pallas_call(
        _in_proj_body,
        grid=(rows // tm,),
        in_specs=[row(D_MODEL), _full_spec(g), _full_spec(wz), _full_spec(wx), _full_spec(wdt), _full_spec(wu)],
        out_specs=[row(SSD_WIDTH), row(SSD_CONV_DIM), row(LANES), row(S5_WIDTH)],
        out_shape=[jax.ShapeDtypeStruct((rows, SSD_WIDTH), act_dtype),
                   jax.ShapeDtypeStruct((rows, SSD_CONV_DIM), act_dtype),
                   jax.ShapeDtypeStruct((rows, LANES), F32),
                   jax.ShapeDtypeStruct((rows, S5_WIDTH), u_dtype)],
        compiler_params=pltpu.CompilerParams(dimension_semantics=("parallel",), vmem_limit_bytes=VMEM_LIMIT),
        name="in_proj",
    )(x2d, g, wz, wx, wdt, wu)


def _ssd_body(mask_rows, xbc_ref, dt_ref, z_ref, cinit_ref, hinit_ref, cw_ref, cb_ref, dtb_ref, alog_ref,
              dexp_ref, nrm_ref, eexp_ref, y_ref, ctail_ref, st_ref, hto_ref, xwin, hT):
    c = pl.program_id(1)
    L = SSD_CHUNK

    @pl.when(c == 0)
    def _init():
        xwin[0:SUBLANES, :] = cinit_ref[0]
        hT[...] = hinit_ref[0]

    xwin[SUBLANES:SUBLANES + L, :] = xbc_ref[0].astype(F32)
    acc = cb_ref[...]
    for k in range(SSD_CONV):
        off = SUBLANES - (SSD_CONV - 1) + k
        acc = acc + xwin[off:off + L, :] * cw_ref[k:k + 1, :]
    tail = xwin[L:L + SUBLANES, :]
    xwin[0:SUBLANES, :] = tail
    ctail_ref[0] = tail

    xact = acc * jax.nn.sigmoid(acc)
    dt = _softplus(dt_ref[0] + dtb_ref[...])
    if mask_rows:
        valid = lax.broadcasted_iota(jnp.int32, (L, 1), 0) >= mask_rows
        xact = jnp.where(valid, xact, 0.0)
        dt = jnp.where(valid, dt, 0.0)

    a_neg = -jnp.exp(alog_ref[...])
    dA = dt * a_neg
    row_i = lax.broadcasted_iota(jnp.int32, (L, L), 0)
    col_i = lax.broadcasted_iota(jnp.int32, (L, L), 1)
    causal = row_i >= col_i
    tril = causal.astype(BF16)
    cs = _dot3_left(tril, dA)
    csT = cs.T
    dtT = dt.T
    ecs = jnp.exp(cs)
    wdec = jnp.exp(cs[L - 1:L, :] - cs) * dt
    eexp = eexp_ref[...]
    ecs_e = _dot3(ecs, eexp)
    wdec_e = _dot3(wdec, eexp)
    lane = lax.broadcasted_iota(jnp.int32, (L, LANES), 1)
    first_half = lane < SSD_HEAD_DIM

    gw = SSD_HPG * SSD_HEAD_DIM
    y_groups = []
    for g in range(SSD_GROUPS):
        b_g = xact[:, SSD_WIDTH + g * SSD_STATE: SSD_WIDTH + (g + 1) * SSD_STATE]
        c_g = xact[:, SSD_WIDTH + (SSD_GROUPS + g) * SSD_STATE: SSD_WIDTH + (SSD_GROUPS + g + 1) * SSD_STATE]
        b_b = b_g.astype(BF16)
        c_b = c_g.astype(BF16)
        cb = lax.dot_general(c_b, b_b, (((1,), (1,)), ((), ())), preferred_element_type=F32)
        xs_g = xact[:, g * gw:(g + 1) * gw]
        h_prev = hT[g]
        y_off = _dot(c_b, h_prev.astype(BF16)) * ecs_e[:, g * gw:(g + 1) * gw]
        xdec = (xs_g * wdec_e[:, g * gw:(g + 1) * gw]).astype(BF16)
        hT[g] = h_prev * ecs_e[L - 1:L, g * gw:(g + 1) * gw] + _dot(b_g.T.astype(BF16), xdec)
        pieces = []
        for j in range(SSD_HPG // 2):
            xs_pair = xs_g[:, j * LANES:(j + 1) * LANES]
            halves = (jnp.where(first_half, xs_pair, 0.0).astype(BF16),
                      jnp.where(first_half, 0.0, xs_pair).astype(BF16))
            yd = None
            for t in range(2):
                h = g * SSD_HPG + 2 * j + t
                seg = cs[:, h:h + 1] - csT[h:h + 1, :]
                lmat = jnp.exp(jnp.where(causal, seg, -jnp.inf))
                m = (cb * lmat * dtT[h:h + 1, :]).astype(BF16)
                part = _dot(m, halves[t])
                yd = part if yd is None else yd + part
            pieces.append(yd)
        y_groups.append(jnp.concatenate(pieces, axis=-1) + y_off + dexp_ref[:, g * gw:(g + 1) * gw] * xs_g)
    y = jnp.concatenate(y_groups, axis=-1)
    z = z_ref[0].astype(F32)
    y_ref[0] = _rms(y * (z * jax.nn.sigmoid(z)), nrm_ref[...]).astype(y_ref.dtype)

    @pl.when(c == pl.num_programs(1) - 1)
    def _emit():
        hto_ref[0] = hT[...]
        for g in range(SSD_GROUPS):
            t = hT[g].T
            for k in range(SSD_HPG):
                st_ref[0, g * SSD_HPG + k] = t[k * SSD_HEAD_DIM:(k + 1) * SSD_HEAD_DIM, :]


def _ssd_chunked(xbc, dt, z, cinit, hinit, cw, cb, dtb, alog, dexp, nrm, eexp, mask_rows):
    bsz, seq, _ = xbc.shape
    nc = seq // SSD_CHUNK
    gw = SSD_HPG * SSD_HEAD_DIM
    blk = lambda w: pl.BlockSpec((1, SSD_CHUNK, w), lambda b, c: (b, c, 0))
    return pl.pallas_call(
        functools.partial(_ssd_body, mask_rows),
        grid=(bsz, nc),
        in_specs=[blk(SSD_CONV_DIM), blk(LANES), blk(SSD_WIDTH),
                  pl.BlockSpec((1, SUBLANES, SSD_CONV_DIM), lambda b, c: (0, 0, 0)),
                  pl.BlockSpec((1, SSD_GROUPS, SSD_STATE, gw), lambda b, c: (0, 0, 0, 0)),
                  _full_spec(cw), _full_spec(cb), _full_spec(dtb), _full_spec(alog),
                  _full_spec(dexp), _full_spec(nrm), _full_spec(eexp)],
        out_specs=[blk(SSD_WIDTH),
                   pl.BlockSpec((1, SUBLANES, SSD_CONV_DIM), lambda b, c: (b, 0, 0)),
                   pl.BlockSpec((1, SSD_HEADS, SSD_HEAD_DIM, SSD_STATE), lambda b, c: (b, 0, 0, 0)),
                   pl.BlockSpec((1, SSD_GROUPS, SSD_STATE, gw), lambda b, c: (b, 0, 0, 0))],
        out_shape=[jax.ShapeDtypeStruct((bsz, seq, SSD_WIDTH), BF16),
                   jax.ShapeDtypeStruct((bsz, SUBLANES, SSD_CONV_DIM), F32),
                   jax.ShapeDtypeStruct((bsz, SSD_HEADS, SSD_HEAD_DIM, SSD_STATE), F32),
                   jax.ShapeDtypeStruct((bsz, SSD_GROUPS, SSD_STATE, gw), F32)],
        scratch_shapes=[pltpu.VMEM((SUBLANES + SSD_CHUNK, SSD_CONV_DIM), F32),
                        pltpu.VMEM((SSD_GROUPS, SSD_STATE, gw), F32)],
        compiler_params=pltpu.CompilerParams(dimension_semantics=("parallel", "arbitrary"),
                                             vmem_limit_bytes=VMEM_LIMIT),
        name="ssd_chunked",
    )(xbc, dt, z, cinit, hinit, cw, cb, dtb, alog, dexp, nrm, eexp)


def _ssd_step_prep_body(xbc_ref, c0_ref, c1_ref, c2_ref, dt_ref, cw_ref, cb_ref, dtb_ref, alog_ref,
                        xt_ref, dt_out_ref, dec_ref, bc_ref, xs_ref):
    acc = cb_ref[...]
    for k, r in enumerate((c0_ref, c1_ref, c2_ref, xbc_ref)):
        acc = acc + r[...] * cw_ref[k:k + 1, :]
    xact = acc * jax.nn.sigmoid(acc)
    xs = xact[:, :SSD_WIDTH]
    dt = _softplus(dt_ref[...] + dtb_ref[...])
    dt_out_ref[...] = dt
    dec_ref[...] = jnp.exp(dt * -jnp.exp(alog_ref[...]))
    bc_ref[...] = xact[:, SSD_WIDTH:]
    xs_ref[...] = xs
    xt_ref[...] = xs.T.astype(xt_ref.dtype)


def _ssd_step_prep(xbc, c0, c1, c2, dt, cw, cb, dtb, alog):
    n = xbc.shape[0]
    args = (xbc, c0, c1, c2, dt, cw, cb, dtb, alog)
    spec = lambda r, w: pl.BlockSpec((r, w), lambda: (0, 0))
    return pl.pallas_call(
        _ssd_step_prep_body,
        in_specs=[_full_spec(a) for a in args],
        out_specs=[spec(SSD_WIDTH, n), spec(n, LANES), spec(n, LANES), spec(n, 2 * SSD_GROUPS * SSD_STATE),
                   spec(n, SSD_WIDTH)],
        out_shape=[jax.ShapeDtypeStruct((SSD_WIDTH, n), BF16), jax.ShapeDtypeStruct((n, LANES), F32),
                   jax.ShapeDtypeStruct((n, LANES), F32),
                   jax.ShapeDtypeStruct((n, 2 * SSD_GROUPS * SSD_STATE), F32),
                   jax.ShapeDtypeStruct((n, SSD_WIDTH), F32)],
        compiler_params=pltpu.CompilerParams(vmem_limit_bytes=VMEM_LIMIT),
        name="ssd_step_prep",
    )(*args)


def _ssd_step_body(dt_ref, dec_ref, st_ref, xt_ref, bc_ref, so_ref, y_ref):
    n = xt_ref.shape[1]
    gw = SSD_HPG * SSD_HEAD_DIM
    blk = pl.program_id(0)
    seq_id = lax.broadcasted_iota(jnp.int32, (n, SSD_STATE), 0)
    sub_id = lax.broadcasted_iota(jnp.int32, (SUBLANES, gw), 0)
    base = pl.multiple_of(blk * SUBLANES, SUBLANES)
    y_acc = [jnp.zeros((SUBLANES, gw), F32) for _ in range(SSD_GROUPS)]
    for i in range(SUBLANES):
        s = blk * SUBLANES + i
        for g in range(SSD_GROUPS):
            b_all = bc_ref[:, g * SSD_STATE:(g + 1) * SSD_STATE]
            rhs = jnp.where(seq_id == s, b_all, 0.0).astype(BF16)
            outer = _dot(xt_ref[g * gw:(g + 1) * gw, :], rhs)
            news = []
            for k in range(SSD_HPG):
                h = g * SSD_HPG + k
                new = (dec_ref[s * SSD_HEADS + h] * st_ref[i, h]
                       + dt_ref[s * SSD_HEADS + h] * outer[k * SSD_HEAD_DIM:(k + 1) * SSD_HEAD_DIM, :])
                so_ref[i, h] = new
                news.append(new)
            new_g = jnp.concatenate(news, axis=0).astype(BF16)
            c_lo = (SSD_GROUPS + g) * SSD_STATE
            c_blk = bc_ref[pl.ds(base, SUBLANES), c_lo:c_lo + SSD_STATE].astype(BF16)
            r = lax.dot_general(c_blk, new_g, (((1,), (1,)), ((), ())), preferred_element_type=F32)
            y_acc[g] = y_acc[g] + jnp.where(sub_id == i, r, 0.0)
    y_ref[...] = jnp.concatenate(y_acc, axis=-1)


def _ssd_step(dt_flat, dec_flat, state, xt, bc):
    n = state.shape[0]
    st_spec = pl.BlockSpec((SUBLANES, SSD_HEADS, SSD_HEAD_DIM, SSD_STATE), lambda i, *_: (i, 0, 0, 0))
    return pl.pallas_call(
        _ssd_step_body,
        grid_spec=pltpu.PrefetchScalarGridSpec(
            num_scalar_prefetch=2,
            grid=(n // SUBLANES,),
            in_specs=[st_spec, pl.BlockSpec(xt.shape, lambda i, *_: (0, 0)),
                      pl.BlockSpec(bc.shape, lambda i, *_: (0, 0))],
            out_specs=[st_spec, pl.BlockSpec((SUBLANES, SSD_WIDTH), lambda i, *_: (i, 0))]),
        out_shape=[jax.ShapeDtypeStruct(state.shape, F32), jax.ShapeDtypeStruct((n, SSD_WIDTH), F32)],
        compiler_params=pltpu.CompilerParams(dimension_semantics=("parallel",), vmem_limit_bytes=VMEM_LIMIT),
        name="ssd_step",
    )(dt_flat, dec_flat, state, xt, bc)


def _s5_project_in(u_b16, wb_ref, store):
    kw = 16 * S5_GROUP_CH
    nw = 16 * S5_STATE
    for j in range(S5_WIDTH // kw):
        r = _dot(u_b16[:, j * kw:(j + 1) * kw], wb_ref[j])
        store(j, r[:, :nw], r[:, nw:])


def _s5_tail(hre_of, him_of, u_f32, wcr_ref, wci_ref, d_ref, wglu_ref, bglu_ref, nrm_ref):
    cols = []
    for j in range(wcr_ref.shape[0]):
        cols.append(_dot(hre_of(j).astype(BF16), wcr_ref[j]) + _dot(him_of(j).astype(BF16), wci_ref[j]))
    y = jnp.concatenate(cols, axis=-1) + d_ref[...] * u_f32
    y = jax.nn.gelu(y)
    y = y * jax.nn.sigmoid(_dot(y.astype(BF16), wglu_ref[...]) + bglu_ref[...])
    return _rms(y, nrm_ref[...])


def _s5_seq_body(u_hbm, um_ref, wb_ref, abr_ref, abi_ref, wcr_ref, wci_ref, d_ref, wglu_ref, bglu_ref, nrm_ref,
                 y_hbm, sre_ref, sim_ref, ubuf, ybuf, bu, h, in_sems, out_sems):
    j = pl.program_id(0)
    last = pl.num_programs(0) - 1
    lc, bsz = ubuf.shape[1], ubuf.shape[2]
    rows = lc * bsz
    nw = 16 * S5_STATE

    def in_copy(step, b):
        return pltpu.make_async_copy(u_hbm.at[b, pl.ds(step * lc, lc), :], ubuf.at[step % 2, :, b, :],
                                     in_sems.at[step % 2, b])

    def out_copy(step, b):
        return pltpu.make_async_copy(ybuf.at[step % 2, :, b, :], y_hbm.at[b, pl.ds(step * lc, lc), :],
                                     out_sems.at[step % 2, b])

    def project_in(u_b16, nrows):
        def store(jj, re, im):
            bu[0:nrows, jj * nw:(jj + 1) * nw] = re
            bu[0:nrows, S5_LANES + jj * nw:S5_LANES + (jj + 1) * nw] = im
        _s5_project_in(u_b16, wb_ref, store)

    def scan(nsteps):
        for k in range(S5_LANES // S5_SCAN_LANES):
            sl_r = pl.ds(k * S5_SCAN_LANES, S5_SCAN_LANES)
            sl_i = pl.ds(S5_LANES + k * S5_SCAN_LANES, S5_SCAN_LANES)
            ar = abr_ref[:, sl_r]
            ai = abi_ref[:, sl_r]

            def step(l, carry):
                hr, hi = carry
                slab = pl.ds(pl.multiple_of(l * bsz, bsz), bsz)
                nr = ar * hr - ai * hi + bu[slab, sl_r]
                ni = ar * hi + ai * hr + bu[slab, sl_i]
                bu[slab, sl_r] = nr
                bu[slab, sl_i] = ni
                return nr, ni

            hr, hi = lax.fori_loop(0, nsteps, step, (h[:, sl_r], h[:, sl_i]))
            h[:, sl_r] = hr
            h[:, sl_i] = hi

    @pl.when(j == 0)
    def _first():
        for b in range(bsz):
            in_copy(0, b).start()
        h[...] = jnp.zeros_like(h)
        project_in(um_ref[...], N_META * bsz)
        scan(N_META)

    @pl.when(j < last)
    def _prefetch():
        for b in range(bsz):
            in_copy(j + 1, b).start()

    for b in range(bsz):
        in_copy(j, b).wait()
    u2 = ubuf[j % 2].reshape(rows, S5_WIDTH)
    project_in(u2.astype(BF16), rows)
    scan(lc)
    y = _s5_tail(lambda jj: bu[:, jj * nw:(jj + 1) * nw], lambda jj: bu[:, S5_LANES + jj * nw:S5_LANES + (jj + 1) * nw],
                 u2, wcr_ref, wci_ref, d_ref, wglu_ref, bglu_ref, nrm_ref)
    ybuf[j % 2] = y.reshape(lc, bsz, S5_WIDTH)
    for b in range(bsz):
        out_copy(j, b).start()

    @pl.when(j > 0)
    def _wait_previous_out():
        for b in range(bsz):
            out_copy(j - 1, b).wait()

    @pl.when(j == last)
    def _emit():
        for b in range(bsz):
            out_copy(j, b).wait()
        sre_ref[...] = h[:, 0:S5_LANES]
        sim_ref[...] = h[:, S5_LANES:]


def _s5_seq(u, um, wb, abr, abi, wcr, wci, d, wglu, bglu, nrm):
    bsz, seq, _ = u.shape
    lc = S5_TIME_TILE
    consts = (um, wb, abr, abi, wcr, wci, d, wglu, bglu, nrm)
    st = pl.BlockSpec((bsz, S5_LANES), lambda j: (0, 0))
    return pl.pallas_call(
        _s5_seq_body,
        grid=(seq // lc,),
        in_specs=[pl.BlockSpec(memory_space=pl.ANY)] + [_full_spec(a) for a in consts],
        out_specs=[pl.BlockSpec(memory_space=pl.ANY), st, st],
        out_shape=[jax.ShapeDtypeStruct((bsz, seq, S5_WIDTH), F32),
                   jax.ShapeDtypeStruct((bsz, S5_LANES), F32), jax.ShapeDtypeStruct((bsz, S5_LANES), F32)],
        scratch_shapes=[pltpu.VMEM((2, lc, bsz, S5_WIDTH), F32), pltpu.VMEM((2, lc, bsz, S5_WIDTH), F32),
                        pltpu.VMEM((lc * bsz, 2 * S5_LANES), F32), pltpu.VMEM((bsz, 2 * S5_LANES), F32),
                        pltpu.SemaphoreType.DMA((2, bsz)), pltpu.SemaphoreType.DMA((2, bsz))],
        compiler_params=pltpu.CompilerParams(dimension_semantics=("arbitrary",), vmem_limit_bytes=VMEM_LIMIT),
        name="s5_seq",
    )(u, *consts)


def _sample_post_body(yc_ref, xs_ref, z_ref, dexp_ref, snrm_ref, u_ref, hr_ref, hi_ref, wb_ref, abr_ref, abi_ref,
                      wcr_ref, wci_ref, d_ref, wglu_ref, bglu_ref, nrm_ref,
                      yssd_ref, ys5_ref, nre_ref, nim_ref):
    z = z_ref[...]
    y = yc_ref[...] + dexp_ref[...] * xs_ref[...]
    yssd_ref[...] = _rms(y * (z * jax.nn.sigmoid(z)), snrm_ref[...]).astype(yssd_ref.dtype)

    u = u_ref[...]
    nw = 16 * S5_STATE
    ar, ai = abr_ref[...], abi_ref[...]

    def store(jj, re, im):
        sl = slice(jj * nw, (jj + 1) * nw)
        h0r, h0i = hr_ref[:, sl], hi_ref[:, sl]
        nre_ref[:, sl] = ar[:, sl] * h0r - ai[:, sl] * h0i + re
        nim_ref[:, sl] = ar[:, sl] * h0i + ai[:, sl] * h0r + im

    _s5_project_in(u.astype(BF16), wb_ref, store)
    slab = lambda ref: (lambda jj: ref[:, jj * nw:(jj + 1) * nw])
    y5 = _s5_tail(slab(nre_ref), slab(nim_ref), u, wcr_ref, wci_ref, d_ref, wglu_ref, bglu_ref, nrm_ref)
    ys5_ref[...] = y5.astype(ys5_ref.dtype)


def _sample_post(yc, xs, z, dexp, snrm, u, h0r, h0i, wb, abr1, abi1, wcr, wci, d, wglu, bglu, nrm):
    n = yc.shape[0]
    args = (yc, xs, z, dexp, snrm, u, h0r, h0i, wb, abr1, abi1, wcr, wci, d, wglu, bglu, nrm)
    spec = lambda w: pl.BlockSpec((n, w), lambda: (0, 0))
    return pl.pallas_call(
        _sample_post_body,
        in_specs=[_full_spec(a) for a in args],
        out_specs=[spec(SSD_WIDTH), spec(S5_WIDTH), spec(S5_LANES), spec(S5_LANES)],
        out_shape=[jax.ShapeDtypeStruct((n, SSD_WIDTH), BF16), jax.ShapeDtypeStruct((n, S5_WIDTH), BF16),
                   jax.ShapeDtypeStruct((n, S5_LANES), F32), jax.ShapeDtypeStruct((n, S5_LANES), F32)],
        compiler_params=pltpu.CompilerParams(vmem_limit_bytes=VMEM_LIMIT),
        name="sample_post",
    )(*args)


def _mix_route_body(n_blocks, xp_ref, ysp_ref, y5p_ref, xs_ref, yss_ref, y5s_ref, *refs):
    cnt_ref, carry = refs[-2:]
    i = pl.program_id(0)

    @pl.when(i == 0)
    def _init():
        carry[...] = jnp.zeros_like(carry)

    @pl.when(i < n_blocks)
    def _prompt_rows():
        _mix_route_compute(xp_ref, ysp_ref, y5p_ref, *refs)

    @pl.when(i == n_blocks)
    def _sample_rows():
        _mix_route_compute(xs_ref, yss_ref, y5s_ref, *refs)

    cnt_ref[...] = carry[...]


def _mix_route_compute(x_ref, ys_ref, y5_ref, wa_ref, wb_ref, nf_ref, wrh_ref, wrl_ref, br_ref,
                       x1_ref, xn_ref, rt_ref, _, carry):
    rows = x_ref.shape[0]
    x1 = x_ref[...] + _dot(ys_ref[...], wa_ref[...]) + _dot(y5_ref[...].astype(BF16), wb_ref[...])
    x1_ref[0:rows, :] = x1
    xn = _rms(x1, nf_ref[...])
    for j in range(SLAB_ROWS):
        xn_ref[0:rows, j, :] = xn[:, j * LANES:(j + 1) * LANES]

    xh = xn.astype(BF16)
    xl = (xn - xh.astype(F32)).astype(BF16)
    logits = _dot(xh, wrh_ref[...]) + _dot(xl, wrh_ref[...]) + _dot(xh, wrl_ref[...]) + br_ref[...]
    tm = logits.shape[0]
    lane = lax.broadcasted_iota(jnp.int32, logits.shape, 1).astype(F32)
    neg = -jnp.inf
    big = float(LANES)

    def first_max(v):
        m = jnp.max(v, axis=-1, keepdims=True)
        return m, jnp.min(jnp.where(v == m, lane, big), axis=-1, keepdims=True)

    coarse = lane < MOE_GROUPS
    mc, gsel = first_max(jnp.where(coarse, logits, neg))
    psel = 1.0 / jnp.sum(jnp.where(coarse, jnp.exp(logits - mc), 0.0), axis=-1, keepdims=True)
    lo = MOE_GROUPS + MOE_EPG * gsel
    lf = jnp.where((lane >= lo) & (lane < lo + MOE_EPG), logits, neg)
    m1, i1 = first_max(lf)
    m2, i2 = first_max(jnp.where(lane == i1, neg, lf))
    e2 = jnp.exp(m2 - m1)
    g1 = psel / (1.0 + e2)
    g2 = psel * e2 / (1.0 + e2)
    e_a, e_b = i1 - MOE_GROUPS, i2 - MOE_GROUPS

    pick_a, pick_b = lane == e_a, lane == e_b
    picks = jnp.where(pick_a | pick_b, 1.0, 0.0)
    earlier = lax.broadcasted_iota(jnp.int32, (tm, tm), 0) > lax.broadcasted_iota(jnp.int32, (tm, tm), 1)
    prior = _dot(earlier.astype(BF16), picks.astype(BF16)) + carry[...]
    rank_a = jnp.sum(jnp.where(pick_a, prior, 0.0), axis=-1, keepdims=True)
    rank_b = jnp.sum(jnp.where(pick_b, prior, 0.0), axis=-1, keepdims=True)
    carry[...] = prior[tm - 1:tm, :] + picks[tm - 1:tm, :]

    out = jnp.zeros_like(logits)
    for k, v in enumerate((e_a, e_b, g1, g2, rank_a, rank_b)):
        out = jnp.where(lane == float(k), v, out)
    rt_ref[0:rows, :] = out


def _mix_route(prompt, sample, consts, tm):
    n_prompt, n_sample = prompt[0].shape[0], sample[0].shape[0]
    assert n_prompt % tm == 0 and n_sample <= tm
    n_blocks = n_prompt // tm
    total_rows = n_prompt + n_sample
    row = lambda w: pl.BlockSpec((tm, w), lambda i: (jnp.minimum(i, n_blocks - 1), 0))
    out_row = lambda w: pl.BlockSpec((tm, w), lambda i: (i, 0))
    return pl.pallas_call(
        functools.partial(_mix_route_body, n_blocks),
        grid=(n_blocks + 1,),
        in_specs=([row(D_MODEL), row(SSD_WIDTH), row(S5_WIDTH)] + [_full_spec(a) for a in sample]
                  + [_full_spec(a) for a in consts]),
        out_specs=[out_row(D_MODEL), pl.BlockSpec((tm, SLAB_ROWS, LANES), lambda i: (i, 0, 0)),
                   out_row(LANES), pl.BlockSpec((1, LANES), lambda i: (0, 0))],
        out_shape=[jax.ShapeDtypeStruct((total_rows, D_MODEL), F32),
                   jax.ShapeDtypeStruct((total_rows, SLAB_ROWS, LANES), F32),
                   jax.ShapeDtypeStruct((total_rows, LANES), F32), jax.ShapeDtypeStruct((1, LANES), F32)],
        scratch_shapes=[pltpu.VMEM((1, LANES), F32)],
        compiler_params=pltpu.CompilerParams(dimension_semantics=("arbitrary",), vmem_limit_bytes=VMEM_LIMIT),
        name="mix_route",
    )(*prompt, *sample, *consts)


def _invert_body(pos_ref, cnt_ref, pst_ref, ntl_ref, nused_ref, pick_ref):
    i = pl.program_id(0)
    n_tiles = pick_ref.shape[0] // MOE_TILE

    @pl.when(i == 0)
    def _mark_padding():
        def per_expert(e, carry):
            def mark(r, c):
                pick_ref[pst_ref[e] + r] = -1
                return c
            return lax.fori_loop(cnt_ref[e], ntl_ref[e] * MOE_TILE, mark, carry)

        lax.fori_loop(0, MOE_EXPERTS, per_expert, 0)

        def mark_tail(d, c):
            pick_ref[d] = -1
            return c

        lax.fori_loop(nused_ref[0] * MOE_TILE, n_tiles * MOE_TILE, mark_tail, 0)

    def place(r, carry):
        a = i * INVERT_BATCH + r
        pick_ref[pos_ref[a]] = a
        return carry

    lax.fori_loop(0, INVERT_BATCH, place, 0, unroll=DMA_UNROLL)


def _invert(pos_flat, counts, pstart, tiles_per, n_used, n_tiles):
    return pl.pallas_call(
        _invert_body,
        grid_spec=pltpu.PrefetchScalarGridSpec(
            num_scalar_prefetch=5,
            grid=(pos_flat.shape[0] // INVERT_BATCH,),
            in_specs=[],
            out_specs=pl.BlockSpec(memory_space=pltpu.SMEM)),
        out_shape=jax.ShapeDtypeStruct((n_tiles * MOE_TILE,), jnp.int32),
        compiler_params=pltpu.CompilerParams(dimension_semantics=("arbitrary",)),
        name="moe_invert",
    )(pos_flat, counts, pstart, tiles_per, n_used)


def _moe_ffn_body(te_ref, nused_ref, pick_ref, xn_hbm, wg_ref, wu_ref, wd_ref, yt_hbm,
                  xbuf, ybuf, wgb, wub, wdb, gsems, ssems):
    i = pl.program_id(0)
    n_used = nused_ref[0]
    slot = i % 2
    n_tiles = pick_ref.shape[0] // MOE_TILE
    n_pick = yt_hbm.shape[0] - MOE_TILE

    def gather(tile, r, s):
        tok = jnp.maximum(pick_ref[tile * MOE_TILE + r], 0) >> 1
        return pltpu.make_async_copy(xn_hbm.at[pl.ds(tok, 1)], xbuf.at[s, pl.ds(r, 1)], gsems.at[r])

    def gather_done(r, s):
        return pltpu.make_async_copy(xn_hbm.at[pl.ds(0, 1)], xbuf.at[s, pl.ds(r, 1)], gsems.at[r])

    def scatter(tile, live, r, s):
        a = pick_ref[tile * MOE_TILE + r]
        dst = jnp.where(live & (a >= 0), a, n_pick + r)
        return pltpu.make_async_copy(ybuf.at[s, pl.ds(r, 1)], yt_hbm.at[pl.ds(dst, 1)], ssems.at[r])

    def scatter_done(r, s):
        return pltpu.make_async_copy(ybuf.at[s, pl.ds(r, 1)], yt_hbm.at[pl.ds(n_pick + r, 1)], ssems.at[r])

    @pl.when(i == 0)
    def _prologue():
        ybuf[1] = jnp.zeros_like(ybuf[1])

        def start(r, c):
            gather(0, r, 0).start()
            return c

        def wait(r, c):
            gather_done(r, 0).wait()
            return c

        lax.fori_loop(0, MOE_TILE, start, 0, unroll=DMA_UNROLL)
        lax.fori_loop(0, MOE_TILE, wait, 0, unroll=DMA_UNROLL)

    @pl.when(i < n_used)
    def _tile():
        @pl.when((i == 0) | (te_ref[i] != te_ref[jnp.maximum(i - 1, 0)]))
        def _cast_weights():
            wgb[...] = wg_ref[0].astype(BF16)
            wub[...] = wu_ref[0].astype(BF16)
            wdb[...] = wd_ref[0].astype(BF16)

        x = jnp.concatenate([xbuf[slot, :, j, :] for j in range(SLAB_ROWS)], axis=-1).astype(BF16)
        nxt = jnp.minimum(i + 1, n_tiles - 1)
        prv = jnp.maximum(i - 1, 0)
        for r in range(MOE_TILE):
            gather(nxt, r, 1 - slot).start()
        for r in range(MOE_TILE):
            scatter(prv, i > 0, r, 1 - slot).start()
        gate = _dot(x, wgb[...])
        hmid = (gate * jax.nn.sigmoid(gate)) * _dot(x, wub[...])
        for r in range(MOE_TILE):
            scatter_done(r, 1 - slot).wait()
        y = _dot(hmid.astype(BF16), wdb[...])
        for r in range(MOE_TILE):
            gather_done(r, 1 - slot).wait()
        for j in range(SLAB_ROWS):
            ybuf[slot, :, j, :] = y[:, j * LANES:(j + 1) * LANES]

    @pl.when(i == n_used)
    def _drain():
        def start(r, c):
            scatter(i - 1, True, r, 1 - slot).start()
            return c

        def wait(r, c):
            scatter_done(r, 1 - slot).wait()
            return c

        lax.fori_loop(0, MOE_TILE, start, 0, unroll=DMA_UNROLL)
        lax.fori_loop(0, MOE_TILE, wait, 0, unroll=DMA_UNROLL)


def _moe_ffn(tile_expert, n_used, pick_of_row, xn, w_gate, w_up, w_down, n_pick):
    n_steps = tile_expert.shape[0]
    wspec = lambda s: pl.BlockSpec((1,) + s, lambda i, te, nu, pk: (te[i], 0, 0))
    buf = pltpu.VMEM((2, MOE_TILE, SLAB_ROWS, LANES), F32)
    return pl.pallas_call(
        _moe_ffn_body,
        grid_spec=pltpu.PrefetchScalarGridSpec(
            num_scalar_prefetch=3,
            grid=(n_steps,),
            in_specs=[pl.BlockSpec(memory_space=pl.ANY),
                      wspec((D_MODEL, MOE_D_FF)), wspec((D_MODEL, MOE_D_FF)), wspec((MOE_D_FF, D_MODEL))],
            out_specs=pl.BlockSpec(memory_space=pl.ANY),
            scratch_shapes=[buf, buf,
                            pltpu.VMEM((D_MODEL, MOE_D_FF), BF16), pltpu.VMEM((D_MODEL, MOE_D_FF), BF16),
                            pltpu.VMEM((MOE_D_FF, D_MODEL), BF16),
                            pltpu.SemaphoreType.DMA((MOE_TILE,)), pltpu.SemaphoreType.DMA((MOE_TILE,))]),
        out_shape=jax.ShapeDtypeStruct((n_pick + MOE_TILE, SLAB_ROWS, LANES), F32),
        compiler_params=pltpu.CompilerParams(dimension_semantics=("arbitrary",), vmem_limit_bytes=VMEM_LIMIT),
        name="moe_ffn",
    )(tile_expert, n_used, pick_of_row, xn, w_gate, w_up, w_down)


def _combine_body(x1_ref, rt_ref, yt_ref, nf_ref, out_ref):
    rt = rt_ref[...]
    x1 = x1_ref[...]
    x2 = jnp.concatenate(
        [x1[:, j * LANES:(j + 1) * LANES] + rt[:, 2:3] * yt_ref[:, 0, j, :] + rt[:, 3:4] * yt_ref[:, 1, j, :]
         for j in range(SLAB_ROWS)], axis=-1)
    out_ref[...] = _rms(x2, nf_ref[...])


def _combine(x1, rt, y_picks, nf, tm, rows, row_block_offset):
    row = lambda w: pl.BlockSpec((tm, w), lambda i: (i + row_block_offset, 0))
    return pl.pallas_call(
        _combine_body,
        grid=(rows // tm,),
        in_specs=[row(D_MODEL), row(LANES),
                  pl.BlockSpec((tm, 2, SLAB_ROWS, LANES), lambda i: (i + row_block_offset, 0, 0, 0)),
                  pl.BlockSpec((1, D_MODEL), lambda i: (0, 0))],
        out_specs=pl.BlockSpec((tm, D_MODEL), lambda i: (i, 0)),
        out_shape=jax.ShapeDtypeStruct((rows, D_MODEL), F32),
        compiler_params=pltpu.CompilerParams(dimension_semantics=("parallel",), vmem_limit_bytes=VMEM_LIMIT),
        name="moe_combine",
    )(x1, rt, y_picks, nf)


def _route_tables(counts, eid, rank, n_steps):
    experts = jnp.arange(MOE_EXPERTS, dtype=jnp.int32)
    tiles_per = (counts + MOE_TILE - 1) // MOE_TILE
    tile_end = jnp.cumsum(tiles_per)
    pstart = (tile_end - tiles_per) * MOE_TILE
    pos = jnp.sum(jnp.where(eid[..., None] == experts, pstart, 0), axis=-1) + rank
    n_used = tile_end[-1]
    steps = jnp.arange(n_steps, dtype=jnp.int32)
    tile_expert = jnp.sum((tile_end[None, :] <= jnp.minimum(steps, n_used - 1)[:, None]).astype(jnp.int32), axis=1)
    return pos, pstart, tiles_per, tile_expert, n_used.reshape(1).astype(jnp.int32)


def _s5_tables(a_re, a_im, log_dt, b_re, b_im, c_re, c_im):
    dt = jnp.exp(log_dt)[:, None]
    mag = jnp.exp(a_re * dt)
    ab_re = mag * jnp.cos(a_im * dt)
    ab_im = mag * jnp.sin(a_im * dt)
    den = a_re * a_re + a_im * a_im
    nr = ab_re - 1.0
    q_re = (nr * a_re + ab_im * a_im) / den
    q_im = (ab_im * a_re - nr * a_im) / den
    bb_re = q_re[..., None] * b_re - q_im[..., None] * b_im
    bb_im = q_re[..., None] * b_im + q_im[..., None] * b_re
    eye = jnp.eye(16, dtype=F32)
    nblk = S5_GROUPS // 16

    def in_map(bb):
        w = jnp.einsum("jgpc,gh->jgchp", bb.reshape(nblk, 16, S5_STATE, S5_GROUP_CH), eye)
        return w.reshape(nblk, 16 * S5_GROUP_CH, 16 * S5_STATE)

    def out_map(cc):
        w = jnp.einsum("jgcp,gh->jgphc", cc.reshape(nblk, 16, S5_GROUP_CH, S5_STATE), eye)
        return w.reshape(nblk, 16 * S5_STATE, 16 * S5_GROUP_CH)

    wb = jnp.concatenate([in_map(bb_re), in_map(bb_im)], axis=-1).astype(BF16)
    return (wb, ab_re.reshape(1, S5_LANES), ab_im.reshape(1, S5_LANES),
            out_map(c_re).astype(BF16), out_map(-c_im).astype(BF16))


def kernel(x_prompt, x_sample, state_ssd_conv, state_ssd_ssm, state_s5_re, state_s5_im, meta_tokens, norm_mix, w_in, conv_w, conv_b, dt_bias, a_log, d_ssd, ssd_norm, s5_a_re, s5_a_im, s5_log_dt, s5_b_re, s5_b_im, s5_c_re, s5_c_im, s5_d, w_glu, b_glu, s5_norm, w_out, norm_ffn, router_coarse_w, router_coarse_b, router_fine_w, router_fine_b, w_gate, w_up, w_down, norm_final):
    bp, seq, _ = x_prompt.shape
    bs = x_sample.shape[0]
    n_prompt = bp * seq
    n_tok = n_prompt + bs
    row2 = lambda v: v.reshape(1, -1)
    pad_heads = lambda v: jnp.pad(v, (0, LANES - SSD_HEADS)).reshape(1, LANES)

    w = w_in[0]
    o1, o2, o3 = SSD_WIDTH, SSD_WIDTH + SSD_CONV_DIM, SSD_WIDTH + SSD_CONV_DIM + SSD_HEADS
    wz, wx, wu = w[:, :o1].astype(BF16), w[:, o1:o2].astype(BF16), w[:, o3:].astype(BF16)
    wdt = jnp.pad(w[:, o2:o3], ((0, 0), (0, LANES - SSD_HEADS))).astype(BF16)
    g_mix = row2(norm_mix[0])
    cw, cb = conv_w[0], row2(conv_b[0])
    dtb, alog = pad_heads(dt_bias[0]), pad_heads(a_log[0])
    dexp = row2(jnp.repeat(d_ssd[0], SSD_HEAD_DIM))
    snrm = row2(ssd_norm[0])
    eexp = (jnp.arange(LANES)[:, None] == (jnp.arange(SSD_WIDTH) // SSD_HEAD_DIM)[None, :]).astype(BF16)
    wb5, ab_re, ab_im, wcr, wci = _s5_tables(s5_a_re[0], s5_a_im[0], s5_log_dt[0], s5_b_re[0], s5_b_im[0],
                                             s5_c_re[0], s5_c_im[0])
    d5, wglu, bglu, nrm5 = row2(s5_d[0]), w_glu[0].astype(BF16), row2(b_glu[0]), row2(s5_norm[0])
    wo_a, wo_b = w_out[0][:SSD_WIDTH].astype(BF16), w_out[0][SSD_WIDTH:].astype(BF16)
    w_r = jnp.concatenate([router_coarse_w[0], router_fine_w[0].transpose(1, 0, 2).reshape(D_MODEL, MOE_EXPERTS)], axis=1)
    w_r = jnp.pad(w_r, ((0, 0), (0, LANES - w_r.shape[1])))
    wrh = w_r.astype(BF16)
    wrl = (w_r - wrh.astype(F32)).astype(BF16)
    b_r = jnp.concatenate([router_coarse_b[0], router_fine_b[0].reshape(-1)])
    b_r = jnp.pad(b_r, (0, LANES - b_r.shape[0])).reshape(1, LANES)

    zp, xbcp, dtp, up = _in_proj(x_prompt.reshape(n_prompt, D_MODEL), g_mix, wz, wx, wdt, wu, TOK_TILE, BF16, F32)
    xsm = jnp.concatenate([x_sample.reshape(bs, D_MODEL), meta_tokens], axis=0)
    zs, xbcs, dts, us = _in_proj(xsm, g_mix, wz, wx, wdt, wu, xsm.shape[0], F32, F32)

    front = SSD_CHUNK - N_META
    padf = lambda a: jnp.pad(a[bs:], ((front, 0), (0, 0)))[None]
    gw = SSD_HPG * SSD_HEAD_DIM
    ssd_consts = (cw, cb, dtb, alog, dexp, snrm, eexp)
    _, ctail_m, _, ht_m = _ssd_chunked(
        padf(xbcs), padf(dts), jnp.zeros((1, SSD_CHUNK, SSD_WIDTH), F32),
        jnp.zeros((1, SUBLANES, SSD_CONV_DIM), F32), jnp.zeros((1, SSD_GROUPS, SSD_STATE, gw), F32),
        *ssd_consts, mask_rows=front)
    y_ssd_p, ctail_p, ssm_p, _ = _ssd_chunked(
        xbcp.reshape(bp, seq, SSD_CONV_DIM), dtp.reshape(bp, seq, LANES), zp.reshape(bp, seq, SSD_WIDTH),
        ctail_m, ht_m, *ssd_consts, mask_rows=0)

    abr8, abi8 = jnp.broadcast_to(ab_re, (bp, S5_LANES)), jnp.broadcast_to(ab_im, (bp, S5_LANES))
    um8 = jnp.repeat(us[bs:], bp, axis=0).astype(BF16)
    y_s5_p, s5re_p, s5im_p = _s5_seq(up.reshape(bp, seq, S5_WIDTH), um8, wb5, abr8, abi8,
                                     wcr, wci, d5, wglu, bglu, nrm5)

    cst = state_ssd_conv[0]
    xt_s, dt_s, dec_s, bc, xs_s = _ssd_step_prep(xbcs[:bs], cst[:, 0], cst[:, 1], cst[:, 2], dts[:bs],
                                                 cw, cb, dtb, alog)
    ssm_s, y_core = _ssd_step(dt_s[:, :SSD_HEADS].reshape(-1), dec_s[:, :SSD_HEADS].reshape(-1),
                              state_ssd_ssm[0], xt_s, bc)
    y_ssd_s, y_s5_s, s5re_s, s5im_s = _sample_post(
        y_core, xs_s, zs[:bs], dexp, snrm, us[:bs], state_s5_re[0].reshape(bs, S5_LANES),
        state_s5_im[0].reshape(bs, S5_LANES), wb5, ab_re, ab_im, wcr, wci, d5, wglu, bglu, nrm5)

    route_consts = (wo_a, wo_b, row2(norm_ffn[0]), wrh, wrl, b_r)
    x1, xn, rt, counts = _mix_route(
        (x_prompt.reshape(n_prompt, D_MODEL), y_ssd_p.reshape(n_prompt, SSD_WIDTH), y_s5_p.reshape(n_prompt, S5_WIDTH)),
        (x_sample.reshape(bs, D_MODEL), y_ssd_s, y_s5_s), route_consts, TOK_TILE)

    n_tiles = -(-2 * n_tok // MOE_TILE) + MOE_EXPERTS
    eid = jnp.clip(rt[:, 0:2].astype(jnp.int32), 0, MOE_EXPERTS - 1)
    counts_i = counts[0, :MOE_EXPERTS].astype(jnp.int32)
    pos, pstart, tiles_per, tile_expert, n_used = _route_tables(counts_i, eid, rt[:, 4:6].astype(jnp.int32),
                                                                n_tiles + 1)
    pick_of_row = _invert(pos.reshape(-1), counts_i, pstart, tiles_per, n_used, n_tiles)
    y_picks = _moe_ffn(tile_expert, n_used, pick_of_row, xn, w_gate[0], w_up[0], w_down[0], 2 * n_tok)
    y_picks = y_picks.reshape(-1, 2, SLAB_ROWS, LANES)
    nfin = row2(norm_final)
    y_p = _combine(x1, rt, y_picks, nfin, MOE_TILE, n_prompt, 0)
    y_s = _combine(x1, rt, y_picks, nfin, bs, bs, n_prompt // bs)

    s5_state = lambda a, b: a.reshape(1, b, S5_GROUPS, S5_STATE)
    new_conv_s = jnp.stack([cst[:, 1], cst[:, 2], xbcs[:bs]], axis=1)[None]
    return (y_p.reshape(bp, seq, D_MODEL), y_s.reshape(bs, 1, D_MODEL),
            ctail_p[:, SUBLANES - (SSD_CONV - 1):][None], ssm_p[None], s5_state(s5re_p, bp), s5_state(s5im_p, bp),
            new_conv_s, ssm_s[None], s5_state(s5re_s, bs), s5_state(s5im_s, bs))
```

```python
import functools

import jax
import jax.numpy as jnp
from jax import lax
from jax.experimental import pallas as pl
from jax.experimental.pallas import tpu as pltpu

F32, BF16 = jnp.float32, jnp.bfloat16

D_MODEL = 1024
N_META = 16
SSD_WIDTH = 1024
SSD_HEAD_DIM = 64
SSD_HEADS = 16
SSD_GROUPS = 2
SSD_HPG = SSD_HEADS // SSD_GROUPS
SSD_STATE = 128
SSD_CONV = 4
SSD_CHUNK = 128
SSD_CONV_DIM = SSD_WIDTH + 2 * SSD_GROUPS * SSD_STATE
S5_WIDTH = 1024
S5_GROUP_CH = 16
S5_GROUPS = 64
S5_STATE = 64
S5_LANES = S5_GROUPS * S5_STATE
MOE_GROUPS = 4
MOE_EPG = 8
MOE_EXPERTS = MOE_GROUPS * MOE_EPG
MOE_D_FF = 512
EPS = 1e-6

LANES = 128
SUBLANES = 8
VMEM_LIMIT = 56 * 1024 * 1024

S5_TIME_TILE = 32
S5_SCAN_LANES = 512
MOE_TILE = 256
SLAB_ROWS = D_MODEL // LANES
INVERT_BATCH = 256
DMA_UNROLL = 8
TOK_TILE = 512


def _dot(a, b):
    return jnp.dot(a, b, preferred_element_type=F32)


def _rms(x, g):
    return x * lax.rsqrt(jnp.mean(x * x, axis=-1, keepdims=True) + EPS) * g


def _softplus(x):
    return jnp.maximum(x, 0.0) + jnp.log1p(jnp.exp(-jnp.abs(x)))


def _split3(x):
    hi = x.astype(BF16)
    r = x - hi.astype(F32)
    mid = r.astype(BF16)
    lo = (r - mid.astype(F32)).astype(BF16)
    return hi, mid, lo


def _dot3(x, w):
    hi, mid, lo = _split3(x)
    return _dot(hi, w) + _dot(mid, w) + _dot(lo, w)


def _dot3_left(w, x):
    hi, mid, lo = _split3(x)
    return _dot(w, hi) + _dot(w, mid) + _dot(w, lo)


def _full_spec(a):
    nd = a.ndim
    return pl.BlockSpec(a.shape, lambda *_: (0,) * nd)


def _in_proj_body(x_ref, g_ref, wz_ref, wx_ref, wdt_ref, wu_ref, z_ref, xbc_ref, dt_ref, u_ref):
    xb = _rms(x_ref[...], g_ref[...]).astype(BF16)
    z_ref[...] = _dot(xb, wz_ref[...]).astype(z_ref.dtype)
    xbc_ref[...] = _dot(xb, wx_ref[...]).astype(xbc_ref.dtype)
    dt_ref[...] = _dot(xb, wdt_ref[...])
    u_ref[...] = _dot(xb, wu_ref[...]).astype(u_ref.dtype)


def _in_proj(x2d, g, wz, wx, wdt, wu, tm, act_dtype, u_dtype):
    rows = x2d.shape[0]
    row = lambda w: pl.BlockSpec((tm, w), lambda i: (i, 0))
    return pl.pallas_call(
        _in_proj_body,
        grid=(rows // tm,),
        in_specs=[row(D_MODEL), _full_spec(g), _full_spec(wz), _full_spec(wx), _full_spec(wdt), _full_spec(wu)],
        out_specs=[row(SSD_WIDTH), row(SSD_CONV_DIM), row(LANES), row(S5_WIDTH)],
        out_shape=[jax.ShapeDtypeStruct((rows, SSD_WIDTH), act_dtype),
                   jax.ShapeDtypeStruct((rows, SSD_CONV_DIM), act_dtype),
                   jax.ShapeDtypeStruct((rows, LANES), F32),
                   jax.ShapeDtypeStruct((rows, S5_WIDTH), u_dtype)],
        compiler_params=pltpu.CompilerParams(dimension_semantics=("parallel",), vmem_limit_bytes=VMEM_LIMIT),
        name="in_proj",
    )(x2d, g, wz, wx, wdt, wu)


def _ssd_body(mask_rows, xbc_ref, dt_ref, z_ref, cinit_ref, hinit_ref, cw_ref, cb_ref, dtb_ref, alog_ref,
              dexp_ref, nrm_ref, eexp_ref, y_ref, ctail_ref, st_ref, hto_ref, xwin, hT):
    c = pl.program_id(1)
    L = SSD_CHUNK

    @pl.when(c == 0)
    def _init():
        xwin[0:SUBLANES, :] = cinit_ref[0]
        hT[...] = hinit_ref[0]

    xwin[SUBLANES:SUBLANES + L, :] = xbc_ref[0].astype(F32)
    acc = cb_ref[...]
    for k in range(SSD_CONV):
        off = SUBLANES - (SSD_CONV - 1) + k
        acc = acc + xwin[off:off + L, :] * cw_ref[k:k + 1, :]
    tail = xwin[L:L + SUBLANES, :]
    xwin[0:SUBLANES, :] = tail
    ctail_ref[0] = tail

    xact = acc * jax.nn.sigmoid(acc)
    dt = _softplus(dt_ref[0] + dtb_ref[...])
    if mask_rows:
        valid = lax.broadcasted_iota(jnp.int32, (L, 1), 0) >= mask_rows
        xact = jnp.where(valid, xact, 0.0)
        dt = jnp.where(valid, dt, 0.0)

    a_neg = -jnp.exp(alog_ref[...])
    dA = dt * a_neg
    row_i = lax.broadcasted_iota(jnp.int32, (L, L), 0)
    col_i = lax.broadcasted_iota(jnp.int32, (L, L), 1)
    causal = row_i >= col_i
    tril = causal.astype(BF16)
    cs = _dot3_left(tril, dA)
    csT = cs.T
    dtT = dt.T
    ecs = jnp.exp(cs)
    wdec = jnp.exp(cs[L - 1:L, :] - cs) * dt
    eexp = eexp_ref[...]
    ecs_e = _dot3(ecs, eexp)
    wdec_e = _dot3(wdec, eexp)
    lane = lax.broadcasted_iota(jnp.int32, (L, LANES), 1)
    first_half = lane < SSD_HEAD_DIM

    gw = SSD_HPG * SSD_HEAD_DIM
    y_groups = []
    for g in range(SSD_GROUPS):
        b_g = xact[:, SSD_WIDTH + g * SSD_STATE: SSD_WIDTH + (g + 1) * SSD_STATE]
        c_g = xact[:, SSD_WIDTH + (SSD_GROUPS + g) * SSD_STATE: SSD_WIDTH + (SSD_GROUPS + g + 1) * SSD_STATE]
        b_b = b_g.astype(BF16)
        c_b = c_g.astype(BF16)
        cb = lax.dot_general(c_b, b_b, (((1,), (1,)), ((), ())), preferred_element_type=F32)
        xs_g = xact[:, g * gw:(g + 1) * gw]
        h_prev = hT[g]
        y_off = _dot(c_b, h_prev.astype(BF16)) * ecs_e[:, g * gw:(g + 1) * gw]
        xdec = (xs_g * wdec_e[:, g * gw:(g + 1) * gw]).astype(BF16)
        hT[g] = h_prev * ecs_e[L - 1:L, g * gw:(g + 1) * gw] + _dot(b_g.T.astype(BF16), xdec)
        pieces = []
        for j in range(SSD_HPG // 2):
            xs_pair = xs_g[:, j * LANES:(j + 1) * LANES]
            halves = (jnp.where(first_half, xs_pair, 0.0).astype(BF16),
                      jnp.where(first_half, 0.0, xs_pair).astype(BF16))
            yd = None
            for t in range(2):
                h = g * SSD_HPG + 2 * j + t
                seg = cs[:, h:h + 1] - csT[h:h + 1, :]
                lmat = jnp.exp(jnp.where(causal, seg, -jnp.inf))
                m = (cb * lmat * dtT[h:h + 1, :]).astype(BF16)
                part = _dot(m, halves[t])
                yd = part if yd is None else yd + part
            pieces.append(yd)
        y_groups.append(jnp.concatenate(pieces, axis=-1) + y_off + dexp_ref[:, g * gw:(g + 1) * gw] * xs_g)
    y = jnp.concatenate(y_groups, axis=-1)
    z = z_ref[0].astype(F32)
    y_ref[0] = _rms(y * (z * jax.nn.sigmoid(z)), nrm_ref[...]).astype(y_ref.dtype)

    @pl.when(c == pl.num_programs(1) - 1)
    def _emit():
        hto_ref[0] = hT[...]
        for g in range(SSD_GROUPS):
            t = hT[g].T
            for k in range(SSD_HPG):
                st_ref[0, g * SSD_HPG + k] = t[k * SSD_HEAD_DIM:(k + 1) * SSD_HEAD_DIM, :]


def _ssd_chunked(xbc, dt, z, cinit, hinit, cw, cb, dtb, alog, dexp, nrm, eexp, mask_rows):
    bsz, seq, _ = xbc.shape
    nc = seq // SSD_CHUNK
    gw = SSD_HPG * SSD_HEAD_DIM
    blk = lambda w: pl.BlockSpec((1, SSD_CHUNK, w), lambda b, c: (b, c, 0))
    return pl.pallas_call(
        functools.partial(_ssd_body, mask_rows),
        grid=(bsz, nc),
        in_specs=[blk(SSD_CONV_DIM), blk(LANES), blk(SSD_WIDTH),
                  pl.BlockSpec((1, SUBLANES, SSD_CONV_DIM), lambda b, c: (0, 0, 0)),
                  pl.BlockSpec((1, SSD_GROUPS, SSD_STATE, gw), lambda b, c: (0, 0, 0, 0)),
                  _full_spec(cw), _full_spec(cb), _full_spec(dtb), _full_spec(alog),
                  _full_spec(dexp), _full_spec(nrm), _full_spec(eexp)],
        out_specs=[blk(SSD_WIDTH),
                   pl.BlockSpec((1, SUBLANES, SSD_CONV_DIM), lambda b, c: (b, 0, 0)),
                   pl.BlockSpec((1, SSD_HEADS, SSD_HEAD_DIM, SSD_STATE), lambda b, c: (b, 0, 0, 0)),
                   pl.BlockSpec((1, SSD_GROUPS, SSD_STATE, gw), lambda b, c: (b, 0, 0, 0))],
        out_shape=[jax.ShapeDtypeStruct((bsz, seq, SSD_WIDTH), BF16),
                   jax.ShapeDtypeStruct((bsz, SUBLANES, SSD_CONV_DIM), F32),
                   jax.ShapeDtypeStruct((bsz, SSD_HEADS, SSD_HEAD_DIM, SSD_STATE), F32),
                   jax.ShapeDtypeStruct((bsz, SSD_GROUPS, SSD_STATE, gw), F32)],
        scratch_shapes=[pltpu.VMEM((SUBLANES + SSD_CHUNK, SSD_CONV_DIM), F32),
                        pltpu.VMEM((SSD_GROUPS, SSD_STATE, gw), F32)],
        compiler_params=pltpu.CompilerParams(dimension_semantics=("parallel", "arbitrary"),
                                             vmem_limit_bytes=VMEM_LIMIT),
        name="ssd_chunked",
    )(xbc, dt, z, cinit, hinit, cw, cb, dtb, alog, dexp, nrm, eexp)


def _ssd_step_prep_body(xbc_ref, c0_ref, c1_ref, c2_ref, dt_ref, cw_ref, cb_ref, dtb_ref, alog_ref,
                        xt_ref, dt_out_ref, dec_ref, bc_ref, xs_ref):
    acc = cb_ref[...]
    for k, r in enumerate((c0_ref, c1_ref, c2_ref, xbc_ref)):
        acc = acc + r[...] * cw_ref[k:k + 1, :]
    xact = acc * jax.nn.sigmoid(acc)
    xs = xact[:, :SSD_WIDTH]
    dt = _softplus(dt_ref[...] + dtb_ref[...])
    dt_out_ref[...] = dt
    dec_ref[...] = jnp.exp(dt * -jnp.exp(alog_ref[...]))
    bc_ref[...] = xact[:, SSD_WIDTH:]
    xs_ref[...] = xs
    xt_ref[...] = xs.T.astype(xt_ref.dtype)


def _ssd_step_prep(xbc, c0, c1, c2, dt, cw, cb, dtb, alog):
    n = xbc.shape[0]
    args = (xbc, c0, c1, c2, dt, cw, cb, dtb, alog)
    spec = lambda r, w: pl.BlockSpec((r, w), lambda: (0, 0))
    return pl.pallas_call(
        _ssd_step_prep_body,
        in_specs=[_full_spec(a) for a in args],
        out_specs=[spec(SSD_WIDTH, n), spec(n, LANES), spec(n, LANES), spec(n, 2 * SSD_GROUPS * SSD_STATE),
                   spec(n, SSD_WIDTH)],
        out_shape=[jax.ShapeDtypeStruct((SSD_WIDTH, n), BF16), jax.ShapeDtypeStruct((n, LANES), F32),
                   jax.ShapeDtypeStruct((n, LANES), F32),
                   jax.ShapeDtypeStruct((n, 2 * SSD_GROUPS * SSD_STATE), F32),
                   jax.ShapeDtypeStruct((n, SSD_WIDTH), F32)],
        compiler_params=pltpu.CompilerParams(vmem_limit_bytes=VMEM_LIMIT),
        name="ssd_step_prep",
    )(*args)


def _ssd_step_body(dt_ref, dec_ref, st_ref, xt_ref, bc_ref, so_ref, y_ref):
    n = xt_ref.shape[1]
    gw = SSD_HPG * SSD_HEAD_DIM
    blk = pl.program_id(0)
    seq_id = lax.broadcasted_iota(jnp.int32, (n, SSD_STATE), 0)
    sub_id = lax.broadcasted_iota(jnp.int32, (SUBLANES, gw), 0)
    base = pl.multiple_of(blk * SUBLANES, SUBLANES)
    y_acc = [jnp.zeros((SUBLANES, gw), F32) for _ in range(SSD_GROUPS)]
    for i in range(SUBLANES):
        s = blk * SUBLANES + i
        for g in range(SSD_GROUPS):
            b_all = bc_ref[:, g * SSD_STATE:(g + 1) * SSD_STATE]
            rhs = jnp.where(seq_id == s, b_all, 0.0).astype(BF16)
            outer = _dot(xt_ref[g * gw:(g + 1) * gw, :], rhs)
            news = []
            for k in range(SSD_HPG):
                h = g * SSD_HPG + k
                new = (dec_ref[s * SSD_HEADS + h] * st_ref[i, h]
                       + dt_ref[s * SSD_HEADS + h] * outer[k * SSD_HEAD_DIM:(k + 1) * SSD_HEAD_DIM, :])
                so_ref[i, h] = new
                news.append(new)
            new_g = jnp.concatenate(news, axis=0).astype(BF16)
            c_lo = (SSD_GROUPS + g) * SSD_STATE
            c_blk = bc_ref[pl.ds(base, SUBLANES), c_lo:c_lo + SSD_STATE].astype(BF16)
            r = lax.dot_general(c_blk, new_g, (((1,), (1,)), ((), ())), preferred_element_type=F32)
            y_acc[g] = y_acc[g] + jnp.where(sub_id == i, r, 0.0)
    y_ref[...] = jnp.concatenate(y_acc, axis=-1)


def _ssd_step(dt_flat, dec_flat, state, xt, bc):
    n = state.shape[0]
    st_spec = pl.BlockSpec((SUBLANES, SSD_HEADS, SSD_HEAD_DIM, SSD_STATE), lambda i, *_: (i, 0, 0, 0))
    return pl.pallas_call(
        _ssd_step_body,
        grid_spec=pltpu.PrefetchScalarGridSpec(
            num_scalar_prefetch=2,
            grid=(n // SUBLANES,),
            in_specs=[st_spec, pl.BlockSpec(xt.shape, lambda i, *_: (0, 0)),
                      pl.BlockSpec(bc.shape, lambda i, *_: (0, 0))],
            out_specs=[st_spec, pl.BlockSpec((SUBLANES, SSD_WIDTH), lambda i, *_: (i, 0))]),
        out_shape=[jax.ShapeDtypeStruct(state.shape, F32), jax.ShapeDtypeStruct((n, SSD_WIDTH), F32)],
        compiler_params=pltpu.CompilerParams(dimension_semantics=("parallel",), vmem_limit_bytes=VMEM_LIMIT),
        name="ssd_step",
    )(dt_flat, dec_flat, state, xt, bc)


def _s5_project_in(u_b16, wb_ref, store):
    kw = 16 * S5_GROUP_CH
    nw = 16 * S5_STATE
    for j in range(S5_WIDTH // kw):
        r = _dot(u_b16[:, j * kw:(j + 1) * kw], wb_ref[j])
        store(j, r[:, :nw], r[:, nw:])


def _s5_tail(hre_of, him_of, u_f32, wcr_ref, wci_ref, d_ref, wglu_ref, bglu_ref, nrm_ref):
    cols = []
    for j in range(wcr_ref.shape[0]):
        cols.append(_dot(hre_of(j).astype(BF16), wcr_ref[j]) + _dot(him_of(j).astype(BF16), wci_ref[j]))
    y = jnp.concatenate(cols, axis=-1) + d_ref[...] * u_f32
    y = jax.nn.gelu(y)
    y = y * jax.nn.sigmoid(_dot(y.astype(BF16), wglu_ref[...]) + bglu_ref[...])
    return _rms(y, nrm_ref[...])


def _s5_seq_body(u_hbm, um_ref, wb_ref, abr_ref, abi_ref, wcr_ref, wci_ref, d_ref, wglu_ref, bglu_ref, nrm_ref,
                 y_hbm, sre_ref, sim_ref, ubuf, ybuf, bu, h, in_sems, out_sems):
    j = pl.program_id(0)
    last = pl.num_programs(0) - 1
    lc, bsz = ubuf.shape[1], ubuf.shape[2]
    rows = lc * bsz
    nw = 16 * S5_STATE

    def in_copy(step, b):
        return pltpu.make_async_copy(u_hbm.at[b, pl.ds(step * lc, lc), :], ubuf.at[step % 2, :, b, :],
                                     in_sems.at[step % 2, b])

    def out_copy(step, b):
        return pltpu.make_async_copy(ybuf.at[step % 2, :, b, :], y_hbm.at[b, pl.ds(step * lc, lc), :],
                                     out_sems.at[step % 2, b])

    def project_in(u_b16, nrows):
        def store(jj, re, im):
            bu[0:nrows, jj * nw:(jj + 1) * nw] = re
            bu[0:nrows, S5_LANES + jj * nw:S5_LANES + (jj + 1) * nw] = im
        _s5_project_in(u_b16, wb_ref, store)

    def scan(nsteps):
        for k in range(S5_LANES // S5_SCAN_LANES):
            sl_r = pl.ds(k * S5_SCAN_LANES, S5_SCAN_LANES)
            sl_i = pl.ds(S5_LANES + k * S5_SCAN_LANES, S5_SCAN_LANES)
            ar = abr_ref[:, sl_r]
            ai = abi_ref[:, sl_r]

            def step(l, carry):
                hr, hi = carry
                slab = pl.ds(pl.multiple_of(l * bsz, bsz), bsz)
                nr = ar * hr - ai * hi + bu[slab, sl_r]
                ni = ar * hi + ai * hr + bu[slab, sl_i]
                bu[slab, sl_r] = nr
                bu[slab, sl_i] = ni
                return nr, ni

            hr, hi = lax.fori_loop(0, nsteps, step, (h[:, sl_r], h[:, sl_i]))
            h[:, sl_r] = hr
            h[:, sl_i] = hi

    @pl.when(j == 0)
    def _first():
        for b in range(bsz):
            in_copy(0, b).start()
        h[...] = jnp.zeros_like(h)
        project_in(um_ref[...], N_META * bsz)
        scan(N_META)

    @pl.when(j < last)
    def _prefetch():
        for b in range(bsz):
            in_copy(j + 1, b).start()

    for b in range(bsz):
        in_copy(j, b).wait()
    u2 = ubuf[j % 2].reshape(rows, S5_WIDTH)
    project_in(u2.astype(BF16), rows)
    scan(lc)
    y = _s5_tail(lambda jj: bu[:, jj * nw:(jj + 1) * nw], lambda jj: bu[:, S5_LANES + jj * nw:S5_LANES + (jj + 1) * nw],
                 u2, wcr_ref, wci_ref, d_ref, wglu_ref, bglu_ref, nrm_ref)
    ybuf[j % 2] = y.reshape(lc, bsz, S5_WIDTH)
    for b in range(bsz):
        out_copy(j, b).start()

    @pl.when(j > 0)
    def _wait_previous_out():
        for b in range(bsz):
            out_copy(j - 1, b).wait()

    @pl.when(j == last)
    def _emit():
        for b in range(bsz):
            out_copy(j, b).wait()
        sre_ref[...] = h[:, 0:S5_LANES]
        sim_ref[...] = h[:, S5_LANES:]


def _s5_seq(u, um, wb, abr, abi, wcr, wci, d, wglu, bglu, nrm):
    bsz, seq, _ = u.shape
    lc = S5_TIME_TILE
    consts = (um, wb, abr, abi, wcr, wci, d, wglu, bglu, nrm)
    st = pl.BlockSpec((bsz, S5_LANES), lambda j: (0, 0))
    return pl.pallas_call(
        _s5_seq_body,
        grid=(seq // lc,),
        in_specs=[pl.BlockSpec(memory_space=pl.ANY)] + [_full_spec(a) for a in consts],
        out_specs=[pl.BlockSpec(memory_space=pl.ANY), st, st],
        out_shape=[jax.ShapeDtypeStruct((bsz, seq, S5_WIDTH), F32),
                   jax.ShapeDtypeStruct((bsz, S5_LANES), F32), jax.ShapeDtypeStruct((bsz, S5_LANES), F32)],
        scratch_shapes=[pltpu.VMEM((2, lc, bsz, S5_WIDTH), F32), pltpu.VMEM((2, lc, bsz, S5_WIDTH), F32),
                        pltpu.VMEM((lc * bsz, 2 * S5_LANES), F32), pltpu.VMEM((bsz, 2 * S5_LANES), F32),
                        pltpu.SemaphoreType.DMA((2, bsz)), pltpu.SemaphoreType.DMA((2, bsz))],
        compiler_params=pltpu.CompilerParams(dimension_semantics=("arbitrary",), vmem_limit_bytes=VMEM_LIMIT),
        name="s5_seq",
    )(u, *consts)


def _sample_post_body(yc_ref, xs_ref, z_ref, dexp_ref, snrm_ref, u_ref, hr_ref, hi_ref, wb_ref, abr_ref, abi_ref,
                      wcr_ref, wci_ref, d_ref, wglu_ref, bglu_ref, nrm_ref,
                      yssd_ref, ys5_ref, nre_ref, nim_ref):
    z = z_ref[...]
    y = yc_ref[...] + dexp_ref[...] * xs_ref[...]
    yssd_ref[...] = _rms(y * (z * jax.nn.sigmoid(z)), snrm_ref[...]).astype(yssd_ref.dtype)

    u = u_ref[...]
    nw = 16 * S5_STATE
    ar, ai = abr_ref[...], abi_ref[...]

    def store(jj, re, im):
        sl = slice(jj * nw, (jj + 1) * nw)
        h0r, h0i = hr_ref[:, sl], hi_ref[:, sl]
        nre_ref[:, sl] = ar[:, sl] * h0r - ai[:, sl] * h0i + re
        nim_ref[:, sl] = ar[:, sl] * h0i + ai[:, sl] * h0r + im

    _s5_project_in(u.astype(BF16), wb_ref, store)
    slab = lambda ref: (lambda jj: ref[:, jj * nw:(jj + 1) * nw])
    y5 = _s5_tail(slab(nre_ref), slab(nim_ref), u, wcr_ref, wci_ref, d_ref, wglu_ref, bglu_ref, nrm_ref)
    ys5_ref[...] = y5.astype(ys5_ref.dtype)


def _sample_post(yc, xs, z, dexp, snrm, u, h0r, h0i, wb, abr1, abi1, wcr, wci, d, wglu, bglu, nrm):
    n = yc.shape[0]
    args = (yc, xs, z, dexp, snrm, u, h0r, h0i, wb, abr1, abi1, wcr, wci, d, wglu, bglu, nrm)
    spec = lambda w: pl.BlockSpec((n, w), lambda: (0, 0))
    return pl.pallas_call(
        _sample_post_body,
        in_specs=[_full_spec(a) for a in args],
        out_specs=[spec(SSD_WIDTH), spec(S5_WIDTH), spec(S5_LANES), spec(S5_LANES)],
        out_shape=[jax.ShapeDtypeStruct((n, SSD_WIDTH), BF16), jax.ShapeDtypeStruct((n, S5_WIDTH), BF16),
                   jax.ShapeDtypeStruct((n, S5_LANES), F32), jax.ShapeDtypeStruct((n, S5_LANES), F32)],
        compiler_params=pltpu.CompilerParams(vmem_limit_bytes=VMEM_LIMIT),
        name="sample_post",
    )(*args)


def _mix_route_body(n_blocks, xp_ref, ysp_ref, y5p_ref, xs_ref, yss_ref, y5s_ref, *refs):
    cnt_ref, carry = refs[-2:]
    i = pl.program_id(0)

    @pl.when(i == 0)
    def _init():
        carry[...] = jnp.zeros_like(carry)

    @pl.when(i < n_blocks)
    def _prompt_rows():
        _mix_route_compute(xp_ref, ysp_ref, y5p_ref, *refs)

    @pl.when(i == n_blocks)
    def _sample_rows():
        _mix_route_compute(xs_ref, yss_ref, y5s_ref, *refs)

    cnt_ref[...] = carry[...]


def _mix_route_compute(x_ref, ys_ref, y5_ref, wa_ref, wb_ref, nf_ref, wrh_ref, wrl_ref, br_ref,
                       x1_ref, xn_ref, rt_ref, _, carry):
    rows = x_ref.shape[0]
    x1 = x_ref[...] + _dot(ys_ref[...], wa_ref[...]) + _dot(y5_ref[...].astype(BF16), wb_ref[...])
    x1_ref[0:rows, :] = x1
    xn = _rms(x1, nf_ref[...])
    for j in range(SLAB_ROWS):
        xn_ref[0:rows, j, :] = xn[:, j * LANES:(j + 1) * LANES]

    xh = xn.astype(BF16)
    xl = (xn - xh.astype(F32)).astype(BF16)
    logits = _dot(xh, wrh_ref[...]) + _dot(xl, wrh_ref[...]) + _dot(xh, wrl_ref[...]) + br_ref[...]
    tm = logits.shape[0]
    lane = lax.broadcasted_iota(jnp.int32, logits.shape, 1).astype(F32)
    neg = -jnp.inf
    big = float(LANES)

    def first_max(v):
        m = jnp.max(v, axis=-1, keepdims=True)
        return m, jnp.min(jnp.where(v == m, lane, big), axis=-1, keepdims=True)

    coarse = lane < MOE_GROUPS
    mc, gsel = first_max(jnp.where(coarse, logits, neg))
    psel = 1.0 / jnp.sum(jnp.where(coarse, jnp.exp(logits - mc), 0.0), axis=-1, keepdims=True)
    lo = MOE_GROUPS + MOE_EPG * gsel
    lf = jnp.where((lane >= lo) & (lane < lo + MOE_EPG), logits, neg)
    m1, i1 = first_max(lf)
    m2, i2 = first_max(jnp.where(lane == i1, neg, lf))
    e2 = jnp.exp(m2 - m1)
    g1 = psel / (1.0 + e2)
    g2 = psel * e2 / (1.0 + e2)
    e_a, e_b = i1 - MOE_GROUPS, i2 - MOE_GROUPS

    pick_a, pick_b = lane == e_a, lane == e_b
    picks = jnp.where(pick_a | pick_b, 1.0, 0.0)
    earlier = lax.broadcasted_iota(jnp.int32, (tm, tm), 0) > lax.broadcasted_iota(jnp.int32, (tm, tm), 1)
    prior = _dot(earlier.astype(BF16), picks.astype(BF16)) + carry[...]
    rank_a = jnp.sum(jnp.where(pick_a, prior, 0.0), axis=-1, keepdims=True)
    rank_b = jnp.sum(jnp.where(pick_b, prior, 0.0), axis=-1, keepdims=True)
    carry[...] = prior[tm - 1:tm, :] + picks[tm - 1:tm, :]

    out = jnp.zeros_like(logits)
    for k, v in enumerate((e_a, e_b, g1, g2, rank_a, rank_b)):
        out = jnp.where(lane == float(k), v, out)
    rt_ref[0:rows, :] = out


def _mix_route(prompt, sample, consts, tm):
    n_prompt, n_sample = prompt[0].shape[0], sample[0].shape[0]
    assert n_prompt % tm == 0 and n_sample <= tm
    n_blocks = n_prompt // tm
    total_rows = n_prompt + n_sample
    row = lambda w: pl.BlockSpec((tm, w), lambda i: (jnp.minimum(i, n_blocks - 1), 0))
    out_row = lambda w: pl.BlockSpec((tm, w), lambda i: (i, 0))
    return pl.pallas_call(
        functools.partial(_mix_route_body, n_blocks),
        grid=(n_blocks + 1,),
        in_specs=([row(D_MODEL), row(SSD_WIDTH), row(S5_WIDTH)] + [_full_spec(a) for a in sample]
                  + [_full_spec(a) for a in consts]),
        out_specs=[out_row(D_MODEL), pl.BlockSpec((tm, SLAB_ROWS, LANES), lambda i: (i, 0, 0)),
                   out_row(LANES), pl.BlockSpec((1, LANES), lambda i: (0, 0))],
        out_shape=[jax.ShapeDtypeStruct((total_rows, D_MODEL), F32),
                   jax.ShapeDtypeStruct((total_rows, SLAB_ROWS, LANES), F32),
                   jax.ShapeDtypeStruct((total_rows, LANES), F32), jax.ShapeDtypeStruct((1, LANES), F32)],
        scratch_shapes=[pltpu.VMEM((1, LANES), F32)],
        compiler_params=pltpu.CompilerParams(dimension_semantics=("arbitrary",), vmem_limit_bytes=VMEM_LIMIT),
        name="mix_route",
    )(*prompt, *sample, *consts)


def _invert_body(pos_ref, cnt_ref, pst_ref, ntl_ref, nused_ref, pick_ref):
    i = pl.program_id(0)
    n_tiles = pick_ref.shape[0] // MOE_TILE

    @pl.when(i == 0)
    def _mark_padding():
        def per_expert(e, carry):
            def mark(r, c):
                pick_ref[pst_ref[e] + r] = -1
                return c
            return lax.fori_loop(cnt_ref[e], ntl_ref[e] * MOE_TILE, mark, carry)

        lax.fori_loop(0, MOE_EXPERTS, per_expert, 0)

        def mark_tail(d, c):
            pick_ref[d] = -1
            return c

        lax.fori_loop(nused_ref[0] * MOE_TILE, n_tiles * MOE_TILE, mark_tail, 0)

    def place(r, carry):
        a = i * INVERT_BATCH + r
        pick_ref[pos_ref[a]] = a
        return carry

    lax.fori_loop(0, INVERT_BATCH, place, 0, unroll=DMA_UNROLL)


def _invert(pos_flat, counts, pstart, tiles_per, n_used, n_tiles):
    return pl.pallas_call(
        _invert_body,
        grid_spec=pltpu.PrefetchScalarGridSpec(
            num_scalar_prefetch=5,
            grid=(pos_flat.shape[0] // INVERT_BATCH,),
            in_specs=[],
            out_specs=pl.BlockSpec(memory_space=pltpu.SMEM)),
        out_shape=jax.ShapeDtypeStruct((n_tiles * MOE_TILE,), jnp.int32),
        compiler_params=pltpu.CompilerParams(dimension_semantics=("arbitrary",)),
        name="moe_invert",
    )(pos_flat, counts, pstart, tiles_per, n_used)


def _moe_ffn_body(te_ref, nused_ref, pick_ref, xn_hbm, wg_ref, wu_ref, wd_ref, yt_hbm,
                  xbuf, ybuf, wgb, wub, wdb, gsems, ssems):
    i = pl.program_id(0)
    n_used = nused_ref[0]
    slot = i % 2
    n_tiles = pick_ref.shape[0] // MOE_TILE
    n_pick = yt_hbm.shape[0] - MOE_TILE

    def gather(tile, r, s):
        tok = jnp.maximum(pick_ref[tile * MOE_TILE + r], 0) >> 1
        return pltpu.make_async_copy(xn_hbm.at[pl.ds(tok, 1)], xbuf.at[s, pl.ds(r, 1)], gsems.at[r])

    def gather_done(r, s):
        return pltpu.make_async_copy(xn_hbm.at[pl.ds(0, 1)], xbuf.at[s, pl.ds(r, 1)], gsems.at[r])

    def scatter(tile, live, r, s):
        a = pick_ref[tile * MOE_TILE + r]
        dst = jnp.where(live & (a >= 0), a, n_pick + r)
        return pltpu.make_async_copy(ybuf.at[s, pl.ds(r, 1)], yt_hbm.at[pl.ds(dst, 1)], ssems.at[r])

    def scatter_done(r, s):
        return pltpu.make_async_copy(ybuf.at[s, pl.ds(r, 1)], yt_hbm.at[pl.ds(n_pick + r, 1)], ssems.at[r])

    @pl.when(i == 0)
    def _prologue():
        ybuf[1] = jnp.zeros_like(ybuf[1])

        def start(r, c):
            gather(0, r, 0).start()
            return c

        def wait(r, c):
            gather_done(r, 0).wait()
            return c

        lax.fori_loop(0, MOE_TILE, start, 0, unroll=DMA_UNROLL)
        lax.fori_loop(0, MOE_TILE, wait, 0, unroll=DMA_UNROLL)

    @pl.when(i < n_used)
    def _tile():
        @pl.when((i == 0) | (te_ref[i] != te_ref[jnp.maximum(i - 1, 0)]))
        def _cast_weights():
            wgb[...] = wg_ref[0].astype(BF16)
            wub[...] = wu_ref[0].astype(BF16)
            wdb[...] = wd_ref[0].astype(BF16)

        x = jnp.concatenate([xbuf[slot, :, j, :] for j in range(SLAB_ROWS)], axis=-1).astype(BF16)
        nxt = jnp.minimum(i + 1, n_tiles - 1)
        prv = jnp.maximum(i - 1, 0)
        for r in range(MOE_TILE):
            gather(nxt, r, 1 - slot).start(priority=1)
        for r in range(MOE_TILE):
            scatter(prv, i > 0, r, 1 - slot).start()
        gate = _dot(x, wgb[...])
        hmid = (gate * jax.nn.sigmoid(gate)) * _dot(x, wub[...])
        y = _dot(hmid.astype(BF16), wdb[...])
        for r in range(MOE_TILE):
            scatter_done(r, 1 - slot).wait()
        for r in range(MOE_TILE):
            gather_done(r, 1 - slot).wait()
        for j in range(SLAB_ROWS):
            ybuf[slot, :, j, :] = y[:, j * LANES:(j + 1) * LANES]

    @pl.when(i == n_used)
    def _drain():
        def start(r, c):
            scatter(i - 1, True, r, 1 - slot).start()
            return c

        def wait(r, c):
            scatter_done(r, 1 - slot).wait()
            return c

        lax.fori_loop(0, MOE_TILE, start, 0, unroll=DMA_UNROLL)
        lax.fori_loop(0, MOE_TILE, wait, 0, unroll=DMA_UNROLL)


def _moe_ffn(tile_expert, n_used, pick_of_row, xn, w_gate, w_up, w_down, n_pick):
    n_steps = tile_expert.shape[0]
    wspec = lambda s: pl.BlockSpec((1,) + s, lambda i, te, nu, pk: (te[i], 0, 0))
    buf = pltpu.VMEM((2, MOE_TILE, SLAB_ROWS, LANES), F32)
    return pl.pallas_call(
        _moe_ffn_body,
        grid_spec=pltpu.PrefetchScalarGridSpec(
            num_scalar_prefetch=3,
            grid=(n_steps,),
            in_specs=[pl.BlockSpec(memory_space=pl.ANY),
                      wspec((D_MODEL, MOE_D_FF)), wspec((D_MODEL, MOE_D_FF)), wspec((MOE_D_FF, D_MODEL))],
            out_specs=pl.BlockSpec(memory_space=pl.ANY),
            scratch_shapes=[buf, buf,
                            pltpu.VMEM((D_MODEL, MOE_D_FF), BF16), pltpu.VMEM((D_MODEL, MOE_D_FF), BF16),
                            pltpu.VMEM((MOE_D_FF, D_MODEL), BF16),
                            pltpu.SemaphoreType.DMA((MOE_TILE,)), pltpu.SemaphoreType.DMA((MOE_TILE,))]),
        out_shape=jax.ShapeDtypeStruct((n_pick + MOE_TILE, SLAB_ROWS, LANES), F32),
        compiler_params=pltpu.CompilerParams(dimension_semantics=("arbitrary",), vmem_limit_bytes=VMEM_LIMIT),
        name="moe_ffn",
    )(tile_expert, n_used, pick_of_row, xn, w_gate, w_up, w_down)


def _combine_body(x1_ref, rt_ref, yt_ref, nf_ref, out_ref):
    rt = rt_ref[...]
    x1 = x1_ref[...]
    x2 = jnp.concatenate(
        [x1[:, j * LANES:(j + 1) * LANES] + rt[:, 2:3] * yt_ref[:, 0, j, :] + rt[:, 3:4] * yt_ref[:, 1, j, :]
         for j in range(SLAB_ROWS)], axis=-1)
    out_ref[...] = _rms(x2, nf_ref[...])


def _combine(x1, rt, y_picks, nf, tm, rows, row_block_offset):
    row = lambda w: pl.BlockSpec((tm, w), lambda i: (i + row_block_offset, 0))
    return pl.pallas_call(
        _combine_body,
        grid=(rows // tm,),
        in_specs=[row(D_MODEL), row(LANES),
                  pl.BlockSpec((tm, 2, SLAB_ROWS, LANES), lambda i: (i + row_block_offset, 0, 0, 0)),
                  pl.BlockSpec((1, D_MODEL), lambda i: (0, 0))],
        out_specs=pl.BlockSpec((tm, D_MODEL), lambda i: (i, 0)),
        out_shape=jax.ShapeDtypeStruct((rows, D_MODEL), F32),
        compiler_params=pltpu.CompilerParams(dimension_semantics=("parallel",), vmem_limit_bytes=VMEM_LIMIT),
        name="moe_combine",
    )(x1, rt, y_picks, nf)


def _route_tables(counts, eid, rank, n_steps):
    experts = jnp.arange(MOE_EXPERTS, dtype=jnp.int32)
    tiles_per = (counts + MOE_TILE - 1) // MOE_TILE
    tile_end = jnp.cumsum(tiles_per)
    pstart = (tile_end - tiles_per) * MOE_TILE
    pos = jnp.sum(jnp.where(eid[..., None] == experts, pstart, 0), axis=-1) + rank
    n_used = tile_end[-1]
    steps = jnp.arange(n_steps, dtype=jnp.int32)
    tile_expert = jnp.sum((tile_end[None, :] <= jnp.minimum(steps, n_used - 1)[:, None]).astype(jnp.int32), axis=1)
    return pos, pstart, tiles_per, tile_expert, n_used.reshape(1).astype(jnp.int32)


def _s5_tables(a_re, a_im, log_dt, b_re, b_im, c_re, c_im):
    dt = jnp.exp(log_dt)[:, None]
    mag = jnp.exp(a_re * dt)
    ab_re = mag * jnp.cos(a_im * dt)
    ab_im = mag * jnp.sin(a_im * dt)
    den = a_re * a_re + a_im * a_im
    nr = ab_re - 1.0
    q_re = (nr * a_re + ab_im * a_im) / den
    q_im = (ab_im * a_re - nr * a_im) / den
    bb_re = q_re[..., None] * b_re - q_im[..., None] * b_im
    bb_im = q_re[..., None] * b_im + q_im[..., None] * b_re
    eye = jnp.eye(16, dtype=F32)
    nblk = S5_GROUPS // 16

    def in_map(bb):
        w = jnp.einsum("jgpc,gh->jgchp", bb.reshape(nblk, 16, S5_STATE, S5_GROUP_CH), eye)
        return w.reshape(nblk, 16 * S5_GROUP_CH, 16 * S5_STATE)

    def out_map(cc):
        w = jnp.einsum("jgcp,gh->jgphc", cc.reshape(nblk, 16, S5_GROUP_CH, S5_STATE), eye)
        return w.reshape(nblk, 16 * S5_STATE, 16 * S5_GROUP_CH)

    wb = jnp.concatenate([in_map(bb_re), in_map(bb_im)], axis=-1).astype(BF16)
    return (wb, ab_re.reshape(1, S5_LANES), ab_im.reshape(1, S5_LANES),
            out_map(c_re).astype(BF16), out_map(-c_im).astype(BF16))


def kernel(x_prompt, x_sample, state_ssd_conv, state_ssd_ssm, state_s5_re, state_s5_im, meta_tokens, norm_mix, w_in, conv_w, conv_b, dt_bias, a_log, d_ssd, ssd_norm, s5_a_re, s5_a_im, s5_log_dt, s5_b_re, s5_b_im, s5_c_re, s5_c_im, s5_d, w_glu, b_glu, s5_norm, w_out, norm_ffn, router_coarse_w, router_coarse_b, router_fine_w, router_fine_b, w_gate, w_up, w_down, norm_final):
    bp, seq, _ = x_prompt.shape
    bs = x_sample.shape[0]
    n_prompt = bp * seq
    n_tok = n_prompt + bs
    row2 = lambda v: v.reshape(1, -1)
    pad_heads = lambda v: jnp.pad(v, (0, LANES - SSD_HEADS)).reshape(1, LANES)

    w = w_in[0]
    o1, o2, o3 = SSD_WIDTH, SSD_WIDTH + SSD_CONV_DIM, SSD_WIDTH + SSD_CONV_DIM + SSD_HEADS
    wz, wx, wu = w[:, :o1].astype(BF16), w[:, o1:o2].astype(BF16), w[:, o3:].astype(BF16)
    wdt = jnp.pad(w[:, o2:o3], ((0, 0), (0, LANES - SSD_HEADS))).astype(BF16)
    g_mix = row2(norm_mix[0])
    cw, cb = conv_w[0], row2(conv_b[0])
    dtb, alog = pad_heads(dt_bias[0]), pad_heads(a_log[0])
    dexp = row2(jnp.repeat(d_ssd[0], SSD_HEAD_DIM))
    snrm = row2(ssd_norm[0])
    eexp = (jnp.arange(LANES)[:, None] == (jnp.arange(SSD_WIDTH) // SSD_HEAD_DIM)[None, :]).astype(BF16)
    wb5, ab_re, ab_im, wcr, wci = _s5_tables(s5_a_re[0], s5_a_im[0], s5_log_dt[0], s5_b_re[0], s5_b_im[0],
                                             s5_c_re[0], s5_c_im[0])
    d5, wglu, bglu, nrm5 = row2(s5_d[0]), w_glu[0].astype(BF16), row2(b_glu[0]), row2(s5_norm[0])
    wo_a, wo_b = w_out[0][:SSD_WIDTH].astype(BF16), w_out[0][SSD_WIDTH:].astype(BF16)
    w_r = jnp.concatenate([router_coarse_w[0], router_fine_w[0].transpose(1, 0, 2).reshape(D_MODEL, MOE_EXPERTS)], axis=1)
    w_r = jnp.pad(w_r, ((0, 0), (0, LANES - w_r.shape[1])))
    wrh = w_r.astype(BF16)
    wrl = (w_r - wrh.astype(F32)).astype(BF16)
    b_r = jnp.concatenate([router_coarse_b[0], router_fine_b[0].reshape(-1)])
    b_r = jnp.pad(b_r, (0, LANES - b_r.shape[0])).reshape(1, LANES)

    zp, xbcp, dtp, up = _in_proj(x_prompt.reshape(n_prompt, D_MODEL), g_mix, wz, wx, wdt, wu, TOK_TILE, BF16, F32)
    xsm = jnp.concatenate([x_sample.reshape(bs, D_MODEL), meta_tokens], axis=0)
    zs, xbcs, dts, us = _in_proj(xsm, g_mix, wz, wx, wdt, wu, xsm.shape[0], F32, F32)

    front = SSD_CHUNK - N_META
    padf = lambda a: jnp.pad(a[bs:], ((front, 0), (0, 0)))[None]
    gw = SSD_HPG * SSD_HEAD_DIM
    ssd_consts = (cw, cb, dtb, alog, dexp, snrm, eexp)
    _, ctail_m, _, ht_m = _ssd_chunked(
        padf(xbcs), padf(dts), jnp.zeros((1, SSD_CHUNK, SSD_WIDTH), F32),
        jnp.zeros((1, SUBLANES, SSD_CONV_DIM), F32), jnp.zeros((1, SSD_GROUPS, SSD_STATE, gw), F32),
        *ssd_consts, mask_rows=front)
    y_ssd_p, ctail_p, ssm_p, _ = _ssd_chunked(
        xbcp.reshape(bp, seq, SSD_CONV_DIM), dtp.reshape(bp, seq, LANES), zp.reshape(bp, seq, SSD_WIDTH),
        ctail_m, ht_m, *ssd_consts, mask_rows=0)

    abr8, abi8 = jnp.broadcast_to(ab_re, (bp, S5_LANES)), jnp.broadcast_to(ab_im, (bp, S5_LANES))
    um8 = jnp.repeat(us[bs:], bp, axis=0).astype(BF16)
    y_s5_p, s5re_p, s5im_p = _s5_seq(up.reshape(bp, seq, S5_WIDTH), um8, wb5, abr8, abi8,
                                     wcr, wci, d5, wglu, bglu, nrm5)

    cst = state_ssd_conv[0]
    xt_s, dt_s, dec_s, bc, xs_s = _ssd_step_prep(xbcs[:bs], cst[:, 0], cst[:, 1], cst[:, 2], dts[:bs],
                                                 cw, cb, dtb, alog)
    ssm_s, y_core = _ssd_step(dt_s[:, :SSD_HEADS].reshape(-1), dec_s[:, :SSD_HEADS].reshape(-1),
                              state_ssd_ssm[0], xt_s, bc)
    y_ssd_s, y_s5_s, s5re_s, s5im_s = _sample_post(
        y_core, xs_s, zs[:bs], dexp, snrm, us[:bs], state_s5_re[0].reshape(bs, S5_LANES),
        state_s5_im[0].reshape(bs, S5_LANES), wb5, ab_re, ab_im, wcr, wci, d5, wglu, bglu, nrm5)

    route_consts = (wo_a, wo_b, row2(norm_ffn[0]), wrh, wrl, b_r)
    x1, xn, rt, counts = _mix_route(
        (x_prompt.reshape(n_prompt, D_MODEL), y_ssd_p.reshape(n_prompt, SSD_WIDTH), y_s5_p.reshape(n_prompt, S5_WIDTH)),
        (x_sample.reshape(bs, D_MODEL), y_ssd_s, y_s5_s), route_consts, TOK_TILE)

    n_tiles = -(-2 * n_tok // MOE_TILE) + MOE_EXPERTS
    eid = jnp.clip(rt[:, 0:2].astype(jnp.int32), 0, MOE_EXPERTS - 1)
    counts_i = counts[0, :MOE_EXPERTS].astype(jnp.int32)
    pos, pstart, tiles_per, tile_expert, n_used = _route_tables(counts_i, eid, rt[:, 4:6].astype(jnp.int32),
                                                                n_tiles + 1)
    pick_of_row = _invert(pos.reshape(-1), counts_i, pstart, tiles_per, n_used, n_tiles)
    y_picks = _moe_ffn(tile_expert, n_used, pick_of_row, xn, w_gate[0], w_up[0], w_down[0], 2 * n_tok)
    y_picks = y_picks.reshape(-1, 2, SLAB_ROWS, LANES)
    nfin = row2(norm_final)
    y_p = _combine(x1, rt, y_picks, nfin, MOE_TILE, n_prompt, 0)
    y_s = _combine(x1, rt, y_picks, nfin, bs, bs, n_prompt // bs)

    s5_state = lambda a, b: a.reshape(1, b, S5_GROUPS, S5_STATE)
    new_conv_s = jnp.stack([cst[:, 1], cst[:, 2], xbcs[:bs]], axis=1)[None]
    return (y_p.reshape(bp, seq, D_MODEL), y_s.reshape(bs, 1, D_MODEL),
            ctail_p[:, SUBLANES - (SSD_CONV - 1):][None], ssm_p[None], s5_state(s5re_p, bp), s5_state(s5im_p, bp),
            new_conv_s, ssm_s[None], s5_state(s5re_s, bs), s5_state(s5im_s, bs))
```

```python
import functools

import jax
import jax.numpy as jnp
from jax import lax
from jax.experimental import pallas as pl
from jax.experimental.pallas import tpu as pltpu

F32, BF16 = jnp.float32, jnp.bfloat16

D_MODEL = 1024
N_META = 16
SSD_WIDTH = 1024
SSD_HEAD_DIM = 64
SSD_HEADS = 16
SSD_GROUPS = 2
SSD_HPG = SSD_HEADS // SSD_GROUPS
SSD_STATE = 128
SSD_CONV = 4
SSD_CHUNK = 128
SSD_CONV_DIM = SSD_WIDTH + 2 * SSD_GROUPS * SSD_STATE
S5_WIDTH = 1024
S5_GROUP_CH = 16
S5_GROUPS = 64
S5_STATE = 64
S5_LANES = S5_GROUPS * S5_STATE
MOE_GROUPS = 4
MOE_EPG = 8
MOE_EXPERTS = MOE_GROUPS * MOE_EPG
MOE_D_FF = 512
EPS = 1e-6

LANES = 128
SUBLANES = 8
VMEM_LIMIT = 56 * 1024 * 1024

S5_TIME_TILE = 32
S5_SCAN_LANES = 512
MOE_TILE = 256
SLAB_ROWS = D_MODEL // LANES
DISPATCH_BATCH = 256
DMA_UNROLL = 8
TOK_TILE = 512


def _dot(a, b):
    return jnp.dot(a, b, preferred_element_type=F32)


def _rms(x, g):
    return x * lax.rsqrt(jnp.mean(x * x, axis=-1, keepdims=True) + EPS) * g


def _softplus(x):
    return jnp.maximum(x, 0.0) + jnp.log1p(jnp.exp(-jnp.abs(x)))


def _split3(x):
    hi = x.astype(BF16)
    r = x - hi.astype(F32)
    mid = r.astype(BF16)
    lo = (r - mid.astype(F32)).astype(BF16)
    return hi, mid, lo


def _dot3(x, w):
    hi, mid, lo = _split3(x)
    return _dot(hi, w) + _dot(mid, w) + _dot(lo, w)


def _dot3_left(w, x):
    hi, mid, lo = _split3(x)
    return _dot(w, hi) + _dot(w, mid) + _dot(w, lo)


def _full_spec(a):
    nd = a.ndim
    return pl.BlockSpec(a.shape, lambda *_: (0,) * nd)


def _in_proj_body(x_ref, g_ref, wz_ref, wx_ref, wdt_ref, wu_ref, z_ref, xbc_ref, dt_ref, u_ref):
    xb = _rms(x_ref[...], g_ref[...]).astype(BF16)
    z_ref[...] = _dot(xb, wz_ref[...]).astype(z_ref.dtype)
    xbc_ref[...] = _dot(xb, wx_ref[...]).astype(xbc_ref.dtype)
    dt_ref[...] = _dot(xb, wdt_ref[...])
    u_ref[...] = _dot(xb, wu_ref[...]).astype(u_ref.dtype)


def _in_proj(x2d, g, wz, wx, wdt, wu, tm, act_dtype, u_dtype):
    rows = x2d.shape[0]
    row = lambda w: pl.BlockSpec((tm, w), lambda i: (i, 0))
    return pl.pallas_call(
        _in_proj_body,
        grid=(rows // tm,),
        in_specs=[row(D_MODEL), _full_spec(g), _full_spec(wz), _full_spec(wx), _full_spec(wdt), _full_spec(wu)],
        out_specs=[row(SSD_WIDTH), row(SSD_CONV_DIM), row(LANES), row(S5_WIDTH)],
        out_shape=[jax.ShapeDtypeStruct((rows, SSD_WIDTH), act_dtype),
                   jax.ShapeDtypeStruct((rows, SSD_CONV_DIM), act_dtype),
                   jax.ShapeDtypeStruct((rows, LANES), F32),
                   jax.ShapeDtypeStruct((rows, S5_WIDTH), u_dtype)],
        compiler_params=pltpu.CompilerParams(dimension_semantics=("parallel",), vmem_limit_bytes=VMEM_LIMIT),
        name="in_proj",
    )(x2d, g, wz, wx, wdt, wu)


def _ssd_body(mask_rows, xbc_ref, dt_ref, z_ref, cinit_ref, hinit_ref, cw_ref, cb_ref, dtb_ref, alog_ref,
              dexp_ref, nrm_ref, eexp_ref, y_ref, ctail_ref, st_ref, hto_ref, xwin, hT):
    c = pl.program_id(1)
    L = SSD_CHUNK

    @pl.when(c == 0)
    def _init():
        xwin[0:SUBLANES, :] = cinit_ref[0]
        hT[...] = hinit_ref[0]

    xwin[SUBLANES:SUBLANES + L, :] = xbc_ref[0].astype(F32)
    acc = cb_ref[...]
    for k in range(SSD_CONV):
        off = SUBLANES - (SSD_CONV - 1) + k
        acc = acc + xwin[off:off + L, :] * cw_ref[k:k + 1, :]
    tail = xwin[L:L + SUBLANES, :]
    xwin[0:SUBLANES, :] = tail
    ctail_ref[0] = tail

    xact = acc * jax.nn.sigmoid(acc)
    dt = _softplus(dt_ref[0] + dtb_ref[...])
    if mask_rows:
        valid = lax.broadcasted_iota(jnp.int32, (L, 1), 0) >= mask_rows
        xact = jnp.where(valid, xact, 0.0)
        dt = jnp.where(valid, dt, 0.0)

    a_neg = -jnp.exp(alog_ref[...])
    dA = dt * a_neg
    row_i = lax.broadcasted_iota(jnp.int32, (L, L), 0)
    col_i = lax.broadcasted_iota(jnp.int32, (L, L), 1)
    causal = row_i >= col_i
    tril = causal.astype(BF16)
    cs = _dot3_left(tril, dA)
    csT = cs.T
    dtT = dt.T
    ecs = jnp.exp(cs)
    wdec = jnp.exp(cs[L - 1:L, :] - cs) * dt
    eexp = eexp_ref[...]
    ecs_e = _dot3(ecs, eexp)
    wdec_e = _dot3(wdec, eexp)
    lane = lax.broadcasted_iota(jnp.int32, (L, LANES), 1)
    first_half = lane < SSD_HEAD_DIM

    gw = SSD_HPG * SSD_HEAD_DIM
    y_groups = []
    for g in range(SSD_GROUPS):
        b_g = xact[:, SSD_WIDTH + g * SSD_STATE: SSD_WIDTH + (g + 1) * SSD_STATE]
        c_g = xact[:, SSD_WIDTH + (SSD_GROUPS + g) * SSD_STATE: SSD_WIDTH + (SSD_GROUPS + g + 1) * SSD_STATE]
        b_b = b_g.astype(BF16)
        c_b = c_g.astype(BF16)
        cb = lax.dot_general(c_b, b_b, (((1,), (1,)), ((), ())), preferred_element_type=F32)
        xs_g = xact[:, g * gw:(g + 1) * gw]
        h_prev = hT[g]
        y_off = _dot(c_b, h_prev.astype(BF16)) * ecs_e[:, g * gw:(g + 1) * gw]
        xdec = (xs_g * wdec_e[:, g * gw:(g + 1) * gw]).astype(BF16)
        hT[g] = h_prev * ecs_e[L - 1:L, g * gw:(g + 1) * gw] + _dot(b_g.T.astype(BF16), xdec)
        pieces = []
        for j in range(SSD_HPG // 2):
            xs_pair = xs_g[:, j * LANES:(j + 1) * LANES]
            halves = (jnp.where(first_half, xs_pair, 0.0).astype(BF16),
                      jnp.where(first_half, 0.0, xs_pair).astype(BF16))
            yd = None
            for t in range(2):
                h = g * SSD_HPG + 2 * j + t
                seg = cs[:, h:h + 1] - csT[h:h + 1, :]
                lmat = jnp.exp(jnp.where(causal, seg, -jnp.inf))
                m = (cb * lmat * dtT[h:h + 1, :]).astype(BF16)
                part = _dot(m, halves[t])
                yd = part if yd is None else yd + part
            pieces.append(yd)
        y_groups.append(jnp.concatenate(pieces, axis=-1) + y_off + dexp_ref[:, g * gw:(g + 1) * gw] * xs_g)
    y = jnp.concatenate(y_groups, axis=-1)
    z = z_ref[0].astype(F32)
    y_ref[0] = _rms(y * (z * jax.nn.sigmoid(z)), nrm_ref[...]).astype(y_ref.dtype)

    @pl.when(c == pl.num_programs(1) - 1)
    def _emit():
        hto_ref[0] = hT[...]
        for g in range(SSD_GROUPS):
            t = hT[g].T
            for k in range(SSD_HPG):
                st_ref[0, g * SSD_HPG + k] = t[k * SSD_HEAD_DIM:(k + 1) * SSD_HEAD_DIM, :]


def _ssd_chunked(xbc, dt, z, cinit, hinit, cw, cb, dtb, alog, dexp, nrm, eexp, mask_rows):
    bsz, seq, _ = xbc.shape
    nc = seq // SSD_CHUNK
    gw = SSD_HPG * SSD_HEAD_DIM
    blk = lambda w: pl.BlockSpec((1, SSD_CHUNK, w), lambda b, c: (b, c, 0))
    return pl.pallas_call(
        functools.partial(_ssd_body, mask_rows),
        grid=(bsz, nc),
        in_specs=[blk(SSD_CONV_DIM), blk(LANES), blk(SSD_WIDTH),
                  pl.BlockSpec((1, SUBLANES, SSD_CONV_DIM), lambda b, c: (0, 0, 0)),
                  pl.BlockSpec((1, SSD_GROUPS, SSD_STATE, gw), lambda b, c: (0, 0, 0, 0)),
                  _full_spec(cw), _full_spec(cb), _full_spec(dtb), _full_spec(alog),
                  _full_spec(dexp), _full_spec(nrm), _full_spec(eexp)],
        out_specs=[blk(SSD_WIDTH),
                   pl.BlockSpec((1, SUBLANES, SSD_CONV_DIM), lambda b, c: (b, 0, 0)),
                   pl.BlockSpec((1, SSD_HEADS, SSD_HEAD_DIM, SSD_STATE), lambda b, c: (b, 0, 0, 0)),
                   pl.BlockSpec((1, SSD_GROUPS, SSD_STATE, gw), lambda b, c: (b, 0, 0, 0))],
        out_shape=[jax.ShapeDtypeStruct((bsz, seq, SSD_WIDTH), BF16),
                   jax.ShapeDtypeStruct((bsz, SUBLANES, SSD_CONV_DIM), F32),
                   jax.ShapeDtypeStruct((bsz, SSD_HEADS, SSD_HEAD_DIM, SSD_STATE), F32),
                   jax.ShapeDtypeStruct((bsz, SSD_GROUPS, SSD_STATE, gw), F32)],
        scratch_shapes=[pltpu.VMEM((SUBLANES + SSD_CHUNK, SSD_CONV_DIM), F32),
                        pltpu.VMEM((SSD_GROUPS, SSD_STATE, gw), F32)],
        compiler_params=pltpu.CompilerParams(dimension_semantics=("parallel", "arbitrary"),
                                             vmem_limit_bytes=VMEM_LIMIT),
        name="ssd_chunked",
    )(xbc, dt, z, cinit, hinit, cw, cb, dtb, alog, dexp, nrm, eexp)


def _ssd_step_prep_body(xbc_ref, c0_ref, c1_ref, c2_ref, dt_ref, cw_ref, cb_ref, dtb_ref, alog_ref,
                        xt_ref, dt_out_ref, dec_ref, bc_ref, xs_ref):
    acc = cb_ref[...]
    for k, r in enumerate((c0_ref, c1_ref, c2_ref, xbc_ref)):
        acc = acc + r[...] * cw_ref[k:k + 1, :]
    xact = acc * jax.nn.sigmoid(acc)
    xs = xact[:, :SSD_WIDTH]
    dt = _softplus(dt_ref[...] + dtb_ref[...])
    dt_out_ref[...] = dt
    dec_ref[...] = jnp.exp(dt * -jnp.exp(alog_ref[...]))
    bc_ref[...] = xact[:, SSD_WIDTH:]
    xs_ref[...] = xs
    xt_ref[...] = xs.T.astype(xt_ref.dtype)


def _ssd_step_prep(xbc, c0, c1, c2, dt, cw, cb, dtb, alog):
    n = xbc.shape[0]
    args = (xbc, c0, c1, c2, dt, cw, cb, dtb, alog)
    spec = lambda r, w: pl.BlockSpec((r, w), lambda: (0, 0))
    return pl.pallas_call(
        _ssd_step_prep_body,
        in_specs=[_full_spec(a) for a in args],
        out_specs=[spec(SSD_WIDTH, n), spec(n, LANES), spec(n, LANES), spec(n, 2 * SSD_GROUPS * SSD_STATE),
                   spec(n, SSD_WIDTH)],
        out_shape=[jax.ShapeDtypeStruct((SSD_WIDTH, n), BF16), jax.ShapeDtypeStruct((n, LANES), F32),
                   jax.ShapeDtypeStruct((n, LANES), F32),
                   jax.ShapeDtypeStruct((n, 2 * SSD_GROUPS * SSD_STATE), F32),
                   jax.ShapeDtypeStruct((n, SSD_WIDTH), F32)],
        compiler_params=pltpu.CompilerParams(vmem_limit_bytes=VMEM_LIMIT),
        name="ssd_step_prep",
    )(*args)


def _ssd_step_body(dt_ref, dec_ref, st_ref, xt_ref, bc_ref, so_ref, y_ref):
    n = xt_ref.shape[1]
    gw = SSD_HPG * SSD_HEAD_DIM
    blk = pl.program_id(0)
    seq_id = lax.broadcasted_iota(jnp.int32, (n, SSD_STATE), 0)
    sub_id = lax.broadcasted_iota(jnp.int32, (SUBLANES, gw), 0)
    base = pl.multiple_of(blk * SUBLANES, SUBLANES)
    y_acc = [jnp.zeros((SUBLANES, gw), F32) for _ in range(SSD_GROUPS)]
    for i in range(SUBLANES):
        s = blk * SUBLANES + i
        for g in range(SSD_GROUPS):
            b_all = bc_ref[:, g * SSD_STATE:(g + 1) * SSD_STATE]
            rhs = jnp.where(seq_id == s, b_all, 0.0).astype(BF16)
            outer = _dot(xt_ref[g * gw:(g + 1) * gw, :], rhs)
            news = []
            for k in range(SSD_HPG):
                h = g * SSD_HPG + k
                new = (dec_ref[s * SSD_HEADS + h] * st_ref[i, h]
                       + dt_ref[s * SSD_HEADS + h] * outer[k * SSD_HEAD_DIM:(k + 1) * SSD_HEAD_DIM, :])
                so_ref[i, h] = new
                news.append(new)
            new_g = jnp.concatenate(news, axis=0).astype(BF16)
            c_lo = (SSD_GROUPS + g) * SSD_STATE
            c_blk = bc_ref[pl.ds(base, SUBLANES), c_lo:c_lo + SSD_STATE].astype(BF16)
            r = lax.dot_general(c_blk, new_g, (((1,), (1,)), ((), ())), preferred_element_type=F32)
            y_acc[g] = y_acc[g] + jnp.where(sub_id == i, r, 0.0)
    y_ref[...] = jnp.concatenate(y_acc, axis=-1)


def _ssd_step(dt_flat, dec_flat, state, xt, bc):
    n = state.shape[0]
    st_spec = pl.BlockSpec((SUBLANES, SSD_HEADS, SSD_HEAD_DIM, SSD_STATE), lambda i, *_: (i, 0, 0, 0))
    return pl.pallas_call(
        _ssd_step_body,
        grid_spec=pltpu.PrefetchScalarGridSpec(
            num_scalar_prefetch=2,
            grid=(n // SUBLANES,),
            in_specs=[st_spec, pl.BlockSpec(xt.shape, lambda i, *_: (0, 0)),
                      pl.BlockSpec(bc.shape, lambda i, *_: (0, 0))],
            out_specs=[st_spec, pl.BlockSpec((SUBLANES, SSD_WIDTH), lambda i, *_: (i, 0))]),
        out_shape=[jax.ShapeDtypeStruct(state.shape, F32), jax.ShapeDtypeStruct((n, SSD_WIDTH), F32)],
        compiler_params=pltpu.CompilerParams(dimension_semantics=("parallel",), vmem_limit_bytes=VMEM_LIMIT),
        name="ssd_step",
    )(dt_flat, dec_flat, state, xt, bc)


def _s5_project_in(u_b16, wb_ref, store):
    kw = 16 * S5_GROUP_CH
    nw = 16 * S5_STATE
    for j in range(S5_WIDTH // kw):
        r = _dot(u_b16[:, j * kw:(j + 1) * kw], wb_ref[j])
        store(j, r[:, :nw], r[:, nw:])


def _s5_tail(hre_of, him_of, u_f32, wcr_ref, wci_ref, d_ref, wglu_ref, bglu_ref, nrm_ref):
    cols = []
    for j in range(wcr_ref.shape[0]):
        cols.append(_dot(hre_of(j).astype(BF16), wcr_ref[j]) + _dot(him_of(j).astype(BF16), wci_ref[j]))
    y = jnp.concatenate(cols, axis=-1) + d_ref[...] * u_f32
    y = jax.nn.gelu(y)
    y = y * jax.nn.sigmoid(_dot(y.astype(BF16), wglu_ref[...]) + bglu_ref[...])
    return _rms(y, nrm_ref[...])


def _s5_seq_body(u_hbm, um_ref, wb_ref, abr_ref, abi_ref, wcr_ref, wci_ref, d_ref, wglu_ref, bglu_ref, nrm_ref,
                 y_hbm, sre_ref, sim_ref, ubuf, ybuf, bu, h, in_sems, out_sems):
    j = pl.program_id(0)
    last = pl.num_programs(0) - 1
    lc, bsz = ubuf.shape[1], ubuf.shape[2]
    rows = lc * bsz
    nw = 16 * S5_STATE

    def in_copy(step, b):
        return pltpu.make_async_copy(u_hbm.at[b, pl.ds(step * lc, lc), :], ubuf.at[step % 2, :, b, :],
                                     in_sems.at[step % 2, b])

    def out_copy(step, b):
        return pltpu.make_async_copy(ybuf.at[step % 2, :, b, :], y_hbm.at[b, pl.ds(step * lc, lc), :],
                                     out_sems.at[step % 2, b])

    def project_in(u_b16, nrows):
        def store(jj, re, im):
            bu[0:nrows, jj * nw:(jj + 1) * nw] = re
            bu[0:nrows, S5_LANES + jj * nw:S5_LANES + (jj + 1) * nw] = im
        _s5_project_in(u_b16, wb_ref, store)

    def scan(nsteps):
        for k in range(S5_LANES // S5_SCAN_LANES):
            sl_r = pl.ds(k * S5_SCAN_LANES, S5_SCAN_LANES)
            sl_i = pl.ds(S5_LANES + k * S5_SCAN_LANES, S5_SCAN_LANES)
            ar = abr_ref[:, sl_r]
            ai = abi_ref[:, sl_r]

            def step(l, carry):
                hr, hi = carry
                slab = pl.ds(pl.multiple_of(l * bsz, bsz), bsz)
                nr = ar * hr - ai * hi + bu[slab, sl_r]
                ni = ar * hi + ai * hr + bu[slab, sl_i]
                bu[slab, sl_r] = nr
                bu[slab, sl_i] = ni
                return nr, ni

            hr, hi = lax.fori_loop(0, nsteps, step, (h[:, sl_r], h[:, sl_i]))
            h[:, sl_r] = hr
            h[:, sl_i] = hi

    @pl.when(j == 0)
    def _first():
        for b in range(bsz):
            in_copy(0, b).start()
        h[...] = jnp.zeros_like(h)
        project_in(um_ref[...], N_META * bsz)
        scan(N_META)

    @pl.when(j < last)
    def _prefetch():
        for b in range(bsz):
            in_copy(j + 1, b).start()

    for b in range(bsz):
        in_copy(j, b).wait()
    u2 = ubuf[j % 2].reshape(rows, S5_WIDTH)
    project_in(u2.astype(BF16), rows)
    scan(lc)
    y = _s5_tail(lambda jj: bu[:, jj * nw:(jj + 1) * nw], lambda jj: bu[:, S5_LANES + jj * nw:S5_LANES + (jj + 1) * nw],
                 u2, wcr_ref, wci_ref, d_ref, wglu_ref, bglu_ref, nrm_ref)
    ybuf[j % 2] = y.reshape(lc, bsz, S5_WIDTH)
    for b in range(bsz):
        out_copy(j, b).start()

    @pl.when(j > 0)
    def _wait_previous_out():
        for b in range(bsz):
            out_copy(j - 1, b).wait()

    @pl.when(j == last)
    def _emit():
        for b in range(bsz):
            out_copy(j, b).wait()
        sre_ref[...] = h[:, 0:S5_LANES]
        sim_ref[...] = h[:, S5_LANES:]


def _s5_seq(u, um, wb, abr, abi, wcr, wci, d, wglu, bglu, nrm):
    bsz, seq, _ = u.shape
    lc = S5_TIME_TILE
    consts = (um, wb, abr, abi, wcr, wci, d, wglu, bglu, nrm)
    st = pl.BlockSpec((bsz, S5_LANES), lambda j: (0, 0))
    return pl.pallas_call(
        _s5_seq_body,
        grid=(seq // lc,),
        in_specs=[pl.BlockSpec(memory_space=pl.ANY)] + [_full_spec(a) for a in consts],
        out_specs=[pl.BlockSpec(memory_space=pl.ANY), st, st],
        out_shape=[jax.ShapeDtypeStruct((bsz, seq, S5_WIDTH), F32),
                   jax.ShapeDtypeStruct((bsz, S5_LANES), F32), jax.ShapeDtypeStruct((bsz, S5_LANES), F32)],
        scratch_shapes=[pltpu.VMEM((2, lc, bsz, S5_WIDTH), F32), pltpu.VMEM((2, lc, bsz, S5_WIDTH), F32),
                        pltpu.VMEM((lc * bsz, 2 * S5_LANES), F32), pltpu.VMEM((bsz, 2 * S5_LANES), F32),
                        pltpu.SemaphoreType.DMA((2, bsz)), pltpu.SemaphoreType.DMA((2, bsz))],
        compiler_params=pltpu.CompilerParams(dimension_semantics=("arbitrary",), vmem_limit_bytes=VMEM_LIMIT),
        name="s5_seq",
    )(u, *consts)


def _sample_post_body(yc_ref, xs_ref, z_ref, dexp_ref, snrm_ref, u_ref, hr_ref, hi_ref, wb_ref, abr_ref, abi_ref,
                      wcr_ref, wci_ref, d_ref, wglu_ref, bglu_ref, nrm_ref,
                      yssd_ref, ys5_ref, nre_ref, nim_ref):
    z = z_ref[...]
    y = yc_ref[...] + dexp_ref[...] * xs_ref[...]
    yssd_ref[...] = _rms(y * (z * jax.nn.sigmoid(z)), snrm_ref[...]).astype(yssd_ref.dtype)

    u = u_ref[...]
    nw = 16 * S5_STATE
    ar, ai = abr_ref[...], abi_ref[...]

    def store(jj, re, im):
        sl = slice(jj * nw, (jj + 1) * nw)
        h0r, h0i = hr_ref[:, sl], hi_ref[:, sl]
        nre_ref[:, sl] = ar[:, sl] * h0r - ai[:, sl] * h0i + re
        nim_ref[:, sl] = ar[:, sl] * h0i + ai[:, sl] * h0r + im

    _s5_project_in(u.astype(BF16), wb_ref, store)
    slab = lambda ref: (lambda jj: ref[:, jj * nw:(jj + 1) * nw])
    y5 = _s5_tail(slab(nre_ref), slab(nim_ref), u, wcr_ref, wci_ref, d_ref, wglu_ref, bglu_ref, nrm_ref)
    ys5_ref[...] = y5.astype(ys5_ref.dtype)


def _sample_post(yc, xs, z, dexp, snrm, u, h0r, h0i, wb, abr1, abi1, wcr, wci, d, wglu, bglu, nrm):
    n = yc.shape[0]
    args = (yc, xs, z, dexp, snrm, u, h0r, h0i, wb, abr1, abi1, wcr, wci, d, wglu, bglu, nrm)
    spec = lambda w: pl.BlockSpec((n, w), lambda: (0, 0))
    return pl.pallas_call(
        _sample_post_body,
        in_specs=[_full_spec(a) for a in args],
        out_specs=[spec(SSD_WIDTH), spec(S5_WIDTH), spec(S5_LANES), spec(S5_LANES)],
        out_shape=[jax.ShapeDtypeStruct((n, SSD_WIDTH), BF16), jax.ShapeDtypeStruct((n, S5_WIDTH), BF16),
                   jax.ShapeDtypeStruct((n, S5_LANES), F32), jax.ShapeDtypeStruct((n, S5_LANES), F32)],
        compiler_params=pltpu.CompilerParams(vmem_limit_bytes=VMEM_LIMIT),
        name="sample_post",
    )(*args)


def _mix_route_body(n_blocks, xp_ref, ysp_ref, y5p_ref, xs_ref, yss_ref, y5s_ref, *refs):
    cnt_ref, carry = refs[-2:]
    i = pl.program_id(0)

    @pl.when(i == 0)
    def _init():
        carry[...] = jnp.zeros_like(carry)

    @pl.when(i < n_blocks)
    def _prompt_rows():
        _mix_route_compute(xp_ref, ysp_ref, y5p_ref, *refs)

    @pl.when(i == n_blocks)
    def _sample_rows():
        _mix_route_compute(xs_ref, yss_ref, y5s_ref, *refs)

    cnt_ref[...] = carry[...]


def _mix_route_compute(x_ref, ys_ref, y5_ref, wa_ref, wb_ref, nf_ref, wrh_ref, wrl_ref, br_ref,
                       x1_ref, xn_ref, rt_ref, _, carry):
    rows = x_ref.shape[0]
    x1 = x_ref[...] + _dot(ys_ref[...], wa_ref[...]) + _dot(y5_ref[...].astype(BF16), wb_ref[...])
    x1_ref[0:rows, :] = x1
    xn = _rms(x1, nf_ref[...])
    for j in range(SLAB_ROWS):
        xn_ref[0:rows, j, :] = xn[:, j * LANES:(j + 1) * LANES]

    xh = xn.astype(BF16)
    xl = (xn - xh.astype(F32)).astype(BF16)
    logits = _dot(xh, wrh_ref[...]) + _dot(xl, wrh_ref[...]) + _dot(xh, wrl_ref[...]) + br_ref[...]
    tm = logits.shape[0]
    lane = lax.broadcasted_iota(jnp.int32, logits.shape, 1).astype(F32)
    neg = -jnp.inf
    big = float(LANES)

    def first_max(v):
        m = jnp.max(v, axis=-1, keepdims=True)
        return m, jnp.min(jnp.where(v == m, lane, big), axis=-1, keepdims=True)

    coarse = lane < MOE_GROUPS
    mc, gsel = first_max(jnp.where(coarse, logits, neg))
    psel = 1.0 / jnp.sum(jnp.where(coarse, jnp.exp(logits - mc), 0.0), axis=-1, keepdims=True)
    lo = MOE_GROUPS + MOE_EPG * gsel
    lf = jnp.where((lane >= lo) & (lane < lo + MOE_EPG), logits, neg)
    m1, i1 = first_max(lf)
    m2, i2 = first_max(jnp.where(lane == i1, neg, lf))
    e2 = jnp.exp(m2 - m1)
    g1 = psel / (1.0 + e2)
    g2 = psel * e2 / (1.0 + e2)
    e_a, e_b = i1 - MOE_GROUPS, i2 - MOE_GROUPS

    pick_a, pick_b = lane == e_a, lane == e_b
    picks = jnp.where(pick_a | pick_b, 1.0, 0.0)
    earlier = lax.broadcasted_iota(jnp.int32, (tm, tm), 0) > lax.broadcasted_iota(jnp.int32, (tm, tm), 1)
    prior = _dot(earlier.astype(BF16), picks.astype(BF16)) + carry[...]
    rank_a = jnp.sum(jnp.where(pick_a, prior, 0.0), axis=-1, keepdims=True)
    rank_b = jnp.sum(jnp.where(pick_b, prior, 0.0), axis=-1, keepdims=True)
    carry[...] = prior[tm - 1:tm, :] + picks[tm - 1:tm, :]

    out = jnp.zeros_like(logits)
    for k, v in enumerate((e_a, e_b, g1, g2, rank_a, rank_b)):
        out = jnp.where(lane == float(k), v, out)
    rt_ref[0:rows, :] = out


def _mix_route(prompt, sample, consts, tm):
    n_prompt, n_sample = prompt[0].shape[0], sample[0].shape[0]
    assert n_prompt % tm == 0 and n_sample <= tm
    n_blocks = n_prompt // tm
    total_rows = n_prompt + n_sample
    row = lambda w: pl.BlockSpec((tm, w), lambda i: (jnp.minimum(i, n_blocks - 1), 0))
    out_row = lambda w: pl.BlockSpec((tm, w), lambda i: (i, 0))
    return pl.pallas_call(
        functools.partial(_mix_route_body, n_blocks),
        grid=(n_blocks + 1,),
        in_specs=([row(D_MODEL), row(SSD_WIDTH), row(S5_WIDTH)] + [_full_spec(a) for a in sample]
                  + [_full_spec(a) for a in consts]),
        out_specs=[out_row(D_MODEL), pl.BlockSpec((tm, SLAB_ROWS, LANES), lambda i: (i, 0, 0)),
                   out_row(LANES), pl.BlockSpec((1, LANES), lambda i: (0, 0))],
        out_shape=[jax.ShapeDtypeStruct((total_rows, D_MODEL), F32),
                   jax.ShapeDtypeStruct((total_rows, SLAB_ROWS, LANES), F32),
                   jax.ShapeDtypeStruct((total_rows, LANES), F32), jax.ShapeDtypeStruct((1, LANES), F32)],
        scratch_shapes=[pltpu.VMEM((1, LANES), F32)],
        compiler_params=pltpu.CompilerParams(dimension_semantics=("arbitrary",), vmem_limit_bytes=VMEM_LIMIT),
        name="mix_route",
    )(*prompt, *sample, *consts)


def _dispatch_body(pos_ref, tz_ref, cnt_ref, pst_ref, ntl_ref, nused_ref, xn_ref, xs_hbm, pick_ref,
                   zbuf, xbuf, zsems, sems):
    i = pl.program_id(0)
    last = pl.num_programs(0) - 1
    n_tiles = pick_ref.shape[0] // MOE_TILE

    @pl.when(i == 0)
    def _padding():
        zbuf[...] = jnp.zeros_like(zbuf)

        def zero_copy(t):
            return pltpu.make_async_copy(zbuf, xs_hbm.at[pl.ds(t * MOE_TILE, MOE_TILE)], zsems.at[t])

        def zero_start(t, carry):
            @pl.when(tz_ref[t] != 0)
            def _():
                zero_copy(t).start()
            return carry

        def zero_wait(t, carry):
            @pl.when(tz_ref[t] != 0)
            def _():
                zero_copy(t).wait()
            return carry

        lax.fori_loop(0, n_tiles, zero_start, 0)
        lax.fori_loop(0, n_tiles, zero_wait, 0)

        def per_expert(e, carry):
            def mark(r, c):
                pick_ref[pst_ref[e] + r] = -1
                return c
            return lax.fori_loop(cnt_ref[e], ntl_ref[e] * MOE_TILE, mark, carry)

        lax.fori_loop(0, MOE_EXPERTS, per_expert, 0)

        def mark_tail(d, c):
            pick_ref[d] = -1
            return c

        lax.fori_loop(nused_ref[0] * MOE_TILE, n_tiles * MOE_TILE, mark_tail, 0)

    batch = DISPATCH_BATCH
    xbuf[i % 2] = xn_ref[...]

    def copy(step, r):
        return pltpu.make_async_copy(xbuf.at[step % 2, pl.ds(r >> 1, 1)],
                                     xs_hbm.at[pl.ds(pos_ref[step * batch + r], 1)], sems.at[step % 2, r])

    def start(r, carry):
        pick_ref[pos_ref[i * batch + r]] = i * batch + r
        copy(i, r).start()
        return carry

    def wait_of(step):
        def wait(r, carry):
            copy(step, r).wait()
            return carry
        return wait

    lax.fori_loop(0, batch, start, 0, unroll=DMA_UNROLL)

    @pl.when(i > 0)
    def _wait_previous():
        lax.fori_loop(0, batch, wait_of(i - 1), 0, unroll=DMA_UNROLL)

    @pl.when(i == last)
    def _wait_own():
        lax.fori_loop(0, batch, wait_of(i), 0, unroll=DMA_UNROLL)


def _dispatch(pos_flat, tile_zero, counts, pstart, tiles_per, n_used, xn, n_tiles):
    return pl.pallas_call(
        _dispatch_body,
        grid_spec=pltpu.PrefetchScalarGridSpec(
            num_scalar_prefetch=6,
            grid=(pos_flat.shape[0] // DISPATCH_BATCH,),
            in_specs=[pl.BlockSpec((DISPATCH_BATCH // 2, SLAB_ROWS, LANES), lambda i, *_: (i, 0, 0))],
            out_specs=[pl.BlockSpec(memory_space=pl.ANY), pl.BlockSpec(memory_space=pltpu.SMEM)],
            scratch_shapes=[pltpu.VMEM((MOE_TILE, SLAB_ROWS, LANES), F32),
                            pltpu.VMEM((2, DISPATCH_BATCH // 2, SLAB_ROWS, LANES), F32),
                            pltpu.SemaphoreType.DMA((n_tiles,)),
                            pltpu.SemaphoreType.DMA((2, DISPATCH_BATCH))]),
        out_shape=[jax.ShapeDtypeStruct((n_tiles * MOE_TILE, SLAB_ROWS, LANES), F32),
                   jax.ShapeDtypeStruct((n_tiles * MOE_TILE,), jnp.int32)],
        compiler_params=pltpu.CompilerParams(dimension_semantics=("arbitrary",), vmem_limit_bytes=VMEM_LIMIT),
        name="moe_dispatch",
    )(pos_flat, tile_zero, counts, pstart, tiles_per, n_used, xn)


def _moe_ffn_body(te_ref, nused_ref, pick_ref, x_ref, wg_ref, wu_ref, wd_ref, yt_hbm, ybuf, wgb, wub, wdb, ssems):
    i = pl.program_id(0)
    n_used = nused_ref[0]
    slot = i % 2
    n_pick = yt_hbm.shape[0] - MOE_TILE

    def scatter(tile, live, r, s):
        a = pick_ref[tile * MOE_TILE + r]
        dst = jnp.where(live & (a >= 0), a, n_pick + r)
        return pltpu.make_async_copy(ybuf.at[s, pl.ds(r, 1)], yt_hbm.at[pl.ds(dst, 1)], ssems.at[r])

    def scatter_done(r, s):
        return pltpu.make_async_copy(ybuf.at[s, pl.ds(r, 1)], yt_hbm.at[pl.ds(n_pick + r, 1)], ssems.at[r])

    @pl.when(i == 0)
    def _prologue():
        ybuf[1] = jnp.zeros_like(ybuf[1])

    @pl.when(i < n_used)
    def _tile():
        @pl.when((i == 0) | (te_ref[i] != te_ref[jnp.maximum(i - 1, 0)]))
        def _cast_weights():
            wgb[...] = wg_ref[0].astype(BF16)
            wub[...] = wu_ref[0].astype(BF16)
            wdb[...] = wd_ref[0].astype(BF16)

        x = jnp.concatenate([x_ref[:, j, :] for j in range(SLAB_ROWS)], axis=-1).astype(BF16)
        prv = jnp.maximum(i - 1, 0)
        for r in range(MOE_TILE):
            scatter(prv, i > 0, r, 1 - slot).start()
        gate = _dot(x, wgb[...])
        hmid = (gate * jax.nn.sigmoid(gate)) * _dot(x, wub[...])
        y = _dot(hmid.astype(BF16), wdb[...])
        for r in range(MOE_TILE):
            scatter_done(r, 1 - slot).wait()
        for j in range(SLAB_ROWS):
            ybuf[slot, :, j, :] = y[:, j * LANES:(j + 1) * LANES]

    @pl.when(i == n_used)
    def _drain():
        def start(r, c):
            scatter(i - 1, True, r, 1 - slot).start()
            return c

        def wait(r, c):
            scatter_done(r, 1 - slot).wait()
            return c

        lax.fori_loop(0, MOE_TILE, start, 0, unroll=DMA_UNROLL)
        lax.fori_loop(0, MOE_TILE, wait, 0, unroll=DMA_UNROLL)


def _moe_ffn(tile_expert, n_used, pick_of_row, xsorted, w_gate, w_up, w_down, n_pick):
    n_steps = tile_expert.shape[0]
    wspec = lambda s: pl.BlockSpec((1,) + s, lambda i, te, nu, pk: (te[i], 0, 0))
    return pl.pallas_call(
        _moe_ffn_body,
        grid_spec=pltpu.PrefetchScalarGridSpec(
            num_scalar_prefetch=3,
            grid=(n_steps,),
            in_specs=[pl.BlockSpec((MOE_TILE, SLAB_ROWS, LANES),
                                   lambda i, te, nu, pk: (jnp.clip(i, 0, jnp.maximum(nu[0] - 1, 0)), 0, 0)),
                      wspec((D_MODEL, MOE_D_FF)), wspec((D_MODEL, MOE_D_FF)), wspec((MOE_D_FF, D_MODEL))],
            out_specs=pl.BlockSpec(memory_space=pl.ANY),
            scratch_shapes=[pltpu.VMEM((2, MOE_TILE, SLAB_ROWS, LANES), F32),
                            pltpu.VMEM((D_MODEL, MOE_D_FF), BF16), pltpu.VMEM((D_MODEL, MOE_D_FF), BF16),
                            pltpu.VMEM((MOE_D_FF, D_MODEL), BF16),
                            pltpu.SemaphoreType.DMA((MOE_TILE,))]),
        out_shape=jax.ShapeDtypeStruct((n_pick + MOE_TILE, SLAB_ROWS, LANES), F32),
        compiler_params=pltpu.CompilerParams(dimension_semantics=("arbitrary",), vmem_limit_bytes=VMEM_LIMIT),
        name="moe_ffn",
    )(tile_expert, n_used, pick_of_row, xsorted, w_gate, w_up, w_down)


def _combine_body(x1_ref, rt_ref, yt_ref, nf_ref, out_ref):
    rt = rt_ref[...]
    x1 = x1_ref[...]
    x2 = jnp.concatenate(
        [x1[:, j * LANES:(j + 1) * LANES] + rt[:, 2:3] * yt_ref[:, 0, j, :] + rt[:, 3:4] * yt_ref[:, 1, j, :]
         for j in range(SLAB_ROWS)], axis=-1)
    out_ref[...] = _rms(x2, nf_ref[...])


def _combine(x1, rt, y_picks, nf, tm, rows, row_block_offset):
    row = lambda w: pl.BlockSpec((tm, w), lambda i: (i + row_block_offset, 0))
    return pl.pallas_call(
        _combine_body,
        grid=(rows // tm,),
        in_specs=[row(D_MODEL), row(LANES),
                  pl.BlockSpec((tm, 2, SLAB_ROWS, LANES), lambda i: (i + row_block_offset, 0, 0, 0)),
                  pl.BlockSpec((1, D_MODEL), lambda i: (0, 0))],
        out_specs=pl.BlockSpec((tm, D_MODEL), lambda i: (i, 0)),
        out_shape=jax.ShapeDtypeStruct((rows, D_MODEL), F32),
        compiler_params=pltpu.CompilerParams(dimension_semantics=("parallel",), vmem_limit_bytes=VMEM_LIMIT),
        name="moe_combine",
    )(x1, rt, y_picks, nf)


def _route_tables(counts, eid, rank, n_steps):
    experts = jnp.arange(MOE_EXPERTS, dtype=jnp.int32)
    tiles_per = (counts + MOE_TILE - 1) // MOE_TILE
    tile_end = jnp.cumsum(tiles_per)
    pstart = (tile_end - tiles_per) * MOE_TILE
    pos = jnp.sum(jnp.where(eid[..., None] == experts, pstart, 0), axis=-1) + rank
    n_used = tile_end[-1]
    steps = jnp.arange(n_steps, dtype=jnp.int32)
    tile_expert = jnp.sum((tile_end[None, :] <= jnp.minimum(steps, n_used - 1)[:, None]).astype(jnp.int32), axis=1)
    tiles = steps[:-1]
    ragged = counts % MOE_TILE != 0
    tile_zero = (tiles >= n_used) | jnp.any((tiles[:, None] == tile_end[None, :] - 1) & ragged[None, :], axis=1)
    return pos, pstart, tiles_per, tile_expert, tile_zero.astype(jnp.int32), n_used.reshape(1).astype(jnp.int32)


def _s5_tables(a_re, a_im, log_dt, b_re, b_im, c_re, c_im):
    dt = jnp.exp(log_dt)[:, None]
    mag = jnp.exp(a_re * dt)
    ab_re = mag * jnp.cos(a_im * dt)
    ab_im = mag * jnp.sin(a_im * dt)
    den = a_re * a_re + a_im * a_im
    nr = ab_re - 1.0
    q_re = (nr * a_re + ab_im * a_im) / den
    q_im = (ab_im * a_re - nr * a_im) / den
    bb_re = q_re[..., None] * b_re - q_im[..., None] * b_im
    bb_im = q_re[..., None] * b_im + q_im[..., None] * b_re
    eye = jnp.eye(16, dtype=F32)
    nblk = S5_GROUPS // 16

    def in_map(bb):
        w = jnp.einsum("jgpc,gh->jgchp", bb.reshape(nblk, 16, S5_STATE, S5_GROUP_CH), eye)
        return w.reshape(nblk, 16 * S5_GROUP_CH, 16 * S5_STATE)

    def out_map(cc):
        w = jnp.einsum("jgcp,gh->jgphc", cc.reshape(nblk, 16, S5_GROUP_CH, S5_STATE), eye)
        return w.reshape(nblk, 16 * S5_STATE, 16 * S5_GROUP_CH)

    wb = jnp.concatenate([in_map(bb_re), in_map(bb_im)], axis=-1).astype(BF16)
    return (wb, ab_re.reshape(1, S5_LANES), ab_im.reshape(1, S5_LANES),
            out_map(c_re).astype(BF16), out_map(-c_im).astype(BF16))


def kernel(x_prompt, x_sample, state_ssd_conv, state_ssd_ssm, state_s5_re, state_s5_im, meta_tokens, norm_mix, w_in, conv_w, conv_b, dt_bias, a_log, d_ssd, ssd_norm, s5_a_re, s5_a_im, s5_log_dt, s5_b_re, s5_b_im, s5_c_re, s5_c_im, s5_d, w_glu, b_glu, s5_norm, w_out, norm_ffn, router_coarse_w, router_coarse_b, router_fine_w, router_fine_b, w_gate, w_up, w_down, norm_final):
    bp, seq, _ = x_prompt.shape
    bs = x_sample.shape[0]
    n_prompt = bp * seq
    n_tok = n_prompt + bs
    row2 = lambda v: v.reshape(1, -1)
    pad_heads = lambda v: jnp.pad(v, (0, LANES - SSD_HEADS)).reshape(1, LANES)

    w = w_in[0]
    o1, o2, o3 = SSD_WIDTH, SSD_WIDTH + SSD_CONV_DIM, SSD_WIDTH + SSD_CONV_DIM + SSD_HEADS
    wz, wx, wu = w[:, :o1].astype(BF16), w[:, o1:o2].astype(BF16), w[:, o3:].astype(BF16)
    wdt = jnp.pad(w[:, o2:o3], ((0, 0), (0, LANES - SSD_HEADS))).astype(BF16)
    g_mix = row2(norm_mix[0])
    cw, cb = conv_w[0], row2(conv_b[0])
    dtb, alog = pad_heads(dt_bias[0]), pad_heads(a_log[0])
    dexp = row2(jnp.repeat(d_ssd[0], SSD_HEAD_DIM))
    snrm = row2(ssd_norm[0])
    eexp = (jnp.arange(LANES)[:, None] == (jnp.arange(SSD_WIDTH) // SSD_HEAD_DIM)[None, :]).astype(BF16)
    wb5, ab_re, ab_im, wcr, wci = _s5_tables(s5_a_re[0], s5_a_im[0], s5_log_dt[0], s5_b_re[0], s5_b_im[0],
                                             s5_c_re[0], s5_c_im[0])
    d5, wglu, bglu, nrm5 = row2(s5_d[0]), w_glu[0].astype(BF16), row2(b_glu[0]), row2(s5_norm[0])
    wo_a, wo_b = w_out[0][:SSD_WIDTH].astype(BF16), w_out[0][SSD_WIDTH:].astype(BF16)
    w_r = jnp.concatenate([router_coarse_w[0], router_fine_w[0].transpose(1, 0, 2).reshape(D_MODEL, MOE_EXPERTS)], axis=1)
    w_r = jnp.pad(w_r, ((0, 0), (0, LANES - w_r.shape[1])))
    wrh = w_r.astype(BF16)
    wrl = (w_r - wrh.astype(F32)).astype(BF16)
    b_r = jnp.concatenate([router_coarse_b[0], router_fine_b[0].reshape(-1)])
    b_r = jnp.pad(b_r, (0, LANES - b_r.shape[0])).reshape(1, LANES)

    zp, xbcp, dtp, up = _in_proj(x_prompt.reshape(n_prompt, D_MODEL), g_mix, wz, wx, wdt, wu, TOK_TILE, BF16, F32)
    xsm = jnp.concatenate([x_sample.reshape(bs, D_MODEL), meta_tokens], axis=0)
    zs, xbcs, dts, us = _in_proj(xsm, g_mix, wz, wx, wdt, wu, xsm.shape[0], F32, F32)

    front = SSD_CHUNK - N_META
    padf = lambda a: jnp.pad(a[bs:], ((front, 0), (0, 0)))[None]
    gw = SSD_HPG * SSD_HEAD_DIM
    ssd_consts = (cw, cb, dtb, alog, dexp, snrm, eexp)
    _, ctail_m, _, ht_m = _ssd_chunked(
        padf(xbcs), padf(dts), jnp.zeros((1, SSD_CHUNK, SSD_WIDTH), F32),
        jnp.zeros((1, SUBLANES, SSD_CONV_DIM), F32), jnp.zeros((1, SSD_GROUPS, SSD_STATE, gw), F32),
        *ssd_consts, mask_rows=front)
    y_ssd_p, ctail_p, ssm_p, _ = _ssd_chunked(
        xbcp.reshape(bp, seq, SSD_CONV_DIM), dtp.reshape(bp, seq, LANES), zp.reshape(bp, seq, SSD_WIDTH),
        ctail_m, ht_m, *ssd_consts, mask_rows=0)

    abr8, abi8 = jnp.broadcast_to(ab_re, (bp, S5_LANES)), jnp.broadcast_to(ab_im, (bp, S5_LANES))
    um8 = jnp.repeat(us[bs:], bp, axis=0).astype(BF16)
    y_s5_p, s5re_p, s5im_p = _s5_seq(up.reshape(bp, seq, S5_WIDTH), um8, wb5, abr8, abi8,
                                     wcr, wci, d5, wglu, bglu, nrm5)

    cst = state_ssd_conv[0]
    xt_s, dt_s, dec_s, bc, xs_s = _ssd_step_prep(xbcs[:bs], cst[:, 0], cst[:, 1], cst[:, 2], dts[:bs],
                                                 cw, cb, dtb, alog)
    ssm_s, y_core = _ssd_step(dt_s[:, :SSD_HEADS].reshape(-1), dec_s[:, :SSD_HEADS].reshape(-1),
                              state_ssd_ssm[0], xt_s, bc)
    y_ssd_s, y_s5_s, s5re_s, s5im_s = _sample_post(
        y_core, xs_s, zs[:bs], dexp, snrm, us[:bs], state_s5_re[0].reshape(bs, S5_LANES),
        state_s5_im[0].reshape(bs, S5_LANES), wb5, ab_re, ab_im, wcr, wci, d5, wglu, bglu, nrm5)

    route_consts = (wo_a, wo_b, row2(norm_ffn[0]), wrh, wrl, b_r)
    x1, xn, rt, counts = _mix_route(
        (x_prompt.reshape(n_prompt, D_MODEL), y_ssd_p.reshape(n_prompt, SSD_WIDTH), y_s5_p.reshape(n_prompt, S5_WIDTH)),
        (x_sample.reshape(bs, D_MODEL), y_ssd_s, y_s5_s), route_consts, TOK_TILE)

    n_tiles = -(-2 * n_tok // MOE_TILE) + MOE_EXPERTS
    eid = jnp.clip(rt[:, 0:2].astype(jnp.int32), 0, MOE_EXPERTS - 1)
    counts_i = counts[0, :MOE_EXPERTS].astype(jnp.int32)
    pos, pstart, tiles_per, tile_expert, tile_zero, n_used = _route_tables(
        counts_i, eid, rt[:, 4:6].astype(jnp.int32), n_tiles + 1)
    xsorted, pick_of_row = _dispatch(pos.reshape(-1), tile_zero, counts_i, pstart, tiles_per, n_used, xn, n_tiles)
    y_picks = _moe_ffn(tile_expert, n_used, pick_of_row, xsorted, w_gate[0], w_up[0], w_down[0], 2 * n_tok)
    y_picks = y_picks.reshape(-1, 2, SLAB_ROWS, LANES)
    nfin = row2(norm_final)
    y_p = _combine(x1, rt, y_picks, nfin, MOE_TILE, n_prompt, 0)
    y_s = _combine(x1, rt, y_picks, nfin, bs, bs, n_prompt // bs)

    s5_state = lambda a, b: a.reshape(1, b, S5_GROUPS, S5_STATE)
    new_conv_s = jnp.stack([cst[:, 1], cst[:, 2], xbcs[:bs]], axis=1)[None]
    return (y_p.reshape(bp, seq, D_MODEL), y_s.reshape(bs, 1, D_MODEL),
            ctail_p[:, SUBLANES - (SSD_CONV - 1):][None], ssm_p[None], s5_state(s5re_p, bp), s5_state(s5im_p, bp),
            new_conv_s, ssm_s[None], s5_state(s5re_s, bs), s5_state(s5im_s, bs))
```

```python
import functools

import jax
import jax.numpy as jnp
from jax import lax
from jax.experimental import pallas as pl
from jax.experimental.pallas import tpu as pltpu
from jax.experimental.pallas import tpu_sc as plsc

F32, BF16 = jnp.float32, jnp.bfloat16

D_MODEL = 1024
N_META = 16
SSD_WIDTH = 1024
SSD_HEAD_DIM = 64
SSD_HEADS = 16
SSD_GROUPS = 2
SSD_HPG = SSD_HEADS // SSD_GROUPS
SSD_STATE = 128
SSD_CONV = 4
SSD_CHUNK = 128
SSD_CONV_DIM = SSD_WIDTH + 2 * SSD_GROUPS * SSD_STATE
S5_WIDTH = 1024
S5_GROUP_CH = 16
S5_GROUPS = 64
S5_STATE = 64
S5_LANES = S5_GROUPS * S5_STATE
MOE_GROUPS = 4
MOE_EPG = 8
MOE_EXPERTS = MOE_GROUPS * MOE_EPG
MOE_D_FF = 512
EPS = 1e-6

LANES = 128
SUBLANES = 8
VMEM_LIMIT = 56 * 1024 * 1024

S5_TIME_TILE = 32
S5_SCAN_LANES = 512
MOE_TILE = 256
SLAB_ROWS = D_MODEL // LANES
SC_CORES = 2
SC_SUBCORES = 16
SC_WORKERS = SC_CORES * SC_SUBCORES
SC_DISPATCH_ROWS = 64
SC_COLLECT_ROWS = 24
TOK_TILE = 512


def _dot(a, b):
    return jnp.dot(a, b, preferred_element_type=F32)


def _rms(x, g):
    return x * lax.rsqrt(jnp.mean(x * x, axis=-1, keepdims=True) + EPS) * g


def _softplus(x):
    return jnp.maximum(x, 0.0) + jnp.log1p(jnp.exp(-jnp.abs(x)))


def _split3(x):
    hi = x.astype(BF16)
    r = x - hi.astype(F32)
    mid = r.astype(BF16)
    lo = (r - mid.astype(F32)).astype(BF16)
    return hi, mid, lo


def _dot3(x, w):
    hi, mid, lo = _split3(x)
    return _dot(hi, w) + _dot(mid, w) + _dot(lo, w)


def _dot3_left(w, x):
    hi, mid, lo = _split3(x)
    return _dot(w, hi) + _dot(w, mid) + _dot(w, lo)


def _full_spec(a):
    nd = a.ndim
    return pl.BlockSpec(a.shape, lambda *_: (0,) * nd)


def _in_proj_body(x_ref, g_ref, wz_ref, wx_ref, wdt_ref, wu_ref, z_ref, xbc_ref, dt_ref, u_ref):
    xb = _rms(x_ref[...], g_ref[...]).astype(BF16)
    z_ref[...] = _dot(xb, wz_ref[...]).astype(z_ref.dtype)
    xbc_ref[...] = _dot(xb, wx_ref[...]).astype(xbc_ref.dtype)
    dt_ref[...] = _dot(xb, wdt_ref[...])
    u_ref[...] = _dot(xb, wu_ref[...]).astype(u_ref.dtype)


def _in_proj(x2d, g, wz, wx, wdt, wu, tm, act_dtype, u_dtype):
    rows = x2d.shape[0]
    row = lambda w: pl.BlockSpec((tm, w), lambda i: (i, 0))
    return pl.pallas_call(
        _in_proj_body,
        grid=(rows // tm,),
        in_specs=[row(D_MODEL), _full_spec(g), _full_spec(wz), _full_spec(wx), _full_spec(wdt), _full_spec(wu)],
        out_specs=[row(SSD_WIDTH), row(SSD_CONV_DIM), row(LANES), row(S5_WIDTH)],
        out_shape=[jax.ShapeDtypeStruct((rows, SSD_WIDTH), act_dtype),
                   jax.ShapeDtypeStruct((rows, SSD_CONV_DIM), act_dtype),
                   jax.ShapeDtypeStruct((rows, LANES), F32),
                   jax.ShapeDtypeStruct((rows, S5_WIDTH), u_dtype)],
        compiler_params=pltpu.CompilerParams(dimension_semantics=("parallel",), vmem_limit_bytes=VMEM_LIMIT),
        name="in_proj",
    )(x2d, g, wz, wx, wdt, wu)


def _ssd_body(mask_rows, xbc_ref, dt_ref, z_ref, cinit_ref, hinit_ref, cw_ref, cb_ref, dtb_ref, alog_ref,
              dexp_ref, nrm_ref, eexp_ref, y_ref, ctail_ref, st_ref, hto_ref, xwin, hT):
    c = pl.program_id(1)
    L = SSD_CHUNK

    @pl.when(c == 0)
    def _init():
        xwin[0:SUBLANES, :] = cinit_ref[0]
        hT[...] = hinit_ref[0]

    xwin[SUBLANES:SUBLANES + L, :] = xbc_ref[0].astype(F32)
    acc = cb_ref[...]
    for k in range(SSD_CONV):
        off = SUBLANES - (SSD_CONV - 1) + k
        acc = acc + xwin[off:off + L, :] * cw_ref[k:k + 1, :]
    tail = xwin[L:L + SUBLANES, :]
    xwin[0:SUBLANES, :] = tail
    ctail_ref[0] = tail

    xact = acc * jax.nn.sigmoid(acc)
    dt = _softplus(dt_ref[0] + dtb_ref[...])
    if mask_rows:
        valid = lax.broadcasted_iota(jnp.int32, (L, 1), 0) >= mask_rows
        xact = jnp.where(valid, xact, 0.0)
        dt = jnp.where(valid, dt, 0.0)

    a_neg = -jnp.exp(alog_ref[...])
    dA = dt * a_neg
    row_i = lax.broadcasted_iota(jnp.int32, (L, L), 0)
    col_i = lax.broadcasted_iota(jnp.int32, (L, L), 1)
    causal = row_i >= col_i
    tril = causal.astype(BF16)
    cs = _dot3_left(tril, dA)
    csT = cs.T
    dtT = dt.T
    ecs = jnp.exp(cs)
    wdec = jnp.exp(cs[L - 1:L, :] - cs) * dt
    eexp = eexp_ref[...]
    ecs_e = _dot3(ecs, eexp)
    wdec_e = _dot3(wdec, eexp)
    lane = lax.broadcasted_iota(jnp.int32, (L, LANES), 1)
    first_half = lane < SSD_HEAD_DIM

    gw = SSD_HPG * SSD_HEAD_DIM
    y_groups = []
    for g in range(SSD_GROUPS):
        b_g = xact[:, SSD_WIDTH + g * SSD_STATE: SSD_WIDTH + (g + 1) * SSD_STATE]
        c_g = xact[:, SSD_WIDTH + (SSD_GROUPS + g) * SSD_STATE: SSD_WIDTH + (SSD_GROUPS + g + 1) * SSD_STATE]
        b_b = b_g.astype(BF16)
        c_b = c_g.astype(BF16)
        cb = lax.dot_general(c_b, b_b, (((1,), (1,)), ((), ())), preferred_element_type=F32)
        xs_g = xact[:, g * gw:(g + 1) * gw]
        h_prev = hT[g]
        y_off = _dot(c_b, h_prev.astype(BF16)) * ecs_e[:, g * gw:(g + 1) * gw]
        xdec = (xs_g * wdec_e[:, g * gw:(g + 1) * gw]).astype(BF16)
        hT[g] = h_prev * ecs_e[L - 1:L, g * gw:(g + 1) * gw] + _dot(b_g.T.astype(BF16), xdec)
        pieces = []
        for j in range(SSD_HPG // 2):
            xs_pair = xs_g[:, j * LANES:(j + 1) * LANES]
            halves = (jnp.where(first_half, xs_pair, 0.0).astype(BF16),
                      jnp.where(first_half, 0.0, xs_pair).astype(BF16))
            yd = None
            for t in range(2):
                h = g * SSD_HPG + 2 * j + t
                seg = cs[:, h:h + 1] - csT[h:h + 1, :]
                lmat = jnp.exp(jnp.where(causal, seg, -jnp.inf))
                m = (cb * lmat * dtT[h:h + 1, :]).astype(BF16)
                part = _dot(m, halves[t])
                yd = part if yd is None else yd + part
            pieces.append(yd)
        y_groups.append(jnp.concatenate(pieces, axis=-1) + y_off + dexp_ref[:, g * gw:(g + 1) * gw] * xs_g)
    y = jnp.concatenate(y_groups, axis=-1)
    z = z_ref[0].astype(F32)
    y_ref[0] = _rms(y * (z * jax.nn.sigmoid(z)), nrm_ref[...]).astype(y_ref.dtype)

    @pl.when(c == pl.num_programs(1) - 1)
    def _emit():
        hto_ref[0] = hT[...]
        for g in range(SSD_GROUPS):
            t = hT[g].T
            for k in range(SSD_HPG):
                st_ref[0, g * SSD_HPG + k] = t[k * SSD_HEAD_DIM:(k + 1) * SSD_HEAD_DIM, :]


def _ssd_chunked(xbc, dt, z, cinit, hinit, cw, cb, dtb, alog, dexp, nrm, eexp, mask_rows):
    bsz, seq, _ = xbc.shape
    nc = seq // SSD_CHUNK
    gw = SSD_HPG * SSD_HEAD_DIM
    blk = lambda w: pl.BlockSpec((1, SSD_CHUNK, w), lambda b, c: (b, c, 0))
    return pl.pallas_call(
        functools.partial(_ssd_body, mask_rows),
        grid=(bsz, nc),
        in_specs=[blk(SSD_CONV_DIM), blk(LANES), blk(SSD_WIDTH),
                  pl.BlockSpec((1, SUBLANES, SSD_CONV_DIM), lambda b, c: (0, 0, 0)),
                  pl.BlockSpec((1, SSD_GROUPS, SSD_STATE, gw), lambda b, c: (0, 0, 0, 0)),
                  _full_spec(cw), _full_spec(cb), _full_spec(dtb), _full_spec(alog),
                  _full_spec(dexp), _full_spec(nrm), _full_spec(eexp)],
        out_specs=[blk(SSD_WIDTH),
                   pl.BlockSpec((1, SUBLANES, SSD_CONV_DIM), lambda b, c: (b, 0, 0)),
                   pl.BlockSpec((1, SSD_HEADS, SSD_HEAD_DIM, SSD_STATE), lambda b, c: (b, 0, 0, 0)),
                   pl.BlockSpec((1, SSD_GROUPS, SSD_STATE, gw), lambda b, c: (b, 0, 0, 0))],
        out_shape=[jax.ShapeDtypeStruct((bsz, seq, SSD_WIDTH), BF16),
                   jax.ShapeDtypeStruct((bsz, SUBLANES, SSD_CONV_DIM), F32),
                   jax.ShapeDtypeStruct((bsz, SSD_HEADS, SSD_HEAD_DIM, SSD_STATE), F32),
                   jax.ShapeDtypeStruct((bsz, SSD_GROUPS, SSD_STATE, gw), F32)],
        scratch_shapes=[pltpu.VMEM((SUBLANES + SSD_CHUNK, SSD_CONV_DIM), F32),
                        pltpu.VMEM((SSD_GROUPS, SSD_STATE, gw), F32)],
        compiler_params=pltpu.CompilerParams(dimension_semantics=("parallel", "arbitrary"),
                                             vmem_limit_bytes=VMEM_LIMIT),
        name="ssd_chunked",
    )(xbc, dt, z, cinit, hinit, cw, cb, dtb, alog, dexp, nrm, eexp)


def _ssd_step_prep_body(xbc_ref, c0_ref, c1_ref, c2_ref, dt_ref, cw_ref, cb_ref, dtb_ref, alog_ref,
                        xt_ref, dt_out_ref, dec_ref, bc_ref, xs_ref):
    acc = cb_ref[...]
    for k, r in enumerate((c0_ref, c1_ref, c2_ref, xbc_ref)):
        acc = acc + r[...] * cw_ref[k:k + 1, :]
    xact = acc * jax.nn.sigmoid(acc)
    xs = xact[:, :SSD_WIDTH]
    dt = _softplus(dt_ref[...] + dtb_ref[...])
    dt_out_ref[...] = dt
    dec_ref[...] = jnp.exp(dt * -jnp.exp(alog_ref[...]))
    bc_ref[...] = xact[:, SSD_WIDTH:]
    xs_ref[...] = xs
    xt_ref[...] = xs.T.astype(xt_ref.dtype)


def _ssd_step_prep(xbc, c0, c1, c2, dt, cw, cb, dtb, alog):
    n = xbc.shape[0]
    args = (xbc, c0, c1, c2, dt, cw, cb, dtb, alog)
    spec = lambda r, w: pl.BlockSpec((r, w), lambda: (0, 0))
    return pl.pallas_call(
        _ssd_step_prep_body,
        in_specs=[_full_spec(a) for a in args],
        out_specs=[spec(SSD_WIDTH, n), spec(n, LANES), spec(n, LANES), spec(n, 2 * SSD_GROUPS * SSD_STATE),
                   spec(n, SSD_WIDTH)],
        out_shape=[jax.ShapeDtypeStruct((SSD_WIDTH, n), BF16), jax.ShapeDtypeStruct((n, LANES), F32),
                   jax.ShapeDtypeStruct((n, LANES), F32),
                   jax.ShapeDtypeStruct((n, 2 * SSD_GROUPS * SSD_STATE), F32),
                   jax.ShapeDtypeStruct((n, SSD_WIDTH), F32)],
        compiler_params=pltpu.CompilerParams(vmem_limit_bytes=VMEM_LIMIT),
        name="ssd_step_prep",
    )(*args)


def _ssd_step_body(dt_ref, dec_ref, st_ref, xt_ref, bc_ref, so_ref, y_ref):
    n = xt_ref.shape[1]
    gw = SSD_HPG * SSD_HEAD_DIM
    blk = pl.program_id(0)
    seq_id = lax.broadcasted_iota(jnp.int32, (n, SSD_STATE), 0)
    sub_id = lax.broadcasted_iota(jnp.int32, (SUBLANES, gw), 0)
    base = pl.multiple_of(blk * SUBLANES, SUBLANES)
    y_acc = [jnp.zeros((SUBLANES, gw), F32) for _ in range(SSD_GROUPS)]
    for i in range(SUBLANES):
        s = blk * SUBLANES + i
        for g in range(SSD_GROUPS):
            b_all = bc_ref[:, g * SSD_STATE:(g + 1) * SSD_STATE]
            rhs = jnp.where(seq_id == s, b_all, 0.0).astype(BF16)
            outer = _dot(xt_ref[g * gw:(g + 1) * gw, :], rhs)
            news = []
            for k in range(SSD_HPG):
                h = g * SSD_HPG + k
                new = (dec_ref[s * SSD_HEADS + h] * st_ref[i, h]
                       + dt_ref[s * SSD_HEADS + h] * outer[k * SSD_HEAD_DIM:(k + 1) * SSD_HEAD_DIM, :])
                so_ref[i, h] = new
                news.append(new)
            new_g = jnp.concatenate(news, axis=0).astype(BF16)
            c_lo = (SSD_GROUPS + g) * SSD_STATE
            c_blk = bc_ref[pl.ds(base, SUBLANES), c_lo:c_lo + SSD_STATE].astype(BF16)
            r = lax.dot_general(c_blk, new_g, (((1,), (1,)), ((), ())), preferred_element_type=F32)
            y_acc[g] = y_acc[g] + jnp.where(sub_id == i, r, 0.0)
    y_ref[...] = jnp.concatenate(y_acc, axis=-1)


def _ssd_step(dt_flat, dec_flat, state, xt, bc):
    n = state.shape[0]
    st_spec = pl.BlockSpec((SUBLANES, SSD_HEADS, SSD_HEAD_DIM, SSD_STATE), lambda i, *_: (i, 0, 0, 0))
    return pl.pallas_call(
        _ssd_step_body,
        grid_spec=pltpu.PrefetchScalarGridSpec(
            num_scalar_prefetch=2,
            grid=(n // SUBLANES,),
            in_specs=[st_spec, pl.BlockSpec(xt.shape, lambda i, *_: (0, 0)),
                      pl.BlockSpec(bc.shape, lambda i, *_: (0, 0))],
            out_specs=[st_spec, pl.BlockSpec((SUBLANES, SSD_WIDTH), lambda i, *_: (i, 0))]),
        out_shape=[jax.ShapeDtypeStruct(state.shape, F32), jax.ShapeDtypeStruct((n, SSD_WIDTH), F32)],
        compiler_params=pltpu.CompilerParams(dimension_semantics=("parallel",), vmem_limit_bytes=VMEM_LIMIT),
        name="ssd_step",
    )(dt_flat, dec_flat, state, xt, bc)


def _s5_project_in(u_b16, wb_ref, store):
    kw = 16 * S5_GROUP_CH
    nw = 16 * S5_STATE
    for j in range(S5_WIDTH // kw):
        r = _dot(u_b16[:, j * kw:(j + 1) * kw], wb_ref[j])
        store(j, r[:, :nw], r[:, nw:])


def _s5_tail(hre_of, him_of, u_f32, wcr_ref, wci_ref, d_ref, wglu_ref, bglu_ref, nrm_ref):
    cols = []
    for j in range(wcr_ref.shape[0]):
        cols.append(_dot(hre_of(j).astype(BF16), wcr_ref[j]) + _dot(him_of(j).astype(BF16), wci_ref[j]))
    y = jnp.concatenate(cols, axis=-1) + d_ref[...] * u_f32
    y = jax.nn.gelu(y)
    y = y * jax.nn.sigmoid(_dot(y.astype(BF16), wglu_ref[...]) + bglu_ref[...])
    return _rms(y, nrm_ref[...])


def _s5_seq_body(u_hbm, um_ref, wb_ref, abr_ref, abi_ref, wcr_ref, wci_ref, d_ref, wglu_ref, bglu_ref, nrm_ref,
                 y_hbm, sre_ref, sim_ref, ubuf, ybuf, bu, h, in_sems, out_sems):
    j = pl.program_id(0)
    last = pl.num_programs(0) - 1
    lc, bsz = ubuf.shape[1], ubuf.shape[2]
    rows = lc * bsz
    nw = 16 * S5_STATE

    def in_copy(step, b):
        return pltpu.make_async_copy(u_hbm.at[b, pl.ds(step * lc, lc), :], ubuf.at[step % 2, :, b, :],
                                     in_sems.at[step % 2, b])

    def out_copy(step, b):
        return pltpu.make_async_copy(ybuf.at[step % 2, :, b, :], y_hbm.at[b, pl.ds(step * lc, lc), :],
                                     out_sems.at[step % 2, b])

    def project_in(u_b16, nrows):
        def store(jj, re, im):
            bu[0:nrows, jj * nw:(jj + 1) * nw] = re
            bu[0:nrows, S5_LANES + jj * nw:S5_LANES + (jj + 1) * nw] = im
        _s5_project_in(u_b16, wb_ref, store)

    def scan(nsteps):
        for k in range(S5_LANES // S5_SCAN_LANES):
            sl_r = pl.ds(k * S5_SCAN_LANES, S5_SCAN_LANES)
            sl_i = pl.ds(S5_LANES + k * S5_SCAN_LANES, S5_SCAN_LANES)
            ar = abr_ref[:, sl_r]
            ai = abi_ref[:, sl_r]

            def step(l, carry):
                hr, hi = carry
                slab = pl.ds(pl.multiple_of(l * bsz, bsz), bsz)
                nr = ar * hr - ai * hi + bu[slab, sl_r]
                ni = ar * hi + ai * hr + bu[slab, sl_i]
                bu[slab, sl_r] = nr
                bu[slab, sl_i] = ni
                return nr, ni

            hr, hi = lax.fori_loop(0, nsteps, step, (h[:, sl_r], h[:, sl_i]))
            h[:, sl_r] = hr
            h[:, sl_i] = hi

    @pl.when(j == 0)
    def _first():
        for b in range(bsz):
            in_copy(0, b).start()
        h[...] = jnp.zeros_like(h)
        project_in(um_ref[...], N_META * bsz)
        scan(N_META)

    @pl.when(j < last)
    def _prefetch():
        for b in range(bsz):
            in_copy(j + 1, b).start()

    for b in range(bsz):
        in_copy(j, b).wait()
    u2 = ubuf[j % 2].reshape(rows, S5_WIDTH)
    project_in(u2.astype(BF16), rows)
    scan(lc)
    y = _s5_tail(lambda jj: bu[:, jj * nw:(jj + 1) * nw], lambda jj: bu[:, S5_LANES + jj * nw:S5_LANES + (jj + 1) * nw],
                 u2, wcr_ref, wci_ref, d_ref, wglu_ref, bglu_ref, nrm_ref)
    ybuf[j % 2] = y.reshape(lc, bsz, S5_WIDTH)
    for b in range(bsz):
        out_copy(j, b).start()

    @pl.when(j > 0)
    def _wait_previous_out():
        for b in range(bsz):
            out_copy(j - 1, b).wait()

    @pl.when(j == last)
    def _emit():
        for b in range(bsz):
            out_copy(j, b).wait()
        sre_ref[...] = h[:, 0:S5_LANES]
        sim_ref[...] = h[:, S5_LANES:]


def _s5_seq(u, um, wb, abr, abi, wcr, wci, d, wglu, bglu, nrm):
    bsz, seq, _ = u.shape
    lc = S5_TIME_TILE
    consts = (um, wb, abr, abi, wcr, wci, d, wglu, bglu, nrm)
    st = pl.BlockSpec((bsz, S5_LANES), lambda j: (0, 0))
    return pl.pallas_call(
        _s5_seq_body,
        grid=(seq // lc,),
        in_specs=[pl.BlockSpec(memory_space=pl.ANY)] + [_full_spec(a) for a in consts],
        out_specs=[pl.BlockSpec(memory_space=pl.ANY), st, st],
        out_shape=[jax.ShapeDtypeStruct((bsz, seq, S5_WIDTH), F32),
                   jax.ShapeDtypeStruct((bsz, S5_LANES), F32), jax.ShapeDtypeStruct((bsz, S5_LANES), F32)],
        scratch_shapes=[pltpu.VMEM((2, lc, bsz, S5_WIDTH), F32), pltpu.VMEM((2, lc, bsz, S5_WIDTH), F32),
                        pltpu.VMEM((lc * bsz, 2 * S5_LANES), F32), pltpu.VMEM((bsz, 2 * S5_LANES), F32),
                        pltpu.SemaphoreType.DMA((2, bsz)), pltpu.SemaphoreType.DMA((2, bsz))],
        compiler_params=pltpu.CompilerParams(dimension_semantics=("arbitrary",), vmem_limit_bytes=VMEM_LIMIT),
        name="s5_seq",
    )(u, *consts)


def _sample_post_body(yc_ref, xs_ref, z_ref, dexp_ref, snrm_ref, u_ref, hr_ref, hi_ref, wb_ref, abr_ref, abi_ref,
                      wcr_ref, wci_ref, d_ref, wglu_ref, bglu_ref, nrm_ref,
                      yssd_ref, ys5_ref, nre_ref, nim_ref):
    z = z_ref[...]
    y = yc_ref[...] + dexp_ref[...] * xs_ref[...]
    yssd_ref[...] = _rms(y * (z * jax.nn.sigmoid(z)), snrm_ref[...]).astype(yssd_ref.dtype)

    u = u_ref[...]
    nw = 16 * S5_STATE
    ar, ai = abr_ref[...], abi_ref[...]

    def store(jj, re, im):
        sl = slice(jj * nw, (jj + 1) * nw)
        h0r, h0i = hr_ref[:, sl], hi_ref[:, sl]
        nre_ref[:, sl] = ar[:, sl] * h0r - ai[:, sl] * h0i + re
        nim_ref[:, sl] = ar[:, sl] * h0i + ai[:, sl] * h0r + im

    _s5_project_in(u.astype(BF16), wb_ref, store)
    slab = lambda ref: (lambda jj: ref[:, jj * nw:(jj + 1) * nw])
    y5 = _s5_tail(slab(nre_ref), slab(nim_ref), u, wcr_ref, wci_ref, d_ref, wglu_ref, bglu_ref, nrm_ref)
    ys5_ref[...] = y5.astype(ys5_ref.dtype)


def _sample_post(yc, xs, z, dexp, snrm, u, h0r, h0i, wb, abr1, abi1, wcr, wci, d, wglu, bglu, nrm):
    n = yc.shape[0]
    args = (yc, xs, z, dexp, snrm, u, h0r, h0i, wb, abr1, abi1, wcr, wci, d, wglu, bglu, nrm)
    spec = lambda w: pl.BlockSpec((n, w), lambda: (0, 0))
    return pl.pallas_call(
        _sample_post_body,
        in_specs=[_full_spec(a) for a in args],
        out_specs=[spec(SSD_WIDTH), spec(S5_WIDTH), spec(S5_LANES), spec(S5_LANES)],
        out_shape=[jax.ShapeDtypeStruct((n, SSD_WIDTH), BF16), jax.ShapeDtypeStruct((n, S5_WIDTH), BF16),
                   jax.ShapeDtypeStruct((n, S5_LANES), F32), jax.ShapeDtypeStruct((n, S5_LANES), F32)],
        compiler_params=pltpu.CompilerParams(vmem_limit_bytes=VMEM_LIMIT),
        name="sample_post",
    )(*args)


def _mix_route_body(n_blocks, xp_ref, ysp_ref, y5p_ref, xs_ref, yss_ref, y5s_ref, *refs):
    cnt_ref, carry = refs[-2:]
    i = pl.program_id(0)

    @pl.when(i == 0)
    def _init():
        carry[...] = jnp.zeros_like(carry)

    @pl.when(i < n_blocks)
    def _prompt_rows():
        _mix_route_compute(xp_ref, ysp_ref, y5p_ref, *refs)

    @pl.when(i == n_blocks)
    def _sample_rows():
        _mix_route_compute(xs_ref, yss_ref, y5s_ref, *refs)

    cnt_ref[...] = carry[...]


def _mix_route_compute(x_ref, ys_ref, y5_ref, wa_ref, wb_ref, nf_ref, wrh_ref, wrl_ref, br_ref,
                       x1_ref, xn_ref, rt_ref, _, carry):
    rows = x_ref.shape[0]
    x1 = x_ref[...] + _dot(ys_ref[...], wa_ref[...]) + _dot(y5_ref[...].astype(BF16), wb_ref[...])
    x1_ref[0:rows, :] = x1
    xn = _rms(x1, nf_ref[...])
    for j in range(SLAB_ROWS):
        xn_ref[0:rows, j, :] = xn[:, j * LANES:(j + 1) * LANES]

    xh = xn.astype(BF16)
    xl = (xn - xh.astype(F32)).astype(BF16)
    logits = _dot(xh, wrh_ref[...]) + _dot(xl, wrh_ref[...]) + _dot(xh, wrl_ref[...]) + br_ref[...]
    tm = logits.shape[0]
    lane = lax.broadcasted_iota(jnp.int32, logits.shape, 1).astype(F32)
    neg = -jnp.inf
    big = float(LANES)

    def first_max(v):
        m = jnp.max(v, axis=-1, keepdims=True)
        return m, jnp.min(jnp.where(v == m, lane, big), axis=-1, keepdims=True)

    coarse = lane < MOE_GROUPS
    mc, gsel = first_max(jnp.where(coarse, logits, neg))
    psel = 1.0 / jnp.sum(jnp.where(coarse, jnp.exp(logits - mc), 0.0), axis=-1, keepdims=True)
    lo = MOE_GROUPS + MOE_EPG * gsel
    lf = jnp.where((lane >= lo) & (lane < lo + MOE_EPG), logits, neg)
    m1, i1 = first_max(lf)
    m2, i2 = first_max(jnp.where(lane == i1, neg, lf))
    e2 = jnp.exp(m2 - m1)
    g1 = psel / (1.0 + e2)
    g2 = psel * e2 / (1.0 + e2)
    e_a, e_b = i1 - MOE_GROUPS, i2 - MOE_GROUPS

    pick_a, pick_b = lane == e_a, lane == e_b
    picks = jnp.where(pick_a | pick_b, 1.0, 0.0)
    earlier = lax.broadcasted_iota(jnp.int32, (tm, tm), 0) > lax.broadcasted_iota(jnp.int32, (tm, tm), 1)
    prior = _dot(earlier.astype(BF16), picks.astype(BF16)) + carry[...]
    rank_a = jnp.sum(jnp.where(pick_a, prior, 0.0), axis=-1, keepdims=True)
    rank_b = jnp.sum(jnp.where(pick_b, prior, 0.0), axis=-1, keepdims=True)
    carry[...] = prior[tm - 1:tm, :] + picks[tm - 1:tm, :]

    out = jnp.zeros_like(logits)
    for k, v in enumerate((e_a, e_b, g1, g2, rank_a, rank_b)):
        out = jnp.where(lane == float(k), v, out)
    rt_ref[0:rows, :] = out


def _mix_route(prompt, sample, consts, tm):
    n_prompt, n_sample = prompt[0].shape[0], sample[0].shape[0]
    assert n_prompt % tm == 0 and n_sample <= tm
    n_blocks = n_prompt // tm
    total_rows = n_prompt + n_sample
    row = lambda w: pl.BlockSpec((tm, w), lambda i: (jnp.minimum(i, n_blocks - 1), 0))
    out_row = lambda w: pl.BlockSpec((tm, w), lambda i: (i, 0))
    return pl.pallas_call(
        functools.partial(_mix_route_body, n_blocks),
        grid=(n_blocks + 1,),
        in_specs=([row(D_MODEL), row(SSD_WIDTH), row(S5_WIDTH)] + [_full_spec(a) for a in sample]
                  + [_full_spec(a) for a in consts]),
        out_specs=[out_row(D_MODEL), pl.BlockSpec((tm, SLAB_ROWS, LANES), lambda i: (i, 0, 0)),
                   out_row(LANES), pl.BlockSpec((1, LANES), lambda i: (0, 0))],
        out_shape=[jax.ShapeDtypeStruct((total_rows, D_MODEL), F32),
                   jax.ShapeDtypeStruct((total_rows, SLAB_ROWS, LANES), F32),
                   jax.ShapeDtypeStruct((total_rows, LANES), F32), jax.ShapeDtypeStruct((1, LANES), F32)],
        scratch_shapes=[pltpu.VMEM((1, LANES), F32)],
        compiler_params=pltpu.CompilerParams(dimension_semantics=("arbitrary",), vmem_limit_bytes=VMEM_LIMIT),
        name="mix_route",
    )(*prompt, *sample, *consts)


def _sc_mesh():
    return plsc.VectorSubcoreMesh(core_axis_name="c", subcore_axis_name="s")


def _sc_worker():
    return lax.axis_index("s") * SC_CORES + lax.axis_index("c")


def _sc_dispatch(xn, pos_a, pos_b, n_rows):
    n_tok = xn.shape[0]
    ch = SC_DISPATCH_ROWS
    assert n_tok % ch == 0

    @functools.partial(
        pl.kernel, mesh=_sc_mesh(),
        out_type=jax.ShapeDtypeStruct((n_rows, SLAB_ROWS, LANES), F32),
        scratch_types=[pltpu.VMEM((ch,), jnp.int32), pltpu.VMEM((ch,), jnp.int32),
                       pltpu.VMEM((ch, SLAB_ROWS, LANES), F32), pltpu.SemaphoreType.DMA])
    def push(xn_hbm, pa_hbm, pb_hbm, xs_hbm, ia, ib, rows, sem):
        @pl.loop(_sc_worker(), n_tok // ch, step=SC_WORKERS)
        def _(c):
            off = pl.multiple_of(c * ch, ch)
            pltpu.sync_copy(pa_hbm.at[pl.ds(off, ch)], ia)
            pltpu.sync_copy(pb_hbm.at[pl.ds(off, ch)], ib)
            pltpu.sync_copy(xn_hbm.at[pl.ds(off, ch)], rows)
            pltpu.async_copy(rows, xs_hbm.at[ia], sem).wait()
            pltpu.async_copy(rows, xs_hbm.at[ib], sem).wait()

    return push(xn, pos_a, pos_b)


def _sc_collect(ysorted, pos_flat):
    n_pick = pos_flat.shape[0]
    ch = SC_COLLECT_ROWS
    per_worker = n_pick // SC_WORKERS
    assert n_pick % SC_WORKERS == 0 and per_worker % ch == 0

    @functools.partial(
        pl.kernel, mesh=_sc_mesh(),
        out_type=jax.ShapeDtypeStruct((n_pick, SLAB_ROWS, LANES), F32),
        scratch_types=[pltpu.VMEM((ch,), jnp.int32), pltpu.VMEM((ch, SLAB_ROWS, LANES), F32),
                       pltpu.SemaphoreType.DMA])
    def pull(ys_hbm, pos_hbm, out_hbm, idx, rows, sem):
        base = _sc_worker() * per_worker

        @pl.loop(0, per_worker // ch)
        def _(j):
            off = pl.multiple_of(base + j * ch, SUBLANES)
            pltpu.sync_copy(pos_hbm.at[pl.ds(off, ch)], idx)
            pltpu.async_copy(ys_hbm.at[idx], rows, sem).wait()
            pltpu.sync_copy(rows, out_hbm.at[pl.ds(off, ch)])

    return pull(ysorted, pos_flat)


def _moe_ffn_body(te_ref, nused_ref, x_ref, wg_ref, wu_ref, wd_ref, y_ref, wgb, wub, wdb):
    i = pl.program_id(0)

    @pl.when(i >= nused_ref[0])
    def _unused_tile():
        y_ref[...] = jnp.zeros_like(y_ref)

    @pl.when(i < nused_ref[0])
    def _tile():
        @pl.when((i == 0) | (te_ref[i] != te_ref[jnp.maximum(i - 1, 0)]))
        def _cast_weights():
            wgb[...] = wg_ref[0].astype(BF16)
            wub[...] = wu_ref[0].astype(BF16)
            wdb[...] = wd_ref[0].astype(BF16)

        x = jnp.concatenate([x_ref[:, j, :] for j in range(SLAB_ROWS)], axis=-1).astype(BF16)
        gate = _dot(x, wgb[...])
        hmid = (gate * jax.nn.sigmoid(gate)) * _dot(x, wub[...])
        y = _dot(hmid.astype(BF16), wdb[...])
        for j in range(SLAB_ROWS):
            y_ref[:, j, :] = y[:, j * LANES:(j + 1) * LANES]


def _moe_ffn(tile_expert, n_used, xsorted, w_gate, w_up, w_down):
    n_tiles = tile_expert.shape[0]
    wspec = lambda s: pl.BlockSpec((1,) + s, lambda i, te, nu: (te[i], 0, 0))
    slab = lambda imap: pl.BlockSpec((MOE_TILE, SLAB_ROWS, LANES), imap)
    return pl.pallas_call(
        _moe_ffn_body,
        grid_spec=pltpu.PrefetchScalarGridSpec(
            num_scalar_prefetch=2,
            grid=(n_tiles,),
            in_specs=[slab(lambda i, te, nu: (jnp.clip(i, 0, jnp.maximum(nu[0] - 1, 0)), 0, 0)),
                      wspec((D_MODEL, MOE_D_FF)), wspec((D_MODEL, MOE_D_FF)), wspec((MOE_D_FF, D_MODEL))],
            out_specs=slab(lambda i, te, nu: (i, 0, 0)),
            scratch_shapes=[pltpu.VMEM((D_MODEL, MOE_D_FF), BF16), pltpu.VMEM((D_MODEL, MOE_D_FF), BF16),
                            pltpu.VMEM((MOE_D_FF, D_MODEL), BF16)]),
        out_shape=jax.ShapeDtypeStruct((n_tiles * MOE_TILE, SLAB_ROWS, LANES), F32),
        compiler_params=pltpu.CompilerParams(dimension_semantics=("arbitrary",), vmem_limit_bytes=VMEM_LIMIT),
        name="moe_ffn",
    )(tile_expert, n_used, xsorted, w_gate, w_up, w_down)


def _combine_body(x1_ref, rt_ref, yt_ref, nf_ref, out_ref):
    rt = rt_ref[...]
    x1 = x1_ref[...]
    x2 = jnp.concatenate(
        [x1[:, j * LANES:(j + 1) * LANES] + rt[:, 2:3] * yt_ref[:, 0, j, :] + rt[:, 3:4] * yt_ref[:, 1, j, :]
         for j in range(SLAB_ROWS)], axis=-1)
    out_ref[...] = _rms(x2, nf_ref[...])


def _combine(x1, rt, y_picks, nf, tm, rows, row_block_offset):
    row = lambda w: pl.BlockSpec((tm, w), lambda i: (i + row_block_offset, 0))
    return pl.pallas_call(
        _combine_body,
        grid=(rows // tm,),
        in_specs=[row(D_MODEL), row(LANES),
                  pl.BlockSpec((tm, 2, SLAB_ROWS, LANES), lambda i: (i + row_block_offset, 0, 0, 0)),
                  pl.BlockSpec((1, D_MODEL), lambda i: (0, 0))],
        out_specs=pl.BlockSpec((tm, D_MODEL), lambda i: (i, 0)),
        out_shape=jax.ShapeDtypeStruct((rows, D_MODEL), F32),
        compiler_params=pltpu.CompilerParams(dimension_semantics=("parallel",), vmem_limit_bytes=VMEM_LIMIT),
        name="moe_combine",
    )(x1, rt, y_picks, nf)


def _route_tables(counts, eid, rank, n_tiles):
    experts = jnp.arange(MOE_EXPERTS, dtype=jnp.int32)
    tiles_per = (counts + MOE_TILE - 1) // MOE_TILE
    tile_end = jnp.cumsum(tiles_per)
    pstart = (tile_end - tiles_per) * MOE_TILE
    pos = jnp.sum(jnp.where(eid[..., None] == experts, pstart, 0), axis=-1) + rank
    n_used = tile_end[-1]
    tiles = jnp.arange(n_tiles, dtype=jnp.int32)
    tile_expert = jnp.sum((tile_end[None, :] <= jnp.minimum(tiles, n_used - 1)[:, None]).astype(jnp.int32), axis=1)
    return pos, tile_expert, n_used.reshape(1).astype(jnp.int32)


def _s5_tables(a_re, a_im, log_dt, b_re, b_im, c_re, c_im):
    dt = jnp.exp(log_dt)[:, None]
    mag = jnp.exp(a_re * dt)
    ab_re = mag * jnp.cos(a_im * dt)
    ab_im = mag * jnp.sin(a_im * dt)
    den = a_re * a_re + a_im * a_im
    nr = ab_re - 1.0
    q_re = (nr * a_re + ab_im * a_im) / den
    q_im = (ab_im * a_re - nr * a_im) / den
    bb_re = q_re[..., None] * b_re - q_im[..., None] * b_im
    bb_im = q_re[..., None] * b_im + q_im[..., None] * b_re
    eye = jnp.eye(16, dtype=F32)
    nblk = S5_GROUPS // 16

    def in_map(bb):
        w = jnp.einsum("jgpc,gh->jgchp", bb.reshape(nblk, 16, S5_STATE, S5_GROUP_CH), eye)
        return w.reshape(nblk, 16 * S5_GROUP_CH, 16 * S5_STATE)

    def out_map(cc):
        w = jnp.einsum("jgcp,gh->jgphc", cc.reshape(nblk, 16, S5_GROUP_CH, S5_STATE), eye)
        return w.reshape(nblk, 16 * S5_STATE, 16 * S5_GROUP_CH)

    wb = jnp.concatenate([in_map(bb_re), in_map(bb_im)], axis=-1).astype(BF16)
    return (wb, ab_re.reshape(1, S5_LANES), ab_im.reshape(1, S5_LANES),
            out_map(c_re).astype(BF16), out_map(-c_im).astype(BF16))


def kernel(x_prompt, x_sample, state_ssd_conv, state_ssd_ssm, state_s5_re, state_s5_im, meta_tokens, norm_mix, w_in, conv_w, conv_b, dt_bias, a_log, d_ssd, ssd_norm, s5_a_re, s5_a_im, s5_log_dt, s5_b_re, s5_b_im, s5_c_re, s5_c_im, s5_d, w_glu, b_glu, s5_norm, w_out, norm_ffn, router_coarse_w, router_coarse_b, router_fine_w, router_fine_b, w_gate, w_up, w_down, norm_final):
    bp, seq, _ = x_prompt.shape
    bs = x_sample.shape[0]
    n_prompt = bp * seq
    n_tok = n_prompt + bs
    row2 = lambda v: v.reshape(1, -1)
    pad_heads = lambda v: jnp.pad(v, (0, LANES - SSD_HEADS)).reshape(1, LANES)

    w = w_in[0]
    o1, o2, o3 = SSD_WIDTH, SSD_WIDTH + SSD_CONV_DIM, SSD_WIDTH + SSD_CONV_DIM + SSD_HEADS
    wz, wx, wu = w[:, :o1].astype(BF16), w[:, o1:o2].astype(BF16), w[:, o3:].astype(BF16)
    wdt = jnp.pad(w[:, o2:o3], ((0, 0), (0, LANES - SSD_HEADS))).astype(BF16)
    g_mix = row2(norm_mix[0])
    cw, cb = conv_w[0], row2(conv_b[0])
    dtb, alog = pad_heads(dt_bias[0]), pad_heads(a_log[0])
    dexp = row2(jnp.repeat(d_ssd[0], SSD_HEAD_DIM))
    snrm = row2(ssd_norm[0])
    eexp = (jnp.arange(LANES)[:, None] == (jnp.arange(SSD_WIDTH) // SSD_HEAD_DIM)[None, :]).astype(BF16)
    wb5, ab_re, ab_im, wcr, wci = _s5_tables(s5_a_re[0], s5_a_im[0], s5_log_dt[0], s5_b_re[0], s5_b_im[0],
                                             s5_c_re[0], s5_c_im[0])
    d5, wglu, bglu, nrm5 = row2(s5_d[0]), w_glu[0].astype(BF16), row2(b_glu[0]), row2(s5_norm[0])
    wo_a, wo_b = w_out[0][:SSD_WIDTH].astype(BF16), w_out[0][SSD_WIDTH:].astype(BF16)
    w_r = jnp.concatenate([router_coarse_w[0], router_fine_w[0].transpose(1, 0, 2).reshape(D_MODEL, MOE_EXPERTS)], axis=1)
    w_r = jnp.pad(w_r, ((0, 0), (0, LANES - w_r.shape[1])))
    wrh = w_r.astype(BF16)
    wrl = (w_r - wrh.astype(F32)).astype(BF16)
    b_r = jnp.concatenate([router_coarse_b[0], router_fine_b[0].reshape(-1)])
    b_r = jnp.pad(b_r, (0, LANES - b_r.shape[0])).reshape(1, LANES)

    zp, xbcp, dtp, up = _in_proj(x_prompt.reshape(n_prompt, D_MODEL), g_mix, wz, wx, wdt, wu, TOK_TILE, BF16, F32)
    xsm = jnp.concatenate([x_sample.reshape(bs, D_MODEL), meta_tokens], axis=0)
    zs, xbcs, dts, us = _in_proj(xsm, g_mix, wz, wx, wdt, wu, xsm.shape[0], F32, F32)

    front = SSD_CHUNK - N_META
    padf = lambda a: jnp.pad(a[bs:], ((front, 0), (0, 0)))[None]
    gw = SSD_HPG * SSD_HEAD_DIM
    ssd_consts = (cw, cb, dtb, alog, dexp, snrm, eexp)
    _, ctail_m, _, ht_m = _ssd_chunked(
        padf(xbcs), padf(dts), jnp.zeros((1, SSD_CHUNK, SSD_WIDTH), F32),
        jnp.zeros((1, SUBLANES, SSD_CONV_DIM), F32), jnp.zeros((1, SSD_GROUPS, SSD_STATE, gw), F32),
        *ssd_consts, mask_rows=front)
    y_ssd_p, ctail_p, ssm_p, _ = _ssd_chunked(
        xbcp.reshape(bp, seq, SSD_CONV_DIM), dtp.reshape(bp, seq, LANES), zp.reshape(bp, seq, SSD_WIDTH),
        ctail_m, ht_m, *ssd_consts, mask_rows=0)

    abr8, abi8 = jnp.broadcast_to(ab_re, (bp, S5_LANES)), jnp.broadcast_to(ab_im, (bp, S5_LANES))
    um8 = jnp.repeat(us[bs:], bp, axis=0).astype(BF16)
    y_s5_p, s5re_p, s5im_p = _s5_seq(up.reshape(bp, seq, S5_WIDTH), um8, wb5, abr8, abi8,
                                     wcr, wci, d5, wglu, bglu, nrm5)

    cst = state_ssd_conv[0]
    xt_s, dt_s, dec_s, bc, xs_s = _ssd_step_prep(xbcs[:bs], cst[:, 0], cst[:, 1], cst[:, 2], dts[:bs],
                                                 cw, cb, dtb, alog)
    ssm_s, y_core = _ssd_step(dt_s[:, :SSD_HEADS].reshape(-1), dec_s[:, :SSD_HEADS].reshape(-1),
                              state_ssd_ssm[0], xt_s, bc)
    y_ssd_s, y_s5_s, s5re_s, s5im_s = _sample_post(
        y_core, xs_s, zs[:bs], dexp, snrm, us[:bs], state_s5_re[0].reshape(bs, S5_LANES),
        state_s5_im[0].reshape(bs, S5_LANES), wb5, ab_re, ab_im, wcr, wci, d5, wglu, bglu, nrm5)

    route_consts = (wo_a, wo_b, row2(norm_ffn[0]), wrh, wrl, b_r)
    x1, xn, rt, counts = _mix_route(
        (x_prompt.reshape(n_prompt, D_MODEL), y_ssd_p.reshape(n_prompt, SSD_WIDTH), y_s5_p.reshape(n_prompt, S5_WIDTH)),
        (x_sample.reshape(bs, D_MODEL), y_ssd_s, y_s5_s), route_consts, TOK_TILE)

    n_tiles = -(-2 * n_tok // MOE_TILE) + MOE_EXPERTS
    eid = jnp.clip(rt[:, 0:2].astype(jnp.int32), 0, MOE_EXPERTS - 1)
    pos, tile_expert, n_used = _route_tables(counts[0, :MOE_EXPERTS].astype(jnp.int32), eid,
                                             rt[:, 4:6].astype(jnp.int32), n_tiles)
    xsorted = _sc_dispatch(xn, pos[:, 0], pos[:, 1], n_tiles * MOE_TILE)
    ysorted = _moe_ffn(tile_expert, n_used, xsorted, w_gate[0], w_up[0], w_down[0])
    y_picks = _sc_collect(ysorted, pos.reshape(-1)).reshape(n_tok, 2, SLAB_ROWS, LANES)
    nfin = row2(norm_final)
    y_p = _combine(x1, rt, y_picks, nfin, MOE_TILE, n_prompt, 0)
    y_s = _combine(x1, rt, y_picks, nfin, bs, bs, n_prompt // bs)

    s5_state = lambda a, b: a.reshape(1, b, S5_GROUPS, S5_STATE)
    new_conv_s = jnp.stack([cst[:, 1], cst[:, 2], xbcs[:bs]], axis=1)[None]
    return (y_p.reshape(bp, seq, D_MODEL), y_s.reshape(bs, 1, D_MODEL),
            ctail_p[:, SUBLANES - (SSD_CONV - 1):][None], ssm_p[None], s5_state(s5re_p, bp), s5_state(s5im_p, bp),
            new_conv_s, ssm_s[None], s5_state(s5re_s, bs), s5_state(s5im_s, bs))
```

```python
import functools

import jax
import jax.numpy as jnp
from jax import lax
from jax.experimental import pallas as pl
from jax.experimental.pallas import tpu as pltpu
from jax.experimental.pallas import tpu_sc as plsc

F32, BF16 = jnp.float32, jnp.bfloat16

D_MODEL = 1024
N_META = 16
SSD_WIDTH = 1024
SSD_HEAD_DIM = 64
SSD_HEADS = 16
SSD_GROUPS = 2
SSD_HPG = SSD_HEADS // SSD_GROUPS
SSD_STATE = 128
SSD_CONV = 4
SSD_CHUNK = 128
SSD_CONV_DIM = SSD_WIDTH + 2 * SSD_GROUPS * SSD_STATE
S5_WIDTH = 1024
S5_GROUP_CH = 16
S5_GROUPS = 64
S5_STATE = 64
S5_LANES = S5_GROUPS * S5_STATE
MOE_GROUPS = 4
MOE_EPG = 8
MOE_EXPERTS = MOE_GROUPS * MOE_EPG
MOE_D_FF = 512
EPS = 1e-6

LANES = 128
SUBLANES = 8
VMEM_LIMIT = 56 * 1024 * 1024

S5_TIME_TILE = 32
S5_SCAN_LANES = 512
MOE_TILE = 256
SLAB_ROWS = D_MODEL // LANES
SC_CORES = 2
SC_SUBCORES = 16
SC_WORKERS = SC_CORES * SC_SUBCORES
SC_DISPATCH_ROWS = 64
SC_COLLECT_ROWS = 24
TOK_TILE = 512


def _dot(a, b):
    return jnp.dot(a, b, preferred_element_type=F32)


def _rms(x, g):
    return x * lax.rsqrt(jnp.mean(x * x, axis=-1, keepdims=True) + EPS) * g


def _softplus(x):
    return jnp.maximum(x, 0.0) + jnp.log1p(jnp.exp(-jnp.abs(x)))


def _split3(x):
    hi = x.astype(BF16)
    r = x - hi.astype(F32)
    mid = r.astype(BF16)
    lo = (r - mid.astype(F32)).astype(BF16)
    return hi, mid, lo


def _dot3(x, w):
    hi, mid, lo = _split3(x)
    return _dot(hi, w) + _dot(mid, w) + _dot(lo, w)


def _dot3_left(w, x):
    hi, mid, lo = _split3(x)
    return _dot(w, hi) + _dot(w, mid) + _dot(w, lo)


def _full_spec(a):
    nd = a.ndim
    return pl.BlockSpec(a.shape, lambda *_: (0,) * nd)


def _in_proj_body(x_ref, g_ref, wz_ref, wx_ref, wdt_ref, wu_ref, z_ref, xbc_ref, dt_ref, u_ref):
    xb = _rms(x_ref[...], g_ref[...]).astype(BF16)
    z_ref[...] = _dot(xb, wz_ref[...]).astype(z_ref.dtype)
    xbc_ref[...] = _dot(xb, wx_ref[...]).astype(xbc_ref.dtype)
    dt_ref[...] = _dot(xb, wdt_ref[...])
    u_ref[...] = _dot(xb, wu_ref[...]).astype(u_ref.dtype)


def _in_proj(x2d, g, wz, wx, wdt, wu, tm, act_dtype, u_dtype):
    rows = x2d.shape[0]
    row = lambda w: pl.BlockSpec((tm, w), lambda i: (i, 0))
    return pl.pallas_call(
        _in_proj_body,
        grid=(rows // tm,),
        in_specs=[row(D_MODEL), _full_spec(g), _full_spec(wz), _full_spec(wx), _full_spec(wdt), _full_spec(wu)],
        out_specs=[row(SSD_WIDTH), row(SSD_CONV_DIM), row(LANES), row(S5_WIDTH)],
        out_shape=[jax.ShapeDtypeStruct((rows, SSD_WIDTH), act_dtype),
                   jax.ShapeDtypeStruct((rows, SSD_CONV_DIM), act_dtype),
                   jax.ShapeDtypeStruct((rows, LANES), F32),
                   jax.ShapeDtypeStruct((rows, S5_WIDTH), u_dtype)],
        compiler_params=pltpu.CompilerParams(dimension_semantics=("parallel",), vmem_limit_bytes=VMEM_LIMIT),
        name="in_proj",
    )(x2d, g, wz, wx, wdt, wu)


def _ssd_body(mask_rows, xbc_ref, dt_ref, z_ref, cinit_ref, hinit_ref, cw_ref, cb_ref, dtb_ref, alog_ref,
              dexp_ref, nrm_ref, eexp_ref, y_ref, ctail_ref, st_ref, hto_ref, xwin, hT):
    c = pl.program_id(1)
    L = SSD_CHUNK

    @pl.when(c == 0)
    def _init():
        xwin[0:SUBLANES, :] = cinit_ref[0]
        hT[...] = hinit_ref[0]

    xwin[SUBLANES:SUBLANES + L, :] = xbc_ref[0].astype(F32)
    acc = cb_ref[...]
    for k in range(SSD_CONV):
        off = SUBLANES - (SSD_CONV - 1) + k
        acc = acc + xwin[off:off + L, :] * cw_ref[k:k + 1, :]
    tail = xwin[L:L + SUBLANES, :]
    xwin[0:SUBLANES, :] = tail
    ctail_ref[0] = tail

    xact = acc * jax.nn.sigmoid(acc)
    dt = _softplus(dt_ref[0] + dtb_ref[...])
    if mask_rows:
        valid = lax.broadcasted_iota(jnp.int32, (L, 1), 0) >= mask_rows
        xact = jnp.where(valid, xact, 0.0)
        dt = jnp.where(valid, dt, 0.0)

    a_neg = -jnp.exp(alog_ref[...])
    dA = dt * a_neg
    row_i = lax.broadcasted_iota(jnp.int32, (L, L), 0)
    col_i = lax.broadcasted_iota(jnp.int32, (L, L), 1)
    causal = row_i >= col_i
    tril = causal.astype(BF16)
    cs = _dot3_left(tril, dA)
    csT = cs.T
    dtT = dt.T
    ecs = jnp.exp(cs)
    wdec = jnp.exp(cs[L - 1:L, :] - cs) * dt
    eexp = eexp_ref[...]
    ecs_e = _dot3(ecs, eexp)
    wdec_e = _dot3(wdec, eexp)
    lane = lax.broadcasted_iota(jnp.int32, (L, LANES), 1)
    first_half = lane < SSD_HEAD_DIM

    gw = SSD_HPG * SSD_HEAD_DIM
    y_groups = []
    for g in range(SSD_GROUPS):
        b_g = xact[:, SSD_WIDTH + g * SSD_STATE: SSD_WIDTH + (g + 1) * SSD_STATE]
        c_g = xact[:, SSD_WIDTH + (SSD_GROUPS + g) * SSD_STATE: SSD_WIDTH + (SSD_GROUPS + g + 1) * SSD_STATE]
        b_b = b_g.astype(BF16)
        c_b = c_g.astype(BF16)
        cb = lax.dot_general(c_b, b_b, (((1,), (1,)), ((), ())), preferred_element_type=F32)
        xs_g = xact[:, g * gw:(g + 1) * gw]
        h_prev = hT[g]
        y_off = _dot(c_b, h_prev.astype(BF16)) * ecs_e[:, g * gw:(g + 1) * gw]
        xdec = (xs_g * wdec_e[:, g * gw:(g + 1) * gw]).astype(BF16)
        hT[g] = h_prev * ecs_e[L - 1:L, g * gw:(g + 1) * gw] + _dot(b_g.T.astype(BF16), xdec)
        pieces = []
        for j in range(SSD_HPG // 2):
            xs_pair = xs_g[:, j * LANES:(j + 1) * LANES]
            halves = (jnp.where(first_half, xs_pair, 0.0).astype(BF16),
                      jnp.where(first_half, 0.0, xs_pair).astype(BF16))
            yd = None
            for t in range(2):
                h = g * SSD_HPG + 2 * j + t
                seg = cs[:, h:h + 1] - csT[h:h + 1, :]
                lmat = jnp.exp(jnp.where(causal, seg, -jnp.inf))
                m = (cb * lmat * dtT[h:h + 1, :]).astype(BF16)
                part = _dot(m, halves[t])
                yd = part if yd is None else yd + part
            pieces.append(yd)
        y_groups.append(jnp.concatenate(pieces, axis=-1) + y_off + dexp_ref[:, g * gw:(g + 1) * gw] * xs_g)
    y = jnp.concatenate(y_groups, axis=-1)
    z = z_ref[0].astype(F32)
    y_ref[0] = _rms(y * (z * jax.nn.sigmoid(z)), nrm_ref[...]).astype(y_ref.dtype)

    @pl.when(c == pl.num_programs(1) - 1)
    def _emit():
        hto_ref[0] = hT[...]
        for g in range(SSD_GROUPS):
            t = hT[g].T
            for k in range(SSD_HPG):
                st_ref[0, g * SSD_HPG + k] = t[k * SSD_HEAD_DIM:(k + 1) * SSD_HEAD_DIM, :]


def _ssd_chunked(xbc, dt, z, cinit, hinit, cw, cb, dtb, alog, dexp, nrm, eexp, mask_rows):
    bsz, seq, _ = xbc.shape
    nc = seq // SSD_CHUNK
    gw = SSD_HPG * SSD_HEAD_DIM
    blk = lambda w: pl.BlockSpec((1, SSD_CHUNK, w), lambda b, c: (b, c, 0))
    return pl.pallas_call(
        functools.partial(_ssd_body, mask_rows),
        grid=(bsz, nc),
        in_specs=[blk(SSD_CONV_DIM), blk(LANES), blk(SSD_WIDTH),
                  pl.BlockSpec((1, SUBLANES, SSD_CONV_DIM), lambda b, c: (0, 0, 0)),
                  pl.BlockSpec((1, SSD_GROUPS, SSD_STATE, gw), lambda b, c: (0, 0, 0, 0)),
                  _full_spec(cw), _full_spec(cb), _full_spec(dtb), _full_spec(alog),
                  _full_spec(dexp), _full_spec(nrm), _full_spec(eexp)],
        out_specs=[blk(SSD_WIDTH),
                   pl.BlockSpec((1, SUBLANES, SSD_CONV_DIM), lambda b, c: (b, 0, 0)),
                   pl.BlockSpec((1, SSD_HEADS, SSD_HEAD_DIM, SSD_STATE), lambda b, c: (b, 0, 0, 0)),
                   pl.BlockSpec((1, SSD_GROUPS, SSD_STATE, gw), lambda b, c: (b, 0, 0, 0))],
        out_shape=[jax.ShapeDtypeStruct((bsz, seq, SSD_WIDTH), BF16),
                   jax.ShapeDtypeStruct((bsz, SUBLANES, SSD_CONV_DIM), F32),
                   jax.ShapeDtypeStruct((bsz, SSD_HEADS, SSD_HEAD_DIM, SSD_STATE), F32),
                   jax.ShapeDtypeStruct((bsz, SSD_GROUPS, SSD_STATE, gw), F32)],
        scratch_shapes=[pltpu.VMEM((SUBLANES + SSD_CHUNK, SSD_CONV_DIM), F32),
                        pltpu.VMEM((SSD_GROUPS, SSD_STATE, gw), F32)],
        compiler_params=pltpu.CompilerParams(dimension_semantics=("parallel", "arbitrary"),
                                             vmem_limit_bytes=VMEM_LIMIT),
        name="ssd_chunked",
    )(xbc, dt, z, cinit, hinit, cw, cb, dtb, alog, dexp, nrm, eexp)


def _ssd_step_prep_body(xbc_ref, c0_ref, c1_ref, c2_ref, dt_ref, cw_ref, cb_ref, dtb_ref, alog_ref,
                        xt_ref, dt_out_ref, dec_ref, bc_ref, xs_ref):
    acc = cb_ref[...]
    for k, r in enumerate((c0_ref, c1_ref, c2_ref, xbc_ref)):
        acc = acc + r[...] * cw_ref[k:k + 1, :]
    xact = acc * jax.nn.sigmoid(acc)
    xs = xact[:, :SSD_WIDTH]
    dt = _softplus(dt_ref[...] + dtb_ref[...])
    dt_out_ref[...] = dt
    dec_ref[...] = jnp.exp(dt * -jnp.exp(alog_ref[...]))
    bc_ref[...] = xact[:, SSD_WIDTH:]
    xs_ref[...] = xs
    xt_ref[...] = xs.T.astype(xt_ref.dtype)


def _ssd_step_prep(xbc, c0, c1, c2, dt, cw, cb, dtb, alog):
    n = xbc.shape[0]
    args = (xbc, c0, c1, c2, dt, cw, cb, dtb, alog)
    spec = lambda r, w: pl.BlockSpec((r, w), lambda: (0, 0))
    return pl.pallas_call(
        _ssd_step_prep_body,
        in_specs=[_full_spec(a) for a in args],
        out_specs=[spec(SSD_WIDTH, n), spec(n, LANES), spec(n, LANES), spec(n, 2 * SSD_GROUPS * SSD_STATE),
                   spec(n, SSD_WIDTH)],
        out_shape=[jax.ShapeDtypeStruct((SSD_WIDTH, n), BF16), jax.ShapeDtypeStruct((n, LANES), F32),
                   jax.ShapeDtypeStruct((n, LANES), F32),
                   jax.ShapeDtypeStruct((n, 2 * SSD_GROUPS * SSD_STATE), F32),
                   jax.ShapeDtypeStruct((n, SSD_WIDTH), F32)],
        compiler_params=pltpu.CompilerParams(vmem_limit_bytes=VMEM_LIMIT),
        name="ssd_step_prep",
    )(*args)


def _ssd_step_body(dt_ref, dec_ref, st_ref, xt_ref, bc_ref, so_ref, y_ref):
    n = xt_ref.shape[1]
    gw = SSD_HPG * SSD_HEAD_DIM
    blk = pl.program_id(0)
    seq_id = lax.broadcasted_iota(jnp.int32, (n, SSD_STATE), 0)
    sub_id = lax.broadcasted_iota(jnp.int32, (SUBLANES, gw), 0)
    base = pl.multiple_of(blk * SUBLANES, SUBLANES)
    y_acc = [jnp.zeros((SUBLANES, gw), F32) for _ in range(SSD_GROUPS)]
    for i in range(SUBLANES):
        s = blk * SUBLANES + i
        for g in range(SSD_GROUPS):
            b_all = bc_ref[:, g * SSD_STATE:(g + 1) * SSD_STATE]
            rhs = jnp.where(seq_id == s, b_all, 0.0).astype(BF16)
            outer = _dot(xt_ref[g * gw:(g + 1) * gw, :], rhs)
            news = []
            for k in range(SSD_HPG):
                h = g * SSD_HPG + k
                new = (dec_ref[s * SSD_HEADS + h] * st_ref[i, h]
                       + dt_ref[s * SSD_HEADS + h] * outer[k * SSD_HEAD_DIM:(k + 1) * SSD_HEAD_DIM, :])
                so_ref[i, h] = new
                news.append(new)
            new_g = jnp.concatenate(news, axis=0).astype(BF16)
            c_lo = (SSD_GROUPS + g) * SSD_STATE
            c_blk = bc_ref[pl.ds(base, SUBLANES), c_lo:c_lo + SSD_STATE].astype(BF16)
            r = lax.dot_general(c_blk, new_g, (((1,), (1,)), ((), ())), preferred_element_type=F32)
            y_acc[g] = y_acc[g] + jnp.where(sub_id == i, r, 0.0)
    y_ref[...] = jnp.concatenate(y_acc, axis=-1)


def _ssd_step(dt_flat, dec_flat, state, xt, bc):
    n = state.shape[0]
    st_spec = pl.BlockSpec((SUBLANES, SSD_HEADS, SSD_HEAD_DIM, SSD_STATE), lambda i, *_: (i, 0, 0, 0))
    return pl.pallas_call(
        _ssd_step_body,
        grid_spec=pltpu.PrefetchScalarGridSpec(
            num_scalar_prefetch=2,
            grid=(n // SUBLANES,),
            in_specs=[st_spec, pl.BlockSpec(xt.shape, lambda i, *_: (0, 0)),
                      pl.BlockSpec(bc.shape, lambda i, *_: (0, 0))],
            out_specs=[st_spec, pl.BlockSpec((SUBLANES, SSD_WIDTH), lambda i, *_: (i, 0))]),
        out_shape=[jax.ShapeDtypeStruct(state.shape, F32), jax.ShapeDtypeStruct((n, SSD_WIDTH), F32)],
        compiler_params=pltpu.CompilerParams(dimension_semantics=("parallel",), vmem_limit_bytes=VMEM_LIMIT),
        name="ssd_step",
    )(dt_flat, dec_flat, state, xt, bc)


def _s5_project_in(u_b16, wb_ref, store):
    kw = 16 * S5_GROUP_CH
    nw = 16 * S5_STATE
    for j in range(S5_WIDTH // kw):
        r = _dot(u_b16[:, j * kw:(j + 1) * kw], wb_ref[j])
        store(j, r[:, :nw], r[:, nw:])


def _s5_tail(hre_of, him_of, u_f32, wcr_ref, wci_ref, d_ref, wglu_ref, bglu_ref, nrm_ref):
    cols = []
    for j in range(wcr_ref.shape[0]):
        cols.append(_dot(hre_of(j).astype(BF16), wcr_ref[j]) + _dot(him_of(j).astype(BF16), wci_ref[j]))
    y = jnp.concatenate(cols, axis=-1) + d_ref[...] * u_f32
    y = jax.nn.gelu(y)
    y = y * jax.nn.sigmoid(_dot(y.astype(BF16), wglu_ref[...]) + bglu_ref[...])
    return _rms(y, nrm_ref[...])


def _s5_seq_body(u_hbm, um_ref, wb_ref, abr_ref, abi_ref, wcr_ref, wci_ref, d_ref, wglu_ref, bglu_ref, nrm_ref,
                 y_hbm, sre_ref, sim_ref, ubuf, ybuf, bu, h, in_sems, out_sems):
    j = pl.program_id(0)
    last = pl.num_programs(0) - 1
    lc, bsz = ubuf.shape[1], ubuf.shape[2]
    rows = lc * bsz
    nw = 16 * S5_STATE

    def in_copy(step, b):
        return pltpu.make_async_copy(u_hbm.at[b, pl.ds(step * lc, lc), :], ubuf.at[step % 2, :, b, :],
                                     in_sems.at[step % 2, b])

    def out_copy(step, b):
        return pltpu.make_async_copy(ybuf.at[step % 2, :, b, :], y_hbm.at[b, pl.ds(step * lc, lc), :],
                                     out_sems.at[step % 2, b])

    def project_in(u_b16, nrows):
        def store(jj, re, im):
            bu[0:nrows, jj * nw:(jj + 1) * nw] = re
            bu[0:nrows, S5_LANES + jj * nw:S5_LANES + (jj + 1) * nw] = im
        _s5_project_in(u_b16, wb_ref, store)

    def scan(nsteps):
        for k in range(S5_LANES // S5_SCAN_LANES):
            sl_r = pl.ds(k * S5_SCAN_LANES, S5_SCAN_LANES)
            sl_i = pl.ds(S5_LANES + k * S5_SCAN_LANES, S5_SCAN_LANES)
            ar = abr_ref[:, sl_r]
            ai = abi_ref[:, sl_r]

            def step(l, carry):
                hr, hi = carry
                slab = pl.ds(pl.multiple_of(l * bsz, bsz), bsz)
                nr = ar * hr - ai * hi + bu[slab, sl_r]
                ni = ar * hi + ai * hr + bu[slab, sl_i]
                bu[slab, sl_r] = nr
                bu[slab, sl_i] = ni
                return nr, ni

            hr, hi = lax.fori_loop(0, nsteps, step, (h[:, sl_r], h[:, sl_i]))
            h[:, sl_r] = hr
            h[:, sl_i] = hi

    @pl.when(j == 0)
    def _first():
        for b in range(bsz):
            in_copy(0, b).start()
        h[...] = jnp.zeros_like(h)
        project_in(um_ref[...], N_META * bsz)
        scan(N_META)

    @pl.when(j < last)
    def _prefetch():
        for b in range(bsz):
            in_copy(j + 1, b).start()

    for b in range(bsz):
        in_copy(j, b).wait()
    u2 = ubuf[j % 2].reshape(rows, S5_WIDTH)
    project_in(u2.astype(BF16), rows)
    scan(lc)
    y = _s5_tail(lambda jj: bu[:, jj * nw:(jj + 1) * nw], lambda jj: bu[:, S5_LANES + jj * nw:S5_LANES + (jj + 1) * nw],
                 u2, wcr_ref, wci_ref, d_ref, wglu_ref, bglu_ref, nrm_ref)
    ybuf[j % 2] = y.reshape(lc, bsz, S5_WIDTH)
    for b in range(bsz):
        out_copy(j, b).start()

    @pl.when(j > 0)
    def _wait_previous_out():
        for b in range(bsz):
            out_copy(j - 1, b).wait()

    @pl.when(j == last)
    def _emit():
        for b in range(bsz):
            out_copy(j, b).wait()
        sre_ref[...] = h[:, 0:S5_LANES]
        sim_ref[...] = h[:, S5_LANES:]


def _s5_seq(u, um, wb, abr, abi, wcr, wci, d, wglu, bglu, nrm):
    bsz, seq, _ = u.shape
    lc = S5_TIME_TILE
    consts = (um, wb, abr, abi, wcr, wci, d, wglu, bglu, nrm)
    st = pl.BlockSpec((bsz, S5_LANES), lambda j: (0, 0))
    return pl.pallas_call(
        _s5_seq_body,
        grid=(seq // lc,),
        in_specs=[pl.BlockSpec(memory_space=pl.ANY)] + [_full_spec(a) for a in consts],
        out_specs=[pl.BlockSpec(memory_space=pl.ANY), st, st],
        out_shape=[jax.ShapeDtypeStruct((bsz, seq, S5_WIDTH), F32),
                   jax.ShapeDtypeStruct((bsz, S5_LANES), F32), jax.ShapeDtypeStruct((bsz, S5_LANES), F32)],
        scratch_shapes=[pltpu.VMEM((2, lc, bsz, S5_WIDTH), F32), pltpu.VMEM((2, lc, bsz, S5_WIDTH), F32),
                        pltpu.VMEM((lc * bsz, 2 * S5_LANES), F32), pltpu.VMEM((bsz, 2 * S5_LANES), F32),
                        pltpu.SemaphoreType.DMA((2, bsz)), pltpu.SemaphoreType.DMA((2, bsz))],
        compiler_params=pltpu.CompilerParams(dimension_semantics=("arbitrary",), vmem_limit_bytes=VMEM_LIMIT),
        name="s5_seq",
    )(u, *consts)


def _sample_post_body(yc_ref, xs_ref, z_ref, dexp_ref, snrm_ref, u_ref, hr_ref, hi_ref, wb_ref, abr_ref, abi_ref,
                      wcr_ref, wci_ref, d_ref, wglu_ref, bglu_ref, nrm_ref,
                      yssd_ref, ys5_ref, nre_ref, nim_ref):
    z = z_ref[...]
    y = yc_ref[...] + dexp_ref[...] * xs_ref[...]
    yssd_ref[...] = _rms(y * (z * jax.nn.sigmoid(z)), snrm_ref[...]).astype(yssd_ref.dtype)

    u = u_ref[...]
    nw = 16 * S5_STATE
    ar, ai = abr_ref[...], abi_ref[...]

    def store(jj, re, im):
        sl = slice(jj * nw, (jj + 1) * nw)
        h0r, h0i = hr_ref[:, sl], hi_ref[:, sl]
        nre_ref[:, sl] = ar[:, sl] * h0r - ai[:, sl] * h0i + re
        nim_ref[:, sl] = ar[:, sl] * h0i + ai[:, sl] * h0r + im

    _s5_project_in(u.astype(BF16), wb_ref, store)
    slab = lambda ref: (lambda jj: ref[:, jj * nw:(jj + 1) * nw])
    y5 = _s5_tail(slab(nre_ref), slab(nim_ref), u, wcr_ref, wci_ref, d_ref, wglu_ref, bglu_ref, nrm_ref)
    ys5_ref[...] = y5.astype(ys5_ref.dtype)


def _sample_post(yc, xs, z, dexp, snrm, u, h0r, h0i, wb, abr1, abi1, wcr, wci, d, wglu, bglu, nrm):
    n = yc.shape[0]
    args = (yc, xs, z, dexp, snrm, u, h0r, h0i, wb, abr1, abi1, wcr, wci, d, wglu, bglu, nrm)
    spec = lambda w: pl.BlockSpec((n, w), lambda: (0, 0))
    return pl.pallas_call(
        _sample_post_body,
        in_specs=[_full_spec(a) for a in args],
        out_specs=[spec(SSD_WIDTH), spec(S5_WIDTH), spec(S5_LANES), spec(S5_LANES)],
        out_shape=[jax.ShapeDtypeStruct((n, SSD_WIDTH), BF16), jax.ShapeDtypeStruct((n, S5_WIDTH), BF16),
                   jax.ShapeDtypeStruct((n, S5_LANES), F32), jax.ShapeDtypeStruct((n, S5_LANES), F32)],
        compiler_params=pltpu.CompilerParams(vmem_limit_bytes=VMEM_LIMIT),
        name="sample_post",
    )(*args)


def _mix_route_body(n_blocks, xp_ref, ysp_ref, y5p_ref, xs_ref, yss_ref, y5s_ref, *refs):
    cnt_ref, carry = refs[-2:]
    i = pl.program_id(0)

    @pl.when(i == 0)
    def _init():
        carry[...] = jnp.zeros_like(carry)

    @pl.when(i < n_blocks)
    def _prompt_rows():
        _mix_route_compute(xp_ref, ysp_ref, y5p_ref, *refs)

    @pl.when(i == n_blocks)
    def _sample_rows():
        _mix_route_compute(xs_ref, yss_ref, y5s_ref, *refs)

    cnt_ref[...] = carry[...]


def _mix_route_compute(x_ref, ys_ref, y5_ref, wa_ref, wb_ref, nf_ref, wrh_ref, wrl_ref, br_ref,
                       x1_ref, xn_ref, rt_ref, _, carry):
    rows = x_ref.shape[0]
    x1 = x_ref[...] + _dot(ys_ref[...], wa_ref[...]) + _dot(y5_ref[...].astype(BF16), wb_ref[...])
    x1_ref[0:rows, :] = x1
    xn = _rms(x1, nf_ref[...])
    for j in range(SLAB_ROWS):
        xn_ref[0:rows, j, :] = xn[:, j * LANES:(j + 1) * LANES]

    xh = xn.astype(BF16)
    xl = (xn - xh.astype(F32)).astype(BF16)
    logits = _dot(xh, wrh_ref[...]) + _dot(xl, wrh_ref[...]) + _dot(xh, wrl_ref[...]) + br_ref[...]
    tm = logits.shape[0]
    lane = lax.broadcasted_iota(jnp.int32, logits.shape, 1).astype(F32)
    neg = -jnp.inf
    big = float(LANES)

    def first_max(v):
        m = jnp.max(v, axis=-1, keepdims=True)
        return m, jnp.min(jnp.where(v == m, lane, big), axis=-1, keepdims=True)

    coarse = lane < MOE_GROUPS
    mc, gsel = first_max(jnp.where(coarse, logits, neg))
    psel = 1.0 / jnp.sum(jnp.where(coarse, jnp.exp(logits - mc), 0.0), axis=-1, keepdims=True)
    lo = MOE_GROUPS + MOE_EPG * gsel
    lf = jnp.where((lane >= lo) & (lane < lo + MOE_EPG), logits, neg)
    m1, i1 = first_max(lf)
    m2, i2 = first_max(jnp.where(lane == i1, neg, lf))
    e2 = jnp.exp(m2 - m1)
    g1 = psel / (1.0 + e2)
    g2 = psel * e2 / (1.0 + e2)
    e_a, e_b = i1 - MOE_GROUPS, i2 - MOE_GROUPS

    pick_a, pick_b = lane == e_a, lane == e_b
    picks = jnp.where(pick_a | pick_b, 1.0, 0.0)
    earlier = lax.broadcasted_iota(jnp.int32, (tm, tm), 0) > lax.broadcasted_iota(jnp.int32, (tm, tm), 1)
    prior = _dot(earlier.astype(BF16), picks.astype(BF16)) + carry[...]
    rank_a = jnp.sum(jnp.where(pick_a, prior, 0.0), axis=-1, keepdims=True)
    rank_b = jnp.sum(jnp.where(pick_b, prior, 0.0), axis=-1, keepdims=True)
    carry[...] = prior[tm - 1:tm, :] + picks[tm - 1:tm, :]

    out = jnp.zeros_like(logits)
    for k, v in enumerate((e_a, e_b, g1, g2, rank_a, rank_b)):
        out = jnp.where(lane == float(k), v, out)
    rt_ref[0:rows, :] = out


def _mix_route(prompt, sample, consts, tm):
    n_prompt, n_sample = prompt[0].shape[0], sample[0].shape[0]
    assert n_prompt % tm == 0 and n_sample <= tm
    n_blocks = n_prompt // tm
    total_rows = n_prompt + n_sample
    row = lambda w: pl.BlockSpec((tm, w), lambda i: (jnp.minimum(i, n_blocks - 1), 0))
    out_row = lambda w: pl.BlockSpec((tm, w), lambda i: (i, 0))
    return pl.pallas_call(
        functools.partial(_mix_route_body, n_blocks),
        grid=(n_blocks + 1,),
        in_specs=([row(D_MODEL), row(SSD_WIDTH), row(S5_WIDTH)] + [_full_spec(a) for a in sample]
                  + [_full_spec(a) for a in consts]),
        out_specs=[out_row(D_MODEL), pl.BlockSpec((tm, SLAB_ROWS, LANES), lambda i: (i, 0, 0)),
                   out_row(LANES), pl.BlockSpec((1, LANES), lambda i: (0, 0))],
        out_shape=[jax.ShapeDtypeStruct((total_rows, D_MODEL), F32),
                   jax.ShapeDtypeStruct((total_rows, SLAB_ROWS, LANES), F32),
                   jax.ShapeDtypeStruct((total_rows, LANES), F32), jax.ShapeDtypeStruct((1, LANES), F32)],
        scratch_shapes=[pltpu.VMEM((1, LANES), F32)],
        compiler_params=pltpu.CompilerParams(dimension_semantics=("arbitrary",), vmem_limit_bytes=VMEM_LIMIT),
        name="mix_route",
    )(*prompt, *sample, *consts)


def _sc_mesh():
    return plsc.VectorSubcoreMesh(core_axis_name="c", subcore_axis_name="s")


def _sc_worker():
    return lax.axis_index("s") * SC_CORES + lax.axis_index("c")


def _sc_dispatch(xn, pos_a, pos_b, n_rows):
    n_tok = xn.shape[0]
    ch = SC_DISPATCH_ROWS
    assert n_tok % ch == 0

    @functools.partial(
        pl.kernel, mesh=_sc_mesh(),
        out_type=jax.ShapeDtypeStruct((n_rows, SLAB_ROWS, LANES), F32),
        scratch_types=[pltpu.VMEM((ch,), jnp.int32), pltpu.VMEM((ch,), jnp.int32),
                       pltpu.VMEM((ch, SLAB_ROWS, LANES), F32), pltpu.SemaphoreType.DMA])
    def push(xn_hbm, pa_hbm, pb_hbm, xs_hbm, ia, ib, rows, sem):
        @pl.loop(_sc_worker(), n_tok // ch, step=SC_WORKERS)
        def _(c):
            off = pl.multiple_of(c * ch, ch)
            pltpu.sync_copy(pa_hbm.at[pl.ds(off, ch)], ia)
            pltpu.sync_copy(pb_hbm.at[pl.ds(off, ch)], ib)
            pltpu.sync_copy(xn_hbm.at[pl.ds(off, ch)], rows)
            pltpu.async_copy(rows, xs_hbm.at[ia], sem).wait()
            pltpu.async_copy(rows, xs_hbm.at[ib], sem).wait()

    return push(xn, pos_a, pos_b)


def _sc_collect(ysorted, pos_flat):
    n_pick = pos_flat.shape[0]
    ch = SC_COLLECT_ROWS
    per_worker = n_pick // SC_WORKERS
    assert n_pick % SC_WORKERS == 0 and per_worker % ch == 0

    @functools.partial(
        pl.kernel, mesh=_sc_mesh(),
        out_type=jax.ShapeDtypeStruct((n_pick, SLAB_ROWS, LANES), F32),
        scratch_types=[pltpu.VMEM((ch,), jnp.int32), pltpu.VMEM((ch, SLAB_ROWS, LANES), F32),
                       pltpu.SemaphoreType.DMA])
    def pull(ys_hbm, pos_hbm, out_hbm, idx, rows, sem):
        base = _sc_worker() * per_worker

        @pl.loop(0, per_worker // ch)
        def _(j):
            off = pl.multiple_of(base + j * ch, SUBLANES)
            pltpu.sync_copy(pos_hbm.at[pl.ds(off, ch)], idx)
            pltpu.async_copy(ys_hbm.at[idx], rows, sem).wait()
            pltpu.sync_copy(rows, out_hbm.at[pl.ds(off, ch)])

    return pull(ysorted, pos_flat)


def _moe_ffn_body(te_ref, nused_ref, xs_hbm, wg_ref, wu_ref, wd_ref, ys_hbm, xbuf, ybuf, wgb, wub, wdb,
                  in_sems, out_sems):
    i = pl.program_id(0)
    n_used = nused_ref[0]

    def in_copy(tile, j):
        return pltpu.make_async_copy(xs_hbm.at[pl.ds(tile * MOE_TILE, MOE_TILE), j, :],
                                     xbuf.at[tile % 2, :, pl.ds(j * LANES, LANES)], in_sems.at[tile % 2, j])

    def out_copy(tile, j):
        return pltpu.make_async_copy(ybuf.at[tile % 2, :, pl.ds(j * LANES, LANES)],
                                     ys_hbm.at[pl.ds(tile * MOE_TILE, MOE_TILE), j, :], out_sems.at[tile % 2, j])

    @pl.when(i == 0)
    def _first_fetch():
        for j in range(SLAB_ROWS):
            in_copy(0, j).start()

    @pl.when(i + 1 < n_used)
    def _prefetch():
        for j in range(SLAB_ROWS):
            in_copy(i + 1, j).start()

    @pl.when(i < n_used)
    def _tile():
        @pl.when((i == 0) | (te_ref[i] != te_ref[jnp.maximum(i - 1, 0)]))
        def _cast_weights():
            wgb[...] = wg_ref[0].astype(BF16)
            wub[...] = wu_ref[0].astype(BF16)
            wdb[...] = wd_ref[0].astype(BF16)

        for j in range(SLAB_ROWS):
            in_copy(i, j).wait()
        x = xbuf[i % 2].astype(BF16)
        gate = _dot(x, wgb[...])
        hmid = (gate * jax.nn.sigmoid(gate)) * _dot(x, wub[...])
        ybuf[i % 2] = _dot(hmid.astype(BF16), wdb[...])
        for j in range(SLAB_ROWS):
            out_copy(i, j).start()

    @pl.when((i > 0) & (i <= n_used))
    def _wait_previous_out():
        for j in range(SLAB_ROWS):
            out_copy(i - 1, j).wait()


def _moe_ffn(tile_expert, n_used, xsorted, w_gate, w_up, w_down):
    n_steps = tile_expert.shape[0]
    wspec = lambda s: pl.BlockSpec((1,) + s, lambda i, te, nu: (te[i], 0, 0))
    buf = pltpu.VMEM((2, MOE_TILE, D_MODEL), F32)
    return pl.pallas_call(
        _moe_ffn_body,
        grid_spec=pltpu.PrefetchScalarGridSpec(
            num_scalar_prefetch=2,
            grid=(n_steps,),
            in_specs=[pl.BlockSpec(memory_space=pl.ANY),
                      wspec((D_MODEL, MOE_D_FF)), wspec((D_MODEL, MOE_D_FF)), wspec((MOE_D_FF, D_MODEL))],
            out_specs=pl.BlockSpec(memory_space=pl.ANY),
            scratch_shapes=[buf, buf,
                            pltpu.VMEM((D_MODEL, MOE_D_FF), BF16), pltpu.VMEM((D_MODEL, MOE_D_FF), BF16),
                            pltpu.VMEM((MOE_D_FF, D_MODEL), BF16),
                            pltpu.SemaphoreType.DMA((2, SLAB_ROWS)), pltpu.SemaphoreType.DMA((2, SLAB_ROWS))]),
        out_shape=jax.ShapeDtypeStruct(xsorted.shape, F32),
        compiler_params=pltpu.CompilerParams(dimension_semantics=("arbitrary",), vmem_limit_bytes=VMEM_LIMIT),
        name="moe_ffn",
    )(tile_expert, n_used, xsorted, w_gate, w_up, w_down)


def _combine_body(x1_ref, rt_ref, yt_ref, nf_ref, out_ref):
    rt = rt_ref[...]
    x1 = x1_ref[...]
    x2 = jnp.concatenate(
        [x1[:, j * LANES:(j + 1) * LANES] + rt[:, 2:3] * yt_ref[:, 0, j, :] + rt[:, 3:4] * yt_ref[:, 1, j, :]
         for j in range(SLAB_ROWS)], axis=-1)
    out_ref[...] = _rms(x2, nf_ref[...])


def _combine(x1, rt, y_picks, nf, tm, rows, row_block_offset):
    row = lambda w: pl.BlockSpec((tm, w), lambda i: (i + row_block_offset, 0))
    return pl.pallas_call(
        _combine_body,
        grid=(rows // tm,),
        in_specs=[row(D_MODEL), row(LANES),
                  pl.BlockSpec((tm, 2, SLAB_ROWS, LANES), lambda i: (i + row_block_offset, 0, 0, 0)),
                  pl.BlockSpec((1, D_MODEL), lambda i: (0, 0))],
        out_specs=pl.BlockSpec((tm, D_MODEL), lambda i: (i, 0)),
        out_shape=jax.ShapeDtypeStruct((rows, D_MODEL), F32),
        compiler_params=pltpu.CompilerParams(dimension_semantics=("parallel",), vmem_limit_bytes=VMEM_LIMIT),
        name="moe_combine",
    )(x1, rt, y_picks, nf)


def _route_tables(counts, eid, rank, n_tiles):
    experts = jnp.arange(MOE_EXPERTS, dtype=jnp.int32)
    tiles_per = (counts + MOE_TILE - 1) // MOE_TILE
    tile_end = jnp.cumsum(tiles_per)
    pstart = (tile_end - tiles_per) * MOE_TILE
    pos = jnp.sum(jnp.where(eid[..., None] == experts, pstart, 0), axis=-1) + rank
    n_used = tile_end[-1]
    tiles = jnp.arange(n_tiles, dtype=jnp.int32)
    tile_expert = jnp.sum((tile_end[None, :] <= jnp.minimum(tiles, n_used - 1)[:, None]).astype(jnp.int32), axis=1)
    return pos, tile_expert, n_used.reshape(1).astype(jnp.int32)


def _s5_tables(a_re, a_im, log_dt, b_re, b_im, c_re, c_im):
    dt = jnp.exp(log_dt)[:, None]
    mag = jnp.exp(a_re * dt)
    ab_re = mag * jnp.cos(a_im * dt)
    ab_im = mag * jnp.sin(a_im * dt)
    den = a_re * a_re + a_im * a_im
    nr = ab_re - 1.0
    q_re = (nr * a_re + ab_im * a_im) / den
    q_im = (ab_im * a_re - nr * a_im) / den
    bb_re = q_re[..., None] * b_re - q_im[..., None] * b_im
    bb_im = q_re[..., None] * b_im + q_im[..., None] * b_re
    eye = jnp.eye(16, dtype=F32)
    nblk = S5_GROUPS // 16

    def in_map(bb):
        w = jnp.einsum("jgpc,gh->jgchp", bb.reshape(nblk, 16, S5_STATE, S5_GROUP_CH), eye)
        return w.reshape(nblk, 16 * S5_GROUP_CH, 16 * S5_STATE)

    def out_map(cc):
        w = jnp.einsum("jgcp,gh->jgphc", cc.reshape(nblk, 16, S5_GROUP_CH, S5_STATE), eye)
        return w.reshape(nblk, 16 * S5_STATE, 16 * S5_GROUP_CH)

    wb = jnp.concatenate([in_map(bb_re), in_map(bb_im)], axis=-1).astype(BF16)
    return (wb, ab_re.reshape(1, S5_LANES), ab_im.reshape(1, S5_LANES),
            out_map(c_re).astype(BF16), out_map(-c_im).astype(BF16))


def kernel(x_prompt, x_sample, state_ssd_conv, state_ssd_ssm, state_s5_re, state_s5_im, meta_tokens, norm_mix, w_in, conv_w, conv_b, dt_bias, a_log, d_ssd, ssd_norm, s5_a_re, s5_a_im, s5_log_dt, s5_b_re, s5_b_im, s5_c_re, s5_c_im, s5_d, w_glu, b_glu, s5_norm, w_out, norm_ffn, router_coarse_w, router_coarse_b, router_fine_w, router_fine_b, w_gate, w_up, w_down, norm_final):
    bp, seq, _ = x_prompt.shape
    bs = x_sample.shape[0]
    n_prompt = bp * seq
    n_tok = n_prompt + bs
    row2 = lambda v: v.reshape(1, -1)
    pad_heads = lambda v: jnp.pad(v, (0, LANES - SSD_HEADS)).reshape(1, LANES)

    w = w_in[0]
    o1, o2, o3 = SSD_WIDTH, SSD_WIDTH + SSD_CONV_DIM, SSD_WIDTH + SSD_CONV_DIM + SSD_HEADS
    wz, wx, wu = w[:, :o1].astype(BF16), w[:, o1:o2].astype(BF16), w[:, o3:].astype(BF16)
    wdt = jnp.pad(w[:, o2:o3], ((0, 0), (0, LANES - SSD_HEADS))).astype(BF16)
    g_mix = row2(norm_mix[0])
    cw, cb = conv_w[0], row2(conv_b[0])
    dtb, alog = pad_heads(dt_bias[0]), pad_heads(a_log[0])
    dexp = row2(jnp.repeat(d_ssd[0], SSD_HEAD_DIM))
    snrm = row2(ssd_norm[0])
    eexp = (jnp.arange(LANES)[:, None] == (jnp.arange(SSD_WIDTH) // SSD_HEAD_DIM)[None, :]).astype(BF16)
    wb5, ab_re, ab_im, wcr, wci = _s5_tables(s5_a_re[0], s5_a_im[0], s5_log_dt[0], s5_b_re[0], s5_b_im[0],
                                             s5_c_re[0], s5_c_im[0])
    d5, wglu, bglu, nrm5 = row2(s5_d[0]), w_glu[0].astype(BF16), row2(b_glu[0]), row2(s5_norm[0])
    wo_a, wo_b = w_out[0][:SSD_WIDTH].astype(BF16), w_out[0][SSD_WIDTH:].astype(BF16)
    w_r = jnp.concatenate([router_coarse_w[0], router_fine_w[0].transpose(1, 0, 2).reshape(D_MODEL, MOE_EXPERTS)], axis=1)
    w_r = jnp.pad(w_r, ((0, 0), (0, LANES - w_r.shape[1])))
    wrh = w_r.astype(BF16)
    wrl = (w_r - wrh.astype(F32)).astype(BF16)
    b_r = jnp.concatenate([router_coarse_b[0], router_fine_b[0].reshape(-1)])
    b_r = jnp.pad(b_r, (0, LANES - b_r.shape[0])).reshape(1, LANES)

    zp, xbcp, dtp, up = _in_proj(x_prompt.reshape(n_prompt, D_MODEL), g_mix, wz, wx, wdt, wu, TOK_TILE, BF16, F32)
    xsm = jnp.concatenate([x_sample.reshape(bs, D_MODEL), meta_tokens], axis=0)
    zs, xbcs, dts, us = _in_proj(xsm, g_mix, wz, wx, wdt, wu, xsm.shape[0], F32, F32)

    front = SSD_CHUNK - N_META
    padf = lambda a: jnp.pad(a[bs:], ((front, 0), (0, 0)))[None]
    gw = SSD_HPG * SSD_HEAD_DIM
    ssd_consts = (cw, cb, dtb, alog, dexp, snrm, eexp)
    _, ctail_m, _, ht_m = _ssd_chunked(
        padf(xbcs), padf(dts), jnp.zeros((1, SSD_CHUNK, SSD_WIDTH), F32),
        jnp.zeros((1, SUBLANES, SSD_CONV_DIM), F32), jnp.zeros((1, SSD_GROUPS, SSD_STATE, gw), F32),
        *ssd_consts, mask_rows=front)
    y_ssd_p, ctail_p, ssm_p, _ = _ssd_chunked(
        xbcp.reshape(bp, seq, SSD_CONV_DIM), dtp.reshape(bp, seq, LANES), zp.reshape(bp, seq, SSD_WIDTH),
        ctail_m, ht_m, *ssd_consts, mask_rows=0)

    abr8, abi8 = jnp.broadcast_to(ab_re, (bp, S5_LANES)), jnp.broadcast_to(ab_im, (bp, S5_LANES))
    um8 = jnp.repeat(us[bs:], bp, axis=0).astype(BF16)
    y_s5_p, s5re_p, s5im_p = _s5_seq(up.reshape(bp, seq, S5_WIDTH), um8, wb5, abr8, abi8,
                                     wcr, wci, d5, wglu, bglu, nrm5)

    cst = state_ssd_conv[0]
    xt_s, dt_s, dec_s, bc, xs_s = _ssd_step_prep(xbcs[:bs], cst[:, 0], cst[:, 1], cst[:, 2], dts[:bs],
                                                 cw, cb, dtb, alog)
    ssm_s, y_core = _ssd_step(dt_s[:, :SSD_HEADS].reshape(-1), dec_s[:, :SSD_HEADS].reshape(-1),
                              state_ssd_ssm[0], xt_s, bc)
    y_ssd_s, y_s5_s, s5re_s, s5im_s = _sample_post(
        y_core, xs_s, zs[:bs], dexp, snrm, us[:bs], state_s5_re[0].reshape(bs, S5_LANES),
        state_s5_im[0].reshape(bs, S5_LANES), wb5, ab_re, ab_im, wcr, wci, d5, wglu, bglu, nrm5)

    route_consts = (wo_a, wo_b, row2(norm_ffn[0]), wrh, wrl, b_r)
    x1, xn, rt, counts = _mix_route(
        (x_prompt.reshape(n_prompt, D_MODEL), y_ssd_p.reshape(n_prompt, SSD_WIDTH), y_s5_p.reshape(n_prompt, S5_WIDTH)),
        (x_sample.reshape(bs, D_MODEL), y_ssd_s, y_s5_s), route_consts, TOK_TILE)

    n_tiles = -(-2 * n_tok // MOE_TILE) + MOE_EXPERTS
    eid = jnp.clip(rt[:, 0:2].astype(jnp.int32), 0, MOE_EXPERTS - 1)
    pos, tile_expert, n_used = _route_tables(counts[0, :MOE_EXPERTS].astype(jnp.int32), eid,
                                             rt[:, 4:6].astype(jnp.int32), n_tiles + 1)
    xsorted = _sc_dispatch(xn, pos[:, 0], pos[:, 1], n_tiles * MOE_TILE)
    ysorted = _moe_ffn(tile_expert, n_used, xsorted, w_gate[0], w_up[0], w_down[0])
    y_picks = _sc_collect(ysorted, pos.reshape(-1)).reshape(n_tok, 2, SLAB_ROWS, LANES)
    nfin = row2(norm_final)
    y_p = _combine(x1, rt, y_picks, nfin, MOE_TILE, n_prompt, 0)
    y_s = _combine(x1, rt, y_picks, nfin, bs, bs, n_prompt // bs)

    s5_state = lambda a, b: a.reshape(1, b, S5_GROUPS, S5_STATE)
    new_conv_s = jnp.stack([cst[:, 1], cst[:, 2], xbcs[:bs]], axis=1)[None]
    return (y_p.reshape(bp, seq, D_MODEL), y_s.reshape(bs, 1, D_MODEL),
            ctail_p[:, SUBLANES - (SSD_CONV - 1):][None], ssm_p[None], s5_state(s5re_p, bp), s5_state(s5im_p, bp),
            new_conv_s, ssm_s[None], s5_state(s5re_s, bs), s5_state(s5im_s, bs))
```

```python
import functools

import jax
import jax.numpy as jnp
from jax import lax
from jax.experimental import pallas as pl
from jax.experimental.pallas import tpu as pltpu
from jax.experimental.pallas import tpu_sc as plsc

F32, BF16 = jnp.float32, jnp.bfloat16

D_MODEL = 1024
N_META = 16
SSD_WIDTH = 1024
SSD_HEAD_DIM = 64
SSD_HEADS = 16
SSD_GROUPS = 2
SSD_HPG = SSD_HEADS // SSD_GROUPS
SSD_STATE = 128
SSD_CONV = 4
SSD_CHUNK = 128
SSD_CONV_DIM = SSD_WIDTH + 2 * SSD_GROUPS * SSD_STATE
S5_WIDTH = 1024
S5_GROUP_CH = 16
S5_GROUPS = 64
S5_STATE = 64
S5_LANES = S5_GROUPS * S5_STATE
MOE_GROUPS = 4
MOE_EPG = 8
MOE_EXPERTS = MOE_GROUPS * MOE_EPG
MOE_D_FF = 512
EPS = 1e-6

LANES = 128
SUBLANES = 8
VMEM_LIMIT = 56 * 1024 * 1024

S5_TIME_TILE = 64
S5_SCAN_LANES = 512
MOE_TILE = 256
SLAB_ROWS = D_MODEL // LANES
SC_CORES = 2
SC_SUBCORES = 16
SC_WORKERS = SC_CORES * SC_SUBCORES
SC_DISPATCH_ROWS = 64
SC_COLLECT_ROWS = 24
TOK_TILE = 512


def _dot(a, b):
    return jnp.dot(a, b, preferred_element_type=F32)


def _rms(x, g):
    return x * lax.rsqrt(jnp.mean(x * x, axis=-1, keepdims=True) + EPS) * g


def _softplus(x):
    return jnp.maximum(x, 0.0) + jnp.log1p(jnp.exp(-jnp.abs(x)))


def _split3(x):
    hi = x.astype(BF16)
    r = x - hi.astype(F32)
    mid = r.astype(BF16)
    lo = (r - mid.astype(F32)).astype(BF16)
    return hi, mid, lo


def _dot3(x, w):
    hi, mid, lo = _split3(x)
    return _dot(hi, w) + _dot(mid, w) + _dot(lo, w)


def _dot3_left(w, x):
    hi, mid, lo = _split3(x)
    return _dot(w, hi) + _dot(w, mid) + _dot(w, lo)


def _full_spec(a):
    nd = a.ndim
    return pl.BlockSpec(a.shape, lambda *_: (0,) * nd)


def _resident_spec(a):
    nd = a.ndim
    return pl.BlockSpec(a.shape, lambda *_: (0,) * nd, pipeline_mode=pl.Buffered(1))


def _in_proj_body(x_ref, g_ref, wz_ref, wx_ref, wdt_ref, wu_ref, z_ref, xbc_ref, dt_ref, u_ref):
    xb = _rms(x_ref[...], g_ref[...]).astype(BF16)
    z_ref[...] = _dot(xb, wz_ref[...]).astype(z_ref.dtype)
    xbc_ref[...] = _dot(xb, wx_ref[...]).astype(xbc_ref.dtype)
    dt_ref[...] = _dot(xb, wdt_ref[...])
    u_ref[...] = _dot(xb, wu_ref[...]).astype(u_ref.dtype)


def _in_proj(x2d, g, wz, wx, wdt, wu, tm, act_dtype, u_dtype):
    rows = x2d.shape[0]
    row = lambda w: pl.BlockSpec((tm, w), lambda i: (i, 0))
    return pl.pallas_call(
        _in_proj_body,
        grid=(rows // tm,),
        in_specs=[row(D_MODEL), _full_spec(g), _full_spec(wz), _full_spec(wx), _full_spec(wdt), _full_spec(wu)],
        out_specs=[row(SSD_WIDTH), row(SSD_CONV_DIM), row(LANES), row(S5_WIDTH)],
        out_shape=[jax.ShapeDtypeStruct((rows, SSD_WIDTH), act_dtype),
                   jax.ShapeDtypeStruct((rows, SSD_CONV_DIM), act_dtype),
                   jax.ShapeDtypeStruct((rows, LANES), F32),
                   jax.ShapeDtypeStruct((rows, S5_WIDTH), u_dtype)],
        compiler_params=pltpu.CompilerParams(dimension_semantics=("parallel",), vmem_limit_bytes=VMEM_LIMIT),
        name="in_proj",
    )(x2d, g, wz, wx, wdt, wu)


def _ssd_body(mask_rows, xbc_ref, dt_ref, z_ref, cinit_ref, hinit_ref, cw_ref, cb_ref, dtb_ref, alog_ref,
              dexp_ref, nrm_ref, eexp_ref, y_ref, ctail_ref, st_ref, hto_ref, xwin, hT):
    c = pl.program_id(1)
    L = SSD_CHUNK

    @pl.when(c == 0)
    def _init():
        xwin[0:SUBLANES, :] = cinit_ref[0]
        hT[...] = hinit_ref[0]

    xwin[SUBLANES:SUBLANES + L, :] = xbc_ref[0].astype(F32)
    acc = cb_ref[...]
    for k in range(SSD_CONV):
        off = SUBLANES - (SSD_CONV - 1) + k
        acc = acc + xwin[off:off + L, :] * cw_ref[k:k + 1, :]
    tail = xwin[L:L + SUBLANES, :]
    xwin[0:SUBLANES, :] = tail
    ctail_ref[0] = tail

    xact = acc * jax.nn.sigmoid(acc)
    dt = _softplus(dt_ref[0] + dtb_ref[...])
    if mask_rows:
        valid = lax.broadcasted_iota(jnp.int32, (L, 1), 0) >= mask_rows
        xact = jnp.where(valid, xact, 0.0)
        dt = jnp.where(valid, dt, 0.0)

    a_neg = -jnp.exp(alog_ref[...])
    dA = dt * a_neg
    row_i = lax.broadcasted_iota(jnp.int32, (L, L), 0)
    col_i = lax.broadcasted_iota(jnp.int32, (L, L), 1)
    causal = row_i >= col_i
    tril = causal.astype(BF16)
    cs = _dot3_left(tril, dA)
    csT = cs.T
    dtT = dt.T
    ecs = jnp.exp(cs)
    wdec = jnp.exp(cs[L - 1:L, :] - cs) * dt
    eexp = eexp_ref[...]
    ecs_e = _dot3(ecs, eexp)
    wdec_e = _dot3(wdec, eexp)
    lane = lax.broadcasted_iota(jnp.int32, (L, LANES), 1)
    first_half = lane < SSD_HEAD_DIM

    gw = SSD_HPG * SSD_HEAD_DIM
    y_groups = []
    for g in range(SSD_GROUPS):
        b_g = xact[:, SSD_WIDTH + g * SSD_STATE: SSD_WIDTH + (g + 1) * SSD_STATE]
        c_g = xact[:, SSD_WIDTH + (SSD_GROUPS + g) * SSD_STATE: SSD_WIDTH + (SSD_GROUPS + g + 1) * SSD_STATE]
        b_b = b_g.astype(BF16)
        c_b = c_g.astype(BF16)
        cb = lax.dot_general(c_b, b_b, (((1,), (1,)), ((), ())), preferred_element_type=F32)
        xs_g = xact[:, g * gw:(g + 1) * gw]
        h_prev = hT[g]
        y_off = _dot(c_b, h_prev.astype(BF16)) * ecs_e[:, g * gw:(g + 1) * gw]
        xdec = (xs_g * wdec_e[:, g * gw:(g + 1) * gw]).astype(BF16)
        hT[g] = h_prev * ecs_e[L - 1:L, g * gw:(g + 1) * gw] + _dot(b_g.T.astype(BF16), xdec)
        pieces = []
        for j in range(SSD_HPG // 2):
            xs_pair = xs_g[:, j * LANES:(j + 1) * LANES]
            halves = (jnp.where(first_half, xs_pair, 0.0).astype(BF16),
                      jnp.where(first_half, 0.0, xs_pair).astype(BF16))
            yd = None
            for t in range(2):
                h = g * SSD_HPG + 2 * j + t
                seg = cs[:, h:h + 1] - csT[h:h + 1, :]
                lmat = jnp.exp(jnp.where(causal, seg, -jnp.inf))
                m = (cb * lmat * dtT[h:h + 1, :]).astype(BF16)
                part = _dot(m, halves[t])
                yd = part if yd is None else yd + part
            pieces.append(yd)
        y_groups.append(jnp.concatenate(pieces, axis=-1) + y_off + dexp_ref[:, g * gw:(g + 1) * gw] * xs_g)
    y = jnp.concatenate(y_groups, axis=-1)
    z = z_ref[0].astype(F32)
    y_ref[0] = _rms(y * (z * jax.nn.sigmoid(z)), nrm_ref[...]).astype(y_ref.dtype)

    @pl.when(c == pl.num_programs(1) - 1)
    def _emit():
        hto_ref[0] = hT[...]
        for g in range(SSD_GROUPS):
            t = hT[g].T
            for k in range(SSD_HPG):
                st_ref[0, g * SSD_HPG + k] = t[k * SSD_HEAD_DIM:(k + 1) * SSD_HEAD_DIM, :]


def _ssd_chunked(xbc, dt, z, cinit, hinit, cw, cb, dtb, alog, dexp, nrm, eexp, mask_rows):
    bsz, seq, _ = xbc.shape
    nc = seq // SSD_CHUNK
    gw = SSD_HPG * SSD_HEAD_DIM
    blk = lambda w: pl.BlockSpec((1, SSD_CHUNK, w), lambda b, c: (b, c, 0))
    return pl.pallas_call(
        functools.partial(_ssd_body, mask_rows),
        grid=(bsz, nc),
        in_specs=[blk(SSD_CONV_DIM), blk(LANES), blk(SSD_WIDTH),
                  pl.BlockSpec((1, SUBLANES, SSD_CONV_DIM), lambda b, c: (0, 0, 0)),
                  pl.BlockSpec((1, SSD_GROUPS, SSD_STATE, gw), lambda b, c: (0, 0, 0, 0)),
                  _full_spec(cw), _full_spec(cb), _full_spec(dtb), _full_spec(alog),
                  _full_spec(dexp), _full_spec(nrm), _full_spec(eexp)],
        out_specs=[blk(SSD_WIDTH),
                   pl.BlockSpec((1, SUBLANES, SSD_CONV_DIM), lambda b, c: (b, 0, 0)),
                   pl.BlockSpec((1, SSD_HEADS, SSD_HEAD_DIM, SSD_STATE), lambda b, c: (b, 0, 0, 0)),
                   pl.BlockSpec((1, SSD_GROUPS, SSD_STATE, gw), lambda b, c: (b, 0, 0, 0))],
        out_shape=[jax.ShapeDtypeStruct((bsz, seq, SSD_WIDTH), BF16),
                   jax.ShapeDtypeStruct((bsz, SUBLANES, SSD_CONV_DIM), F32),
                   jax.ShapeDtypeStruct((bsz, SSD_HEADS, SSD_HEAD_DIM, SSD_STATE), F32),
                   jax.ShapeDtypeStruct((bsz, SSD_GROUPS, SSD_STATE, gw), F32)],
        scratch_shapes=[pltpu.VMEM((SUBLANES + SSD_CHUNK, SSD_CONV_DIM), F32),
                        pltpu.VMEM((SSD_GROUPS, SSD_STATE, gw), F32)],
        compiler_params=pltpu.CompilerParams(dimension_semantics=("parallel", "arbitrary"),
                                             vmem_limit_bytes=VMEM_LIMIT),
        name="ssd_chunked",
    )(xbc, dt, z, cinit, hinit, cw, cb, dtb, alog, dexp, nrm, eexp)


def _ssd_step_prep_body(xbc_ref, c0_ref, c1_ref, c2_ref, dt_ref, cw_ref, cb_ref, dtb_ref, alog_ref,
                        xt_ref, dt_out_ref, dec_ref, bc_ref, xs_ref):
    acc = cb_ref[...]
    for k, r in enumerate((c0_ref, c1_ref, c2_ref, xbc_ref)):
        acc = acc + r[...] * cw_ref[k:k + 1, :]
    xact = acc * jax.nn.sigmoid(acc)
    xs = xact[:, :SSD_WIDTH]
    dt = _softplus(dt_ref[...] + dtb_ref[...])
    dt_out_ref[...] = dt
    dec_ref[...] = jnp.exp(dt * -jnp.exp(alog_ref[...]))
    bc_ref[...] = xact[:, SSD_WIDTH:]
    xs_ref[...] = xs
    xt_ref[...] = xs.T.astype(xt_ref.dtype)


def _ssd_step_prep(xbc, c0, c1, c2, dt, cw, cb, dtb, alog):
    n = xbc.shape[0]
    args = (xbc, c0, c1, c2, dt, cw, cb, dtb, alog)
    spec = lambda r, w: pl.BlockSpec((r, w), lambda: (0, 0))
    return pl.pallas_call(
        _ssd_step_prep_body,
        in_specs=[_full_spec(a) for a in args],
        out_specs=[spec(SSD_WIDTH, n), spec(n, LANES), spec(n, LANES), spec(n, 2 * SSD_GROUPS * SSD_STATE),
                   spec(n, SSD_WIDTH)],
        out_shape=[jax.ShapeDtypeStruct((SSD_WIDTH, n), BF16), jax.ShapeDtypeStruct((n, LANES), F32),
                   jax.ShapeDtypeStruct((n, LANES), F32),
                   jax.ShapeDtypeStruct((n, 2 * SSD_GROUPS * SSD_STATE), F32),
                   jax.ShapeDtypeStruct((n, SSD_WIDTH), F32)],
        compiler_params=pltpu.CompilerParams(vmem_limit_bytes=VMEM_LIMIT),
        name="ssd_step_prep",
    )(*args)


def _ssd_step_body(dt_ref, dec_ref, st_ref, xt_ref, bc_ref, so_ref, y_ref):
    n = xt_ref.shape[1]
    gw = SSD_HPG * SSD_HEAD_DIM
    blk = pl.program_id(0)
    seq_id = lax.broadcasted_iota(jnp.int32, (n, SSD_STATE), 0)
    sub_id = lax.broadcasted_iota(jnp.int32, (SUBLANES, gw), 0)
    base = pl.multiple_of(blk * SUBLANES, SUBLANES)
    y_acc = [jnp.zeros((SUBLANES, gw), F32) for _ in range(SSD_GROUPS)]
    for i in range(SUBLANES):
        s = blk * SUBLANES + i
        for g in range(SSD_GROUPS):
            b_all = bc_ref[:, g * SSD_STATE:(g + 1) * SSD_STATE]
            rhs = jnp.where(seq_id == s, b_all, 0.0).astype(BF16)
            outer = _dot(xt_ref[g * gw:(g + 1) * gw, :], rhs)
            news = []
            for k in range(SSD_HPG):
                h = g * SSD_HPG + k
                new = (dec_ref[s * SSD_HEADS + h] * st_ref[i, h]
                       + dt_ref[s * SSD_HEADS + h] * outer[k * SSD_HEAD_DIM:(k + 1) * SSD_HEAD_DIM, :])
                so_ref[i, h] = new
                news.append(new)
            new_g = jnp.concatenate(news, axis=0).astype(BF16)
            c_lo = (SSD_GROUPS + g) * SSD_STATE
            c_blk = bc_ref[pl.ds(base, SUBLANES), c_lo:c_lo + SSD_STATE].astype(BF16)
            r = lax.dot_general(c_blk, new_g, (((1,), (1,)), ((), ())), preferred_element_type=F32)
            y_acc[g] = y_acc[g] + jnp.where(sub_id == i, r, 0.0)
    y_ref[...] = jnp.concatenate(y_acc, axis=-1)


def _ssd_step(dt_flat, dec_flat, state, xt, bc):
    n = state.shape[0]
    st_spec = pl.BlockSpec((SUBLANES, SSD_HEADS, SSD_HEAD_DIM, SSD_STATE), lambda i, *_: (i, 0, 0, 0))
    return pl.pallas_call(
        _ssd_step_body,
        grid_spec=pltpu.PrefetchScalarGridSpec(
            num_scalar_prefetch=2,
            grid=(n // SUBLANES,),
            in_specs=[st_spec, pl.BlockSpec(xt.shape, lambda i, *_: (0, 0)),
                      pl.BlockSpec(bc.shape, lambda i, *_: (0, 0))],
            out_specs=[st_spec, pl.BlockSpec((SUBLANES, SSD_WIDTH), lambda i, *_: (i, 0))]),
        out_shape=[jax.ShapeDtypeStruct(state.shape, F32), jax.ShapeDtypeStruct((n, SSD_WIDTH), F32)],
        compiler_params=pltpu.CompilerParams(dimension_semantics=("parallel",), vmem_limit_bytes=VMEM_LIMIT),
        name="ssd_step",
    )(dt_flat, dec_flat, state, xt, bc)


def _s5_project_in(u_b16, wb_ref, store):
    kw = 16 * S5_GROUP_CH
    nw = 16 * S5_STATE
    for j in range(S5_WIDTH // kw):
        r = _dot(u_b16[:, j * kw:(j + 1) * kw], wb_ref[j])
        store(j, r[:, :nw], r[:, nw:])


def _s5_tail(hre_of, him_of, u_f32, wcr_ref, wci_ref, d_ref, wglu_ref, bglu_ref, nrm_ref):
    cols = []
    for j in range(wcr_ref.shape[0]):
        cols.append(_dot(hre_of(j).astype(BF16), wcr_ref[j]) + _dot(him_of(j).astype(BF16), wci_ref[j]))
    y = jnp.concatenate(cols, axis=-1) + d_ref[...] * u_f32
    y = jax.nn.gelu(y)
    y = y * jax.nn.sigmoid(_dot(y.astype(BF16), wglu_ref[...]) + bglu_ref[...])
    return _rms(y, nrm_ref[...])


def _s5_seq_body(u_hbm, um_ref, wb_ref, abr_ref, abi_ref, wcr_ref, wci_ref, d_ref, wglu_ref, bglu_ref, nrm_ref,
                 y_hbm, sre_ref, sim_ref, ubuf, ybuf, bu, h, in_sems, out_sems):
    j = pl.program_id(0)
    last = pl.num_programs(0) - 1
    lc, bsz = ubuf.shape[1], ubuf.shape[2]
    rows = lc * bsz
    nw = 16 * S5_STATE

    def in_copy(step, b):
        return pltpu.make_async_copy(u_hbm.at[b, pl.ds(step * lc, lc), :], ubuf.at[step % 2, :, b, :],
                                     in_sems.at[step % 2, b])

    def out_copy(step, b):
        return pltpu.make_async_copy(ybuf.at[step % 2, :, b, :], y_hbm.at[b, pl.ds(step * lc, lc), :],
                                     out_sems.at[step % 2, b])

    def project_in(u_b16, nrows):
        def store(jj, re, im):
            bu[0:nrows, jj * nw:(jj + 1) * nw] = re
            bu[0:nrows, S5_LANES + jj * nw:S5_LANES + (jj + 1) * nw] = im
        _s5_project_in(u_b16, wb_ref, store)

    def scan(nsteps):
        for k in range(S5_LANES // S5_SCAN_LANES):
            sl_r = pl.ds(k * S5_SCAN_LANES, S5_SCAN_LANES)
            sl_i = pl.ds(S5_LANES + k * S5_SCAN_LANES, S5_SCAN_LANES)
            ar = abr_ref[:, sl_r]
            ai = abi_ref[:, sl_r]

            def step(l, carry):
                hr, hi = carry
                slab = pl.ds(pl.multiple_of(l * bsz, bsz), bsz)
                nr = ar * hr - ai * hi + bu[slab, sl_r]
                ni = ar * hi + ai * hr + bu[slab, sl_i]
                bu[slab, sl_r] = nr
                bu[slab, sl_i] = ni
                return nr, ni

            hr, hi = lax.fori_loop(0, nsteps, step, (h[:, sl_r], h[:, sl_i]))
            h[:, sl_r] = hr
            h[:, sl_i] = hi

    @pl.when(j == 0)
    def _first():
        for b in range(bsz):
            in_copy(0, b).start()
        h[...] = jnp.zeros_like(h)
        project_in(um_ref[...], N_META * bsz)
        scan(N_META)

    @pl.when(j < last)
    def _prefetch():
        for b in range(bsz):
            in_copy(j + 1, b).start()

    for b in range(bsz):
        in_copy(j, b).wait()
    u2 = ubuf[j % 2].reshape(rows, S5_WIDTH)
    project_in(u2.astype(BF16), rows)
    scan(lc)
    y = _s5_tail(lambda jj: bu[:, jj * nw:(jj + 1) * nw], lambda jj: bu[:, S5_LANES + jj * nw:S5_LANES + (jj + 1) * nw],
                 u2, wcr_ref, wci_ref, d_ref, wglu_ref, bglu_ref, nrm_ref)
    ybuf[j % 2] = y.reshape(lc, bsz, S5_WIDTH)
    for b in range(bsz):
        out_copy(j, b).start()

    @pl.when(j > 0)
    def _wait_previous_out():
        for b in range(bsz):
            out_copy(j - 1, b).wait()

    @pl.when(j == last)
    def _emit():
        for b in range(bsz):
            out_copy(j, b).wait()
        sre_ref[...] = h[:, 0:S5_LANES]
        sim_ref[...] = h[:, S5_LANES:]


def _s5_seq(u, um, wb, abr, abi, wcr, wci, d, wglu, bglu, nrm):
    bsz, seq, _ = u.shape
    lc = S5_TIME_TILE
    consts = (um, wb, abr, abi, wcr, wci, d, wglu, bglu, nrm)
    st = pl.BlockSpec((bsz, S5_LANES), lambda j: (0, 0))
    return pl.pallas_call(
        _s5_seq_body,
        grid=(seq // lc,),
        in_specs=[pl.BlockSpec(memory_space=pl.ANY)] + [_resident_spec(a) for a in consts],
        out_specs=[pl.BlockSpec(memory_space=pl.ANY), st, st],
        out_shape=[jax.ShapeDtypeStruct((bsz, seq, S5_WIDTH), F32),
                   jax.ShapeDtypeStruct((bsz, S5_LANES), F32), jax.ShapeDtypeStruct((bsz, S5_LANES), F32)],
        scratch_shapes=[pltpu.VMEM((2, lc, bsz, S5_WIDTH), F32), pltpu.VMEM((2, lc, bsz, S5_WIDTH), F32),
                        pltpu.VMEM((lc * bsz, 2 * S5_LANES), F32), pltpu.VMEM((bsz, 2 * S5_LANES), F32),
                        pltpu.SemaphoreType.DMA((2, bsz)), pltpu.SemaphoreType.DMA((2, bsz))],
        compiler_params=pltpu.CompilerParams(dimension_semantics=("arbitrary",), vmem_limit_bytes=VMEM_LIMIT),
        name="s5_seq",
    )(u, *consts)


def _sample_post_body(yc_ref, xs_ref, z_ref, dexp_ref, snrm_ref, u_ref, hr_ref, hi_ref, wb_ref, abr_ref, abi_ref,
                      wcr_ref, wci_ref, d_ref, wglu_ref, bglu_ref, nrm_ref,
                      yssd_ref, ys5_ref, nre_ref, nim_ref):
    z = z_ref[...]
    y = yc_ref[...] + dexp_ref[...] * xs_ref[...]
    yssd_ref[...] = _rms(y * (z * jax.nn.sigmoid(z)), snrm_ref[...]).astype(yssd_ref.dtype)

    u = u_ref[...]
    nw = 16 * S5_STATE
    ar, ai = abr_ref[...], abi_ref[...]

    def store(jj, re, im):
        sl = slice(jj * nw, (jj + 1) * nw)
        h0r, h0i = hr_ref[:, sl], hi_ref[:, sl]
        nre_ref[:, sl] = ar[:, sl] * h0r - ai[:, sl] * h0i + re
        nim_ref[:, sl] = ar[:, sl] * h0i + ai[:, sl] * h0r + im

    _s5_project_in(u.astype(BF16), wb_ref, store)
    slab = lambda ref: (lambda jj: ref[:, jj * nw:(jj + 1) * nw])
    y5 = _s5_tail(slab(nre_ref), slab(nim_ref), u, wcr_ref, wci_ref, d_ref, wglu_ref, bglu_ref, nrm_ref)
    ys5_ref[...] = y5.astype(ys5_ref.dtype)


def _sample_post(yc, xs, z, dexp, snrm, u, h0r, h0i, wb, abr1, abi1, wcr, wci, d, wglu, bglu, nrm):
    n = yc.shape[0]
    args = (yc, xs, z, dexp, snrm, u, h0r, h0i, wb, abr1, abi1, wcr, wci, d, wglu, bglu, nrm)
    spec = lambda w: pl.BlockSpec((n, w), lambda: (0, 0))
    return pl.pallas_call(
        _sample_post_body,
        in_specs=[_full_spec(a) for a in args],
        out_specs=[spec(SSD_WIDTH), spec(S5_WIDTH), spec(S5_LANES), spec(S5_LANES)],
        out_shape=[jax.ShapeDtypeStruct((n, SSD_WIDTH), BF16), jax.ShapeDtypeStruct((n, S5_WIDTH), BF16),
                   jax.ShapeDtypeStruct((n, S5_LANES), F32), jax.ShapeDtypeStruct((n, S5_LANES), F32)],
        compiler_params=pltpu.CompilerParams(vmem_limit_bytes=VMEM_LIMIT),
        name="sample_post",
    )(*args)


def _mix_route_body(n_blocks, xp_ref, ysp_ref, y5p_ref, xs_ref, yss_ref, y5s_ref, *refs):
    cnt_ref, carry = refs[-2:]
    i = pl.program_id(0)

    @pl.when(i == 0)
    def _init():
        carry[...] = jnp.zeros_like(carry)

    @pl.when(i < n_blocks)
    def _prompt_rows():
        _mix_route_compute(xp_ref, ysp_ref, y5p_ref, *refs)

    @pl.when(i == n_blocks)
    def _sample_rows():
        _mix_route_compute(xs_ref, yss_ref, y5s_ref, *refs)

    cnt_ref[...] = carry[...]


def _mix_route_compute(x_ref, ys_ref, y5_ref, wa_ref, wb_ref, nf_ref, wrh_ref, wrl_ref, br_ref,
                       x1_ref, xn_ref, rt_ref, _, carry):
    rows = x_ref.shape[0]
    x1 = x_ref[...] + _dot(ys_ref[...], wa_ref[...]) + _dot(y5_ref[...].astype(BF16), wb_ref[...])
    x1_ref[0:rows, :] = x1
    xn = _rms(x1, nf_ref[...])
    for j in range(SLAB_ROWS):
        xn_ref[0:rows, j, :] = xn[:, j * LANES:(j + 1) * LANES]

    xh = xn.astype(BF16)
    xl = (xn - xh.astype(F32)).astype(BF16)
    logits = _dot(xh, wrh_ref[...]) + _dot(xl, wrh_ref[...]) + _dot(xh, wrl_ref[...]) + br_ref[...]
    tm = logits.shape[0]
    lane = lax.broadcasted_iota(jnp.int32, logits.shape, 1).astype(F32)
    neg = -jnp.inf
    big = float(LANES)

    def first_max(v):
        m = jnp.max(v, axis=-1, keepdims=True)
        return m, jnp.min(jnp.where(v == m, lane, big), axis=-1, keepdims=True)

    coarse = lane < MOE_GROUPS
    mc, gsel = first_max(jnp.where(coarse, logits, neg))
    psel = 1.0 / jnp.sum(jnp.where(coarse, jnp.exp(logits - mc), 0.0), axis=-1, keepdims=True)
    lo = MOE_GROUPS + MOE_EPG * gsel
    lf = jnp.where((lane >= lo) & (lane < lo + MOE_EPG), logits, neg)
    m1, i1 = first_max(lf)
    m2, i2 = first_max(jnp.where(lane == i1, neg, lf))
    e2 = jnp.exp(m2 - m1)
    g1 = psel / (1.0 + e2)
    g2 = psel * e2 / (1.0 + e2)
    e_a, e_b = i1 - MOE_GROUPS, i2 - MOE_GROUPS

    pick_a, pick_b = lane == e_a, lane == e_b
    picks = jnp.where(pick_a | pick_b, 1.0, 0.0)
    earlier = lax.broadcasted_iota(jnp.int32, (tm, tm), 0) > lax.broadcasted_iota(jnp.int32, (tm, tm), 1)
    prior = _dot(earlier.astype(BF16), picks.astype(BF16)) + carry[...]
    rank_a = jnp.sum(jnp.where(pick_a, prior, 0.0), axis=-1, keepdims=True)
    rank_b = jnp.sum(jnp.where(pick_b, prior, 0.0), axis=-1, keepdims=True)
    carry[...] = prior[tm - 1:tm, :] + picks[tm - 1:tm, :]

    out = jnp.zeros_like(logits)
    for k, v in enumerate((e_a, e_b, g1, g2, rank_a, rank_b)):
        out = jnp.where(lane == float(k), v, out)
    rt_ref[0:rows, :] = out


def _mix_route(prompt, sample, consts, tm):
    n_prompt, n_sample = prompt[0].shape[0], sample[0].shape[0]
    assert n_prompt % tm == 0 and n_sample <= tm
    n_blocks = n_prompt // tm
    total_rows = n_prompt + n_sample
    row = lambda w: pl.BlockSpec((tm, w), lambda i: (jnp.minimum(i, n_blocks - 1), 0))
    out_row = lambda w: pl.BlockSpec((tm, w), lambda i: (i, 0))
    return pl.pallas_call(
        functools.partial(_mix_route_body, n_blocks),
        grid=(n_blocks + 1,),
        in_specs=([row(D_MODEL), row(SSD_WIDTH), row(S5_WIDTH)] + [_full_spec(a) for a in sample]
                  + [_full_spec(a) for a in consts]),
        out_specs=[out_row(D_MODEL), pl.BlockSpec((tm, SLAB_ROWS, LANES), lambda i: (i, 0, 0)),
                   out_row(LANES), pl.BlockSpec((1, LANES), lambda i: (0, 0))],
        out_shape=[jax.ShapeDtypeStruct((total_rows, D_MODEL), F32),
                   jax.ShapeDtypeStruct((total_rows, SLAB_ROWS, LANES), F32),
                   jax.ShapeDtypeStruct((total_rows, LANES), F32), jax.ShapeDtypeStruct((1, LANES), F32)],
        scratch_shapes=[pltpu.VMEM((1, LANES), F32)],
        compiler_params=pltpu.CompilerParams(dimension_semantics=("arbitrary",), vmem_limit_bytes=VMEM_LIMIT),
        name="mix_route",
    )(*prompt, *sample, *consts)


def _sc_mesh():
    return plsc.VectorSubcoreMesh(core_axis_name="c", subcore_axis_name="s")


def _sc_worker():
    return lax.axis_index("s") * SC_CORES + lax.axis_index("c")


def _sc_dispatch(xn, pos_a, pos_b, n_rows):
    n_tok = xn.shape[0]
    ch = SC_DISPATCH_ROWS
    assert n_tok % ch == 0

    @functools.partial(
        pl.kernel, mesh=_sc_mesh(),
        out_type=jax.ShapeDtypeStruct((n_rows, SLAB_ROWS, LANES), F32),
        scratch_types=[pltpu.VMEM((ch,), jnp.int32), pltpu.VMEM((ch,), jnp.int32),
                       pltpu.VMEM((ch, SLAB_ROWS, LANES), F32), pltpu.SemaphoreType.DMA])
    def push(xn_hbm, pa_hbm, pb_hbm, xs_hbm, ia, ib, rows, sem):
        @pl.loop(_sc_worker(), n_tok // ch, step=SC_WORKERS)
        def _(c):
            off = pl.multiple_of(c * ch, ch)
            pltpu.sync_copy(pa_hbm.at[pl.ds(off, ch)], ia)
            pltpu.sync_copy(pb_hbm.at[pl.ds(off, ch)], ib)
            pltpu.sync_copy(xn_hbm.at[pl.ds(off, ch)], rows)
            pltpu.async_copy(rows, xs_hbm.at[ia], sem).wait()
            pltpu.async_copy(rows, xs_hbm.at[ib], sem).wait()

    return push(xn, pos_a, pos_b)


def _sc_collect(ysorted, pos_flat):
    n_pick = pos_flat.shape[0]
    ch = SC_COLLECT_ROWS
    per_worker = n_pick // SC_WORKERS
    n_chunks = per_worker // ch
    assert n_pick % SC_WORKERS == 0 and per_worker % ch == 0

    @functools.partial(
        pl.kernel, mesh=_sc_mesh(),
        out_type=jax.ShapeDtypeStruct((n_pick, SLAB_ROWS, LANES), F32),
        scratch_types=[pltpu.VMEM((ch,), jnp.int32), pltpu.VMEM((ch,), jnp.int32),
                       pltpu.VMEM((ch, SLAB_ROWS, LANES), F32), pltpu.VMEM((ch, SLAB_ROWS, LANES), F32),
                       pltpu.SemaphoreType.DMA, pltpu.SemaphoreType.DMA])
    def pull(ys_hbm, pos_hbm, out_hbm, idx0, idx1, rows0, rows1, sem0, sem1):
        base = _sc_worker() * per_worker
        bufs = ((idx0, rows0, sem0), (idx1, rows1, sem1))

        def offset(j):
            return pl.multiple_of(base + j * ch, SUBLANES)

        def fetch(j, b):
            idx, rows, sem = bufs[b]
            pltpu.sync_copy(pos_hbm.at[pl.ds(offset(j), ch)], idx)
            pltpu.async_copy(ys_hbm.at[idx], rows, sem)

        def flush(j, b):
            idx, rows, sem = bufs[b]
            pltpu.make_async_copy(ys_hbm.at[idx], rows, sem).wait()
            pltpu.sync_copy(rows, out_hbm.at[pl.ds(offset(j), ch)])

        fetch(0, 0)

        @pl.loop(0, n_chunks // 2)
        def _(p):
            j = 2 * p
            fetch(j + 1, 1)
            flush(j, 0)

            @pl.when(j + 2 < n_chunks)
            def _():
                fetch(j + 2, 0)

            flush(j + 1, 1)

        if n_chunks % 2:
            flush(n_chunks - 1, 0)

    return pull(ysorted, pos_flat)


def _moe_ffn_body(te_ref, nused_ref, xs_hbm, wg_ref, wu_ref, wd_ref, ys_hbm, xbuf, ybuf, wgb, wub, wdb,
                  in_sems, out_sems):
    i = pl.program_id(0)
    n_used = nused_ref[0]

    def in_copy(tile, j):
        return pltpu.make_async_copy(xs_hbm.at[pl.ds(tile * MOE_TILE, MOE_TILE), j, :],
                                     xbuf.at[tile % 2, :, pl.ds(j * LANES, LANES)], in_sems.at[tile % 2, j])

    def out_copy(tile, j):
        return pltpu.make_async_copy(ybuf.at[tile % 2, :, pl.ds(j * LANES, LANES)],
                                     ys_hbm.at[pl.ds(tile * MOE_TILE, MOE_TILE), j, :], out_sems.at[tile % 2, j])

    @pl.when(i == 0)
    def _first_fetch():
        for j in range(SLAB_ROWS):
            in_copy(0, j).start()

    @pl.when(i + 1 < n_used)
    def _prefetch():
        for j in range(SLAB_ROWS):
            in_copy(i + 1, j).start()

    @pl.when(i < n_used)
    def _tile():
        @pl.when((i == 0) | (te_ref[i] != te_ref[jnp.maximum(i - 1, 0)]))
        def _cast_weights():
            wgb[...] = wg_ref[0].astype(BF16)
            wub[...] = wu_ref[0].astype(BF16)
            wdb[...] = wd_ref[0].astype(BF16)

        for j in range(SLAB_ROWS):
            in_copy(i, j).wait()
        x = xbuf[i % 2].astype(BF16)
        gate = _dot(x, wgb[...])
        hmid = (gate * jax.nn.sigmoid(gate)) * _dot(x, wub[...])
        ybuf[i % 2] = _dot(hmid.astype(BF16), wdb[...])
        for j in range(SLAB_ROWS):
            out_copy(i, j).start()

    @pl.when((i > 0) & (i <= n_used))
    def _wait_previous_out():
        for j in range(SLAB_ROWS):
            out_copy(i - 1, j).wait()


def _moe_ffn(tile_expert, n_used, xsorted, w_gate, w_up, w_down):
    n_steps = tile_expert.shape[0]
    wspec = lambda s: pl.BlockSpec((1,) + s, lambda i, te, nu: (te[i], 0, 0))
    buf = pltpu.VMEM((2, MOE_TILE, D_MODEL), F32)
    return pl.pallas_call(
        _moe_ffn_body,
        grid_spec=pltpu.PrefetchScalarGridSpec(
            num_scalar_prefetch=2,
            grid=(n_steps,),
            in_specs=[pl.BlockSpec(memory_space=pl.ANY),
                      wspec((D_MODEL, MOE_D_FF)), wspec((D_MODEL, MOE_D_FF)), wspec((MOE_D_FF, D_MODEL))],
            out_specs=pl.BlockSpec(memory_space=pl.ANY),
            scratch_shapes=[buf, buf,
                            pltpu.VMEM((D_MODEL, MOE_D_FF), BF16), pltpu.VMEM((D_MODEL, MOE_D_FF), BF16),
                            pltpu.VMEM((MOE_D_FF, D_MODEL), BF16),
                            pltpu.SemaphoreType.DMA((2, SLAB_ROWS)), pltpu.SemaphoreType.DMA((2, SLAB_ROWS))]),
        out_shape=jax.ShapeDtypeStruct(xsorted.shape, F32),
        compiler_params=pltpu.CompilerParams(dimension_semantics=("arbitrary",), vmem_limit_bytes=VMEM_LIMIT),
        name="moe_ffn",
    )(tile_expert, n_used, xsorted, w_gate, w_up, w_down)


def _combine_body(x1_ref, rt_ref, ya_ref, yb_ref, nf_ref, out_ref):
    rt = rt_ref[...]
    x1 = x1_ref[...]
    x2 = jnp.concatenate(
        [x1[:, j * LANES:(j + 1) * LANES] + rt[:, 2:3] * ya_ref[0, :, j, :] + rt[:, 3:4] * yb_ref[0, :, j, :]
         for j in range(SLAB_ROWS)], axis=-1)
    out_ref[...] = _rms(x2, nf_ref[...])


def _combine(x1, rt, y_picks, nf, tm, rows, row_block_offset):
    row = lambda w: pl.BlockSpec((tm, w), lambda i: (i + row_block_offset, 0))
    pick = lambda k: pl.BlockSpec((1, tm, SLAB_ROWS, LANES), lambda i: (k, i + row_block_offset, 0, 0))
    return pl.pallas_call(
        _combine_body,
        grid=(rows // tm,),
        in_specs=[row(D_MODEL), row(LANES), pick(0), pick(1),
                  pl.BlockSpec((1, D_MODEL), lambda i: (0, 0))],
        out_specs=pl.BlockSpec((tm, D_MODEL), lambda i: (i, 0)),
        out_shape=jax.ShapeDtypeStruct((rows, D_MODEL), F32),
        compiler_params=pltpu.CompilerParams(dimension_semantics=("parallel",), vmem_limit_bytes=VMEM_LIMIT),
        name="moe_combine",
    )(x1, rt, y_picks, y_picks, nf)


def _route_tables(counts, eid, rank, n_tiles):
    experts = jnp.arange(MOE_EXPERTS, dtype=jnp.int32)
    tiles_per = (counts + MOE_TILE - 1) // MOE_TILE
    tile_end = jnp.cumsum(tiles_per)
    pstart = (tile_end - tiles_per) * MOE_TILE
    pos = [jnp.sum(jnp.where(e[:, None] == experts, pstart, 0), axis=-1) + r for e, r in zip(eid, rank)]
    n_used = tile_end[-1]
    tiles = jnp.arange(n_tiles, dtype=jnp.int32)
    tile_expert = jnp.sum((tile_end[None, :] <= jnp.minimum(tiles, n_used - 1)[:, None]).astype(jnp.int32), axis=1)
    return pos, tile_expert, n_used.reshape(1).astype(jnp.int32)


def _s5_tables(a_re, a_im, log_dt, b_re, b_im, c_re, c_im):
    dt = jnp.exp(log_dt)[:, None]
    mag = jnp.exp(a_re * dt)
    ab_re = mag * jnp.cos(a_im * dt)
    ab_im = mag * jnp.sin(a_im * dt)
    den = a_re * a_re + a_im * a_im
    nr = ab_re - 1.0
    q_re = (nr * a_re + ab_im * a_im) / den
    q_im = (ab_im * a_re - nr * a_im) / den
    bb_re = q_re[..., None] * b_re - q_im[..., None] * b_im
    bb_im = q_re[..., None] * b_im + q_im[..., None] * b_re
    nblk = S5_GROUPS // 16
    kw, nw = 16 * S5_GROUP_CH, 16 * S5_STATE
    same_group = (jnp.arange(kw)[:, None] // S5_GROUP_CH) == (jnp.arange(nw)[None, :] // S5_STATE)

    def in_map(bb):
        rows = bb.reshape(nblk, 16, S5_STATE, S5_GROUP_CH).transpose(0, 1, 3, 2).reshape(nblk, kw, S5_STATE)
        return jnp.where(same_group, jnp.tile(rows, (1, 1, 16)), 0.0)

    def out_map(cc):
        cols = cc.reshape(nblk, 16, S5_GROUP_CH, S5_STATE).transpose(0, 3, 1, 2).reshape(nblk, S5_STATE, kw)
        return jnp.where(same_group.T, jnp.tile(cols, (1, 16, 1)), 0.0)

    wb = jnp.concatenate([in_map(bb_re), in_map(bb_im)], axis=-1).astype(BF16)
    return (wb, ab_re.reshape(1, S5_LANES), ab_im.reshape(1, S5_LANES),
            out_map(c_re).astype(BF16), out_map(-c_im).astype(BF16))


def kernel(x_prompt, x_sample, state_ssd_conv, state_ssd_ssm, state_s5_re, state_s5_im, meta_tokens, norm_mix, w_in, conv_w, conv_b, dt_bias, a_log, d_ssd, ssd_norm, s5_a_re, s5_a_im, s5_log_dt, s5_b_re, s5_b_im, s5_c_re, s5_c_im, s5_d, w_glu, b_glu, s5_norm, w_out, norm_ffn, router_coarse_w, router_coarse_b, router_fine_w, router_fine_b, w_gate, w_up, w_down, norm_final):
    bp, seq, _ = x_prompt.shape
    bs = x_sample.shape[0]
    n_prompt = bp * seq
    n_tok = n_prompt + bs
    row2 = lambda v: v.reshape(1, -1)
    pad_heads = lambda v: jnp.pad(v, (0, LANES - SSD_HEADS)).reshape(1, LANES)

    w = w_in[0]
    o1, o2, o3 = SSD_WIDTH, SSD_WIDTH + SSD_CONV_DIM, SSD_WIDTH + SSD_CONV_DIM + SSD_HEADS
    wz, wx, wu = w[:, :o1].astype(BF16), w[:, o1:o2].astype(BF16), w[:, o3:].astype(BF16)
    wdt = jnp.pad(w[:, o2:o3], ((0, 0), (0, LANES - SSD_HEADS))).astype(BF16)
    g_mix = row2(norm_mix[0])
    cw, cb = conv_w[0], row2(conv_b[0])
    dtb, alog = pad_heads(dt_bias[0]), pad_heads(a_log[0])
    dexp = row2(jnp.repeat(d_ssd[0], SSD_HEAD_DIM))
    snrm = row2(ssd_norm[0])
    eexp = (jnp.arange(LANES)[:, None] == (jnp.arange(SSD_WIDTH) // SSD_HEAD_DIM)[None, :]).astype(BF16)
    wb5, ab_re, ab_im, wcr, wci = _s5_tables(s5_a_re[0], s5_a_im[0], s5_log_dt[0], s5_b_re[0], s5_b_im[0],
                                             s5_c_re[0], s5_c_im[0])
    d5, wglu, bglu, nrm5 = row2(s5_d[0]), w_glu[0].astype(BF16), row2(b_glu[0]), row2(s5_norm[0])
    wo_a, wo_b = w_out[0][:SSD_WIDTH].astype(BF16), w_out[0][SSD_WIDTH:].astype(BF16)
    w_r = jnp.concatenate([router_coarse_w[0], router_fine_w[0].transpose(1, 0, 2).reshape(D_MODEL, MOE_EXPERTS)], axis=1)
    w_r = jnp.pad(w_r, ((0, 0), (0, LANES - w_r.shape[1])))
    wrh = w_r.astype(BF16)
    wrl = (w_r - wrh.astype(F32)).astype(BF16)
    b_r = jnp.concatenate([router_coarse_b[0], router_fine_b[0].reshape(-1)])
    b_r = jnp.pad(b_r, (0, LANES - b_r.shape[0])).reshape(1, LANES)

    zp, xbcp, dtp, up = _in_proj(x_prompt.reshape(n_prompt, D_MODEL), g_mix, wz, wx, wdt, wu, TOK_TILE, BF16, F32)
    xsm = jnp.concatenate([x_sample.reshape(bs, D_MODEL), meta_tokens], axis=0)
    zs, xbcs, dts, us = _in_proj(xsm, g_mix, wz, wx, wdt, wu, xsm.shape[0], F32, F32)

    front = SSD_CHUNK - N_META
    padf = lambda a: jnp.pad(a[bs:], ((front, 0), (0, 0)))[None]
    gw = SSD_HPG * SSD_HEAD_DIM
    ssd_consts = (cw, cb, dtb, alog, dexp, snrm, eexp)
    _, ctail_m, _, ht_m = _ssd_chunked(
        padf(xbcs), padf(dts), jnp.zeros((1, SSD_CHUNK, SSD_WIDTH), F32),
        jnp.zeros((1, SUBLANES, SSD_CONV_DIM), F32), jnp.zeros((1, SSD_GROUPS, SSD_STATE, gw), F32),
        *ssd_consts, mask_rows=front)
    y_ssd_p, ctail_p, ssm_p, _ = _ssd_chunked(
        xbcp.reshape(bp, seq, SSD_CONV_DIM), dtp.reshape(bp, seq, LANES), zp.reshape(bp, seq, SSD_WIDTH),
        ctail_m, ht_m, *ssd_consts, mask_rows=0)

    abr8, abi8 = jnp.broadcast_to(ab_re, (bp, S5_LANES)), jnp.broadcast_to(ab_im, (bp, S5_LANES))
    um8 = jnp.repeat(us[bs:], bp, axis=0).astype(BF16)
    y_s5_p, s5re_p, s5im_p = _s5_seq(up.reshape(bp, seq, S5_WIDTH), um8, wb5, abr8, abi8,
                                     wcr, wci, d5, wglu, bglu, nrm5)

    cst = state_ssd_conv[0]
    xt_s, dt_s, dec_s, bc, xs_s = _ssd_step_prep(xbcs[:bs], cst[:, 0], cst[:, 1], cst[:, 2], dts[:bs],
                                                 cw, cb, dtb, alog)
    ssm_s, y_core = _ssd_step(dt_s[:, :SSD_HEADS].reshape(-1), dec_s[:, :SSD_HEADS].reshape(-1),
                              state_ssd_ssm[0], xt_s, bc)
    y_ssd_s, y_s5_s, s5re_s, s5im_s = _sample_post(
        y_core, xs_s, zs[:bs], dexp, snrm, us[:bs], state_s5_re[0].reshape(bs, S5_LANES),
        state_s5_im[0].reshape(bs, S5_LANES), wb5, ab_re, ab_im, wcr, wci, d5, wglu, bglu, nrm5)

    route_consts = (wo_a, wo_b, row2(norm_ffn[0]), wrh, wrl, b_r)
    x1, xn, rt, counts = _mix_route(
        (x_prompt.reshape(n_prompt, D_MODEL), y_ssd_p.reshape(n_prompt, SSD_WIDTH), y_s5_p.reshape(n_prompt, S5_WIDTH)),
        (x_sample.reshape(bs, D_MODEL), y_ssd_s, y_s5_s), route_consts, TOK_TILE)

    n_tiles = -(-2 * n_tok // MOE_TILE) + MOE_EXPERTS
    lane_i32 = lambda k: rt[:, k].astype(jnp.int32)
    eid = [jnp.clip(lane_i32(k), 0, MOE_EXPERTS - 1) for k in (0, 1)]
    (pos_a, pos_b), tile_expert, n_used = _route_tables(counts[0, :MOE_EXPERTS].astype(jnp.int32), eid,
                                                        [lane_i32(4), lane_i32(5)], n_tiles + 1)
    xsorted = _sc_dispatch(xn, pos_a, pos_b, n_tiles * MOE_TILE)
    ysorted = _moe_ffn(tile_expert, n_used, xsorted, w_gate[0], w_up[0], w_down[0])
    y_picks = _sc_collect(ysorted, jnp.concatenate([pos_a, pos_b])).reshape(2, n_tok, SLAB_ROWS, LANES)
    nfin = row2(norm_final)
    y_p = _combine(x1, rt, y_picks, nfin, MOE_TILE, n_prompt, 0)
    y_s = _combine(x1, rt, y_picks, nfin, bs, bs, n_prompt // bs)

    s5_state = lambda a, b: a.reshape(1, b, S5_GROUPS, S5_STATE)
    new_conv_s = jnp.stack([cst[:, 1], cst[:, 2], xbcs[:bs]], axis=1)[None]
    return (y_p.reshape(bp, seq, D_MODEL), y_s.reshape(bs, 1, D_MODEL),
            ctail_p[:, SUBLANES - (SSD_CONV - 1):][None], ssm_p[None], s5_state(s5re_p, bp), s5_state(s5im_p, bp),
            new_conv_s, ssm_s[None], s5_state(s5re_s, bs), s5_state(s5im_s, bs))
```

```python
import functools

import jax
import jax.numpy as jnp
from jax import lax
from jax.experimental import pallas as pl
from jax.experimental.pallas import tpu as pltpu
from jax.experimental.pallas import tpu_sc as plsc

F32, BF16 = jnp.float32, jnp.bfloat16

D_MODEL = 1024
N_META = 16
SSD_WIDTH = 1024
SSD_HEAD_DIM = 64
SSD_HEADS = 16
SSD_GROUPS = 2
SSD_HPG = SSD_HEADS // SSD_GROUPS
SSD_STATE = 128
SSD_CONV = 4
SSD_CHUNK = 128
SSD_CONV_DIM = SSD_WIDTH + 2 * SSD_GROUPS * SSD_STATE
S5_WIDTH = 1024
S5_GROUP_CH = 16
S5_GROUPS = 64
S5_STATE = 64
S5_LANES = S5_GROUPS * S5_STATE
MOE_GROUPS = 4
MOE_EPG = 8
MOE_EXPERTS = MOE_GROUPS * MOE_EPG
MOE_D_FF = 512
EPS = 1e-6

LANES = 128
SUBLANES = 8
VMEM_LIMIT = 56 * 1024 * 1024

S5_TIME_TILE = 64
S5_SCAN_LANES = 512
MOE_TILE = 256
SLAB_ROWS = D_MODEL // LANES
SC_CORES = 2
SC_SUBCORES = 16
SC_WORKERS = SC_CORES * SC_SUBCORES
SC_DISPATCH_ROWS = 64
SC_COLLECT_ROWS = 24
TOK_TILE = 512


def _dot(a, b):
    return jnp.dot(a, b, preferred_element_type=F32)


def _rms(x, g):
    return x * lax.rsqrt(jnp.mean(x * x, axis=-1, keepdims=True) + EPS) * g


def _softplus(x):
    return jnp.maximum(x, 0.0) + jnp.log1p(jnp.exp(-jnp.abs(x)))


def _split3(x):
    hi = x.astype(BF16)
    r = x - hi.astype(F32)
    mid = r.astype(BF16)
    lo = (r - mid.astype(F32)).astype(BF16)
    return hi, mid, lo


def _dot3(x, w):
    hi, mid, lo = _split3(x)
    return _dot(hi, w) + _dot(mid, w) + _dot(lo, w)


def _dot3_left(w, x):
    hi, mid, lo = _split3(x)
    return _dot(w, hi) + _dot(w, mid) + _dot(w, lo)


def _full_spec(a):
    nd = a.ndim
    return pl.BlockSpec(a.shape, lambda *_: (0,) * nd)


def _resident_spec(a):
    nd = a.ndim
    return pl.BlockSpec(a.shape, lambda *_: (0,) * nd, pipeline_mode=pl.Buffered(1))


def _in_proj_body(x_ref, g_ref, wz_ref, wx_ref, wdt_ref, wu_ref, z_ref, xbc_ref, dt_ref, u_ref):
    xb = _rms(x_ref[...], g_ref[...]).astype(BF16)
    z_ref[...] = _dot(xb, wz_ref[...]).astype(z_ref.dtype)
    xbc_ref[...] = _dot(xb, wx_ref[...]).astype(xbc_ref.dtype)
    dt_ref[...] = _dot(xb, wdt_ref[...])
    u_ref[...] = _dot(xb, wu_ref[...]).astype(u_ref.dtype)


def _in_proj(x2d, g, wz, wx, wdt, wu, tm, act_dtype, u_dtype):
    rows = x2d.shape[0]
    row = lambda w: pl.BlockSpec((tm, w), lambda i: (i, 0))
    return pl.pallas_call(
        _in_proj_body,
        grid=(rows // tm,),
        in_specs=[row(D_MODEL), _full_spec(g), _full_spec(wz), _full_spec(wx), _full_spec(wdt), _full_spec(wu)],
        out_specs=[row(SSD_WIDTH), row(SSD_CONV_DIM), row(LANES), row(S5_WIDTH)],
        out_shape=[jax.ShapeDtypeStruct((rows, SSD_WIDTH), act_dtype),
                   jax.ShapeDtypeStruct((rows, SSD_CONV_DIM), act_dtype),
                   jax.ShapeDtypeStruct((rows, LANES), F32),
                   jax.ShapeDtypeStruct((rows, S5_WIDTH), u_dtype)],
        compiler_params=pltpu.CompilerParams(dimension_semantics=("parallel",), vmem_limit_bytes=VMEM_LIMIT),
        name="in_proj",
    )(x2d, g, wz, wx, wdt, wu)


def _ssd_body(mask_rows, xbc_ref, dt_ref, z_ref, cinit_ref, hinit_ref, cw_ref, cb_ref, dtb_ref, alog_ref,
              dexp_ref, nrm_ref, eexp_ref, y_ref, ctail_ref, st_ref, hto_ref, xwin, hT):
    c = pl.program_id(1)
    L = SSD_CHUNK

    @pl.when(c == 0)
    def _init():
        xwin[0:SUBLANES, :] = cinit_ref[0]
        hT[...] = hinit_ref[0]

    xwin[SUBLANES:SUBLANES + L, :] = xbc_ref[0].astype(F32)
    acc = cb_ref[...]
    for k in range(SSD_CONV):
        off = SUBLANES - (SSD_CONV - 1) + k
        acc = acc + xwin[off:off + L, :] * cw_ref[k:k + 1, :]
    tail = xwin[L:L + SUBLANES, :]
    xwin[0:SUBLANES, :] = tail
    ctail_ref[0] = tail

    xact = acc * jax.nn.sigmoid(acc)
    dt = _softplus(dt_ref[0] + dtb_ref[...])
    if mask_rows:
        valid = lax.broadcasted_iota(jnp.int32, (L, 1), 0) >= mask_rows
        xact = jnp.where(valid, xact, 0.0)
        dt = jnp.where(valid, dt, 0.0)

    a_neg = -jnp.exp(alog_ref[...])
    dA = dt * a_neg
    row_i = lax.broadcasted_iota(jnp.int32, (L, L), 0)
    col_i = lax.broadcasted_iota(jnp.int32, (L, L), 1)
    causal = row_i >= col_i
    tril = causal.astype(BF16)
    cs = _dot3_left(tril, dA)
    csT = cs.T
    dtT = dt.T
    ecs = jnp.exp(cs)
    wdec = jnp.exp(cs[L - 1:L, :] - cs) * dt
    eexp = eexp_ref[...]
    ecs_e = _dot3(ecs, eexp)
    wdec_e = _dot3(wdec, eexp)
    lane = lax.broadcasted_iota(jnp.int32, (L, LANES), 1)
    first_half = lane < SSD_HEAD_DIM

    gw = SSD_HPG * SSD_HEAD_DIM
    y_groups = []
    for g in range(SSD_GROUPS):
        b_g = xact[:, SSD_WIDTH + g * SSD_STATE: SSD_WIDTH + (g + 1) * SSD_STATE]
        c_g = xact[:, SSD_WIDTH + (SSD_GROUPS + g) * SSD_STATE: SSD_WIDTH + (SSD_GROUPS + g + 1) * SSD_STATE]
        b_b = b_g.astype(BF16)
        c_b = c_g.astype(BF16)
        cb = lax.dot_general(c_b, b_b, (((1,), (1,)), ((), ())), preferred_element_type=F32)
        xs_g = xact[:, g * gw:(g + 1) * gw]
        h_prev = hT[g]
        y_off = _dot(c_b, h_prev.astype(BF16)) * ecs_e[:, g * gw:(g + 1) * gw]
        xdec = (xs_g * wdec_e[:, g * gw:(g + 1) * gw]).astype(BF16)
        hT[g] = h_prev * ecs_e[L - 1:L, g * gw:(g + 1) * gw] + _dot(b_g.T.astype(BF16), xdec)
        pieces = []
        for j in range(SSD_HPG // 2):
            xs_pair = xs_g[:, j * LANES:(j + 1) * LANES]
            halves = (jnp.where(first_half, xs_pair, 0.0).astype(BF16),
                      jnp.where(first_half, 0.0, xs_pair).astype(BF16))
            yd = None
            for t in range(2):
                h = g * SSD_HPG + 2 * j + t
                seg = cs[:, h:h + 1] - csT[h:h + 1, :]
                lmat = jnp.exp(jnp.where(causal, seg, -jnp.inf))
                m = (cb * lmat * dtT[h:h + 1, :]).astype(BF16)
                part = _dot(m, halves[t])
                yd = part if yd is None else yd + part
            pieces.append(yd)
        y_groups.append(jnp.concatenate(pieces, axis=-1) + y_off + dexp_ref[:, g * gw:(g + 1) * gw] * xs_g)
    y = jnp.concatenate(y_groups, axis=-1)
    z = z_ref[0].astype(F32)
    y_ref[0] = _rms(y * (z * jax.nn.sigmoid(z)), nrm_ref[...]).astype(y_ref.dtype)

    @pl.when(c == pl.num_programs(1) - 1)
    def _emit():
        hto_ref[0] = hT[...]
        for g in range(SSD_GROUPS):
            t = hT[g].T
            for k in range(SSD_HPG):
                st_ref[0, g * SSD_HPG + k] = t[k * SSD_HEAD_DIM:(k + 1) * SSD_HEAD_DIM, :]


def _ssd_chunked(xbc, dt, z, cinit, hinit, cw, cb, dtb, alog, dexp, nrm, eexp, mask_rows):
    bsz, seq, _ = xbc.shape
    nc = seq // SSD_CHUNK
    gw = SSD_HPG * SSD_HEAD_DIM
    blk = lambda w: pl.BlockSpec((1, SSD_CHUNK, w), lambda b, c: (b, c, 0))
    return pl.pallas_call(
        functools.partial(_ssd_body, mask_rows),
        grid=(bsz, nc),
        in_specs=[blk(SSD_CONV_DIM), blk(LANES), blk(SSD_WIDTH),
                  pl.BlockSpec((1, SUBLANES, SSD_CONV_DIM), lambda b, c: (0, 0, 0)),
                  pl.BlockSpec((1, SSD_GROUPS, SSD_STATE, gw), lambda b, c: (0, 0, 0, 0)),
                  _full_spec(cw), _full_spec(cb), _full_spec(dtb), _full_spec(alog),
                  _full_spec(dexp), _full_spec(nrm), _full_spec(eexp)],
        out_specs=[blk(SSD_WIDTH),
                   pl.BlockSpec((1, SUBLANES, SSD_CONV_DIM), lambda b, c: (b, 0, 0)),
                   pl.BlockSpec((1, SSD_HEADS, SSD_HEAD_DIM, SSD_STATE), lambda b, c: (b, 0, 0, 0)),
                   pl.BlockSpec((1, SSD_GROUPS, SSD_STATE, gw), lambda b, c: (b, 0, 0, 0))],
        out_shape=[jax.ShapeDtypeStruct((bsz, seq, SSD_WIDTH), BF16),
                   jax.ShapeDtypeStruct((bsz, SUBLANES, SSD_CONV_DIM), F32),
                   jax.ShapeDtypeStruct((bsz, SSD_HEADS, SSD_HEAD_DIM, SSD_STATE), F32),
                   jax.ShapeDtypeStruct((bsz, SSD_GROUPS, SSD_STATE, gw), F32)],
        scratch_shapes=[pltpu.VMEM((SUBLANES + SSD_CHUNK, SSD_CONV_DIM), F32),
                        pltpu.VMEM((SSD_GROUPS, SSD_STATE, gw), F32)],
        compiler_params=pltpu.CompilerParams(dimension_semantics=("parallel", "arbitrary"),
                                             vmem_limit_bytes=VMEM_LIMIT),
        name="ssd_chunked",
    )(xbc, dt, z, cinit, hinit, cw, cb, dtb, alog, dexp, nrm, eexp)


def _ssd_step_prep_body(xbc_ref, c0_ref, c1_ref, c2_ref, dt_ref, cw_ref, cb_ref, dtb_ref, alog_ref,
                        xt_ref, dt_out_ref, dec_ref, bc_ref, xs_ref):
    acc = cb_ref[...]
    for k, r in enumerate((c0_ref, c1_ref, c2_ref, xbc_ref)):
        acc = acc + r[...] * cw_ref[k:k + 1, :]
    xact = acc * jax.nn.sigmoid(acc)
    xs = xact[:, :SSD_WIDTH]
    dt = _softplus(dt_ref[...] + dtb_ref[...])
    dt_out_ref[...] = dt
    dec_ref[...] = jnp.exp(dt * -jnp.exp(alog_ref[...]))
    bc_ref[...] = xact[:, SSD_WIDTH:]
    xs_ref[...] = xs
    xt_ref[...] = xs.T.astype(xt_ref.dtype)


def _ssd_step_prep(xbc, c0, c1, c2, dt, cw, cb, dtb, alog):
    n = xbc.shape[0]
    args = (xbc, c0, c1, c2, dt, cw, cb, dtb, alog)
    spec = lambda r, w: pl.BlockSpec((r, w), lambda: (0, 0))
    return pl.pallas_call(
        _ssd_step_prep_body,
        in_specs=[_full_spec(a) for a in args],
        out_specs=[spec(SSD_WIDTH, n), spec(n, LANES), spec(n, LANES), spec(n, 2 * SSD_GROUPS * SSD_STATE),
                   spec(n, SSD_WIDTH)],
        out_shape=[jax.ShapeDtypeStruct((SSD_WIDTH, n), BF16), jax.ShapeDtypeStruct((n, LANES), F32),
                   jax.ShapeDtypeStruct((n, LANES), F32),
                   jax.ShapeDtypeStruct((n, 2 * SSD_GROUPS * SSD_STATE), F32),
                   jax.ShapeDtypeStruct((n, SSD_WIDTH), F32)],
        compiler_params=pltpu.CompilerParams(vmem_limit_bytes=VMEM_LIMIT),
        name="ssd_step_prep",
    )(*args)


def _ssd_step_body(dt_ref, dec_ref, st_ref, xt_ref, bc_ref, so_ref, y_ref):
    n = xt_ref.shape[1]
    gw = SSD_HPG * SSD_HEAD_DIM
    blk = pl.program_id(0)
    seq_id = lax.broadcasted_iota(jnp.int32, (n, SSD_STATE), 0)
    sub_id = lax.broadcasted_iota(jnp.int32, (SUBLANES, gw), 0)
    base = pl.multiple_of(blk * SUBLANES, SUBLANES)
    y_acc = [jnp.zeros((SUBLANES, gw), F32) for _ in range(SSD_GROUPS)]
    for i in range(SUBLANES):
        s = blk * SUBLANES + i
        for g in range(SSD_GROUPS):
            b_all = bc_ref[:, g * SSD_STATE:(g + 1) * SSD_STATE]
            rhs = jnp.where(seq_id == s, b_all, 0.0).astype(BF16)
            outer = _dot(xt_ref[g * gw:(g + 1) * gw, :], rhs)
            news = []
            for k in range(SSD_HPG):
                h = g * SSD_HPG + k
                new = (dec_ref[s * SSD_HEADS + h] * st_ref[i, h]
                       + dt_ref[s * SSD_HEADS + h] * outer[k * SSD_HEAD_DIM:(k + 1) * SSD_HEAD_DIM, :])
                so_ref[i, h] = new
                news.append(new)
            new_g = jnp.concatenate(news, axis=0).astype(BF16)
            c_lo = (SSD_GROUPS + g) * SSD_STATE
            c_blk = bc_ref[pl.ds(base, SUBLANES), c_lo:c_lo + SSD_STATE].astype(BF16)
            r = lax.dot_general(c_blk, new_g, (((1,), (1,)), ((), ())), preferred_element_type=F32)
            y_acc[g] = y_acc[g] + jnp.where(sub_id == i, r, 0.0)
    y_ref[...] = jnp.concatenate(y_acc, axis=-1)


def _ssd_step(dt_flat, dec_flat, state, xt, bc):
    n = state.shape[0]
    st_spec = pl.BlockSpec((SUBLANES, SSD_HEADS, SSD_HEAD_DIM, SSD_STATE), lambda i, *_: (i, 0, 0, 0))
    return pl.pallas_call(
        _ssd_step_body,
        grid_spec=pltpu.PrefetchScalarGridSpec(
            num_scalar_prefetch=2,
            grid=(n // SUBLANES,),
            in_specs=[st_spec, pl.BlockSpec(xt.shape, lambda i, *_: (0, 0)),
                      pl.BlockSpec(bc.shape, lambda i, *_: (0, 0))],
            out_specs=[st_spec, pl.BlockSpec((SUBLANES, SSD_WIDTH), lambda i, *_: (i, 0))]),
        out_shape=[jax.ShapeDtypeStruct(state.shape, F32), jax.ShapeDtypeStruct((n, SSD_WIDTH), F32)],
        compiler_params=pltpu.CompilerParams(dimension_semantics=("parallel",), vmem_limit_bytes=VMEM_LIMIT),
        name="ssd_step",
    )(dt_flat, dec_flat, state, xt, bc)


def _s5_project_in(u_b16, wb_ref, store):
    kw = 16 * S5_GROUP_CH
    nw = 16 * S5_STATE
    for j in range(S5_WIDTH // kw):
        r = _dot(u_b16[:, j * kw:(j + 1) * kw], wb_ref[j])
        store(j, r[:, :nw], r[:, nw:])


def _s5_tail(hre_of, him_of, u_f32, wcr_ref, wci_ref, d_ref, wglu_ref, bglu_ref, nrm_ref):
    cols = []
    for j in range(wcr_ref.shape[0]):
        cols.append(_dot(hre_of(j).astype(BF16), wcr_ref[j]) + _dot(him_of(j).astype(BF16), wci_ref[j]))
    return _s5_finish(cols, u_f32, d_ref, wglu_ref, bglu_ref, nrm_ref)


def _s5_finish(cols, u_f32, d_ref, wglu_ref, bglu_ref, nrm_ref):
    y = jnp.concatenate(cols, axis=-1) + d_ref[...] * u_f32
    y = jax.nn.gelu(y)
    y = y * jax.nn.sigmoid(_dot(y.astype(BF16), wglu_ref[...]) + bglu_ref[...])
    return _rms(y, nrm_ref[...])


def _s5_seq_body(u_hbm, um_ref, wb_ref, abr_ref, abi_ref, wcr_ref, wci_ref, d_ref, wglu_ref, bglu_ref, nrm_ref,
                 y_hbm, sre_ref, sim_ref, ubuf, ybuf, bu, h, in_sems, out_sems):
    j = pl.program_id(0)
    last = pl.num_programs(0) - 1
    lc, bsz = ubuf.shape[1], ubuf.shape[2]
    rows = lc * bsz
    nw = 16 * S5_STATE

    def in_copy(step, b):
        return pltpu.make_async_copy(u_hbm.at[b, pl.ds(step * lc, lc), :], ubuf.at[step % 2, :, b, :],
                                     in_sems.at[step % 2, b])

    def out_copy(step, b):
        return pltpu.make_async_copy(ybuf.at[step % 2, :, b, :], y_hbm.at[b, pl.ds(step * lc, lc), :],
                                     out_sems.at[step % 2, b])

    def project_in(u_b16, nrows):
        def store(jj, re, im):
            bu[0:nrows, jj * nw:(jj + 1) * nw] = re
            bu[0:nrows, S5_LANES + jj * nw:S5_LANES + (jj + 1) * nw] = im
        _s5_project_in(u_b16, wb_ref, store)

    def scan(nsteps):
        for k in range(S5_LANES // S5_SCAN_LANES):
            sl_r = pl.ds(k * S5_SCAN_LANES, S5_SCAN_LANES)
            sl_i = pl.ds(S5_LANES + k * S5_SCAN_LANES, S5_SCAN_LANES)
            ar = abr_ref[:, sl_r]
            ai = abi_ref[:, sl_r]

            def step(l, carry):
                hr, hi = carry
                slab = pl.ds(pl.multiple_of(l * bsz, bsz), bsz)
                nr = ar * hr - ai * hi + bu[slab, sl_r]
                ni = ar * hi + ai * hr + bu[slab, sl_i]
                bu[slab, sl_r] = nr
                bu[slab, sl_i] = ni
                return nr, ni

            hr, hi = lax.fori_loop(0, nsteps, step, (h[:, sl_r], h[:, sl_i]))
            h[:, sl_r] = hr
            h[:, sl_i] = hi

    @pl.when(j == 0)
    def _first():
        for b in range(bsz):
            in_copy(0, b).start()
        h[...] = jnp.zeros_like(h)
        project_in(um_ref[...], N_META * bsz)
        scan(N_META)

    @pl.when(j < last)
    def _prefetch():
        for b in range(bsz):
            in_copy(j + 1, b).start()

    for b in range(bsz):
        in_copy(j, b).wait()
    u2 = ubuf[j % 2].reshape(rows, S5_WIDTH)
    u_b16 = u2.astype(BF16)
    kw = 16 * S5_GROUP_CH

    def project_block(jj):
        r = _dot(u_b16[:, jj * kw:(jj + 1) * kw], wb_ref[jj])
        bu[0:rows, jj * nw:(jj + 1) * nw] = r[:, :nw]
        bu[0:rows, S5_LANES + jj * nw:S5_LANES + (jj + 1) * nw] = r[:, nw:]

    def scan_block(jj):
        for k in range(nw // S5_SCAN_LANES):
            lo = jj * nw + k * S5_SCAN_LANES
            sl_r = slice(lo, lo + S5_SCAN_LANES)
            sl_i = slice(S5_LANES + lo, S5_LANES + lo + S5_SCAN_LANES)
            ar, ai = abr_ref[:, sl_r], abi_ref[:, sl_r]
            hr, hi = h[:, sl_r], h[:, sl_i]
            for l in range(lc):
                slab = slice(l * bsz, (l + 1) * bsz)
                hr, hi = (ar * hr - ai * hi + bu[slab, sl_r], ar * hi + ai * hr + bu[slab, sl_i])
                bu[slab, sl_r] = hr
                bu[slab, sl_i] = hi
            h[:, sl_r] = hr
            h[:, sl_i] = hi

    def readout_block(jj):
        return (_dot(bu[:, jj * nw:(jj + 1) * nw].astype(BF16), wcr_ref[jj])
                + _dot(bu[:, S5_LANES + jj * nw:S5_LANES + (jj + 1) * nw].astype(BF16), wci_ref[jj]))

    n_blocks = S5_WIDTH // kw
    project_block(0)
    cols = []
    for jj in range(n_blocks):
        if jj + 1 < n_blocks:
            project_block(jj + 1)
        scan_block(jj)
        cols.append(readout_block(jj))
    y = _s5_finish(cols, u2, d_ref, wglu_ref, bglu_ref, nrm_ref)
    ybuf[j % 2] = y.reshape(lc, bsz, S5_WIDTH)
    for b in range(bsz):
        out_copy(j, b).start()

    @pl.when(j > 0)
    def _wait_previous_out():
        for b in range(bsz):
            out_copy(j - 1, b).wait()

    @pl.when(j == last)
    def _emit():
        for b in range(bsz):
            out_copy(j, b).wait()
        sre_ref[...] = h[:, 0:S5_LANES]
        sim_ref[...] = h[:, S5_LANES:]


def _s5_seq(u, um, wb, abr, abi, wcr, wci, d, wglu, bglu, nrm):
    bsz, seq, _ = u.shape
    lc = S5_TIME_TILE
    consts = (um, wb, abr, abi, wcr, wci, d, wglu, bglu, nrm)
    st = pl.BlockSpec((bsz, S5_LANES), lambda j: (0, 0))
    return pl.pallas_call(
        _s5_seq_body,
        grid=(seq // lc,),
        in_specs=[pl.BlockSpec(memory_space=pl.ANY)] + [_resident_spec(a) for a in consts],
        out_specs=[pl.BlockSpec(memory_space=pl.ANY), st, st],
        out_shape=[jax.ShapeDtypeStruct((bsz, seq, S5_WIDTH), F32),
                   jax.ShapeDtypeStruct((bsz, S5_LANES), F32), jax.ShapeDtypeStruct((bsz, S5_LANES), F32)],
        scratch_shapes=[pltpu.VMEM((2, lc, bsz, S5_WIDTH), F32), pltpu.VMEM((2, lc, bsz, S5_WIDTH), F32),
                        pltpu.VMEM((lc * bsz, 2 * S5_LANES), F32), pltpu.VMEM((bsz, 2 * S5_LANES), F32),
                        pltpu.SemaphoreType.DMA((2, bsz)), pltpu.SemaphoreType.DMA((2, bsz))],
        compiler_params=pltpu.CompilerParams(dimension_semantics=("arbitrary",), vmem_limit_bytes=VMEM_LIMIT),
        name="s5_seq",
    )(u, *consts)


def _sample_post_body(yc_ref, xs_ref, z_ref, dexp_ref, snrm_ref, u_ref, hr_ref, hi_ref, wb_ref, abr_ref, abi_ref,
                      wcr_ref, wci_ref, d_ref, wglu_ref, bglu_ref, nrm_ref,
                      yssd_ref, ys5_ref, nre_ref, nim_ref):
    z = z_ref[...]
    y = yc_ref[...] + dexp_ref[...] * xs_ref[...]
    yssd_ref[...] = _rms(y * (z * jax.nn.sigmoid(z)), snrm_ref[...]).astype(yssd_ref.dtype)

    u = u_ref[...]
    nw = 16 * S5_STATE
    ar, ai = abr_ref[...], abi_ref[...]

    def store(jj, re, im):
        sl = slice(jj * nw, (jj + 1) * nw)
        h0r, h0i = hr_ref[:, sl], hi_ref[:, sl]
        nre_ref[:, sl] = ar[:, sl] * h0r - ai[:, sl] * h0i + re
        nim_ref[:, sl] = ar[:, sl] * h0i + ai[:, sl] * h0r + im

    _s5_project_in(u.astype(BF16), wb_ref, store)
    slab = lambda ref: (lambda jj: ref[:, jj * nw:(jj + 1) * nw])
    y5 = _s5_tail(slab(nre_ref), slab(nim_ref), u, wcr_ref, wci_ref, d_ref, wglu_ref, bglu_ref, nrm_ref)
    ys5_ref[...] = y5.astype(ys5_ref.dtype)


def _sample_post(yc, xs, z, dexp, snrm, u, h0r, h0i, wb, abr1, abi1, wcr, wci, d, wglu, bglu, nrm):
    n = yc.shape[0]
    args = (yc, xs, z, dexp, snrm, u, h0r, h0i, wb, abr1, abi1, wcr, wci, d, wglu, bglu, nrm)
    spec = lambda w: pl.BlockSpec((n, w), lambda: (0, 0))
    return pl.pallas_call(
        _sample_post_body,
        in_specs=[_full_spec(a) for a in args],
        out_specs=[spec(SSD_WIDTH), spec(S5_WIDTH), spec(S5_LANES), spec(S5_LANES)],
        out_shape=[jax.ShapeDtypeStruct((n, SSD_WIDTH), BF16), jax.ShapeDtypeStruct((n, S5_WIDTH), BF16),
                   jax.ShapeDtypeStruct((n, S5_LANES), F32), jax.ShapeDtypeStruct((n, S5_LANES), F32)],
        compiler_params=pltpu.CompilerParams(vmem_limit_bytes=VMEM_LIMIT),
        name="sample_post",
    )(*args)


def _mix_route_body(n_blocks, xp_ref, ysp_ref, y5p_ref, xs_ref, yss_ref, y5s_ref, *refs):
    cnt_ref, carry = refs[-2:]
    i = pl.program_id(0)

    @pl.when(i == 0)
    def _init():
        carry[...] = jnp.zeros_like(carry)

    @pl.when(i < n_blocks)
    def _prompt_rows():
        _mix_route_compute(xp_ref, ysp_ref, y5p_ref, *refs)

    @pl.when(i == n_blocks)
    def _sample_rows():
        _mix_route_compute(xs_ref, yss_ref, y5s_ref, *refs)

    cnt_ref[...] = carry[...]


def _mix_route_compute(x_ref, ys_ref, y5_ref, wa_ref, wb_ref, nf_ref, wrh_ref, wrl_ref, br_ref,
                       x1_ref, xn_ref, rt_ref, _, carry):
    rows = x_ref.shape[0]
    x1 = x_ref[...] + _dot(ys_ref[...], wa_ref[...]) + _dot(y5_ref[...].astype(BF16), wb_ref[...])
    x1_ref[0:rows, :] = x1
    xn = _rms(x1, nf_ref[...])
    for j in range(SLAB_ROWS):
        xn_ref[0:rows, j, :] = xn[:, j * LANES:(j + 1) * LANES]

    xh = xn.astype(BF16)
    xl = (xn - xh.astype(F32)).astype(BF16)
    logits = _dot(xh, wrh_ref[...]) + _dot(xl, wrh_ref[...]) + _dot(xh, wrl_ref[...]) + br_ref[...]
    tm = logits.shape[0]
    lane = lax.broadcasted_iota(jnp.int32, logits.shape, 1).astype(F32)
    neg = -jnp.inf
    big = float(LANES)

    def first_max(v):
        m = jnp.max(v, axis=-1, keepdims=True)
        return m, jnp.min(jnp.where(v == m, lane, big), axis=-1, keepdims=True)

    coarse = lane < MOE_GROUPS
    mc, gsel = first_max(jnp.where(coarse, logits, neg))
    psel = 1.0 / jnp.sum(jnp.where(coarse, jnp.exp(logits - mc), 0.0), axis=-1, keepdims=True)
    lo = MOE_GROUPS + MOE_EPG * gsel
    lf = jnp.where((lane >= lo) & (lane < lo + MOE_EPG), logits, neg)
    m1, i1 = first_max(lf)
    m2, i2 = first_max(jnp.where(lane == i1, neg, lf))
    e2 = jnp.exp(m2 - m1)
    g1 = psel / (1.0 + e2)
    g2 = psel * e2 / (1.0 + e2)
    e_a, e_b = i1 - MOE_GROUPS, i2 - MOE_GROUPS

    pick_a, pick_b = lane == e_a, lane == e_b
    picks = jnp.where(pick_a | pick_b, 1.0, 0.0)
    earlier = lax.broadcasted_iota(jnp.int32, (tm, tm), 0) > lax.broadcasted_iota(jnp.int32, (tm, tm), 1)
    prior = _dot(earlier.astype(BF16), picks.astype(BF16)) + carry[...]
    rank_a = jnp.sum(jnp.where(pick_a, prior, 0.0), axis=-1, keepdims=True)
    rank_b = jnp.sum(jnp.where(pick_b, prior, 0.0), axis=-1, keepdims=True)
    carry[...] = prior[tm - 1:tm, :] + picks[tm - 1:tm, :]

    out = jnp.zeros_like(logits)
    for k, v in enumerate((e_a, e_b, g1, g2, rank_a, rank_b)):
        out = jnp.where(lane == float(k), v, out)
    rt_ref[0:rows, :] = out


def _mix_route(prompt, sample, consts, tm):
    n_prompt, n_sample = prompt[0].shape[0], sample[0].shape[0]
    assert n_prompt % tm == 0 and n_sample <= tm
    n_blocks = n_prompt // tm
    total_rows = n_prompt + n_sample
    row = lambda w: pl.BlockSpec((tm, w), lambda i: (jnp.minimum(i, n_blocks - 1), 0))
    out_row = lambda w: pl.BlockSpec((tm, w), lambda i: (i, 0))
    return pl.pallas_call(
        functools.partial(_mix_route_body, n_blocks),
        grid=(n_blocks + 1,),
        in_specs=([row(D_MODEL), row(SSD_WIDTH), row(S5_WIDTH)] + [_full_spec(a) for a in sample]
                  + [_full_spec(a) for a in consts]),
        out_specs=[out_row(D_MODEL), pl.BlockSpec((tm, SLAB_ROWS, LANES), lambda i: (i, 0, 0)),
                   out_row(LANES), pl.BlockSpec((1, LANES), lambda i: (0, 0))],
        out_shape=[jax.ShapeDtypeStruct((total_rows, D_MODEL), F32),
                   jax.ShapeDtypeStruct((total_rows, SLAB_ROWS, LANES), F32),
                   jax.ShapeDtypeStruct((total_rows, LANES), F32), jax.ShapeDtypeStruct((1, LANES), F32)],
        scratch_shapes=[pltpu.VMEM((1, LANES), F32)],
        compiler_params=pltpu.CompilerParams(dimension_semantics=("arbitrary",), vmem_limit_bytes=VMEM_LIMIT),
        name="mix_route",
    )(*prompt, *sample, *consts)


def _sc_mesh():
    return plsc.VectorSubcoreMesh(core_axis_name="c", subcore_axis_name="s")


def _sc_worker():
    return lax.axis_index("s") * SC_CORES + lax.axis_index("c")


def _sc_dispatch(xn, pos_a, pos_b, n_rows):
    n_tok = xn.shape[0]
    ch = SC_DISPATCH_ROWS
    assert n_tok % ch == 0

    @functools.partial(
        pl.kernel, mesh=_sc_mesh(),
        out_type=jax.ShapeDtypeStruct((n_rows, SLAB_ROWS, LANES), F32),
        scratch_types=[pltpu.VMEM((ch,), jnp.int32), pltpu.VMEM((ch,), jnp.int32),
                       pltpu.VMEM((ch, SLAB_ROWS, LANES), F32), pltpu.SemaphoreType.DMA])
    def push(xn_hbm, pa_hbm, pb_hbm, xs_hbm, ia, ib, rows, sem):
        @pl.loop(_sc_worker(), n_tok // ch, step=SC_WORKERS)
        def _(c):
            off = pl.multiple_of(c * ch, ch)
            pltpu.sync_copy(pa_hbm.at[pl.ds(off, ch)], ia)
            pltpu.sync_copy(pb_hbm.at[pl.ds(off, ch)], ib)
            pltpu.sync_copy(xn_hbm.at[pl.ds(off, ch)], rows)
            pltpu.async_copy(rows, xs_hbm.at[ia], sem).wait()
            pltpu.async_copy(rows, xs_hbm.at[ib], sem).wait()

    return push(xn, pos_a, pos_b)


def _sc_collect(ysorted, pos_flat):
    n_pick = pos_flat.shape[0]
    ch = SC_COLLECT_ROWS
    per_worker = n_pick // SC_WORKERS
    n_chunks = per_worker // ch
    assert n_pick % SC_WORKERS == 0 and per_worker % ch == 0

    @functools.partial(
        pl.kernel, mesh=_sc_mesh(),
        out_type=jax.ShapeDtypeStruct((n_pick, SLAB_ROWS, LANES), F32),
        scratch_types=[pltpu.VMEM((ch,), jnp.int32), pltpu.VMEM((ch,), jnp.int32),
                       pltpu.VMEM((ch, SLAB_ROWS, LANES), F32), pltpu.VMEM((ch, SLAB_ROWS, LANES), F32),
                       pltpu.SemaphoreType.DMA, pltpu.SemaphoreType.DMA])
    def pull(ys_hbm, pos_hbm, out_hbm, idx0, idx1, rows0, rows1, sem0, sem1):
        base = _sc_worker() * per_worker
        bufs = ((idx0, rows0, sem0), (idx1, rows1, sem1))

        def offset(j):
            return pl.multiple_of(base + j * ch, SUBLANES)

        def fetch(j, b):
            idx, rows, sem = bufs[b]
            pltpu.sync_copy(pos_hbm.at[pl.ds(offset(j), ch)], idx)
            pltpu.async_copy(ys_hbm.at[idx], rows, sem)

        def flush(j, b):
            idx, rows, sem = bufs[b]
            pltpu.make_async_copy(ys_hbm.at[idx], rows, sem).wait()
            pltpu.sync_copy(rows, out_hbm.at[pl.ds(offset(j), ch)])

        fetch(0, 0)

        @pl.loop(0, n_chunks // 2)
        def _(p):
            j = 2 * p
            fetch(j + 1, 1)
            flush(j, 0)

            @pl.when(j + 2 < n_chunks)
            def _():
                fetch(j + 2, 0)

            flush(j + 1, 1)

        if n_chunks % 2:
            flush(n_chunks - 1, 0)

    return pull(ysorted, pos_flat)


def _moe_ffn_body(te_ref, nused_ref, xs_hbm, wg_ref, wu_ref, wd_ref, ys_hbm, xbuf, ybuf, wgb, wub, wdb,
                  in_sems, out_sems):
    i = pl.program_id(0)
    n_used = nused_ref[0]

    def in_copy(tile, j):
        return pltpu.make_async_copy(xs_hbm.at[pl.ds(tile * MOE_TILE, MOE_TILE), j, :],
                                     xbuf.at[tile % 2, :, pl.ds(j * LANES, LANES)], in_sems.at[tile % 2, j])

    def out_copy(tile, j):
        return pltpu.make_async_copy(ybuf.at[tile % 2, :, pl.ds(j * LANES, LANES)],
                                     ys_hbm.at[pl.ds(tile * MOE_TILE, MOE_TILE), j, :], out_sems.at[tile % 2, j])

    @pl.when(i == 0)
    def _first_fetch():
        for j in range(SLAB_ROWS):
            in_copy(0, j).start()

    @pl.when(i + 1 < n_used)
    def _prefetch():
        for j in range(SLAB_ROWS):
            in_copy(i + 1, j).start()

    @pl.when(i < n_used)
    def _tile():
        @pl.when((i == 0) | (te_ref[i] != te_ref[jnp.maximum(i - 1, 0)]))
        def _cast_weights():
            wgb[...] = wg_ref[0].astype(BF16)
            wub[...] = wu_ref[0].astype(BF16)
            wdb[...] = wd_ref[0].astype(BF16)

        for j in range(SLAB_ROWS):
            in_copy(i, j).wait()
        x = xbuf[i % 2].astype(BF16)
        gate = _dot(x, wgb[...])
        hmid = (gate * jax.nn.sigmoid(gate)) * _dot(x, wub[...])
        ybuf[i % 2] = _dot(hmid.astype(BF16), wdb[...])
        for j in range(SLAB_ROWS):
            out_copy(i, j).start()

    @pl.when((i > 0) & (i <= n_used))
    def _wait_previous_out():
        for j in range(SLAB_ROWS):
            out_copy(i - 1, j).wait()


def _moe_ffn(tile_expert, n_used, xsorted, w_gate, w_up, w_down):
    n_steps = tile_expert.shape[0]
    wspec = lambda s: pl.BlockSpec((1,) + s, lambda i, te, nu: (te[i], 0, 0))
    buf = pltpu.VMEM((2, MOE_TILE, D_MODEL), F32)
    return pl.pallas_call(
        _moe_ffn_body,
        grid_spec=pltpu.PrefetchScalarGridSpec(
            num_scalar_prefetch=2,
            grid=(n_steps,),
            in_specs=[pl.BlockSpec(memory_space=pl.ANY),
                      wspec((D_MODEL, MOE_D_FF)), wspec((D_MODEL, MOE_D_FF)), wspec((MOE_D_FF, D_MODEL))],
            out_specs=pl.BlockSpec(memory_space=pl.ANY),
            scratch_shapes=[buf, buf,
                            pltpu.VMEM((D_MODEL, MOE_D_FF), BF16), pltpu.VMEM((D_MODEL, MOE_D_FF), BF16),
                            pltpu.VMEM((MOE_D_FF, D_MODEL), BF16),
                            pltpu.SemaphoreType.DMA((2, SLAB_ROWS)), pltpu.SemaphoreType.DMA((2, SLAB_ROWS))]),
        out_shape=jax.ShapeDtypeStruct(xsorted.shape, F32),
        compiler_params=pltpu.CompilerParams(dimension_semantics=("arbitrary",), vmem_limit_bytes=VMEM_LIMIT),
        name="moe_ffn",
    )(tile_expert, n_used, xsorted, w_gate, w_up, w_down)


def _combine_body(x1_ref, rt_ref, ya_ref, yb_ref, nf_ref, out_ref):
    rt = rt_ref[...]
    x1 = x1_ref[...]
    x2 = jnp.concatenate(
        [x1[:, j * LANES:(j + 1) * LANES] + rt[:, 2:3] * ya_ref[0, :, j, :] + rt[:, 3:4] * yb_ref[0, :, j, :]
         for j in range(SLAB_ROWS)], axis=-1)
    out_ref[...] = _rms(x2, nf_ref[...])


def _combine(x1, rt, y_picks, nf, tm, rows, row_block_offset):
    row = lambda w: pl.BlockSpec((tm, w), lambda i: (i + row_block_offset, 0))
    pick = lambda k: pl.BlockSpec((1, tm, SLAB_ROWS, LANES), lambda i: (k, i + row_block_offset, 0, 0))
    return pl.pallas_call(
        _combine_body,
        grid=(rows // tm,),
        in_specs=[row(D_MODEL), row(LANES), pick(0), pick(1),
                  pl.BlockSpec((1, D_MODEL), lambda i: (0, 0))],
        out_specs=pl.BlockSpec((tm, D_MODEL), lambda i: (i, 0)),
        out_shape=jax.ShapeDtypeStruct((rows, D_MODEL), F32),
        compiler_params=pltpu.CompilerParams(dimension_semantics=("parallel",), vmem_limit_bytes=VMEM_LIMIT),
        name="moe_combine",
    )(x1, rt, y_picks, y_picks, nf)


def _route_tables(counts, eid, rank, n_tiles):
    tiles_per = (counts + MOE_TILE - 1) // MOE_TILE
    tile_end = jnp.cumsum(tiles_per)
    pstart = (tile_end - tiles_per) * MOE_TILE
    pos = [jnp.take(pstart, e) + r for e, r in zip(eid, rank)]
    n_used = tile_end[-1]
    tiles = jnp.arange(n_tiles, dtype=jnp.int32)
    tile_expert = jnp.sum((tile_end[None, :] <= jnp.minimum(tiles, n_used - 1)[:, None]).astype(jnp.int32), axis=1)
    return pos, tile_expert, n_used.reshape(1).astype(jnp.int32)


def _s5_tables(a_re, a_im, log_dt, b_re, b_im, c_re, c_im):
    dt = jnp.exp(log_dt)[:, None]
    mag = jnp.exp(a_re * dt)
    ab_re = mag * jnp.cos(a_im * dt)
    ab_im = mag * jnp.sin(a_im * dt)
    den = a_re * a_re + a_im * a_im
    nr = ab_re - 1.0
    q_re = (nr * a_re + ab_im * a_im) / den
    q_im = (ab_im * a_re - nr * a_im) / den
    bb_re = q_re[..., None] * b_re - q_im[..., None] * b_im
    bb_im = q_re[..., None] * b_im + q_im[..., None] * b_re
    nblk = S5_GROUPS // 16
    kw, nw = 16 * S5_GROUP_CH, 16 * S5_STATE
    same_group = (jnp.arange(kw)[:, None] // S5_GROUP_CH) == (jnp.arange(nw)[None, :] // S5_STATE)

    def in_map(bb):
        rows = bb.reshape(nblk, 16, S5_STATE, S5_GROUP_CH).transpose(0, 1, 3, 2).reshape(nblk, kw, S5_STATE)
        return jnp.where(same_group, jnp.tile(rows, (1, 1, 16)), 0.0)

    def out_map(cc):
        cols = cc.reshape(nblk, 16, S5_GROUP_CH, S5_STATE).transpose(0, 3, 1, 2).reshape(nblk, S5_STATE, kw)
        return jnp.where(same_group.T, jnp.tile(cols, (1, 16, 1)), 0.0)

    wb = jnp.concatenate([in_map(bb_re), in_map(bb_im)], axis=-1).astype(BF16)
    return (wb, ab_re.reshape(1, S5_LANES), ab_im.reshape(1, S5_LANES),
            out_map(c_re).astype(BF16), out_map(-c_im).astype(BF16))


def kernel(x_prompt, x_sample, state_ssd_conv, state_ssd_ssm, state_s5_re, state_s5_im, meta_tokens, norm_mix, w_in, conv_w, conv_b, dt_bias, a_log, d_ssd, ssd_norm, s5_a_re, s5_a_im, s5_log_dt, s5_b_re, s5_b_im, s5_c_re, s5_c_im, s5_d, w_glu, b_glu, s5_norm, w_out, norm_ffn, router_coarse_w, router_coarse_b, router_fine_w, router_fine_b, w_gate, w_up, w_down, norm_final):
    bp, seq, _ = x_prompt.shape
    bs = x_sample.shape[0]
    n_prompt = bp * seq
    n_tok = n_prompt + bs
    row2 = lambda v: v.reshape(1, -1)
    pad_heads = lambda v: jnp.pad(v, (0, LANES - SSD_HEADS)).reshape(1, LANES)

    w = w_in[0]
    o1, o2, o3 = SSD_WIDTH, SSD_WIDTH + SSD_CONV_DIM, SSD_WIDTH + SSD_CONV_DIM + SSD_HEADS
    wz, wx, wu = w[:, :o1].astype(BF16), w[:, o1:o2].astype(BF16), w[:, o3:].astype(BF16)
    wdt = jnp.pad(w[:, o2:o3], ((0, 0), (0, LANES - SSD_HEADS))).astype(BF16)
    g_mix = row2(norm_mix[0])
    cw, cb = conv_w[0], row2(conv_b[0])
    dtb, alog = pad_heads(dt_bias[0]), pad_heads(a_log[0])
    dexp = row2(jnp.repeat(d_ssd[0], SSD_HEAD_DIM))
    snrm = row2(ssd_norm[0])
    eexp = (jnp.arange(LANES)[:, None] == (jnp.arange(SSD_WIDTH) // SSD_HEAD_DIM)[None, :]).astype(BF16)
    wb5, ab_re, ab_im, wcr, wci = _s5_tables(s5_a_re[0], s5_a_im[0], s5_log_dt[0], s5_b_re[0], s5_b_im[0],
                                             s5_c_re[0], s5_c_im[0])
    d5, wglu, bglu, nrm5 = row2(s5_d[0]), w_glu[0].astype(BF16), row2(b_glu[0]), row2(s5_norm[0])
    wo_a, wo_b = w_out[0][:SSD_WIDTH].astype(BF16), w_out[0][SSD_WIDTH:].astype(BF16)
    w_r = jnp.concatenate([router_coarse_w[0], router_fine_w[0].transpose(1, 0, 2).reshape(D_MODEL, MOE_EXPERTS)], axis=1)
    w_r = jnp.pad(w_r, ((0, 0), (0, LANES - w_r.shape[1])))
    wrh = w_r.astype(BF16)
    wrl = (w_r - wrh.astype(F32)).astype(BF16)
    b_r = jnp.concatenate([router_coarse_b[0], router_fine_b[0].reshape(-1)])
    b_r = jnp.pad(b_r, (0, LANES - b_r.shape[0])).reshape(1, LANES)

    zp, xbcp, dtp, up = _in_proj(x_prompt.reshape(n_prompt, D_MODEL), g_mix, wz, wx, wdt, wu, TOK_TILE, BF16, F32)
    xsm = jnp.concatenate([x_sample.reshape(bs, D_MODEL), meta_tokens], axis=0)
    zs, xbcs, dts, us = _in_proj(xsm, g_mix, wz, wx, wdt, wu, xsm.shape[0], F32, F32)

    front = SSD_CHUNK - N_META
    padf = lambda a: jnp.pad(a[bs:], ((front, 0), (0, 0)))[None]
    gw = SSD_HPG * SSD_HEAD_DIM
    ssd_consts = (cw, cb, dtb, alog, dexp, snrm, eexp)
    _, ctail_m, _, ht_m = _ssd_chunked(
        padf(xbcs), padf(dts), jnp.zeros((1, SSD_CHUNK, SSD_WIDTH), F32),
        jnp.zeros((1, SUBLANES, SSD_CONV_DIM), F32), jnp.zeros((1, SSD_GROUPS, SSD_STATE, gw), F32),
        *ssd_consts, mask_rows=front)
    y_ssd_p, ctail_p, ssm_p, _ = _ssd_chunked(
        xbcp.reshape(bp, seq, SSD_CONV_DIM), dtp.reshape(bp, seq, LANES), zp.reshape(bp, seq, SSD_WIDTH),
        ctail_m, ht_m, *ssd_consts, mask_rows=0)

    abr8, abi8 = jnp.broadcast_to(ab_re, (bp, S5_LANES)), jnp.broadcast_to(ab_im, (bp, S5_LANES))
    um8 = jnp.repeat(us[bs:], bp, axis=0).astype(BF16)
    y_s5_p, s5re_p, s5im_p = _s5_seq(up.reshape(bp, seq, S5_WIDTH), um8, wb5, abr8, abi8,
                                     wcr, wci, d5, wglu, bglu, nrm5)

    cst = state_ssd_conv[0]
    xt_s, dt_s, dec_s, bc, xs_s = _ssd_step_prep(xbcs[:bs], cst[:, 0], cst[:, 1], cst[:, 2], dts[:bs],
                                                 cw, cb, dtb, alog)
    ssm_s, y_core = _ssd_step(dt_s[:, :SSD_HEADS].reshape(-1), dec_s[:, :SSD_HEADS].reshape(-1),
                              state_ssd_ssm[0], xt_s, bc)
    y_ssd_s, y_s5_s, s5re_s, s5im_s = _sample_post(
        y_core, xs_s, zs[:bs], dexp, snrm, us[:bs], state_s5_re[0].reshape(bs, S5_LANES),
        state_s5_im[0].reshape(bs, S5_LANES), wb5, ab_re, ab_im, wcr, wci, d5, wglu, bglu, nrm5)

    route_consts = (wo_a, wo_b, row2(norm_ffn[0]), wrh, wrl, b_r)
    x1, xn, rt, counts = _mix_route(
        (x_prompt.reshape(n_prompt, D_MODEL), y_ssd_p.reshape(n_prompt, SSD_WIDTH), y_s5_p.reshape(n_prompt, S5_WIDTH)),
        (x_sample.reshape(bs, D_MODEL), y_ssd_s, y_s5_s), route_consts, TOK_TILE)

    n_tiles = -(-2 * n_tok // MOE_TILE) + MOE_EXPERTS
    lane_i32 = lambda k: rt[:, k].astype(jnp.int32)
    eid = [jnp.clip(lane_i32(k), 0, MOE_EXPERTS - 1) for k in (0, 1)]
    (pos_a, pos_b), tile_expert, n_used = _route_tables(counts[0, :MOE_EXPERTS].astype(jnp.int32), eid,
                                                        [lane_i32(4), lane_i32(5)], n_tiles + 1)
    xsorted = _sc_dispatch(xn, pos_a, pos_b, n_tiles * MOE_TILE)
    ysorted = _moe_ffn(tile_expert, n_used, xsorted, w_gate[0], w_up[0], w_down[0])
    y_picks = _sc_collect(ysorted, jnp.concatenate([pos_a, pos_b])).reshape(2, n_tok, SLAB_ROWS, LANES)
    nfin = row2(norm_final)
    y_p = _combine(x1, rt, y_picks, nfin, MOE_TILE, n_prompt, 0)
    y_s = _combine(x1, rt, y_picks, nfin, bs, bs, n_prompt // bs)

    s5_state = lambda a, b: a.reshape(1, b, S5_GROUPS, S5_STATE)
    new_conv_s = jnp.stack([cst[:, 1], cst[:, 2], xbcs[:bs]], axis=1)[None]
    return (y_p.reshape(bp, seq, D_MODEL), y_s.reshape(bs, 1, D_MODEL),
            ctail_p[:, SUBLANES - (SSD_CONV - 1):][None], ssm_p[None], s5_state(s5re_p, bp), s5_state(s5im_p, bp),
            new_conv_s, ssm_s[None], s5_state(s5re_s, bs), s5_state(s5im_s, bs))
```

```python
import functools

import jax
import jax.numpy as jnp
from jax import lax
from jax.experimental import pallas as pl
from jax.experimental.pallas import tpu as pltpu
from jax.experimental.pallas import tpu_sc as plsc

F32, BF16 = jnp.float32, jnp.bfloat16

D_MODEL = 1024
N_META = 16
SSD_WIDTH = 1024
SSD_HEAD_DIM = 64
SSD_HEADS = 16
SSD_GROUPS = 2
SSD_HPG = SSD_HEADS // SSD_GROUPS
SSD_STATE = 128
SSD_CONV = 4
SSD_CHUNK = 128
SSD_CONV_DIM = SSD_WIDTH + 2 * SSD_GROUPS * SSD_STATE
S5_WIDTH = 1024
S5_GROUP_CH = 16
S5_GROUPS = 64
S5_STATE = 64
S5_LANES = S5_GROUPS * S5_STATE
MOE_GROUPS = 4
MOE_EPG = 8
MOE_EXPERTS = MOE_GROUPS * MOE_EPG
MOE_D_FF = 512
EPS = 1e-6

LANES = 128
SUBLANES = 8
VMEM_LIMIT = 56 * 1024 * 1024

S5_TIME_TILE = 64
S5_SCAN_LANES = 512
MOE_TILE = 256
SLAB_ROWS = D_MODEL // LANES
SC_CORES = 2
SC_SUBCORES = 16
SC_WORKERS = SC_CORES * SC_SUBCORES
SC_DISPATCH_ROWS = 64
SC_COLLECT_ROWS = 24
TOK_TILE = 512


def _dot(a, b):
    return jnp.dot(a, b, preferred_element_type=F32)


def _rms(x, g):
    return x * lax.rsqrt(jnp.mean(x * x, axis=-1, keepdims=True) + EPS) * g


def _softplus(x):
    return jnp.maximum(x, 0.0) + jnp.log1p(jnp.exp(-jnp.abs(x)))


def _split3(x):
    hi = x.astype(BF16)
    r = x - hi.astype(F32)
    mid = r.astype(BF16)
    lo = (r - mid.astype(F32)).astype(BF16)
    return hi, mid, lo


def _dot3(x, w):
    hi, mid, lo = _split3(x)
    return _dot(hi, w) + _dot(mid, w) + _dot(lo, w)


def _dot3_left(w, x):
    hi, mid, lo = _split3(x)
    return _dot(w, hi) + _dot(w, mid) + _dot(w, lo)


def _full_spec(a):
    nd = a.ndim
    return pl.BlockSpec(a.shape, lambda *_: (0,) * nd)


def _resident_spec(a):
    nd = a.ndim
    return pl.BlockSpec(a.shape, lambda *_: (0,) * nd, pipeline_mode=pl.Buffered(1))


def _in_proj_body(x_ref, g_ref, wz_ref, wx_ref, wdt_ref, wu_ref, z_ref, xbc_ref, dt_ref, u_ref):
    xb = _rms(x_ref[...], g_ref[...]).astype(BF16)
    z_ref[...] = _dot(xb, wz_ref[...]).astype(z_ref.dtype)
    xbc_ref[...] = _dot(xb, wx_ref[...]).astype(xbc_ref.dtype)
    dt_ref[...] = _dot(xb, wdt_ref[...])
    u_ref[...] = _dot(xb, wu_ref[...]).astype(u_ref.dtype)


def _in_proj(x2d, g, wz, wx, wdt, wu, tm, act_dtype, u_dtype):
    rows = x2d.shape[0]
    row = lambda w: pl.BlockSpec((tm, w), lambda i: (i, 0))
    return pl.pallas_call(
        _in_proj_body,
        grid=(rows // tm,),
        in_specs=[row(D_MODEL), _full_spec(g), _full_spec(wz), _full_spec(wx), _full_spec(wdt), _full_spec(wu)],
        out_specs=[row(SSD_WIDTH), row(SSD_CONV_DIM), row(LANES), row(S5_WIDTH)],
        out_shape=[jax.ShapeDtypeStruct((rows, SSD_WIDTH), act_dtype),
                   jax.ShapeDtypeStruct((rows, SSD_CONV_DIM), act_dtype),
                   jax.ShapeDtypeStruct((rows, LANES), F32),
                   jax.ShapeDtypeStruct((rows, S5_WIDTH), u_dtype)],
        compiler_params=pltpu.CompilerParams(dimension_semantics=("parallel",), vmem_limit_bytes=VMEM_LIMIT),
        name="in_proj",
    )(x2d, g, wz, wx, wdt, wu)


def _ssd_body(mask_rows, xbc_ref, dt_ref, z_ref, cinit_ref, hinit_ref, cw_ref, cb_ref, dtb_ref, alog_ref,
              dexp_ref, nrm_ref, eexp_ref, y_ref, ctail_ref, st_ref, hto_ref, xwin, hT):
    c = pl.program_id(1)
    L = SSD_CHUNK

    @pl.when(c == 0)
    def _init():
        xwin[0:SUBLANES, :] = cinit_ref[0]
        hT[...] = hinit_ref[0]

    xwin[SUBLANES:SUBLANES + L, :] = xbc_ref[0].astype(F32)
    acc = cb_ref[...]
    for k in range(SSD_CONV):
        off = SUBLANES - (SSD_CONV - 1) + k
        acc = acc + xwin[off:off + L, :] * cw_ref[k:k + 1, :]
    tail = xwin[L:L + SUBLANES, :]
    xwin[0:SUBLANES, :] = tail
    ctail_ref[0] = tail

    xact = acc * jax.nn.sigmoid(acc)
    dt = _softplus(dt_ref[0] + dtb_ref[...])
    if mask_rows:
        valid = lax.broadcasted_iota(jnp.int32, (L, 1), 0) >= mask_rows
        xact = jnp.where(valid, xact, 0.0)
        dt = jnp.where(valid, dt, 0.0)

    a_neg = -jnp.exp(alog_ref[...])
    dA = dt * a_neg
    row_i = lax.broadcasted_iota(jnp.int32, (L, L), 0)
    col_i = lax.broadcasted_iota(jnp.int32, (L, L), 1)
    causal = row_i >= col_i
    tril = causal.astype(BF16)
    cs = _dot3_left(tril, dA)
    csT = cs.T
    dtT = dt.T
    ecs = jnp.exp(cs)
    wdec = jnp.exp(cs[L - 1:L, :] - cs) * dt
    eexp = eexp_ref[...]
    ecs_e = _dot3(ecs, eexp)
    wdec_e = _dot3(wdec, eexp)
    lane = lax.broadcasted_iota(jnp.int32, (L, LANES), 1)
    first_half = lane < SSD_HEAD_DIM

    gw = SSD_HPG * SSD_HEAD_DIM
    y_groups = []
    for g in range(SSD_GROUPS):
        b_g = xact[:, SSD_WIDTH + g * SSD_STATE: SSD_WIDTH + (g + 1) * SSD_STATE]
        c_g = xact[:, SSD_WIDTH + (SSD_GROUPS + g) * SSD_STATE: SSD_WIDTH + (SSD_GROUPS + g + 1) * SSD_STATE]
        b_b = b_g.astype(BF16)
        c_b = c_g.astype(BF16)
        cb = lax.dot_general(c_b, b_b, (((1,), (1,)), ((), ())), preferred_element_type=F32)
        xs_g = xact[:, g * gw:(g + 1) * gw]
        h_prev = hT[g]
        y_off = _dot(c_b, h_prev.astype(BF16)) * ecs_e[:, g * gw:(g + 1) * gw]
        xdec = (xs_g * wdec_e[:, g * gw:(g + 1) * gw]).astype(BF16)
        hT[g] = h_prev * ecs_e[L - 1:L, g * gw:(g + 1) * gw] + _dot(b_g.T.astype(BF16), xdec)
        pieces = []
        for j in range(SSD_HPG // 2):
            xs_pair = xs_g[:, j * LANES:(j + 1) * LANES]
            halves = (jnp.where(first_half, xs_pair, 0.0).astype(BF16),
                      jnp.where(first_half, 0.0, xs_pair).astype(BF16))
            yd = None
            for t in range(2):
                h = g * SSD_HPG + 2 * j + t
                seg = cs[:, h:h + 1] - csT[h:h + 1, :]
                lmat = jnp.exp(jnp.where(causal, seg, -jnp.inf))
                m = (cb * lmat * dtT[h:h + 1, :]).astype(BF16)
                part = _dot(m, halves[t])
                yd = part if yd is None else yd + part
            pieces.append(yd)
        y_groups.append(jnp.concatenate(pieces, axis=-1) + y_off + dexp_ref[:, g * gw:(g + 1) * gw] * xs_g)
    y = jnp.concatenate(y_groups, axis=-1)
    z = z_ref[0].astype(F32)
    y_ref[0] = _rms(y * (z * jax.nn.sigmoid(z)), nrm_ref[...]).astype(y_ref.dtype)

    @pl.when(c == pl.num_programs(1) - 1)
    def _emit():
        hto_ref[0] = hT[...]
        for g in range(SSD_GROUPS):
            t = hT[g].T
            for k in range(SSD_HPG):
                st_ref[0, g * SSD_HPG + k] = t[k * SSD_HEAD_DIM:(k + 1) * SSD_HEAD_DIM, :]


def _ssd_chunked(xbc, dt, z, cinit, hinit, cw, cb, dtb, alog, dexp, nrm, eexp, mask_rows):
    bsz, seq, _ = xbc.shape
    nc = seq // SSD_CHUNK
    gw = SSD_HPG * SSD_HEAD_DIM
    blk = lambda w: pl.BlockSpec((1, SSD_CHUNK, w), lambda b, c: (b, c, 0))
    return pl.pallas_call(
        functools.partial(_ssd_body, mask_rows),
        grid=(bsz, nc),
        in_specs=[blk(SSD_CONV_DIM), blk(LANES), blk(SSD_WIDTH),
                  pl.BlockSpec((1, SUBLANES, SSD_CONV_DIM), lambda b, c: (0, 0, 0)),
                  pl.BlockSpec((1, SSD_GROUPS, SSD_STATE, gw), lambda b, c: (0, 0, 0, 0)),
                  _full_spec(cw), _full_spec(cb), _full_spec(dtb), _full_spec(alog),
                  _full_spec(dexp), _full_spec(nrm), _full_spec(eexp)],
        out_specs=[blk(SSD_WIDTH),
                   pl.BlockSpec((1, SUBLANES, SSD_CONV_DIM), lambda b, c: (b, 0, 0)),
                   pl.BlockSpec((1, SSD_HEADS, SSD_HEAD_DIM, SSD_STATE), lambda b, c: (b, 0, 0, 0)),
                   pl.BlockSpec((1, SSD_GROUPS, SSD_STATE, gw), lambda b, c: (b, 0, 0, 0))],
        out_shape=[jax.ShapeDtypeStruct((bsz, seq, SSD_WIDTH), BF16),
                   jax.ShapeDtypeStruct((bsz, SUBLANES, SSD_CONV_DIM), F32),
                   jax.ShapeDtypeStruct((bsz, SSD_HEADS, SSD_HEAD_DIM, SSD_STATE), F32),
                   jax.ShapeDtypeStruct((bsz, SSD_GROUPS, SSD_STATE, gw), F32)],
        scratch_shapes=[pltpu.VMEM((SUBLANES + SSD_CHUNK, SSD_CONV_DIM), F32),
                        pltpu.VMEM((SSD_GROUPS, SSD_STATE, gw), F32)],
        compiler_params=pltpu.CompilerParams(dimension_semantics=("parallel", "arbitrary"),
                                             vmem_limit_bytes=VMEM_LIMIT),
        name="ssd_chunked",
    )(xbc, dt, z, cinit, hinit, cw, cb, dtb, alog, dexp, nrm, eexp)


def _ssd_step_prep_body(xbc_ref, c0_ref, c1_ref, c2_ref, dt_ref, cw_ref, cb_ref, dtb_ref, alog_ref,
                        xt_ref, dt_out_ref, dec_ref, bc_ref, xs_ref):
    acc = cb_ref[...]
    for k, r in enumerate((c0_ref, c1_ref, c2_ref, xbc_ref)):
        acc = acc + r[...] * cw_ref[k:k + 1, :]
    xact = acc * jax.nn.sigmoid(acc)
    xs = xact[:, :SSD_WIDTH]
    dt = _softplus(dt_ref[...] + dtb_ref[...])
    dt_out_ref[...] = dt
    dec_ref[...] = jnp.exp(dt * -jnp.exp(alog_ref[...]))
    bc_ref[...] = xact[:, SSD_WIDTH:]
    xs_ref[...] = xs
    xt_ref[...] = xs.T.astype(xt_ref.dtype)


def _ssd_step_prep(xbc, c0, c1, c2, dt, cw, cb, dtb, alog):
    n = xbc.shape[0]
    args = (xbc, c0, c1, c2, dt, cw, cb, dtb, alog)
    spec = lambda r, w: pl.BlockSpec((r, w), lambda: (0, 0))
    return pl.pallas_call(
        _ssd_step_prep_body,
        in_specs=[_full_spec(a) for a in args],
        out_specs=[spec(SSD_WIDTH, n), spec(n, LANES), spec(n, LANES), spec(n, 2 * SSD_GROUPS * SSD_STATE),
                   spec(n, SSD_WIDTH)],
        out_shape=[jax.ShapeDtypeStruct((SSD_WIDTH, n), BF16), jax.ShapeDtypeStruct((n, LANES), F32),
                   jax.ShapeDtypeStruct((n, LANES), F32),
                   jax.ShapeDtypeStruct((n, 2 * SSD_GROUPS * SSD_STATE), F32),
                   jax.ShapeDtypeStruct((n, SSD_WIDTH), F32)],
        compiler_params=pltpu.CompilerParams(vmem_limit_bytes=VMEM_LIMIT),
        name="ssd_step_prep",
    )(*args)


def _ssd_step_body(dt_ref, dec_ref, st_ref, xt_ref, bc_ref, so_ref, y_ref):
    n = xt_ref.shape[1]
    gw = SSD_HPG * SSD_HEAD_DIM
    blk = pl.program_id(0)
    seq_id = lax.broadcasted_iota(jnp.int32, (n, SSD_STATE), 0)
    sub_id = lax.broadcasted_iota(jnp.int32, (SUBLANES, gw), 0)
    base = pl.multiple_of(blk * SUBLANES, SUBLANES)
    y_acc = [jnp.zeros((SUBLANES, gw), F32) for _ in range(SSD_GROUPS)]
    for i in range(SUBLANES):
        s = blk * SUBLANES + i
        for g in range(SSD_GROUPS):
            b_all = bc_ref[:, g * SSD_STATE:(g + 1) * SSD_STATE]
            rhs = jnp.where(seq_id == s, b_all, 0.0).astype(BF16)
            outer = _dot(xt_ref[g * gw:(g + 1) * gw, :], rhs)
            news = []
            for k in range(SSD_HPG):
                h = g * SSD_HPG + k
                new = (dec_ref[s * SSD_HEADS + h] * st_ref[i, h]
                       + dt_ref[s * SSD_HEADS + h] * outer[k * SSD_HEAD_DIM:(k + 1) * SSD_HEAD_DIM, :])
                so_ref[i, h] = new
                news.append(new)
            new_g = jnp.concatenate(news, axis=0).astype(BF16)
            c_lo = (SSD_GROUPS + g) * SSD_STATE
            c_blk = bc_ref[pl.ds(base, SUBLANES), c_lo:c_lo + SSD_STATE].astype(BF16)
            r = lax.dot_general(c_blk, new_g, (((1,), (1,)), ((), ())), preferred_element_type=F32)
            y_acc[g] = y_acc[g] + jnp.where(sub_id == i, r, 0.0)
    y_ref[...] = jnp.concatenate(y_acc, axis=-1)


def _ssd_step(dt_flat, dec_flat, state, xt, bc):
    n = state.shape[0]
    st_spec = pl.BlockSpec((SUBLANES, SSD_HEADS, SSD_HEAD_DIM, SSD_STATE), lambda i, *_: (i, 0, 0, 0))
    return pl.pallas_call(
        _ssd_step_body,
        grid_spec=pltpu.PrefetchScalarGridSpec(
            num_scalar_prefetch=2,
            grid=(n // SUBLANES,),
            in_specs=[st_spec, pl.BlockSpec(xt.shape, lambda i, *_: (0, 0)),
                      pl.BlockSpec(bc.shape, lambda i, *_: (0, 0))],
            out_specs=[st_spec, pl.BlockSpec((SUBLANES, SSD_WIDTH), lambda i, *_: (i, 0))]),
        out_shape=[jax.ShapeDtypeStruct(state.shape, F32), jax.ShapeDtypeStruct((n, SSD_WIDTH), F32)],
        compiler_params=pltpu.CompilerParams(dimension_semantics=("parallel",), vmem_limit_bytes=VMEM_LIMIT),
        name="ssd_step",
    )(dt_flat, dec_flat, state, xt, bc)


def _s5_project_in(u_b16, wb_ref, store):
    kw = 16 * S5_GROUP_CH
    nw = 16 * S5_STATE
    for j in range(S5_WIDTH // kw):
        r = _dot(u_b16[:, j * kw:(j + 1) * kw], wb_ref[j])
        store(j, r[:, :nw], r[:, nw:])


def _s5_tail(hre_of, him_of, u_f32, wcr_ref, wci_ref, d_ref, wglu_ref, bglu_ref, nrm_ref):
    cols = []
    for j in range(wcr_ref.shape[0]):
        cols.append(_dot(hre_of(j).astype(BF16), wcr_ref[j]) + _dot(him_of(j).astype(BF16), wci_ref[j]))
    return _s5_finish(cols, u_f32, d_ref, wglu_ref, bglu_ref, nrm_ref)


def _s5_finish(cols, u_f32, d_ref, wglu_ref, bglu_ref, nrm_ref):
    y = jnp.concatenate(cols, axis=-1) + d_ref[...] * u_f32
    y = jax.nn.gelu(y)
    y = y * jax.nn.sigmoid(_dot(y.astype(BF16), wglu_ref[...]) + bglu_ref[...])
    return _rms(y, nrm_ref[...])


def _s5_seq_body(u_hbm, um_ref, wb_ref, abr_ref, abi_ref, wcr_ref, wci_ref, d_ref, wglu_ref, bglu_ref, nrm_ref,
                 y_hbm, sre_ref, sim_ref, ubuf, ybuf, bu, h, in_sems, out_sems):
    j = pl.program_id(0)
    last = pl.num_programs(0) - 1
    lc, bsz = ubuf.shape[1], ubuf.shape[2]
    rows = lc * bsz
    nw = 16 * S5_STATE

    def in_copy(step, b):
        return pltpu.make_async_copy(u_hbm.at[b, pl.ds(step * lc, lc), :], ubuf.at[step % 2, :, b, :],
                                     in_sems.at[step % 2, b])

    def out_copy(step, b):
        return pltpu.make_async_copy(ybuf.at[step % 2, :, b, :], y_hbm.at[b, pl.ds(step * lc, lc), :],
                                     out_sems.at[step % 2, b])

    def project_in(u_b16, nrows):
        def store(jj, re, im):
            bu[0:nrows, jj * nw:(jj + 1) * nw] = re
            bu[0:nrows, S5_LANES + jj * nw:S5_LANES + (jj + 1) * nw] = im
        _s5_project_in(u_b16, wb_ref, store)

    def scan(nsteps):
        for k in range(S5_LANES // S5_SCAN_LANES):
            sl_r = pl.ds(k * S5_SCAN_LANES, S5_SCAN_LANES)
            sl_i = pl.ds(S5_LANES + k * S5_SCAN_LANES, S5_SCAN_LANES)
            ar = abr_ref[:, sl_r]
            ai = abi_ref[:, sl_r]

            def step(l, carry):
                hr, hi = carry
                slab = pl.ds(pl.multiple_of(l * bsz, bsz), bsz)
                nr = ar * hr - ai * hi + bu[slab, sl_r]
                ni = ar * hi + ai * hr + bu[slab, sl_i]
                bu[slab, sl_r] = nr
                bu[slab, sl_i] = ni
                return nr, ni

            hr, hi = lax.fori_loop(0, nsteps, step, (h[:, sl_r], h[:, sl_i]))
            h[:, sl_r] = hr
            h[:, sl_i] = hi

    @pl.when(j == 0)
    def _first():
        for b in range(bsz):
            in_copy(0, b).start()
        h[...] = jnp.zeros_like(h)
        project_in(um_ref[...], N_META * bsz)
        scan(N_META)

    @pl.when(j < last)
    def _prefetch():
        for b in range(bsz):
            in_copy(j + 1, b).start()

    for b in range(bsz):
        in_copy(j, b).wait()
    u2 = ubuf[j % 2].reshape(rows, S5_WIDTH)
    u_b16 = u2.astype(BF16)
    kw = 16 * S5_GROUP_CH

    def project_block(jj):
        r = _dot(u_b16[:, jj * kw:(jj + 1) * kw], wb_ref[jj])
        bu[0:rows, jj * nw:(jj + 1) * nw] = r[:, :nw]
        bu[0:rows, S5_LANES + jj * nw:S5_LANES + (jj + 1) * nw] = r[:, nw:]

    def scan_block(jj):
        for k in range(nw // S5_SCAN_LANES):
            lo = jj * nw + k * S5_SCAN_LANES
            sl_r = slice(lo, lo + S5_SCAN_LANES)
            sl_i = slice(S5_LANES + lo, S5_LANES + lo + S5_SCAN_LANES)
            ar, ai = abr_ref[:, sl_r], abi_ref[:, sl_r]
            hr, hi = h[:, sl_r], h[:, sl_i]
            for l in range(lc):
                slab = slice(l * bsz, (l + 1) * bsz)
                hr, hi = (ar * hr - ai * hi + bu[slab, sl_r], ar * hi + ai * hr + bu[slab, sl_i])
                bu[slab, sl_r] = hr
                bu[slab, sl_i] = hi
            h[:, sl_r] = hr
            h[:, sl_i] = hi

    def readout_block(jj):
        return (_dot(bu[:, jj * nw:(jj + 1) * nw].astype(BF16), wcr_ref[jj])
                + _dot(bu[:, S5_LANES + jj * nw:S5_LANES + (jj + 1) * nw].astype(BF16), wci_ref[jj]))

    n_blocks = S5_WIDTH // kw
    project_block(0)
    cols = []
    for jj in range(n_blocks):
        if jj + 1 < n_blocks:
            project_block(jj + 1)
        scan_block(jj)
        cols.append(readout_block(jj))
    y = _s5_finish(cols, u2, d_ref, wglu_ref, bglu_ref, nrm_ref)
    ybuf[j % 2] = y.reshape(lc, bsz, S5_WIDTH)
    for b in range(bsz):
        out_copy(j, b).start()

    @pl.when(j > 0)
    def _wait_previous_out():
        for b in range(bsz):
            out_copy(j - 1, b).wait()

    @pl.when(j == last)
    def _emit():
        for b in range(bsz):
            out_copy(j, b).wait()
        sre_ref[...] = h[:, 0:S5_LANES]
        sim_ref[...] = h[:, S5_LANES:]


def _s5_seq(u, um, wb, abr, abi, wcr, wci, d, wglu, bglu, nrm):
    bsz, seq, _ = u.shape
    lc = S5_TIME_TILE
    consts = (um, wb, abr, abi, wcr, wci, d, wglu, bglu, nrm)
    st = pl.BlockSpec((bsz, S5_LANES), lambda j: (0, 0))
    return pl.pallas_call(
        _s5_seq_body,
        grid=(seq // lc,),
        in_specs=[pl.BlockSpec(memory_space=pl.ANY)] + [_resident_spec(a) for a in consts],
        out_specs=[pl.BlockSpec(memory_space=pl.ANY), st, st],
        out_shape=[jax.ShapeDtypeStruct((bsz, seq, S5_WIDTH), F32),
                   jax.ShapeDtypeStruct((bsz, S5_LANES), F32), jax.ShapeDtypeStruct((bsz, S5_LANES), F32)],
        scratch_shapes=[pltpu.VMEM((2, lc, bsz, S5_WIDTH), F32), pltpu.VMEM((2, lc, bsz, S5_WIDTH), F32),
                        pltpu.VMEM((lc * bsz, 2 * S5_LANES), F32), pltpu.VMEM((bsz, 2 * S5_LANES), F32),
                        pltpu.SemaphoreType.DMA((2, bsz)), pltpu.SemaphoreType.DMA((2, bsz))],
        compiler_params=pltpu.CompilerParams(dimension_semantics=("arbitrary",), vmem_limit_bytes=VMEM_LIMIT),
        name="s5_seq",
    )(u, *consts)


def _sample_post_body(yc_ref, xs_ref, z_ref, dexp_ref, snrm_ref, u_ref, hr_ref, hi_ref, wb_ref, abr_ref, abi_ref,
                      wcr_ref, wci_ref, d_ref, wglu_ref, bglu_ref, nrm_ref,
                      yssd_ref, ys5_ref, nre_ref, nim_ref):
    z = z_ref[...]
    y = yc_ref[...] + dexp_ref[...] * xs_ref[...]
    yssd_ref[...] = _rms(y * (z * jax.nn.sigmoid(z)), snrm_ref[...]).astype(yssd_ref.dtype)

    u = u_ref[...]
    nw = 16 * S5_STATE
    ar, ai = abr_ref[...], abi_ref[...]

    def store(jj, re, im):
        sl = slice(jj * nw, (jj + 1) * nw)
        h0r, h0i = hr_ref[:, sl], hi_ref[:, sl]
        nre_ref[:, sl] = ar[:, sl] * h0r - ai[:, sl] * h0i + re
        nim_ref[:, sl] = ar[:, sl] * h0i + ai[:, sl] * h0r + im

    _s5_project_in(u.astype(BF16), wb_ref, store)
    slab = lambda ref: (lambda jj: ref[:, jj * nw:(jj + 1) * nw])
    y5 = _s5_tail(slab(nre_ref), slab(nim_ref), u, wcr_ref, wci_ref, d_ref, wglu_ref, bglu_ref, nrm_ref)
    ys5_ref[...] = y5.astype(ys5_ref.dtype)


def _sample_post(yc, xs, z, dexp, snrm, u, h0r, h0i, wb, abr1, abi1, wcr, wci, d, wglu, bglu, nrm):
    n = yc.shape[0]
    args = (yc, xs, z, dexp, snrm, u, h0r, h0i, wb, abr1, abi1, wcr, wci, d, wglu, bglu, nrm)
    spec = lambda w: pl.BlockSpec((n, w), lambda: (0, 0))
    return pl.pallas_call(
        _sample_post_body,
        in_specs=[_full_spec(a) for a in args],
        out_specs=[spec(SSD_WIDTH), spec(S5_WIDTH), spec(S5_LANES), spec(S5_LANES)],
        out_shape=[jax.ShapeDtypeStruct((n, SSD_WIDTH), BF16), jax.ShapeDtypeStruct((n, S5_WIDTH), BF16),
                   jax.ShapeDtypeStruct((n, S5_LANES), F32), jax.ShapeDtypeStruct((n, S5_LANES), F32)],
        compiler_params=pltpu.CompilerParams(vmem_limit_bytes=VMEM_LIMIT),
        name="sample_post",
    )(*args)


def _mix_route_body(n_blocks, xp_ref, ysp_ref, y5p_ref, xs_ref, yss_ref, y5s_ref, *refs):
    xn_hbm, _, cnt_ref, carry, xbuf, sems = refs[-6:]
    i = pl.program_id(0)
    tm, n_sample = xp_ref.shape[0], xs_ref.shape[0]

    def xn_copy(step, rows, j):
        return pltpu.make_async_copy(xbuf.at[step % 2, pl.ds(0, rows), pl.ds(j * LANES, LANES)],
                                     xn_hbm.at[pl.ds(step * tm, rows), j, :], sems.at[step % 2, j])

    @pl.when(i == 0)
    def _init():
        carry[...] = jnp.zeros_like(carry)

    @pl.when(i < n_blocks)
    def _prompt_rows():
        _mix_route_compute(xp_ref, ysp_ref, y5p_ref, *refs[:-2], xbuf.at[i % 2])
        for j in range(SLAB_ROWS):
            xn_copy(i, tm, j).start()

    @pl.when(i == n_blocks)
    def _sample_rows():
        _mix_route_compute(xs_ref, yss_ref, y5s_ref, *refs[:-2], xbuf.at[i % 2])
        for j in range(SLAB_ROWS):
            xn_copy(i, n_sample, j).start()
        for j in range(SLAB_ROWS):
            xn_copy(i, n_sample, j).wait()

    @pl.when(i > 0)
    def _wait_previous_rows():
        for j in range(SLAB_ROWS):
            xn_copy(i - 1, tm, j).wait()

    cnt_ref[...] = carry[...]


def _mix_route_compute(x_ref, ys_ref, y5_ref, wa_ref, wb_ref, nf_ref, wrh_ref, wrl_ref, br_ref,
                       x1_ref, _xn, rt_ref, _cnt, carry, xn_buf):
    rows = x_ref.shape[0]
    x1 = x_ref[...] + _dot(ys_ref[...], wa_ref[...]) + _dot(y5_ref[...].astype(BF16), wb_ref[...])
    x1_ref[0:rows, :] = x1
    xn = _rms(x1, nf_ref[...])
    xn_buf[0:rows, :] = xn

    xh = xn.astype(BF16)
    xl = (xn - xh.astype(F32)).astype(BF16)
    logits = _dot(xh, wrh_ref[...]) + _dot(xl, wrh_ref[...]) + _dot(xh, wrl_ref[...]) + br_ref[...]
    tm = logits.shape[0]
    lane = lax.broadcasted_iota(jnp.int32, logits.shape, 1).astype(F32)
    neg = -jnp.inf
    big = float(LANES)

    def first_max(v):
        m = jnp.max(v, axis=-1, keepdims=True)
        return m, jnp.min(jnp.where(v == m, lane, big), axis=-1, keepdims=True)

    coarse = lane < MOE_GROUPS
    mc, gsel = first_max(jnp.where(coarse, logits, neg))
    psel = 1.0 / jnp.sum(jnp.where(coarse, jnp.exp(logits - mc), 0.0), axis=-1, keepdims=True)
    lo = MOE_GROUPS + MOE_EPG * gsel
    lf = jnp.where((lane >= lo) & (lane < lo + MOE_EPG), logits, neg)
    m1, i1 = first_max(lf)
    m2, i2 = first_max(jnp.where(lane == i1, neg, lf))
    e2 = jnp.exp(m2 - m1)
    g1 = psel / (1.0 + e2)
    g2 = psel * e2 / (1.0 + e2)
    e_a, e_b = i1 - MOE_GROUPS, i2 - MOE_GROUPS

    pick_a, pick_b = lane == e_a, lane == e_b
    picks = jnp.where(pick_a | pick_b, 1.0, 0.0)
    earlier = lax.broadcasted_iota(jnp.int32, (tm, tm), 0) > lax.broadcasted_iota(jnp.int32, (tm, tm), 1)
    prior = _dot(earlier.astype(BF16), picks.astype(BF16)) + carry[...]
    rank_a = jnp.sum(jnp.where(pick_a, prior, 0.0), axis=-1, keepdims=True)
    rank_b = jnp.sum(jnp.where(pick_b, prior, 0.0), axis=-1, keepdims=True)
    carry[...] = prior[tm - 1:tm, :] + picks[tm - 1:tm, :]

    out = jnp.zeros_like(logits)
    for k, v in enumerate((e_a, e_b, g1, g2, rank_a, rank_b)):
        out = jnp.where(lane == float(k), v, out)
    rt_ref[0:rows, :] = out


def _mix_route(prompt, sample, consts, tm):
    n_prompt, n_sample = prompt[0].shape[0], sample[0].shape[0]
    assert n_prompt % tm == 0 and n_sample <= tm
    n_blocks = n_prompt // tm
    total_rows = n_prompt + n_sample
    row = lambda w: pl.BlockSpec((tm, w), lambda i: (jnp.minimum(i, n_blocks - 1), 0))
    out_row = lambda w: pl.BlockSpec((tm, w), lambda i: (i, 0))
    return pl.pallas_call(
        functools.partial(_mix_route_body, n_blocks),
        grid=(n_blocks + 1,),
        in_specs=([row(D_MODEL), row(SSD_WIDTH), row(S5_WIDTH)] + [_full_spec(a) for a in sample]
                  + [_full_spec(a) for a in consts]),
        out_specs=[out_row(D_MODEL), pl.BlockSpec(memory_space=pl.ANY),
                   out_row(LANES), pl.BlockSpec((1, LANES), lambda i: (0, 0))],
        out_shape=[jax.ShapeDtypeStruct((total_rows, D_MODEL), F32),
                   jax.ShapeDtypeStruct((total_rows, SLAB_ROWS, LANES), F32),
                   jax.ShapeDtypeStruct((total_rows, LANES), F32), jax.ShapeDtypeStruct((1, LANES), F32)],
        scratch_shapes=[pltpu.VMEM((1, LANES), F32), pltpu.VMEM((2, tm, D_MODEL), F32),
                        pltpu.SemaphoreType.DMA((2, SLAB_ROWS))],
        compiler_params=pltpu.CompilerParams(dimension_semantics=("arbitrary",), vmem_limit_bytes=VMEM_LIMIT),
        name="mix_route",
    )(*prompt, *sample, *consts)


def _sc_mesh():
    return plsc.VectorSubcoreMesh(core_axis_name="c", subcore_axis_name="s")


def _sc_worker():
    return lax.axis_index("s") * SC_CORES + lax.axis_index("c")


def _sc_dispatch(xn, pos_a, pos_b, n_rows):
    n_tok = xn.shape[0]
    ch = SC_DISPATCH_ROWS
    assert n_tok % ch == 0

    @functools.partial(
        pl.kernel, mesh=_sc_mesh(),
        out_type=jax.ShapeDtypeStruct((n_rows, SLAB_ROWS, LANES), F32),
        scratch_types=[pltpu.VMEM((ch,), jnp.int32), pltpu.VMEM((ch,), jnp.int32),
                       pltpu.VMEM((ch, SLAB_ROWS, LANES), F32), pltpu.SemaphoreType.DMA])
    def push(xn_hbm, pa_hbm, pb_hbm, xs_hbm, ia, ib, rows, sem):
        @pl.loop(_sc_worker(), n_tok // ch, step=SC_WORKERS)
        def _(c):
            off = pl.multiple_of(c * ch, ch)
            pltpu.sync_copy(pa_hbm.at[pl.ds(off, ch)], ia)
            pltpu.sync_copy(pb_hbm.at[pl.ds(off, ch)], ib)
            pltpu.sync_copy(xn_hbm.at[pl.ds(off, ch)], rows)
            pltpu.async_copy(rows, xs_hbm.at[ia], sem).wait()
            pltpu.async_copy(rows, xs_hbm.at[ib], sem).wait()

    return push(xn, pos_a, pos_b)


def _sc_collect(ysorted, pos_flat):
    n_pick = pos_flat.shape[0]
    ch = SC_COLLECT_ROWS
    per_worker = n_pick // SC_WORKERS
    n_chunks = per_worker // ch
    assert n_pick % SC_WORKERS == 0 and per_worker % ch == 0

    @functools.partial(
        pl.kernel, mesh=_sc_mesh(),
        out_type=jax.ShapeDtypeStruct((n_pick, SLAB_ROWS, LANES), F32),
        scratch_types=[pltpu.VMEM((ch,), jnp.int32), pltpu.VMEM((ch,), jnp.int32),
                       pltpu.VMEM((ch, SLAB_ROWS, LANES), F32), pltpu.VMEM((ch, SLAB_ROWS, LANES), F32),
                       pltpu.SemaphoreType.DMA, pltpu.SemaphoreType.DMA])
    def pull(ys_hbm, pos_hbm, out_hbm, idx0, idx1, rows0, rows1, sem0, sem1):
        base = _sc_worker() * per_worker
        bufs = ((idx0, rows0, sem0), (idx1, rows1, sem1))

        def offset(j):
            return pl.multiple_of(base + j * ch, SUBLANES)

        def fetch(j, b):
            idx, rows, sem = bufs[b]
            pltpu.sync_copy(pos_hbm.at[pl.ds(offset(j), ch)], idx)
            pltpu.async_copy(ys_hbm.at[idx], rows, sem)

        def flush(j, b):
            idx, rows, sem = bufs[b]
            pltpu.make_async_copy(ys_hbm.at[idx], rows, sem).wait()
            pltpu.sync_copy(rows, out_hbm.at[pl.ds(offset(j), ch)])

        fetch(0, 0)

        @pl.loop(0, n_chunks // 2)
        def _(p):
            j = 2 * p
            fetch(j + 1, 1)
            flush(j, 0)

            @pl.when(j + 2 < n_chunks)
            def _():
                fetch(j + 2, 0)

            flush(j + 1, 1)

        if n_chunks % 2:
            flush(n_chunks - 1, 0)

    return pull(ysorted, pos_flat)


def _moe_ffn_body(te_ref, nused_ref, xs_hbm, wg_ref, wu_ref, wd_ref, ys_hbm, xbuf, ybuf, wgb, wub, wdb,
                  in_sems, out_sems):
    i = pl.program_id(0)
    n_used = nused_ref[0]

    def in_copy(tile, j):
        return pltpu.make_async_copy(xs_hbm.at[pl.ds(tile * MOE_TILE, MOE_TILE), j, :],
                                     xbuf.at[tile % 2, :, pl.ds(j * LANES, LANES)], in_sems.at[tile % 2, j])

    def out_copy(tile, j):
        return pltpu.make_async_copy(ybuf.at[tile % 2, :, pl.ds(j * LANES, LANES)],
                                     ys_hbm.at[pl.ds(tile * MOE_TILE, MOE_TILE), j, :], out_sems.at[tile % 2, j])

    @pl.when(i == 0)
    def _first_fetch():
        for j in range(SLAB_ROWS):
            in_copy(0, j).start()

    @pl.when(i + 1 < n_used)
    def _prefetch():
        for j in range(SLAB_ROWS):
            in_copy(i + 1, j).start()

    @pl.when(i < n_used)
    def _tile():
        @pl.when((i == 0) | (te_ref[i] != te_ref[jnp.maximum(i - 1, 0)]))
        def _cast_weights():
            wgb[...] = wg_ref[0].astype(BF16)
            wub[...] = wu_ref[0].astype(BF16)
            wdb[...] = wd_ref[0].astype(BF16)

        for j in range(SLAB_ROWS):
            in_copy(i, j).wait()
        x = xbuf[i % 2].astype(BF16)
        gate = _dot(x, wgb[...])
        hmid = (gate * jax.nn.sigmoid(gate)) * _dot(x, wub[...])
        ybuf[i % 2] = _dot(hmid.astype(BF16), wdb[...])
        for j in range(SLAB_ROWS):
            out_copy(i, j).start()

    @pl.when((i > 0) & (i <= n_used))
    def _wait_previous_out():
        for j in range(SLAB_ROWS):
            out_copy(i - 1, j).wait()


def _moe_ffn(tile_expert, n_used, xsorted, w_gate, w_up, w_down):
    n_steps = tile_expert.shape[0]
    wspec = lambda s: pl.BlockSpec((1,) + s, lambda i, te, nu: (te[i], 0, 0))
    buf = pltpu.VMEM((2, MOE_TILE, D_MODEL), F32)
    return pl.pallas_call(
        _moe_ffn_body,
        grid_spec=pltpu.PrefetchScalarGridSpec(
            num_scalar_prefetch=2,
            grid=(n_steps,),
            in_specs=[pl.BlockSpec(memory_space=pl.ANY),
                      wspec((D_MODEL, MOE_D_FF)), wspec((D_MODEL, MOE_D_FF)), wspec((MOE_D_FF, D_MODEL))],
            out_specs=pl.BlockSpec(memory_space=pl.ANY),
            scratch_shapes=[buf, buf,
                            pltpu.VMEM((D_MODEL, MOE_D_FF), BF16), pltpu.VMEM((D_MODEL, MOE_D_FF), BF16),
                            pltpu.VMEM((MOE_D_FF, D_MODEL), BF16),
                            pltpu.SemaphoreType.DMA((2, SLAB_ROWS)), pltpu.SemaphoreType.DMA((2, SLAB_ROWS))]),
        out_shape=jax.ShapeDtypeStruct(xsorted.shape, F32),
        compiler_params=pltpu.CompilerParams(dimension_semantics=("arbitrary",), vmem_limit_bytes=VMEM_LIMIT),
        name="moe_ffn",
    )(tile_expert, n_used, xsorted, w_gate, w_up, w_down)


def _combine_body(x1_ref, rt_ref, ya_ref, yb_ref, nf_ref, out_ref):
    rt = rt_ref[...]
    x1 = x1_ref[...]
    x2 = jnp.concatenate(
        [x1[:, j * LANES:(j + 1) * LANES] + rt[:, 2:3] * ya_ref[0, :, j, :] + rt[:, 3:4] * yb_ref[0, :, j, :]
         for j in range(SLAB_ROWS)], axis=-1)
    out_ref[...] = _rms(x2, nf_ref[...])


def _combine(x1, rt, y_picks, nf, tm, rows, row_block_offset):
    row = lambda w: pl.BlockSpec((tm, w), lambda i: (i + row_block_offset, 0))
    pick = lambda k: pl.BlockSpec((1, tm, SLAB_ROWS, LANES), lambda i: (k, i + row_block_offset, 0, 0))
    return pl.pallas_call(
        _combine_body,
        grid=(rows // tm,),
        in_specs=[row(D_MODEL), row(LANES), pick(0), pick(1),
                  pl.BlockSpec((1, D_MODEL), lambda i: (0, 0))],
        out_specs=pl.BlockSpec((tm, D_MODEL), lambda i: (i, 0)),
        out_shape=jax.ShapeDtypeStruct((rows, D_MODEL), F32),
        compiler_params=pltpu.CompilerParams(dimension_semantics=("parallel",), vmem_limit_bytes=VMEM_LIMIT),
        name="moe_combine",
    )(x1, rt, y_picks, y_picks, nf)


def _route_tables(counts, eid, rank, n_tiles):
    tiles_per = (counts + MOE_TILE - 1) // MOE_TILE
    tile_end = jnp.cumsum(tiles_per)
    pstart = (tile_end - tiles_per) * MOE_TILE
    experts = jnp.arange(MOE_EXPERTS, dtype=jnp.int32)
    pos = [jnp.sum(jnp.where(e[:, None] == experts, pstart, 0), axis=-1) + r for e, r in zip(eid, rank)]
    n_used = tile_end[-1]
    tiles = jnp.arange(n_tiles, dtype=jnp.int32)
    tile_expert = jnp.sum((tile_end[None, :] <= jnp.minimum(tiles, n_used - 1)[:, None]).astype(jnp.int32), axis=1)
    return pos, tile_expert, n_used.reshape(1).astype(jnp.int32)


def _s5_tables(a_re, a_im, log_dt, b_re, b_im, c_re, c_im):
    dt = jnp.exp(log_dt)[:, None]
    mag = jnp.exp(a_re * dt)
    ab_re = mag * jnp.cos(a_im * dt)
    ab_im = mag * jnp.sin(a_im * dt)
    den = a_re * a_re + a_im * a_im
    nr = ab_re - 1.0
    q_re = (nr * a_re + ab_im * a_im) / den
    q_im = (ab_im * a_re - nr * a_im) / den
    bb_re = q_re[..., None] * b_re - q_im[..., None] * b_im
    bb_im = q_re[..., None] * b_im + q_im[..., None] * b_re
    nblk = S5_GROUPS // 16
    kw, nw = 16 * S5_GROUP_CH, 16 * S5_STATE
    same_group = (jnp.arange(kw)[:, None] // S5_GROUP_CH) == (jnp.arange(nw)[None, :] // S5_STATE)

    def in_map(bb):
        rows = bb.reshape(nblk, 16, S5_STATE, S5_GROUP_CH).transpose(0, 1, 3, 2).reshape(nblk, kw, S5_STATE)
        return jnp.where(same_group, jnp.tile(rows, (1, 1, 16)), 0.0)

    def out_map(cc):
        cols = cc.reshape(nblk, 16, S5_GROUP_CH, S5_STATE).transpose(0, 3, 1, 2).reshape(nblk, S5_STATE, kw)
        return jnp.where(same_group.T, jnp.tile(cols, (1, 16, 1)), 0.0)

    wb = jnp.concatenate([in_map(bb_re), in_map(bb_im)], axis=-1).astype(BF16)
    return (wb, ab_re.reshape(1, S5_LANES), ab_im.reshape(1, S5_LANES),
            out_map(c_re).astype(BF16), out_map(-c_im).astype(BF16))


def kernel(x_prompt, x_sample, state_ssd_conv, state_ssd_ssm, state_s5_re, state_s5_im, meta_tokens, norm_mix, w_in, conv_w, conv_b, dt_bias, a_log, d_ssd, ssd_norm, s5_a_re, s5_a_im, s5_log_dt, s5_b_re, s5_b_im, s5_c_re, s5_c_im, s5_d, w_glu, b_glu, s5_norm, w_out, norm_ffn, router_coarse_w, router_coarse_b, router_fine_w, router_fine_b, w_gate, w_up, w_down, norm_final):
    bp, seq, _ = x_prompt.shape
    bs = x_sample.shape[0]
    n_prompt = bp * seq
    n_tok = n_prompt + bs
    row2 = lambda v: v.reshape(1, -1)
    pad_heads = lambda v: jnp.pad(v, (0, LANES - SSD_HEADS)).reshape(1, LANES)

    w = w_in[0]
    o1, o2, o3 = SSD_WIDTH, SSD_WIDTH + SSD_CONV_DIM, SSD_WIDTH + SSD_CONV_DIM + SSD_HEADS
    wz, wx, wu = w[:, :o1].astype(BF16), w[:, o1:o2].astype(BF16), w[:, o3:].astype(BF16)
    wdt = jnp.pad(w[:, o2:o3], ((0, 0), (0, LANES - SSD_HEADS))).astype(BF16)
    g_mix = row2(norm_mix[0])
    cw, cb = conv_w[0], row2(conv_b[0])
    dtb, alog = pad_heads(dt_bias[0]), pad_heads(a_log[0])
    dexp = row2(jnp.repeat(d_ssd[0], SSD_HEAD_DIM))
    snrm = row2(ssd_norm[0])
    eexp = (jnp.arange(LANES)[:, None] == (jnp.arange(SSD_WIDTH) // SSD_HEAD_DIM)[None, :]).astype(BF16)
    wb5, ab_re, ab_im, wcr, wci = _s5_tables(s5_a_re[0], s5_a_im[0], s5_log_dt[0], s5_b_re[0], s5_b_im[0],
                                             s5_c_re[0], s5_c_im[0])
    d5, wglu, bglu, nrm5 = row2(s5_d[0]), w_glu[0].astype(BF16), row2(b_glu[0]), row2(s5_norm[0])
    wo_a, wo_b = w_out[0][:SSD_WIDTH].astype(BF16), w_out[0][SSD_WIDTH:].astype(BF16)
    w_r = jnp.concatenate([router_coarse_w[0], router_fine_w[0].transpose(1, 0, 2).reshape(D_MODEL, MOE_EXPERTS)], axis=1)
    w_r = jnp.pad(w_r, ((0, 0), (0, LANES - w_r.shape[1])))
    wrh = w_r.astype(BF16)
    wrl = (w_r - wrh.astype(F32)).astype(BF16)
    b_r = jnp.concatenate([router_coarse_b[0], router_fine_b[0].reshape(-1)])
    b_r = jnp.pad(b_r, (0, LANES - b_r.shape[0])).reshape(1, LANES)

    zp, xbcp, dtp, up = _in_proj(x_prompt.reshape(n_prompt, D_MODEL), g_mix, wz, wx, wdt, wu, TOK_TILE, BF16, F32)
    xsm = jnp.concatenate([x_sample.reshape(bs, D_MODEL), meta_tokens], axis=0)
    zs, xbcs, dts, us = _in_proj(xsm, g_mix, wz, wx, wdt, wu, xsm.shape[0], F32, F32)

    front = SSD_CHUNK - N_META
    padf = lambda a: jnp.pad(a[bs:], ((front, 0), (0, 0)))[None]
    gw = SSD_HPG * SSD_HEAD_DIM
    ssd_consts = (cw, cb, dtb, alog, dexp, snrm, eexp)
    _, ctail_m, _, ht_m = _ssd_chunked(
        padf(xbcs), padf(dts), jnp.zeros((1, SSD_CHUNK, SSD_WIDTH), F32),
        jnp.zeros((1, SUBLANES, SSD_CONV_DIM), F32), jnp.zeros((1, SSD_GROUPS, SSD_STATE, gw), F32),
        *ssd_consts, mask_rows=front)
    y_ssd_p, ctail_p, ssm_p, _ = _ssd_chunked(
        xbcp.reshape(bp, seq, SSD_CONV_DIM), dtp.reshape(bp, seq, LANES), zp.reshape(bp, seq, SSD_WIDTH),
        ctail_m, ht_m, *ssd_consts, mask_rows=0)

    abr8, abi8 = jnp.broadcast_to(ab_re, (bp, S5_LANES)), jnp.broadcast_to(ab_im, (bp, S5_LANES))
    um8 = jnp.repeat(us[bs:], bp, axis=0).astype(BF16)
    y_s5_p, s5re_p, s5im_p = _s5_seq(up.reshape(bp, seq, S5_WIDTH), um8, wb5, abr8, abi8,
                                     wcr, wci, d5, wglu, bglu, nrm5)

    cst = state_ssd_conv[0]
    xt_s, dt_s, dec_s, bc, xs_s = _ssd_step_prep(xbcs[:bs], cst[:, 0], cst[:, 1], cst[:, 2], dts[:bs],
                                                 cw, cb, dtb, alog)
    ssm_s, y_core = _ssd_step(dt_s[:, :SSD_HEADS].reshape(-1), dec_s[:, :SSD_HEADS].reshape(-1),
                              state_ssd_ssm[0], xt_s, bc)
    y_ssd_s, y_s5_s, s5re_s, s5im_s = _sample_post(
        y_core, xs_s, zs[:bs], dexp, snrm, us[:bs], state_s5_re[0].reshape(bs, S5_LANES),
        state_s5_im[0].reshape(bs, S5_LANES), wb5, ab_re, ab_im, wcr, wci, d5, wglu, bglu, nrm5)

    route_consts = (wo_a, wo_b, row2(norm_ffn[0]), wrh, wrl, b_r)
    x1, xn, rt, counts = _mix_route(
        (x_prompt.reshape(n_prompt, D_MODEL), y_ssd_p.reshape(n_prompt, SSD_WIDTH), y_s5_p.reshape(n_prompt, S5_WIDTH)),
        (x_sample.reshape(bs, D_MODEL), y_ssd_s, y_s5_s), route_consts, TOK_TILE)

    n_tiles = -(-2 * n_tok // MOE_TILE) + MOE_EXPERTS
    lane_i32 = lambda k: rt[:, k].astype(jnp.int32)
    eid = [jnp.clip(lane_i32(k), 0, MOE_EXPERTS - 1) for k in (0, 1)]
    (pos_a, pos_b), tile_expert, n_used = _route_tables(counts[0, :MOE_EXPERTS].astype(jnp.int32), eid,
                                                        [lane_i32(4), lane_i32(5)], n_tiles + 1)
    xsorted = _sc_dispatch(xn, pos_a, pos_b, n_tiles * MOE_TILE)
    ysorted = _moe_ffn(tile_expert, n_used, xsorted, w_gate[0], w_up[0], w_down[0])
    y_picks = _sc_collect(ysorted, jnp.concatenate([pos_a, pos_b])).reshape(2, n_tok, SLAB_ROWS, LANES)
    nfin = row2(norm_final)
    y_p = _combine(x1, rt, y_picks, nfin, MOE_TILE, n_prompt, 0)
    y_s = _combine(x1, rt, y_picks, nfin, bs, bs, n_prompt // bs)

    s5_state = lambda a, b: a.reshape(1, b, S5_GROUPS, S5_STATE)
    new_conv_s = jnp.stack([cst[:, 1], cst[:, 2], xbcs[:bs]], axis=1)[None]
    return (y_p.reshape(bp, seq, D_MODEL), y_s.reshape(bs, 1, D_MODEL),
            ctail_p[:, SUBLANES - (SSD_CONV - 1):][None], ssm_p[None], s5_state(s5re_p, bp), s5_state(s5im_p, bp),
            new_conv_s, ssm_s[None], s5_state(s5re_s, bs), s5_state(s5im_s, bs))
```

```python
import functools

import jax
import jax.numpy as jnp
from jax import lax
from jax.experimental import pallas as pl
from jax.experimental.pallas import tpu as pltpu
from jax.experimental.pallas import tpu_sc as plsc

F32, BF16 = jnp.float32, jnp.bfloat16

D_MODEL = 1024
N_META = 16
SSD_WIDTH = 1024
SSD_HEAD_DIM = 64
SSD_HEADS = 16
SSD_GROUPS = 2
SSD_HPG = SSD_HEADS // SSD_GROUPS
SSD_STATE = 128
SSD_CONV = 4
SSD_CHUNK = 128
SSD_CONV_DIM = SSD_WIDTH + 2 * SSD_GROUPS * SSD_STATE
S5_WIDTH = 1024
S5_GROUP_CH = 16
S5_GROUPS = 64
S5_STATE = 64
S5_LANES = S5_GROUPS * S5_STATE
MOE_GROUPS = 4
MOE_EPG = 8
MOE_EXPERTS = MOE_GROUPS * MOE_EPG
MOE_D_FF = 512
EPS = 1e-6

LANES = 128
SUBLANES = 8
VMEM_LIMIT = 56 * 1024 * 1024

S5_TIME_TILE = 64
S5_SCAN_LANES = 512
MOE_TILE = 256
SLAB_ROWS = D_MODEL // LANES
SC_CORES = 2
SC_SUBCORES = 16
SC_WORKERS = SC_CORES * SC_SUBCORES
SC_DISPATCH_ROWS = 64
SC_COLLECT_ROWS = 24
TOK_TILE = 512


def _dot(a, b):
    return jnp.dot(a, b, preferred_element_type=F32)


def _rms(x, g):
    return x * lax.rsqrt(jnp.mean(x * x, axis=-1, keepdims=True) + EPS) * g


def _softplus(x):
    return jnp.maximum(x, 0.0) + jnp.log1p(jnp.exp(-jnp.abs(x)))


def _split3(x):
    hi = x.astype(BF16)
    r = x - hi.astype(F32)
    mid = r.astype(BF16)
    lo = (r - mid.astype(F32)).astype(BF16)
    return hi, mid, lo


def _dot3(x, w):
    hi, mid, lo = _split3(x)
    return _dot(hi, w) + _dot(mid, w) + _dot(lo, w)


def _dot3_left(w, x):
    hi, mid, lo = _split3(x)
    return _dot(w, hi) + _dot(w, mid) + _dot(w, lo)


def _full_spec(a):
    nd = a.ndim
    return pl.BlockSpec(a.shape, lambda *_: (0,) * nd)


def _resident_spec(a):
    nd = a.ndim
    return pl.BlockSpec(a.shape, lambda *_: (0,) * nd, pipeline_mode=pl.Buffered(1))


def _in_proj_body(x_ref, g_ref, wz_ref, wx_ref, wdt_ref, wu_ref, z_ref, xbc_ref, dt_ref, u_ref):
    xb = _rms(x_ref[...], g_ref[...]).astype(BF16)
    z_ref[...] = _dot(xb, wz_ref[...]).astype(z_ref.dtype)
    xbc_ref[...] = _dot(xb, wx_ref[...]).astype(xbc_ref.dtype)
    dt_ref[...] = _dot(xb, wdt_ref[...])
    u_ref[...] = _dot(xb, wu_ref[...]).astype(u_ref.dtype)


def _in_proj(x2d, g, wz, wx, wdt, wu, tm, act_dtype, u_dtype):
    rows = x2d.shape[0]
    row = lambda w: pl.BlockSpec((tm, w), lambda i: (i, 0))
    return pl.pallas_call(
        _in_proj_body,
        grid=(rows // tm,),
        in_specs=[row(D_MODEL), _full_spec(g), _full_spec(wz), _full_spec(wx), _full_spec(wdt), _full_spec(wu)],
        out_specs=[row(SSD_WIDTH), row(SSD_CONV_DIM), row(LANES), row(S5_WIDTH)],
        out_shape=[jax.ShapeDtypeStruct((rows, SSD_WIDTH), act_dtype),
                   jax.ShapeDtypeStruct((rows, SSD_CONV_DIM), act_dtype),
                   jax.ShapeDtypeStruct((rows, LANES), F32),
                   jax.ShapeDtypeStruct((rows, S5_WIDTH), u_dtype)],
        compiler_params=pltpu.CompilerParams(dimension_semantics=("parallel",), vmem_limit_bytes=VMEM_LIMIT),
        name="in_proj",
    )(x2d, g, wz, wx, wdt, wu)


def _ssd_body(mask_rows, xbc_ref, dt_ref, z_ref, cinit_ref, hinit_ref, cw_ref, cb_ref, dtb_ref, alog_ref,
              dexp_ref, nrm_ref, eexp_ref, y_ref, ctail_ref, st_ref, hto_ref, xwin, hT):
    c = pl.program_id(1)
    L = SSD_CHUNK

    @pl.when(c == 0)
    def _init():
        xwin[0:SUBLANES, :] = cinit_ref[0]
        hT[...] = hinit_ref[0]

    xwin[SUBLANES:SUBLANES + L, :] = xbc_ref[0].astype(F32)
    acc = cb_ref[...]
    for k in range(SSD_CONV):
        off = SUBLANES - (SSD_CONV - 1) + k
        acc = acc + xwin[off:off + L, :] * cw_ref[k:k + 1, :]
    tail = xwin[L:L + SUBLANES, :]
    xwin[0:SUBLANES, :] = tail
    ctail_ref[0] = tail

    xact = acc * jax.nn.sigmoid(acc)
    dt = _softplus(dt_ref[0] + dtb_ref[...])
    if mask_rows:
        valid = lax.broadcasted_iota(jnp.int32, (L, 1), 0) >= mask_rows
        xact = jnp.where(valid, xact, 0.0)
        dt = jnp.where(valid, dt, 0.0)

    a_neg = -jnp.exp(alog_ref[...])
    dA = dt * a_neg
    row_i = lax.broadcasted_iota(jnp.int32, (L, L), 0)
    col_i = lax.broadcasted_iota(jnp.int32, (L, L), 1)
    causal = row_i >= col_i
    tril = causal.astype(BF16)
    cs = _dot3_left(tril, dA)
    csT = cs.T
    dtT = dt.T
    ecs = jnp.exp(cs)
    wdec = jnp.exp(cs[L - 1:L, :] - cs) * dt
    eexp = eexp_ref[...]
    ecs_e = _dot3(ecs, eexp)
    wdec_e = _dot3(wdec, eexp)
    lane = lax.broadcasted_iota(jnp.int32, (L, LANES), 1)
    first_half = lane < SSD_HEAD_DIM

    gw = SSD_HPG * SSD_HEAD_DIM
    y_groups = []
    for g in range(SSD_GROUPS):
        b_g = xact[:, SSD_WIDTH + g * SSD_STATE: SSD_WIDTH + (g + 1) * SSD_STATE]
        c_g = xact[:, SSD_WIDTH + (SSD_GROUPS + g) * SSD_STATE: SSD_WIDTH + (SSD_GROUPS + g + 1) * SSD_STATE]
        b_b = b_g.astype(BF16)
        c_b = c_g.astype(BF16)
        cb = lax.dot_general(c_b, b_b, (((1,), (1,)), ((), ())), preferred_element_type=F32)
        xs_g = xact[:, g * gw:(g + 1) * gw]
        h_prev = hT[g]
        y_off = _dot(c_b, h_prev.astype(BF16)) * ecs_e[:, g * gw:(g + 1) * gw]
        xdec = (xs_g * wdec_e[:, g * gw:(g + 1) * gw]).astype(BF16)
        hT[g] = h_prev * ecs_e[L - 1:L, g * gw:(g + 1) * gw] + _dot(b_g.T.astype(BF16), xdec)
        pieces = []
        for j in range(SSD_HPG // 2):
            xs_pair = xs_g[:, j * LANES:(j + 1) * LANES]
            halves = (jnp.where(first_half, xs_pair, 0.0).astype(BF16),
                      jnp.where(first_half, 0.0, xs_pair).astype(BF16))
            yd = None
            for t in range(2):
                h = g * SSD_HPG + 2 * j + t
                seg = cs[:, h:h + 1] - csT[h:h + 1, :]
                lmat = jnp.exp(jnp.where(causal, seg, -jnp.inf))
                m = (cb * lmat * dtT[h:h + 1, :]).astype(BF16)
                part = _dot(m, halves[t])
                yd = part if yd is None else yd + part
            pieces.append(yd)
        y_groups.append(jnp.concatenate(pieces, axis=-1) + y_off + dexp_ref[:, g * gw:(g + 1) * gw] * xs_g)
    y = jnp.concatenate(y_groups, axis=-1)
    z = z_ref[0].astype(F32)
    y_ref[0] = _rms(y * (z * jax.nn.sigmoid(z)), nrm_ref[...]).astype(y_ref.dtype)

    @pl.when(c == pl.num_programs(1) - 1)
    def _emit():
        hto_ref[0] = hT[...]
        for g in range(SSD_GROUPS):
            t = hT[g].T
            for k in range(SSD_HPG):
                st_ref[0, g * SSD_HPG + k] = t[k * SSD_HEAD_DIM:(k + 1) * SSD_HEAD_DIM, :]


def _ssd_chunked(xbc, dt, z, cinit, hinit, cw, cb, dtb, alog, dexp, nrm, eexp, mask_rows):
    bsz, seq, _ = xbc.shape
    nc = seq // SSD_CHUNK
    gw = SSD_HPG * SSD_HEAD_DIM
    blk = lambda w: pl.BlockSpec((1, SSD_CHUNK, w), lambda b, c: (b, c, 0))
    return pl.pallas_call(
        functools.partial(_ssd_body, mask_rows),
        grid=(bsz, nc),
        in_specs=[blk(SSD_CONV_DIM), blk(LANES), blk(SSD_WIDTH),
                  pl.BlockSpec((1, SUBLANES, SSD_CONV_DIM), lambda b, c: (0, 0, 0)),
                  pl.BlockSpec((1, SSD_GROUPS, SSD_STATE, gw), lambda b, c: (0, 0, 0, 0)),
                  _full_spec(cw), _full_spec(cb), _full_spec(dtb), _full_spec(alog),
                  _full_spec(dexp), _full_spec(nrm), _full_spec(eexp)],
        out_specs=[blk(SSD_WIDTH),
                   pl.BlockSpec((1, SUBLANES, SSD_CONV_DIM), lambda b, c: (b, 0, 0)),
                   pl.BlockSpec((1, SSD_HEADS, SSD_HEAD_DIM, SSD_STATE), lambda b, c: (b, 0, 0, 0)),
                   pl.BlockSpec((1, SSD_GROUPS, SSD_STATE, gw), lambda b, c: (b, 0, 0, 0))],
        out_shape=[jax.ShapeDtypeStruct((bsz, seq, SSD_WIDTH), BF16),
                   jax.ShapeDtypeStruct((bsz, SUBLANES, SSD_CONV_DIM), F32),
                   jax.ShapeDtypeStruct((bsz, SSD_HEADS, SSD_HEAD_DIM, SSD_STATE), F32),
                   jax.ShapeDtypeStruct((bsz, SSD_GROUPS, SSD_STATE, gw), F32)],
        scratch_shapes=[pltpu.VMEM((SUBLANES + SSD_CHUNK, SSD_CONV_DIM), F32),
                        pltpu.VMEM((SSD_GROUPS, SSD_STATE, gw), F32)],
        compiler_params=pltpu.CompilerParams(dimension_semantics=("parallel", "arbitrary"),
                                             vmem_limit_bytes=VMEM_LIMIT),
        name="ssd_chunked",
    )(xbc, dt, z, cinit, hinit, cw, cb, dtb, alog, dexp, nrm, eexp)


def _ssd_step_prep_body(xbc_ref, c0_ref, c1_ref, c2_ref, dt_ref, cw_ref, cb_ref, dtb_ref, alog_ref,
                        xt_ref, dt_out_ref, dec_ref, bc_ref, xs_ref):
    acc = cb_ref[...]
    for k, r in enumerate((c0_ref, c1_ref, c2_ref, xbc_ref)):
        acc = acc + r[...] * cw_ref[k:k + 1, :]
    xact = acc * jax.nn.sigmoid(acc)
    xs = xact[:, :SSD_WIDTH]
    dt = _softplus(dt_ref[...] + dtb_ref[...])
    dt_out_ref[...] = dt
    dec_ref[...] = jnp.exp(dt * -jnp.exp(alog_ref[...]))
    bc_ref[...] = xact[:, SSD_WIDTH:]
    xs_ref[...] = xs
    xt_ref[...] = xs.T.astype(xt_ref.dtype)


def _ssd_step_prep(xbc, c0, c1, c2, dt, cw, cb, dtb, alog):
    n = xbc.shape[0]
    args = (xbc, c0, c1, c2, dt, cw, cb, dtb, alog)
    spec = lambda r, w: pl.BlockSpec((r, w), lambda: (0, 0))
    return pl.pallas_call(
        _ssd_step_prep_body,
        in_specs=[_full_spec(a) for a in args],
        out_specs=[spec(SSD_WIDTH, n), spec(n, LANES), spec(n, LANES), spec(n, 2 * SSD_GROUPS * SSD_STATE),
                   spec(n, SSD_WIDTH)],
        out_shape=[jax.ShapeDtypeStruct((SSD_WIDTH, n), BF16), jax.ShapeDtypeStruct((n, LANES), F32),
                   jax.ShapeDtypeStruct((n, LANES), F32),
                   jax.ShapeDtypeStruct((n, 2 * SSD_GROUPS * SSD_STATE), F32),
                   jax.ShapeDtypeStruct((n, SSD_WIDTH), F32)],
        compiler_params=pltpu.CompilerParams(vmem_limit_bytes=VMEM_LIMIT),
        name="ssd_step_prep",
    )(*args)


def _ssd_step_body(dt_ref, dec_ref, st_ref, xt_ref, bc_ref, so_ref, y_ref):
    n = xt_ref.shape[1]
    gw = SSD_HPG * SSD_HEAD_DIM
    blk = pl.program_id(0)
    seq_id = lax.broadcasted_iota(jnp.int32, (n, SSD_STATE), 0)
    sub_id = lax.broadcasted_iota(jnp.int32, (SUBLANES, gw), 0)
    base = pl.multiple_of(blk * SUBLANES, SUBLANES)
    y_acc = [jnp.zeros((SUBLANES, gw), F32) for _ in range(SSD_GROUPS)]
    for i in range(SUBLANES):
        s = blk * SUBLANES + i
        for g in range(SSD_GROUPS):
            b_all = bc_ref[:, g * SSD_STATE:(g + 1) * SSD_STATE]
            rhs = jnp.where(seq_id == s, b_all, 0.0).astype(BF16)
            outer = _dot(xt_ref[g * gw:(g + 1) * gw, :], rhs)
            news = []
            for k in range(SSD_HPG):
                h = g * SSD_HPG + k
                new = (dec_ref[s * SSD_HEADS + h] * st_ref[i, h]
                       + dt_ref[s * SSD_HEADS + h] * outer[k * SSD_HEAD_DIM:(k + 1) * SSD_HEAD_DIM, :])
                so_ref[i, h] = new
                news.append(new)
            new_g = jnp.concatenate(news, axis=0).astype(BF16)
            c_lo = (SSD_GROUPS + g) * SSD_STATE
            c_blk = bc_ref[pl.ds(base, SUBLANES), c_lo:c_lo + SSD_STATE].astype(BF16)
            r = lax.dot_general(c_blk, new_g, (((1,), (1,)), ((), ())), preferred_element_type=F32)
            y_acc[g] = y_acc[g] + jnp.where(sub_id == i, r, 0.0)
    y_ref[...] = jnp.concatenate(y_acc, axis=-1)


def _ssd_step(dt_flat, dec_flat, state, xt, bc):
    n = state.shape[0]
    st_spec = pl.BlockSpec((SUBLANES, SSD_HEADS, SSD_HEAD_DIM, SSD_STATE), lambda i, *_: (i, 0, 0, 0))
    return pl.pallas_call(
        _ssd_step_body,
        grid_spec=pltpu.PrefetchScalarGridSpec(
            num_scalar_prefetch=2,
            grid=(n // SUBLANES,),
            in_specs=[st_spec, pl.BlockSpec(xt.shape, lambda i, *_: (0, 0)),
                      pl.BlockSpec(bc.shape, lambda i, *_: (0, 0))],
            out_specs=[st_spec, pl.BlockSpec((SUBLANES, SSD_WIDTH), lambda i, *_: (i, 0))]),
        out_shape=[jax.ShapeDtypeStruct(state.shape, F32), jax.ShapeDtypeStruct((n, SSD_WIDTH), F32)],
        compiler_params=pltpu.CompilerParams(dimension_semantics=("parallel",), vmem_limit_bytes=VMEM_LIMIT),
        name="ssd_step",
    )(dt_flat, dec_flat, state, xt, bc)


def _s5_project_in(u_b16, wb_ref, store):
    kw = 16 * S5_GROUP_CH
    nw = 16 * S5_STATE
    for j in range(S5_WIDTH // kw):
        r = _dot(u_b16[:, j * kw:(j + 1) * kw], wb_ref[j])
        store(j, r[:, :nw], r[:, nw:])


def _s5_tail(hre_of, him_of, u_f32, wcr_ref, wci_ref, d_ref, wglu_ref, bglu_ref, nrm_ref):
    cols = []
    for j in range(wcr_ref.shape[0]):
        cols.append(_dot(hre_of(j).astype(BF16), wcr_ref[j]) + _dot(him_of(j).astype(BF16), wci_ref[j]))
    return _s5_finish(cols, u_f32, d_ref, wglu_ref, bglu_ref, nrm_ref)


def _s5_finish(cols, u_f32, d_ref, wglu_ref, bglu_ref, nrm_ref):
    y = jnp.concatenate(cols, axis=-1) + d_ref[...] * u_f32
    y = jax.nn.gelu(y)
    y = y * jax.nn.sigmoid(_dot(y.astype(BF16), wglu_ref[...]) + bglu_ref[...])
    return _rms(y, nrm_ref[...])


def _s5_seq_body(u_hbm, um_ref, wb_ref, abr_ref, abi_ref, wcr_ref, wci_ref, d_ref, wglu_ref, bglu_ref, nrm_ref,
                 y_hbm, sre_ref, sim_ref, ubuf, ybuf, bu, h, in_sems, out_sems):
    j = pl.program_id(0)
    last = pl.num_programs(0) - 1
    lc, bsz = ubuf.shape[1], ubuf.shape[2]
    rows = lc * bsz
    nw = 16 * S5_STATE

    def in_copy(step, b):
        return pltpu.make_async_copy(u_hbm.at[b, pl.ds(step * lc, lc), :], ubuf.at[step % 2, :, b, :],
                                     in_sems.at[step % 2, b])

    def out_copy(step, b):
        return pltpu.make_async_copy(ybuf.at[step % 2, :, b, :], y_hbm.at[b, pl.ds(step * lc, lc), :],
                                     out_sems.at[step % 2, b])

    def project_in(u_b16, nrows):
        def store(jj, re, im):
            bu[0:nrows, jj * nw:(jj + 1) * nw] = re
            bu[0:nrows, S5_LANES + jj * nw:S5_LANES + (jj + 1) * nw] = im
        _s5_project_in(u_b16, wb_ref, store)

    def scan(nsteps):
        for k in range(S5_LANES // S5_SCAN_LANES):
            sl_r = pl.ds(k * S5_SCAN_LANES, S5_SCAN_LANES)
            sl_i = pl.ds(S5_LANES + k * S5_SCAN_LANES, S5_SCAN_LANES)
            ar = abr_ref[:, sl_r]
            ai = abi_ref[:, sl_r]

            def step(l, carry):
                hr, hi = carry
                slab = pl.ds(pl.multiple_of(l * bsz, bsz), bsz)
                nr = ar * hr - ai * hi + bu[slab, sl_r]
                ni = ar * hi + ai * hr + bu[slab, sl_i]
                bu[slab, sl_r] = nr
                bu[slab, sl_i] = ni
                return nr, ni

            hr, hi = lax.fori_loop(0, nsteps, step, (h[:, sl_r], h[:, sl_i]))
            h[:, sl_r] = hr
            h[:, sl_i] = hi

    @pl.when(j == 0)
    def _first():
        for b in range(bsz):
            in_copy(0, b).start()
        h[...] = jnp.zeros_like(h)
        project_in(um_ref[...], N_META * bsz)
        scan(N_META)

    @pl.when(j < last)
    def _prefetch():
        for b in range(bsz):
            in_copy(j + 1, b).start()

    for b in range(bsz):
        in_copy(j, b).wait()
    u2 = ubuf[j % 2].reshape(rows, S5_WIDTH)
    u_b16 = u2.astype(BF16)
    kw = 16 * S5_GROUP_CH

    def project_block(jj):
        r = _dot(u_b16[:, jj * kw:(jj + 1) * kw], wb_ref[jj])
        bu[0:rows, jj * nw:(jj + 1) * nw] = r[:, :nw]
        bu[0:rows, S5_LANES + jj * nw:S5_LANES + (jj + 1) * nw] = r[:, nw:]

    def scan_block(jj):
        for k in range(nw // S5_SCAN_LANES):
            lo = jj * nw + k * S5_SCAN_LANES
            sl_r = slice(lo, lo + S5_SCAN_LANES)
            sl_i = slice(S5_LANES + lo, S5_LANES + lo + S5_SCAN_LANES)
            ar, ai = abr_ref[:, sl_r], abi_ref[:, sl_r]
            hr, hi = h[:, sl_r], h[:, sl_i]
            for l in range(lc):
                slab = slice(l * bsz, (l + 1) * bsz)
                hr, hi = (ar * hr - ai * hi + bu[slab, sl_r], ar * hi + ai * hr + bu[slab, sl_i])
                bu[slab, sl_r] = hr
                bu[slab, sl_i] = hi
            h[:, sl_r] = hr
            h[:, sl_i] = hi

    def readout_block(jj):
        return (_dot(bu[:, jj * nw:(jj + 1) * nw].astype(BF16), wcr_ref[jj])
                + _dot(bu[:, S5_LANES + jj * nw:S5_LANES + (jj + 1) * nw].astype(BF16), wci_ref[jj]))

    n_blocks = S5_WIDTH // kw
    project_block(0)
    cols = []
    for jj in range(n_blocks):
        if jj + 1 < n_blocks:
            project_block(jj + 1)
        scan_block(jj)
        cols.append(readout_block(jj))
    y = _s5_finish(cols, u2, d_ref, wglu_ref, bglu_ref, nrm_ref)
    ybuf[j % 2] = y.reshape(lc, bsz, S5_WIDTH)
    for b in range(bsz):
        out_copy(j, b).start()

    @pl.when(j > 0)
    def _wait_previous_out():
        for b in range(bsz):
            out_copy(j - 1, b).wait()

    @pl.when(j == last)
    def _emit():
        for b in range(bsz):
            out_copy(j, b).wait()
        sre_ref[...] = h[:, 0:S5_LANES]
        sim_ref[...] = h[:, S5_LANES:]


def _s5_seq(u, um, wb, abr, abi, wcr, wci, d, wglu, bglu, nrm):
    bsz, seq, _ = u.shape
    lc = S5_TIME_TILE
    consts = (um, wb, abr, abi, wcr, wci, d, wglu, bglu, nrm)
    st = pl.BlockSpec((bsz, S5_LANES), lambda j: (0, 0))
    return pl.pallas_call(
        _s5_seq_body,
        grid=(seq // lc,),
        in_specs=[pl.BlockSpec(memory_space=pl.ANY)] + [_resident_spec(a) for a in consts],
        out_specs=[pl.BlockSpec(memory_space=pl.ANY), st, st],
        out_shape=[jax.ShapeDtypeStruct((bsz, seq, S5_WIDTH), F32),
                   jax.ShapeDtypeStruct((bsz, S5_LANES), F32), jax.ShapeDtypeStruct((bsz, S5_LANES), F32)],
        scratch_shapes=[pltpu.VMEM((2, lc, bsz, S5_WIDTH), F32), pltpu.VMEM((2, lc, bsz, S5_WIDTH), F32),
                        pltpu.VMEM((lc * bsz, 2 * S5_LANES), F32), pltpu.VMEM((bsz, 2 * S5_LANES), F32),
                        pltpu.SemaphoreType.DMA((2, bsz)), pltpu.SemaphoreType.DMA((2, bsz))],
        compiler_params=pltpu.CompilerParams(dimension_semantics=("arbitrary",), vmem_limit_bytes=VMEM_LIMIT),
        name="s5_seq",
    )(u, *consts)


def _sample_post_body(yc_ref, xs_ref, z_ref, dexp_ref, snrm_ref, u_ref, hr_ref, hi_ref, wb_ref, abr_ref, abi_ref,
                      wcr_ref, wci_ref, d_ref, wglu_ref, bglu_ref, nrm_ref,
                      yssd_ref, ys5_ref, nre_ref, nim_ref):
    z = z_ref[...]
    y = yc_ref[...] + dexp_ref[...] * xs_ref[...]
    yssd_ref[...] = _rms(y * (z * jax.nn.sigmoid(z)), snrm_ref[...]).astype(yssd_ref.dtype)

    u = u_ref[...]
    nw = 16 * S5_STATE
    ar, ai = abr_ref[...], abi_ref[...]

    def store(jj, re, im):
        sl = slice(jj * nw, (jj + 1) * nw)
        h0r, h0i = hr_ref[:, sl], hi_ref[:, sl]
        nre_ref[:, sl] = ar[:, sl] * h0r - ai[:, sl] * h0i + re
        nim_ref[:, sl] = ar[:, sl] * h0i + ai[:, sl] * h0r + im

    _s5_project_in(u.astype(BF16), wb_ref, store)
    slab = lambda ref: (lambda jj: ref[:, jj * nw:(jj + 1) * nw])
    y5 = _s5_tail(slab(nre_ref), slab(nim_ref), u, wcr_ref, wci_ref, d_ref, wglu_ref, bglu_ref, nrm_ref)
    ys5_ref[...] = y5.astype(ys5_ref.dtype)


def _sample_post(yc, xs, z, dexp, snrm, u, h0r, h0i, wb, abr1, abi1, wcr, wci, d, wglu, bglu, nrm):
    n = yc.shape[0]
    args = (yc, xs, z, dexp, snrm, u, h0r, h0i, wb, abr1, abi1, wcr, wci, d, wglu, bglu, nrm)
    spec = lambda w: pl.BlockSpec((n, w), lambda: (0, 0))
    return pl.pallas_call(
        _sample_post_body,
        in_specs=[_full_spec(a) for a in args],
        out_specs=[spec(SSD_WIDTH), spec(S5_WIDTH), spec(S5_LANES), spec(S5_LANES)],
        out_shape=[jax.ShapeDtypeStruct((n, SSD_WIDTH), BF16), jax.ShapeDtypeStruct((n, S5_WIDTH), BF16),
                   jax.ShapeDtypeStruct((n, S5_LANES), F32), jax.ShapeDtypeStruct((n, S5_LANES), F32)],
        compiler_params=pltpu.CompilerParams(vmem_limit_bytes=VMEM_LIMIT),
        name="sample_post",
    )(*args)


def _mix_route_body(n_blocks, xp_ref, ysp_ref, y5p_ref, xs_ref, yss_ref, y5s_ref, *refs):
    xn_hbm, _, cnt_ref, carry, xbuf, sems = refs[-6:]
    i = pl.program_id(0)
    tm, n_sample = xp_ref.shape[0], xs_ref.shape[0]

    def xn_copy(step, rows, j):
        return pltpu.make_async_copy(xbuf.at[step % 2, pl.ds(0, rows), pl.ds(j * LANES, LANES)],
                                     xn_hbm.at[pl.ds(step * tm, rows), j, :], sems.at[step % 2, j])

    @pl.when(i == 0)
    def _init():
        carry[...] = jnp.zeros_like(carry)

    @pl.when(i < n_blocks)
    def _prompt_rows():
        _mix_route_compute(xp_ref, ysp_ref, y5p_ref, *refs[:-2], xbuf.at[i % 2])
        for j in range(SLAB_ROWS):
            xn_copy(i, tm, j).start()

    @pl.when(i == n_blocks)
    def _sample_rows():
        _mix_route_compute(xs_ref, yss_ref, y5s_ref, *refs[:-2], xbuf.at[i % 2])
        for j in range(SLAB_ROWS):
            xn_copy(i, n_sample, j).start()
        for j in range(SLAB_ROWS):
            xn_copy(i, n_sample, j).wait()

    @pl.when(i > 0)
    def _wait_previous_rows():
        for j in range(SLAB_ROWS):
            xn_copy(i - 1, tm, j).wait()

    cnt_ref[...] = carry[...]


def _mix_route_compute(x_ref, ys_ref, y5_ref, wa_ref, wb_ref, nf_ref, wrh_ref, wrl_ref, br_ref,
                       x1_ref, _xn, rt_ref, _cnt, carry, xn_buf):
    rows = x_ref.shape[0]
    x1 = x_ref[...] + _dot(ys_ref[...], wa_ref[...]) + _dot(y5_ref[...].astype(BF16), wb_ref[...])
    x1_ref[0:rows, :] = x1
    xn = _rms(x1, nf_ref[...])
    xn_buf[0:rows, :] = xn

    xh = xn.astype(BF16)
    xl = (xn - xh.astype(F32)).astype(BF16)
    logits = _dot(xh, wrh_ref[...]) + _dot(xl, wrh_ref[...]) + _dot(xh, wrl_ref[...]) + br_ref[...]
    tm = logits.shape[0]
    lane = lax.broadcasted_iota(jnp.int32, logits.shape, 1).astype(F32)
    neg = -jnp.inf
    big = float(LANES)

    def first_max(v):
        m = jnp.max(v, axis=-1, keepdims=True)
        return m, jnp.min(jnp.where(v == m, lane, big), axis=-1, keepdims=True)

    coarse = lane < MOE_GROUPS
    mc, gsel = first_max(jnp.where(coarse, logits, neg))
    psel = 1.0 / jnp.sum(jnp.where(coarse, jnp.exp(logits - mc), 0.0), axis=-1, keepdims=True)
    lo = MOE_GROUPS + MOE_EPG * gsel
    lf = jnp.where((lane >= lo) & (lane < lo + MOE_EPG), logits, neg)
    m1, i1 = first_max(lf)
    m2, i2 = first_max(jnp.where(lane == i1, neg, lf))
    e2 = jnp.exp(m2 - m1)
    g1 = psel / (1.0 + e2)
    g2 = psel * e2 / (1.0 + e2)
    e_a, e_b = i1 - MOE_GROUPS, i2 - MOE_GROUPS

    pick_a, pick_b = lane == e_a, lane == e_b
    picks = jnp.where(pick_a | pick_b, 1.0, 0.0)
    earlier = lax.broadcasted_iota(jnp.int32, (tm, tm), 0) > lax.broadcasted_iota(jnp.int32, (tm, tm), 1)
    prior = _dot(earlier.astype(BF16), picks.astype(BF16)) + carry[...]
    rank_a = jnp.sum(jnp.where(pick_a, prior, 0.0), axis=-1, keepdims=True)
    rank_b = jnp.sum(jnp.where(pick_b, prior, 0.0), axis=-1, keepdims=True)
    carry[...] = prior[tm - 1:tm, :] + picks[tm - 1:tm, :]

    out = jnp.zeros_like(logits)
    for k, v in enumerate((e_a, e_b, g1, g2, rank_a, rank_b)):
        out = jnp.where(lane == float(k), v, out)
    rt_ref[0:rows, :] = out


def _mix_route(prompt, sample, consts, tm):
    n_prompt, n_sample = prompt[0].shape[0], sample[0].shape[0]
    assert n_prompt % tm == 0 and n_sample <= tm
    n_blocks = n_prompt // tm
    total_rows = n_prompt + n_sample
    row = lambda w: pl.BlockSpec((tm, w), lambda i: (jnp.minimum(i, n_blocks - 1), 0))
    out_row = lambda w: pl.BlockSpec((tm, w), lambda i: (i, 0))
    return pl.pallas_call(
        functools.partial(_mix_route_body, n_blocks),
        grid=(n_blocks + 1,),
        in_specs=([row(D_MODEL), row(SSD_WIDTH), row(S5_WIDTH)] + [_full_spec(a) for a in sample]
                  + [_full_spec(a) for a in consts]),
        out_specs=[out_row(D_MODEL), pl.BlockSpec(memory_space=pl.ANY),
                   out_row(LANES), pl.BlockSpec((1, LANES), lambda i: (0, 0))],
        out_shape=[jax.ShapeDtypeStruct((total_rows, D_MODEL), F32),
                   jax.ShapeDtypeStruct((total_rows, SLAB_ROWS, LANES), F32),
                   jax.ShapeDtypeStruct((total_rows, LANES), F32), jax.ShapeDtypeStruct((1, LANES), F32)],
        scratch_shapes=[pltpu.VMEM((1, LANES), F32), pltpu.VMEM((2, tm, D_MODEL), F32),
                        pltpu.SemaphoreType.DMA((2, SLAB_ROWS))],
        compiler_params=pltpu.CompilerParams(dimension_semantics=("arbitrary",), vmem_limit_bytes=VMEM_LIMIT),
        name="mix_route",
    )(*prompt, *sample, *consts)


def _sc_mesh():
    return plsc.VectorSubcoreMesh(core_axis_name="c", subcore_axis_name="s")


def _sc_worker():
    return lax.axis_index("s") * SC_CORES + lax.axis_index("c")


def _sc_dispatch(xn, pos_a, pos_b, n_rows):
    n_tok = xn.shape[0]
    ch = SC_DISPATCH_ROWS
    assert n_tok % ch == 0

    @functools.partial(
        pl.kernel, mesh=_sc_mesh(),
        out_type=jax.ShapeDtypeStruct((n_rows, SLAB_ROWS, LANES), F32),
        scratch_types=[pltpu.VMEM((ch,), jnp.int32), pltpu.VMEM((ch,), jnp.int32),
                       pltpu.VMEM((ch, SLAB_ROWS, LANES), F32), pltpu.SemaphoreType.DMA])
    def push(xn_hbm, pa_hbm, pb_hbm, xs_hbm, ia, ib, rows, sem):
        @pl.loop(_sc_worker(), n_tok // ch, step=SC_WORKERS)
        def _(c):
            off = pl.multiple_of(c * ch, ch)
            pltpu.sync_copy(pa_hbm.at[pl.ds(off, ch)], ia)
            pltpu.sync_copy(pb_hbm.at[pl.ds(off, ch)], ib)
            pltpu.sync_copy(xn_hbm.at[pl.ds(off, ch)], rows)
            pltpu.async_copy(rows, xs_hbm.at[ia], sem).wait()
            pltpu.async_copy(rows, xs_hbm.at[ib], sem).wait()

    return push(xn, pos_a, pos_b)


def _sc_collect(ysorted, pos_flat):
    n_pick = pos_flat.shape[0]
    ch = SC_COLLECT_ROWS
    per_worker = n_pick // SC_WORKERS
    n_chunks = per_worker // ch
    assert n_pick % SC_WORKERS == 0 and per_worker % ch == 0

    @functools.partial(
        pl.kernel, mesh=_sc_mesh(),
        out_type=jax.ShapeDtypeStruct((n_pick, SLAB_ROWS, LANES), F32),
        scratch_types=[pltpu.VMEM((ch,), jnp.int32), pltpu.VMEM((ch,), jnp.int32),
                       pltpu.VMEM((ch, SLAB_ROWS, LANES), F32), pltpu.VMEM((ch, SLAB_ROWS, LANES), F32),
                       pltpu.SemaphoreType.DMA, pltpu.SemaphoreType.DMA])
    def pull(ys_hbm, pos_hbm, out_hbm, idx0, idx1, rows0, rows1, sem0, sem1):
        base = _sc_worker() * per_worker
        bufs = ((idx0, rows0, sem0), (idx1, rows1, sem1))

        def offset(j):
            return pl.multiple_of(base + j * ch, SUBLANES)

        def fetch(j, b):
            idx, rows, sem = bufs[b]
            pltpu.sync_copy(pos_hbm.at[pl.ds(offset(j), ch)], idx)
            pltpu.async_copy(ys_hbm.at[idx], rows, sem)

        def flush(j, b):
            idx, rows, sem = bufs[b]
            pltpu.make_async_copy(ys_hbm.at[idx], rows, sem).wait()
            pltpu.sync_copy(rows, out_hbm.at[pl.ds(offset(j), ch)])

        fetch(0, 0)

        @pl.loop(0, n_chunks // 2)
        def _(p):
            j = 2 * p
            fetch(j + 1, 1)
            flush(j, 0)

            @pl.when(j + 2 < n_chunks)
            def _():
                fetch(j + 2, 0)

            flush(j + 1, 1)

        if n_chunks % 2:
            flush(n_chunks - 1, 0)

    return pull(ysorted, pos_flat)


def _slab_columns(ref, rows, j):
    return ref[pl.ds(j, rows, stride=SLAB_ROWS), :]


def _moe_ffn_body(te_ref, nused_ref, x_ref, wg_ref, wu_ref, wd_ref, y_ref, wgb, wub, wdb):
    i = pl.program_id(0)

    @pl.when(i >= nused_ref[0])
    def _unused_tile():
        y_ref[...] = jnp.zeros_like(y_ref)

    @pl.when(i < nused_ref[0])
    def _tile():
        @pl.when((i == 0) | (te_ref[i] != te_ref[jnp.maximum(i - 1, 0)]))
        def _cast_weights():
            wgb[...] = wg_ref[0].astype(BF16)
            wub[...] = wu_ref[0].astype(BF16)
            wdb[...] = wd_ref[0].astype(BF16)

        x = jnp.concatenate([_slab_columns(x_ref, MOE_TILE, j) for j in range(SLAB_ROWS)], axis=-1).astype(BF16)
        gate = _dot(x, wgb[...])
        hmid = (gate * jax.nn.sigmoid(gate)) * _dot(x, wub[...])
        y = _dot(hmid.astype(BF16), wdb[...])
        for j in range(SLAB_ROWS):
            y_ref[pl.ds(j, MOE_TILE, stride=SLAB_ROWS), :] = y[:, j * LANES:(j + 1) * LANES]


def _moe_ffn(tile_expert, n_used, xsorted, w_gate, w_up, w_down):
    n_tiles = tile_expert.shape[0]
    wspec = lambda s: pl.BlockSpec((1,) + s, lambda i, te, nu: (te[i], 0, 0))
    tile = lambda imap: pl.BlockSpec((MOE_TILE * SLAB_ROWS, LANES), imap)
    return pl.pallas_call(
        _moe_ffn_body,
        grid_spec=pltpu.PrefetchScalarGridSpec(
            num_scalar_prefetch=2,
            grid=(n_tiles,),
            in_specs=[tile(lambda i, te, nu: (jnp.clip(i, 0, jnp.maximum(nu[0] - 1, 0)), 0)),
                      wspec((D_MODEL, MOE_D_FF)), wspec((D_MODEL, MOE_D_FF)), wspec((MOE_D_FF, D_MODEL))],
            out_specs=tile(lambda i, te, nu: (i, 0)),
            scratch_shapes=[pltpu.VMEM((D_MODEL, MOE_D_FF), BF16), pltpu.VMEM((D_MODEL, MOE_D_FF), BF16),
                            pltpu.VMEM((MOE_D_FF, D_MODEL), BF16)]),
        out_shape=jax.ShapeDtypeStruct(xsorted.shape, F32),
        compiler_params=pltpu.CompilerParams(dimension_semantics=("arbitrary",), vmem_limit_bytes=VMEM_LIMIT),
        name="moe_ffn",
    )(tile_expert, n_used, xsorted, w_gate, w_up, w_down)


def _combine_body(x1_ref, rt_ref, ya_ref, yb_ref, nf_ref, out_ref):
    rt = rt_ref[...]
    x1 = x1_ref[...]
    tm = x1.shape[0]
    ya, yb = ya_ref.at[0], yb_ref.at[0]
    x2 = jnp.concatenate(
        [x1[:, j * LANES:(j + 1) * LANES] + rt[:, 2:3] * _slab_columns(ya, tm, j) + rt[:, 3:4] * _slab_columns(yb, tm, j)
         for j in range(SLAB_ROWS)], axis=-1)
    out_ref[...] = _rms(x2, nf_ref[...])


def _combine(x1, rt, y_picks, nf, tm, rows, row_block_offset):
    row = lambda w: pl.BlockSpec((tm, w), lambda i: (i + row_block_offset, 0))
    pick = lambda k: pl.BlockSpec((1, tm * SLAB_ROWS, LANES), lambda i: (k, i + row_block_offset, 0))
    return pl.pallas_call(
        _combine_body,
        grid=(rows // tm,),
        in_specs=[row(D_MODEL), row(LANES), pick(0), pick(1),
                  pl.BlockSpec((1, D_MODEL), lambda i: (0, 0))],
        out_specs=pl.BlockSpec((tm, D_MODEL), lambda i: (i, 0)),
        out_shape=jax.ShapeDtypeStruct((rows, D_MODEL), F32),
        compiler_params=pltpu.CompilerParams(dimension_semantics=("parallel",), vmem_limit_bytes=VMEM_LIMIT),
        name="moe_combine",
    )(x1, rt, y_picks, y_picks, nf)


def _route_tables(counts, eid, rank, n_tiles):
    tiles_per = (counts + MOE_TILE - 1) // MOE_TILE
    tile_end = jnp.cumsum(tiles_per)
    pstart = (tile_end - tiles_per) * MOE_TILE
    experts = jnp.arange(MOE_EXPERTS, dtype=jnp.int32)
    pos = [jnp.sum(jnp.where(e[:, None] == experts, pstart, 0), axis=-1) + r for e, r in zip(eid, rank)]
    n_used = tile_end[-1]
    tiles = jnp.arange(n_tiles, dtype=jnp.int32)
    tile_expert = jnp.sum((tile_end[None, :] <= jnp.minimum(tiles, n_used - 1)[:, None]).astype(jnp.int32), axis=1)
    return pos, tile_expert, n_used.reshape(1).astype(jnp.int32)


def _s5_tables(a_re, a_im, log_dt, b_re, b_im, c_re, c_im):
    dt = jnp.exp(log_dt)[:, None]
    mag = jnp.exp(a_re * dt)
    ab_re = mag * jnp.cos(a_im * dt)
    ab_im = mag * jnp.sin(a_im * dt)
    den = a_re * a_re + a_im * a_im
    nr = ab_re - 1.0
    q_re = (nr * a_re + ab_im * a_im) / den
    q_im = (ab_im * a_re - nr * a_im) / den
    bb_re = q_re[..., None] * b_re - q_im[..., None] * b_im
    bb_im = q_re[..., None] * b_im + q_im[..., None] * b_re
    nblk = S5_GROUPS // 16
    kw, nw = 16 * S5_GROUP_CH, 16 * S5_STATE
    same_group = (jnp.arange(kw)[:, None] // S5_GROUP_CH) == (jnp.arange(nw)[None, :] // S5_STATE)

    def in_map(bb):
        rows = bb.reshape(nblk, 16, S5_STATE, S5_GROUP_CH).transpose(0, 1, 3, 2).reshape(nblk, kw, S5_STATE)
        return jnp.where(same_group, jnp.tile(rows, (1, 1, 16)), 0.0)

    def out_map(cc):
        cols = cc.reshape(nblk, 16, S5_GROUP_CH, S5_STATE).transpose(0, 3, 1, 2).reshape(nblk, S5_STATE, kw)
        return jnp.where(same_group.T, jnp.tile(cols, (1, 16, 1)), 0.0)

    wb = jnp.concatenate([in_map(bb_re), in_map(bb_im)], axis=-1).astype(BF16)
    return (wb, ab_re.reshape(1, S5_LANES), ab_im.reshape(1, S5_LANES),
            out_map(c_re).astype(BF16), out_map(-c_im).astype(BF16))


def kernel(x_prompt, x_sample, state_ssd_conv, state_ssd_ssm, state_s5_re, state_s5_im, meta_tokens, norm_mix, w_in, conv_w, conv_b, dt_bias, a_log, d_ssd, ssd_norm, s5_a_re, s5_a_im, s5_log_dt, s5_b_re, s5_b_im, s5_c_re, s5_c_im, s5_d, w_glu, b_glu, s5_norm, w_out, norm_ffn, router_coarse_w, router_coarse_b, router_fine_w, router_fine_b, w_gate, w_up, w_down, norm_final):
    bp, seq, _ = x_prompt.shape
    bs = x_sample.shape[0]
    n_prompt = bp * seq
    n_tok = n_prompt + bs
    row2 = lambda v: v.reshape(1, -1)
    pad_heads = lambda v: jnp.pad(v, (0, LANES - SSD_HEADS)).reshape(1, LANES)

    w = w_in[0]
    o1, o2, o3 = SSD_WIDTH, SSD_WIDTH + SSD_CONV_DIM, SSD_WIDTH + SSD_CONV_DIM + SSD_HEADS
    wz, wx, wu = w[:, :o1].astype(BF16), w[:, o1:o2].astype(BF16), w[:, o3:].astype(BF16)
    wdt = jnp.pad(w[:, o2:o3], ((0, 0), (0, LANES - SSD_HEADS))).astype(BF16)
    g_mix = row2(norm_mix[0])
    cw, cb = conv_w[0], row2(conv_b[0])
    dtb, alog = pad_heads(dt_bias[0]), pad_heads(a_log[0])
    dexp = row2(jnp.repeat(d_ssd[0], SSD_HEAD_DIM))
    snrm = row2(ssd_norm[0])
    eexp = (jnp.arange(LANES)[:, None] == (jnp.arange(SSD_WIDTH) // SSD_HEAD_DIM)[None, :]).astype(BF16)
    wb5, ab_re, ab_im, wcr, wci = _s5_tables(s5_a_re[0], s5_a_im[0], s5_log_dt[0], s5_b_re[0], s5_b_im[0],
                                             s5_c_re[0], s5_c_im[0])
    d5, wglu, bglu, nrm5 = row2(s5_d[0]), w_glu[0].astype(BF16), row2(b_glu[0]), row2(s5_norm[0])
    wo_a, wo_b = w_out[0][:SSD_WIDTH].astype(BF16), w_out[0][SSD_WIDTH:].astype(BF16)
    w_r = jnp.concatenate([router_coarse_w[0], router_fine_w[0].transpose(1, 0, 2).reshape(D_MODEL, MOE_EXPERTS)], axis=1)
    w_r = jnp.pad(w_r, ((0, 0), (0, LANES - w_r.shape[1])))
    wrh = w_r.astype(BF16)
    wrl = (w_r - wrh.astype(F32)).astype(BF16)
    b_r = jnp.concatenate([router_coarse_b[0], router_fine_b[0].reshape(-1)])
    b_r = jnp.pad(b_r, (0, LANES - b_r.shape[0])).reshape(1, LANES)

    zp, xbcp, dtp, up = _in_proj(x_prompt.reshape(n_prompt, D_MODEL), g_mix, wz, wx, wdt, wu, TOK_TILE, BF16, F32)
    xsm = jnp.concatenate([x_sample.reshape(bs, D_MODEL), meta_tokens], axis=0)
    zs, xbcs, dts, us = _in_proj(xsm, g_mix, wz, wx, wdt, wu, xsm.shape[0], F32, F32)

    front = SSD_CHUNK - N_META
    padf = lambda a: jnp.pad(a[bs:], ((front, 0), (0, 0)))[None]
    gw = SSD_HPG * SSD_HEAD_DIM
    ssd_consts = (cw, cb, dtb, alog, dexp, snrm, eexp)
    _, ctail_m, _, ht_m = _ssd_chunked(
        padf(xbcs), padf(dts), jnp.zeros((1, SSD_CHUNK, SSD_WIDTH), F32),
        jnp.zeros((1, SUBLANES, SSD_CONV_DIM), F32), jnp.zeros((1, SSD_GROUPS, SSD_STATE, gw), F32),
        *ssd_consts, mask_rows=front)
    y_ssd_p, ctail_p, ssm_p, _ = _ssd_chunked(
        xbcp.reshape(bp, seq, SSD_CONV_DIM), dtp.reshape(bp, seq, LANES), zp.reshape(bp, seq, SSD_WIDTH),
        ctail_m, ht_m, *ssd_consts, mask_rows=0)

    abr8, abi8 = jnp.broadcast_to(ab_re, (bp, S5_LANES)), jnp.broadcast_to(ab_im, (bp, S5_LANES))
    um8 = jnp.repeat(us[bs:], bp, axis=0).astype(BF16)
    y_s5_p, s5re_p, s5im_p = _s5_seq(up.reshape(bp, seq, S5_WIDTH), um8, wb5, abr8, abi8,
                                     wcr, wci, d5, wglu, bglu, nrm5)

    cst = state_ssd_conv[0]
    xt_s, dt_s, dec_s, bc, xs_s = _ssd_step_prep(xbcs[:bs], cst[:, 0], cst[:, 1], cst[:, 2], dts[:bs],
                                                 cw, cb, dtb, alog)
    ssm_s, y_core = _ssd_step(dt_s[:, :SSD_HEADS].reshape(-1), dec_s[:, :SSD_HEADS].reshape(-1),
                              state_ssd_ssm[0], xt_s, bc)
    y_ssd_s, y_s5_s, s5re_s, s5im_s = _sample_post(
        y_core, xs_s, zs[:bs], dexp, snrm, us[:bs], state_s5_re[0].reshape(bs, S5_LANES),
        state_s5_im[0].reshape(bs, S5_LANES), wb5, ab_re, ab_im, wcr, wci, d5, wglu, bglu, nrm5)

    route_consts = (wo_a, wo_b, row2(norm_ffn[0]), wrh, wrl, b_r)
    x1, xn, rt, counts = _mix_route(
        (x_prompt.reshape(n_prompt, D_MODEL), y_ssd_p.reshape(n_prompt, SSD_WIDTH), y_s5_p.reshape(n_prompt, S5_WIDTH)),
        (x_sample.reshape(bs, D_MODEL), y_ssd_s, y_s5_s), route_consts, TOK_TILE)

    n_tiles = -(-2 * n_tok // MOE_TILE) + MOE_EXPERTS
    lane_i32 = lambda k: rt[:, k].astype(jnp.int32)
    eid = [jnp.clip(lane_i32(k), 0, MOE_EXPERTS - 1) for k in (0, 1)]
    (pos_a, pos_b), tile_expert, n_used = _route_tables(counts[0, :MOE_EXPERTS].astype(jnp.int32), eid,
                                                        [lane_i32(4), lane_i32(5)], n_tiles)
    slabs = lambda a: a.reshape(-1, SLAB_ROWS, LANES)
    xsorted = _sc_dispatch(xn, pos_a, pos_b, n_tiles * MOE_TILE)
    ysorted = _moe_ffn(tile_expert, n_used, xsorted.reshape(-1, LANES), w_gate[0], w_up[0], w_down[0])
    y_picks = _sc_collect(slabs(ysorted), jnp.concatenate([pos_a, pos_b])).reshape(2, n_tok * SLAB_ROWS, LANES)
    nfin = row2(norm_final)
    y_p = _combine(x1, rt, y_picks, nfin, MOE_TILE, n_prompt, 0)
    y_s = _combine(x1, rt, y_picks, nfin, bs, bs, n_prompt // bs)

    s5_state = lambda a, b: a.reshape(1, b, S5_GROUPS, S5_STATE)
    new_conv_s = jnp.stack([cst[:, 1], cst[:, 2], xbcs[:bs]], axis=1)[None]
    return (y_p.reshape(bp, seq, D_MODEL), y_s.reshape(bs, 1, D_MODEL),
            ctail_p[:, SUBLANES - (SSD_CONV - 1):][None], ssm_p[None], s5_state(s5re_p, bp), s5_state(s5im_p, bp),
            new_conv_s, ssm_s[None], s5_state(s5re_s, bs), s5_state(s5im_s, bs))
```

```python
import functools

import jax
import jax.numpy as jnp
from jax import lax
from jax.experimental import pallas as pl
from jax.experimental.pallas import tpu as pltpu
from jax.experimental.pallas import tpu_sc as plsc

F32, BF16 = jnp.float32, jnp.bfloat16

D_MODEL = 1024
N_META = 16
SSD_WIDTH = 1024
SSD_HEAD_DIM = 64
SSD_HEADS = 16
SSD_GROUPS = 2
SSD_HPG = SSD_HEADS // SSD_GROUPS
SSD_STATE = 128
SSD_CONV = 4
SSD_CHUNK = 128
SSD_CONV_DIM = SSD_WIDTH + 2 * SSD_GROUPS * SSD_STATE
S5_WIDTH = 1024
S5_GROUP_CH = 16
S5_GROUPS = 64
S5_STATE = 64
S5_LANES = S5_GROUPS * S5_STATE
MOE_GROUPS = 4
MOE_EPG = 8
MOE_EXPERTS = MOE_GROUPS * MOE_EPG
MOE_D_FF = 512
EPS = 1e-6

LANES = 128
SUBLANES = 8
VMEM_LIMIT = 56 * 1024 * 1024

S5_TIME_TILE = 64
S5_SCAN_LANES = 512
MOE_TILE = 256
SLAB_ROWS = D_MODEL // LANES
SC_CORES = 2
SC_SUBCORES = 16
SC_WORKERS = SC_CORES * SC_SUBCORES
SC_DISPATCH_ROWS = 64
SC_COLLECT_ROWS = (32, 40)
TOK_TILE = 512


def _dot(a, b):
    return jnp.dot(a, b, preferred_element_type=F32)


def _rms(x, g):
    return x * lax.rsqrt(jnp.mean(x * x, axis=-1, keepdims=True) + EPS) * g


def _softplus(x):
    return jnp.maximum(x, 0.0) + jnp.log1p(jnp.exp(-jnp.abs(x)))


def _split3(x):
    hi = x.astype(BF16)
    r = x - hi.astype(F32)
    mid = r.astype(BF16)
    lo = (r - mid.astype(F32)).astype(BF16)
    return hi, mid, lo


def _dot3(x, w):
    hi, mid, lo = _split3(x)
    return _dot(hi, w) + _dot(mid, w) + _dot(lo, w)


def _dot3_left(w, x):
    hi, mid, lo = _split3(x)
    return _dot(w, hi) + _dot(w, mid) + _dot(w, lo)


def _full_spec(a):
    nd = a.ndim
    return pl.BlockSpec(a.shape, lambda *_: (0,) * nd)


def _resident_spec(a):
    nd = a.ndim
    return pl.BlockSpec(a.shape, lambda *_: (0,) * nd, pipeline_mode=pl.Buffered(1))


def _in_proj_body(x_ref, g_ref, wz_ref, wx_ref, wdt_ref, wu_ref, z_ref, xbc_ref, dt_ref, u_ref):
    xb = _rms(x_ref[...], g_ref[...]).astype(BF16)
    z_ref[...] = _dot(xb, wz_ref[...]).astype(z_ref.dtype)
    xbc_ref[...] = _dot(xb, wx_ref[...]).astype(xbc_ref.dtype)
    dt_ref[...] = _dot(xb, wdt_ref[...])
    u_ref[...] = _dot(xb, wu_ref[...]).astype(u_ref.dtype)


def _in_proj(x2d, g, wz, wx, wdt, wu, tm, act_dtype, u_dtype):
    rows = x2d.shape[0]
    row = lambda w: pl.BlockSpec((tm, w), lambda i: (i, 0))
    return pl.pallas_call(
        _in_proj_body,
        grid=(rows // tm,),
        in_specs=[row(D_MODEL), _full_spec(g), _full_spec(wz), _full_spec(wx), _full_spec(wdt), _full_spec(wu)],
        out_specs=[row(SSD_WIDTH), row(SSD_CONV_DIM), row(LANES), row(S5_WIDTH)],
        out_shape=[jax.ShapeDtypeStruct((rows, SSD_WIDTH), act_dtype),
                   jax.ShapeDtypeStruct((rows, SSD_CONV_DIM), act_dtype),
                   jax.ShapeDtypeStruct((rows, LANES), F32),
                   jax.ShapeDtypeStruct((rows, S5_WIDTH), u_dtype)],
        compiler_params=pltpu.CompilerParams(dimension_semantics=("parallel",), vmem_limit_bytes=VMEM_LIMIT),
        name="in_proj",
    )(x2d, g, wz, wx, wdt, wu)


def _ssd_body(mask_rows, xbc_ref, dt_ref, z_ref, cinit_ref, hinit_ref, cw_ref, cb_ref, dtb_ref, alog_ref,
              dexp_ref, nrm_ref, eexp_ref, y_ref, ctail_ref, st_ref, hto_ref, xwin, hT):
    c = pl.program_id(1)
    L = SSD_CHUNK

    @pl.when(c == 0)
    def _init():
        xwin[0:SUBLANES, :] = cinit_ref[0]
        hT[...] = hinit_ref[0]

    xwin[SUBLANES:SUBLANES + L, :] = xbc_ref[0].astype(F32)
    acc = cb_ref[...]
    for k in range(SSD_CONV):
        off = SUBLANES - (SSD_CONV - 1) + k
        acc = acc + xwin[off:off + L, :] * cw_ref[k:k + 1, :]
    tail = xwin[L:L + SUBLANES, :]
    xwin[0:SUBLANES, :] = tail
    ctail_ref[0] = tail

    xact = acc * jax.nn.sigmoid(acc)
    dt = _softplus(dt_ref[0] + dtb_ref[...])
    if mask_rows:
        valid = lax.broadcasted_iota(jnp.int32, (L, 1), 0) >= mask_rows
        xact = jnp.where(valid, xact, 0.0)
        dt = jnp.where(valid, dt, 0.0)

    a_neg = -jnp.exp(alog_ref[...])
    dA = dt * a_neg
    row_i = lax.broadcasted_iota(jnp.int32, (L, L), 0)
    col_i = lax.broadcasted_iota(jnp.int32, (L, L), 1)
    causal = row_i >= col_i
    tril = causal.astype(BF16)
    cs = _dot3_left(tril, dA)
    csT = cs.T
    dtT = dt.T
    ecs = jnp.exp(cs)
    wdec = jnp.exp(cs[L - 1:L, :] - cs) * dt
    eexp = eexp_ref[...]
    ecs_e = _dot3(ecs, eexp)
    wdec_e = _dot3(wdec, eexp)
    lane = lax.broadcasted_iota(jnp.int32, (L, LANES), 1)
    first_half = lane < SSD_HEAD_DIM

    gw = SSD_HPG * SSD_HEAD_DIM
    y_groups = []
    for g in range(SSD_GROUPS):
        b_g = xact[:, SSD_WIDTH + g * SSD_STATE: SSD_WIDTH + (g + 1) * SSD_STATE]
        c_g = xact[:, SSD_WIDTH + (SSD_GROUPS + g) * SSD_STATE: SSD_WIDTH + (SSD_GROUPS + g + 1) * SSD_STATE]
        b_b = b_g.astype(BF16)
        c_b = c_g.astype(BF16)
        cb = lax.dot_general(c_b, b_b, (((1,), (1,)), ((), ())), preferred_element_type=F32)
        xs_g = xact[:, g * gw:(g + 1) * gw]
        h_prev = hT[g]
        y_off = _dot(c_b, h_prev.astype(BF16)) * ecs_e[:, g * gw:(g + 1) * gw]
        xdec = (xs_g * wdec_e[:, g * gw:(g + 1) * gw]).astype(BF16)
        hT[g] = h_prev * ecs_e[L - 1:L, g * gw:(g + 1) * gw] + _dot(b_g.T.astype(BF16), xdec)
        pieces = []
        for j in range(SSD_HPG // 2):
            xs_pair = xs_g[:, j * LANES:(j + 1) * LANES]
            halves = (jnp.where(first_half, xs_pair, 0.0).astype(BF16),
                      jnp.where(first_half, 0.0, xs_pair).astype(BF16))
            yd = None
            for t in range(2):
                h = g * SSD_HPG + 2 * j + t
                seg = cs[:, h:h + 1] - csT[h:h + 1, :]
                lmat = jnp.exp(jnp.where(causal, seg, -jnp.inf))
                m = (cb * lmat * dtT[h:h + 1, :]).astype(BF16)
                part = _dot(m, halves[t])
                yd = part if yd is None else yd + part
            pieces.append(yd)
        y_groups.append(jnp.concatenate(pieces, axis=-1) + y_off + dexp_ref[:, g * gw:(g + 1) * gw] * xs_g)
    y = jnp.concatenate(y_groups, axis=-1)
    z = z_ref[0].astype(F32)
    y_ref[0] = _rms(y * (z * jax.nn.sigmoid(z)), nrm_ref[...]).astype(y_ref.dtype)

    @pl.when(c == pl.num_programs(1) - 1)
    def _emit():
        hto_ref[0] = hT[...]
        for g in range(SSD_GROUPS):
            t = hT[g].T
            for k in range(SSD_HPG):
                st_ref[0, g * SSD_HPG + k] = t[k * SSD_HEAD_DIM:(k + 1) * SSD_HEAD_DIM, :]


def _ssd_chunked(xbc, dt, z, cinit, hinit, cw, cb, dtb, alog, dexp, nrm, eexp, mask_rows):
    bsz, seq, _ = xbc.shape
    nc = seq // SSD_CHUNK
    gw = SSD_HPG * SSD_HEAD_DIM
    blk = lambda w: pl.BlockSpec((1, SSD_CHUNK, w), lambda b, c: (b, c, 0))
    return pl.pallas_call(
        functools.partial(_ssd_body, mask_rows),
        grid=(bsz, nc),
        in_specs=[blk(SSD_CONV_DIM), blk(LANES), blk(SSD_WIDTH),
                  pl.BlockSpec((1, SUBLANES, SSD_CONV_DIM), lambda b, c: (0, 0, 0)),
                  pl.BlockSpec((1, SSD_GROUPS, SSD_STATE, gw), lambda b, c: (0, 0, 0, 0)),
                  _full_spec(cw), _full_spec(cb), _full_spec(dtb), _full_spec(alog),
                  _full_spec(dexp), _full_spec(nrm), _full_spec(eexp)],
        out_specs=[blk(SSD_WIDTH),
                   pl.BlockSpec((1, SUBLANES, SSD_CONV_DIM), lambda b, c: (b, 0, 0)),
                   pl.BlockSpec((1, SSD_HEADS, SSD_HEAD_DIM, SSD_STATE), lambda b, c: (b, 0, 0, 0)),
                   pl.BlockSpec((1, SSD_GROUPS, SSD_STATE, gw), lambda b, c: (b, 0, 0, 0))],
        out_shape=[jax.ShapeDtypeStruct((bsz, seq, SSD_WIDTH), BF16),
                   jax.ShapeDtypeStruct((bsz, SUBLANES, SSD_CONV_DIM), F32),
                   jax.ShapeDtypeStruct((bsz, SSD_HEADS, SSD_HEAD_DIM, SSD_STATE), F32),
                   jax.ShapeDtypeStruct((bsz, SSD_GROUPS, SSD_STATE, gw), F32)],
        scratch_shapes=[pltpu.VMEM((SUBLANES + SSD_CHUNK, SSD_CONV_DIM), F32),
                        pltpu.VMEM((SSD_GROUPS, SSD_STATE, gw), F32)],
        compiler_params=pltpu.CompilerParams(dimension_semantics=("parallel", "arbitrary"),
                                             vmem_limit_bytes=VMEM_LIMIT),
        name="ssd_chunked",
    )(xbc, dt, z, cinit, hinit, cw, cb, dtb, alog, dexp, nrm, eexp)


def _ssd_step_prep_body(xbc_ref, c0_ref, c1_ref, c2_ref, dt_ref, cw_ref, cb_ref, dtb_ref, alog_ref,
                        xt_ref, dt_out_ref, dec_ref, bc_ref, xs_ref):
    acc = cb_ref[...]
    for k, r in enumerate((c0_ref, c1_ref, c2_ref, xbc_ref)):
        acc = acc + r[...] * cw_ref[k:k + 1, :]
    xact = acc * jax.nn.sigmoid(acc)
    xs = xact[:, :SSD_WIDTH]
    dt = _softplus(dt_ref[...] + dtb_ref[...])
    dt_out_ref[...] = dt
    dec_ref[...] = jnp.exp(dt * -jnp.exp(alog_ref[...]))
    bc_ref[...] = xact[:, SSD_WIDTH:]
    xs_ref[...] = xs
    xt_ref[...] = xs.T.astype(xt_ref.dtype)


def _ssd_step_prep(xbc, c0, c1, c2, dt, cw, cb, dtb, alog):
    n = xbc.shape[0]
    args = (xbc, c0, c1, c2, dt, cw, cb, dtb, alog)
    spec = lambda r, w: pl.BlockSpec((r, w), lambda: (0, 0))
    return pl.pallas_call(
        _ssd_step_prep_body,
        in_specs=[_full_spec(a) for a in args],
        out_specs=[spec(SSD_WIDTH, n), spec(n, LANES), spec(n, LANES), spec(n, 2 * SSD_GROUPS * SSD_STATE),
                   spec(n, SSD_WIDTH)],
        out_shape=[jax.ShapeDtypeStruct((SSD_WIDTH, n), BF16), jax.ShapeDtypeStruct((n, LANES), F32),
                   jax.ShapeDtypeStruct((n, LANES), F32),
                   jax.ShapeDtypeStruct((n, 2 * SSD_GROUPS * SSD_STATE), F32),
                   jax.ShapeDtypeStruct((n, SSD_WIDTH), F32)],
        compiler_params=pltpu.CompilerParams(vmem_limit_bytes=VMEM_LIMIT),
        name="ssd_step_prep",
    )(*args)


def _ssd_step_body(dt_ref, dec_ref, st_ref, xt_ref, bc_ref, so_ref, y_ref):
    n = xt_ref.shape[1]
    gw = SSD_HPG * SSD_HEAD_DIM
    blk = pl.program_id(0)
    seq_id = lax.broadcasted_iota(jnp.int32, (n, SSD_STATE), 0)
    sub_id = lax.broadcasted_iota(jnp.int32, (SUBLANES, gw), 0)
    base = pl.multiple_of(blk * SUBLANES, SUBLANES)
    y_acc = [jnp.zeros((SUBLANES, gw), F32) for _ in range(SSD_GROUPS)]
    for i in range(SUBLANES):
        s = blk * SUBLANES + i
        for g in range(SSD_GROUPS):
            b_all = bc_ref[:, g * SSD_STATE:(g + 1) * SSD_STATE]
            rhs = jnp.where(seq_id == s, b_all, 0.0).astype(BF16)
            outer = _dot(xt_ref[g * gw:(g + 1) * gw, :], rhs)
            news = []
            for k in range(SSD_HPG):
                h = g * SSD_HPG + k
                new = (dec_ref[s * SSD_HEADS + h] * st_ref[i, h]
                       + dt_ref[s * SSD_HEADS + h] * outer[k * SSD_HEAD_DIM:(k + 1) * SSD_HEAD_DIM, :])
                so_ref[i, h] = new
                news.append(new)
            new_g = jnp.concatenate(news, axis=0).astype(BF16)
            c_lo = (SSD_GROUPS + g) * SSD_STATE
            c_blk = bc_ref[pl.ds(base, SUBLANES), c_lo:c_lo + SSD_STATE].astype(BF16)
            r = lax.dot_general(c_blk, new_g, (((1,), (1,)), ((), ())), preferred_element_type=F32)
            y_acc[g] = y_acc[g] + jnp.where(sub_id == i, r, 0.0)
    y_ref[...] = jnp.concatenate(y_acc, axis=-1)


def _ssd_step(dt_flat, dec_flat, state, xt, bc):
    n = state.shape[0]
    st_spec = pl.BlockSpec((SUBLANES, SSD_HEADS, SSD_HEAD_DIM, SSD_STATE), lambda i, *_: (i, 0, 0, 0))
    return pl.pallas_call(
        _ssd_step_body,
        grid_spec=pltpu.PrefetchScalarGridSpec(
            num_scalar_prefetch=2,
            grid=(n // SUBLANES,),
            in_specs=[st_spec, pl.BlockSpec(xt.shape, lambda i, *_: (0, 0)),
                      pl.BlockSpec(bc.shape, lambda i, *_: (0, 0))],
            out_specs=[st_spec, pl.BlockSpec((SUBLANES, SSD_WIDTH), lambda i, *_: (i, 0))]),
        out_shape=[jax.ShapeDtypeStruct(state.shape, F32), jax.ShapeDtypeStruct((n, SSD_WIDTH), F32)],
        compiler_params=pltpu.CompilerParams(dimension_semantics=("parallel",), vmem_limit_bytes=VMEM_LIMIT),
        name="ssd_step",
    )(dt_flat, dec_flat, state, xt, bc)


def _s5_project_in(u_b16, wb_ref, store):
    kw = 16 * S5_GROUP_CH
    nw = 16 * S5_STATE
    for j in range(S5_WIDTH // kw):
        r = _dot(u_b16[:, j * kw:(j + 1) * kw], wb_ref[j])
        store(j, r[:, :nw], r[:, nw:])


def _s5_tail(hre_of, him_of, u_f32, wcr_ref, wci_ref, d_ref, wglu_ref, bglu_ref, nrm_ref):
    cols = []
    for j in range(wcr_ref.shape[0]):
        cols.append(_dot(hre_of(j).astype(BF16), wcr_ref[j]) + _dot(him_of(j).astype(BF16), wci_ref[j]))
    return _s5_finish(cols, u_f32, d_ref, wglu_ref, bglu_ref, nrm_ref)


def _s5_finish(cols, u_f32, d_ref, wglu_ref, bglu_ref, nrm_ref):
    y = jnp.concatenate(cols, axis=-1) + d_ref[...] * u_f32
    y = jax.nn.gelu(y)
    y = y * jax.nn.sigmoid(_dot(y.astype(BF16), wglu_ref[...]) + bglu_ref[...])
    return _rms(y, nrm_ref[...])


def _s5_seq_body(u_hbm, um_ref, wb_ref, abr_ref, abi_ref, wcr_ref, wci_ref, d_ref, wglu_ref, bglu_ref, nrm_ref,
                 y_hbm, sre_ref, sim_ref, ubuf, ybuf, bu, h, in_sems, out_sems):
    j = pl.program_id(0)
    last = pl.num_programs(0) - 1
    lc, bsz = ubuf.shape[1], ubuf.shape[2]
    rows = lc * bsz
    nw = 16 * S5_STATE

    def in_copy(step, b):
        return pltpu.make_async_copy(u_hbm.at[b, pl.ds(step * lc, lc), :], ubuf.at[step % 2, :, b, :],
                                     in_sems.at[step % 2, b])

    def out_copy(step, b):
        return pltpu.make_async_copy(ybuf.at[step % 2, :, b, :], y_hbm.at[b, pl.ds(step * lc, lc), :],
                                     out_sems.at[step % 2, b])

    def project_in(u_b16, nrows):
        def store(jj, re, im):
            bu[0:nrows, jj * nw:(jj + 1) * nw] = re
            bu[0:nrows, S5_LANES + jj * nw:S5_LANES + (jj + 1) * nw] = im
        _s5_project_in(u_b16, wb_ref, store)

    def scan(nsteps):
        for k in range(S5_LANES // S5_SCAN_LANES):
            sl_r = pl.ds(k * S5_SCAN_LANES, S5_SCAN_LANES)
            sl_i = pl.ds(S5_LANES + k * S5_SCAN_LANES, S5_SCAN_LANES)
            ar = abr_ref[:, sl_r]
            ai = abi_ref[:, sl_r]

            def step(l, carry):
                hr, hi = carry
                slab = pl.ds(pl.multiple_of(l * bsz, bsz), bsz)
                nr = ar * hr - ai * hi + bu[slab, sl_r]
                ni = ar * hi + ai * hr + bu[slab, sl_i]
                bu[slab, sl_r] = nr
                bu[slab, sl_i] = ni
                return nr, ni

            hr, hi = lax.fori_loop(0, nsteps, step, (h[:, sl_r], h[:, sl_i]))
            h[:, sl_r] = hr
            h[:, sl_i] = hi

    @pl.when(j == 0)
    def _first():
        for b in range(bsz):
            in_copy(0, b).start()
        h[...] = jnp.zeros_like(h)
        project_in(um_ref[...], N_META * bsz)
        scan(N_META)

    @pl.when(j < last)
    def _prefetch():
        for b in range(bsz):
            in_copy(j + 1, b).start()

    for b in range(bsz):
        in_copy(j, b).wait()
    u2 = ubuf[j % 2].reshape(rows, S5_WIDTH)
    u_b16 = u2.astype(BF16)
    kw = 16 * S5_GROUP_CH

    def project_block(jj):
        r = _dot(u_b16[:, jj * kw:(jj + 1) * kw], wb_ref[jj])
        bu[0:rows, jj * nw:(jj + 1) * nw] = r[:, :nw]
        bu[0:rows, S5_LANES + jj * nw:S5_LANES + (jj + 1) * nw] = r[:, nw:]

    def scan_block(jj):
        for k in range(nw // S5_SCAN_LANES):
            lo = jj * nw + k * S5_SCAN_LANES
            sl_r = slice(lo, lo + S5_SCAN_LANES)
            sl_i = slice(S5_LANES + lo, S5_LANES + lo + S5_SCAN_LANES)
            ar, ai = abr_ref[:, sl_r], abi_ref[:, sl_r]
            hr, hi = h[:, sl_r], h[:, sl_i]
            for l in range(lc):
                slab = slice(l * bsz, (l + 1) * bsz)
                hr, hi = (ar * hr - ai * hi + bu[slab, sl_r], ar * hi + ai * hr + bu[slab, sl_i])
                bu[slab, sl_r] = hr
                bu[slab, sl_i] = hi
            h[:, sl_r] = hr
            h[:, sl_i] = hi

    def readout_block(jj):
        return (_dot(bu[:, jj * nw:(jj + 1) * nw].astype(BF16), wcr_ref[jj])
                + _dot(bu[:, S5_LANES + jj * nw:S5_LANES + (jj + 1) * nw].astype(BF16), wci_ref[jj]))

    n_blocks = S5_WIDTH // kw
    project_block(0)
    cols = []
    for jj in range(n_blocks):
        if jj + 1 < n_blocks:
            project_block(jj + 1)
        scan_block(jj)
        cols.append(readout_block(jj))
    y = _s5_finish(cols, u2, d_ref, wglu_ref, bglu_ref, nrm_ref)
    ybuf[j % 2] = y.reshape(lc, bsz, S5_WIDTH)
    for b in range(bsz):
        out_copy(j, b).start()

    @pl.when(j > 0)
    def _wait_previous_out():
        for b in range(bsz):
            out_copy(j - 1, b).wait()

    @pl.when(j == last)
    def _emit():
        for b in range(bsz):
            out_copy(j, b).wait()
        sre_ref[...] = h[:, 0:S5_LANES]
        sim_ref[...] = h[:, S5_LANES:]


def _s5_seq(u, um, wb, abr, abi, wcr, wci, d, wglu, bglu, nrm):
    bsz, seq, _ = u.shape
    lc = S5_TIME_TILE
    consts = (um, wb, abr, abi, wcr, wci, d, wglu, bglu, nrm)
    st = pl.BlockSpec((bsz, S5_LANES), lambda j: (0, 0))
    return pl.pallas_call(
        _s5_seq_body,
        grid=(seq // lc,),
        in_specs=[pl.BlockSpec(memory_space=pl.ANY)] + [_resident_spec(a) for a in consts],
        out_specs=[pl.BlockSpec(memory_space=pl.ANY), st, st],
        out_shape=[jax.ShapeDtypeStruct((bsz, seq, S5_WIDTH), F32),
                   jax.ShapeDtypeStruct((bsz, S5_LANES), F32), jax.ShapeDtypeStruct((bsz, S5_LANES), F32)],
        scratch_shapes=[pltpu.VMEM((2, lc, bsz, S5_WIDTH), F32), pltpu.VMEM((2, lc, bsz, S5_WIDTH), F32),
                        pltpu.VMEM((lc * bsz, 2 * S5_LANES), F32), pltpu.VMEM((bsz, 2 * S5_LANES), F32),
                        pltpu.SemaphoreType.DMA((2, bsz)), pltpu.SemaphoreType.DMA((2, bsz))],
        compiler_params=pltpu.CompilerParams(dimension_semantics=("arbitrary",), vmem_limit_bytes=VMEM_LIMIT),
        name="s5_seq",
    )(u, *consts)


def _sample_post_body(yc_ref, xs_ref, z_ref, dexp_ref, snrm_ref, u_ref, hr_ref, hi_ref, wb_ref, abr_ref, abi_ref,
                      wcr_ref, wci_ref, d_ref, wglu_ref, bglu_ref, nrm_ref,
                      yssd_ref, ys5_ref, nre_ref, nim_ref):
    z = z_ref[...]
    y = yc_ref[...] + dexp_ref[...] * xs_ref[...]
    yssd_ref[...] = _rms(y * (z * jax.nn.sigmoid(z)), snrm_ref[...]).astype(yssd_ref.dtype)

    u = u_ref[...]
    nw = 16 * S5_STATE
    ar, ai = abr_ref[...], abi_ref[...]

    def store(jj, re, im):
        sl = slice(jj * nw, (jj + 1) * nw)
        h0r, h0i = hr_ref[:, sl], hi_ref[:, sl]
        nre_ref[:, sl] = ar[:, sl] * h0r - ai[:, sl] * h0i + re
        nim_ref[:, sl] = ar[:, sl] * h0i + ai[:, sl] * h0r + im

    _s5_project_in(u.astype(BF16), wb_ref, store)
    slab = lambda ref: (lambda jj: ref[:, jj * nw:(jj + 1) * nw])
    y5 = _s5_tail(slab(nre_ref), slab(nim_ref), u, wcr_ref, wci_ref, d_ref, wglu_ref, bglu_ref, nrm_ref)
    ys5_ref[...] = y5.astype(ys5_ref.dtype)


def _sample_post(yc, xs, z, dexp, snrm, u, h0r, h0i, wb, abr1, abi1, wcr, wci, d, wglu, bglu, nrm):
    n = yc.shape[0]
    args = (yc, xs, z, dexp, snrm, u, h0r, h0i, wb, abr1, abi1, wcr, wci, d, wglu, bglu, nrm)
    spec = lambda w: pl.BlockSpec((n, w), lambda: (0, 0))
    return pl.pallas_call(
        _sample_post_body,
        in_specs=[_full_spec(a) for a in args],
        out_specs=[spec(SSD_WIDTH), spec(S5_WIDTH), spec(S5_LANES), spec(S5_LANES)],
        out_shape=[jax.ShapeDtypeStruct((n, SSD_WIDTH), BF16), jax.ShapeDtypeStruct((n, S5_WIDTH), BF16),
                   jax.ShapeDtypeStruct((n, S5_LANES), F32), jax.ShapeDtypeStruct((n, S5_LANES), F32)],
        compiler_params=pltpu.CompilerParams(vmem_limit_bytes=VMEM_LIMIT),
        name="sample_post",
    )(*args)


def _mix_route_body(n_blocks, xp_ref, ysp_ref, y5p_ref, xs_ref, yss_ref, y5s_ref, *refs):
    xn_hbm, _, cnt_ref, carry, xbuf, sems = refs[-6:]
    i = pl.program_id(0)
    tm, n_sample = xp_ref.shape[0], xs_ref.shape[0]

    def xn_copy(step, rows, j):
        return pltpu.make_async_copy(xbuf.at[step % 2, pl.ds(0, rows), pl.ds(j * LANES, LANES)],
                                     xn_hbm.at[pl.ds(step * tm, rows), j, :], sems.at[step % 2, j])

    @pl.when(i == 0)
    def _init():
        carry[...] = jnp.zeros_like(carry)

    @pl.when(i < n_blocks)
    def _prompt_rows():
        _mix_route_compute(xp_ref, ysp_ref, y5p_ref, *refs[:-2], xbuf.at[i % 2])
        for j in range(SLAB_ROWS):
            xn_copy(i, tm, j).start()

    @pl.when(i == n_blocks)
    def _sample_rows():
        _mix_route_compute(xs_ref, yss_ref, y5s_ref, *refs[:-2], xbuf.at[i % 2])
        for j in range(SLAB_ROWS):
            xn_copy(i, n_sample, j).start()
        for j in range(SLAB_ROWS):
            xn_copy(i, n_sample, j).wait()

    @pl.when(i > 0)
    def _wait_previous_rows():
        for j in range(SLAB_ROWS):
            xn_copy(i - 1, tm, j).wait()

    cnt_ref[...] = carry[...]


def _mix_route_compute(x_ref, ys_ref, y5_ref, wa_ref, wb_ref, nf_ref, wrh_ref, wrl_ref, br_ref,
                       x1_ref, _xn, rt_ref, _cnt, carry, xn_buf):
    rows = x_ref.shape[0]
    x1 = x_ref[...] + _dot(ys_ref[...], wa_ref[...]) + _dot(y5_ref[...].astype(BF16), wb_ref[...])
    x1_ref[0:rows, :] = x1
    xn = _rms(x1, nf_ref[...])
    xn_buf[0:rows, :] = xn

    xh = xn.astype(BF16)
    xl = (xn - xh.astype(F32)).astype(BF16)
    logits = _dot(xh, wrh_ref[...]) + _dot(xl, wrh_ref[...]) + _dot(xh, wrl_ref[...]) + br_ref[...]
    tm = logits.shape[0]
    lane = lax.broadcasted_iota(jnp.int32, logits.shape, 1).astype(F32)
    neg = -jnp.inf
    big = float(LANES)

    def first_max(v):
        m = jnp.max(v, axis=-1, keepdims=True)
        return m, jnp.min(jnp.where(v == m, lane, big), axis=-1, keepdims=True)

    coarse = lane < MOE_GROUPS
    mc, gsel = first_max(jnp.where(coarse, logits, neg))
    psel = 1.0 / jnp.sum(jnp.where(coarse, jnp.exp(logits - mc), 0.0), axis=-1, keepdims=True)
    lo = MOE_GROUPS + MOE_EPG * gsel
    lf = jnp.where((lane >= lo) & (lane < lo + MOE_EPG), logits, neg)
    m1, i1 = first_max(lf)
    m2, i2 = first_max(jnp.where(lane == i1, neg, lf))
    e2 = jnp.exp(m2 - m1)
    g1 = psel / (1.0 + e2)
    g2 = psel * e2 / (1.0 + e2)
    e_a, e_b = i1 - MOE_GROUPS, i2 - MOE_GROUPS

    pick_a, pick_b = lane == e_a, lane == e_b
    picks = jnp.where(pick_a | pick_b, 1.0, 0.0)
    earlier = lax.broadcasted_iota(jnp.int32, (tm, tm), 0) > lax.broadcasted_iota(jnp.int32, (tm, tm), 1)
    prior = _dot(earlier.astype(BF16), picks.astype(BF16)) + carry[...]
    rank_a = jnp.sum(jnp.where(pick_a, prior, 0.0), axis=-1, keepdims=True)
    rank_b = jnp.sum(jnp.where(pick_b, prior, 0.0), axis=-1, keepdims=True)
    carry[...] = prior[tm - 1:tm, :] + picks[tm - 1:tm, :]

    out = jnp.zeros_like(logits)
    for k, v in enumerate((e_a, e_b, g1, g2, rank_a, rank_b)):
        out = jnp.where(lane == float(k), v, out)
    rt_ref[0:rows, :] = out


def _mix_route(prompt, sample, consts, tm):
    n_prompt, n_sample = prompt[0].shape[0], sample[0].shape[0]
    assert n_prompt % tm == 0 and n_sample <= tm
    n_blocks = n_prompt // tm
    total_rows = n_prompt + n_sample
    row = lambda w: pl.BlockSpec((tm, w), lambda i: (jnp.minimum(i, n_blocks - 1), 0))
    out_row = lambda w: pl.BlockSpec((tm, w), lambda i: (i, 0))
    return pl.pallas_call(
        functools.partial(_mix_route_body, n_blocks),
        grid=(n_blocks + 1,),
        in_specs=([row(D_MODEL), row(SSD_WIDTH), row(S5_WIDTH)] + [_full_spec(a) for a in sample]
                  + [_full_spec(a) for a in consts]),
        out_specs=[out_row(D_MODEL), pl.BlockSpec(memory_space=pl.ANY),
                   out_row(LANES), pl.BlockSpec((1, LANES), lambda i: (0, 0))],
        out_shape=[jax.ShapeDtypeStruct((total_rows, D_MODEL), F32),
                   jax.ShapeDtypeStruct((total_rows, SLAB_ROWS, LANES), F32),
                   jax.ShapeDtypeStruct((total_rows, LANES), F32), jax.ShapeDtypeStruct((1, LANES), F32)],
        scratch_shapes=[pltpu.VMEM((1, LANES), F32), pltpu.VMEM((2, tm, D_MODEL), F32),
                        pltpu.SemaphoreType.DMA((2, SLAB_ROWS))],
        compiler_params=pltpu.CompilerParams(dimension_semantics=("arbitrary",), vmem_limit_bytes=VMEM_LIMIT),
        name="mix_route",
    )(*prompt, *sample, *consts)


def _sc_mesh():
    return plsc.VectorSubcoreMesh(core_axis_name="c", subcore_axis_name="s")


def _sc_worker():
    return lax.axis_index("s") * SC_CORES + lax.axis_index("c")


def _sc_dispatch(xn, pos_a, pos_b, n_rows):
    n_tok = xn.shape[0]
    ch = SC_DISPATCH_ROWS
    assert n_tok % ch == 0

    @functools.partial(
        pl.kernel, mesh=_sc_mesh(),
        out_type=jax.ShapeDtypeStruct((n_rows, SLAB_ROWS, LANES), F32),
        scratch_types=[pltpu.VMEM((ch,), jnp.int32), pltpu.VMEM((ch,), jnp.int32),
                       pltpu.VMEM((ch, SLAB_ROWS, LANES), F32), pltpu.SemaphoreType.DMA])
    def push(xn_hbm, pa_hbm, pb_hbm, xs_hbm, ia, ib, rows, sem):
        @pl.loop(_sc_worker(), n_tok // ch, step=SC_WORKERS)
        def _(c):
            off = pl.multiple_of(c * ch, ch)
            pltpu.sync_copy(pa_hbm.at[pl.ds(off, ch)], ia)
            pltpu.sync_copy(pb_hbm.at[pl.ds(off, ch)], ib)
            pltpu.sync_copy(xn_hbm.at[pl.ds(off, ch)], rows)
            pltpu.async_copy(rows, xs_hbm.at[ia], sem).wait()
            pltpu.async_copy(rows, xs_hbm.at[ib], sem).wait()

    return push(xn, pos_a, pos_b)


def _sc_collect(ysorted, pos_flat, ch):
    n_pick = pos_flat.shape[0]
    per_worker = n_pick // SC_WORKERS
    n_chunks = per_worker // ch
    assert n_pick % SC_WORKERS == 0 and per_worker % ch == 0

    @functools.partial(
        pl.kernel, mesh=_sc_mesh(),
        out_type=jax.ShapeDtypeStruct((n_pick, SLAB_ROWS, LANES), F32),
        scratch_types=[pltpu.VMEM((ch,), jnp.int32), pltpu.VMEM((ch,), jnp.int32),
                       pltpu.VMEM((ch, SLAB_ROWS, LANES), F32), pltpu.VMEM((ch, SLAB_ROWS, LANES), F32),
                       pltpu.SemaphoreType.DMA, pltpu.SemaphoreType.DMA])
    def pull(ys_hbm, pos_hbm, out_hbm, idx0, idx1, rows0, rows1, sem0, sem1):
        base = _sc_worker() * per_worker
        bufs = ((idx0, rows0, sem0), (idx1, rows1, sem1))

        def offset(j):
            return pl.multiple_of(base + j * ch, SUBLANES)

        def fetch(j, b):
            idx, rows, sem = bufs[b]
            pltpu.sync_copy(pos_hbm.at[pl.ds(offset(j), ch)], idx)
            pltpu.async_copy(ys_hbm.at[idx], rows, sem)

        def flush(j, b):
            idx, rows, sem = bufs[b]
            pltpu.make_async_copy(ys_hbm.at[idx], rows, sem).wait()
            pltpu.sync_copy(rows, out_hbm.at[pl.ds(offset(j), ch)])

        fetch(0, 0)

        @pl.loop(0, n_chunks // 2)
        def _(p):
            j = 2 * p
            fetch(j + 1, 1)
            flush(j, 0)

            @pl.when(j + 2 < n_chunks)
            def _():
                fetch(j + 2, 0)

            flush(j + 1, 1)

        if n_chunks % 2:
            flush(n_chunks - 1, 0)

    return pull(ysorted, pos_flat)


def _slab_columns(ref, rows, j):
    return ref[pl.ds(j, rows, stride=SLAB_ROWS), :]


def _moe_ffn_body(te_ref, nused_ref, x_ref, wg_ref, wu_ref, wd_ref, y_ref, wgb, wub, wdb):
    i = pl.program_id(0)

    @pl.when(i >= nused_ref[0])
    def _unused_tile():
        y_ref[...] = jnp.zeros_like(y_ref)

    @pl.when(i < nused_ref[0])
    def _tile():
        @pl.when((i == 0) | (te_ref[i] != te_ref[jnp.maximum(i - 1, 0)]))
        def _cast_weights():
            wgb[...] = wg_ref[0].astype(BF16)
            wub[...] = wu_ref[0].astype(BF16)
            wdb[...] = wd_ref[0].astype(BF16)

        x = jnp.concatenate([_slab_columns(x_ref, MOE_TILE, j) for j in range(SLAB_ROWS)], axis=-1).astype(BF16)
        gate = _dot(x, wgb[...])
        hmid = (gate * jax.nn.sigmoid(gate)) * _dot(x, wub[...])
        y = _dot(hmid.astype(BF16), wdb[...])
        for j in range(SLAB_ROWS):
            y_ref[pl.ds(j, MOE_TILE, stride=SLAB_ROWS), :] = y[:, j * LANES:(j + 1) * LANES]


def _moe_ffn(tile_expert, n_used, xsorted, w_gate, w_up, w_down):
    n_tiles = tile_expert.shape[0]
    wspec = lambda s: pl.BlockSpec((1,) + s, lambda i, te, nu: (te[i], 0, 0))
    tile = lambda imap: pl.BlockSpec((MOE_TILE * SLAB_ROWS, LANES), imap)
    return pl.pallas_call(
        _moe_ffn_body,
        grid_spec=pltpu.PrefetchScalarGridSpec(
            num_scalar_prefetch=2,
            grid=(n_tiles,),
            in_specs=[tile(lambda i, te, nu: (jnp.clip(i, 0, jnp.maximum(nu[0] - 1, 0)), 0)),
                      wspec((D_MODEL, MOE_D_FF)), wspec((D_MODEL, MOE_D_FF)), wspec((MOE_D_FF, D_MODEL))],
            out_specs=tile(lambda i, te, nu: (i, 0)),
            scratch_shapes=[pltpu.VMEM((D_MODEL, MOE_D_FF), BF16), pltpu.VMEM((D_MODEL, MOE_D_FF), BF16),
                            pltpu.VMEM((MOE_D_FF, D_MODEL), BF16)]),
        out_shape=jax.ShapeDtypeStruct(xsorted.shape, F32),
        compiler_params=pltpu.CompilerParams(dimension_semantics=("arbitrary",), vmem_limit_bytes=VMEM_LIMIT),
        name="moe_ffn",
    )(tile_expert, n_used, xsorted, w_gate, w_up, w_down)


def _combine_body(x1_ref, rt_ref, ya_ref, yb_ref, nf_ref, *rest):
    out_ref = rest[-1]
    rt = rt_ref[...]
    x1 = x1_ref[...]
    tm = x1.shape[0]
    ya, yb = ya_ref.at[0], yb_ref.at[0]
    x2 = jnp.concatenate(
        [x1[:, j * LANES:(j + 1) * LANES] + rt[:, 2:3] * _slab_columns(ya, tm, j) + rt[:, 3:4] * _slab_columns(yb, tm, j)
         for j in range(SLAB_ROWS)], axis=-1)
    out_ref[...] = _rms(x2, nf_ref[...])


def _combine(x1, rt, y_picks, nf, tm, rows, x_block, y_block, out_rows, out_block, out_buf=None):
    row = lambda w: pl.BlockSpec((tm, w), lambda i: (i + x_block, 0))
    pick = lambda k: pl.BlockSpec((1, tm * SLAB_ROWS, LANES), lambda i: (k, i + y_block, 0))
    in_specs = [row(D_MODEL), row(LANES), pick(0), pick(1), pl.BlockSpec((1, D_MODEL), lambda i: (0, 0))]
    args = [x1, rt, y_picks, y_picks, nf]
    aliases = {}
    if out_buf is not None:
        in_specs.append(pl.BlockSpec(memory_space=pl.ANY))
        aliases[len(args)] = 0
        args.append(out_buf)
    return pl.pallas_call(
        _combine_body,
        grid=(rows // tm,),
        in_specs=in_specs,
        out_specs=pl.BlockSpec((tm, D_MODEL), lambda i: (i + out_block, 0)),
        out_shape=jax.ShapeDtypeStruct((out_rows, D_MODEL), F32),
        input_output_aliases=aliases,
        compiler_params=pltpu.CompilerParams(dimension_semantics=("parallel",), vmem_limit_bytes=VMEM_LIMIT),
        name="moe_combine",
    )(*args)


def _route_tables(counts, eid, rank, n_tiles):
    tiles_per = (counts + MOE_TILE - 1) // MOE_TILE
    tile_end = jnp.cumsum(tiles_per)
    pstart = (tile_end - tiles_per) * MOE_TILE
    experts = jnp.arange(MOE_EXPERTS, dtype=jnp.int32)
    pos = [jnp.sum(jnp.where(e[:, None] == experts, pstart, 0), axis=-1) + r for e, r in zip(eid, rank)]
    n_used = tile_end[-1]
    tiles = jnp.arange(n_tiles, dtype=jnp.int32)
    tile_expert = jnp.sum((tile_end[None, :] <= jnp.minimum(tiles, n_used - 1)[:, None]).astype(jnp.int32), axis=1)
    return pos, tile_expert, n_used.reshape(1).astype(jnp.int32)


def _s5_tables(a_re, a_im, log_dt, b_re, b_im, c_re, c_im):
    dt = jnp.exp(log_dt)[:, None]
    mag = jnp.exp(a_re * dt)
    ab_re = mag * jnp.cos(a_im * dt)
    ab_im = mag * jnp.sin(a_im * dt)
    den = a_re * a_re + a_im * a_im
    nr = ab_re - 1.0
    q_re = (nr * a_re + ab_im * a_im) / den
    q_im = (ab_im * a_re - nr * a_im) / den
    bb_re = q_re[..., None] * b_re - q_im[..., None] * b_im
    bb_im = q_re[..., None] * b_im + q_im[..., None] * b_re
    nblk = S5_GROUPS // 16
    kw, nw = 16 * S5_GROUP_CH, 16 * S5_STATE
    same_group = (jnp.arange(kw)[:, None] // S5_GROUP_CH) == (jnp.arange(nw)[None, :] // S5_STATE)

    def in_map(bb):
        rows = bb.reshape(nblk, 16, S5_STATE, S5_GROUP_CH).transpose(0, 1, 3, 2).reshape(nblk, kw, S5_STATE)
        return jnp.where(same_group, jnp.tile(rows, (1, 1, 16)), 0.0)

    def out_map(cc):
        cols = cc.reshape(nblk, 16, S5_GROUP_CH, S5_STATE).transpose(0, 3, 1, 2).reshape(nblk, S5_STATE, kw)
        return jnp.where(same_group.T, jnp.tile(cols, (1, 16, 1)), 0.0)

    wb = jnp.concatenate([in_map(bb_re), in_map(bb_im)], axis=-1).astype(BF16)
    return (wb, ab_re.reshape(1, S5_LANES), ab_im.reshape(1, S5_LANES),
            out_map(c_re).astype(BF16), out_map(-c_im).astype(BF16))


def kernel(x_prompt, x_sample, state_ssd_conv, state_ssd_ssm, state_s5_re, state_s5_im, meta_tokens, norm_mix, w_in, conv_w, conv_b, dt_bias, a_log, d_ssd, ssd_norm, s5_a_re, s5_a_im, s5_log_dt, s5_b_re, s5_b_im, s5_c_re, s5_c_im, s5_d, w_glu, b_glu, s5_norm, w_out, norm_ffn, router_coarse_w, router_coarse_b, router_fine_w, router_fine_b, w_gate, w_up, w_down, norm_final):
    bp, seq, _ = x_prompt.shape
    bs = x_sample.shape[0]
    n_prompt = bp * seq
    n_tok = n_prompt + bs
    row2 = lambda v: v.reshape(1, -1)
    pad_heads = lambda v: jnp.pad(v, (0, LANES - SSD_HEADS)).reshape(1, LANES)

    w = w_in[0]
    o1, o2, o3 = SSD_WIDTH, SSD_WIDTH + SSD_CONV_DIM, SSD_WIDTH + SSD_CONV_DIM + SSD_HEADS
    wz, wx, wu = w[:, :o1].astype(BF16), w[:, o1:o2].astype(BF16), w[:, o3:].astype(BF16)
    wdt = jnp.pad(w[:, o2:o3], ((0, 0), (0, LANES - SSD_HEADS))).astype(BF16)
    g_mix = row2(norm_mix[0])
    cw, cb = conv_w[0], row2(conv_b[0])
    dtb, alog = pad_heads(dt_bias[0]), pad_heads(a_log[0])
    dexp = row2(jnp.repeat(d_ssd[0], SSD_HEAD_DIM))
    snrm = row2(ssd_norm[0])
    eexp = (jnp.arange(LANES)[:, None] == (jnp.arange(SSD_WIDTH) // SSD_HEAD_DIM)[None, :]).astype(BF16)
    wb5, ab_re, ab_im, wcr, wci = _s5_tables(s5_a_re[0], s5_a_im[0], s5_log_dt[0], s5_b_re[0], s5_b_im[0],
                                             s5_c_re[0], s5_c_im[0])
    d5, wglu, bglu, nrm5 = row2(s5_d[0]), w_glu[0].astype(BF16), row2(b_glu[0]), row2(s5_norm[0])
    wo_a, wo_b = w_out[0][:SSD_WIDTH].astype(BF16), w_out[0][SSD_WIDTH:].astype(BF16)
    w_r = jnp.concatenate([router_coarse_w[0], router_fine_w[0].transpose(1, 0, 2).reshape(D_MODEL, MOE_EXPERTS)], axis=1)
    w_r = jnp.pad(w_r, ((0, 0), (0, LANES - w_r.shape[1])))
    wrh = w_r.astype(BF16)
    wrl = (w_r - wrh.astype(F32)).astype(BF16)
    b_r = jnp.concatenate([router_coarse_b[0], router_fine_b[0].reshape(-1)])
    b_r = jnp.pad(b_r, (0, LANES - b_r.shape[0])).reshape(1, LANES)

    zp, xbcp, dtp, up = _in_proj(x_prompt.reshape(n_prompt, D_MODEL), g_mix, wz, wx, wdt, wu, TOK_TILE, BF16, F32)
    xsm = jnp.concatenate([x_sample.reshape(bs, D_MODEL), meta_tokens], axis=0)
    zs, xbcs, dts, us = _in_proj(xsm, g_mix, wz, wx, wdt, wu, xsm.shape[0], F32, F32)

    front = SSD_CHUNK - N_META
    padf = lambda a: jnp.pad(a[bs:], ((front, 0), (0, 0)))[None]
    gw = SSD_HPG * SSD_HEAD_DIM
    ssd_consts = (cw, cb, dtb, alog, dexp, snrm, eexp)
    _, ctail_m, _, ht_m = _ssd_chunked(
        padf(xbcs), padf(dts), jnp.zeros((1, SSD_CHUNK, SSD_WIDTH), F32),
        jnp.zeros((1, SUBLANES, SSD_CONV_DIM), F32), jnp.zeros((1, SSD_GROUPS, SSD_STATE, gw), F32),
        *ssd_consts, mask_rows=front)
    y_ssd_p, ctail_p, ssm_p, _ = _ssd_chunked(
        xbcp.reshape(bp, seq, SSD_CONV_DIM), dtp.reshape(bp, seq, LANES), zp.reshape(bp, seq, SSD_WIDTH),
        ctail_m, ht_m, *ssd_consts, mask_rows=0)

    abr8, abi8 = jnp.broadcast_to(ab_re, (bp, S5_LANES)), jnp.broadcast_to(ab_im, (bp, S5_LANES))
    um8 = jnp.repeat(us[bs:], bp, axis=0).astype(BF16)
    y_s5_p, s5re_p, s5im_p = _s5_seq(up.reshape(bp, seq, S5_WIDTH), um8, wb5, abr8, abi8,
                                     wcr, wci, d5, wglu, bglu, nrm5)

    cst = state_ssd_conv[0]
    xt_s, dt_s, dec_s, bc, xs_s = _ssd_step_prep(xbcs[:bs], cst[:, 0], cst[:, 1], cst[:, 2], dts[:bs],
                                                 cw, cb, dtb, alog)
    ssm_s, y_core = _ssd_step(dt_s[:, :SSD_HEADS].reshape(-1), dec_s[:, :SSD_HEADS].reshape(-1),
                              state_ssd_ssm[0], xt_s, bc)
    y_ssd_s, y_s5_s, s5re_s, s5im_s = _sample_post(
        y_core, xs_s, zs[:bs], dexp, snrm, us[:bs], state_s5_re[0].reshape(bs, S5_LANES),
        state_s5_im[0].reshape(bs, S5_LANES), wb5, ab_re, ab_im, wcr, wci, d5, wglu, bglu, nrm5)

    route_consts = (wo_a, wo_b, row2(norm_ffn[0]), wrh, wrl, b_r)
    x1, xn, rt, counts = _mix_route(
        (x_prompt.reshape(n_prompt, D_MODEL), y_ssd_p.reshape(n_prompt, SSD_WIDTH), y_s5_p.reshape(n_prompt, S5_WIDTH)),
        (x_sample.reshape(bs, D_MODEL), y_ssd_s, y_s5_s), route_consts, TOK_TILE)

    n_tiles = -(-2 * n_tok // MOE_TILE) + MOE_EXPERTS
    lane_i32 = lambda k: rt[:, k].astype(jnp.int32)
    eid = [jnp.clip(lane_i32(k), 0, MOE_EXPERTS - 1) for k in (0, 1)]
    (pos_a, pos_b), tile_expert, n_used = _route_tables(counts[0, :MOE_EXPERTS].astype(jnp.int32), eid,
                                                        [lane_i32(4), lane_i32(5)], n_tiles)
    slabs = lambda a: a.reshape(-1, SLAB_ROWS, LANES)
    xsorted = _sc_dispatch(xn, pos_a, pos_b, n_tiles * MOE_TILE)
    ysorted = _moe_ffn(tile_expert, n_used, xsorted.reshape(-1, LANES), w_gate[0], w_up[0], w_down[0])
    nfin = row2(norm_final)

    half = n_prompt // 2

    def collect(lo, hi, ch):
        picks = jnp.concatenate([pos_a[lo:hi], pos_b[lo:hi]])
        return _sc_collect(slabs(ysorted), picks, ch).reshape(2, (hi - lo) * SLAB_ROWS, LANES)

    picks_1 = collect(0, half, SC_COLLECT_ROWS[0])
    picks_2 = collect(half, n_tok, SC_COLLECT_ROWS[1])
    blocks = half // MOE_TILE
    y_p = _combine(x1, rt, picks_1, nfin, MOE_TILE, half, 0, 0, n_prompt, 0)
    y_p = _combine(x1, rt, picks_2, nfin, MOE_TILE, half, blocks, 0, n_prompt, blocks, out_buf=y_p)
    y_s = _combine(x1, rt, picks_2, nfin, bs, bs, n_prompt // bs, half // bs, bs, 0)

    s5_state = lambda a, b: a.reshape(1, b, S5_GROUPS, S5_STATE)
    new_conv_s = jnp.stack([cst[:, 1], cst[:, 2], xbcs[:bs]], axis=1)[None]
    return (y_p.reshape(bp, seq, D_MODEL), y_s.reshape(bs, 1, D_MODEL),
            ctail_p[:, SUBLANES - (SSD_CONV - 1):][None], ssm_p[None], s5_state(s5re_p, bp), s5_state(s5im_p, bp),
            new_conv_s, ssm_s[None], s5_state(s5re_s, bs), s5_state(s5im_s, bs))
```

```python
import functools

import jax
import jax.numpy as jnp
from jax import lax
from jax.experimental import pallas as pl
from jax.experimental.pallas import tpu as pltpu
from jax.experimental.pallas import tpu_sc as plsc

F32, BF16 = jnp.float32, jnp.bfloat16

D_MODEL = 1024
N_META = 16
SSD_WIDTH = 1024
SSD_HEAD_DIM = 64
SSD_HEADS = 16
SSD_GROUPS = 2
SSD_HPG = SSD_HEADS // SSD_GROUPS
SSD_STATE = 128
SSD_CONV = 4
SSD_CHUNK = 128
SSD_CONV_DIM = SSD_WIDTH + 2 * SSD_GROUPS * SSD_STATE
S5_WIDTH = 1024
S5_GROUP_CH = 16
S5_GROUPS = 64
S5_STATE = 64
S5_LANES = S5_GROUPS * S5_STATE
MOE_GROUPS = 4
MOE_EPG = 8
MOE_EXPERTS = MOE_GROUPS * MOE_EPG
MOE_D_FF = 512
EPS = 1e-6

LANES = 128
SUBLANES = 8
VMEM_LIMIT = 56 * 1024 * 1024

S5_TIME_TILE = 64
S5_SCAN_LANES = 512
MOE_TILE = 256
SLAB_ROWS = D_MODEL // LANES
SC_CORES = 2
SC_SUBCORES = 16
SC_WORKERS = SC_CORES * SC_SUBCORES
SC_DISPATCH_ROWS = 64
SC_COLLECT_ROWS = (32, 40)
TOK_TILE = 512


def _dot(a, b):
    return jnp.dot(a, b, preferred_element_type=F32)


def _rms(x, g):
    return x * lax.rsqrt(jnp.mean(x * x, axis=-1, keepdims=True) + EPS) * g


def _softplus(x):
    return jnp.maximum(x, 0.0) + jnp.log1p(jnp.exp(-jnp.abs(x)))


def _split3(x):
    hi = x.astype(BF16)
    r = x - hi.astype(F32)
    mid = r.astype(BF16)
    lo = (r - mid.astype(F32)).astype(BF16)
    return hi, mid, lo


def _dot3(x, w):
    hi, mid, lo = _split3(x)
    return _dot(hi, w) + _dot(mid, w) + _dot(lo, w)


def _dot3_left(w, x):
    hi, mid, lo = _split3(x)
    return _dot(w, hi) + _dot(w, mid) + _dot(w, lo)


def _full_spec(a):
    nd = a.ndim
    return pl.BlockSpec(a.shape, lambda *_: (0,) * nd)


def _resident_spec(a):
    nd = a.ndim
    return pl.BlockSpec(a.shape, lambda *_: (0,) * nd, pipeline_mode=pl.Buffered(1))


def _in_proj_body(x_ref, g_ref, wz_ref, wx_ref, wdt_ref, wu_ref, z_ref, xbc_ref, dt_ref, u_ref):
    xb = _rms(x_ref[...], g_ref[...]).astype(BF16)
    z_ref[...] = _dot(xb, wz_ref[...]).astype(z_ref.dtype)
    xbc_ref[...] = _dot(xb, wx_ref[...]).astype(xbc_ref.dtype)
    dt_ref[...] = _dot(xb, wdt_ref[...])
    u_ref[...] = _dot(xb, wu_ref[...]).astype(u_ref.dtype)


def _in_proj(x2d, g, wz, wx, wdt, wu, tm, act_dtype, u_dtype):
    rows = x2d.shape[0]
    row = lambda w: pl.BlockSpec((tm, w), lambda i: (i, 0))
    return pl.pallas_call(
        _in_proj_body,
        grid=(rows // tm,),
        in_specs=[row(D_MODEL), _full_spec(g), _full_spec(wz), _full_spec(wx), _full_spec(wdt), _full_spec(wu)],
        out_specs=[row(SSD_WIDTH), row(SSD_CONV_DIM), row(LANES), row(S5_WIDTH)],
        out_shape=[jax.ShapeDtypeStruct((rows, SSD_WIDTH), act_dtype),
                   jax.ShapeDtypeStruct((rows, SSD_CONV_DIM), act_dtype),
                   jax.ShapeDtypeStruct((rows, LANES), F32),
                   jax.ShapeDtypeStruct((rows, S5_WIDTH), u_dtype)],
        compiler_params=pltpu.CompilerParams(dimension_semantics=("parallel",), vmem_limit_bytes=VMEM_LIMIT),
        name="in_proj",
    )(x2d, g, wz, wx, wdt, wu)


def _ssd_body(mask_rows, xbc_ref, dt_ref, z_ref, cinit_ref, hinit_ref, cw_ref, cb_ref, dtb_ref, alog_ref,
              dexp_ref, nrm_ref, eexp_ref, y_ref, ctail_ref, st_ref, hto_ref, xwin, hT):
    c = pl.program_id(1)
    L = SSD_CHUNK

    @pl.when(c == 0)
    def _init():
        xwin[...] = cinit_ref[0]
        hT[...] = hinit_ref[0]

    x_b = xbc_ref[0]
    x_f = x_b.astype(F32)
    taps = SSD_CONV - 1
    m_i = lax.broadcasted_iota(jnp.int32, (taps * L, L), 0)
    r_i = lax.broadcasted_iota(jnp.int32, (taps * L, L), 1)
    shift = (r_i + (taps - m_i // L) == m_i % L).astype(BF16)
    shifted = _dot(shift, x_b)
    acc = cb_ref[...] + x_f * cw_ref[taps:taps + 1, :]
    for k in range(taps):
        acc = acc + shifted[k * L:(k + 1) * L, :] * cw_ref[k:k + 1, :]
    joint = jnp.concatenate([xwin[...], x_f[0:SUBLANES, :]], axis=0)
    row8 = lax.broadcasted_iota(jnp.int32, (SUBLANES, 1), 0)
    head = acc[0:SUBLANES, :]
    for k in range(taps):
        d = taps - k
        head = head + jnp.where(row8 < d, joint[SUBLANES - d:2 * SUBLANES - d, :], 0.0) * cw_ref[k:k + 1, :]
    acc = jnp.concatenate([head, acc[SUBLANES:, :]], axis=0)
    tail = x_f[L - SUBLANES:, :]
    xwin[...] = tail
    ctail_ref[0] = tail

    xact = acc * jax.nn.sigmoid(acc)
    dt = _softplus(dt_ref[0] + dtb_ref[...])
    if mask_rows:
        valid = lax.broadcasted_iota(jnp.int32, (L, 1), 0) >= mask_rows
        xact = jnp.where(valid, xact, 0.0)
        dt = jnp.where(valid, dt, 0.0)

    a_neg = -jnp.exp(alog_ref[...])
    dA = dt * a_neg
    row_i = lax.broadcasted_iota(jnp.int32, (L, L), 0)
    col_i = lax.broadcasted_iota(jnp.int32, (L, L), 1)
    causal = row_i >= col_i
    tril = causal.astype(BF16)
    cs = _dot3_left(tril, dA)
    csT = cs.T
    dtT = dt.T
    ecs = jnp.exp(cs)
    wdec = jnp.exp(cs[L - 1:L, :] - cs) * dt
    eexp = eexp_ref[...]
    ecs_e = _dot3(ecs, eexp)
    wdec_e = _dot3(wdec, eexp)
    lane = lax.broadcasted_iota(jnp.int32, (L, LANES), 1)
    first_half = lane < SSD_HEAD_DIM

    gw = SSD_HPG * SSD_HEAD_DIM
    y_groups = []
    for g in range(SSD_GROUPS):
        b_g = xact[:, SSD_WIDTH + g * SSD_STATE: SSD_WIDTH + (g + 1) * SSD_STATE]
        c_g = xact[:, SSD_WIDTH + (SSD_GROUPS + g) * SSD_STATE: SSD_WIDTH + (SSD_GROUPS + g + 1) * SSD_STATE]
        b_b = b_g.astype(BF16)
        c_b = c_g.astype(BF16)
        cb = lax.dot_general(c_b, b_b, (((1,), (1,)), ((), ())), preferred_element_type=F32)
        xs_g = xact[:, g * gw:(g + 1) * gw]
        h_prev = hT[g]
        y_off = _dot(c_b, h_prev.astype(BF16)) * ecs_e[:, g * gw:(g + 1) * gw]
        xdec = (xs_g * wdec_e[:, g * gw:(g + 1) * gw]).astype(BF16)
        hT[g] = h_prev * ecs_e[L - 1:L, g * gw:(g + 1) * gw] + _dot(b_g.T.astype(BF16), xdec)
        pieces = []
        for j in range(SSD_HPG // 2):
            xs_pair = xs_g[:, j * LANES:(j + 1) * LANES]
            halves = (jnp.where(first_half, xs_pair, 0.0).astype(BF16),
                      jnp.where(first_half, 0.0, xs_pair).astype(BF16))
            yd = None
            for t in range(2):
                h = g * SSD_HPG + 2 * j + t
                seg = cs[:, h:h + 1] - csT[h:h + 1, :]
                lmat = jnp.exp(jnp.where(causal, seg, -jnp.inf))
                m = (cb * lmat * dtT[h:h + 1, :]).astype(BF16)
                part = _dot(m, halves[t])
                yd = part if yd is None else yd + part
            pieces.append(yd)
        y_groups.append(jnp.concatenate(pieces, axis=-1) + y_off + dexp_ref[:, g * gw:(g + 1) * gw] * xs_g)
    y = jnp.concatenate(y_groups, axis=-1)
    z = z_ref[0].astype(F32)
    y_ref[0] = _rms(y * (z * jax.nn.sigmoid(z)), nrm_ref[...]).astype(y_ref.dtype)

    @pl.when(c == pl.num_programs(1) - 1)
    def _emit():
        hto_ref[0] = hT[...]
        for g in range(SSD_GROUPS):
            t = hT[g].T
            for k in range(SSD_HPG):
                st_ref[0, g * SSD_HPG + k] = t[k * SSD_HEAD_DIM:(k + 1) * SSD_HEAD_DIM, :]


def _ssd_chunked(xbc, dt, z, cinit, hinit, cw, cb, dtb, alog, dexp, nrm, eexp, mask_rows):
    bsz, seq, _ = xbc.shape
    nc = seq // SSD_CHUNK
    gw = SSD_HPG * SSD_HEAD_DIM
    blk = lambda w: pl.BlockSpec((1, SSD_CHUNK, w), lambda b, c: (b, c, 0))
    return pl.pallas_call(
        functools.partial(_ssd_body, mask_rows),
        grid=(bsz, nc),
        in_specs=[blk(SSD_CONV_DIM), blk(LANES), blk(SSD_WIDTH),
                  pl.BlockSpec((1, SUBLANES, SSD_CONV_DIM), lambda b, c: (0, 0, 0)),
                  pl.BlockSpec((1, SSD_GROUPS, SSD_STATE, gw), lambda b, c: (0, 0, 0, 0)),
                  _full_spec(cw), _full_spec(cb), _full_spec(dtb), _full_spec(alog),
                  _full_spec(dexp), _full_spec(nrm), _full_spec(eexp)],
        out_specs=[blk(SSD_WIDTH),
                   pl.BlockSpec((1, SUBLANES, SSD_CONV_DIM), lambda b, c: (b, 0, 0)),
                   pl.BlockSpec((1, SSD_HEADS, SSD_HEAD_DIM, SSD_STATE), lambda b, c: (b, 0, 0, 0)),
                   pl.BlockSpec((1, SSD_GROUPS, SSD_STATE, gw), lambda b, c: (b, 0, 0, 0))],
        out_shape=[jax.ShapeDtypeStruct((bsz, seq, SSD_WIDTH), BF16),
                   jax.ShapeDtypeStruct((bsz, SUBLANES, SSD_CONV_DIM), F32),
                   jax.ShapeDtypeStruct((bsz, SSD_HEADS, SSD_HEAD_DIM, SSD_STATE), F32),
                   jax.ShapeDtypeStruct((bsz, SSD_GROUPS, SSD_STATE, gw), F32)],
        scratch_shapes=[pltpu.VMEM((SUBLANES, SSD_CONV_DIM), F32),
                        pltpu.VMEM((SSD_GROUPS, SSD_STATE, gw), F32)],
        compiler_params=pltpu.CompilerParams(dimension_semantics=("parallel", "arbitrary"),
                                             vmem_limit_bytes=VMEM_LIMIT),
        name="ssd_chunked",
    )(xbc, dt, z, cinit, hinit, cw, cb, dtb, alog, dexp, nrm, eexp)


def _ssd_step_prep_body(xbc_ref, c0_ref, c1_ref, c2_ref, dt_ref, cw_ref, cb_ref, dtb_ref, alog_ref,
                        xt_ref, dt_out_ref, dec_ref, bc_ref, xs_ref):
    acc = cb_ref[...]
    for k, r in enumerate((c0_ref, c1_ref, c2_ref, xbc_ref)):
        acc = acc + r[...] * cw_ref[k:k + 1, :]
    xact = acc * jax.nn.sigmoid(acc)
    xs = xact[:, :SSD_WIDTH]
    dt = _softplus(dt_ref[...] + dtb_ref[...])
    dt_out_ref[...] = dt
    dec_ref[...] = jnp.exp(dt * -jnp.exp(alog_ref[...]))
    bc_ref[...] = xact[:, SSD_WIDTH:]
    xs_ref[...] = xs
    xt_ref[...] = xs.T.astype(xt_ref.dtype)


def _ssd_step_prep(xbc, c0, c1, c2, dt, cw, cb, dtb, alog):
    n = xbc.shape[0]
    args = (xbc, c0, c1, c2, dt, cw, cb, dtb, alog)
    spec = lambda r, w: pl.BlockSpec((r, w), lambda: (0, 0))
    return pl.pallas_call(
        _ssd_step_prep_body,
        in_specs=[_full_spec(a) for a in args],
        out_specs=[spec(SSD_WIDTH, n), spec(n, LANES), spec(n, LANES), spec(n, 2 * SSD_GROUPS * SSD_STATE),
                   spec(n, SSD_WIDTH)],
        out_shape=[jax.ShapeDtypeStruct((SSD_WIDTH, n), BF16), jax.ShapeDtypeStruct((n, LANES), F32),
                   jax.ShapeDtypeStruct((n, LANES), F32),
                   jax.ShapeDtypeStruct((n, 2 * SSD_GROUPS * SSD_STATE), F32),
                   jax.ShapeDtypeStruct((n, SSD_WIDTH), F32)],
        compiler_params=pltpu.CompilerParams(vmem_limit_bytes=VMEM_LIMIT),
        name="ssd_step_prep",
    )(*args)


def _ssd_step_body(dt_ref, dec_ref, st_ref, xt_ref, bc_ref, so_ref, y_ref):
    n = xt_ref.shape[1]
    gw = SSD_HPG * SSD_HEAD_DIM
    blk = pl.program_id(0)
    seq_id = lax.broadcasted_iota(jnp.int32, (n, SSD_STATE), 0)
    sub_id = lax.broadcasted_iota(jnp.int32, (SUBLANES, gw), 0)
    base = pl.multiple_of(blk * SUBLANES, SUBLANES)
    y_acc = [jnp.zeros((SUBLANES, gw), F32) for _ in range(SSD_GROUPS)]
    for i in range(SUBLANES):
        s = blk * SUBLANES + i
        for g in range(SSD_GROUPS):
            b_all = bc_ref[:, g * SSD_STATE:(g + 1) * SSD_STATE]
            rhs = jnp.where(seq_id == s, b_all, 0.0).astype(BF16)
            outer = _dot(xt_ref[g * gw:(g + 1) * gw, :], rhs)
            news = []
            for k in range(SSD_HPG):
                h = g * SSD_HPG + k
                new = (dec_ref[s * SSD_HEADS + h] * st_ref[i, h]
                       + dt_ref[s * SSD_HEADS + h] * outer[k * SSD_HEAD_DIM:(k + 1) * SSD_HEAD_DIM, :])
                so_ref[i, h] = new
                news.append(new)
            new_g = jnp.concatenate(news, axis=0).astype(BF16)
            c_lo = (SSD_GROUPS + g) * SSD_STATE
            c_blk = bc_ref[pl.ds(base, SUBLANES), c_lo:c_lo + SSD_STATE].astype(BF16)
            r = lax.dot_general(c_blk, new_g, (((1,), (1,)), ((), ())), preferred_element_type=F32)
            y_acc[g] = y_acc[g] + jnp.where(sub_id == i, r, 0.0)
    y_ref[...] = jnp.concatenate(y_acc, axis=-1)


def _ssd_step(dt_flat, dec_flat, state, xt, bc):
    n = state.shape[0]
    st_spec = pl.BlockSpec((SUBLANES, SSD_HEADS, SSD_HEAD_DIM, SSD_STATE), lambda i, *_: (i, 0, 0, 0))
    return pl.pallas_call(
        _ssd_step_body,
        grid_spec=pltpu.PrefetchScalarGridSpec(
            num_scalar_prefetch=2,
            grid=(n // SUBLANES,),
            in_specs=[st_spec, pl.BlockSpec(xt.shape, lambda i, *_: (0, 0)),
                      pl.BlockSpec(bc.shape, lambda i, *_: (0, 0))],
            out_specs=[st_spec, pl.BlockSpec((SUBLANES, SSD_WIDTH), lambda i, *_: (i, 0))]),
        out_shape=[jax.ShapeDtypeStruct(state.shape, F32), jax.ShapeDtypeStruct((n, SSD_WIDTH), F32)],
        compiler_params=pltpu.CompilerParams(dimension_semantics=("parallel",), vmem_limit_bytes=VMEM_LIMIT),
        name="ssd_step",
    )(dt_flat, dec_flat, state, xt, bc)


def _s5_project_in(u_b16, wb_ref, store):
    kw = 16 * S5_GROUP_CH
    nw = 16 * S5_STATE
    for j in range(S5_WIDTH // kw):
        r = _dot(u_b16[:, j * kw:(j + 1) * kw], wb_ref[j])
        store(j, r[:, :nw], r[:, nw:])


def _s5_tail(hre_of, him_of, u_f32, wcr_ref, wci_ref, d_ref, wglu_ref, bglu_ref, nrm_ref):
    cols = []
    for j in range(wcr_ref.shape[0]):
        cols.append(_dot(hre_of(j).astype(BF16), wcr_ref[j]) + _dot(him_of(j).astype(BF16), wci_ref[j]))
    return _s5_finish(cols, u_f32, d_ref, wglu_ref, bglu_ref, nrm_ref)


def _s5_finish(cols, u_f32, d_ref, wglu_ref, bglu_ref, nrm_ref):
    y = jnp.concatenate(cols, axis=-1) + d_ref[...] * u_f32
    y = jax.nn.gelu(y)
    y = y * jax.nn.sigmoid(_dot(y.astype(BF16), wglu_ref[...]) + bglu_ref[...])
    return _rms(y, nrm_ref[...])


def _s5_seq_body(u_hbm, um_ref, wb_ref, abr_ref, abi_ref, wcr_ref, wci_ref, d_ref, wglu_ref, bglu_ref, nrm_ref,
                 y_hbm, sre_ref, sim_ref, ubuf, ybuf, bu, h, in_sems, out_sems):
    j = pl.program_id(0)
    last = pl.num_programs(0) - 1
    lc, bsz = ubuf.shape[1], ubuf.shape[2]
    rows = lc * bsz
    nw = 16 * S5_STATE

    def in_copy(step, b):
        return pltpu.make_async_copy(u_hbm.at[b, pl.ds(step * lc, lc), :], ubuf.at[step % 2, :, b, :],
                                     in_sems.at[step % 2, b])

    def out_copy(step, b):
        return pltpu.make_async_copy(ybuf.at[step % 2, :, b, :], y_hbm.at[b, pl.ds(step * lc, lc), :],
                                     out_sems.at[step % 2, b])

    def project_in(u_b16, nrows):
        def store(jj, re, im):
            bu[0:nrows, jj * nw:(jj + 1) * nw] = re
            bu[0:nrows, S5_LANES + jj * nw:S5_LANES + (jj + 1) * nw] = im
        _s5_project_in(u_b16, wb_ref, store)

    def scan(nsteps):
        for k in range(S5_LANES // S5_SCAN_LANES):
            sl_r = pl.ds(k * S5_SCAN_LANES, S5_SCAN_LANES)
            sl_i = pl.ds(S5_LANES + k * S5_SCAN_LANES, S5_SCAN_LANES)
            ar = abr_ref[:, sl_r]
            ai = abi_ref[:, sl_r]

            def step(l, carry):
                hr, hi = carry
                slab = pl.ds(pl.multiple_of(l * bsz, bsz), bsz)
                nr = ar * hr - ai * hi + bu[slab, sl_r]
                ni = ar * hi + ai * hr + bu[slab, sl_i]
                bu[slab, sl_r] = nr
                bu[slab, sl_i] = ni
                return nr, ni

            hr, hi = lax.fori_loop(0, nsteps, step, (h[:, sl_r], h[:, sl_i]))
            h[:, sl_r] = hr
            h[:, sl_i] = hi

    @pl.when(j == 0)
    def _first():
        for b in range(bsz):
            in_copy(0, b).start()
        h[...] = jnp.zeros_like(h)
        project_in(um_ref[...], N_META * bsz)
        scan(N_META)

    @pl.when(j < last)
    def _prefetch():
        for b in range(bsz):
            in_copy(j + 1, b).start()

    for b in range(bsz):
        in_copy(j, b).wait()
    u2 = ubuf[j % 2].reshape(rows, S5_WIDTH)
    u_b16 = u2.astype(BF16)
    kw = 16 * S5_GROUP_CH

    def project_block(jj):
        r = _dot(u_b16[:, jj * kw:(jj + 1) * kw], wb_ref[jj])
        bu[0:rows, jj * nw:(jj + 1) * nw] = r[:, :nw]
        bu[0:rows, S5_LANES + jj * nw:S5_LANES + (jj + 1) * nw] = r[:, nw:]

    def scan_block(jj):
        for k in range(nw // S5_SCAN_LANES):
            lo = jj * nw + k * S5_SCAN_LANES
            sl_r = slice(lo, lo + S5_SCAN_LANES)
            sl_i = slice(S5_LANES + lo, S5_LANES + lo + S5_SCAN_LANES)
            ar, ai = abr_ref[:, sl_r], abi_ref[:, sl_r]
            hr, hi = h[:, sl_r], h[:, sl_i]
            for l in range(lc):
                slab = slice(l * bsz, (l + 1) * bsz)
                hr, hi = (ar * hr - ai * hi + bu[slab, sl_r], ar * hi + ai * hr + bu[slab, sl_i])
                bu[slab, sl_r] = hr
                bu[slab, sl_i] = hi
            h[:, sl_r] = hr
            h[:, sl_i] = hi

    def readout_block(jj):
        return (_dot(bu[:, jj * nw:(jj + 1) * nw].astype(BF16), wcr_ref[jj])
                + _dot(bu[:, S5_LANES + jj * nw:S5_LANES + (jj + 1) * nw].astype(BF16), wci_ref[jj]))

    n_blocks = S5_WIDTH // kw
    project_block(0)
    cols = []
    for jj in range(n_blocks):
        if jj + 1 < n_blocks:
            project_block(jj + 1)
        scan_block(jj)
        cols.append(readout_block(jj))
    y = _s5_finish(cols, u2, d_ref, wglu_ref, bglu_ref, nrm_ref)
    ybuf[j % 2] = y.reshape(lc, bsz, S5_WIDTH)
    for b in range(bsz):
        out_copy(j, b).start()

    @pl.when(j > 0)
    def _wait_previous_out():
        for b in range(bsz):
            out_copy(j - 1, b).wait()

    @pl.when(j == last)
    def _emit():
        for b in range(bsz):
            out_copy(j, b).wait()
        sre_ref[...] = h[:, 0:S5_LANES]
        sim_ref[...] = h[:, S5_LANES:]


def _s5_seq(u, um, wb, abr, abi, wcr, wci, d, wglu, bglu, nrm):
    bsz, seq, _ = u.shape
    lc = S5_TIME_TILE
    consts = (um, wb, abr, abi, wcr, wci, d, wglu, bglu, nrm)
    st = pl.BlockSpec((bsz, S5_LANES), lambda j: (0, 0))
    return pl.pallas_call(
        _s5_seq_body,
        grid=(seq // lc,),
        in_specs=[pl.BlockSpec(memory_space=pl.ANY)] + [_resident_spec(a) for a in consts],
        out_specs=[pl.BlockSpec(memory_space=pl.ANY), st, st],
        out_shape=[jax.ShapeDtypeStruct((bsz, seq, S5_WIDTH), F32),
                   jax.ShapeDtypeStruct((bsz, S5_LANES), F32), jax.ShapeDtypeStruct((bsz, S5_LANES), F32)],
        scratch_shapes=[pltpu.VMEM((2, lc, bsz, S5_WIDTH), F32), pltpu.VMEM((2, lc, bsz, S5_WIDTH), F32),
                        pltpu.VMEM((lc * bsz, 2 * S5_LANES), F32), pltpu.VMEM((bsz, 2 * S5_LANES), F32),
                        pltpu.SemaphoreType.DMA((2, bsz)), pltpu.SemaphoreType.DMA((2, bsz))],
        compiler_params=pltpu.CompilerParams(dimension_semantics=("arbitrary",), vmem_limit_bytes=VMEM_LIMIT),
        name="s5_seq",
    )(u, *consts)


def _sample_post_body(yc_ref, xs_ref, z_ref, dexp_ref, snrm_ref, u_ref, hr_ref, hi_ref, wb_ref, abr_ref, abi_ref,
                      wcr_ref, wci_ref, d_ref, wglu_ref, bglu_ref, nrm_ref,
                      yssd_ref, ys5_ref, nre_ref, nim_ref):
    z = z_ref[...]
    y = yc_ref[...] + dexp_ref[...] * xs_ref[...]
    yssd_ref[...] = _rms(y * (z * jax.nn.sigmoid(z)), snrm_ref[...]).astype(yssd_ref.dtype)

    u = u_ref[...]
    nw = 16 * S5_STATE
    ar, ai = abr_ref[...], abi_ref[...]

    def store(jj, re, im):
        sl = slice(jj * nw, (jj + 1) * nw)
        h0r, h0i = hr_ref[:, sl], hi_ref[:, sl]
        nre_ref[:, sl] = ar[:, sl] * h0r - ai[:, sl] * h0i + re
        nim_ref[:, sl] = ar[:, sl] * h0i + ai[:, sl] * h0r + im

    _s5_project_in(u.astype(BF16), wb_ref, store)
    slab = lambda ref: (lambda jj: ref[:, jj * nw:(jj + 1) * nw])
    y5 = _s5_tail(slab(nre_ref), slab(nim_ref), u, wcr_ref, wci_ref, d_ref, wglu_ref, bglu_ref, nrm_ref)
    ys5_ref[...] = y5.astype(ys5_ref.dtype)


def _sample_post(yc, xs, z, dexp, snrm, u, h0r, h0i, wb, abr1, abi1, wcr, wci, d, wglu, bglu, nrm):
    n = yc.shape[0]
    args = (yc, xs, z, dexp, snrm, u, h0r, h0i, wb, abr1, abi1, wcr, wci, d, wglu, bglu, nrm)
    spec = lambda w: pl.BlockSpec((n, w), lambda: (0, 0))
    return pl.pallas_call(
        _sample_post_body,
        in_specs=[_full_spec(a) for a in args],
        out_specs=[spec(SSD_WIDTH), spec(S5_WIDTH), spec(S5_LANES), spec(S5_LANES)],
        out_shape=[jax.ShapeDtypeStruct((n, SSD_WIDTH), BF16), jax.ShapeDtypeStruct((n, S5_WIDTH), BF16),
                   jax.ShapeDtypeStruct((n, S5_LANES), F32), jax.ShapeDtypeStruct((n, S5_LANES), F32)],
        compiler_params=pltpu.CompilerParams(vmem_limit_bytes=VMEM_LIMIT),
        name="sample_post",
    )(*args)


def _mix_route_body(n_blocks, xp_ref, ysp_ref, y5p_ref, xs_ref, yss_ref, y5s_ref, *refs):
    xn_hbm, _, cnt_ref, carry, xbuf, sems = refs[-6:]
    i = pl.program_id(0)
    tm, n_sample = xp_ref.shape[0], xs_ref.shape[0]

    def xn_copy(step, rows, j):
        return pltpu.make_async_copy(xbuf.at[step % 2, pl.ds(0, rows), pl.ds(j * LANES, LANES)],
                                     xn_hbm.at[pl.ds(step * tm, rows), j, :], sems.at[step % 2, j])

    @pl.when(i == 0)
    def _init():
        carry[...] = jnp.zeros_like(carry)

    @pl.when(i < n_blocks)
    def _prompt_rows():
        _mix_route_compute(xp_ref, ysp_ref, y5p_ref, *refs[:-2], xbuf.at[i % 2])
        for j in range(SLAB_ROWS):
            xn_copy(i, tm, j).start()

    @pl.when(i == n_blocks)
    def _sample_rows():
        _mix_route_compute(xs_ref, yss_ref, y5s_ref, *refs[:-2], xbuf.at[i % 2])
        for j in range(SLAB_ROWS):
            xn_copy(i, n_sample, j).start()
        for j in range(SLAB_ROWS):
            xn_copy(i, n_sample, j).wait()

    @pl.when(i > 0)
    def _wait_previous_rows():
        for j in range(SLAB_ROWS):
            xn_copy(i - 1, tm, j).wait()

    cnt_ref[...] = carry[...]


def _mix_route_compute(x_ref, ys_ref, y5_ref, wa_ref, wb_ref, nf_ref, wrh_ref, wrl_ref, br_ref,
                       x1_ref, _xn, rt_ref, _cnt, carry, xn_buf):
    rows = x_ref.shape[0]
    x1 = x_ref[...] + _dot(ys_ref[...], wa_ref[...]) + _dot(y5_ref[...].astype(BF16), wb_ref[...])
    x1_ref[0:rows, :] = x1
    xn = _rms(x1, nf_ref[...])
    xn_buf[0:rows, :] = xn

    xh = xn.astype(BF16)
    xl = (xn - xh.astype(F32)).astype(BF16)
    logits = _dot(xh, wrh_ref[...]) + _dot(xl, wrh_ref[...]) + _dot(xh, wrl_ref[...]) + br_ref[...]
    tm = logits.shape[0]
    lane = lax.broadcasted_iota(jnp.int32, logits.shape, 1).astype(F32)
    neg = -jnp.inf
    big = float(LANES)

    def first_max(v):
        m = jnp.max(v, axis=-1, keepdims=True)
        return m, jnp.min(jnp.where(v == m, lane, big), axis=-1, keepdims=True)

    coarse = lane < MOE_GROUPS
    mc, gsel = first_max(jnp.where(coarse, logits, neg))
    psel = 1.0 / jnp.sum(jnp.where(coarse, jnp.exp(logits - mc), 0.0), axis=-1, keepdims=True)
    lo = MOE_GROUPS + MOE_EPG * gsel
    lf = jnp.where((lane >= lo) & (lane < lo + MOE_EPG), logits, neg)
    m1, i1 = first_max(lf)
    m2, i2 = first_max(jnp.where(lane == i1, neg, lf))
    e2 = jnp.exp(m2 - m1)
    g1 = psel / (1.0 + e2)
    g2 = psel * e2 / (1.0 + e2)
    e_a, e_b = i1 - MOE_GROUPS, i2 - MOE_GROUPS

    pick_a, pick_b = lane == e_a, lane == e_b
    picks = jnp.where(pick_a | pick_b, 1.0, 0.0)
    earlier = lax.broadcasted_iota(jnp.int32, (tm, tm), 0) > lax.broadcasted_iota(jnp.int32, (tm, tm), 1)
    prior = _dot(earlier.astype(BF16), picks.astype(BF16)) + carry[...]
    rank_a = jnp.sum(jnp.where(pick_a, prior, 0.0), axis=-1, keepdims=True)
    rank_b = jnp.sum(jnp.where(pick_b, prior, 0.0), axis=-1, keepdims=True)
    carry[...] = prior[tm - 1:tm, :] + picks[tm - 1:tm, :]

    out = jnp.zeros_like(logits)
    for k, v in enumerate((e_a, e_b, g1, g2, rank_a, rank_b)):
        out = jnp.where(lane == float(k), v, out)
    rt_ref[0:rows, :] = out


def _mix_route(prompt, sample, consts, tm):
    n_prompt, n_sample = prompt[0].shape[0], sample[0].shape[0]
    assert n_prompt % tm == 0 and n_sample <= tm
    n_blocks = n_prompt // tm
    total_rows = n_prompt + n_sample
    row = lambda w: pl.BlockSpec((tm, w), lambda i: (jnp.minimum(i, n_blocks - 1), 0))
    out_row = lambda w: pl.BlockSpec((tm, w), lambda i: (i, 0))
    return pl.pallas_call(
        functools.partial(_mix_route_body, n_blocks),
        grid=(n_blocks + 1,),
        in_specs=([row(D_MODEL), row(SSD_WIDTH), row(S5_WIDTH)] + [_full_spec(a) for a in sample]
                  + [_full_spec(a) for a in consts]),
        out_specs=[out_row(D_MODEL), pl.BlockSpec(memory_space=pl.ANY),
                   out_row(LANES), pl.BlockSpec((1, LANES), lambda i: (0, 0))],
        out_shape=[jax.ShapeDtypeStruct((total_rows, D_MODEL), F32),
                   jax.ShapeDtypeStruct((total_rows, SLAB_ROWS, LANES), F32),
                   jax.ShapeDtypeStruct((total_rows, LANES), F32), jax.ShapeDtypeStruct((1, LANES), F32)],
        scratch_shapes=[pltpu.VMEM((1, LANES), F32), pltpu.VMEM((2, tm, D_MODEL), F32),
                        pltpu.SemaphoreType.DMA((2, SLAB_ROWS))],
        compiler_params=pltpu.CompilerParams(dimension_semantics=("arbitrary",), vmem_limit_bytes=VMEM_LIMIT),
        name="mix_route",
    )(*prompt, *sample, *consts)


def _sc_mesh():
    return plsc.VectorSubcoreMesh(core_axis_name="c", subcore_axis_name="s")


def _sc_worker():
    return lax.axis_index("s") * SC_CORES + lax.axis_index("c")


def _sc_dispatch(xn, pos_a, pos_b, n_rows):
    n_tok = xn.shape[0]
    ch = SC_DISPATCH_ROWS
    assert n_tok % ch == 0

    @functools.partial(
        pl.kernel, mesh=_sc_mesh(),
        out_type=jax.ShapeDtypeStruct((n_rows, SLAB_ROWS, LANES), F32),
        scratch_types=[pltpu.VMEM((ch,), jnp.int32), pltpu.VMEM((ch,), jnp.int32),
                       pltpu.VMEM((ch, SLAB_ROWS, LANES), F32), pltpu.SemaphoreType.DMA])
    def push(xn_hbm, pa_hbm, pb_hbm, xs_hbm, ia, ib, rows, sem):
        @pl.loop(_sc_worker(), n_tok // ch, step=SC_WORKERS)
        def _(c):
            off = pl.multiple_of(c * ch, ch)
            pltpu.sync_copy(pa_hbm.at[pl.ds(off, ch)], ia)
            pltpu.sync_copy(pb_hbm.at[pl.ds(off, ch)], ib)
            pltpu.sync_copy(xn_hbm.at[pl.ds(off, ch)], rows)
            pltpu.async_copy(rows, xs_hbm.at[ia], sem).wait()
            pltpu.async_copy(rows, xs_hbm.at[ib], sem).wait()

    return push(xn, pos_a, pos_b)


def _sc_collect(ysorted, pos_flat, ch):
    n_pick = pos_flat.shape[0]
    per_worker = n_pick // SC_WORKERS
    n_chunks = per_worker // ch
    assert n_pick % SC_WORKERS == 0 and per_worker % ch == 0

    @functools.partial(
        pl.kernel, mesh=_sc_mesh(),
        out_type=jax.ShapeDtypeStruct((n_pick, SLAB_ROWS, LANES), F32),
        scratch_types=[pltpu.VMEM((ch,), jnp.int32), pltpu.VMEM((ch,), jnp.int32),
                       pltpu.VMEM((ch, SLAB_ROWS, LANES), F32), pltpu.VMEM((ch, SLAB_ROWS, LANES), F32),
                       pltpu.SemaphoreType.DMA, pltpu.SemaphoreType.DMA])
    def pull(ys_hbm, pos_hbm, out_hbm, idx0, idx1, rows0, rows1, sem0, sem1):
        base = _sc_worker() * per_worker
        bufs = ((idx0, rows0, sem0), (idx1, rows1, sem1))

        def offset(j):
            return pl.multiple_of(base + j * ch, SUBLANES)

        def fetch(j, b):
            idx, rows, sem = bufs[b]
            pltpu.sync_copy(pos_hbm.at[pl.ds(offset(j), ch)], idx)
            pltpu.async_copy(ys_hbm.at[idx], rows, sem)

        def flush(j, b):
            idx, rows, sem = bufs[b]
            pltpu.make_async_copy(ys_hbm.at[idx], rows, sem).wait()
            pltpu.sync_copy(rows, out_hbm.at[pl.ds(offset(j), ch)])

        fetch(0, 0)

        @pl.loop(0, n_chunks // 2)
        def _(p):
            j = 2 * p
            fetch(j + 1, 1)
            flush(j, 0)

            @pl.when(j + 2 < n_chunks)
            def _():
                fetch(j + 2, 0)

            flush(j + 1, 1)

        if n_chunks % 2:
            flush(n_chunks - 1, 0)

    return pull(ysorted, pos_flat)


def _slab_columns(ref, rows, j):
    return ref[pl.ds(j, rows, stride=SLAB_ROWS), :]


def _moe_ffn_body(te_ref, nused_ref, x_ref, wg_ref, wu_ref, wd_ref, y_ref, wgb, wub, wdb):
    i = pl.program_id(0)

    @pl.when(i >= nused_ref[0])
    def _unused_tile():
        y_ref[...] = jnp.zeros_like(y_ref)

    @pl.when(i < nused_ref[0])
    def _tile():
        @pl.when((i == 0) | (te_ref[i] != te_ref[jnp.maximum(i - 1, 0)]))
        def _cast_weights():
            wgb[...] = wg_ref[0].astype(BF16)
            wub[...] = wu_ref[0].astype(BF16)
            wdb[...] = wd_ref[0].astype(BF16)

        x = jnp.concatenate([_slab_columns(x_ref, MOE_TILE, j) for j in range(SLAB_ROWS)], axis=-1).astype(BF16)
        gate = _dot(x, wgb[...])
        hmid = (gate * jax.nn.sigmoid(gate)) * _dot(x, wub[...])
        y = _dot(hmid.astype(BF16), wdb[...])
        for j in range(SLAB_ROWS):
            y_ref[pl.ds(j, MOE_TILE, stride=SLAB_ROWS), :] = y[:, j * LANES:(j + 1) * LANES]


def _moe_ffn(tile_expert, n_used, xsorted, w_gate, w_up, w_down):
    n_tiles = tile_expert.shape[0]
    wspec = lambda s: pl.BlockSpec((1,) + s, lambda i, te, nu: (te[i], 0, 0))
    tile = lambda imap: pl.BlockSpec((MOE_TILE * SLAB_ROWS, LANES), imap)
    return pl.pallas_call(
        _moe_ffn_body,
        grid_spec=pltpu.PrefetchScalarGridSpec(
            num_scalar_prefetch=2,
            grid=(n_tiles,),
            in_specs=[tile(lambda i, te, nu: (jnp.clip(i, 0, jnp.maximum(nu[0] - 1, 0)), 0)),
                      wspec((D_MODEL, MOE_D_FF)), wspec((D_MODEL, MOE_D_FF)), wspec((MOE_D_FF, D_MODEL))],
            out_specs=tile(lambda i, te, nu: (i, 0)),
            scratch_shapes=[pltpu.VMEM((D_MODEL, MOE_D_FF), BF16), pltpu.VMEM((D_MODEL, MOE_D_FF), BF16),
                            pltpu.VMEM((MOE_D_FF, D_MODEL), BF16)]),
        out_shape=jax.ShapeDtypeStruct(xsorted.shape, F32),
        compiler_params=pltpu.CompilerParams(dimension_semantics=("arbitrary",), vmem_limit_bytes=VMEM_LIMIT),
        name="moe_ffn",
    )(tile_expert, n_used, xsorted, w_gate, w_up, w_down)


def _combine_body(x1_ref, rt_ref, ya_ref, yb_ref, nf_ref, *rest):
    out_ref = rest[-1]
    rt = rt_ref[...]
    x1 = x1_ref[...]
    tm = x1.shape[0]
    ya, yb = ya_ref.at[0], yb_ref.at[0]
    x2 = jnp.concatenate(
        [x1[:, j * LANES:(j + 1) * LANES] + rt[:, 2:3] * _slab_columns(ya, tm, j) + rt[:, 3:4] * _slab_columns(yb, tm, j)
         for j in range(SLAB_ROWS)], axis=-1)
    out_ref[...] = _rms(x2, nf_ref[...])


def _combine(x1, rt, y_picks, nf, tm, rows, x_block, y_block, out_rows, out_block, out_buf=None):
    row = lambda w: pl.BlockSpec((tm, w), lambda i: (i + x_block, 0))
    pick = lambda k: pl.BlockSpec((1, tm * SLAB_ROWS, LANES), lambda i: (k, i + y_block, 0))
    in_specs = [row(D_MODEL), row(LANES), pick(0), pick(1), pl.BlockSpec((1, D_MODEL), lambda i: (0, 0))]
    args = [x1, rt, y_picks, y_picks, nf]
    aliases = {}
    if out_buf is not None:
        in_specs.append(pl.BlockSpec(memory_space=pl.ANY))
        aliases[len(args)] = 0
        args.append(out_buf)
    return pl.pallas_call(
        _combine_body,
        grid=(rows // tm,),
        in_specs=in_specs,
        out_specs=pl.BlockSpec((tm, D_MODEL), lambda i: (i + out_block, 0)),
        out_shape=jax.ShapeDtypeStruct((out_rows, D_MODEL), F32),
        input_output_aliases=aliases,
        compiler_params=pltpu.CompilerParams(dimension_semantics=("parallel",), vmem_limit_bytes=VMEM_LIMIT),
        name="moe_combine",
    )(*args)


def _route_tables(counts, eid, rank, n_tiles):
    tiles_per = (counts + MOE_TILE - 1) // MOE_TILE
    tile_end = jnp.cumsum(tiles_per)
    pstart = (tile_end - tiles_per) * MOE_TILE
    experts = jnp.arange(MOE_EXPERTS, dtype=jnp.int32)
    pos = [jnp.sum(jnp.where(e[:, None] == experts, pstart, 0), axis=-1) + r for e, r in zip(eid, rank)]
    n_used = tile_end[-1]
    tiles = jnp.arange(n_tiles, dtype=jnp.int32)
    tile_expert = jnp.sum((tile_end[None, :] <= jnp.minimum(tiles, n_used - 1)[:, None]).astype(jnp.int32), axis=1)
    return pos, tile_expert, n_used.reshape(1).astype(jnp.int32)


def _s5_tables(a_re, a_im, log_dt, b_re, b_im, c_re, c_im):
    dt = jnp.exp(log_dt)[:, None]
    mag = jnp.exp(a_re * dt)
    ab_re = mag * jnp.cos(a_im * dt)
    ab_im = mag * jnp.sin(a_im * dt)
    den = a_re * a_re + a_im * a_im
    nr = ab_re - 1.0
    q_re = (nr * a_re + ab_im * a_im) / den
    q_im = (ab_im * a_re - nr * a_im) / den
    bb_re = q_re[..., None] * b_re - q_im[..., None] * b_im
    bb_im = q_re[..., None] * b_im + q_im[..., None] * b_re
    nblk = S5_GROUPS // 16
    kw, nw = 16 * S5_GROUP_CH, 16 * S5_STATE
    same_group = (jnp.arange(kw)[:, None] // S5_GROUP_CH) == (jnp.arange(nw)[None, :] // S5_STATE)

    def in_map(bb):
        rows = bb.reshape(nblk, 16, S5_STATE, S5_GROUP_CH).transpose(0, 1, 3, 2).reshape(nblk, kw, S5_STATE)
        return jnp.where(same_group, jnp.tile(rows, (1, 1, 16)), 0.0)

    def out_map(cc):
        cols = cc.reshape(nblk, 16, S5_GROUP_CH, S5_STATE).transpose(0, 3, 1, 2).reshape(nblk, S5_STATE, kw)
        return jnp.where(same_group.T, jnp.tile(cols, (1, 16, 1)), 0.0)

    wb = jnp.concatenate([in_map(bb_re), in_map(bb_im)], axis=-1).astype(BF16)
    return (wb, ab_re.reshape(1, S5_LANES), ab_im.reshape(1, S5_LANES),
            out_map(c_re).astype(BF16), out_map(-c_im).astype(BF16))


def kernel(x_prompt, x_sample, state_ssd_conv, state_ssd_ssm, state_s5_re, state_s5_im, meta_tokens, norm_mix, w_in, conv_w, conv_b, dt_bias, a_log, d_ssd, ssd_norm, s5_a_re, s5_a_im, s5_log_dt, s5_b_re, s5_b_im, s5_c_re, s5_c_im, s5_d, w_glu, b_glu, s5_norm, w_out, norm_ffn, router_coarse_w, router_coarse_b, router_fine_w, router_fine_b, w_gate, w_up, w_down, norm_final):
    bp, seq, _ = x_prompt.shape
    bs = x_sample.shape[0]
    n_prompt = bp * seq
    n_tok = n_prompt + bs
    row2 = lambda v: v.reshape(1, -1)
    pad_heads = lambda v: jnp.pad(v, (0, LANES - SSD_HEADS)).reshape(1, LANES)

    w = w_in[0]
    o1, o2, o3 = SSD_WIDTH, SSD_WIDTH + SSD_CONV_DIM, SSD_WIDTH + SSD_CONV_DIM + SSD_HEADS
    wz, wx, wu = w[:, :o1].astype(BF16), w[:, o1:o2].astype(BF16), w[:, o3:].astype(BF16)
    wdt = jnp.pad(w[:, o2:o3], ((0, 0), (0, LANES - SSD_HEADS))).astype(BF16)
    g_mix = row2(norm_mix[0])
    cw, cb = conv_w[0], row2(conv_b[0])
    dtb, alog = pad_heads(dt_bias[0]), pad_heads(a_log[0])
    dexp = row2(jnp.repeat(d_ssd[0], SSD_HEAD_DIM))
    snrm = row2(ssd_norm[0])
    eexp = (jnp.arange(LANES)[:, None] == (jnp.arange(SSD_WIDTH) // SSD_HEAD_DIM)[None, :]).astype(BF16)
    wb5, ab_re, ab_im, wcr, wci = _s5_tables(s5_a_re[0], s5_a_im[0], s5_log_dt[0], s5_b_re[0], s5_b_im[0],
                                             s5_c_re[0], s5_c_im[0])
    d5, wglu, bglu, nrm5 = row2(s5_d[0]), w_glu[0].astype(BF16), row2(b_glu[0]), row2(s5_norm[0])
    wo_a, wo_b = w_out[0][:SSD_WIDTH].astype(BF16), w_out[0][SSD_WIDTH:].astype(BF16)
    w_r = jnp.concatenate([router_coarse_w[0], router_fine_w[0].transpose(1, 0, 2).reshape(D_MODEL, MOE_EXPERTS)], axis=1)
    w_r = jnp.pad(w_r, ((0, 0), (0, LANES - w_r.shape[1])))
    wrh = w_r.astype(BF16)
    wrl = (w_r - wrh.astype(F32)).astype(BF16)
    b_r = jnp.concatenate([router_coarse_b[0], router_fine_b[0].reshape(-1)])
    b_r = jnp.pad(b_r, (0, LANES - b_r.shape[0])).reshape(1, LANES)

    zp, xbcp, dtp, up = _in_proj(x_prompt.reshape(n_prompt, D_MODEL), g_mix, wz, wx, wdt, wu, TOK_TILE, BF16, F32)
    xsm = jnp.concatenate([x_sample.reshape(bs, D_MODEL), meta_tokens], axis=0)
    zs, xbcs, dts, us = _in_proj(xsm, g_mix, wz, wx, wdt, wu, xsm.shape[0], F32, F32)

    front = SSD_CHUNK - N_META
    padf = lambda a: jnp.pad(a[bs:], ((front, 0), (0, 0)))[None]
    gw = SSD_HPG * SSD_HEAD_DIM
    ssd_consts = (cw, cb, dtb, alog, dexp, snrm, eexp)
    _, ctail_m, _, ht_m = _ssd_chunked(
        padf(xbcs).astype(BF16), padf(dts), jnp.zeros((1, SSD_CHUNK, SSD_WIDTH), F32),
        jnp.zeros((1, SUBLANES, SSD_CONV_DIM), F32), jnp.zeros((1, SSD_GROUPS, SSD_STATE, gw), F32),
        *ssd_consts, mask_rows=front)
    y_ssd_p, ctail_p, ssm_p, _ = _ssd_chunked(
        xbcp.reshape(bp, seq, SSD_CONV_DIM), dtp.reshape(bp, seq, LANES), zp.reshape(bp, seq, SSD_WIDTH),
        ctail_m, ht_m, *ssd_consts, mask_rows=0)

    abr8, abi8 = jnp.broadcast_to(ab_re, (bp, S5_LANES)), jnp.broadcast_to(ab_im, (bp, S5_LANES))
    um8 = jnp.repeat(us[bs:], bp, axis=0).astype(BF16)
    y_s5_p, s5re_p, s5im_p = _s5_seq(up.reshape(bp, seq, S5_WIDTH), um8, wb5, abr8, abi8,
                                     wcr, wci, d5, wglu, bglu, nrm5)

    cst = state_ssd_conv[0]
    xt_s, dt_s, dec_s, bc, xs_s = _ssd_step_prep(xbcs[:bs], cst[:, 0], cst[:, 1], cst[:, 2], dts[:bs],
                                                 cw, cb, dtb, alog)
    ssm_s, y_core = _ssd_step(dt_s[:, :SSD_HEADS].reshape(-1), dec_s[:, :SSD_HEADS].reshape(-1),
                              state_ssd_ssm[0], xt_s, bc)
    y_ssd_s, y_s5_s, s5re_s, s5im_s = _sample_post(
        y_core, xs_s, zs[:bs], dexp, snrm, us[:bs], state_s5_re[0].reshape(bs, S5_LANES),
        state_s5_im[0].reshape(bs, S5_LANES), wb5, ab_re, ab_im, wcr, wci, d5, wglu, bglu, nrm5)

    route_consts = (wo_a, wo_b, row2(norm_ffn[0]), wrh, wrl, b_r)
    x1, xn, rt, counts = _mix_route(
        (x_prompt.reshape(n_prompt, D_MODEL), y_ssd_p.reshape(n_prompt, SSD_WIDTH), y_s5_p.reshape(n_prompt, S5_WIDTH)),
        (x_sample.reshape(bs, D_MODEL), y_ssd_s, y_s5_s), route_consts, TOK_TILE)

    n_tiles = -(-2 * n_tok // MOE_TILE) + MOE_EXPERTS
    lane_i32 = lambda k: rt[:, k].astype(jnp.int32)
    eid = [jnp.clip(lane_i32(k), 0, MOE_EXPERTS - 1) for k in (0, 1)]
    (pos_a, pos_b), tile_expert, n_used = _route_tables(counts[0, :MOE_EXPERTS].astype(jnp.int32), eid,
                                                        [lane_i32(4), lane_i32(5)], n_tiles)
    slabs = lambda a: a.reshape(-1, SLAB_ROWS, LANES)
    xsorted = _sc_dispatch(xn, pos_a, pos_b, n_tiles * MOE_TILE)
    ysorted = _moe_ffn(tile_expert, n_used, xsorted.reshape(-1, LANES), w_gate[0], w_up[0], w_down[0])
    nfin = row2(norm_final)

    half = n_prompt // 2

    def collect(lo, hi, ch):
        picks = jnp.concatenate([pos_a[lo:hi], pos_b[lo:hi]])
        return _sc_collect(slabs(ysorted), picks, ch).reshape(2, (hi - lo) * SLAB_ROWS, LANES)

    picks_1 = collect(0, half, SC_COLLECT_ROWS[0])
    picks_2 = collect(half, n_tok, SC_COLLECT_ROWS[1])
    blocks = half // MOE_TILE
    y_p = _combine(x1, rt, picks_1, nfin, MOE_TILE, half, 0, 0, n_prompt, 0)
    y_p = _combine(x1, rt, picks_2, nfin, MOE_TILE, half, blocks, 0, n_prompt, blocks, out_buf=y_p)
    y_s = _combine(x1, rt, picks_2, nfin, bs, bs, n_prompt // bs, half // bs, bs, 0)

    s5_state = lambda a, b: a.reshape(1, b, S5_GROUPS, S5_STATE)
    new_conv_s = jnp.stack([cst[:, 1], cst[:, 2], xbcs[:bs]], axis=1)[None]
    return (y_p.reshape(bp, seq, D_MODEL), y_s.reshape(bs, 1, D_MODEL),
            ctail_p[:, SUBLANES - (SSD_CONV - 1):][None], ssm_p[None], s5_state(s5re_p, bp), s5_state(s5im_p, bp),
            new_conv_s, ssm_s[None], s5_state(s5re_s, bs), s5_state(s5im_s, bs))
```

```python
import functools

import jax
import jax.numpy as jnp
from jax import lax
from jax.experimental import pallas as pl
from jax.experimental.pallas import tpu as pltpu
from jax.experimental.pallas import tpu_sc as plsc

F32, BF16 = jnp.float32, jnp.bfloat16

D_MODEL = 1024
N_META = 16
SSD_WIDTH = 1024
SSD_HEAD_DIM = 64
SSD_HEADS = 16
SSD_GROUPS = 2
SSD_HPG = SSD_HEADS // SSD_GROUPS
SSD_STATE = 128
SSD_CONV = 4
SSD_CHUNK = 128
SSD_CONV_DIM = SSD_WIDTH + 2 * SSD_GROUPS * SSD_STATE
S5_WIDTH = 1024
S5_GROUP_CH = 16
S5_GROUPS = 64
S5_STATE = 64
S5_LANES = S5_GROUPS * S5_STATE
MOE_GROUPS = 4
MOE_EPG = 8
MOE_EXPERTS = MOE_GROUPS * MOE_EPG
MOE_D_FF = 512
EPS = 1e-6

LANES = 128
SUBLANES = 8
VMEM_LIMIT = 56 * 1024 * 1024

S5_TIME_TILE = 64
S5_SCAN_LANES = 512
MOE_TILE = 256
SLAB_ROWS = D_MODEL // LANES
SC_CORES = 2
SC_SUBCORES = 16
SC_WORKERS = SC_CORES * SC_SUBCORES
SC_DISPATCH_ROWS = 32
SC_COLLECT_ROWS = (32, 40)
TOK_TILE = 512


def _dot(a, b):
    return jnp.dot(a, b, preferred_element_type=F32)


def _rms(x, g):
    return x * lax.rsqrt(jnp.mean(x * x, axis=-1, keepdims=True) + EPS) * g


def _softplus(x):
    return jnp.maximum(x, 0.0) + jnp.log1p(jnp.exp(-jnp.abs(x)))


def _split3(x):
    hi = x.astype(BF16)
    r = x - hi.astype(F32)
    mid = r.astype(BF16)
    lo = (r - mid.astype(F32)).astype(BF16)
    return hi, mid, lo


def _dot3(x, w):
    hi, mid, lo = _split3(x)
    return _dot(hi, w) + _dot(mid, w) + _dot(lo, w)


def _dot3_left(w, x):
    hi, mid, lo = _split3(x)
    return _dot(w, hi) + _dot(w, mid) + _dot(w, lo)


def _full_spec(a):
    nd = a.ndim
    return pl.BlockSpec(a.shape, lambda *_: (0,) * nd)


def _resident_spec(a):
    nd = a.ndim
    return pl.BlockSpec(a.shape, lambda *_: (0,) * nd, pipeline_mode=pl.Buffered(1))


def _in_proj_body(x_ref, g_ref, wz_ref, wx_ref, wdt_ref, wu_ref, z_ref, xbc_ref, dt_ref, u_ref):
    xb = _rms(x_ref[...], g_ref[...]).astype(BF16)
    z_ref[...] = _dot(xb, wz_ref[...]).astype(z_ref.dtype)
    xbc_ref[...] = _dot(xb, wx_ref[...]).astype(xbc_ref.dtype)
    dt_ref[...] = _dot(xb, wdt_ref[...])
    u_ref[...] = _dot(xb, wu_ref[...]).astype(u_ref.dtype)


def _in_proj(x2d, g, wz, wx, wdt, wu, tm, act_dtype, u_dtype):
    rows = x2d.shape[0]
    row = lambda w: pl.BlockSpec((tm, w), lambda i: (i, 0))
    return pl.pallas_call(
        _in_proj_body,
        grid=(rows // tm,),
        in_specs=[row(D_MODEL), _full_spec(g), _full_spec(wz), _full_spec(wx), _full_spec(wdt), _full_spec(wu)],
        out_specs=[row(SSD_WIDTH), row(SSD_CONV_DIM), row(LANES), row(S5_WIDTH)],
        out_shape=[jax.ShapeDtypeStruct((rows, SSD_WIDTH), act_dtype),
                   jax.ShapeDtypeStruct((rows, SSD_CONV_DIM), act_dtype),
                   jax.ShapeDtypeStruct((rows, LANES), F32),
                   jax.ShapeDtypeStruct((rows, S5_WIDTH), u_dtype)],
        compiler_params=pltpu.CompilerParams(dimension_semantics=("parallel",), vmem_limit_bytes=VMEM_LIMIT),
        name="in_proj",
    )(x2d, g, wz, wx, wdt, wu)


def _ssd_body(mask_rows, xbc_ref, dt_ref, z_ref, cinit_ref, hinit_ref, cw_ref, cb_ref, dtb_ref, alog_ref,
              dexp_ref, nrm_ref, eexp_ref, y_ref, ctail_ref, st_ref, hto_ref, xwin, hT):
    c = pl.program_id(1)
    L = SSD_CHUNK

    @pl.when(c == 0)
    def _init():
        xwin[...] = cinit_ref[0]
        hT[...] = hinit_ref[0]

    x_b = xbc_ref[0]
    x_f = x_b.astype(F32)
    taps = SSD_CONV - 1
    m_i = lax.broadcasted_iota(jnp.int32, (taps * L, L), 0)
    r_i = lax.broadcasted_iota(jnp.int32, (taps * L, L), 1)
    shift = (r_i + (taps - m_i // L) == m_i % L).astype(BF16)
    shifted = _dot(shift, x_b)
    acc = cb_ref[...] + x_f * cw_ref[taps:taps + 1, :]
    for k in range(taps):
        acc = acc + shifted[k * L:(k + 1) * L, :] * cw_ref[k:k + 1, :]
    joint = jnp.concatenate([xwin[...], x_f[0:SUBLANES, :]], axis=0)
    row8 = lax.broadcasted_iota(jnp.int32, (SUBLANES, 1), 0)
    head = acc[0:SUBLANES, :]
    for k in range(taps):
        d = taps - k
        head = head + jnp.where(row8 < d, joint[SUBLANES - d:2 * SUBLANES - d, :], 0.0) * cw_ref[k:k + 1, :]
    acc = jnp.concatenate([head, acc[SUBLANES:, :]], axis=0)
    tail = x_f[L - SUBLANES:, :]
    xwin[...] = tail
    ctail_ref[0] = tail

    xact = acc * jax.nn.sigmoid(acc)
    dt = _softplus(dt_ref[0] + dtb_ref[...])
    if mask_rows:
        valid = lax.broadcasted_iota(jnp.int32, (L, 1), 0) >= mask_rows
        xact = jnp.where(valid, xact, 0.0)
        dt = jnp.where(valid, dt, 0.0)

    a_neg = -jnp.exp(alog_ref[...])
    dA = dt * a_neg
    row_i = lax.broadcasted_iota(jnp.int32, (L, L), 0)
    col_i = lax.broadcasted_iota(jnp.int32, (L, L), 1)
    causal = row_i >= col_i
    tril = causal.astype(BF16)
    cs = _dot3_left(tril, dA)
    csT = cs.T
    dtT = dt.T
    ecs = jnp.exp(cs)
    wdec = jnp.exp(cs[L - 1:L, :] - cs) * dt
    eexp = eexp_ref[...]
    ecs_e = _dot3(ecs, eexp)
    wdec_e = _dot3(wdec, eexp)
    lane = lax.broadcasted_iota(jnp.int32, (L, LANES), 1)
    first_half = lane < SSD_HEAD_DIM

    gw = SSD_HPG * SSD_HEAD_DIM
    y_groups = []
    for g in range(SSD_GROUPS):
        b_g = xact[:, SSD_WIDTH + g * SSD_STATE: SSD_WIDTH + (g + 1) * SSD_STATE]
        c_g = xact[:, SSD_WIDTH + (SSD_GROUPS + g) * SSD_STATE: SSD_WIDTH + (SSD_GROUPS + g + 1) * SSD_STATE]
        b_b = b_g.astype(BF16)
        c_b = c_g.astype(BF16)
        cb = lax.dot_general(c_b, b_b, (((1,), (1,)), ((), ())), preferred_element_type=F32)
        xs_g = xact[:, g * gw:(g + 1) * gw]
        h_prev = hT[g]
        y_off = _dot(c_b, h_prev.astype(BF16)) * ecs_e[:, g * gw:(g + 1) * gw]
        xdec = (xs_g * wdec_e[:, g * gw:(g + 1) * gw]).astype(BF16)
        hT[g] = h_prev * ecs_e[L - 1:L, g * gw:(g + 1) * gw] + _dot(b_g.T.astype(BF16), xdec)
        pieces = []
        for j in range(SSD_HPG // 2):
            xs_pair = xs_g[:, j * LANES:(j + 1) * LANES]
            halves = (jnp.where(first_half, xs_pair, 0.0).astype(BF16),
                      jnp.where(first_half, 0.0, xs_pair).astype(BF16))
            yd = None
            for t in range(2):
                h = g * SSD_HPG + 2 * j + t
                seg = cs[:, h:h + 1] - csT[h:h + 1, :]
                lmat = jnp.exp(jnp.where(causal, seg, -jnp.inf))
                m = (cb * lmat * dtT[h:h + 1, :]).astype(BF16)
                part = _dot(m, halves[t])
                yd = part if yd is None else yd + part
            pieces.append(yd)
        y_groups.append(jnp.concatenate(pieces, axis=-1) + y_off + dexp_ref[:, g * gw:(g + 1) * gw] * xs_g)
    y = jnp.concatenate(y_groups, axis=-1)
    z = z_ref[0].astype(F32)
    y_ref[0] = _rms(y * (z * jax.nn.sigmoid(z)), nrm_ref[...]).astype(y_ref.dtype)

    @pl.when(c == pl.num_programs(1) - 1)
    def _emit():
        hto_ref[0] = hT[...]
        for g in range(SSD_GROUPS):
            t = hT[g].T
            for k in range(SSD_HPG):
                st_ref[0, g * SSD_HPG + k] = t[k * SSD_HEAD_DIM:(k + 1) * SSD_HEAD_DIM, :]


def _ssd_chunked(xbc, dt, z, cinit, hinit, cw, cb, dtb, alog, dexp, nrm, eexp, mask_rows):
    bsz, seq, _ = xbc.shape
    nc = seq // SSD_CHUNK
    gw = SSD_HPG * SSD_HEAD_DIM
    blk = lambda w: pl.BlockSpec((1, SSD_CHUNK, w), lambda b, c: (b, c, 0))
    return pl.pallas_call(
        functools.partial(_ssd_body, mask_rows),
        grid=(bsz, nc),
        in_specs=[blk(SSD_CONV_DIM), blk(LANES), blk(SSD_WIDTH),
                  pl.BlockSpec((1, SUBLANES, SSD_CONV_DIM), lambda b, c: (0, 0, 0)),
                  pl.BlockSpec((1, SSD_GROUPS, SSD_STATE, gw), lambda b, c: (0, 0, 0, 0)),
                  _full_spec(cw), _full_spec(cb), _full_spec(dtb), _full_spec(alog),
                  _full_spec(dexp), _full_spec(nrm), _full_spec(eexp)],
        out_specs=[blk(SSD_WIDTH),
                   pl.BlockSpec((1, SUBLANES, SSD_CONV_DIM), lambda b, c: (b, 0, 0)),
                   pl.BlockSpec((1, SSD_HEADS, SSD_HEAD_DIM, SSD_STATE), lambda b, c: (b, 0, 0, 0)),
                   pl.BlockSpec((1, SSD_GROUPS, SSD_STATE, gw), lambda b, c: (b, 0, 0, 0))],
        out_shape=[jax.ShapeDtypeStruct((bsz, seq, SSD_WIDTH), BF16),
                   jax.ShapeDtypeStruct((bsz, SUBLANES, SSD_CONV_DIM), F32),
                   jax.ShapeDtypeStruct((bsz, SSD_HEADS, SSD_HEAD_DIM, SSD_STATE), F32),
                   jax.ShapeDtypeStruct((bsz, SSD_GROUPS, SSD_STATE, gw), F32)],
        scratch_shapes=[pltpu.VMEM((SUBLANES, SSD_CONV_DIM), F32),
                        pltpu.VMEM((SSD_GROUPS, SSD_STATE, gw), F32)],
        compiler_params=pltpu.CompilerParams(dimension_semantics=("parallel", "arbitrary"),
                                             vmem_limit_bytes=VMEM_LIMIT),
        name="ssd_chunked",
    )(xbc, dt, z, cinit, hinit, cw, cb, dtb, alog, dexp, nrm, eexp)


def _ssd_step_prep_body(xbc_ref, c0_ref, c1_ref, c2_ref, dt_ref, cw_ref, cb_ref, dtb_ref, alog_ref,
                        xt_ref, dt_out_ref, dec_ref, bc_ref, xs_ref):
    acc = cb_ref[...]
    for k, r in enumerate((c0_ref, c1_ref, c2_ref, xbc_ref)):
        acc = acc + r[...] * cw_ref[k:k + 1, :]
    xact = acc * jax.nn.sigmoid(acc)
    xs = xact[:, :SSD_WIDTH]
    dt = _softplus(dt_ref[...] + dtb_ref[...])
    dt_out_ref[...] = dt
    dec_ref[...] = jnp.exp(dt * -jnp.exp(alog_ref[...]))
    bc_ref[...] = xact[:, SSD_WIDTH:]
    xs_ref[...] = xs
    xt_ref[...] = xs.T.astype(xt_ref.dtype)


def _ssd_step_prep(xbc, c0, c1, c2, dt, cw, cb, dtb, alog):
    n = xbc.shape[0]
    args = (xbc, c0, c1, c2, dt, cw, cb, dtb, alog)
    spec = lambda r, w: pl.BlockSpec((r, w), lambda: (0, 0))
    return pl.pallas_call(
        _ssd_step_prep_body,
        in_specs=[_full_spec(a) for a in args],
        out_specs=[spec(SSD_WIDTH, n), spec(n, LANES), spec(n, LANES), spec(n, 2 * SSD_GROUPS * SSD_STATE),
                   spec(n, SSD_WIDTH)],
        out_shape=[jax.ShapeDtypeStruct((SSD_WIDTH, n), BF16), jax.ShapeDtypeStruct((n, LANES), F32),
                   jax.ShapeDtypeStruct((n, LANES), F32),
                   jax.ShapeDtypeStruct((n, 2 * SSD_GROUPS * SSD_STATE), F32),
                   jax.ShapeDtypeStruct((n, SSD_WIDTH), F32)],
        compiler_params=pltpu.CompilerParams(vmem_limit_bytes=VMEM_LIMIT),
        name="ssd_step_prep",
    )(*args)


def _ssd_step_body(dt_ref, dec_ref, st_ref, xt_ref, bc_ref, so_ref, y_ref):
    n = xt_ref.shape[1]
    gw = SSD_HPG * SSD_HEAD_DIM
    blk = pl.program_id(0)
    seq_id = lax.broadcasted_iota(jnp.int32, (n, SSD_STATE), 0)
    sub_id = lax.broadcasted_iota(jnp.int32, (SUBLANES, gw), 0)
    base = pl.multiple_of(blk * SUBLANES, SUBLANES)
    y_acc = [jnp.zeros((SUBLANES, gw), F32) for _ in range(SSD_GROUPS)]
    for i in range(SUBLANES):
        s = blk * SUBLANES + i
        for g in range(SSD_GROUPS):
            b_all = bc_ref[:, g * SSD_STATE:(g + 1) * SSD_STATE]
            rhs = jnp.where(seq_id == s, b_all, 0.0).astype(BF16)
            outer = _dot(xt_ref[g * gw:(g + 1) * gw, :], rhs)
            news = []
            for k in range(SSD_HPG):
                h = g * SSD_HPG + k
                new = (dec_ref[s * SSD_HEADS + h] * st_ref[i, h]
                       + dt_ref[s * SSD_HEADS + h] * outer[k * SSD_HEAD_DIM:(k + 1) * SSD_HEAD_DIM, :])
                so_ref[i, h] = new
                news.append(new)
            new_g = jnp.concatenate(news, axis=0).astype(BF16)
            c_lo = (SSD_GROUPS + g) * SSD_STATE
            c_blk = bc_ref[pl.ds(base, SUBLANES), c_lo:c_lo + SSD_STATE].astype(BF16)
            r = lax.dot_general(c_blk, new_g, (((1,), (1,)), ((), ())), preferred_element_type=F32)
            y_acc[g] = y_acc[g] + jnp.where(sub_id == i, r, 0.0)
    y_ref[...] = jnp.concatenate(y_acc, axis=-1)


def _ssd_step(dt_flat, dec_flat, state, xt, bc):
    n = state.shape[0]
    st_spec = pl.BlockSpec((SUBLANES, SSD_HEADS, SSD_HEAD_DIM, SSD_STATE), lambda i, *_: (i, 0, 0, 0))
    return pl.pallas_call(
        _ssd_step_body,
        grid_spec=pltpu.PrefetchScalarGridSpec(
            num_scalar_prefetch=2,
            grid=(n // SUBLANES,),
            in_specs=[st_spec, pl.BlockSpec(xt.shape, lambda i, *_: (0, 0)),
                      pl.BlockSpec(bc.shape, lambda i, *_: (0, 0))],
            out_specs=[st_spec, pl.BlockSpec((SUBLANES, SSD_WIDTH), lambda i, *_: (i, 0))]),
        out_shape=[jax.ShapeDtypeStruct(state.shape, F32), jax.ShapeDtypeStruct((n, SSD_WIDTH), F32)],
        compiler_params=pltpu.CompilerParams(dimension_semantics=("parallel",), vmem_limit_bytes=VMEM_LIMIT),
        name="ssd_step",
    )(dt_flat, dec_flat, state, xt, bc)


def _s5_project_in(u_b16, wb_ref, store):
    kw = 16 * S5_GROUP_CH
    nw = 16 * S5_STATE
    for j in range(S5_WIDTH // kw):
        r = _dot(u_b16[:, j * kw:(j + 1) * kw], wb_ref[j])
        store(j, r[:, :nw], r[:, nw:])


def _s5_tail(hre_of, him_of, u_f32, wcr_ref, wci_ref, d_ref, wglu_ref, bglu_ref, nrm_ref):
    cols = []
    for j in range(wcr_ref.shape[0]):
        cols.append(_dot(hre_of(j).astype(BF16), wcr_ref[j]) + _dot(him_of(j).astype(BF16), wci_ref[j]))
    return _s5_finish(cols, u_f32, d_ref, wglu_ref, bglu_ref, nrm_ref)


def _s5_finish(cols, u_f32, d_ref, wglu_ref, bglu_ref, nrm_ref):
    y = jnp.concatenate(cols, axis=-1) + d_ref[...] * u_f32
    y = jax.nn.gelu(y)
    y = y * jax.nn.sigmoid(_dot(y.astype(BF16), wglu_ref[...]) + bglu_ref[...])
    return _rms(y, nrm_ref[...])


def _s5_seq_body(u_hbm, um_ref, wb_ref, abr_ref, abi_ref, wcr_ref, wci_ref, d_ref, wglu_ref, bglu_ref, nrm_ref,
                 y_hbm, sre_ref, sim_ref, ubuf, ybuf, bu, h, in_sems, out_sems):
    j = pl.program_id(0)
    last = pl.num_programs(0) - 1
    lc, bsz = ubuf.shape[1], ubuf.shape[2]
    rows = lc * bsz
    nw = 16 * S5_STATE

    def in_copy(step, b):
        return pltpu.make_async_copy(u_hbm.at[b, pl.ds(step * lc, lc), :], ubuf.at[step % 2, :, b, :],
                                     in_sems.at[step % 2, b])

    def out_copy(step, b):
        return pltpu.make_async_copy(ybuf.at[step % 2, :, b, :], y_hbm.at[b, pl.ds(step * lc, lc), :],
                                     out_sems.at[step % 2, b])

    def project_in(u_b16, nrows):
        def store(jj, re, im):
            bu[0:nrows, jj * nw:(jj + 1) * nw] = re
            bu[0:nrows, S5_LANES + jj * nw:S5_LANES + (jj + 1) * nw] = im
        _s5_project_in(u_b16, wb_ref, store)

    def scan(nsteps):
        for k in range(S5_LANES // S5_SCAN_LANES):
            sl_r = pl.ds(k * S5_SCAN_LANES, S5_SCAN_LANES)
            sl_i = pl.ds(S5_LANES + k * S5_SCAN_LANES, S5_SCAN_LANES)
            ar = abr_ref[:, sl_r]
            ai = abi_ref[:, sl_r]

            def step(l, carry):
                hr, hi = carry
                slab = pl.ds(pl.multiple_of(l * bsz, bsz), bsz)
                nr = ar * hr - ai * hi + bu[slab, sl_r]
                ni = ar * hi + ai * hr + bu[slab, sl_i]
                bu[slab, sl_r] = nr
                bu[slab, sl_i] = ni
                return nr, ni

            hr, hi = lax.fori_loop(0, nsteps, step, (h[:, sl_r], h[:, sl_i]))
            h[:, sl_r] = hr
            h[:, sl_i] = hi

    @pl.when(j == 0)
    def _first():
        for b in range(bsz):
            in_copy(0, b).start()
        h[...] = jnp.zeros_like(h)
        project_in(um_ref[...], N_META * bsz)
        scan(N_META)

    @pl.when(j < last)
    def _prefetch():
        for b in range(bsz):
            in_copy(j + 1, b).start()

    for b in range(bsz):
        in_copy(j, b).wait()
    u2 = ubuf[j % 2].reshape(rows, S5_WIDTH)
    u_b16 = u2.astype(BF16)
    kw = 16 * S5_GROUP_CH

    def project_block(jj):
        r = _dot(u_b16[:, jj * kw:(jj + 1) * kw], wb_ref[jj])
        bu[0:rows, jj * nw:(jj + 1) * nw] = r[:, :nw]
        bu[0:rows, S5_LANES + jj * nw:S5_LANES + (jj + 1) * nw] = r[:, nw:]

    def scan_block(jj):
        for k in range(nw // S5_SCAN_LANES):
            lo = jj * nw + k * S5_SCAN_LANES
            sl_r = slice(lo, lo + S5_SCAN_LANES)
            sl_i = slice(S5_LANES + lo, S5_LANES + lo + S5_SCAN_LANES)
            ar, ai = abr_ref[:, sl_r], abi_ref[:, sl_r]
            hr, hi = h[:, sl_r], h[:, sl_i]
            for l in range(lc):
                slab = slice(l * bsz, (l + 1) * bsz)
                hr, hi = (ar * hr - ai * hi + bu[slab, sl_r], ar * hi + ai * hr + bu[slab, sl_i])
                bu[slab, sl_r] = hr
                bu[slab, sl_i] = hi
            h[:, sl_r] = hr
            h[:, sl_i] = hi

    def readout_block(jj):
        return (_dot(bu[:, jj * nw:(jj + 1) * nw].astype(BF16), wcr_ref[jj])
                + _dot(bu[:, S5_LANES + jj * nw:S5_LANES + (jj + 1) * nw].astype(BF16), wci_ref[jj]))

    n_blocks = S5_WIDTH // kw
    project_block(0)
    cols = []
    for jj in range(n_blocks):
        if jj + 1 < n_blocks:
            project_block(jj + 1)
        scan_block(jj)
        cols.append(readout_block(jj))
    y = _s5_finish(cols, u2, d_ref, wglu_ref, bglu_ref, nrm_ref)
    ybuf[j % 2] = y.reshape(lc, bsz, S5_WIDTH)
    for b in range(bsz):
        out_copy(j, b).start()

    @pl.when(j > 0)
    def _wait_previous_out():
        for b in range(bsz):
            out_copy(j - 1, b).wait()

    @pl.when(j == last)
    def _emit():
        for b in range(bsz):
            out_copy(j, b).wait()
        sre_ref[...] = h[:, 0:S5_LANES]
        sim_ref[...] = h[:, S5_LANES:]


def _s5_seq(u, um, wb, abr, abi, wcr, wci, d, wglu, bglu, nrm):
    bsz, seq, _ = u.shape
    lc = S5_TIME_TILE
    consts = (um, wb, abr, abi, wcr, wci, d, wglu, bglu, nrm)
    st = pl.BlockSpec((bsz, S5_LANES), lambda j: (0, 0))
    return pl.pallas_call(
        _s5_seq_body,
        grid=(seq // lc,),
        in_specs=[pl.BlockSpec(memory_space=pl.ANY)] + [_resident_spec(a) for a in consts],
        out_specs=[pl.BlockSpec(memory_space=pl.ANY), st, st],
        out_shape=[jax.ShapeDtypeStruct((bsz, seq, S5_WIDTH), F32),
                   jax.ShapeDtypeStruct((bsz, S5_LANES), F32), jax.ShapeDtypeStruct((bsz, S5_LANES), F32)],
        scratch_shapes=[pltpu.VMEM((2, lc, bsz, S5_WIDTH), F32), pltpu.VMEM((2, lc, bsz, S5_WIDTH), F32),
                        pltpu.VMEM((lc * bsz, 2 * S5_LANES), F32), pltpu.VMEM((bsz, 2 * S5_LANES), F32),
                        pltpu.SemaphoreType.DMA((2, bsz)), pltpu.SemaphoreType.DMA((2, bsz))],
        compiler_params=pltpu.CompilerParams(dimension_semantics=("arbitrary",), vmem_limit_bytes=VMEM_LIMIT),
        name="s5_seq",
    )(u, *consts)


def _sample_post_body(yc_ref, xs_ref, z_ref, dexp_ref, snrm_ref, u_ref, hr_ref, hi_ref, wb_ref, abr_ref, abi_ref,
                      wcr_ref, wci_ref, d_ref, wglu_ref, bglu_ref, nrm_ref,
                      yssd_ref, ys5_ref, nre_ref, nim_ref):
    z = z_ref[...]
    y = yc_ref[...] + dexp_ref[...] * xs_ref[...]
    yssd_ref[...] = _rms(y * (z * jax.nn.sigmoid(z)), snrm_ref[...]).astype(yssd_ref.dtype)

    u = u_ref[...]
    nw = 16 * S5_STATE
    ar, ai = abr_ref[...], abi_ref[...]

    def store(jj, re, im):
        sl = slice(jj * nw, (jj + 1) * nw)
        h0r, h0i = hr_ref[:, sl], hi_ref[:, sl]
        nre_ref[:, sl] = ar[:, sl] * h0r - ai[:, sl] * h0i + re
        nim_ref[:, sl] = ar[:, sl] * h0i + ai[:, sl] * h0r + im

    _s5_project_in(u.astype(BF16), wb_ref, store)
    slab = lambda ref: (lambda jj: ref[:, jj * nw:(jj + 1) * nw])
    y5 = _s5_tail(slab(nre_ref), slab(nim_ref), u, wcr_ref, wci_ref, d_ref, wglu_ref, bglu_ref, nrm_ref)
    ys5_ref[...] = y5.astype(ys5_ref.dtype)


def _sample_post(yc, xs, z, dexp, snrm, u, h0r, h0i, wb, abr1, abi1, wcr, wci, d, wglu, bglu, nrm):
    n = yc.shape[0]
    args = (yc, xs, z, dexp, snrm, u, h0r, h0i, wb, abr1, abi1, wcr, wci, d, wglu, bglu, nrm)
    spec = lambda w: pl.BlockSpec((n, w), lambda: (0, 0))
    return pl.pallas_call(
        _sample_post_body,
        in_specs=[_full_spec(a) for a in args],
        out_specs=[spec(SSD_WIDTH), spec(S5_WIDTH), spec(S5_LANES), spec(S5_LANES)],
        out_shape=[jax.ShapeDtypeStruct((n, SSD_WIDTH), BF16), jax.ShapeDtypeStruct((n, S5_WIDTH), BF16),
                   jax.ShapeDtypeStruct((n, S5_LANES), F32), jax.ShapeDtypeStruct((n, S5_LANES), F32)],
        compiler_params=pltpu.CompilerParams(vmem_limit_bytes=VMEM_LIMIT),
        name="sample_post",
    )(*args)


def _mix_route_body(n_blocks, xp_ref, ysp_ref, y5p_ref, xs_ref, yss_ref, y5s_ref, *refs):
    xn_hbm, _, cnt_ref, carry, xbuf, sems = refs[-6:]
    i = pl.program_id(0)
    tm, n_sample = xp_ref.shape[0], xs_ref.shape[0]

    def xn_copy(step, rows, j):
        return pltpu.make_async_copy(xbuf.at[step % 2, pl.ds(0, rows), pl.ds(j * LANES, LANES)],
                                     xn_hbm.at[pl.ds(step * tm, rows), j, :], sems.at[step % 2, j])

    @pl.when(i == 0)
    def _init():
        carry[...] = jnp.zeros_like(carry)

    @pl.when(i < n_blocks)
    def _prompt_rows():
        _mix_route_compute(xp_ref, ysp_ref, y5p_ref, *refs[:-2], xbuf.at[i % 2])
        for j in range(SLAB_ROWS):
            xn_copy(i, tm, j).start()

    @pl.when(i == n_blocks)
    def _sample_rows():
        _mix_route_compute(xs_ref, yss_ref, y5s_ref, *refs[:-2], xbuf.at[i % 2])
        for j in range(SLAB_ROWS):
            xn_copy(i, n_sample, j).start()
        for j in range(SLAB_ROWS):
            xn_copy(i, n_sample, j).wait()

    @pl.when(i > 0)
    def _wait_previous_rows():
        for j in range(SLAB_ROWS):
            xn_copy(i - 1, tm, j).wait()

    cnt_ref[...] = carry[...]


def _mix_route_compute(x_ref, ys_ref, y5_ref, wa_ref, wb_ref, nf_ref, wrh_ref, wrl_ref, br_ref,
                       x1_ref, _xn, rt_ref, _cnt, carry, xn_buf):
    rows = x_ref.shape[0]
    x1 = x_ref[...] + _dot(ys_ref[...], wa_ref[...]) + _dot(y5_ref[...].astype(BF16), wb_ref[...])
    x1_ref[0:rows, :] = x1
    xn = _rms(x1, nf_ref[...])
    xn_buf[0:rows, :] = xn

    xh = xn.astype(BF16)
    xl = (xn - xh.astype(F32)).astype(BF16)
    logits = _dot(xh, wrh_ref[...]) + _dot(xl, wrh_ref[...]) + _dot(xh, wrl_ref[...]) + br_ref[...]
    tm = logits.shape[0]
    lane = lax.broadcasted_iota(jnp.int32, logits.shape, 1).astype(F32)
    neg = -jnp.inf
    big = float(LANES)

    def first_max(v):
        m = jnp.max(v, axis=-1, keepdims=True)
        return m, jnp.min(jnp.where(v == m, lane, big), axis=-1, keepdims=True)

    coarse = lane < MOE_GROUPS
    mc, gsel = first_max(jnp.where(coarse, logits, neg))
    psel = 1.0 / jnp.sum(jnp.where(coarse, jnp.exp(logits - mc), 0.0), axis=-1, keepdims=True)
    lo = MOE_GROUPS + MOE_EPG * gsel
    lf = jnp.where((lane >= lo) & (lane < lo + MOE_EPG), logits, neg)
    m1, i1 = first_max(lf)
    m2, i2 = first_max(jnp.where(lane == i1, neg, lf))
    e2 = jnp.exp(m2 - m1)
    g1 = psel / (1.0 + e2)
    g2 = psel * e2 / (1.0 + e2)
    e_a, e_b = i1 - MOE_GROUPS, i2 - MOE_GROUPS

    pick_a, pick_b = lane == e_a, lane == e_b
    picks = jnp.where(pick_a | pick_b, 1.0, 0.0)
    earlier = lax.broadcasted_iota(jnp.int32, (tm, tm), 0) > lax.broadcasted_iota(jnp.int32, (tm, tm), 1)
    prior = _dot(earlier.astype(BF16), picks.astype(BF16)) + carry[...]
    rank_a = jnp.sum(jnp.where(pick_a, prior, 0.0), axis=-1, keepdims=True)
    rank_b = jnp.sum(jnp.where(pick_b, prior, 0.0), axis=-1, keepdims=True)
    carry[...] = prior[tm - 1:tm, :] + picks[tm - 1:tm, :]

    out = jnp.zeros_like(logits)
    for k, v in enumerate((e_a, e_b, g1, g2, rank_a, rank_b)):
        out = jnp.where(lane == float(k), v, out)
    rt_ref[0:rows, :] = out


def _mix_route(prompt, sample, consts, tm):
    n_prompt, n_sample = prompt[0].shape[0], sample[0].shape[0]
    assert n_prompt % tm == 0 and n_sample <= tm
    n_blocks = n_prompt // tm
    total_rows = n_prompt + n_sample
    row = lambda w: pl.BlockSpec((tm, w), lambda i: (jnp.minimum(i, n_blocks - 1), 0))
    out_row = lambda w: pl.BlockSpec((tm, w), lambda i: (i, 0))
    return pl.pallas_call(
        functools.partial(_mix_route_body, n_blocks),
        grid=(n_blocks + 1,),
        in_specs=([row(D_MODEL), row(SSD_WIDTH), row(S5_WIDTH)] + [_full_spec(a) for a in sample]
                  + [_full_spec(a) for a in consts]),
        out_specs=[out_row(D_MODEL), pl.BlockSpec(memory_space=pl.ANY),
                   out_row(LANES), pl.BlockSpec((1, LANES), lambda i: (0, 0))],
        out_shape=[jax.ShapeDtypeStruct((total_rows, D_MODEL), F32),
                   jax.ShapeDtypeStruct((total_rows, SLAB_ROWS, LANES), F32),
                   jax.ShapeDtypeStruct((total_rows, LANES), F32), jax.ShapeDtypeStruct((1, LANES), F32)],
        scratch_shapes=[pltpu.VMEM((1, LANES), F32), pltpu.VMEM((2, tm, D_MODEL), F32),
                        pltpu.SemaphoreType.DMA((2, SLAB_ROWS))],
        compiler_params=pltpu.CompilerParams(dimension_semantics=("arbitrary",), vmem_limit_bytes=VMEM_LIMIT),
        name="mix_route",
    )(*prompt, *sample, *consts)


def _sc_mesh():
    return plsc.VectorSubcoreMesh(core_axis_name="c", subcore_axis_name="s")


def _sc_worker():
    return lax.axis_index("s") * SC_CORES + lax.axis_index("c")


def _sc_dispatch(xn, pos_a, pos_b, n_rows):
    n_tok = xn.shape[0]
    ch = SC_DISPATCH_ROWS
    n_chunks = n_tok // ch
    assert n_tok % ch == 0 and n_chunks >= SC_WORKERS
    max_mine = -(-n_chunks // SC_WORKERS)
    stage = [pltpu.VMEM((ch,), jnp.int32), pltpu.VMEM((ch,), jnp.int32), pltpu.VMEM((ch, SLAB_ROWS, LANES), F32),
             pltpu.SemaphoreType.DMA]

    @functools.partial(
        pl.kernel, mesh=_sc_mesh(),
        out_type=jax.ShapeDtypeStruct((n_rows, SLAB_ROWS, LANES), F32),
        scratch_types=stage + stage + [pltpu.SemaphoreType.DMA])
    def push(xn_hbm, pa_hbm, pb_hbm, xs_hbm, ia0, ib0, rows0, lsem0, ia1, ib1, rows1, lsem1, ssem):
        wid = _sc_worker()
        mine = (n_chunks - wid + SC_WORKERS - 1) // SC_WORKERS
        bufs = ((ia0, ib0, rows0, lsem0), (ia1, ib1, rows1, lsem1))

        def loads(t, b):
            ia, ib, rows, sem = bufs[b]
            off = pl.multiple_of((wid + t * SC_WORKERS) * ch, ch)
            return (pltpu.make_async_copy(pa_hbm.at[pl.ds(off, ch)], ia, sem),
                    pltpu.make_async_copy(pb_hbm.at[pl.ds(off, ch)], ib, sem),
                    pltpu.make_async_copy(xn_hbm.at[pl.ds(off, ch)], rows, sem))

        def stage_in(t, b):
            for c in loads(t, b):
                c.start()

        def scatter(t, b):
            ia, ib, rows, _ = bufs[b]
            for c in loads(t, b):
                c.wait()
            first = pltpu.async_copy(rows, xs_hbm.at[ia], ssem)
            second = pltpu.async_copy(rows, xs_hbm.at[ib], ssem)
            first.wait()
            second.wait()

        stage_in(0, 0)

        @pl.loop(0, (max_mine + 1) // 2)
        def _(p):
            t = 2 * p

            @pl.when(t + 1 < mine)
            def _():
                stage_in(t + 1, 1)

            @pl.when(t < mine)
            def _():
                scatter(t, 0)

            @pl.when(t + 2 < mine)
            def _():
                stage_in(t + 2, 0)

            @pl.when(t + 1 < mine)
            def _():
                scatter(t + 1, 1)

    return push(xn, pos_a, pos_b)


def _sc_collect(ysorted, pos_flat, ch):
    n_pick = pos_flat.shape[0]
    per_worker = n_pick // SC_WORKERS
    n_chunks = per_worker // ch
    assert n_pick % SC_WORKERS == 0 and per_worker % ch == 0

    @functools.partial(
        pl.kernel, mesh=_sc_mesh(),
        out_type=jax.ShapeDtypeStruct((n_pick, SLAB_ROWS, LANES), F32),
        scratch_types=[pltpu.VMEM((ch,), jnp.int32), pltpu.VMEM((ch,), jnp.int32),
                       pltpu.VMEM((ch, SLAB_ROWS, LANES), F32), pltpu.VMEM((ch, SLAB_ROWS, LANES), F32),
                       pltpu.SemaphoreType.DMA, pltpu.SemaphoreType.DMA])
    def pull(ys_hbm, pos_hbm, out_hbm, idx0, idx1, rows0, rows1, sem0, sem1):
        base = _sc_worker() * per_worker
        bufs = ((idx0, rows0, sem0), (idx1, rows1, sem1))

        def offset(j):
            return pl.multiple_of(base + j * ch, SUBLANES)

        def fetch(j, b):
            idx, rows, sem = bufs[b]
            pltpu.sync_copy(pos_hbm.at[pl.ds(offset(j), ch)], idx)
            pltpu.async_copy(ys_hbm.at[idx], rows, sem)

        def flush(j, b):
            idx, rows, sem = bufs[b]
            pltpu.make_async_copy(ys_hbm.at[idx], rows, sem).wait()
            pltpu.sync_copy(rows, out_hbm.at[pl.ds(offset(j), ch)])

        fetch(0, 0)

        @pl.loop(0, n_chunks // 2)
        def _(p):
            j = 2 * p
            fetch(j + 1, 1)
            flush(j, 0)

            @pl.when(j + 2 < n_chunks)
            def _():
                fetch(j + 2, 0)

            flush(j + 1, 1)

        if n_chunks % 2:
            flush(n_chunks - 1, 0)

    return pull(ysorted, pos_flat)


def _slab_columns(ref, rows, j):
    return ref[pl.ds(j, rows, stride=SLAB_ROWS), :]


def _moe_ffn_body(te_ref, nused_ref, x_ref, wg_ref, wu_ref, wd_ref, y_ref, wgb, wub, wdb):
    i = pl.program_id(0)

    @pl.when(i >= nused_ref[0])
    def _unused_tile():
        y_ref[...] = jnp.zeros_like(y_ref)

    @pl.when(i < nused_ref[0])
    def _tile():
        @pl.when((i == 0) | (te_ref[i] != te_ref[jnp.maximum(i - 1, 0)]))
        def _cast_weights():
            wgb[...] = wg_ref[0].astype(BF16)
            wub[...] = wu_ref[0].astype(BF16)
            wdb[...] = wd_ref[0].astype(BF16)

        x = jnp.concatenate([_slab_columns(x_ref, MOE_TILE, j) for j in range(SLAB_ROWS)], axis=-1).astype(BF16)
        gate = _dot(x, wgb[...])
        hmid = (gate * jax.nn.sigmoid(gate)) * _dot(x, wub[...])
        y = _dot(hmid.astype(BF16), wdb[...])
        for j in range(SLAB_ROWS):
            y_ref[pl.ds(j, MOE_TILE, stride=SLAB_ROWS), :] = y[:, j * LANES:(j + 1) * LANES]


def _moe_ffn(tile_expert, n_used, xsorted, w_gate, w_up, w_down):
    n_tiles = tile_expert.shape[0]
    wspec = lambda s: pl.BlockSpec((1,) + s, lambda i, te, nu: (te[i], 0, 0))
    tile = lambda imap: pl.BlockSpec((MOE_TILE * SLAB_ROWS, LANES), imap)
    return pl.pallas_call(
        _moe_ffn_body,
        grid_spec=pltpu.PrefetchScalarGridSpec(
            num_scalar_prefetch=2,
            grid=(n_tiles,),
            in_specs=[tile(lambda i, te, nu: (jnp.clip(i, 0, jnp.maximum(nu[0] - 1, 0)), 0)),
                      wspec((D_MODEL, MOE_D_FF)), wspec((D_MODEL, MOE_D_FF)), wspec((MOE_D_FF, D_MODEL))],
            out_specs=tile(lambda i, te, nu: (i, 0)),
            scratch_shapes=[pltpu.VMEM((D_MODEL, MOE_D_FF), BF16), pltpu.VMEM((D_MODEL, MOE_D_FF), BF16),
                            pltpu.VMEM((MOE_D_FF, D_MODEL), BF16)]),
        out_shape=jax.ShapeDtypeStruct(xsorted.shape, F32),
        compiler_params=pltpu.CompilerParams(dimension_semantics=("arbitrary",), vmem_limit_bytes=VMEM_LIMIT),
        name="moe_ffn",
    )(tile_expert, n_used, xsorted, w_gate, w_up, w_down)


def _combine_body(x1_ref, rt_ref, ya_ref, yb_ref, nf_ref, *rest):
    out_ref = rest[-1]
    rt = rt_ref[...]
    x1 = x1_ref[...]
    tm = x1.shape[0]
    ya, yb = ya_ref.at[0], yb_ref.at[0]
    x2 = jnp.concatenate(
        [x1[:, j * LANES:(j + 1) * LANES] + rt[:, 2:3] * _slab_columns(ya, tm, j) + rt[:, 3:4] * _slab_columns(yb, tm, j)
         for j in range(SLAB_ROWS)], axis=-1)
    out_ref[...] = _rms(x2, nf_ref[...])


def _combine(x1, rt, y_picks, nf, tm, rows, x_block, y_block, out_rows, out_block, out_buf=None):
    row = lambda w: pl.BlockSpec((tm, w), lambda i: (i + x_block, 0))
    pick = lambda k: pl.BlockSpec((1, tm * SLAB_ROWS, LANES), lambda i: (k, i + y_block, 0))
    in_specs = [row(D_MODEL), row(LANES), pick(0), pick(1), pl.BlockSpec((1, D_MODEL), lambda i: (0, 0))]
    args = [x1, rt, y_picks, y_picks, nf]
    aliases = {}
    if out_buf is not None:
        in_specs.append(pl.BlockSpec(memory_space=pl.ANY))
        aliases[len(args)] = 0
        args.append(out_buf)
    return pl.pallas_call(
        _combine_body,
        grid=(rows // tm,),
        in_specs=in_specs,
        out_specs=pl.BlockSpec((tm, D_MODEL), lambda i: (i + out_block, 0)),
        out_shape=jax.ShapeDtypeStruct((out_rows, D_MODEL), F32),
        input_output_aliases=aliases,
        compiler_params=pltpu.CompilerParams(dimension_semantics=("parallel",), vmem_limit_bytes=VMEM_LIMIT),
        name="moe_combine",
    )(*args)


def _route_tables(counts, eid, rank, n_tiles):
    tiles_per = (counts + MOE_TILE - 1) // MOE_TILE
    tile_end = jnp.cumsum(tiles_per)
    pstart = (tile_end - tiles_per) * MOE_TILE
    experts = jnp.arange(MOE_EXPERTS, dtype=jnp.int32)
    pos = [jnp.sum(jnp.where(e[:, None] == experts, pstart, 0), axis=-1) + r for e, r in zip(eid, rank)]
    n_used = tile_end[-1]
    tiles = jnp.arange(n_tiles, dtype=jnp.int32)
    tile_expert = jnp.sum((tile_end[None, :] <= jnp.minimum(tiles, n_used - 1)[:, None]).astype(jnp.int32), axis=1)
    return pos, tile_expert, n_used.reshape(1).astype(jnp.int32)


def _s5_tables(a_re, a_im, log_dt, b_re, b_im, c_re, c_im):
    dt = jnp.exp(log_dt)[:, None]
    mag = jnp.exp(a_re * dt)
    ab_re = mag * jnp.cos(a_im * dt)
    ab_im = mag * jnp.sin(a_im * dt)
    den = a_re * a_re + a_im * a_im
    nr = ab_re - 1.0
    q_re = (nr * a_re + ab_im * a_im) / den
    q_im = (ab_im * a_re - nr * a_im) / den
    bb_re = q_re[..., None] * b_re - q_im[..., None] * b_im
    bb_im = q_re[..., None] * b_im + q_im[..., None] * b_re
    nblk = S5_GROUPS // 16
    kw, nw = 16 * S5_GROUP_CH, 16 * S5_STATE
    same_group = (jnp.arange(kw)[:, None] // S5_GROUP_CH) == (jnp.arange(nw)[None, :] // S5_STATE)

    def in_map(bb):
        rows = bb.reshape(nblk, 16, S5_STATE, S5_GROUP_CH).transpose(0, 1, 3, 2).reshape(nblk, kw, S5_STATE)
        return jnp.where(same_group, jnp.tile(rows, (1, 1, 16)), 0.0)

    def out_map(cc):
        cols = cc.reshape(nblk, 16, S5_GROUP_CH, S5_STATE).transpose(0, 3, 1, 2).reshape(nblk, S5_STATE, kw)
        return jnp.where(same_group.T, jnp.tile(cols, (1, 16, 1)), 0.0)

    wb = jnp.concatenate([in_map(bb_re), in_map(bb_im)], axis=-1).astype(BF16)
    return (wb, ab_re.reshape(1, S5_LANES), ab_im.reshape(1, S5_LANES),
            out_map(c_re).astype(BF16), out_map(-c_im).astype(BF16))


def kernel(x_prompt, x_sample, state_ssd_conv, state_ssd_ssm, state_s5_re, state_s5_im, meta_tokens, norm_mix, w_in, conv_w, conv_b, dt_bias, a_log, d_ssd, ssd_norm, s5_a_re, s5_a_im, s5_log_dt, s5_b_re, s5_b_im, s5_c_re, s5_c_im, s5_d, w_glu, b_glu, s5_norm, w_out, norm_ffn, router_coarse_w, router_coarse_b, router_fine_w, router_fine_b, w_gate, w_up, w_down, norm_final):
    bp, seq, _ = x_prompt.shape
    bs = x_sample.shape[0]
    n_prompt = bp * seq
    n_tok = n_prompt + bs
    row2 = lambda v: v.reshape(1, -1)
    pad_heads = lambda v: jnp.pad(v, (0, LANES - SSD_HEADS)).reshape(1, LANES)

    w = w_in[0]
    o1, o2, o3 = SSD_WIDTH, SSD_WIDTH + SSD_CONV_DIM, SSD_WIDTH + SSD_CONV_DIM + SSD_HEADS
    wz, wx, wu = w[:, :o1].astype(BF16), w[:, o1:o2].astype(BF16), w[:, o3:].astype(BF16)
    wdt = jnp.pad(w[:, o2:o3], ((0, 0), (0, LANES - SSD_HEADS))).astype(BF16)
    g_mix = row2(norm_mix[0])
    cw, cb = conv_w[0], row2(conv_b[0])
    dtb, alog = pad_heads(dt_bias[0]), pad_heads(a_log[0])
    dexp = row2(jnp.repeat(d_ssd[0], SSD_HEAD_DIM))
    snrm = row2(ssd_norm[0])
    eexp = (jnp.arange(LANES)[:, None] == (jnp.arange(SSD_WIDTH) // SSD_HEAD_DIM)[None, :]).astype(BF16)
    wb5, ab_re, ab_im, wcr, wci = _s5_tables(s5_a_re[0], s5_a_im[0], s5_log_dt[0], s5_b_re[0], s5_b_im[0],
                                             s5_c_re[0], s5_c_im[0])
    d5, wglu, bglu, nrm5 = row2(s5_d[0]), w_glu[0].astype(BF16), row2(b_glu[0]), row2(s5_norm[0])
    wo_a, wo_b = w_out[0][:SSD_WIDTH].astype(BF16), w_out[0][SSD_WIDTH:].astype(BF16)
    w_r = jnp.concatenate([router_coarse_w[0], router_fine_w[0].transpose(1, 0, 2).reshape(D_MODEL, MOE_EXPERTS)], axis=1)
    w_r = jnp.pad(w_r, ((0, 0), (0, LANES - w_r.shape[1])))
    wrh = w_r.astype(BF16)
    wrl = (w_r - wrh.astype(F32)).astype(BF16)
    b_r = jnp.concatenate([router_coarse_b[0], router_fine_b[0].reshape(-1)])
    b_r = jnp.pad(b_r, (0, LANES - b_r.shape[0])).reshape(1, LANES)

    zp, xbcp, dtp, up = _in_proj(x_prompt.reshape(n_prompt, D_MODEL), g_mix, wz, wx, wdt, wu, TOK_TILE, BF16, F32)
    xsm = jnp.concatenate([x_sample.reshape(bs, D_MODEL), meta_tokens], axis=0)
    zs, xbcs, dts, us = _in_proj(xsm, g_mix, wz, wx, wdt, wu, xsm.shape[0], F32, F32)

    front = SSD_CHUNK - N_META
    padf = lambda a: jnp.pad(a[bs:], ((front, 0), (0, 0)))[None]
    gw = SSD_HPG * SSD_HEAD_DIM
    ssd_consts = (cw, cb, dtb, alog, dexp, snrm, eexp)
    _, ctail_m, _, ht_m = _ssd_chunked(
        padf(xbcs).astype(BF16), padf(dts), jnp.zeros((1, SSD_CHUNK, SSD_WIDTH), F32),
        jnp.zeros((1, SUBLANES, SSD_CONV_DIM), F32), jnp.zeros((1, SSD_GROUPS, SSD_STATE, gw), F32),
        *ssd_consts, mask_rows=front)
    y_ssd_p, ctail_p, ssm_p, _ = _ssd_chunked(
        xbcp.reshape(bp, seq, SSD_CONV_DIM), dtp.reshape(bp, seq, LANES), zp.reshape(bp, seq, SSD_WIDTH),
        ctail_m, ht_m, *ssd_consts, mask_rows=0)

    abr8, abi8 = jnp.broadcast_to(ab_re, (bp, S5_LANES)), jnp.broadcast_to(ab_im, (bp, S5_LANES))
    um8 = jnp.repeat(us[bs:], bp, axis=0).astype(BF16)
    y_s5_p, s5re_p, s5im_p = _s5_seq(up.reshape(bp, seq, S5_WIDTH), um8, wb5, abr8, abi8,
                                     wcr, wci, d5, wglu, bglu, nrm5)

    cst = state_ssd_conv[0]
    xt_s, dt_s, dec_s, bc, xs_s = _ssd_step_prep(xbcs[:bs], cst[:, 0], cst[:, 1], cst[:, 2], dts[:bs],
                                                 cw, cb, dtb, alog)
    ssm_s, y_core = _ssd_step(dt_s[:, :SSD_HEADS].reshape(-1), dec_s[:, :SSD_HEADS].reshape(-1),
                              state_ssd_ssm[0], xt_s, bc)
    y_ssd_s, y_s5_s, s5re_s, s5im_s = _sample_post(
        y_core, xs_s, zs[:bs], dexp, snrm, us[:bs], state_s5_re[0].reshape(bs, S5_LANES),
        state_s5_im[0].reshape(bs, S5_LANES), wb5, ab_re, ab_im, wcr, wci, d5, wglu, bglu, nrm5)

    route_consts = (wo_a, wo_b, row2(norm_ffn[0]), wrh, wrl, b_r)
    x1, xn, rt, counts = _mix_route(
        (x_prompt.reshape(n_prompt, D_MODEL), y_ssd_p.reshape(n_prompt, SSD_WIDTH), y_s5_p.reshape(n_prompt, S5_WIDTH)),
        (x_sample.reshape(bs, D_MODEL), y_ssd_s, y_s5_s), route_consts, TOK_TILE)

    n_tiles = -(-2 * n_tok // MOE_TILE) + MOE_EXPERTS
    lane_i32 = lambda k: rt[:, k].astype(jnp.int32)
    eid = [jnp.clip(lane_i32(k), 0, MOE_EXPERTS - 1) for k in (0, 1)]
    (pos_a, pos_b), tile_expert, n_used = _route_tables(counts[0, :MOE_EXPERTS].astype(jnp.int32), eid,
                                                        [lane_i32(4), lane_i32(5)], n_tiles)
    slabs = lambda a: a.reshape(-1, SLAB_ROWS, LANES)
    xsorted = _sc_dispatch(xn, pos_a, pos_b, n_tiles * MOE_TILE)
    ysorted = _moe_ffn(tile_expert, n_used, xsorted.reshape(-1, LANES), w_gate[0], w_up[0], w_down[0])
    nfin = row2(norm_final)

    half = n_prompt // 2

    def collect(lo, hi, ch):
        picks = jnp.concatenate([pos_a[lo:hi], pos_b[lo:hi]])
        return _sc_collect(slabs(ysorted), picks, ch).reshape(2, (hi - lo) * SLAB_ROWS, LANES)

    picks_1 = collect(0, half, SC_COLLECT_ROWS[0])
    picks_2 = collect(half, n_tok, SC_COLLECT_ROWS[1])
    blocks = half // MOE_TILE
    y_p = _combine(x1, rt, picks_1, nfin, MOE_TILE, half, 0, 0, n_prompt, 0)
    y_p = _combine(x1, rt, picks_2, nfin, MOE_TILE, half, blocks, 0, n_prompt, blocks, out_buf=y_p)
    y_s = _combine(x1, rt, picks_2, nfin, bs, bs, n_prompt // bs, half // bs, bs, 0)

    s5_state = lambda a, b: a.reshape(1, b, S5_GROUPS, S5_STATE)
    new_conv_s = jnp.stack([cst[:, 1], cst[:, 2], xbcs[:bs]], axis=1)[None]
    return (y_p.reshape(bp, seq, D_MODEL), y_s.reshape(bs, 1, D_MODEL),
            ctail_p[:, SUBLANES - (SSD_CONV - 1):][None], ssm_p[None], s5_state(s5re_p, bp), s5_state(s5im_p, bp),
            new_conv_s, ssm_s[None], s5_state(s5re_s, bs), s5_state(s5im_s, bs))
```

```python
import functools

import jax
import jax.numpy as jnp
from jax import lax
from jax.experimental import pallas as pl
from jax.experimental.pallas import tpu as pltpu
from jax.experimental.pallas import tpu_sc as plsc

F32, BF16 = jnp.float32, jnp.bfloat16

D_MODEL = 1024
N_META = 16
SSD_WIDTH = 1024
SSD_HEAD_DIM = 64
SSD_HEADS = 16
SSD_GROUPS = 2
SSD_HPG = SSD_HEADS // SSD_GROUPS
SSD_STATE = 128
SSD_CONV = 4
SSD_CHUNK = 128
SSD_CONV_DIM = SSD_WIDTH + 2 * SSD_GROUPS * SSD_STATE
S5_WIDTH = 1024
S5_GROUP_CH = 16
S5_GROUPS = 64
S5_STATE = 64
S5_LANES = S5_GROUPS * S5_STATE
MOE_GROUPS = 4
MOE_EPG = 8
MOE_EXPERTS = MOE_GROUPS * MOE_EPG
MOE_D_FF = 512
EPS = 1e-6

LANES = 128
SUBLANES = 8
VMEM_LIMIT = 56 * 1024 * 1024

S5_TIME_TILE = 64
S5_SCAN_LANES = 512
MOE_TILE = 256
SLAB_ROWS = D_MODEL // LANES
SC_CORES = 2
SC_SUBCORES = 16
SC_WORKERS = SC_CORES * SC_SUBCORES
SC_DISPATCH_ROWS = 32
SC_COLLECT_ROWS = (32, 40)
TOK_TILE = 512


def _dot(a, b):
    return jnp.dot(a, b, preferred_element_type=F32)


def _rms(x, g):
    return x * lax.rsqrt(jnp.mean(x * x, axis=-1, keepdims=True) + EPS) * g


def _softplus(x):
    return jnp.maximum(x, 0.0) + jnp.log1p(jnp.exp(-jnp.abs(x)))


def _split3(x):
    hi = x.astype(BF16)
    r = x - hi.astype(F32)
    mid = r.astype(BF16)
    lo = (r - mid.astype(F32)).astype(BF16)
    return hi, mid, lo


def _dot3(x, w):
    hi, mid, lo = _split3(x)
    return _dot(hi, w) + _dot(mid, w) + _dot(lo, w)


def _dot3_left(w, x):
    hi, mid, lo = _split3(x)
    return _dot(w, hi) + _dot(w, mid) + _dot(w, lo)


def _full_spec(a):
    nd = a.ndim
    return pl.BlockSpec(a.shape, lambda *_: (0,) * nd)


def _resident_spec(a):
    nd = a.ndim
    return pl.BlockSpec(a.shape, lambda *_: (0,) * nd, pipeline_mode=pl.Buffered(1))


def _in_proj_body(x_ref, g_ref, wz_ref, wx_ref, wdt_ref, wu_ref, z_ref, xbc_ref, dt_ref, u_ref):
    xb = _rms(x_ref[...], g_ref[...]).astype(BF16)
    z_ref[...] = _dot(xb, wz_ref[...]).astype(z_ref.dtype)
    xbc_ref[...] = _dot(xb, wx_ref[...]).astype(xbc_ref.dtype)
    dt_ref[...] = _dot(xb, wdt_ref[...])
    u_ref[...] = _dot(xb, wu_ref[...]).astype(u_ref.dtype)


def _in_proj(x2d, g, wz, wx, wdt, wu, tm, act_dtype, u_dtype):
    rows = x2d.shape[0]
    row = lambda w: pl.BlockSpec((tm, w), lambda i: (i, 0))
    return pl.pallas_call(
        _in_proj_body,
        grid=(rows // tm,),
        in_specs=[row(D_MODEL), _full_spec(g), _full_spec(wz), _full_spec(wx), _full_spec(wdt), _full_spec(wu)],
        out_specs=[row(SSD_WIDTH), row(SSD_CONV_DIM), row(LANES), row(S5_WIDTH)],
        out_shape=[jax.ShapeDtypeStruct((rows, SSD_WIDTH), act_dtype),
                   jax.ShapeDtypeStruct((rows, SSD_CONV_DIM), act_dtype),
                   jax.ShapeDtypeStruct((rows, LANES), F32),
                   jax.ShapeDtypeStruct((rows, S5_WIDTH), u_dtype)],
        compiler_params=pltpu.CompilerParams(dimension_semantics=("parallel",), vmem_limit_bytes=VMEM_LIMIT),
        name="in_proj",
    )(x2d, g, wz, wx, wdt, wu)


def _ssd_body(mask_rows, xbc_ref, dt_ref, z_ref, cinit_ref, hinit_ref, cw_ref, cb_ref, dtb_ref, alog_ref,
              dexp_ref, nrm_ref, eexp_ref, y_ref, ctail_ref, st_ref, hto_ref, xwin, hT):
    c = pl.program_id(1)
    L = SSD_CHUNK

    @pl.when(c == 0)
    def _init():
        xwin[...] = cinit_ref[0]
        hT[...] = hinit_ref[0]

    x_b = xbc_ref[0]
    x_f = x_b.astype(F32)
    taps = SSD_CONV - 1
    m_i = lax.broadcasted_iota(jnp.int32, (taps * L, L), 0)
    r_i = lax.broadcasted_iota(jnp.int32, (taps * L, L), 1)
    shift = (r_i + (taps - m_i // L) == m_i % L).astype(BF16)
    shifted = _dot(shift, x_b)
    acc = cb_ref[...] + x_f * cw_ref[taps:taps + 1, :]
    for k in range(taps):
        acc = acc + shifted[k * L:(k + 1) * L, :] * cw_ref[k:k + 1, :]
    joint = jnp.concatenate([xwin[...], x_f[0:SUBLANES, :]], axis=0)
    row8 = lax.broadcasted_iota(jnp.int32, (SUBLANES, 1), 0)
    head = acc[0:SUBLANES, :]
    for k in range(taps):
        d = taps - k
        head = head + jnp.where(row8 < d, joint[SUBLANES - d:2 * SUBLANES - d, :], 0.0) * cw_ref[k:k + 1, :]
    acc = jnp.concatenate([head, acc[SUBLANES:, :]], axis=0)
    tail = x_f[L - SUBLANES:, :]
    xwin[...] = tail
    ctail_ref[0] = tail

    xact = acc * jax.nn.sigmoid(acc)
    dt = _softplus(dt_ref[0] + dtb_ref[...])
    if mask_rows:
        valid = lax.broadcasted_iota(jnp.int32, (L, 1), 0) >= mask_rows
        xact = jnp.where(valid, xact, 0.0)
        dt = jnp.where(valid, dt, 0.0)

    a_neg = -jnp.exp(alog_ref[...])
    dA = dt * a_neg
    row_i = lax.broadcasted_iota(jnp.int32, (L, L), 0)
    col_i = lax.broadcasted_iota(jnp.int32, (L, L), 1)
    causal = row_i >= col_i
    tril = causal.astype(BF16)
    cs = _dot3_left(tril, dA)
    csT = cs.T
    dtT = dt.T
    ecs = jnp.exp(cs)
    wdec = jnp.exp(cs[L - 1:L, :] - cs) * dt
    eexp = eexp_ref[...]
    ecs_e = _dot3(ecs, eexp)
    wdec_e = _dot3(wdec, eexp)
    lane = lax.broadcasted_iota(jnp.int32, (L, LANES), 1)
    first_half = lane < SSD_HEAD_DIM

    gw = SSD_HPG * SSD_HEAD_DIM
    y_groups = []
    for g in range(SSD_GROUPS):
        b_g = xact[:, SSD_WIDTH + g * SSD_STATE: SSD_WIDTH + (g + 1) * SSD_STATE]
        c_g = xact[:, SSD_WIDTH + (SSD_GROUPS + g) * SSD_STATE: SSD_WIDTH + (SSD_GROUPS + g + 1) * SSD_STATE]
        b_b = b_g.astype(BF16)
        c_b = c_g.astype(BF16)
        cb = lax.dot_general(c_b, b_b, (((1,), (1,)), ((), ())), preferred_element_type=F32)
        xs_g = xact[:, g * gw:(g + 1) * gw]
        h_prev = hT[g]
        y_off = _dot(c_b, h_prev.astype(BF16)) * ecs_e[:, g * gw:(g + 1) * gw]
        xdec = (xs_g * wdec_e[:, g * gw:(g + 1) * gw]).astype(BF16)
        hT[g] = h_prev * ecs_e[L - 1:L, g * gw:(g + 1) * gw] + _dot(b_g.T.astype(BF16), xdec)
        pieces = []
        for j in range(SSD_HPG // 2):
            xs_pair = xs_g[:, j * LANES:(j + 1) * LANES]
            halves = (jnp.where(first_half, xs_pair, 0.0).astype(BF16),
                      jnp.where(first_half, 0.0, xs_pair).astype(BF16))
            yd = None
            for t in range(2):
                h = g * SSD_HPG + 2 * j + t
                seg = cs[:, h:h + 1] - csT[h:h + 1, :]
                lmat = jnp.exp(jnp.where(causal, seg, -jnp.inf))
                m = (cb * lmat * dtT[h:h + 1, :]).astype(BF16)
                part = _dot(m, halves[t])
                yd = part if yd is None else yd + part
            pieces.append(yd)
        y_groups.append(jnp.concatenate(pieces, axis=-1) + y_off + dexp_ref[:, g * gw:(g + 1) * gw] * xs_g)
    y = jnp.concatenate(y_groups, axis=-1)
    z = z_ref[0].astype(F32)
    y_ref[0] = _rms(y * (z * jax.nn.sigmoid(z)), nrm_ref[...]).astype(y_ref.dtype)

    @pl.when(c == pl.num_programs(1) - 1)
    def _emit():
        hto_ref[0] = hT[...]
        for g in range(SSD_GROUPS):
            t = hT[g].T
            for k in range(SSD_HPG):
                st_ref[0, g * SSD_HPG + k] = t[k * SSD_HEAD_DIM:(k + 1) * SSD_HEAD_DIM, :]


def _ssd_chunked(xbc, dt, z, cinit, hinit, cw, cb, dtb, alog, dexp, nrm, eexp, mask_rows):
    bsz, seq, _ = xbc.shape
    nc = seq // SSD_CHUNK
    gw = SSD_HPG * SSD_HEAD_DIM
    blk = lambda w: pl.BlockSpec((1, SSD_CHUNK, w), lambda b, c: (b, c, 0))
    return pl.pallas_call(
        functools.partial(_ssd_body, mask_rows),
        grid=(bsz, nc),
        in_specs=[blk(SSD_CONV_DIM), blk(LANES), blk(SSD_WIDTH),
                  pl.BlockSpec((1, SUBLANES, SSD_CONV_DIM), lambda b, c: (0, 0, 0)),
                  pl.BlockSpec((1, SSD_GROUPS, SSD_STATE, gw), lambda b, c: (0, 0, 0, 0)),
                  _full_spec(cw), _full_spec(cb), _full_spec(dtb), _full_spec(alog),
                  _full_spec(dexp), _full_spec(nrm), _full_spec(eexp)],
        out_specs=[blk(SSD_WIDTH),
                   pl.BlockSpec((1, SUBLANES, SSD_CONV_DIM), lambda b, c: (b, 0, 0)),
                   pl.BlockSpec((1, SSD_HEADS, SSD_HEAD_DIM, SSD_STATE), lambda b, c: (b, 0, 0, 0)),
                   pl.BlockSpec((1, SSD_GROUPS, SSD_STATE, gw), lambda b, c: (b, 0, 0, 0))],
        out_shape=[jax.ShapeDtypeStruct((bsz, seq, SSD_WIDTH), BF16),
                   jax.ShapeDtypeStruct((bsz, SUBLANES, SSD_CONV_DIM), F32),
                   jax.ShapeDtypeStruct((bsz, SSD_HEADS, SSD_HEAD_DIM, SSD_STATE), F32),
                   jax.ShapeDtypeStruct((bsz, SSD_GROUPS, SSD_STATE, gw), F32)],
        scratch_shapes=[pltpu.VMEM((SUBLANES, SSD_CONV_DIM), F32),
                        pltpu.VMEM((SSD_GROUPS, SSD_STATE, gw), F32)],
        compiler_params=pltpu.CompilerParams(dimension_semantics=("parallel", "arbitrary"),
                                             vmem_limit_bytes=VMEM_LIMIT),
        name="ssd_chunked",
    )(xbc, dt, z, cinit, hinit, cw, cb, dtb, alog, dexp, nrm, eexp)


def _ssd_step_prep_body(xbc_ref, c0_ref, c1_ref, c2_ref, dt_ref, cw_ref, cb_ref, dtb_ref, alog_ref,
                        xt_ref, dt_out_ref, dec_ref, bc_ref, xs_ref):
    acc = cb_ref[...]
    for k, r in enumerate((c0_ref, c1_ref, c2_ref, xbc_ref)):
        acc = acc + r[...] * cw_ref[k:k + 1, :]
    xact = acc * jax.nn.sigmoid(acc)
    xs = xact[:, :SSD_WIDTH]
    dt = _softplus(dt_ref[...] + dtb_ref[...])
    dt_out_ref[...] = dt
    dec_ref[...] = jnp.exp(dt * -jnp.exp(alog_ref[...]))
    bc_ref[...] = xact[:, SSD_WIDTH:]
    xs_ref[...] = xs
    xt_ref[...] = xs.T.astype(xt_ref.dtype)


def _ssd_step_prep(xbc, c0, c1, c2, dt, cw, cb, dtb, alog):
    n = xbc.shape[0]
    args = (xbc, c0, c1, c2, dt, cw, cb, dtb, alog)
    spec = lambda r, w: pl.BlockSpec((r, w), lambda: (0, 0))
    return pl.pallas_call(
        _ssd_step_prep_body,
        in_specs=[_full_spec(a) for a in args],
        out_specs=[spec(SSD_WIDTH, n), spec(n, LANES), spec(n, LANES), spec(n, 2 * SSD_GROUPS * SSD_STATE),
                   spec(n, SSD_WIDTH)],
        out_shape=[jax.ShapeDtypeStruct((SSD_WIDTH, n), BF16), jax.ShapeDtypeStruct((n, LANES), F32),
                   jax.ShapeDtypeStruct((n, LANES), F32),
                   jax.ShapeDtypeStruct((n, 2 * SSD_GROUPS * SSD_STATE), F32),
                   jax.ShapeDtypeStruct((n, SSD_WIDTH), F32)],
        compiler_params=pltpu.CompilerParams(vmem_limit_bytes=VMEM_LIMIT),
        name="ssd_step_prep",
    )(*args)


def _ssd_step_body(dt_ref, dec_ref, st_ref, xt_ref, bc_ref, so_ref, y_ref):
    n = xt_ref.shape[1]
    gw = SSD_HPG * SSD_HEAD_DIM
    blk = pl.program_id(0)
    seq_id = lax.broadcasted_iota(jnp.int32, (n, SSD_STATE), 0)
    sub_id = lax.broadcasted_iota(jnp.int32, (SUBLANES, gw), 0)
    base = pl.multiple_of(blk * SUBLANES, SUBLANES)
    y_acc = [jnp.zeros((SUBLANES, gw), F32) for _ in range(SSD_GROUPS)]
    for i in range(SUBLANES):
        s = blk * SUBLANES + i
        for g in range(SSD_GROUPS):
            b_all = bc_ref[:, g * SSD_STATE:(g + 1) * SSD_STATE]
            rhs = jnp.where(seq_id == s, b_all, 0.0).astype(BF16)
            outer = _dot(xt_ref[g * gw:(g + 1) * gw, :], rhs)
            news = []
            for k in range(SSD_HPG):
                h = g * SSD_HPG + k
                new = (dec_ref[s * SSD_HEADS + h] * st_ref[i, h]
                       + dt_ref[s * SSD_HEADS + h] * outer[k * SSD_HEAD_DIM:(k + 1) * SSD_HEAD_DIM, :])
                so_ref[i, h] = new
                news.append(new)
            new_g = jnp.concatenate(news, axis=0).astype(BF16)
            c_lo = (SSD_GROUPS + g) * SSD_STATE
            c_blk = bc_ref[pl.ds(base, SUBLANES), c_lo:c_lo + SSD_STATE].astype(BF16)
            r = lax.dot_general(c_blk, new_g, (((1,), (1,)), ((), ())), preferred_element_type=F32)
            y_acc[g] = y_acc[g] + jnp.where(sub_id == i, r, 0.0)
    y_ref[...] = jnp.concatenate(y_acc, axis=-1)


def _ssd_step(dt_flat, dec_flat, state, xt, bc):
    n = state.shape[0]
    st_spec = pl.BlockSpec((SUBLANES, SSD_HEADS, SSD_HEAD_DIM, SSD_STATE), lambda i, *_: (i, 0, 0, 0))
    return pl.pallas_call(
        _ssd_step_body,
        grid_spec=pltpu.PrefetchScalarGridSpec(
            num_scalar_prefetch=2,
            grid=(n // SUBLANES,),
            in_specs=[st_spec, pl.BlockSpec(xt.shape, lambda i, *_: (0, 0)),
                      pl.BlockSpec(bc.shape, lambda i, *_: (0, 0))],
            out_specs=[st_spec, pl.BlockSpec((SUBLANES, SSD_WIDTH), lambda i, *_: (i, 0))]),
        out_shape=[jax.ShapeDtypeStruct(state.shape, F32), jax.ShapeDtypeStruct((n, SSD_WIDTH), F32)],
        compiler_params=pltpu.CompilerParams(dimension_semantics=("parallel",), vmem_limit_bytes=VMEM_LIMIT),
        name="ssd_step",
    )(dt_flat, dec_flat, state, xt, bc)


def _s5_project_in(u_b16, wb_ref, store):
    kw = 16 * S5_GROUP_CH
    nw = 16 * S5_STATE
    for j in range(S5_WIDTH // kw):
        r = _dot(u_b16[:, j * kw:(j + 1) * kw], wb_ref[j])
        store(j, r[:, :nw], r[:, nw:])


def _s5_tail(hre_of, him_of, u_f32, wcr_ref, wci_ref, d_ref, wglu_ref, bglu_ref, nrm_ref):
    cols = []
    for j in range(wcr_ref.shape[0]):
        cols.append(_dot(hre_of(j).astype(BF16), wcr_ref[j]) + _dot(him_of(j).astype(BF16), wci_ref[j]))
    return _s5_finish(cols, u_f32, d_ref, wglu_ref, bglu_ref, nrm_ref)


def _s5_finish(cols, u_f32, d_ref, wglu_ref, bglu_ref, nrm_ref):
    y = jnp.concatenate(cols, axis=-1) + d_ref[...] * u_f32
    y = jax.nn.gelu(y)
    y = y * jax.nn.sigmoid(_dot(y.astype(BF16), wglu_ref[...]) + bglu_ref[...])
    return _rms(y, nrm_ref[...])


def _s5_seq_body(u_hbm, um_ref, wb_ref, abr_ref, abi_ref, wcr_ref, wci_ref, d_ref, wglu_ref, bglu_ref, nrm_ref,
                 y_hbm, sre_ref, sim_ref, ubuf, ybuf, bu, h, in_sems, out_sems):
    j = pl.program_id(0)
    last = pl.num_programs(0) - 1
    lc, bsz = ubuf.shape[1], ubuf.shape[2]
    rows = lc * bsz
    nw = 16 * S5_STATE

    def in_copy(step, b):
        return pltpu.make_async_copy(u_hbm.at[b, pl.ds(step * lc, lc), :], ubuf.at[step % 2, :, b, :],
                                     in_sems.at[step % 2, b])

    def out_copy(step, b):
        return pltpu.make_async_copy(ybuf.at[step % 2, :, b, :], y_hbm.at[b, pl.ds(step * lc, lc), :],
                                     out_sems.at[step % 2, b])

    def project_in(u_b16, nrows):
        def store(jj, re, im):
            bu[0:nrows, jj * nw:(jj + 1) * nw] = re
            bu[0:nrows, S5_LANES + jj * nw:S5_LANES + (jj + 1) * nw] = im
        _s5_project_in(u_b16, wb_ref, store)

    def scan(nsteps):
        for k in range(S5_LANES // S5_SCAN_LANES):
            sl_r = pl.ds(k * S5_SCAN_LANES, S5_SCAN_LANES)
            sl_i = pl.ds(S5_LANES + k * S5_SCAN_LANES, S5_SCAN_LANES)
            ar = abr_ref[:, sl_r]
            ai = abi_ref[:, sl_r]

            def step(l, carry):
                hr, hi = carry
                slab = pl.ds(pl.multiple_of(l * bsz, bsz), bsz)
                nr = ar * hr - ai * hi + bu[slab, sl_r]
                ni = ar * hi + ai * hr + bu[slab, sl_i]
                bu[slab, sl_r] = nr
                bu[slab, sl_i] = ni
                return nr, ni

            hr, hi = lax.fori_loop(0, nsteps, step, (h[:, sl_r], h[:, sl_i]))
            h[:, sl_r] = hr
            h[:, sl_i] = hi

    @pl.when(j == 0)
    def _first():
        for b in range(bsz):
            in_copy(0, b).start()
        h[...] = jnp.zeros_like(h)
        project_in(um_ref[...], N_META * bsz)
        scan(N_META)

    @pl.when(j < last)
    def _prefetch():
        for b in range(bsz):
            in_copy(j + 1, b).start()

    for b in range(bsz):
        in_copy(j, b).wait()
    u2 = ubuf[j % 2].reshape(rows, S5_WIDTH)
    u_b16 = u2.astype(BF16)
    kw = 16 * S5_GROUP_CH

    def project_block(jj):
        r = _dot(u_b16[:, jj * kw:(jj + 1) * kw], wb_ref[jj])
        bu[0:rows, jj * nw:(jj + 1) * nw] = r[:, :nw]
        bu[0:rows, S5_LANES + jj * nw:S5_LANES + (jj + 1) * nw] = r[:, nw:]

    def scan_block(jj):
        for k in range(nw // S5_SCAN_LANES):
            lo = jj * nw + k * S5_SCAN_LANES
            sl_r = slice(lo, lo + S5_SCAN_LANES)
            sl_i = slice(S5_LANES + lo, S5_LANES + lo + S5_SCAN_LANES)
            ar, ai = abr_ref[:, sl_r], abi_ref[:, sl_r]
            hr, hi = h[:, sl_r], h[:, sl_i]
            for l in range(lc):
                slab = slice(l * bsz, (l + 1) * bsz)
                hr, hi = (ar * hr - ai * hi + bu[slab, sl_r], ar * hi + ai * hr + bu[slab, sl_i])
                bu[slab, sl_r] = hr
                bu[slab, sl_i] = hi
            h[:, sl_r] = hr
            h[:, sl_i] = hi

    def readout_block(jj):
        return (_dot(bu[:, jj * nw:(jj + 1) * nw].astype(BF16), wcr_ref[jj])
                + _dot(bu[:, S5_LANES + jj * nw:S5_LANES + (jj + 1) * nw].astype(BF16), wci_ref[jj]))

    n_blocks = S5_WIDTH // kw
    project_block(0)
    cols = []
    for jj in range(n_blocks):
        if jj + 1 < n_blocks:
            project_block(jj + 1)
        scan_block(jj)
        cols.append(readout_block(jj))
    y = _s5_finish(cols, u2, d_ref, wglu_ref, bglu_ref, nrm_ref)
    ybuf[j % 2] = y.reshape(lc, bsz, S5_WIDTH)
    for b in range(bsz):
        out_copy(j, b).start()

    @pl.when(j > 0)
    def _wait_previous_out():
        for b in range(bsz):
            out_copy(j - 1, b).wait()

    @pl.when(j == last)
    def _emit():
        for b in range(bsz):
            out_copy(j, b).wait()
        sre_ref[...] = h[:, 0:S5_LANES]
        sim_ref[...] = h[:, S5_LANES:]


def _s5_seq(u, um, wb, abr, abi, wcr, wci, d, wglu, bglu, nrm):
    bsz, seq, _ = u.shape
    lc = S5_TIME_TILE
    consts = (um, wb, abr, abi, wcr, wci, d, wglu, bglu, nrm)
    st = pl.BlockSpec((bsz, S5_LANES), lambda j: (0, 0))
    return pl.pallas_call(
        _s5_seq_body,
        grid=(seq // lc,),
        in_specs=[pl.BlockSpec(memory_space=pl.ANY)] + [_resident_spec(a) for a in consts],
        out_specs=[pl.BlockSpec(memory_space=pl.ANY), st, st],
        out_shape=[jax.ShapeDtypeStruct((bsz, seq, S5_WIDTH), F32),
                   jax.ShapeDtypeStruct((bsz, S5_LANES), F32), jax.ShapeDtypeStruct((bsz, S5_LANES), F32)],
        scratch_shapes=[pltpu.VMEM((2, lc, bsz, S5_WIDTH), F32), pltpu.VMEM((2, lc, bsz, S5_WIDTH), F32),
                        pltpu.VMEM((lc * bsz, 2 * S5_LANES), F32), pltpu.VMEM((bsz, 2 * S5_LANES), F32),
                        pltpu.SemaphoreType.DMA((2, bsz)), pltpu.SemaphoreType.DMA((2, bsz))],
        compiler_params=pltpu.CompilerParams(dimension_semantics=("arbitrary",), vmem_limit_bytes=VMEM_LIMIT),
        name="s5_seq",
    )(u, *consts)


def _sample_post_body(yc_ref, xs_ref, z_ref, dexp_ref, snrm_ref, u_ref, hr_ref, hi_ref, wb_ref, abr_ref, abi_ref,
                      wcr_ref, wci_ref, d_ref, wglu_ref, bglu_ref, nrm_ref,
                      yssd_ref, ys5_ref, nre_ref, nim_ref):
    z = z_ref[...]
    y = yc_ref[...] + dexp_ref[...] * xs_ref[...]
    yssd_ref[...] = _rms(y * (z * jax.nn.sigmoid(z)), snrm_ref[...]).astype(yssd_ref.dtype)

    u = u_ref[...]
    nw = 16 * S5_STATE
    ar, ai = abr_ref[...], abi_ref[...]

    def store(jj, re, im):
        sl = slice(jj * nw, (jj + 1) * nw)
        h0r, h0i = hr_ref[:, sl], hi_ref[:, sl]
        nre_ref[:, sl] = ar[:, sl] * h0r - ai[:, sl] * h0i + re
        nim_ref[:, sl] = ar[:, sl] * h0i + ai[:, sl] * h0r + im

    _s5_project_in(u.astype(BF16), wb_ref, store)
    slab = lambda ref: (lambda jj: ref[:, jj * nw:(jj + 1) * nw])
    y5 = _s5_tail(slab(nre_ref), slab(nim_ref), u, wcr_ref, wci_ref, d_ref, wglu_ref, bglu_ref, nrm_ref)
    ys5_ref[...] = y5.astype(ys5_ref.dtype)


def _sample_post(yc, xs, z, dexp, snrm, u, h0r, h0i, wb, abr1, abi1, wcr, wci, d, wglu, bglu, nrm):
    n = yc.shape[0]
    args = (yc, xs, z, dexp, snrm, u, h0r, h0i, wb, abr1, abi1, wcr, wci, d, wglu, bglu, nrm)
    spec = lambda w: pl.BlockSpec((n, w), lambda: (0, 0))
    return pl.pallas_call(
        _sample_post_body,
        in_specs=[_full_spec(a) for a in args],
        out_specs=[spec(SSD_WIDTH), spec(S5_WIDTH), spec(S5_LANES), spec(S5_LANES)],
        out_shape=[jax.ShapeDtypeStruct((n, SSD_WIDTH), BF16), jax.ShapeDtypeStruct((n, S5_WIDTH), BF16),
                   jax.ShapeDtypeStruct((n, S5_LANES), F32), jax.ShapeDtypeStruct((n, S5_LANES), F32)],
        compiler_params=pltpu.CompilerParams(vmem_limit_bytes=VMEM_LIMIT),
        name="sample_post",
    )(*args)


def _mix_route_body(n_blocks, n_sorted, xp_ref, ysp_ref, y5p_ref, xs_ref, yss_ref, y5s_ref, *refs):
    consts = refs[:6]
    x1_ref, xn_hbm, rt_ref, pos_ref, meta_ref, carry, fields, xbuf, sems = refs[6:]
    i = pl.program_id(0)
    tm, n_sample = xp_ref.shape[0], xs_ref.shape[0]
    col0 = pl.multiple_of(i * tm, LANES)

    def xn_copy(step, rows, j):
        return pltpu.make_async_copy(xbuf.at[step % 2, pl.ds(0, rows), pl.ds(j * LANES, LANES)],
                                     xn_hbm.at[pl.ds(step * tm, rows), j, :], sems.at[step % 2, j])

    @pl.when(i == 0)
    def _init():
        carry[...] = jnp.zeros_like(carry)

    @pl.when(i < n_blocks)
    def _prompt_rows():
        _mix_route_compute(xp_ref, ysp_ref, y5p_ref, *consts, x1_ref, rt_ref, carry, xbuf.at[i % 2], fields, col0)
        for j in range(SLAB_ROWS):
            xn_copy(i, tm, j).start()

    @pl.when(i == n_blocks)
    def _sample_rows():
        _mix_route_compute(xs_ref, yss_ref, y5s_ref, *consts, x1_ref, rt_ref, carry, xbuf.at[i % 2], fields, col0)
        for j in range(SLAB_ROWS):
            xn_copy(i, n_sample, j).start()
        _route_layout(carry, fields, pos_ref, meta_ref, n_sorted)
        for j in range(SLAB_ROWS):
            xn_copy(i, n_sample, j).wait()

    @pl.when(i > 0)
    def _wait_previous_rows():
        for j in range(SLAB_ROWS):
            xn_copy(i - 1, tm, j).wait()


def _route_layout(carry, fields, pos_ref, meta_ref, n_sorted):
    counts = carry[...]
    tiles_per = jnp.floor((counts + (MOE_TILE - 1)) * (1.0 / MOE_TILE))
    upto = lax.broadcasted_iota(jnp.int32, (LANES, LANES), 0) <= lax.broadcasted_iota(jnp.int32, (LANES, LANES), 1)
    tile_end = _dot(tiles_per.astype(BF16), upto.astype(BF16))
    pstart = (tile_end - tiles_per) * MOE_TILE
    n_used = tile_end[:, MOE_EXPERTS - 1:MOE_EXPERTS]

    f = fields[...]
    first_row = jnp.zeros_like(f)
    tile_id = jnp.minimum(lax.broadcasted_iota(jnp.int32, meta_ref.shape, 1).astype(F32), n_used - 1.0)
    tile_expert = jnp.zeros(meta_ref.shape, F32)
    for e in range(MOE_EXPERTS):
        first_row = first_row + jnp.where(f == float(e), pstart[:, e:e + 1], 0.0)
        tile_expert = tile_expert + jnp.where(tile_end[:, e:e + 1] <= tile_id, 1.0, 0.0)
    pos = first_row + pltpu.roll(f, shift=4, axis=0)
    pos_ref[...] = jnp.clip(pos, 0.0, n_sorted - 1.0).astype(jnp.int32)
    is_row0 = lax.broadcasted_iota(jnp.int32, meta_ref.shape, 0) == 0
    meta_ref[...] = jnp.where(is_row0, tile_expert, n_used).astype(jnp.int32)


def _mix_route_compute(x_ref, ys_ref, y5_ref, wa_ref, wb_ref, nf_ref, wrh_ref, wrl_ref, br_ref,
                       x1_ref, rt_ref, carry, xn_buf, fields, col0):
    rows = x_ref.shape[0]
    x1 = x_ref[...] + _dot(ys_ref[...], wa_ref[...]) + _dot(y5_ref[...].astype(BF16), wb_ref[...])
    x1_ref[0:rows, :] = x1
    xn = _rms(x1, nf_ref[...])
    xn_buf[0:rows, :] = xn

    xh = xn.astype(BF16)
    xl = (xn - xh.astype(F32)).astype(BF16)
    logits = _dot(xh, wrh_ref[...]) + _dot(xl, wrh_ref[...]) + _dot(xh, wrl_ref[...]) + br_ref[...]
    tm = logits.shape[0]
    lane = lax.broadcasted_iota(jnp.int32, logits.shape, 1).astype(F32)
    neg = -jnp.inf
    big = float(LANES)

    def first_max(v):
        m = jnp.max(v, axis=-1, keepdims=True)
        return m, jnp.min(jnp.where(v == m, lane, big), axis=-1, keepdims=True)

    coarse = lane < MOE_GROUPS
    mc, gsel = first_max(jnp.where(coarse, logits, neg))
    psel = 1.0 / jnp.sum(jnp.where(coarse, jnp.exp(logits - mc), 0.0), axis=-1, keepdims=True)
    lo = MOE_GROUPS + MOE_EPG * gsel
    lf = jnp.where((lane >= lo) & (lane < lo + MOE_EPG), logits, neg)
    m1, i1 = first_max(lf)
    m2, i2 = first_max(jnp.where(lane == i1, neg, lf))
    e2 = jnp.exp(m2 - m1)
    g1 = psel / (1.0 + e2)
    g2 = psel * e2 / (1.0 + e2)
    e_a, e_b = i1 - MOE_GROUPS, i2 - MOE_GROUPS

    pick_a, pick_b = lane == e_a, lane == e_b
    picks = jnp.where(pick_a | pick_b, 1.0, 0.0)
    earlier = lax.broadcasted_iota(jnp.int32, (tm, tm), 0) > lax.broadcasted_iota(jnp.int32, (tm, tm), 1)
    prior = _dot(earlier.astype(BF16), picks.astype(BF16)) + carry[...]
    rank_a = jnp.sum(jnp.where(pick_a, prior, 0.0), axis=-1, keepdims=True)
    rank_b = jnp.sum(jnp.where(pick_b, prior, 0.0), axis=-1, keepdims=True)
    carry[...] = prior[tm - 1:tm, :] + picks[tm - 1:tm, :]

    out = jnp.zeros_like(logits)
    for k, v in enumerate((e_a, e_b, g1, g2, rank_a, rank_b)):
        out = jnp.where(lane == float(k), v, out)
    rt_ref[0:rows, :] = out
    fields[:, pl.ds(col0, rows)] = out.T[0:SUBLANES, :]


def _mix_route(prompt, sample, consts, tm, n_tiles):
    n_prompt, n_sample = prompt[0].shape[0], sample[0].shape[0]
    assert n_prompt % tm == 0 and n_sample <= tm
    n_blocks = n_prompt // tm
    total_rows = n_prompt + n_sample
    row = lambda w: pl.BlockSpec((tm, w), lambda i: (jnp.minimum(i, n_blocks - 1), 0))
    out_row = lambda w: pl.BlockSpec((tm, w), lambda i: (i, 0))
    assert total_rows % LANES == 0 and n_tiles <= 2 * LANES
    whole = lambda shape: pl.BlockSpec(shape, lambda i: (0, 0))
    return pl.pallas_call(
        functools.partial(_mix_route_body, n_blocks, n_tiles * MOE_TILE),
        grid=(n_blocks + 1,),
        in_specs=([row(D_MODEL), row(SSD_WIDTH), row(S5_WIDTH)] + [_full_spec(a) for a in sample]
                  + [_full_spec(a) for a in consts]),
        out_specs=[out_row(D_MODEL), pl.BlockSpec(memory_space=pl.ANY), out_row(LANES),
                   whole((SUBLANES, total_rows)), whole((SUBLANES, 2 * LANES))],
        out_shape=[jax.ShapeDtypeStruct((total_rows, D_MODEL), F32),
                   jax.ShapeDtypeStruct((total_rows, SLAB_ROWS, LANES), F32),
                   jax.ShapeDtypeStruct((total_rows, LANES), F32),
                   jax.ShapeDtypeStruct((SUBLANES, total_rows), jnp.int32),
                   jax.ShapeDtypeStruct((SUBLANES, 2 * LANES), jnp.int32)],
        scratch_shapes=[pltpu.VMEM((1, LANES), F32), pltpu.VMEM((SUBLANES, total_rows), F32),
                        pltpu.VMEM((2, tm, D_MODEL), F32), pltpu.SemaphoreType.DMA((2, SLAB_ROWS))],
        compiler_params=pltpu.CompilerParams(dimension_semantics=("arbitrary",), vmem_limit_bytes=VMEM_LIMIT),
        name="mix_route",
    )(*prompt, *sample, *consts)


def _sc_mesh():
    return plsc.VectorSubcoreMesh(core_axis_name="c", subcore_axis_name="s")


def _sc_worker():
    return lax.axis_index("s") * SC_CORES + lax.axis_index("c")


def _sc_dispatch(xn, pos_a, pos_b, n_rows):
    n_tok = xn.shape[0]
    ch = SC_DISPATCH_ROWS
    n_chunks = n_tok // ch
    assert n_tok % ch == 0 and n_chunks >= SC_WORKERS
    max_mine = -(-n_chunks // SC_WORKERS)
    stage = [pltpu.VMEM((ch,), jnp.int32), pltpu.VMEM((ch,), jnp.int32), pltpu.VMEM((ch, SLAB_ROWS, LANES), F32),
             pltpu.SemaphoreType.DMA]

    @functools.partial(
        pl.kernel, mesh=_sc_mesh(),
        out_type=jax.ShapeDtypeStruct((n_rows, SLAB_ROWS, LANES), F32),
        scratch_types=stage + stage + [pltpu.SemaphoreType.DMA])
    def push(xn_hbm, pa_hbm, pb_hbm, xs_hbm, ia0, ib0, rows0, lsem0, ia1, ib1, rows1, lsem1, ssem):
        wid = _sc_worker()
        mine = (n_chunks - wid + SC_WORKERS - 1) // SC_WORKERS
        bufs = ((ia0, ib0, rows0, lsem0), (ia1, ib1, rows1, lsem1))

        def loads(t, b):
            ia, ib, rows, sem = bufs[b]
            off = pl.multiple_of((wid + t * SC_WORKERS) * ch, ch)
            return (pltpu.make_async_copy(pa_hbm.at[pl.ds(off, ch)], ia, sem),
                    pltpu.make_async_copy(pb_hbm.at[pl.ds(off, ch)], ib, sem),
                    pltpu.make_async_copy(xn_hbm.at[pl.ds(off, ch)], rows, sem))

        def stage_in(t, b):
            for c in loads(t, b):
                c.start()

        def scatter(t, b):
            ia, ib, rows, _ = bufs[b]
            for c in loads(t, b):
                c.wait()
            first = pltpu.async_copy(rows, xs_hbm.at[ia], ssem)
            second = pltpu.async_copy(rows, xs_hbm.at[ib], ssem)
            first.wait()
            second.wait()

        stage_in(0, 0)

        @pl.loop(0, (max_mine + 1) // 2)
        def _(p):
            t = 2 * p

            @pl.when(t + 1 < mine)
            def _():
                stage_in(t + 1, 1)

            @pl.when(t < mine)
            def _():
                scatter(t, 0)

            @pl.when(t + 2 < mine)
            def _():
                stage_in(t + 2, 0)

            @pl.when(t + 1 < mine)
            def _():
                scatter(t + 1, 1)

    return push(xn, pos_a, pos_b)


def _sc_collect(ysorted, pos_flat, ch):
    n_pick = pos_flat.shape[0]
    per_worker = n_pick // SC_WORKERS
    n_chunks = per_worker // ch
    assert n_pick % SC_WORKERS == 0 and per_worker % ch == 0

    @functools.partial(
        pl.kernel, mesh=_sc_mesh(),
        out_type=jax.ShapeDtypeStruct((n_pick, SLAB_ROWS, LANES), F32),
        scratch_types=[pltpu.VMEM((ch,), jnp.int32), pltpu.VMEM((ch,), jnp.int32),
                       pltpu.VMEM((ch, SLAB_ROWS, LANES), F32), pltpu.VMEM((ch, SLAB_ROWS, LANES), F32),
                       pltpu.SemaphoreType.DMA, pltpu.SemaphoreType.DMA])
    def pull(ys_hbm, pos_hbm, out_hbm, idx0, idx1, rows0, rows1, sem0, sem1):
        base = _sc_worker() * per_worker
        bufs = ((idx0, rows0, sem0), (idx1, rows1, sem1))

        def offset(j):
            return pl.multiple_of(base + j * ch, SUBLANES)

        def fetch(j, b):
            idx, rows, sem = bufs[b]
            pltpu.sync_copy(pos_hbm.at[pl.ds(offset(j), ch)], idx)
            pltpu.async_copy(ys_hbm.at[idx], rows, sem)

        def flush(j, b):
            idx, rows, sem = bufs[b]
            pltpu.make_async_copy(ys_hbm.at[idx], rows, sem).wait()
            pltpu.sync_copy(rows, out_hbm.at[pl.ds(offset(j), ch)])

        fetch(0, 0)

        @pl.loop(0, n_chunks // 2)
        def _(p):
            j = 2 * p
            fetch(j + 1, 1)
            flush(j, 0)

            @pl.when(j + 2 < n_chunks)
            def _():
                fetch(j + 2, 0)

            flush(j + 1, 1)

        if n_chunks % 2:
            flush(n_chunks - 1, 0)

    return pull(ysorted, pos_flat)


def _slab_columns(ref, rows, j):
    return ref[pl.ds(j, rows, stride=SLAB_ROWS), :]


def _moe_ffn_body(te_ref, nused_ref, x_ref, wg_ref, wu_ref, wd_ref, y_ref, wgb, wub, wdb):
    i = pl.program_id(0)

    @pl.when(i >= nused_ref[0])
    def _unused_tile():
        y_ref[...] = jnp.zeros_like(y_ref)

    @pl.when(i < nused_ref[0])
    def _tile():
        @pl.when((i == 0) | (te_ref[i] != te_ref[jnp.maximum(i - 1, 0)]))
        def _cast_weights():
            wgb[...] = wg_ref[0].astype(BF16)
            wub[...] = wu_ref[0].astype(BF16)
            wdb[...] = wd_ref[0].astype(BF16)

        x = jnp.concatenate([_slab_columns(x_ref, MOE_TILE, j) for j in range(SLAB_ROWS)], axis=-1).astype(BF16)
        gate = _dot(x, wgb[...])
        hmid = (gate * jax.nn.sigmoid(gate)) * _dot(x, wub[...])
        y = _dot(hmid.astype(BF16), wdb[...])
        for j in range(SLAB_ROWS):
            y_ref[pl.ds(j, MOE_TILE, stride=SLAB_ROWS), :] = y[:, j * LANES:(j + 1) * LANES]


def _moe_ffn(tile_expert, n_used, xsorted, w_gate, w_up, w_down):
    n_tiles = tile_expert.shape[0]
    wspec = lambda s: pl.BlockSpec((1,) + s, lambda i, te, nu: (te[i], 0, 0))
    tile = lambda imap: pl.BlockSpec((MOE_TILE * SLAB_ROWS, LANES), imap)
    return pl.pallas_call(
        _moe_ffn_body,
        grid_spec=pltpu.PrefetchScalarGridSpec(
            num_scalar_prefetch=2,
            grid=(n_tiles,),
            in_specs=[tile(lambda i, te, nu: (jnp.clip(i, 0, jnp.maximum(nu[0] - 1, 0)), 0)),
                      wspec((D_MODEL, MOE_D_FF)), wspec((D_MODEL, MOE_D_FF)), wspec((MOE_D_FF, D_MODEL))],
            out_specs=tile(lambda i, te, nu: (i, 0)),
            scratch_shapes=[pltpu.VMEM((D_MODEL, MOE_D_FF), BF16), pltpu.VMEM((D_MODEL, MOE_D_FF), BF16),
                            pltpu.VMEM((MOE_D_FF, D_MODEL), BF16)]),
        out_shape=jax.ShapeDtypeStruct(xsorted.shape, F32),
        compiler_params=pltpu.CompilerParams(dimension_semantics=("arbitrary",), vmem_limit_bytes=VMEM_LIMIT),
        name="moe_ffn",
    )(tile_expert, n_used, xsorted, w_gate, w_up, w_down)


def _combine_body(x1_ref, rt_ref, ya_ref, yb_ref, nf_ref, *rest):
    out_ref = rest[-1]
    rt = rt_ref[...]
    x1 = x1_ref[...]
    tm = x1.shape[0]
    ya, yb = ya_ref.at[0], yb_ref.at[0]
    x2 = jnp.concatenate(
        [x1[:, j * LANES:(j + 1) * LANES] + rt[:, 2:3] * _slab_columns(ya, tm, j) + rt[:, 3:4] * _slab_columns(yb, tm, j)
         for j in range(SLAB_ROWS)], axis=-1)
    out_ref[...] = _rms(x2, nf_ref[...])


def _combine(x1, rt, y_picks, nf, tm, rows, x_block, y_block, out_rows, out_block, out_buf=None):
    row = lambda w: pl.BlockSpec((tm, w), lambda i: (i + x_block, 0))
    pick = lambda k: pl.BlockSpec((1, tm * SLAB_ROWS, LANES), lambda i: (k, i + y_block, 0))
    in_specs = [row(D_MODEL), row(LANES), pick(0), pick(1), pl.BlockSpec((1, D_MODEL), lambda i: (0, 0))]
    args = [x1, rt, y_picks, y_picks, nf]
    aliases = {}
    if out_buf is not None:
        in_specs.append(pl.BlockSpec(memory_space=pl.ANY))
        aliases[len(args)] = 0
        args.append(out_buf)
    return pl.pallas_call(
        _combine_body,
        grid=(rows // tm,),
        in_specs=in_specs,
        out_specs=pl.BlockSpec((tm, D_MODEL), lambda i: (i + out_block, 0)),
        out_shape=jax.ShapeDtypeStruct((out_rows, D_MODEL), F32),
        input_output_aliases=aliases,
        compiler_params=pltpu.CompilerParams(dimension_semantics=("parallel",), vmem_limit_bytes=VMEM_LIMIT),
        name="moe_combine",
    )(*args)


def _s5_tables(a_re, a_im, log_dt, b_re, b_im, c_re, c_im):
    dt = jnp.exp(log_dt)[:, None]
    mag = jnp.exp(a_re * dt)
    ab_re = mag * jnp.cos(a_im * dt)
    ab_im = mag * jnp.sin(a_im * dt)
    den = a_re * a_re + a_im * a_im
    nr = ab_re - 1.0
    q_re = (nr * a_re + ab_im * a_im) / den
    q_im = (ab_im * a_re - nr * a_im) / den
    bb_re = q_re[..., None] * b_re - q_im[..., None] * b_im
    bb_im = q_re[..., None] * b_im + q_im[..., None] * b_re
    nblk = S5_GROUPS // 16
    kw, nw = 16 * S5_GROUP_CH, 16 * S5_STATE
    same_group = (jnp.arange(kw)[:, None] // S5_GROUP_CH) == (jnp.arange(nw)[None, :] // S5_STATE)

    def in_map(bb):
        rows = bb.reshape(nblk, 16, S5_STATE, S5_GROUP_CH).transpose(0, 1, 3, 2).reshape(nblk, kw, S5_STATE)
        return jnp.where(same_group, jnp.tile(rows, (1, 1, 16)), 0.0)

    def out_map(cc):
        cols = cc.reshape(nblk, 16, S5_GROUP_CH, S5_STATE).transpose(0, 3, 1, 2).reshape(nblk, S5_STATE, kw)
        return jnp.where(same_group.T, jnp.tile(cols, (1, 16, 1)), 0.0)

    wb = jnp.concatenate([in_map(bb_re), in_map(bb_im)], axis=-1).astype(BF16)
    return (wb, ab_re.reshape(1, S5_LANES), ab_im.reshape(1, S5_LANES),
            out_map(c_re).astype(BF16), out_map(-c_im).astype(BF16))


def kernel(x_prompt, x_sample, state_ssd_conv, state_ssd_ssm, state_s5_re, state_s5_im, meta_tokens, norm_mix, w_in, conv_w, conv_b, dt_bias, a_log, d_ssd, ssd_norm, s5_a_re, s5_a_im, s5_log_dt, s5_b_re, s5_b_im, s5_c_re, s5_c_im, s5_d, w_glu, b_glu, s5_norm, w_out, norm_ffn, router_coarse_w, router_coarse_b, router_fine_w, router_fine_b, w_gate, w_up, w_down, norm_final):
    bp, seq, _ = x_prompt.shape
    bs = x_sample.shape[0]
    n_prompt = bp * seq
    n_tok = n_prompt + bs
    row2 = lambda v: v.reshape(1, -1)
    pad_heads = lambda v: jnp.pad(v, (0, LANES - SSD_HEADS)).reshape(1, LANES)

    w = w_in[0]
    o1, o2, o3 = SSD_WIDTH, SSD_WIDTH + SSD_CONV_DIM, SSD_WIDTH + SSD_CONV_DIM + SSD_HEADS
    wz, wx, wu = w[:, :o1].astype(BF16), w[:, o1:o2].astype(BF16), w[:, o3:].astype(BF16)
    wdt = jnp.pad(w[:, o2:o3], ((0, 0), (0, LANES - SSD_HEADS))).astype(BF16)
    g_mix = row2(norm_mix[0])
    cw, cb = conv_w[0], row2(conv_b[0])
    dtb, alog = pad_heads(dt_bias[0]), pad_heads(a_log[0])
    dexp = row2(jnp.repeat(d_ssd[0], SSD_HEAD_DIM))
    snrm = row2(ssd_norm[0])
    eexp = (jnp.arange(LANES)[:, None] == (jnp.arange(SSD_WIDTH) // SSD_HEAD_DIM)[None, :]).astype(BF16)
    wb5, ab_re, ab_im, wcr, wci = _s5_tables(s5_a_re[0], s5_a_im[0], s5_log_dt[0], s5_b_re[0], s5_b_im[0],
                                             s5_c_re[0], s5_c_im[0])
    d5, wglu, bglu, nrm5 = row2(s5_d[0]), w_glu[0].astype(BF16), row2(b_glu[0]), row2(s5_norm[0])
    wo_a, wo_b = w_out[0][:SSD_WIDTH].astype(BF16), w_out[0][SSD_WIDTH:].astype(BF16)
    w_r = jnp.concatenate([router_coarse_w[0], router_fine_w[0].transpose(1, 0, 2).reshape(D_MODEL, MOE_EXPERTS)], axis=1)
    w_r = jnp.pad(w_r, ((0, 0), (0, LANES - w_r.shape[1])))
    wrh = w_r.astype(BF16)
    wrl = (w_r - wrh.astype(F32)).astype(BF16)
    b_r = jnp.concatenate([router_coarse_b[0], router_fine_b[0].reshape(-1)])
    b_r = jnp.pad(b_r, (0, LANES - b_r.shape[0])).reshape(1, LANES)

    zp, xbcp, dtp, up = _in_proj(x_prompt.reshape(n_prompt, D_MODEL), g_mix, wz, wx, wdt, wu, TOK_TILE, BF16, F32)
    xsm = jnp.concatenate([x_sample.reshape(bs, D_MODEL), meta_tokens], axis=0)
    zs, xbcs, dts, us = _in_proj(xsm, g_mix, wz, wx, wdt, wu, xsm.shape[0], F32, F32)

    front = SSD_CHUNK - N_META
    padf = lambda a: jnp.pad(a[bs:], ((front, 0), (0, 0)))[None]
    gw = SSD_HPG * SSD_HEAD_DIM
    ssd_consts = (cw, cb, dtb, alog, dexp, snrm, eexp)
    _, ctail_m, _, ht_m = _ssd_chunked(
        padf(xbcs).astype(BF16), padf(dts), jnp.zeros((1, SSD_CHUNK, SSD_WIDTH), F32),
        jnp.zeros((1, SUBLANES, SSD_CONV_DIM), F32), jnp.zeros((1, SSD_GROUPS, SSD_STATE, gw), F32),
        *ssd_consts, mask_rows=front)
    y_ssd_p, ctail_p, ssm_p, _ = _ssd_chunked(
        xbcp.reshape(bp, seq, SSD_CONV_DIM), dtp.reshape(bp, seq, LANES), zp.reshape(bp, seq, SSD_WIDTH),
        ctail_m, ht_m, *ssd_consts, mask_rows=0)

    abr8, abi8 = jnp.broadcast_to(ab_re, (bp, S5_LANES)), jnp.broadcast_to(ab_im, (bp, S5_LANES))
    um8 = jnp.repeat(us[bs:], bp, axis=0).astype(BF16)
    y_s5_p, s5re_p, s5im_p = _s5_seq(up.reshape(bp, seq, S5_WIDTH), um8, wb5, abr8, abi8,
                                     wcr, wci, d5, wglu, bglu, nrm5)

    cst = state_ssd_conv[0]
    xt_s, dt_s, dec_s, bc, xs_s = _ssd_step_prep(xbcs[:bs], cst[:, 0], cst[:, 1], cst[:, 2], dts[:bs],
                                                 cw, cb, dtb, alog)
    ssm_s, y_core = _ssd_step(dt_s[:, :SSD_HEADS].reshape(-1), dec_s[:, :SSD_HEADS].reshape(-1),
                              state_ssd_ssm[0], xt_s, bc)
    y_ssd_s, y_s5_s, s5re_s, s5im_s = _sample_post(
        y_core, xs_s, zs[:bs], dexp, snrm, us[:bs], state_s5_re[0].reshape(bs, S5_LANES),
        state_s5_im[0].reshape(bs, S5_LANES), wb5, ab_re, ab_im, wcr, wci, d5, wglu, bglu, nrm5)

    route_consts = (wo_a, wo_b, row2(norm_ffn[0]), wrh, wrl, b_r)
    n_tiles = -(-2 * n_tok // MOE_TILE) + MOE_EXPERTS
    x1, xn, rt, pos, meta = _mix_route(
        (x_prompt.reshape(n_prompt, D_MODEL), y_ssd_p.reshape(n_prompt, SSD_WIDTH), y_s5_p.reshape(n_prompt, S5_WIDTH)),
        (x_sample.reshape(bs, D_MODEL), y_ssd_s, y_s5_s), route_consts, TOK_TILE, n_tiles)

    pos_a, pos_b = pos[0], pos[1]
    tile_expert, n_used = meta[0, :n_tiles], meta[1, :1]
    slabs = lambda a: a.reshape(-1, SLAB_ROWS, LANES)
    xsorted = _sc_dispatch(xn, pos_a, pos_b, n_tiles * MOE_TILE)
    ysorted = _moe_ffn(tile_expert, n_used, xsorted.reshape(-1, LANES), w_gate[0], w_up[0], w_down[0])
    nfin = row2(norm_final)

    half = n_prompt // 2

    def collect(lo, hi, ch):
        picks = jnp.concatenate([pos_a[lo:hi], pos_b[lo:hi]])
        return _sc_collect(slabs(ysorted), picks, ch).reshape(2, (hi - lo) * SLAB_ROWS, LANES)

    picks_1 = collect(0, half, SC_COLLECT_ROWS[0])
    picks_2 = collect(half, n_tok, SC_COLLECT_ROWS[1])
    blocks = half // MOE_TILE
    y_p = _combine(x1, rt, picks_1, nfin, MOE_TILE, half, 0, 0, n_prompt, 0)
    y_p = _combine(x1, rt, picks_2, nfin, MOE_TILE, half, blocks, 0, n_prompt, blocks, out_buf=y_p)
    y_s = _combine(x1, rt, picks_2, nfin, bs, bs, n_prompt // bs, half // bs, bs, 0)

    s5_state = lambda a, b: a.reshape(1, b, S5_GROUPS, S5_STATE)
    new_conv_s = jnp.stack([cst[:, 1], cst[:, 2], xbcs[:bs]], axis=1)[None]
    return (y_p.reshape(bp, seq, D_MODEL), y_s.reshape(bs, 1, D_MODEL),
            ctail_p[:, SUBLANES - (SSD_CONV - 1):][None], ssm_p[None], s5_state(s5re_p, bp), s5_state(s5im_p, bp),
            new_conv_s, ssm_s[None], s5_state(s5re_s, bs), s5_state(s5im_s, bs))
```

```python
import functools

import jax
import jax.numpy as jnp
from jax import lax
from jax.experimental import pallas as pl
from jax.experimental.pallas import tpu as pltpu
from jax.experimental.pallas import tpu_sc as plsc

F32, BF16 = jnp.float32, jnp.bfloat16

D_MODEL = 1024
N_META = 16
SSD_WIDTH = 1024
SSD_HEAD_DIM = 64
SSD_HEADS = 16
SSD_GROUPS = 2
SSD_HPG = SSD_HEADS // SSD_GROUPS
SSD_STATE = 128
SSD_CONV = 4
SSD_CHUNK = 128
SSD_CONV_DIM = SSD_WIDTH + 2 * SSD_GROUPS * SSD_STATE
S5_WIDTH = 1024
S5_GROUP_CH = 16
S5_GROUPS = 64
S5_STATE = 64
S5_LANES = S5_GROUPS * S5_STATE
MOE_GROUPS = 4
MOE_EPG = 8
MOE_EXPERTS = MOE_GROUPS * MOE_EPG
MOE_D_FF = 512
EPS = 1e-6

LANES = 128
SUBLANES = 8
VMEM_LIMIT = 56 * 1024 * 1024

S5_TIME_TILE = 64
S5_SCAN_LANES = 512
MOE_TILE = 256
SLAB_ROWS = D_MODEL // LANES
PACK_ROWS = SLAB_ROWS // 2
SC_CORES = 2
SC_SUBCORES = 16
SC_WORKERS = SC_CORES * SC_SUBCORES
SC_DISPATCH_ROWS = 32
SC_COLLECT_ROWS = (32, 40)
TOK_TILE = 512


def _dot(a, b):
    return jnp.dot(a, b, preferred_element_type=F32)


def _rms(x, g):
    return x * lax.rsqrt(jnp.mean(x * x, axis=-1, keepdims=True) + EPS) * g


def _softplus(x):
    return jnp.maximum(x, 0.0) + jnp.log1p(jnp.exp(-jnp.abs(x)))


def _split3(x):
    hi = x.astype(BF16)
    r = x - hi.astype(F32)
    mid = r.astype(BF16)
    lo = (r - mid.astype(F32)).astype(BF16)
    return hi, mid, lo


def _dot3(x, w):
    hi, mid, lo = _split3(x)
    return _dot(hi, w) + _dot(mid, w) + _dot(lo, w)


def _dot3_left(w, x):
    hi, mid, lo = _split3(x)
    return _dot(w, hi) + _dot(w, mid) + _dot(w, lo)


def _full_spec(a):
    nd = a.ndim
    return pl.BlockSpec(a.shape, lambda *_: (0,) * nd)


def _resident_spec(a):
    nd = a.ndim
    return pl.BlockSpec(a.shape, lambda *_: (0,) * nd, pipeline_mode=pl.Buffered(1))


def _in_proj_body(x_ref, g_ref, wz_ref, wx_ref, wdt_ref, wu_ref, z_ref, xbc_ref, dt_ref, u_ref):
    xb = _rms(x_ref[...], g_ref[...]).astype(BF16)
    z_ref[...] = _dot(xb, wz_ref[...]).astype(z_ref.dtype)
    xbc_ref[...] = _dot(xb, wx_ref[...]).astype(xbc_ref.dtype)
    dt_ref[...] = _dot(xb, wdt_ref[...])
    u_ref[...] = _dot(xb, wu_ref[...]).astype(u_ref.dtype)


def _in_proj(x2d, g, wz, wx, wdt, wu, tm, act_dtype, u_dtype):
    rows = x2d.shape[0]
    row = lambda w: pl.BlockSpec((tm, w), lambda i: (i, 0))
    return pl.pallas_call(
        _in_proj_body,
        grid=(rows // tm,),
        in_specs=[row(D_MODEL), _full_spec(g), _full_spec(wz), _full_spec(wx), _full_spec(wdt), _full_spec(wu)],
        out_specs=[row(SSD_WIDTH), row(SSD_CONV_DIM), row(LANES), row(S5_WIDTH)],
        out_shape=[jax.ShapeDtypeStruct((rows, SSD_WIDTH), act_dtype),
                   jax.ShapeDtypeStruct((rows, SSD_CONV_DIM), act_dtype),
                   jax.ShapeDtypeStruct((rows, LANES), F32),
                   jax.ShapeDtypeStruct((rows, S5_WIDTH), u_dtype)],
        compiler_params=pltpu.CompilerParams(dimension_semantics=("parallel",), vmem_limit_bytes=VMEM_LIMIT),
        name="in_proj",
    )(x2d, g, wz, wx, wdt, wu)


def _ssd_body(mask_rows, xbc_ref, dt_ref, z_ref, cinit_ref, hinit_ref, cw_ref, cb_ref, dtb_ref, alog_ref,
              dexp_ref, nrm_ref, eexp_ref, y_ref, ctail_ref, st_ref, hto_ref, xwin, hT):
    c = pl.program_id(1)
    L = SSD_CHUNK

    @pl.when(c == 0)
    def _init():
        xwin[...] = cinit_ref[0]
        hT[...] = hinit_ref[0]

    x_b = xbc_ref[0]
    x_f = x_b.astype(F32)
    taps = SSD_CONV - 1
    m_i = lax.broadcasted_iota(jnp.int32, (taps * L, L), 0)
    r_i = lax.broadcasted_iota(jnp.int32, (taps * L, L), 1)
    shift = (r_i + (taps - m_i // L) == m_i % L).astype(BF16)
    shifted = _dot(shift, x_b)
    acc = cb_ref[...] + x_f * cw_ref[taps:taps + 1, :]
    for k in range(taps):
        acc = acc + shifted[k * L:(k + 1) * L, :] * cw_ref[k:k + 1, :]
    joint = jnp.concatenate([xwin[...], x_f[0:SUBLANES, :]], axis=0)
    row8 = lax.broadcasted_iota(jnp.int32, (SUBLANES, 1), 0)
    head = acc[0:SUBLANES, :]
    for k in range(taps):
        d = taps - k
        head = head + jnp.where(row8 < d, joint[SUBLANES - d:2 * SUBLANES - d, :], 0.0) * cw_ref[k:k + 1, :]
    acc = jnp.concatenate([head, acc[SUBLANES:, :]], axis=0)
    tail = x_f[L - SUBLANES:, :]
    xwin[...] = tail
    ctail_ref[0] = tail

    xact = acc * jax.nn.sigmoid(acc)
    dt = _softplus(dt_ref[0] + dtb_ref[...])
    if mask_rows:
        valid = lax.broadcasted_iota(jnp.int32, (L, 1), 0) >= mask_rows
        xact = jnp.where(valid, xact, 0.0)
        dt = jnp.where(valid, dt, 0.0)

    a_neg = -jnp.exp(alog_ref[...])
    dA = dt * a_neg
    row_i = lax.broadcasted_iota(jnp.int32, (L, L), 0)
    col_i = lax.broadcasted_iota(jnp.int32, (L, L), 1)
    causal = row_i >= col_i
    tril = causal.astype(BF16)
    cs = _dot3_left(tril, dA)
    csT = cs.T
    dtT = dt.T
    ecs = jnp.exp(cs)
    wdec = jnp.exp(cs[L - 1:L, :] - cs) * dt
    eexp = eexp_ref[...]
    ecs_e = _dot3(ecs, eexp)
    wdec_e = _dot3(wdec, eexp)
    lane = lax.broadcasted_iota(jnp.int32, (L, LANES), 1)
    first_half = lane < SSD_HEAD_DIM

    gw = SSD_HPG * SSD_HEAD_DIM
    y_groups = []
    for g in range(SSD_GROUPS):
        b_g = xact[:, SSD_WIDTH + g * SSD_STATE: SSD_WIDTH + (g + 1) * SSD_STATE]
        c_g = xact[:, SSD_WIDTH + (SSD_GROUPS + g) * SSD_STATE: SSD_WIDTH + (SSD_GROUPS + g + 1) * SSD_STATE]
        b_b = b_g.astype(BF16)
        c_b = c_g.astype(BF16)
        cb = lax.dot_general(c_b, b_b, (((1,), (1,)), ((), ())), preferred_element_type=F32)
        xs_g = xact[:, g * gw:(g + 1) * gw]
        h_prev = hT[g]
        y_off = _dot(c_b, h_prev.astype(BF16)) * ecs_e[:, g * gw:(g + 1) * gw]
        xdec = (xs_g * wdec_e[:, g * gw:(g + 1) * gw]).astype(BF16)
        hT[g] = h_prev * ecs_e[L - 1:L, g * gw:(g + 1) * gw] + _dot(b_g.T.astype(BF16), xdec)
        pieces = []
        for j in range(SSD_HPG // 2):
            xs_pair = xs_g[:, j * LANES:(j + 1) * LANES]
            halves = (jnp.where(first_half, xs_pair, 0.0).astype(BF16),
                      jnp.where(first_half, 0.0, xs_pair).astype(BF16))
            yd = None
            for t in range(2):
                h = g * SSD_HPG + 2 * j + t
                seg = cs[:, h:h + 1] - csT[h:h + 1, :]
                lmat = jnp.exp(jnp.where(causal, seg, -jnp.inf))
                m = (cb * lmat * dtT[h:h + 1, :]).astype(BF16)
                part = _dot(m, halves[t])
                yd = part if yd is None else yd + part
            pieces.append(yd)
        y_groups.append(jnp.concatenate(pieces, axis=-1) + y_off + dexp_ref[:, g * gw:(g + 1) * gw] * xs_g)
    y = jnp.concatenate(y_groups, axis=-1)
    z = z_ref[0].astype(F32)
    y_ref[0] = _rms(y * (z * jax.nn.sigmoid(z)), nrm_ref[...]).astype(y_ref.dtype)

    @pl.when(c == pl.num_programs(1) - 1)
    def _emit():
        hto_ref[0] = hT[...]
        for g in range(SSD_GROUPS):
            t = hT[g].T
            for k in range(SSD_HPG):
                st_ref[0, g * SSD_HPG + k] = t[k * SSD_HEAD_DIM:(k + 1) * SSD_HEAD_DIM, :]


def _ssd_chunked(xbc, dt, z, cinit, hinit, cw, cb, dtb, alog, dexp, nrm, eexp, mask_rows):
    bsz, seq, _ = xbc.shape
    nc = seq // SSD_CHUNK
    gw = SSD_HPG * SSD_HEAD_DIM
    blk = lambda w: pl.BlockSpec((1, SSD_CHUNK, w), lambda b, c: (b, c, 0))
    return pl.pallas_call(
        functools.partial(_ssd_body, mask_rows),
        grid=(bsz, nc),
        in_specs=[blk(SSD_CONV_DIM), blk(LANES), blk(SSD_WIDTH),
                  pl.BlockSpec((1, SUBLANES, SSD_CONV_DIM), lambda b, c: (0, 0, 0)),
                  pl.BlockSpec((1, SSD_GROUPS, SSD_STATE, gw), lambda b, c: (0, 0, 0, 0)),
                  _full_spec(cw), _full_spec(cb), _full_spec(dtb), _full_spec(alog),
                  _full_spec(dexp), _full_spec(nrm), _full_spec(eexp)],
        out_specs=[blk(SSD_WIDTH),
                   pl.BlockSpec((1, SUBLANES, SSD_CONV_DIM), lambda b, c: (b, 0, 0)),
                   pl.BlockSpec((1, SSD_HEADS, SSD_HEAD_DIM, SSD_STATE), lambda b, c: (b, 0, 0, 0)),
                   pl.BlockSpec((1, SSD_GROUPS, SSD_STATE, gw), lambda b, c: (b, 0, 0, 0))],
        out_shape=[jax.ShapeDtypeStruct((bsz, seq, SSD_WIDTH), BF16),
                   jax.ShapeDtypeStruct((bsz, SUBLANES, SSD_CONV_DIM), F32),
                   jax.ShapeDtypeStruct((bsz, SSD_HEADS, SSD_HEAD_DIM, SSD_STATE), F32),
                   jax.ShapeDtypeStruct((bsz, SSD_GROUPS, SSD_STATE, gw), F32)],
        scratch_shapes=[pltpu.VMEM((SUBLANES, SSD_CONV_DIM), F32),
                        pltpu.VMEM((SSD_GROUPS, SSD_STATE, gw), F32)],
        compiler_params=pltpu.CompilerParams(dimension_semantics=("parallel", "arbitrary"),
                                             vmem_limit_bytes=VMEM_LIMIT),
        name="ssd_chunked",
    )(xbc, dt, z, cinit, hinit, cw, cb, dtb, alog, dexp, nrm, eexp)


def _ssd_step_prep_body(xbc_ref, c0_ref, c1_ref, c2_ref, dt_ref, cw_ref, cb_ref, dtb_ref, alog_ref,
                        xt_ref, dt_out_ref, dec_ref, bc_ref, xs_ref):
    acc = cb_ref[...]
    for k, r in enumerate((c0_ref, c1_ref, c2_ref, xbc_ref)):
        acc = acc + r[...] * cw_ref[k:k + 1, :]
    xact = acc * jax.nn.sigmoid(acc)
    xs = xact[:, :SSD_WIDTH]
    dt = _softplus(dt_ref[...] + dtb_ref[...])
    dt_out_ref[...] = dt
    dec_ref[...] = jnp.exp(dt * -jnp.exp(alog_ref[...]))
    bc_ref[...] = xact[:, SSD_WIDTH:]
    xs_ref[...] = xs
    xt_ref[...] = xs.T.astype(xt_ref.dtype)


def _ssd_step_prep(xbc, c0, c1, c2, dt, cw, cb, dtb, alog):
    n = xbc.shape[0]
    args = (xbc, c0, c1, c2, dt, cw, cb, dtb, alog)
    spec = lambda r, w: pl.BlockSpec((r, w), lambda: (0, 0))
    return pl.pallas_call(
        _ssd_step_prep_body,
        in_specs=[_full_spec(a) for a in args],
        out_specs=[spec(SSD_WIDTH, n), spec(n, LANES), spec(n, LANES), spec(n, 2 * SSD_GROUPS * SSD_STATE),
                   spec(n, SSD_WIDTH)],
        out_shape=[jax.ShapeDtypeStruct((SSD_WIDTH, n), BF16), jax.ShapeDtypeStruct((n, LANES), F32),
                   jax.ShapeDtypeStruct((n, LANES), F32),
                   jax.ShapeDtypeStruct((n, 2 * SSD_GROUPS * SSD_STATE), F32),
                   jax.ShapeDtypeStruct((n, SSD_WIDTH), F32)],
        compiler_params=pltpu.CompilerParams(vmem_limit_bytes=VMEM_LIMIT),
        name="ssd_step_prep",
    )(*args)


def _ssd_step_body(dt_ref, dec_ref, st_ref, xt_ref, bc_ref, so_ref, y_ref):
    n = xt_ref.shape[1]
    gw = SSD_HPG * SSD_HEAD_DIM
    blk = pl.program_id(0)
    seq_id = lax.broadcasted_iota(jnp.int32, (n, SSD_STATE), 0)
    sub_id = lax.broadcasted_iota(jnp.int32, (SUBLANES, gw), 0)
    base = pl.multiple_of(blk * SUBLANES, SUBLANES)
    y_acc = [jnp.zeros((SUBLANES, gw), F32) for _ in range(SSD_GROUPS)]
    for i in range(SUBLANES):
        s = blk * SUBLANES + i
        for g in range(SSD_GROUPS):
            b_all = bc_ref[:, g * SSD_STATE:(g + 1) * SSD_STATE]
            rhs = jnp.where(seq_id == s, b_all, 0.0).astype(BF16)
            outer = _dot(xt_ref[g * gw:(g + 1) * gw, :], rhs)
            news = []
            for k in range(SSD_HPG):
                h = g * SSD_HPG + k
                new = (dec_ref[s * SSD_HEADS + h] * st_ref[i, h]
                       + dt_ref[s * SSD_HEADS + h] * outer[k * SSD_HEAD_DIM:(k + 1) * SSD_HEAD_DIM, :])
                so_ref[i, h] = new
                news.append(new)
            new_g = jnp.concatenate(news, axis=0).astype(BF16)
            c_lo = (SSD_GROUPS + g) * SSD_STATE
            c_blk = bc_ref[pl.ds(base, SUBLANES), c_lo:c_lo + SSD_STATE].astype(BF16)
            r = lax.dot_general(c_blk, new_g, (((1,), (1,)), ((), ())), preferred_element_type=F32)
            y_acc[g] = y_acc[g] + jnp.where(sub_id == i, r, 0.0)
    y_ref[...] = jnp.concatenate(y_acc, axis=-1)


def _ssd_step(dt_flat, dec_flat, state, xt, bc):
    n = state.shape[0]
    st_spec = pl.BlockSpec((SUBLANES, SSD_HEADS, SSD_HEAD_DIM, SSD_STATE), lambda i, *_: (i, 0, 0, 0))
    return pl.pallas_call(
        _ssd_step_body,
        grid_spec=pltpu.PrefetchScalarGridSpec(
            num_scalar_prefetch=2,
            grid=(n // SUBLANES,),
            in_specs=[st_spec, pl.BlockSpec(xt.shape, lambda i, *_: (0, 0)),
                      pl.BlockSpec(bc.shape, lambda i, *_: (0, 0))],
            out_specs=[st_spec, pl.BlockSpec((SUBLANES, SSD_WIDTH), lambda i, *_: (i, 0))]),
        out_shape=[jax.ShapeDtypeStruct(state.shape, F32), jax.ShapeDtypeStruct((n, SSD_WIDTH), F32)],
        compiler_params=pltpu.CompilerParams(dimension_semantics=("parallel",), vmem_limit_bytes=VMEM_LIMIT),
        name="ssd_step",
    )(dt_flat, dec_flat, state, xt, bc)


def _s5_project_in(u_b16, wb_ref, store):
    kw = 16 * S5_GROUP_CH
    nw = 16 * S5_STATE
    for j in range(S5_WIDTH // kw):
        r = _dot(u_b16[:, j * kw:(j + 1) * kw], wb_ref[j])
        store(j, r[:, :nw], r[:, nw:])


def _s5_tail(hre_of, him_of, u_f32, wcr_ref, wci_ref, d_ref, wglu_ref, bglu_ref, nrm_ref):
    cols = []
    for j in range(wcr_ref.shape[0]):
        cols.append(_dot(hre_of(j).astype(BF16), wcr_ref[j]) + _dot(him_of(j).astype(BF16), wci_ref[j]))
    return _s5_finish(cols, u_f32, d_ref, wglu_ref, bglu_ref, nrm_ref)


def _s5_finish(cols, u_f32, d_ref, wglu_ref, bglu_ref, nrm_ref):
    y = jnp.concatenate(cols, axis=-1) + d_ref[...] * u_f32
    y = jax.nn.gelu(y)
    y = y * jax.nn.sigmoid(_dot(y.astype(BF16), wglu_ref[...]) + bglu_ref[...])
    return _rms(y, nrm_ref[...])


def _s5_seq_body(u_hbm, um_ref, wb_ref, abr_ref, abi_ref, wcr_ref, wci_ref, d_ref, wglu_ref, bglu_ref, nrm_ref,
                 y_hbm, sre_ref, sim_ref, ubuf, ybuf, bu, h, in_sems, out_sems):
    j = pl.program_id(0)
    last = pl.num_programs(0) - 1
    lc, bsz = ubuf.shape[1], ubuf.shape[2]
    rows = lc * bsz
    nw = 16 * S5_STATE

    def in_copy(step, b):
        return pltpu.make_async_copy(u_hbm.at[b, pl.ds(step * lc, lc), :], ubuf.at[step % 2, :, b, :],
                                     in_sems.at[step % 2, b])

    def out_copy(step, b):
        return pltpu.make_async_copy(ybuf.at[step % 2, :, b, :], y_hbm.at[b, pl.ds(step * lc, lc), :],
                                     out_sems.at[step % 2, b])

    def project_in(u_b16, nrows):
        def store(jj, re, im):
            bu[0:nrows, jj * nw:(jj + 1) * nw] = re
            bu[0:nrows, S5_LANES + jj * nw:S5_LANES + (jj + 1) * nw] = im
        _s5_project_in(u_b16, wb_ref, store)

    def scan(nsteps):
        for k in range(S5_LANES // S5_SCAN_LANES):
            sl_r = pl.ds(k * S5_SCAN_LANES, S5_SCAN_LANES)
            sl_i = pl.ds(S5_LANES + k * S5_SCAN_LANES, S5_SCAN_LANES)
            ar = abr_ref[:, sl_r]
            ai = abi_ref[:, sl_r]

            def step(l, carry):
                hr, hi = carry
                slab = pl.ds(pl.multiple_of(l * bsz, bsz), bsz)
                nr = ar * hr - ai * hi + bu[slab, sl_r]
                ni = ar * hi + ai * hr + bu[slab, sl_i]
                bu[slab, sl_r] = nr
                bu[slab, sl_i] = ni
                return nr, ni

            hr, hi = lax.fori_loop(0, nsteps, step, (h[:, sl_r], h[:, sl_i]))
            h[:, sl_r] = hr
            h[:, sl_i] = hi

    @pl.when(j == 0)
    def _first():
        for b in range(bsz):
            in_copy(0, b).start()
        h[...] = jnp.zeros_like(h)
        project_in(um_ref[...], N_META * bsz)
        scan(N_META)

    @pl.when(j < last)
    def _prefetch():
        for b in range(bsz):
            in_copy(j + 1, b).start()

    for b in range(bsz):
        in_copy(j, b).wait()
    u2 = ubuf[j % 2].reshape(rows, S5_WIDTH)
    u_b16 = u2.astype(BF16)
    kw = 16 * S5_GROUP_CH

    def project_block(jj):
        r = _dot(u_b16[:, jj * kw:(jj + 1) * kw], wb_ref[jj])
        bu[0:rows, jj * nw:(jj + 1) * nw] = r[:, :nw]
        bu[0:rows, S5_LANES + jj * nw:S5_LANES + (jj + 1) * nw] = r[:, nw:]

    def scan_block(jj):
        for k in range(nw // S5_SCAN_LANES):
            lo = jj * nw + k * S5_SCAN_LANES
            sl_r = slice(lo, lo + S5_SCAN_LANES)
            sl_i = slice(S5_LANES + lo, S5_LANES + lo + S5_SCAN_LANES)
            ar, ai = abr_ref[:, sl_r], abi_ref[:, sl_r]
            hr, hi = h[:, sl_r], h[:, sl_i]
            for l in range(lc):
                slab = slice(l * bsz, (l + 1) * bsz)
                hr, hi = (ar * hr - ai * hi + bu[slab, sl_r], ar * hi + ai * hr + bu[slab, sl_i])
                bu[slab, sl_r] = hr
                bu[slab, sl_i] = hi
            h[:, sl_r] = hr
            h[:, sl_i] = hi

    def readout_block(jj):
        return (_dot(bu[:, jj * nw:(jj + 1) * nw].astype(BF16), wcr_ref[jj])
                + _dot(bu[:, S5_LANES + jj * nw:S5_LANES + (jj + 1) * nw].astype(BF16), wci_ref[jj]))

    n_blocks = S5_WIDTH // kw
    project_block(0)
    cols = []
    for jj in range(n_blocks):
        if jj + 1 < n_blocks:
            project_block(jj + 1)
        scan_block(jj)
        cols.append(readout_block(jj))
    y = _s5_finish(cols, u2, d_ref, wglu_ref, bglu_ref, nrm_ref)
    ybuf[j % 2] = y.reshape(lc, bsz, S5_WIDTH)
    for b in range(bsz):
        out_copy(j, b).start()

    @pl.when(j > 0)
    def _wait_previous_out():
        for b in range(bsz):
            out_copy(j - 1, b).wait()

    @pl.when(j == last)
    def _emit():
        for b in range(bsz):
            out_copy(j, b).wait()
        sre_ref[...] = h[:, 0:S5_LANES]
        sim_ref[...] = h[:, S5_LANES:]


def _s5_seq(u, um, wb, abr, abi, wcr, wci, d, wglu, bglu, nrm):
    bsz, seq, _ = u.shape
    lc = S5_TIME_TILE
    consts = (um, wb, abr, abi, wcr, wci, d, wglu, bglu, nrm)
    st = pl.BlockSpec((bsz, S5_LANES), lambda j: (0, 0))
    return pl.pallas_call(
        _s5_seq_body,
        grid=(seq // lc,),
        in_specs=[pl.BlockSpec(memory_space=pl.ANY)] + [_resident_spec(a) for a in consts],
        out_specs=[pl.BlockSpec(memory_space=pl.ANY), st, st],
        out_shape=[jax.ShapeDtypeStruct((bsz, seq, S5_WIDTH), F32),
                   jax.ShapeDtypeStruct((bsz, S5_LANES), F32), jax.ShapeDtypeStruct((bsz, S5_LANES), F32)],
        scratch_shapes=[pltpu.VMEM((2, lc, bsz, S5_WIDTH), F32), pltpu.VMEM((2, lc, bsz, S5_WIDTH), F32),
                        pltpu.VMEM((lc * bsz, 2 * S5_LANES), F32), pltpu.VMEM((bsz, 2 * S5_LANES), F32),
                        pltpu.SemaphoreType.DMA((2, bsz)), pltpu.SemaphoreType.DMA((2, bsz))],
        compiler_params=pltpu.CompilerParams(dimension_semantics=("arbitrary",), vmem_limit_bytes=VMEM_LIMIT),
        name="s5_seq",
    )(u, *consts)


def _sample_post_body(yc_ref, xs_ref, z_ref, dexp_ref, snrm_ref, u_ref, hr_ref, hi_ref, wb_ref, abr_ref, abi_ref,
                      wcr_ref, wci_ref, d_ref, wglu_ref, bglu_ref, nrm_ref,
                      yssd_ref, ys5_ref, nre_ref, nim_ref):
    z = z_ref[...]
    y = yc_ref[...] + dexp_ref[...] * xs_ref[...]
    yssd_ref[...] = _rms(y * (z * jax.nn.sigmoid(z)), snrm_ref[...]).astype(yssd_ref.dtype)

    u = u_ref[...]
    nw = 16 * S5_STATE
    ar, ai = abr_ref[...], abi_ref[...]

    def store(jj, re, im):
        sl = slice(jj * nw, (jj + 1) * nw)
        h0r, h0i = hr_ref[:, sl], hi_ref[:, sl]
        nre_ref[:, sl] = ar[:, sl] * h0r - ai[:, sl] * h0i + re
        nim_ref[:, sl] = ar[:, sl] * h0i + ai[:, sl] * h0r + im

    _s5_project_in(u.astype(BF16), wb_ref, store)
    slab = lambda ref: (lambda jj: ref[:, jj * nw:(jj + 1) * nw])
    y5 = _s5_tail(slab(nre_ref), slab(nim_ref), u, wcr_ref, wci_ref, d_ref, wglu_ref, bglu_ref, nrm_ref)
    ys5_ref[...] = y5.astype(ys5_ref.dtype)


def _sample_post(yc, xs, z, dexp, snrm, u, h0r, h0i, wb, abr1, abi1, wcr, wci, d, wglu, bglu, nrm):
    n = yc.shape[0]
    args = (yc, xs, z, dexp, snrm, u, h0r, h0i, wb, abr1, abi1, wcr, wci, d, wglu, bglu, nrm)
    spec = lambda w: pl.BlockSpec((n, w), lambda: (0, 0))
    return pl.pallas_call(
        _sample_post_body,
        in_specs=[_full_spec(a) for a in args],
        out_specs=[spec(SSD_WIDTH), spec(S5_WIDTH), spec(S5_LANES), spec(S5_LANES)],
        out_shape=[jax.ShapeDtypeStruct((n, SSD_WIDTH), BF16), jax.ShapeDtypeStruct((n, S5_WIDTH), BF16),
                   jax.ShapeDtypeStruct((n, S5_LANES), F32), jax.ShapeDtypeStruct((n, S5_LANES), F32)],
        compiler_params=pltpu.CompilerParams(vmem_limit_bytes=VMEM_LIMIT),
        name="sample_post",
    )(*args)


def _mix_route_body(n_blocks, n_sorted, xp_ref, ysp_ref, y5p_ref, xs_ref, yss_ref, y5s_ref, *refs):
    consts = refs[:6]
    x1_ref, xn_hbm, rt_ref, pos_ref, meta_ref, carry, fields, xbuf, sems = refs[6:]
    i = pl.program_id(0)
    tm, n_sample = xp_ref.shape[0], xs_ref.shape[0]
    col0 = pl.multiple_of(i * tm, LANES)

    def xn_copy(step, rows, j):
        return pltpu.make_async_copy(xbuf.at[step % 2, pl.ds(0, rows), pl.ds(j * LANES, LANES)],
                                     xn_hbm.at[pl.ds(step * tm, rows), j, :], sems.at[step % 2, j])

    @pl.when(i == 0)
    def _init():
        carry[...] = jnp.zeros_like(carry)

    @pl.when(i < n_blocks)
    def _prompt_rows():
        _mix_route_compute(xp_ref, ysp_ref, y5p_ref, *consts, x1_ref, rt_ref, carry, xbuf.at[i % 2], fields, col0)
        for j in range(SLAB_ROWS):
            xn_copy(i, tm, j).start()

    @pl.when(i == n_blocks)
    def _sample_rows():
        _mix_route_compute(xs_ref, yss_ref, y5s_ref, *consts, x1_ref, rt_ref, carry, xbuf.at[i % 2], fields, col0)
        for j in range(SLAB_ROWS):
            xn_copy(i, n_sample, j).start()
        _route_layout(carry, fields, pos_ref, meta_ref, n_sorted)
        for j in range(SLAB_ROWS):
            xn_copy(i, n_sample, j).wait()

    @pl.when(i > 0)
    def _wait_previous_rows():
        for j in range(SLAB_ROWS):
            xn_copy(i - 1, tm, j).wait()


def _route_layout(carry, fields, pos_ref, meta_ref, n_sorted):
    counts = carry[...]
    tiles_per = jnp.floor((counts + (MOE_TILE - 1)) * (1.0 / MOE_TILE))
    upto = lax.broadcasted_iota(jnp.int32, (LANES, LANES), 0) <= lax.broadcasted_iota(jnp.int32, (LANES, LANES), 1)
    tile_end = _dot(tiles_per.astype(BF16), upto.astype(BF16))
    pstart = (tile_end - tiles_per) * MOE_TILE
    n_used = tile_end[:, MOE_EXPERTS - 1:MOE_EXPERTS]

    f = fields[...]
    first_row = jnp.zeros_like(f)
    tile_id = jnp.minimum(lax.broadcasted_iota(jnp.int32, meta_ref.shape, 1).astype(F32), n_used - 1.0)
    tile_expert = jnp.zeros(meta_ref.shape, F32)
    for e in range(MOE_EXPERTS):
        first_row = first_row + jnp.where(f == float(e), pstart[:, e:e + 1], 0.0)
        tile_expert = tile_expert + jnp.where(tile_end[:, e:e + 1] <= tile_id, 1.0, 0.0)
    pos = first_row + pltpu.roll(f, shift=4, axis=0)
    pos_ref[...] = jnp.clip(pos, 0.0, n_sorted - 1.0).astype(jnp.int32)
    is_row0 = lax.broadcasted_iota(jnp.int32, meta_ref.shape, 0) == 0
    meta_ref[...] = jnp.where(is_row0, tile_expert, n_used).astype(jnp.int32)


def _mix_route_compute(x_ref, ys_ref, y5_ref, wa_ref, wb_ref, nf_ref, wrh_ref, wrl_ref, br_ref,
                       x1_ref, rt_ref, carry, xn_buf, fields, col0):
    rows = x_ref.shape[0]
    x1 = x_ref[...] + _dot(ys_ref[...], wa_ref[...]) + _dot(y5_ref[...].astype(BF16), wb_ref[...])
    x1_ref[0:rows, :] = x1
    xn = _rms(x1, nf_ref[...])
    xn_buf[0:rows, :] = xn

    xh = xn.astype(BF16)
    xl = (xn - xh.astype(F32)).astype(BF16)
    logits = _dot(xh, wrh_ref[...]) + _dot(xl, wrh_ref[...]) + _dot(xh, wrl_ref[...]) + br_ref[...]
    tm = logits.shape[0]
    lane = lax.broadcasted_iota(jnp.int32, logits.shape, 1).astype(F32)
    neg = -jnp.inf
    big = float(LANES)

    def first_max(v):
        m = jnp.max(v, axis=-1, keepdims=True)
        return m, jnp.min(jnp.where(v == m, lane, big), axis=-1, keepdims=True)

    coarse = lane < MOE_GROUPS
    mc, gsel = first_max(jnp.where(coarse, logits, neg))
    psel = 1.0 / jnp.sum(jnp.where(coarse, jnp.exp(logits - mc), 0.0), axis=-1, keepdims=True)
    lo = MOE_GROUPS + MOE_EPG * gsel
    lf = jnp.where((lane >= lo) & (lane < lo + MOE_EPG), logits, neg)
    m1, i1 = first_max(lf)
    m2, i2 = first_max(jnp.where(lane == i1, neg, lf))
    e2 = jnp.exp(m2 - m1)
    g1 = psel / (1.0 + e2)
    g2 = psel * e2 / (1.0 + e2)
    e_a, e_b = i1 - MOE_GROUPS, i2 - MOE_GROUPS

    pick_a, pick_b = lane == e_a, lane == e_b
    picks = jnp.where(pick_a | pick_b, 1.0, 0.0)
    earlier = lax.broadcasted_iota(jnp.int32, (tm, tm), 0) > lax.broadcasted_iota(jnp.int32, (tm, tm), 1)
    prior = _dot(earlier.astype(BF16), picks.astype(BF16)) + carry[...]
    rank_a = jnp.sum(jnp.where(pick_a, prior, 0.0), axis=-1, keepdims=True)
    rank_b = jnp.sum(jnp.where(pick_b, prior, 0.0), axis=-1, keepdims=True)
    carry[...] = prior[tm - 1:tm, :] + picks[tm - 1:tm, :]

    out = jnp.zeros_like(logits)
    for k, v in enumerate((e_a, e_b, g1, g2, rank_a, rank_b)):
        out = jnp.where(lane == float(k), v, out)
    rt_ref[0:rows, :] = out
    fields[:, pl.ds(col0, rows)] = out.T[0:SUBLANES, :]


def _mix_route(prompt, sample, consts, tm, n_tiles):
    n_prompt, n_sample = prompt[0].shape[0], sample[0].shape[0]
    assert n_prompt % tm == 0 and n_sample <= tm
    n_blocks = n_prompt // tm
    total_rows = n_prompt + n_sample
    row = lambda w: pl.BlockSpec((tm, w), lambda i: (jnp.minimum(i, n_blocks - 1), 0))
    out_row = lambda w: pl.BlockSpec((tm, w), lambda i: (i, 0))
    assert total_rows % LANES == 0 and n_tiles <= 2 * LANES
    whole = lambda shape: pl.BlockSpec(shape, lambda i: (0, 0))
    return pl.pallas_call(
        functools.partial(_mix_route_body, n_blocks, n_tiles * MOE_TILE),
        grid=(n_blocks + 1,),
        in_specs=([row(D_MODEL), row(SSD_WIDTH), row(S5_WIDTH)] + [_full_spec(a) for a in sample]
                  + [_full_spec(a) for a in consts]),
        out_specs=[out_row(D_MODEL), pl.BlockSpec(memory_space=pl.ANY), out_row(LANES),
                   whole((SUBLANES, total_rows)), whole((SUBLANES, 2 * LANES))],
        out_shape=[jax.ShapeDtypeStruct((total_rows, D_MODEL), F32),
                   jax.ShapeDtypeStruct((total_rows, SLAB_ROWS, LANES), F32),
                   jax.ShapeDtypeStruct((total_rows, LANES), F32),
                   jax.ShapeDtypeStruct((SUBLANES, total_rows), jnp.int32),
                   jax.ShapeDtypeStruct((SUBLANES, 2 * LANES), jnp.int32)],
        scratch_shapes=[pltpu.VMEM((1, LANES), F32), pltpu.VMEM((SUBLANES, total_rows), F32),
                        pltpu.VMEM((2, tm, D_MODEL), F32), pltpu.SemaphoreType.DMA((2, SLAB_ROWS))],
        compiler_params=pltpu.CompilerParams(dimension_semantics=("arbitrary",), vmem_limit_bytes=VMEM_LIMIT),
        name="mix_route",
    )(*prompt, *sample, *consts)


def _sc_mesh():
    return plsc.VectorSubcoreMesh(core_axis_name="c", subcore_axis_name="s")


def _sc_worker():
    return lax.axis_index("s") * SC_CORES + lax.axis_index("c")


def _sc_dispatch(xn, pos_a, pos_b, n_rows):
    n_tok = xn.shape[0]
    ch = SC_DISPATCH_ROWS
    n_chunks = n_tok // ch
    assert n_tok % ch == 0 and n_chunks >= SC_WORKERS
    max_mine = -(-n_chunks // SC_WORKERS)
    stage = [pltpu.VMEM((ch,), jnp.int32), pltpu.VMEM((ch,), jnp.int32), pltpu.VMEM((ch, SLAB_ROWS, LANES), F32),
             pltpu.SemaphoreType.DMA]

    @functools.partial(
        pl.kernel, mesh=_sc_mesh(),
        out_type=jax.ShapeDtypeStruct((n_rows, SLAB_ROWS, LANES), F32),
        scratch_types=stage + stage + [pltpu.SemaphoreType.DMA])
    def push(xn_hbm, pa_hbm, pb_hbm, xs_hbm, ia0, ib0, rows0, lsem0, ia1, ib1, rows1, lsem1, ssem):
        wid = _sc_worker()
        mine = (n_chunks - wid + SC_WORKERS - 1) // SC_WORKERS
        bufs = ((ia0, ib0, rows0, lsem0), (ia1, ib1, rows1, lsem1))

        def loads(t, b):
            ia, ib, rows, sem = bufs[b]
            off = pl.multiple_of((wid + t * SC_WORKERS) * ch, ch)
            return (pltpu.make_async_copy(pa_hbm.at[pl.ds(off, ch)], ia, sem),
                    pltpu.make_async_copy(pb_hbm.at[pl.ds(off, ch)], ib, sem),
                    pltpu.make_async_copy(xn_hbm.at[pl.ds(off, ch)], rows, sem))

        def stage_in(t, b):
            for c in loads(t, b):
                c.start()

        def scatter(t, b):
            ia, ib, rows, _ = bufs[b]
            for c in loads(t, b):
                c.wait()
            first = pltpu.async_copy(rows, xs_hbm.at[ia], ssem)
            second = pltpu.async_copy(rows, xs_hbm.at[ib], ssem)
            first.wait()
            second.wait()

        stage_in(0, 0)

        @pl.loop(0, (max_mine + 1) // 2)
        def _(p):
            t = 2 * p

            @pl.when(t + 1 < mine)
            def _():
                stage_in(t + 1, 1)

            @pl.when(t < mine)
            def _():
                scatter(t, 0)

            @pl.when(t + 2 < mine)
            def _():
                stage_in(t + 2, 0)

            @pl.when(t + 1 < mine)
            def _():
                scatter(t + 1, 1)

    return push(xn, pos_a, pos_b)


def _sc_collect(ysorted, pos_flat, ch):
    n_pick = pos_flat.shape[0]
    per_worker = n_pick // SC_WORKERS
    n_chunks = per_worker // ch
    assert n_pick % SC_WORKERS == 0 and per_worker % ch == 0
    row_shape, dtype = ysorted.shape[1:], ysorted.dtype

    @functools.partial(
        pl.kernel, mesh=_sc_mesh(),
        out_type=jax.ShapeDtypeStruct((n_pick,) + row_shape, dtype),
        scratch_types=[pltpu.VMEM((ch,), jnp.int32), pltpu.VMEM((ch,), jnp.int32),
                       pltpu.VMEM((ch,) + row_shape, dtype), pltpu.VMEM((ch,) + row_shape, dtype),
                       pltpu.SemaphoreType.DMA, pltpu.SemaphoreType.DMA])
    def pull(ys_hbm, pos_hbm, out_hbm, idx0, idx1, rows0, rows1, sem0, sem1):
        base = _sc_worker() * per_worker
        bufs = ((idx0, rows0, sem0), (idx1, rows1, sem1))

        def offset(j):
            return pl.multiple_of(base + j * ch, SUBLANES)

        def fetch(j, b):
            idx, rows, sem = bufs[b]
            pltpu.sync_copy(pos_hbm.at[pl.ds(offset(j), ch)], idx)
            pltpu.async_copy(ys_hbm.at[idx], rows, sem)

        def flush(j, b):
            idx, rows, sem = bufs[b]
            pltpu.make_async_copy(ys_hbm.at[idx], rows, sem).wait()
            pltpu.sync_copy(rows, out_hbm.at[pl.ds(offset(j), ch)])

        fetch(0, 0)

        @pl.loop(0, n_chunks // 2)
        def _(p):
            j = 2 * p
            fetch(j + 1, 1)
            flush(j, 0)

            @pl.when(j + 2 < n_chunks)
            def _():
                fetch(j + 2, 0)

            flush(j + 1, 1)

        if n_chunks % 2:
            flush(n_chunks - 1, 0)

    return pull(ysorted, pos_flat)


def _slab_columns(ref, rows, j):
    return ref[pl.ds(j, rows, stride=SLAB_ROWS), :]


def _moe_ffn_body(te_ref, nused_ref, x_ref, wg_ref, wu_ref, wd_ref, y_ref, wgb, wub, wdb):
    i = pl.program_id(0)

    @pl.when(i >= nused_ref[0])
    def _unused_tile():
        y_ref[...] = jnp.zeros_like(y_ref)

    @pl.when(i < nused_ref[0])
    def _tile():
        @pl.when((i == 0) | (te_ref[i] != te_ref[jnp.maximum(i - 1, 0)]))
        def _cast_weights():
            wgb[...] = wg_ref[0].astype(BF16)
            wub[...] = wu_ref[0].astype(BF16)
            wdb[...] = wd_ref[0].astype(BF16)

        x = jnp.concatenate([_slab_columns(x_ref, MOE_TILE, j) for j in range(SLAB_ROWS)], axis=-1).astype(BF16)
        gate = _dot(x, wgb[...])
        hmid = (gate * jax.nn.sigmoid(gate)) * _dot(x, wub[...])
        y = _dot(hmid.astype(BF16), wdb[...])
        bits = pltpu.bitcast(y.astype(BF16).astype(F32), jnp.uint32)
        half = D_MODEL // 2
        packed = (bits[:, :half] & jnp.uint32(0xFFFF0000)) | (bits[:, half:] >> jnp.uint32(16))
        for j in range(PACK_ROWS):
            y_ref[pl.ds(j, MOE_TILE, stride=PACK_ROWS), :] = packed[:, j * LANES:(j + 1) * LANES]


def _moe_ffn(tile_expert, n_used, xsorted, w_gate, w_up, w_down):
    n_tiles = tile_expert.shape[0]
    wspec = lambda s: pl.BlockSpec((1,) + s, lambda i, te, nu: (te[i], 0, 0))
    tile = lambda rows, imap: pl.BlockSpec((MOE_TILE * rows, LANES), imap)
    return pl.pallas_call(
        _moe_ffn_body,
        grid_spec=pltpu.PrefetchScalarGridSpec(
            num_scalar_prefetch=2,
            grid=(n_tiles,),
            in_specs=[tile(SLAB_ROWS, lambda i, te, nu: (jnp.clip(i, 0, jnp.maximum(nu[0] - 1, 0)), 0)),
                      wspec((D_MODEL, MOE_D_FF)), wspec((D_MODEL, MOE_D_FF)), wspec((MOE_D_FF, D_MODEL))],
            out_specs=tile(PACK_ROWS, lambda i, te, nu: (i, 0)),
            scratch_shapes=[pltpu.VMEM((D_MODEL, MOE_D_FF), BF16), pltpu.VMEM((D_MODEL, MOE_D_FF), BF16),
                            pltpu.VMEM((MOE_D_FF, D_MODEL), BF16)]),
        out_shape=jax.ShapeDtypeStruct((n_tiles * MOE_TILE * PACK_ROWS, LANES), jnp.uint32),
        compiler_params=pltpu.CompilerParams(dimension_semantics=("arbitrary",), vmem_limit_bytes=VMEM_LIMIT),
        name="moe_ffn",
    )(tile_expert, n_used, xsorted, w_gate, w_up, w_down)


def _combine_body(x1_ref, rt_ref, ya_ref, yb_ref, nf_ref, *rest):
    out_ref = rest[-1]
    rt = rt_ref[...]
    x1 = x1_ref[...]
    tm = x1.shape[0]

    def unpack(y_ref):
        words = [y_ref[0, pl.ds(j, tm, stride=PACK_ROWS), :] for j in range(PACK_ROWS)]
        high = [pltpu.bitcast(w & jnp.uint32(0xFFFF0000), F32) for w in words]
        low = [pltpu.bitcast(w << jnp.uint32(16), F32) for w in words]
        return jnp.concatenate(high + low, axis=-1)

    x2 = x1 + rt[:, 2:3] * unpack(ya_ref) + rt[:, 3:4] * unpack(yb_ref)
    out_ref[...] = _rms(x2, nf_ref[...])


def _combine(x1, rt, y_picks, nf, tm, rows, x_block, y_block, out_rows, out_block, out_buf=None):
    row = lambda w: pl.BlockSpec((tm, w), lambda i: (i + x_block, 0))
    pick = lambda k: pl.BlockSpec((1, tm * PACK_ROWS, LANES), lambda i: (k, i + y_block, 0))
    in_specs = [row(D_MODEL), row(LANES), pick(0), pick(1), pl.BlockSpec((1, D_MODEL), lambda i: (0, 0))]
    args = [x1, rt, y_picks, y_picks, nf]
    aliases = {}
    if out_buf is not None:
        in_specs.append(pl.BlockSpec(memory_space=pl.ANY))
        aliases[len(args)] = 0
        args.append(out_buf)
    return pl.pallas_call(
        _combine_body,
        grid=(rows // tm,),
        in_specs=in_specs,
        out_specs=pl.BlockSpec((tm, D_MODEL), lambda i: (i + out_block, 0)),
        out_shape=jax.ShapeDtypeStruct((out_rows, D_MODEL), F32),
        input_output_aliases=aliases,
        compiler_params=pltpu.CompilerParams(dimension_semantics=("parallel",), vmem_limit_bytes=VMEM_LIMIT),
        name="moe_combine",
    )(*args)


def _s5_tables(a_re, a_im, log_dt, b_re, b_im, c_re, c_im):
    dt = jnp.exp(log_dt)[:, None]
    mag = jnp.exp(a_re * dt)
    ab_re = mag * jnp.cos(a_im * dt)
    ab_im = mag * jnp.sin(a_im * dt)
    den = a_re * a_re + a_im * a_im
    nr = ab_re - 1.0
    q_re = (nr * a_re + ab_im * a_im) / den
    q_im = (ab_im * a_re - nr * a_im) / den
    bb_re = q_re[..., None] * b_re - q_im[..., None] * b_im
    bb_im = q_re[..., None] * b_im + q_im[..., None] * b_re
    nblk = S5_GROUPS // 16
    kw, nw = 16 * S5_GROUP_CH, 16 * S5_STATE
    same_group = (jnp.arange(kw)[:, None] // S5_GROUP_CH) == (jnp.arange(nw)[None, :] // S5_STATE)

    def in_map(bb):
        rows = bb.reshape(nblk, 16, S5_STATE, S5_GROUP_CH).transpose(0, 1, 3, 2).reshape(nblk, kw, S5_STATE)
        return jnp.where(same_group, jnp.tile(rows, (1, 1, 16)), 0.0)

    def out_map(cc):
        cols = cc.reshape(nblk, 16, S5_GROUP_CH, S5_STATE).transpose(0, 3, 1, 2).reshape(nblk, S5_STATE, kw)
        return jnp.where(same_group.T, jnp.tile(cols, (1, 16, 1)), 0.0)

    wb = jnp.concatenate([in_map(bb_re), in_map(bb_im)], axis=-1).astype(BF16)
    return (wb, ab_re.reshape(1, S5_LANES), ab_im.reshape(1, S5_LANES),
            out_map(c_re).astype(BF16), out_map(-c_im).astype(BF16))


def kernel(x_prompt, x_sample, state_ssd_conv, state_ssd_ssm, state_s5_re, state_s5_im, meta_tokens, norm_mix, w_in, conv_w, conv_b, dt_bias, a_log, d_ssd, ssd_norm, s5_a_re, s5_a_im, s5_log_dt, s5_b_re, s5_b_im, s5_c_re, s5_c_im, s5_d, w_glu, b_glu, s5_norm, w_out, norm_ffn, router_coarse_w, router_coarse_b, router_fine_w, router_fine_b, w_gate, w_up, w_down, norm_final):
    bp, seq, _ = x_prompt.shape
    bs = x_sample.shape[0]
    n_prompt = bp * seq
    n_tok = n_prompt + bs
    row2 = lambda v: v.reshape(1, -1)
    pad_heads = lambda v: jnp.pad(v, (0, LANES - SSD_HEADS)).reshape(1, LANES)

    w = w_in[0]
    o1, o2, o3 = SSD_WIDTH, SSD_WIDTH + SSD_CONV_DIM, SSD_WIDTH + SSD_CONV_DIM + SSD_HEADS
    wz, wx, wu = w[:, :o1].astype(BF16), w[:, o1:o2].astype(BF16), w[:, o3:].astype(BF16)
    wdt = jnp.pad(w[:, o2:o3], ((0, 0), (0, LANES - SSD_HEADS))).astype(BF16)
    g_mix = row2(norm_mix[0])
    cw, cb = conv_w[0], row2(conv_b[0])
    dtb, alog = pad_heads(dt_bias[0]), pad_heads(a_log[0])
    dexp = row2(jnp.repeat(d_ssd[0], SSD_HEAD_DIM))
    snrm = row2(ssd_norm[0])
    eexp = (jnp.arange(LANES)[:, None] == (jnp.arange(SSD_WIDTH) // SSD_HEAD_DIM)[None, :]).astype(BF16)
    wb5, ab_re, ab_im, wcr, wci = _s5_tables(s5_a_re[0], s5_a_im[0], s5_log_dt[0], s5_b_re[0], s5_b_im[0],
                                             s5_c_re[0], s5_c_im[0])
    d5, wglu, bglu, nrm5 = row2(s5_d[0]), w_glu[0].astype(BF16), row2(b_glu[0]), row2(s5_norm[0])
    wo_a, wo_b = w_out[0][:SSD_WIDTH].astype(BF16), w_out[0][SSD_WIDTH:].astype(BF16)
    w_r = jnp.concatenate([router_coarse_w[0], router_fine_w[0].transpose(1, 0, 2).reshape(D_MODEL, MOE_EXPERTS)], axis=1)
    w_r = jnp.pad(w_r, ((0, 0), (0, LANES - w_r.shape[1])))
    wrh = w_r.astype(BF16)
    wrl = (w_r - wrh.astype(F32)).astype(BF16)
    b_r = jnp.concatenate([router_coarse_b[0], router_fine_b[0].reshape(-1)])
    b_r = jnp.pad(b_r, (0, LANES - b_r.shape[0])).reshape(1, LANES)

    zp, xbcp, dtp, up = _in_proj(x_prompt.reshape(n_prompt, D_MODEL), g_mix, wz, wx, wdt, wu, TOK_TILE, BF16, F32)
    xsm = jnp.concatenate([x_sample.reshape(bs, D_MODEL), meta_tokens], axis=0)
    zs, xbcs, dts, us = _in_proj(xsm, g_mix, wz, wx, wdt, wu, xsm.shape[0], F32, F32)

    front = SSD_CHUNK - N_META
    padf = lambda a: jnp.pad(a[bs:], ((front, 0), (0, 0)))[None]
    gw = SSD_HPG * SSD_HEAD_DIM
    ssd_consts = (cw, cb, dtb, alog, dexp, snrm, eexp)
    _, ctail_m, _, ht_m = _ssd_chunked(
        padf(xbcs).astype(BF16), padf(dts), jnp.zeros((1, SSD_CHUNK, SSD_WIDTH), F32),
        jnp.zeros((1, SUBLANES, SSD_CONV_DIM), F32), jnp.zeros((1, SSD_GROUPS, SSD_STATE, gw), F32),
        *ssd_consts, mask_rows=front)
    y_ssd_p, ctail_p, ssm_p, _ = _ssd_chunked(
        xbcp.reshape(bp, seq, SSD_CONV_DIM), dtp.reshape(bp, seq, LANES), zp.reshape(bp, seq, SSD_WIDTH),
        ctail_m, ht_m, *ssd_consts, mask_rows=0)

    abr8, abi8 = jnp.broadcast_to(ab_re, (bp, S5_LANES)), jnp.broadcast_to(ab_im, (bp, S5_LANES))
    um8 = jnp.repeat(us[bs:], bp, axis=0).astype(BF16)
    y_s5_p, s5re_p, s5im_p = _s5_seq(up.reshape(bp, seq, S5_WIDTH), um8, wb5, abr8, abi8,
                                     wcr, wci, d5, wglu, bglu, nrm5)

    cst = state_ssd_conv[0]
    xt_s, dt_s, dec_s, bc, xs_s = _ssd_step_prep(xbcs[:bs], cst[:, 0], cst[:, 1], cst[:, 2], dts[:bs],
                                                 cw, cb, dtb, alog)
    ssm_s, y_core = _ssd_step(dt_s[:, :SSD_HEADS].reshape(-1), dec_s[:, :SSD_HEADS].reshape(-1),
                              state_ssd_ssm[0], xt_s, bc)
    y_ssd_s, y_s5_s, s5re_s, s5im_s = _sample_post(
        y_core, xs_s, zs[:bs], dexp, snrm, us[:bs], state_s5_re[0].reshape(bs, S5_LANES),
        state_s5_im[0].reshape(bs, S5_LANES), wb5, ab_re, ab_im, wcr, wci, d5, wglu, bglu, nrm5)

    route_consts = (wo_a, wo_b, row2(norm_ffn[0]), wrh, wrl, b_r)
    n_tiles = -(-2 * n_tok // MOE_TILE) + MOE_EXPERTS
    x1, xn, rt, pos, meta = _mix_route(
        (x_prompt.reshape(n_prompt, D_MODEL), y_ssd_p.reshape(n_prompt, SSD_WIDTH), y_s5_p.reshape(n_prompt, S5_WIDTH)),
        (x_sample.reshape(bs, D_MODEL), y_ssd_s, y_s5_s), route_consts, TOK_TILE, n_tiles)

    pos_a, pos_b = pos[0], pos[1]
    tile_expert, n_used = meta[0, :n_tiles], meta[1, :1]
    xsorted = _sc_dispatch(xn, pos_a, pos_b, n_tiles * MOE_TILE)
    ysorted = _moe_ffn(tile_expert, n_used, xsorted.reshape(-1, LANES), w_gate[0], w_up[0], w_down[0])
    nfin = row2(norm_final)

    half = n_prompt // 2

    def collect(lo, hi, ch):
        picks = jnp.concatenate([pos_a[lo:hi], pos_b[lo:hi]])
        packed_rows = ysorted.reshape(-1, PACK_ROWS, LANES)
        return _sc_collect(packed_rows, picks, ch).reshape(2, (hi - lo) * PACK_ROWS, LANES)

    picks_1 = collect(0, half, SC_COLLECT_ROWS[0])
    picks_2 = collect(half, n_tok, SC_COLLECT_ROWS[1])
    blocks = half // MOE_TILE
    y_p = _combine(x1, rt, picks_1, nfin, MOE_TILE, half, 0, 0, n_prompt, 0)
    y_p = _combine(x1, rt, picks_2, nfin, MOE_TILE, half, blocks, 0, n_prompt, blocks, out_buf=y_p)
    y_s = _combine(x1, rt, picks_2, nfin, bs, bs, n_prompt // bs, half // bs, bs, 0)

    s5_state = lambda a, b: a.reshape(1, b, S5_GROUPS, S5_STATE)
    new_conv_s = jnp.stack([cst[:, 1], cst[:, 2], xbcs[:bs]], axis=1)[None]
    return (y_p.reshape(bp, seq, D_MODEL), y_s.reshape(bs, 1, D_MODEL),
            ctail_p[:, SUBLANES - (SSD_CONV - 1):][None], ssm_p[None], s5_state(s5re_p, bp), s5_state(s5im_p, bp),
            new_conv_s, ssm_s[None], s5_state(s5re_s, bs), s5_state(s5im_s, bs))
```

```python
import functools

import jax
import jax.numpy as jnp
from jax import lax
from jax.experimental import pallas as pl
from jax.experimental.pallas import tpu as pltpu
from jax.experimental.pallas import tpu_sc as plsc

F32, BF16 = jnp.float32, jnp.bfloat16

D_MODEL = 1024
N_META = 16
SSD_WIDTH = 1024
SSD_HEAD_DIM = 64
SSD_HEADS = 16
SSD_GROUPS = 2
SSD_HPG = SSD_HEADS // SSD_GROUPS
SSD_STATE = 128
SSD_CONV = 4
SSD_CHUNK = 128
SSD_CONV_DIM = SSD_WIDTH + 2 * SSD_GROUPS * SSD_STATE
S5_WIDTH = 1024
S5_GROUP_CH = 16
S5_GROUPS = 64
S5_STATE = 64
S5_LANES = S5_GROUPS * S5_STATE
MOE_GROUPS = 4
MOE_EPG = 8
MOE_EXPERTS = MOE_GROUPS * MOE_EPG
MOE_D_FF = 512
EPS = 1e-6

LANES = 128
SUBLANES = 8
VMEM_LIMIT = 56 * 1024 * 1024

S5_TIME_TILE = 64
S5_SCAN_LANES = 512
MOE_TILE = 256
SLAB_ROWS = D_MODEL // LANES
PACK_ROWS = SLAB_ROWS // 2
SC_CORES = 2
SC_SUBCORES = 16
SC_WORKERS = SC_CORES * SC_SUBCORES
SC_DISPATCH_ROWS = 32
SC_COLLECT_ROWS = (32, 40)
TOK_TILE = 512


def _dot(a, b):
    return jnp.dot(a, b, preferred_element_type=F32)


def _rms(x, g):
    return x * lax.rsqrt(jnp.mean(x * x, axis=-1, keepdims=True) + EPS) * g


def _softplus(x):
    return jnp.maximum(x, 0.0) + jnp.log1p(jnp.exp(-jnp.abs(x)))


def _split3(x):
    hi = x.astype(BF16)
    r = x - hi.astype(F32)
    mid = r.astype(BF16)
    lo = (r - mid.astype(F32)).astype(BF16)
    return hi, mid, lo


def _dot3(x, w):
    hi, mid, lo = _split3(x)
    return _dot(hi, w) + _dot(mid, w) + _dot(lo, w)


def _dot3_left(w, x):
    hi, mid, lo = _split3(x)
    return _dot(w, hi) + _dot(w, mid) + _dot(w, lo)


def _pack_bf16_pairs(x):
    bits = pltpu.bitcast(x.astype(BF16).astype(F32), jnp.uint32)
    half = x.shape[1] // 2
    return (bits[:, :half] & jnp.uint32(0xFFFF0000)) | (bits[:, half:] >> jnp.uint32(16))


def _unpack_bf16_pairs(ref, rows):
    words = [ref[pl.ds(j, rows, stride=PACK_ROWS), :] for j in range(PACK_ROWS)]
    high = [pltpu.bitcast(w & jnp.uint32(0xFFFF0000), F32) for w in words]
    low = [pltpu.bitcast(w << jnp.uint32(16), F32) for w in words]
    return jnp.concatenate(high + low, axis=-1)


def _full_spec(a):
    nd = a.ndim
    return pl.BlockSpec(a.shape, lambda *_: (0,) * nd)


def _resident_spec(a):
    nd = a.ndim
    return pl.BlockSpec(a.shape, lambda *_: (0,) * nd, pipeline_mode=pl.Buffered(1))


def _in_proj_body(x_ref, g_ref, wz_ref, wx_ref, wdt_ref, wu_ref, z_ref, xbc_ref, dt_ref, u_ref):
    xb = _rms(x_ref[...], g_ref[...]).astype(BF16)
    z_ref[...] = _dot(xb, wz_ref[...]).astype(z_ref.dtype)
    xbc_ref[...] = _dot(xb, wx_ref[...]).astype(xbc_ref.dtype)
    dt_ref[...] = _dot(xb, wdt_ref[...])
    u_ref[...] = _dot(xb, wu_ref[...]).astype(u_ref.dtype)


def _in_proj(x2d, g, wz, wx, wdt, wu, tm, act_dtype, u_dtype):
    rows = x2d.shape[0]
    row = lambda w: pl.BlockSpec((tm, w), lambda i: (i, 0))
    return pl.pallas_call(
        _in_proj_body,
        grid=(rows // tm,),
        in_specs=[row(D_MODEL), _full_spec(g), _full_spec(wz), _full_spec(wx), _full_spec(wdt), _full_spec(wu)],
        out_specs=[row(SSD_WIDTH), row(SSD_CONV_DIM), row(LANES), row(S5_WIDTH)],
        out_shape=[jax.ShapeDtypeStruct((rows, SSD_WIDTH), act_dtype),
                   jax.ShapeDtypeStruct((rows, SSD_CONV_DIM), act_dtype),
                   jax.ShapeDtypeStruct((rows, LANES), F32),
                   jax.ShapeDtypeStruct((rows, S5_WIDTH), u_dtype)],
        compiler_params=pltpu.CompilerParams(dimension_semantics=("parallel",), vmem_limit_bytes=VMEM_LIMIT),
        name="in_proj",
    )(x2d, g, wz, wx, wdt, wu)


def _ssd_body(mask_rows, xbc_ref, dt_ref, z_ref, cinit_ref, hinit_ref, cw_ref, cb_ref, dtb_ref, alog_ref,
              dexp_ref, nrm_ref, eexp_ref, y_ref, ctail_ref, st_ref, hto_ref, xwin, hT):
    c = pl.program_id(1)
    L = SSD_CHUNK

    @pl.when(c == 0)
    def _init():
        xwin[...] = cinit_ref[0]
        hT[...] = hinit_ref[0]

    x_b = xbc_ref[0]
    x_f = x_b.astype(F32)
    taps = SSD_CONV - 1
    m_i = lax.broadcasted_iota(jnp.int32, (taps * L, L), 0)
    r_i = lax.broadcasted_iota(jnp.int32, (taps * L, L), 1)
    shift = (r_i + (taps - m_i // L) == m_i % L).astype(BF16)
    shifted = _dot(shift, x_b)
    acc = cb_ref[...] + x_f * cw_ref[taps:taps + 1, :]
    for k in range(taps):
        acc = acc + shifted[k * L:(k + 1) * L, :] * cw_ref[k:k + 1, :]
    joint = jnp.concatenate([xwin[...], x_f[0:SUBLANES, :]], axis=0)
    row8 = lax.broadcasted_iota(jnp.int32, (SUBLANES, 1), 0)
    head = acc[0:SUBLANES, :]
    for k in range(taps):
        d = taps - k
        head = head + jnp.where(row8 < d, joint[SUBLANES - d:2 * SUBLANES - d, :], 0.0) * cw_ref[k:k + 1, :]
    acc = jnp.concatenate([head, acc[SUBLANES:, :]], axis=0)
    tail = x_f[L - SUBLANES:, :]
    xwin[...] = tail
    ctail_ref[0] = tail

    xact = acc * jax.nn.sigmoid(acc)
    dt = _softplus(dt_ref[0] + dtb_ref[...])
    if mask_rows:
        valid = lax.broadcasted_iota(jnp.int32, (L, 1), 0) >= mask_rows
        xact = jnp.where(valid, xact, 0.0)
        dt = jnp.where(valid, dt, 0.0)

    a_neg = -jnp.exp(alog_ref[...])
    dA = dt * a_neg
    row_i = lax.broadcasted_iota(jnp.int32, (L, L), 0)
    col_i = lax.broadcasted_iota(jnp.int32, (L, L), 1)
    causal = row_i >= col_i
    tril = causal.astype(BF16)
    cs = _dot3_left(tril, dA)
    csT = cs.T
    dtT = dt.T
    ecs = jnp.exp(cs)
    wdec = jnp.exp(cs[L - 1:L, :] - cs) * dt
    eexp = eexp_ref[...]
    ecs_e = _dot3(ecs, eexp)
    wdec_e = _dot3(wdec, eexp)
    lane = lax.broadcasted_iota(jnp.int32, (L, LANES), 1)
    first_half = lane < SSD_HEAD_DIM

    gw = SSD_HPG * SSD_HEAD_DIM
    y_groups = []
    for g in range(SSD_GROUPS):
        b_g = xact[:, SSD_WIDTH + g * SSD_STATE: SSD_WIDTH + (g + 1) * SSD_STATE]
        c_g = xact[:, SSD_WIDTH + (SSD_GROUPS + g) * SSD_STATE: SSD_WIDTH + (SSD_GROUPS + g + 1) * SSD_STATE]
        b_b = b_g.astype(BF16)
        c_b = c_g.astype(BF16)
        cb = lax.dot_general(c_b, b_b, (((1,), (1,)), ((), ())), preferred_element_type=F32)
        xs_g = xact[:, g * gw:(g + 1) * gw]
        h_prev = hT[g]
        y_off = _dot(c_b, h_prev.astype(BF16)) * ecs_e[:, g * gw:(g + 1) * gw]
        xdec = (xs_g * wdec_e[:, g * gw:(g + 1) * gw]).astype(BF16)
        hT[g] = h_prev * ecs_e[L - 1:L, g * gw:(g + 1) * gw] + _dot(b_g.T.astype(BF16), xdec)
        pieces = []
        for j in range(SSD_HPG // 2):
            xs_pair = xs_g[:, j * LANES:(j + 1) * LANES]
            halves = (jnp.where(first_half, xs_pair, 0.0).astype(BF16),
                      jnp.where(first_half, 0.0, xs_pair).astype(BF16))
            yd = None
            for t in range(2):
                h = g * SSD_HPG + 2 * j + t
                seg = cs[:, h:h + 1] - csT[h:h + 1, :]
                lmat = jnp.exp(jnp.where(causal, seg, -jnp.inf))
                m = (cb * lmat * dtT[h:h + 1, :]).astype(BF16)
                part = _dot(m, halves[t])
                yd = part if yd is None else yd + part
            pieces.append(yd)
        y_groups.append(jnp.concatenate(pieces, axis=-1) + y_off + dexp_ref[:, g * gw:(g + 1) * gw] * xs_g)
    y = jnp.concatenate(y_groups, axis=-1)
    z = z_ref[0].astype(F32)
    y_ref[0] = _rms(y * (z * jax.nn.sigmoid(z)), nrm_ref[...]).astype(y_ref.dtype)

    @pl.when(c == pl.num_programs(1) - 1)
    def _emit():
        hto_ref[0] = hT[...]
        for g in range(SSD_GROUPS):
            t = hT[g].T
            for k in range(SSD_HPG):
                st_ref[0, g * SSD_HPG + k] = t[k * SSD_HEAD_DIM:(k + 1) * SSD_HEAD_DIM, :]


def _ssd_chunked(xbc, dt, z, cinit, hinit, cw, cb, dtb, alog, dexp, nrm, eexp, mask_rows):
    bsz, seq, _ = xbc.shape
    nc = seq // SSD_CHUNK
    gw = SSD_HPG * SSD_HEAD_DIM
    blk = lambda w: pl.BlockSpec((1, SSD_CHUNK, w), lambda b, c: (b, c, 0))
    return pl.pallas_call(
        functools.partial(_ssd_body, mask_rows),
        grid=(bsz, nc),
        in_specs=[blk(SSD_CONV_DIM), blk(LANES), blk(SSD_WIDTH),
                  pl.BlockSpec((1, SUBLANES, SSD_CONV_DIM), lambda b, c: (0, 0, 0)),
                  pl.BlockSpec((1, SSD_GROUPS, SSD_STATE, gw), lambda b, c: (0, 0, 0, 0)),
                  _full_spec(cw), _full_spec(cb), _full_spec(dtb), _full_spec(alog),
                  _full_spec(dexp), _full_spec(nrm), _full_spec(eexp)],
        out_specs=[blk(SSD_WIDTH),
                   pl.BlockSpec((1, SUBLANES, SSD_CONV_DIM), lambda b, c: (b, 0, 0)),
                   pl.BlockSpec((1, SSD_HEADS, SSD_HEAD_DIM, SSD_STATE), lambda b, c: (b, 0, 0, 0)),
                   pl.BlockSpec((1, SSD_GROUPS, SSD_STATE, gw), lambda b, c: (b, 0, 0, 0))],
        out_shape=[jax.ShapeDtypeStruct((bsz, seq, SSD_WIDTH), BF16),
                   jax.ShapeDtypeStruct((bsz, SUBLANES, SSD_CONV_DIM), F32),
                   jax.ShapeDtypeStruct((bsz, SSD_HEADS, SSD_HEAD_DIM, SSD_STATE), F32),
                   jax.ShapeDtypeStruct((bsz, SSD_GROUPS, SSD_STATE, gw), F32)],
        scratch_shapes=[pltpu.VMEM((SUBLANES, SSD_CONV_DIM), F32),
                        pltpu.VMEM((SSD_GROUPS, SSD_STATE, gw), F32)],
        compiler_params=pltpu.CompilerParams(dimension_semantics=("parallel", "arbitrary"),
                                             vmem_limit_bytes=VMEM_LIMIT),
        name="ssd_chunked",
    )(xbc, dt, z, cinit, hinit, cw, cb, dtb, alog, dexp, nrm, eexp)


def _ssd_step_prep_body(xbc_ref, c0_ref, c1_ref, c2_ref, dt_ref, cw_ref, cb_ref, dtb_ref, alog_ref,
                        xt_ref, dt_out_ref, dec_ref, bc_ref, xs_ref):
    acc = cb_ref[...]
    for k, r in enumerate((c0_ref, c1_ref, c2_ref, xbc_ref)):
        acc = acc + r[...] * cw_ref[k:k + 1, :]
    xact = acc * jax.nn.sigmoid(acc)
    xs = xact[:, :SSD_WIDTH]
    dt = _softplus(dt_ref[...] + dtb_ref[...])
    dt_out_ref[...] = dt
    dec_ref[...] = jnp.exp(dt * -jnp.exp(alog_ref[...]))
    bc_ref[...] = xact[:, SSD_WIDTH:]
    xs_ref[...] = xs
    xt_ref[...] = xs.T.astype(xt_ref.dtype)


def _ssd_step_prep(xbc, c0, c1, c2, dt, cw, cb, dtb, alog):
    n = xbc.shape[0]
    args = (xbc, c0, c1, c2, dt, cw, cb, dtb, alog)
    spec = lambda r, w: pl.BlockSpec((r, w), lambda: (0, 0))
    return pl.pallas_call(
        _ssd_step_prep_body,
        in_specs=[_full_spec(a) for a in args],
        out_specs=[spec(SSD_WIDTH, n), spec(n, LANES), spec(n, LANES), spec(n, 2 * SSD_GROUPS * SSD_STATE),
                   spec(n, SSD_WIDTH)],
        out_shape=[jax.ShapeDtypeStruct((SSD_WIDTH, n), BF16), jax.ShapeDtypeStruct((n, LANES), F32),
                   jax.ShapeDtypeStruct((n, LANES), F32),
                   jax.ShapeDtypeStruct((n, 2 * SSD_GROUPS * SSD_STATE), F32),
                   jax.ShapeDtypeStruct((n, SSD_WIDTH), F32)],
        compiler_params=pltpu.CompilerParams(vmem_limit_bytes=VMEM_LIMIT),
        name="ssd_step_prep",
    )(*args)


def _ssd_step_body(dt_ref, dec_ref, st_ref, xt_ref, bc_ref, so_ref, y_ref):
    n = xt_ref.shape[1]
    gw = SSD_HPG * SSD_HEAD_DIM
    blk = pl.program_id(0)
    seq_id = lax.broadcasted_iota(jnp.int32, (n, SSD_STATE), 0)
    sub_id = lax.broadcasted_iota(jnp.int32, (SUBLANES, gw), 0)
    base = pl.multiple_of(blk * SUBLANES, SUBLANES)
    y_acc = [jnp.zeros((SUBLANES, gw), F32) for _ in range(SSD_GROUPS)]
    for i in range(SUBLANES):
        s = blk * SUBLANES + i
        for g in range(SSD_GROUPS):
            b_all = bc_ref[:, g * SSD_STATE:(g + 1) * SSD_STATE]
            rhs = jnp.where(seq_id == s, b_all, 0.0).astype(BF16)
            outer = _dot(xt_ref[g * gw:(g + 1) * gw, :], rhs)
            news = []
            for k in range(SSD_HPG):
                h = g * SSD_HPG + k
                new = (dec_ref[s * SSD_HEADS + h] * st_ref[i, h]
                       + dt_ref[s * SSD_HEADS + h] * outer[k * SSD_HEAD_DIM:(k + 1) * SSD_HEAD_DIM, :])
                so_ref[i, h] = new
                news.append(new)
            new_g = jnp.concatenate(news, axis=0).astype(BF16)
            c_lo = (SSD_GROUPS + g) * SSD_STATE
            c_blk = bc_ref[pl.ds(base, SUBLANES), c_lo:c_lo + SSD_STATE].astype(BF16)
            r = lax.dot_general(c_blk, new_g, (((1,), (1,)), ((), ())), preferred_element_type=F32)
            y_acc[g] = y_acc[g] + jnp.where(sub_id == i, r, 0.0)
    y_ref[...] = jnp.concatenate(y_acc, axis=-1)


def _ssd_step(dt_flat, dec_flat, state, xt, bc):
    n = state.shape[0]
    st_spec = pl.BlockSpec((SUBLANES, SSD_HEADS, SSD_HEAD_DIM, SSD_STATE), lambda i, *_: (i, 0, 0, 0))
    return pl.pallas_call(
        _ssd_step_body,
        grid_spec=pltpu.PrefetchScalarGridSpec(
            num_scalar_prefetch=2,
            grid=(n // SUBLANES,),
            in_specs=[st_spec, pl.BlockSpec(xt.shape, lambda i, *_: (0, 0)),
                      pl.BlockSpec(bc.shape, lambda i, *_: (0, 0))],
            out_specs=[st_spec, pl.BlockSpec((SUBLANES, SSD_WIDTH), lambda i, *_: (i, 0))]),
        out_shape=[jax.ShapeDtypeStruct(state.shape, F32), jax.ShapeDtypeStruct((n, SSD_WIDTH), F32)],
        compiler_params=pltpu.CompilerParams(dimension_semantics=("parallel",), vmem_limit_bytes=VMEM_LIMIT),
        name="ssd_step",
    )(dt_flat, dec_flat, state, xt, bc)


def _s5_project_in(u_b16, wb_ref, store):
    kw = 16 * S5_GROUP_CH
    nw = 16 * S5_STATE
    for j in range(S5_WIDTH // kw):
        r = _dot(u_b16[:, j * kw:(j + 1) * kw], wb_ref[j])
        store(j, r[:, :nw], r[:, nw:])


def _s5_tail(hre_of, him_of, u_f32, wcr_ref, wci_ref, d_ref, wglu_ref, bglu_ref, nrm_ref):
    cols = []
    for j in range(wcr_ref.shape[0]):
        cols.append(_dot(hre_of(j).astype(BF16), wcr_ref[j]) + _dot(him_of(j).astype(BF16), wci_ref[j]))
    return _s5_finish(cols, u_f32, d_ref, wglu_ref, bglu_ref, nrm_ref)


def _s5_finish(cols, u_f32, d_ref, wglu_ref, bglu_ref, nrm_ref):
    y = jnp.concatenate(cols, axis=-1) + d_ref[...] * u_f32
    y = jax.nn.gelu(y)
    y = y * jax.nn.sigmoid(_dot(y.astype(BF16), wglu_ref[...]) + bglu_ref[...])
    return _rms(y, nrm_ref[...])


def _s5_seq_body(u_hbm, um_ref, wb_ref, abr_ref, abi_ref, wcr_ref, wci_ref, d_ref, wglu_ref, bglu_ref, nrm_ref,
                 y_hbm, sre_ref, sim_ref, ubuf, ybuf, bu, h, in_sems, out_sems):
    j = pl.program_id(0)
    last = pl.num_programs(0) - 1
    lc, bsz = ubuf.shape[1], ubuf.shape[2]
    rows = lc * bsz
    nw = 16 * S5_STATE

    def in_copy(step, b):
        return pltpu.make_async_copy(u_hbm.at[b, pl.ds(step * lc, lc), :], ubuf.at[step % 2, :, b, :],
                                     in_sems.at[step % 2, b])

    def out_copy(step, b):
        return pltpu.make_async_copy(ybuf.at[step % 2, :, b, :], y_hbm.at[b, pl.ds(step * lc, lc), :],
                                     out_sems.at[step % 2, b])

    def project_in(u_b16, nrows):
        def store(jj, re, im):
            bu[0:nrows, jj * nw:(jj + 1) * nw] = re
            bu[0:nrows, S5_LANES + jj * nw:S5_LANES + (jj + 1) * nw] = im
        _s5_project_in(u_b16, wb_ref, store)

    def scan(nsteps):
        for k in range(S5_LANES // S5_SCAN_LANES):
            sl_r = pl.ds(k * S5_SCAN_LANES, S5_SCAN_LANES)
            sl_i = pl.ds(S5_LANES + k * S5_SCAN_LANES, S5_SCAN_LANES)
            ar = abr_ref[:, sl_r]
            ai = abi_ref[:, sl_r]

            def step(l, carry):
                hr, hi = carry
                slab = pl.ds(pl.multiple_of(l * bsz, bsz), bsz)
                nr = ar * hr - ai * hi + bu[slab, sl_r]
                ni = ar * hi + ai * hr + bu[slab, sl_i]
                bu[slab, sl_r] = nr
                bu[slab, sl_i] = ni
                return nr, ni

            hr, hi = lax.fori_loop(0, nsteps, step, (h[:, sl_r], h[:, sl_i]))
            h[:, sl_r] = hr
            h[:, sl_i] = hi

    @pl.when(j == 0)
    def _first():
        for b in range(bsz):
            in_copy(0, b).start()
        h[...] = jnp.zeros_like(h)
        project_in(um_ref[...], N_META * bsz)
        scan(N_META)

    @pl.when(j < last)
    def _prefetch():
        for b in range(bsz):
            in_copy(j + 1, b).start()

    for b in range(bsz):
        in_copy(j, b).wait()
    u2 = ubuf[j % 2].reshape(rows, S5_WIDTH)
    u_b16 = u2.astype(BF16)
    kw = 16 * S5_GROUP_CH

    def project_block(jj):
        r = _dot(u_b16[:, jj * kw:(jj + 1) * kw], wb_ref[jj])
        bu[0:rows, jj * nw:(jj + 1) * nw] = r[:, :nw]
        bu[0:rows, S5_LANES + jj * nw:S5_LANES + (jj + 1) * nw] = r[:, nw:]

    def scan_block(jj):
        for k in range(nw // S5_SCAN_LANES):
            lo = jj * nw + k * S5_SCAN_LANES
            sl_r = slice(lo, lo + S5_SCAN_LANES)
            sl_i = slice(S5_LANES + lo, S5_LANES + lo + S5_SCAN_LANES)
            ar, ai = abr_ref[:, sl_r], abi_ref[:, sl_r]
            hr, hi = h[:, sl_r], h[:, sl_i]
            for l in range(lc):
                slab = slice(l * bsz, (l + 1) * bsz)
                hr, hi = (ar * hr - ai * hi + bu[slab, sl_r], ar * hi + ai * hr + bu[slab, sl_i])
                bu[slab, sl_r] = hr
                bu[slab, sl_i] = hi
            h[:, sl_r] = hr
            h[:, sl_i] = hi

    def readout_block(jj):
        return (_dot(bu[:, jj * nw:(jj + 1) * nw].astype(BF16), wcr_ref[jj])
                + _dot(bu[:, S5_LANES + jj * nw:S5_LANES + (jj + 1) * nw].astype(BF16), wci_ref[jj]))

    n_blocks = S5_WIDTH // kw
    project_block(0)
    cols = []
    for jj in range(n_blocks):
        if jj + 1 < n_blocks:
            project_block(jj + 1)
        scan_block(jj)
        cols.append(readout_block(jj))
    y = _s5_finish(cols, u2, d_ref, wglu_ref, bglu_ref, nrm_ref)
    ybuf[j % 2] = y.reshape(lc, bsz, S5_WIDTH)
    for b in range(bsz):
        out_copy(j, b).start()

    @pl.when(j > 0)
    def _wait_previous_out():
        for b in range(bsz):
            out_copy(j - 1, b).wait()

    @pl.when(j == last)
    def _emit():
        for b in range(bsz):
            out_copy(j, b).wait()
        sre_ref[...] = h[:, 0:S5_LANES]
        sim_ref[...] = h[:, S5_LANES:]


def _s5_seq(u, um, wb, abr, abi, wcr, wci, d, wglu, bglu, nrm):
    bsz, seq, _ = u.shape
    lc = S5_TIME_TILE
    consts = (um, wb, abr, abi, wcr, wci, d, wglu, bglu, nrm)
    st = pl.BlockSpec((bsz, S5_LANES), lambda j: (0, 0))
    return pl.pallas_call(
        _s5_seq_body,
        grid=(seq // lc,),
        in_specs=[pl.BlockSpec(memory_space=pl.ANY)] + [_resident_spec(a) for a in consts],
        out_specs=[pl.BlockSpec(memory_space=pl.ANY), st, st],
        out_shape=[jax.ShapeDtypeStruct((bsz, seq, S5_WIDTH), F32),
                   jax.ShapeDtypeStruct((bsz, S5_LANES), F32), jax.ShapeDtypeStruct((bsz, S5_LANES), F32)],
        scratch_shapes=[pltpu.VMEM((2, lc, bsz, S5_WIDTH), F32), pltpu.VMEM((2, lc, bsz, S5_WIDTH), F32),
                        pltpu.VMEM((lc * bsz, 2 * S5_LANES), F32), pltpu.VMEM((bsz, 2 * S5_LANES), F32),
                        pltpu.SemaphoreType.DMA((2, bsz)), pltpu.SemaphoreType.DMA((2, bsz))],
        compiler_params=pltpu.CompilerParams(dimension_semantics=("arbitrary",), vmem_limit_bytes=VMEM_LIMIT),
        name="s5_seq",
    )(u, *consts)


def _sample_post_body(yc_ref, xs_ref, z_ref, dexp_ref, snrm_ref, u_ref, hr_ref, hi_ref, wb_ref, abr_ref, abi_ref,
                      wcr_ref, wci_ref, d_ref, wglu_ref, bglu_ref, nrm_ref,
                      yssd_ref, ys5_ref, nre_ref, nim_ref):
    z = z_ref[...]
    y = yc_ref[...] + dexp_ref[...] * xs_ref[...]
    yssd_ref[...] = _rms(y * (z * jax.nn.sigmoid(z)), snrm_ref[...]).astype(yssd_ref.dtype)

    u = u_ref[...]
    nw = 16 * S5_STATE
    ar, ai = abr_ref[...], abi_ref[...]

    def store(jj, re, im):
        sl = slice(jj * nw, (jj + 1) * nw)
        h0r, h0i = hr_ref[:, sl], hi_ref[:, sl]
        nre_ref[:, sl] = ar[:, sl] * h0r - ai[:, sl] * h0i + re
        nim_ref[:, sl] = ar[:, sl] * h0i + ai[:, sl] * h0r + im

    _s5_project_in(u.astype(BF16), wb_ref, store)
    slab = lambda ref: (lambda jj: ref[:, jj * nw:(jj + 1) * nw])
    y5 = _s5_tail(slab(nre_ref), slab(nim_ref), u, wcr_ref, wci_ref, d_ref, wglu_ref, bglu_ref, nrm_ref)
    ys5_ref[...] = y5.astype(ys5_ref.dtype)


def _sample_post(yc, xs, z, dexp, snrm, u, h0r, h0i, wb, abr1, abi1, wcr, wci, d, wglu, bglu, nrm):
    n = yc.shape[0]
    args = (yc, xs, z, dexp, snrm, u, h0r, h0i, wb, abr1, abi1, wcr, wci, d, wglu, bglu, nrm)
    spec = lambda w: pl.BlockSpec((n, w), lambda: (0, 0))
    return pl.pallas_call(
        _sample_post_body,
        in_specs=[_full_spec(a) for a in args],
        out_specs=[spec(SSD_WIDTH), spec(S5_WIDTH), spec(S5_LANES), spec(S5_LANES)],
        out_shape=[jax.ShapeDtypeStruct((n, SSD_WIDTH), BF16), jax.ShapeDtypeStruct((n, S5_WIDTH), BF16),
                   jax.ShapeDtypeStruct((n, S5_LANES), F32), jax.ShapeDtypeStruct((n, S5_LANES), F32)],
        compiler_params=pltpu.CompilerParams(vmem_limit_bytes=VMEM_LIMIT),
        name="sample_post",
    )(*args)


def _mix_route_body(n_blocks, n_sorted, xp_ref, ysp_ref, y5p_ref, xs_ref, yss_ref, y5s_ref, *refs):
    consts = refs[:6]
    x1_ref, xn_hbm, rt_ref, pos_ref, meta_ref, carry, fields, xbuf, sems = refs[6:]
    i = pl.program_id(0)
    tm, n_sample = xp_ref.shape[0], xs_ref.shape[0]
    col0 = pl.multiple_of(i * tm, LANES)

    def xn_copy(step, rows, j):
        return pltpu.make_async_copy(xbuf.at[step % 2, pl.ds(0, rows), pl.ds(j * LANES, LANES)],
                                     xn_hbm.at[pl.ds(step * tm, rows), j, :], sems.at[step % 2, j])

    @pl.when(i == 0)
    def _init():
        carry[...] = jnp.zeros_like(carry)

    @pl.when(i < n_blocks)
    def _prompt_rows():
        _mix_route_compute(xp_ref, ysp_ref, y5p_ref, *consts, x1_ref, rt_ref, carry, xbuf.at[i % 2], fields, col0)
        for j in range(PACK_ROWS):
            xn_copy(i, tm, j).start()

    @pl.when(i == n_blocks)
    def _sample_rows():
        _mix_route_compute(xs_ref, yss_ref, y5s_ref, *consts, x1_ref, rt_ref, carry, xbuf.at[i % 2], fields, col0)
        for j in range(PACK_ROWS):
            xn_copy(i, n_sample, j).start()
        _route_layout(carry, fields, pos_ref, meta_ref, n_sorted)
        for j in range(PACK_ROWS):
            xn_copy(i, n_sample, j).wait()

    @pl.when(i > 0)
    def _wait_previous_rows():
        for j in range(PACK_ROWS):
            xn_copy(i - 1, tm, j).wait()


def _route_layout(carry, fields, pos_ref, meta_ref, n_sorted):
    counts = carry[...]
    tiles_per = jnp.floor((counts + (MOE_TILE - 1)) * (1.0 / MOE_TILE))
    upto = lax.broadcasted_iota(jnp.int32, (LANES, LANES), 0) <= lax.broadcasted_iota(jnp.int32, (LANES, LANES), 1)
    tile_end = _dot(tiles_per.astype(BF16), upto.astype(BF16))
    pstart = (tile_end - tiles_per) * MOE_TILE
    n_used = tile_end[:, MOE_EXPERTS - 1:MOE_EXPERTS]

    f = fields[...]
    first_row = jnp.zeros_like(f)
    tile_id = jnp.minimum(lax.broadcasted_iota(jnp.int32, meta_ref.shape, 1).astype(F32), n_used - 1.0)
    tile_expert = jnp.zeros(meta_ref.shape, F32)
    for e in range(MOE_EXPERTS):
        first_row = first_row + jnp.where(f == float(e), pstart[:, e:e + 1], 0.0)
        tile_expert = tile_expert + jnp.where(tile_end[:, e:e + 1] <= tile_id, 1.0, 0.0)
    pos = first_row + pltpu.roll(f, shift=4, axis=0)
    pos_ref[...] = jnp.clip(pos, 0.0, n_sorted - 1.0).astype(jnp.int32)
    is_row0 = lax.broadcasted_iota(jnp.int32, meta_ref.shape, 0) == 0
    meta_ref[...] = jnp.where(is_row0, tile_expert, n_used).astype(jnp.int32)


def _mix_route_compute(x_ref, ys_ref, y5_ref, wa_ref, wb_ref, nf_ref, wrh_ref, wrl_ref, br_ref,
                       x1_ref, rt_ref, carry, xn_buf, fields, col0):
    rows = x_ref.shape[0]
    x1 = x_ref[...] + _dot(ys_ref[...], wa_ref[...]) + _dot(y5_ref[...].astype(BF16), wb_ref[...])
    x1_ref[0:rows, :] = x1
    xn = _rms(x1, nf_ref[...])
    xn_buf[0:rows, :] = _pack_bf16_pairs(xn)

    xh = xn.astype(BF16)
    xl = (xn - xh.astype(F32)).astype(BF16)
    logits = _dot(xh, wrh_ref[...]) + _dot(xl, wrh_ref[...]) + _dot(xh, wrl_ref[...]) + br_ref[...]
    tm = logits.shape[0]
    lane = lax.broadcasted_iota(jnp.int32, logits.shape, 1).astype(F32)
    neg = -jnp.inf
    big = float(LANES)

    def first_max(v):
        m = jnp.max(v, axis=-1, keepdims=True)
        return m, jnp.min(jnp.where(v == m, lane, big), axis=-1, keepdims=True)

    coarse = lane < MOE_GROUPS
    mc, gsel = first_max(jnp.where(coarse, logits, neg))
    psel = 1.0 / jnp.sum(jnp.where(coarse, jnp.exp(logits - mc), 0.0), axis=-1, keepdims=True)
    lo = MOE_GROUPS + MOE_EPG * gsel
    lf = jnp.where((lane >= lo) & (lane < lo + MOE_EPG), logits, neg)
    m1, i1 = first_max(lf)
    m2, i2 = first_max(jnp.where(lane == i1, neg, lf))
    e2 = jnp.exp(m2 - m1)
    g1 = psel / (1.0 + e2)
    g2 = psel * e2 / (1.0 + e2)
    e_a, e_b = i1 - MOE_GROUPS, i2 - MOE_GROUPS

    pick_a, pick_b = lane == e_a, lane == e_b
    picks = jnp.where(pick_a | pick_b, 1.0, 0.0)
    earlier = lax.broadcasted_iota(jnp.int32, (tm, tm), 0) > lax.broadcasted_iota(jnp.int32, (tm, tm), 1)
    prior = _dot(earlier.astype(BF16), picks.astype(BF16)) + carry[...]
    rank_a = jnp.sum(jnp.where(pick_a, prior, 0.0), axis=-1, keepdims=True)
    rank_b = jnp.sum(jnp.where(pick_b, prior, 0.0), axis=-1, keepdims=True)
    carry[...] = prior[tm - 1:tm, :] + picks[tm - 1:tm, :]

    out = jnp.zeros_like(logits)
    for k, v in enumerate((e_a, e_b, g1, g2, rank_a, rank_b)):
        out = jnp.where(lane == float(k), v, out)
    rt_ref[0:rows, :] = out
    fields[:, pl.ds(col0, rows)] = out.T[0:SUBLANES, :]


def _mix_route(prompt, sample, consts, tm, n_tiles):
    n_prompt, n_sample = prompt[0].shape[0], sample[0].shape[0]
    assert n_prompt % tm == 0 and n_sample <= tm
    n_blocks = n_prompt // tm
    total_rows = n_prompt + n_sample
    row = lambda w: pl.BlockSpec((tm, w), lambda i: (jnp.minimum(i, n_blocks - 1), 0))
    out_row = lambda w: pl.BlockSpec((tm, w), lambda i: (i, 0))
    assert total_rows % LANES == 0 and n_tiles <= 2 * LANES
    whole = lambda shape: pl.BlockSpec(shape, lambda i: (0, 0))
    return pl.pallas_call(
        functools.partial(_mix_route_body, n_blocks, n_tiles * MOE_TILE),
        grid=(n_blocks + 1,),
        in_specs=([row(D_MODEL), row(SSD_WIDTH), row(S5_WIDTH)] + [_full_spec(a) for a in sample]
                  + [_full_spec(a) for a in consts]),
        out_specs=[out_row(D_MODEL), pl.BlockSpec(memory_space=pl.ANY), out_row(LANES),
                   whole((SUBLANES, total_rows)), whole((SUBLANES, 2 * LANES))],
        out_shape=[jax.ShapeDtypeStruct((total_rows, D_MODEL), F32),
                   jax.ShapeDtypeStruct((total_rows, PACK_ROWS, LANES), jnp.uint32),
                   jax.ShapeDtypeStruct((total_rows, LANES), F32),
                   jax.ShapeDtypeStruct((SUBLANES, total_rows), jnp.int32),
                   jax.ShapeDtypeStruct((SUBLANES, 2 * LANES), jnp.int32)],
        scratch_shapes=[pltpu.VMEM((1, LANES), F32), pltpu.VMEM((SUBLANES, total_rows), F32),
                        pltpu.VMEM((2, tm, D_MODEL // 2), jnp.uint32), pltpu.SemaphoreType.DMA((2, PACK_ROWS))],
        compiler_params=pltpu.CompilerParams(dimension_semantics=("arbitrary",), vmem_limit_bytes=VMEM_LIMIT),
        name="mix_route",
    )(*prompt, *sample, *consts)


def _sc_mesh():
    return plsc.VectorSubcoreMesh(core_axis_name="c", subcore_axis_name="s")


def _sc_worker():
    return lax.axis_index("s") * SC_CORES + lax.axis_index("c")


def _sc_dispatch(xn, pos_a, pos_b, n_rows):
    n_tok = xn.shape[0]
    ch = SC_DISPATCH_ROWS
    n_chunks = n_tok // ch
    assert n_tok % ch == 0 and n_chunks >= SC_WORKERS
    max_mine = -(-n_chunks // SC_WORKERS)
    row_shape, dtype = xn.shape[1:], xn.dtype
    stage = [pltpu.VMEM((ch,), jnp.int32), pltpu.VMEM((ch,), jnp.int32), pltpu.VMEM((ch,) + row_shape, dtype),
             pltpu.SemaphoreType.DMA]

    @functools.partial(
        pl.kernel, mesh=_sc_mesh(),
        out_type=jax.ShapeDtypeStruct((n_rows,) + row_shape, dtype),
        scratch_types=stage + stage + [pltpu.SemaphoreType.DMA])
    def push(xn_hbm, pa_hbm, pb_hbm, xs_hbm, ia0, ib0, rows0, lsem0, ia1, ib1, rows1, lsem1, ssem):
        wid = _sc_worker()
        mine = (n_chunks - wid + SC_WORKERS - 1) // SC_WORKERS
        bufs = ((ia0, ib0, rows0, lsem0), (ia1, ib1, rows1, lsem1))

        def loads(t, b):
            ia, ib, rows, sem = bufs[b]
            off = pl.multiple_of((wid + t * SC_WORKERS) * ch, ch)
            return (pltpu.make_async_copy(pa_hbm.at[pl.ds(off, ch)], ia, sem),
                    pltpu.make_async_copy(pb_hbm.at[pl.ds(off, ch)], ib, sem),
                    pltpu.make_async_copy(xn_hbm.at[pl.ds(off, ch)], rows, sem))

        def stage_in(t, b):
            for c in loads(t, b):
                c.start()

        def scatter(t, b):
            ia, ib, rows, _ = bufs[b]
            for c in loads(t, b):
                c.wait()
            first = pltpu.async_copy(rows, xs_hbm.at[ia], ssem)
            second = pltpu.async_copy(rows, xs_hbm.at[ib], ssem)
            first.wait()
            second.wait()

        stage_in(0, 0)

        @pl.loop(0, (max_mine + 1) // 2)
        def _(p):
            t = 2 * p

            @pl.when(t + 1 < mine)
            def _():
                stage_in(t + 1, 1)

            @pl.when(t < mine)
            def _():
                scatter(t, 0)

            @pl.when(t + 2 < mine)
            def _():
                stage_in(t + 2, 0)

            @pl.when(t + 1 < mine)
            def _():
                scatter(t + 1, 1)

    return push(xn, pos_a, pos_b)


def _sc_collect(ysorted, pos_flat, ch):
    n_pick = pos_flat.shape[0]
    per_worker = n_pick // SC_WORKERS
    n_chunks = per_worker // ch
    assert n_pick % SC_WORKERS == 0 and per_worker % ch == 0
    row_shape, dtype = ysorted.shape[1:], ysorted.dtype

    @functools.partial(
        pl.kernel, mesh=_sc_mesh(),
        out_type=jax.ShapeDtypeStruct((n_pick,) + row_shape, dtype),
        scratch_types=[pltpu.VMEM((ch,), jnp.int32), pltpu.VMEM((ch,), jnp.int32),
                       pltpu.VMEM((ch,) + row_shape, dtype), pltpu.VMEM((ch,) + row_shape, dtype),
                       pltpu.SemaphoreType.DMA, pltpu.SemaphoreType.DMA])
    def pull(ys_hbm, pos_hbm, out_hbm, idx0, idx1, rows0, rows1, sem0, sem1):
        base = _sc_worker() * per_worker
        bufs = ((idx0, rows0, sem0), (idx1, rows1, sem1))

        def offset(j):
            return pl.multiple_of(base + j * ch, SUBLANES)

        def fetch(j, b):
            idx, rows, sem = bufs[b]
            pltpu.sync_copy(pos_hbm.at[pl.ds(offset(j), ch)], idx)
            pltpu.async_copy(ys_hbm.at[idx], rows, sem)

        def flush(j, b):
            idx, rows, sem = bufs[b]
            pltpu.make_async_copy(ys_hbm.at[idx], rows, sem).wait()
            pltpu.sync_copy(rows, out_hbm.at[pl.ds(offset(j), ch)])

        fetch(0, 0)

        @pl.loop(0, n_chunks // 2)
        def _(p):
            j = 2 * p
            fetch(j + 1, 1)
            flush(j, 0)

            @pl.when(j + 2 < n_chunks)
            def _():
                fetch(j + 2, 0)

            flush(j + 1, 1)

        if n_chunks % 2:
            flush(n_chunks - 1, 0)

    return pull(ysorted, pos_flat)


def _moe_ffn_body(te_ref, nused_ref, x_ref, wg_ref, wu_ref, wd_ref, y_ref, wgb, wub, wdb):
    i = pl.program_id(0)

    @pl.when(i >= nused_ref[0])
    def _unused_tile():
        y_ref[...] = jnp.zeros_like(y_ref)

    @pl.when(i < nused_ref[0])
    def _tile():
        @pl.when((i == 0) | (te_ref[i] != te_ref[jnp.maximum(i - 1, 0)]))
        def _cast_weights():
            wgb[...] = wg_ref[0].astype(BF16)
            wub[...] = wu_ref[0].astype(BF16)
            wdb[...] = wd_ref[0].astype(BF16)

        x = _unpack_bf16_pairs(x_ref, MOE_TILE).astype(BF16)
        gate = _dot(x, wgb[...])
        hmid = (gate * jax.nn.sigmoid(gate)) * _dot(x, wub[...])
        y = _dot(hmid.astype(BF16), wdb[...])
        packed = _pack_bf16_pairs(y)
        for j in range(PACK_ROWS):
            y_ref[pl.ds(j, MOE_TILE, stride=PACK_ROWS), :] = packed[:, j * LANES:(j + 1) * LANES]


def _moe_ffn(tile_expert, n_used, xsorted, w_gate, w_up, w_down):
    n_tiles = tile_expert.shape[0]
    wspec = lambda s: pl.BlockSpec((1,) + s, lambda i, te, nu: (te[i], 0, 0))
    tile = lambda rows, imap: pl.BlockSpec((MOE_TILE * rows, LANES), imap)
    return pl.pallas_call(
        _moe_ffn_body,
        grid_spec=pltpu.PrefetchScalarGridSpec(
            num_scalar_prefetch=2,
            grid=(n_tiles,),
            in_specs=[tile(PACK_ROWS, lambda i, te, nu: (jnp.clip(i, 0, jnp.maximum(nu[0] - 1, 0)), 0)),
                      wspec((D_MODEL, MOE_D_FF)), wspec((D_MODEL, MOE_D_FF)), wspec((MOE_D_FF, D_MODEL))],
            out_specs=tile(PACK_ROWS, lambda i, te, nu: (i, 0)),
            scratch_shapes=[pltpu.VMEM((D_MODEL, MOE_D_FF), BF16), pltpu.VMEM((D_MODEL, MOE_D_FF), BF16),
                            pltpu.VMEM((MOE_D_FF, D_MODEL), BF16)]),
        out_shape=jax.ShapeDtypeStruct((n_tiles * MOE_TILE * PACK_ROWS, LANES), jnp.uint32),
        compiler_params=pltpu.CompilerParams(dimension_semantics=("arbitrary",), vmem_limit_bytes=VMEM_LIMIT),
        name="moe_ffn",
    )(tile_expert, n_used, xsorted, w_gate, w_up, w_down)


def _combine_body(x1_ref, rt_ref, ya_ref, yb_ref, nf_ref, *rest):
    out_ref = rest[-1]
    rt = rt_ref[...]
    x1 = x1_ref[...]
    tm = x1.shape[0]

    x2 = (x1 + rt[:, 2:3] * _unpack_bf16_pairs(ya_ref.at[0], tm)
          + rt[:, 3:4] * _unpack_bf16_pairs(yb_ref.at[0], tm))
    out_ref[...] = _rms(x2, nf_ref[...])


def _combine(x1, rt, y_picks, nf, tm, rows, x_block, y_block, out_rows, out_block, out_buf=None):
    row = lambda w: pl.BlockSpec((tm, w), lambda i: (i + x_block, 0))
    pick = lambda k: pl.BlockSpec((1, tm * PACK_ROWS, LANES), lambda i: (k, i + y_block, 0))
    in_specs = [row(D_MODEL), row(LANES), pick(0), pick(1), pl.BlockSpec((1, D_MODEL), lambda i: (0, 0))]
    args = [x1, rt, y_picks, y_picks, nf]
    aliases = {}
    if out_buf is not None:
        in_specs.append(pl.BlockSpec(memory_space=pl.ANY))
        aliases[len(args)] = 0
        args.append(out_buf)
    return pl.pallas_call(
        _combine_body,
        grid=(rows // tm,),
        in_specs=in_specs,
        out_specs=pl.BlockSpec((tm, D_MODEL), lambda i: (i + out_block, 0)),
        out_shape=jax.ShapeDtypeStruct((out_rows, D_MODEL), F32),
        input_output_aliases=aliases,
        compiler_params=pltpu.CompilerParams(dimension_semantics=("parallel",), vmem_limit_bytes=VMEM_LIMIT),
        name="moe_combine",
    )(*args)


def _s5_tables(a_re, a_im, log_dt, b_re, b_im, c_re, c_im):
    dt = jnp.exp(log_dt)[:, None]
    mag = jnp.exp(a_re * dt)
    ab_re = mag * jnp.cos(a_im * dt)
    ab_im = mag * jnp.sin(a_im * dt)
    den = a_re * a_re + a_im * a_im
    nr = ab_re - 1.0
    q_re = (nr * a_re + ab_im * a_im) / den
    q_im = (ab_im * a_re - nr * a_im) / den
    bb_re = q_re[..., None] * b_re - q_im[..., None] * b_im
    bb_im = q_re[..., None] * b_im + q_im[..., None] * b_re
    nblk = S5_GROUPS // 16
    kw, nw = 16 * S5_GROUP_CH, 16 * S5_STATE
    same_group = (jnp.arange(kw)[:, None] // S5_GROUP_CH) == (jnp.arange(nw)[None, :] // S5_STATE)

    def in_map(bb):
        rows = bb.reshape(nblk, 16, S5_STATE, S5_GROUP_CH).transpose(0, 1, 3, 2).reshape(nblk, kw, S5_STATE)
        return jnp.where(same_group, jnp.tile(rows, (1, 1, 16)), 0.0)

    def out_map(cc):
        cols = cc.reshape(nblk, 16, S5_GROUP_CH, S5_STATE).transpose(0, 3, 1, 2).reshape(nblk, S5_STATE, kw)
        return jnp.where(same_group.T, jnp.tile(cols, (1, 16, 1)), 0.0)

    wb = jnp.concatenate([in_map(bb_re), in_map(bb_im)], axis=-1).astype(BF16)
    return (wb, ab_re.reshape(1, S5_LANES), ab_im.reshape(1, S5_LANES),
            out_map(c_re).astype(BF16), out_map(-c_im).astype(BF16))


def kernel(x_prompt, x_sample, state_ssd_conv, state_ssd_ssm, state_s5_re, state_s5_im, meta_tokens, norm_mix, w_in, conv_w, conv_b, dt_bias, a_log, d_ssd, ssd_norm, s5_a_re, s5_a_im, s5_log_dt, s5_b_re, s5_b_im, s5_c_re, s5_c_im, s5_d, w_glu, b_glu, s5_norm, w_out, norm_ffn, router_coarse_w, router_coarse_b, router_fine_w, router_fine_b, w_gate, w_up, w_down, norm_final):
    bp, seq, _ = x_prompt.shape
    bs = x_sample.shape[0]
    n_prompt = bp * seq
    n_tok = n_prompt + bs
    row2 = lambda v: v.reshape(1, -1)
    pad_heads = lambda v: jnp.pad(v, (0, LANES - SSD_HEADS)).reshape(1, LANES)

    w = w_in[0]
    o1, o2, o3 = SSD_WIDTH, SSD_WIDTH + SSD_CONV_DIM, SSD_WIDTH + SSD_CONV_DIM + SSD_HEADS
    wz, wx, wu = w[:, :o1].astype(BF16), w[:, o1:o2].astype(BF16), w[:, o3:].astype(BF16)
    wdt = jnp.pad(w[:, o2:o3], ((0, 0), (0, LANES - SSD_HEADS))).astype(BF16)
    g_mix = row2(norm_mix[0])
    cw, cb = conv_w[0], row2(conv_b[0])
    dtb, alog = pad_heads(dt_bias[0]), pad_heads(a_log[0])
    dexp = row2(jnp.repeat(d_ssd[0], SSD_HEAD_DIM))
    snrm = row2(ssd_norm[0])
    eexp = (jnp.arange(LANES)[:, None] == (jnp.arange(SSD_WIDTH) // SSD_HEAD_DIM)[None, :]).astype(BF16)
    wb5, ab_re, ab_im, wcr, wci = _s5_tables(s5_a_re[0], s5_a_im[0], s5_log_dt[0], s5_b_re[0], s5_b_im[0],
                                             s5_c_re[0], s5_c_im[0])
    d5, wglu, bglu, nrm5 = row2(s5_d[0]), w_glu[0].astype(BF16), row2(b_glu[0]), row2(s5_norm[0])
    wo_a, wo_b = w_out[0][:SSD_WIDTH].astype(BF16), w_out[0][SSD_WIDTH:].astype(BF16)
    w_r = jnp.concatenate([router_coarse_w[0], router_fine_w[0].transpose(1, 0, 2).reshape(D_MODEL, MOE_EXPERTS)], axis=1)
    w_r = jnp.pad(w_r, ((0, 0), (0, LANES - w_r.shape[1])))
    wrh = w_r.astype(BF16)
    wrl = (w_r - wrh.astype(F32)).astype(BF16)
    b_r = jnp.concatenate([router_coarse_b[0], router_fine_b[0].reshape(-1)])
    b_r = jnp.pad(b_r, (0, LANES - b_r.shape[0])).reshape(1, LANES)

    zp, xbcp, dtp, up = _in_proj(x_prompt.reshape(n_prompt, D_MODEL), g_mix, wz, wx, wdt, wu, TOK_TILE, BF16, F32)
    xsm = jnp.concatenate([x_sample.reshape(bs, D_MODEL), meta_tokens], axis=0)
    zs, xbcs, dts, us = _in_proj(xsm, g_mix, wz, wx, wdt, wu, xsm.shape[0], F32, F32)

    front = SSD_CHUNK - N_META
    padf = lambda a: jnp.pad(a[bs:], ((front, 0), (0, 0)))[None]
    gw = SSD_HPG * SSD_HEAD_DIM
    ssd_consts = (cw, cb, dtb, alog, dexp, snrm, eexp)
    _, ctail_m, _, ht_m = _ssd_chunked(
        padf(xbcs).astype(BF16), padf(dts), jnp.zeros((1, SSD_CHUNK, SSD_WIDTH), F32),
        jnp.zeros((1, SUBLANES, SSD_CONV_DIM), F32), jnp.zeros((1, SSD_GROUPS, SSD_STATE, gw), F32),
        *ssd_consts, mask_rows=front)
    y_ssd_p, ctail_p, ssm_p, _ = _ssd_chunked(
        xbcp.reshape(bp, seq, SSD_CONV_DIM), dtp.reshape(bp, seq, LANES), zp.reshape(bp, seq, SSD_WIDTH),
        ctail_m, ht_m, *ssd_consts, mask_rows=0)

    abr8, abi8 = jnp.broadcast_to(ab_re, (bp, S5_LANES)), jnp.broadcast_to(ab_im, (bp, S5_LANES))
    um8 = jnp.repeat(us[bs:], bp, axis=0).astype(BF16)
    y_s5_p, s5re_p, s5im_p = _s5_seq(up.reshape(bp, seq, S5_WIDTH), um8, wb5, abr8, abi8,
                                     wcr, wci, d5, wglu, bglu, nrm5)

    cst = state_ssd_conv[0]
    xt_s, dt_s, dec_s, bc, xs_s = _ssd_step_prep(xbcs[:bs], cst[:, 0], cst[:, 1], cst[:, 2], dts[:bs],
                                                 cw, cb, dtb, alog)
    ssm_s, y_core = _ssd_step(dt_s[:, :SSD_HEADS].reshape(-1), dec_s[:, :SSD_HEADS].reshape(-1),
                              state_ssd_ssm[0], xt_s, bc)
    y_ssd_s, y_s5_s, s5re_s, s5im_s = _sample_post(
        y_core, xs_s, zs[:bs], dexp, snrm, us[:bs], state_s5_re[0].reshape(bs, S5_LANES),
        state_s5_im[0].reshape(bs, S5_LANES), wb5, ab_re, ab_im, wcr, wci, d5, wglu, bglu, nrm5)

    route_consts = (wo_a, wo_b, row2(norm_ffn[0]), wrh, wrl, b_r)
    n_tiles = -(-2 * n_tok // MOE_TILE) + MOE_EXPERTS
    x1, xn, rt, pos, meta = _mix_route(
        (x_prompt.reshape(n_prompt, D_MODEL), y_ssd_p.reshape(n_prompt, SSD_WIDTH), y_s5_p.reshape(n_prompt, S5_WIDTH)),
        (x_sample.reshape(bs, D_MODEL), y_ssd_s, y_s5_s), route_consts, TOK_TILE, n_tiles)

    pos_a, pos_b = pos[0], pos[1]
    tile_expert, n_used = meta[0, :n_tiles], meta[1, :1]
    xsorted = _sc_dispatch(xn, pos_a, pos_b, n_tiles * MOE_TILE)
    ysorted = _moe_ffn(tile_expert, n_used, xsorted.reshape(-1, LANES), w_gate[0], w_up[0], w_down[0])
    nfin = row2(norm_final)

    half = n_prompt // 2

    def collect(lo, hi, ch):
        picks = jnp.concatenate([pos_a[lo:hi], pos_b[lo:hi]])
        packed_rows = ysorted.reshape(-1, PACK_ROWS, LANES)
        return _sc_collect(packed_rows, picks, ch).reshape(2, (hi - lo) * PACK_ROWS, LANES)

    picks_1 = collect(0, half, SC_COLLECT_ROWS[0])
    picks_2 = collect(half, n_tok, SC_COLLECT_ROWS[1])
    blocks = half // MOE_TILE
    y_p = _combine(x1, rt, picks_1, nfin, MOE_TILE, half, 0, 0, n_prompt, 0)
    y_p = _combine(x1, rt, picks_2, nfin, MOE_TILE, half, blocks, 0, n_prompt, blocks, out_buf=y_p)
    y_s = _combine(x1, rt, picks_2, nfin, bs, bs, n_prompt // bs, half // bs, bs, 0)

    s5_state = lambda a, b: a.reshape(1, b, S5_GROUPS, S5_STATE)
    new_conv_s = jnp.stack([cst[:, 1], cst[:, 2], xbcs[:bs]], axis=1)[None]
    return (y_p.reshape(bp, seq, D_MODEL), y_s.reshape(bs, 1, D_MODEL),
            ctail_p[:, SUBLANES - (SSD_CONV - 1):][None], ssm_p[None], s5_state(s5re_p, bp), s5_state(s5im_p, bp),
            new_conv_s, ssm_s[None], s5_state(s5re_s, bs), s5_state(s5im_s, bs))
```

```python
import functools

import jax
import jax.numpy as jnp
from jax import lax
from jax.experimental import pallas as pl
from jax.experimental.pallas import tpu as pltpu
from jax.experimental.pallas import tpu_sc as plsc

F32, BF16 = jnp.float32, jnp.bfloat16

D_MODEL = 1024
N_META = 16
SSD_WIDTH = 1024
SSD_HEAD_DIM = 64
SSD_HEADS = 16
SSD_GROUPS = 2
SSD_HPG = SSD_HEADS // SSD_GROUPS
SSD_STATE = 128
SSD_CONV = 4
SSD_CHUNK = 128
SSD_CONV_DIM = SSD_WIDTH + 2 * SSD_GROUPS * SSD_STATE
S5_WIDTH = 1024
S5_GROUP_CH = 16
S5_GROUPS = 64
S5_STATE = 64
S5_LANES = S5_GROUPS * S5_STATE
MOE_GROUPS = 4
MOE_EPG = 8
MOE_EXPERTS = MOE_GROUPS * MOE_EPG
MOE_D_FF = 512
EPS = 1e-6

LANES = 128
SUBLANES = 8
VMEM_LIMIT = 56 * 1024 * 1024

S5_TIME_TILE = 64
S5_SCAN_LANES = 512
MOE_TILE = 256
SLAB_ROWS = D_MODEL // LANES
PACK_ROWS = SLAB_ROWS // 2
SC_CORES = 2
SC_SUBCORES = 16
SC_WORKERS = SC_CORES * SC_SUBCORES
SC_DISPATCH_ROWS = 32
SC_COLLECT_ROWS = (32, 40)
TOK_TILE = 512


def _dot(a, b):
    return jnp.dot(a, b, preferred_element_type=F32)


def _rms(x, g):
    return x * lax.rsqrt(jnp.mean(x * x, axis=-1, keepdims=True) + EPS) * g


def _softplus(x):
    return jnp.maximum(x, 0.0) + jnp.log1p(jnp.exp(-jnp.abs(x)))


def _split3(x):
    hi = x.astype(BF16)
    r = x - hi.astype(F32)
    mid = r.astype(BF16)
    lo = (r - mid.astype(F32)).astype(BF16)
    return hi, mid, lo


def _dot3(x, w):
    hi, mid, lo = _split3(x)
    return _dot(hi, w) + _dot(mid, w) + _dot(lo, w)


def _dot3_left(w, x):
    hi, mid, lo = _split3(x)
    return _dot(w, hi) + _dot(w, mid) + _dot(w, lo)


def _pack_bf16_pairs(x):
    bits = pltpu.bitcast(x.astype(BF16).astype(F32), jnp.uint32)
    half = x.shape[1] // 2
    return (bits[:, :half] & jnp.uint32(0xFFFF0000)) | (bits[:, half:] >> jnp.uint32(16))


def _unpack_bf16_pairs(ref, rows):
    words = [ref[pl.ds(j, rows, stride=PACK_ROWS), :] for j in range(PACK_ROWS)]
    high = [pltpu.bitcast(w & jnp.uint32(0xFFFF0000), F32) for w in words]
    low = [pltpu.bitcast(w << jnp.uint32(16), F32) for w in words]
    return jnp.concatenate(high + low, axis=-1)


def _full_spec(a):
    nd = a.ndim
    return pl.BlockSpec(a.shape, lambda *_: (0,) * nd)


def _resident_spec(a):
    nd = a.ndim
    return pl.BlockSpec(a.shape, lambda *_: (0,) * nd, pipeline_mode=pl.Buffered(1))


def _in_proj_body(x_ref, g_ref, wz_ref, wx_ref, wdt_ref, wu_ref, z_ref, xbc_ref, dt_ref, u_ref):
    xb = _rms(x_ref[...], g_ref[...]).astype(BF16)
    z_ref[...] = _dot(xb, wz_ref[...]).astype(z_ref.dtype)
    xbc_ref[...] = _dot(xb, wx_ref[...]).astype(xbc_ref.dtype)
    dt_ref[...] = _dot(xb, wdt_ref[...])
    u_ref[...] = _dot(xb, wu_ref[...]).astype(u_ref.dtype)


def _in_proj(x2d, g, wz, wx, wdt, wu, tm, act_dtype, u_dtype):
    rows = x2d.shape[0]
    row = lambda w: pl.BlockSpec((tm, w), lambda i: (i, 0))
    return pl.pallas_call(
        _in_proj_body,
        grid=(rows // tm,),
        in_specs=[row(D_MODEL), _full_spec(g), _full_spec(wz), _full_spec(wx), _full_spec(wdt), _full_spec(wu)],
        out_specs=[row(SSD_WIDTH), row(SSD_CONV_DIM), row(LANES), row(S5_WIDTH)],
        out_shape=[jax.ShapeDtypeStruct((rows, SSD_WIDTH), act_dtype),
                   jax.ShapeDtypeStruct((rows, SSD_CONV_DIM), act_dtype),
                   jax.ShapeDtypeStruct((rows, LANES), F32),
                   jax.ShapeDtypeStruct((rows, S5_WIDTH), u_dtype)],
        compiler_params=pltpu.CompilerParams(dimension_semantics=("parallel",), vmem_limit_bytes=VMEM_LIMIT),
        name="in_proj",
    )(x2d, g, wz, wx, wdt, wu)


def _ssd_body(mask_rows, xbc_ref, dt_ref, z_ref, cinit_ref, hinit_ref, cw_ref, cb_ref, dtb_ref, alog_ref,
              dexp_ref, nrm_ref, eexp_ref, y_ref, ctail_ref, st_ref, hto_ref, xwin, hT):
    c = pl.program_id(1)
    L = SSD_CHUNK

    @pl.when(c == 0)
    def _init():
        xwin[...] = cinit_ref[0]
        hT[...] = hinit_ref[0]

    x_b = xbc_ref[0]
    x_f = x_b.astype(F32)
    taps = SSD_CONV - 1
    m_i = lax.broadcasted_iota(jnp.int32, (taps * L, L), 0)
    r_i = lax.broadcasted_iota(jnp.int32, (taps * L, L), 1)
    shift = (r_i + (taps - m_i // L) == m_i % L).astype(BF16)
    shifted = _dot(shift, x_b)
    acc = cb_ref[...] + x_f * cw_ref[taps:taps + 1, :]
    for k in range(taps):
        acc = acc + shifted[k * L:(k + 1) * L, :] * cw_ref[k:k + 1, :]
    joint = jnp.concatenate([xwin[...], x_f[0:SUBLANES, :]], axis=0)
    row8 = lax.broadcasted_iota(jnp.int32, (SUBLANES, 1), 0)
    head = acc[0:SUBLANES, :]
    for k in range(taps):
        d = taps - k
        head = head + jnp.where(row8 < d, joint[SUBLANES - d:2 * SUBLANES - d, :], 0.0) * cw_ref[k:k + 1, :]
    acc = jnp.concatenate([head, acc[SUBLANES:, :]], axis=0)
    tail = x_f[L - SUBLANES:, :]
    xwin[...] = tail
    ctail_ref[0] = tail

    xact = acc * jax.nn.sigmoid(acc)
    dt = _softplus(dt_ref[0] + dtb_ref[...])
    if mask_rows:
        valid = lax.broadcasted_iota(jnp.int32, (L, 1), 0) >= mask_rows
        xact = jnp.where(valid, xact, 0.0)
        dt = jnp.where(valid, dt, 0.0)

    a_neg = -jnp.exp(alog_ref[...])
    dA = dt * a_neg
    row_i = lax.broadcasted_iota(jnp.int32, (L, L), 0)
    col_i = lax.broadcasted_iota(jnp.int32, (L, L), 1)
    causal = row_i >= col_i
    tril = causal.astype(BF16)
    cs = _dot3_left(tril, dA)
    csT = cs.T
    dtT = dt.T
    ecs = jnp.exp(cs)
    wdec = jnp.exp(cs[L - 1:L, :] - cs) * dt
    eexp = eexp_ref[...]
    ecs_e = _dot3(ecs, eexp)
    wdec_e = _dot3(wdec, eexp)
    lane = lax.broadcasted_iota(jnp.int32, (L, LANES), 1)
    first_half = lane < SSD_HEAD_DIM

    gw = SSD_HPG * SSD_HEAD_DIM
    y_groups = []
    for g in range(SSD_GROUPS):
        b_g = xact[:, SSD_WIDTH + g * SSD_STATE: SSD_WIDTH + (g + 1) * SSD_STATE]
        c_g = xact[:, SSD_WIDTH + (SSD_GROUPS + g) * SSD_STATE: SSD_WIDTH + (SSD_GROUPS + g + 1) * SSD_STATE]
        b_b = b_g.astype(BF16)
        c_b = c_g.astype(BF16)
        cb = lax.dot_general(c_b, b_b, (((1,), (1,)), ((), ())), preferred_element_type=F32)
        xs_g = xact[:, g * gw:(g + 1) * gw]
        h_prev = hT[g]
        y_off = _dot(c_b, h_prev.astype(BF16)) * ecs_e[:, g * gw:(g + 1) * gw]
        xdec = (xs_g * wdec_e[:, g * gw:(g + 1) * gw]).astype(BF16)
        hT[g] = h_prev * ecs_e[L - 1:L, g * gw:(g + 1) * gw] + _dot(b_g.T.astype(BF16), xdec)
        pieces = []
        for j in range(SSD_HPG // 2):
            xs_pair = xs_g[:, j * LANES:(j + 1) * LANES]
            halves = (jnp.where(first_half, xs_pair, 0.0).astype(BF16),
                      jnp.where(first_half, 0.0, xs_pair).astype(BF16))
            yd = None
            for t in range(2):
                h = g * SSD_HPG + 2 * j + t
                seg = cs[:, h:h + 1] - csT[h:h + 1, :]
                lmat = jnp.exp(jnp.where(causal, seg, -jnp.inf))
                m = (cb * lmat * dtT[h:h + 1, :]).astype(BF16)
                part = _dot(m, halves[t])
                yd = part if yd is None else yd + part
            pieces.append(yd)
        y_groups.append(jnp.concatenate(pieces, axis=-1) + y_off + dexp_ref[:, g * gw:(g + 1) * gw] * xs_g)
    y = jnp.concatenate(y_groups, axis=-1)
    z = z_ref[0].astype(F32)
    y_ref[0] = _rms(y * (z * jax.nn.sigmoid(z)), nrm_ref[...]).astype(y_ref.dtype)

    @pl.when(c == pl.num_programs(1) - 1)
    def _emit():
        hto_ref[0] = hT[...]
        for g in range(SSD_GROUPS):
            t = hT[g].T
            for k in range(SSD_HPG):
                st_ref[0, g * SSD_HPG + k] = t[k * SSD_HEAD_DIM:(k + 1) * SSD_HEAD_DIM, :]


def _ssd_chunked(xbc, dt, z, cinit, hinit, cw, cb, dtb, alog, dexp, nrm, eexp, mask_rows):
    bsz, seq, _ = xbc.shape
    nc = seq // SSD_CHUNK
    gw = SSD_HPG * SSD_HEAD_DIM
    blk = lambda w: pl.BlockSpec((1, SSD_CHUNK, w), lambda b, c: (b, c, 0))
    return pl.pallas_call(
        functools.partial(_ssd_body, mask_rows),
        grid=(bsz, nc),
        in_specs=[blk(SSD_CONV_DIM), blk(LANES), blk(SSD_WIDTH),
                  pl.BlockSpec((1, SUBLANES, SSD_CONV_DIM), lambda b, c: (0, 0, 0)),
                  pl.BlockSpec((1, SSD_GROUPS, SSD_STATE, gw), lambda b, c: (0, 0, 0, 0)),
                  _full_spec(cw), _full_spec(cb), _full_spec(dtb), _full_spec(alog),
                  _full_spec(dexp), _full_spec(nrm), _full_spec(eexp)],
        out_specs=[blk(SSD_WIDTH),
                   pl.BlockSpec((1, SUBLANES, SSD_CONV_DIM), lambda b, c: (b, 0, 0)),
                   pl.BlockSpec((1, SSD_HEADS, SSD_HEAD_DIM, SSD_STATE), lambda b, c: (b, 0, 0, 0)),
                   pl.BlockSpec((1, SSD_GROUPS, SSD_STATE, gw), lambda b, c: (b, 0, 0, 0))],
        out_shape=[jax.ShapeDtypeStruct((bsz, seq, SSD_WIDTH), BF16),
                   jax.ShapeDtypeStruct((bsz, SUBLANES, SSD_CONV_DIM), F32),
                   jax.ShapeDtypeStruct((bsz, SSD_HEADS, SSD_HEAD_DIM, SSD_STATE), F32),
                   jax.ShapeDtypeStruct((bsz, SSD_GROUPS, SSD_STATE, gw), F32)],
        scratch_shapes=[pltpu.VMEM((SUBLANES, SSD_CONV_DIM), F32),
                        pltpu.VMEM((SSD_GROUPS, SSD_STATE, gw), F32)],
        compiler_params=pltpu.CompilerParams(dimension_semantics=("parallel", "arbitrary"),
                                             vmem_limit_bytes=VMEM_LIMIT),
        name="ssd_chunked",
    )(xbc, dt, z, cinit, hinit, cw, cb, dtb, alog, dexp, nrm, eexp)


def _ssd_step_prep_body(xbc_ref, c0_ref, c1_ref, c2_ref, dt_ref, cw_ref, cb_ref, dtb_ref, alog_ref,
                        xt_ref, dt_out_ref, dec_ref, bc_ref, xs_ref):
    acc = cb_ref[...]
    for k, r in enumerate((c0_ref, c1_ref, c2_ref, xbc_ref)):
        acc = acc + r[...] * cw_ref[k:k + 1, :]
    xact = acc * jax.nn.sigmoid(acc)
    xs = xact[:, :SSD_WIDTH]
    dt = _softplus(dt_ref[...] + dtb_ref[...])
    dt_out_ref[...] = dt
    dec_ref[...] = jnp.exp(dt * -jnp.exp(alog_ref[...]))
    bc_ref[...] = xact[:, SSD_WIDTH:]
    xs_ref[...] = xs
    xt_ref[...] = xs.T.astype(xt_ref.dtype)


def _ssd_step_prep(xbc, c0, c1, c2, dt, cw, cb, dtb, alog):
    n = xbc.shape[0]
    args = (xbc, c0, c1, c2, dt, cw, cb, dtb, alog)
    spec = lambda r, w: pl.BlockSpec((r, w), lambda: (0, 0))
    return pl.pallas_call(
        _ssd_step_prep_body,
        in_specs=[_full_spec(a) for a in args],
        out_specs=[spec(SSD_WIDTH, n), spec(n, LANES), spec(n, LANES), spec(n, 2 * SSD_GROUPS * SSD_STATE),
                   spec(n, SSD_WIDTH)],
        out_shape=[jax.ShapeDtypeStruct((SSD_WIDTH, n), BF16), jax.ShapeDtypeStruct((n, LANES), F32),
                   jax.ShapeDtypeStruct((n, LANES), F32),
                   jax.ShapeDtypeStruct((n, 2 * SSD_GROUPS * SSD_STATE), F32),
                   jax.ShapeDtypeStruct((n, SSD_WIDTH), F32)],
        compiler_params=pltpu.CompilerParams(vmem_limit_bytes=VMEM_LIMIT),
        name="ssd_step_prep",
    )(*args)


def _ssd_step_body(dt_ref, dec_ref, st_ref, xt_ref, bc_ref, so_ref, y_ref):
    n = xt_ref.shape[1]
    gw = SSD_HPG * SSD_HEAD_DIM
    blk = pl.program_id(0)
    seq_id = lax.broadcasted_iota(jnp.int32, (n, SSD_STATE), 0)
    sub_id = lax.broadcasted_iota(jnp.int32, (SUBLANES, gw), 0)
    base = pl.multiple_of(blk * SUBLANES, SUBLANES)
    y_acc = [jnp.zeros((SUBLANES, gw), F32) for _ in range(SSD_GROUPS)]
    for i in range(SUBLANES):
        s = blk * SUBLANES + i
        for g in range(SSD_GROUPS):
            b_all = bc_ref[:, g * SSD_STATE:(g + 1) * SSD_STATE]
            rhs = jnp.where(seq_id == s, b_all, 0.0).astype(BF16)
            outer = _dot(xt_ref[g * gw:(g + 1) * gw, :], rhs)
            news = []
            for k in range(SSD_HPG):
                h = g * SSD_HPG + k
                new = (dec_ref[s * SSD_HEADS + h] * st_ref[i, h]
                       + dt_ref[s * SSD_HEADS + h] * outer[k * SSD_HEAD_DIM:(k + 1) * SSD_HEAD_DIM, :])
                so_ref[i, h] = new
                news.append(new)
            new_g = jnp.concatenate(news, axis=0).astype(BF16)
            c_lo = (SSD_GROUPS + g) * SSD_STATE
            c_blk = bc_ref[pl.ds(base, SUBLANES), c_lo:c_lo + SSD_STATE].astype(BF16)
            r = lax.dot_general(c_blk, new_g, (((1,), (1,)), ((), ())), preferred_element_type=F32)
            y_acc[g] = y_acc[g] + jnp.where(sub_id == i, r, 0.0)
    y_ref[...] = jnp.concatenate(y_acc, axis=-1)


def _ssd_step(dt_flat, dec_flat, state, xt, bc):
    n = state.shape[0]
    st_spec = pl.BlockSpec((SUBLANES, SSD_HEADS, SSD_HEAD_DIM, SSD_STATE), lambda i, *_: (i, 0, 0, 0))
    return pl.pallas_call(
        _ssd_step_body,
        grid_spec=pltpu.PrefetchScalarGridSpec(
            num_scalar_prefetch=2,
            grid=(n // SUBLANES,),
            in_specs=[st_spec, pl.BlockSpec(xt.shape, lambda i, *_: (0, 0)),
                      pl.BlockSpec(bc.shape, lambda i, *_: (0, 0))],
            out_specs=[st_spec, pl.BlockSpec((SUBLANES, SSD_WIDTH), lambda i, *_: (i, 0))]),
        out_shape=[jax.ShapeDtypeStruct(state.shape, F32), jax.ShapeDtypeStruct((n, SSD_WIDTH), F32)],
        compiler_params=pltpu.CompilerParams(dimension_semantics=("parallel",), vmem_limit_bytes=VMEM_LIMIT),
        name="ssd_step",
    )(dt_flat, dec_flat, state, xt, bc)


def _s5_project_in(u_b16, wb_ref, store):
    kw = 16 * S5_GROUP_CH
    nw = 16 * S5_STATE
    for j in range(S5_WIDTH // kw):
        r = _dot(u_b16[:, j * kw:(j + 1) * kw], wb_ref[j])
        store(j, r[:, :nw], r[:, nw:])


def _s5_tail(hre_of, him_of, u_f32, wcr_ref, wci_ref, d_ref, wglu_ref, bglu_ref, nrm_ref):
    cols = []
    for j in range(wcr_ref.shape[0]):
        cols.append(_dot(hre_of(j).astype(BF16), wcr_ref[j]) + _dot(him_of(j).astype(BF16), wci_ref[j]))
    return _s5_finish(cols, u_f32, d_ref, wglu_ref, bglu_ref, nrm_ref)


def _s5_finish(cols, u_f32, d_ref, wglu_ref, bglu_ref, nrm_ref):
    y = jnp.concatenate(cols, axis=-1) + d_ref[...] * u_f32
    y = jax.nn.gelu(y)
    y = y * jax.nn.sigmoid(_dot(y.astype(BF16), wglu_ref[...]) + bglu_ref[...])
    return _rms(y, nrm_ref[...])


def _s5_seq_body(u_hbm, um_ref, wb_ref, abr_ref, abi_ref, wcr_ref, wci_ref, d_ref, wglu_ref, bglu_ref, nrm_ref,
                 y_hbm, sre_ref, sim_ref, ubuf, ybuf, bu, h, in_sems, out_sems):
    j = pl.program_id(0)
    last = pl.num_programs(0) - 1
    lc, bsz = ubuf.shape[1], ubuf.shape[2]
    rows = lc * bsz
    nw = 16 * S5_STATE

    def in_copy(step, b):
        return pltpu.make_async_copy(u_hbm.at[b, pl.ds(step * lc, lc), :], ubuf.at[step % 2, :, b, :],
                                     in_sems.at[step % 2, b])

    def out_copy(step, b):
        return pltpu.make_async_copy(ybuf.at[step % 2, :, b, :], y_hbm.at[b, pl.ds(step * lc, lc), :],
                                     out_sems.at[step % 2, b])

    def project_in(u_b16, nrows):
        def store(jj, re, im):
            bu[0:nrows, jj * nw:(jj + 1) * nw] = re
            bu[0:nrows, S5_LANES + jj * nw:S5_LANES + (jj + 1) * nw] = im
        _s5_project_in(u_b16, wb_ref, store)

    def scan(nsteps):
        for k in range(S5_LANES // S5_SCAN_LANES):
            sl_r = pl.ds(k * S5_SCAN_LANES, S5_SCAN_LANES)
            sl_i = pl.ds(S5_LANES + k * S5_SCAN_LANES, S5_SCAN_LANES)
            ar = abr_ref[:, sl_r]
            ai = abi_ref[:, sl_r]

            def step(l, carry):
                hr, hi = carry
                slab = pl.ds(pl.multiple_of(l * bsz, bsz), bsz)
                nr = ar * hr - ai * hi + bu[slab, sl_r]
                ni = ar * hi + ai * hr + bu[slab, sl_i]
                bu[slab, sl_r] = nr
                bu[slab, sl_i] = ni
                return nr, ni

            hr, hi = lax.fori_loop(0, nsteps, step, (h[:, sl_r], h[:, sl_i]))
            h[:, sl_r] = hr
            h[:, sl_i] = hi

    @pl.when(j == 0)
    def _first():
        for b in range(bsz):
            in_copy(0, b).start()
        h[...] = jnp.zeros_like(h)
        project_in(um_ref[...], N_META * bsz)
        scan(N_META)

    @pl.when(j < last)
    def _prefetch():
        for b in range(bsz):
            in_copy(j + 1, b).start()

    for b in range(bsz):
        in_copy(j, b).wait()
    u2 = ubuf[j % 2].reshape(rows, S5_WIDTH)
    u_b16 = u2.astype(BF16)
    kw = 16 * S5_GROUP_CH

    def project_block(jj):
        r = _dot(u_b16[:, jj * kw:(jj + 1) * kw], wb_ref[jj])
        bu[0:rows, jj * nw:(jj + 1) * nw] = r[:, :nw]
        bu[0:rows, S5_LANES + jj * nw:S5_LANES + (jj + 1) * nw] = r[:, nw:]

    def scan_block(jj):
        for k in range(nw // S5_SCAN_LANES):
            lo = jj * nw + k * S5_SCAN_LANES
            sl_r = slice(lo, lo + S5_SCAN_LANES)
            sl_i = slice(S5_LANES + lo, S5_LANES + lo + S5_SCAN_LANES)
            ar, ai = abr_ref[:, sl_r], abi_ref[:, sl_r]
            hr, hi = h[:, sl_r], h[:, sl_i]
            for l in range(lc):
                slab = slice(l * bsz, (l + 1) * bsz)
                hr, hi = (ar * hr - ai * hi + bu[slab, sl_r], ar * hi + ai * hr + bu[slab, sl_i])
                bu[slab, sl_r] = hr
                bu[slab, sl_i] = hi
            h[:, sl_r] = hr
            h[:, sl_i] = hi

    def readout_block(jj):
        return (_dot(bu[:, jj * nw:(jj + 1) * nw].astype(BF16), wcr_ref[jj])
                + _dot(bu[:, S5_LANES + jj * nw:S5_LANES + (jj + 1) * nw].astype(BF16), wci_ref[jj]))

    n_blocks = S5_WIDTH // kw
    project_block(0)
    cols = []
    for jj in range(n_blocks):
        if jj + 1 < n_blocks:
            project_block(jj + 1)
        scan_block(jj)
        cols.append(readout_block(jj))
    y = _s5_finish(cols, u2, d_ref, wglu_ref, bglu_ref, nrm_ref)
    ybuf[j % 2] = y.reshape(lc, bsz, S5_WIDTH)
    for b in range(bsz):
        out_copy(j, b).start()

    @pl.when(j > 0)
    def _wait_previous_out():
        for b in range(bsz):
            out_copy(j - 1, b).wait()

    @pl.when(j == last)
    def _emit():
        for b in range(bsz):
            out_copy(j, b).wait()
        sre_ref[...] = h[:, 0:S5_LANES]
        sim_ref[...] = h[:, S5_LANES:]


def _s5_seq(u, um, wb, abr, abi, wcr, wci, d, wglu, bglu, nrm):
    bsz, seq, _ = u.shape
    lc = S5_TIME_TILE
    consts = (um, wb, abr, abi, wcr, wci, d, wglu, bglu, nrm)
    st = pl.BlockSpec((bsz, S5_LANES), lambda j: (0, 0))
    return pl.pallas_call(
        _s5_seq_body,
        grid=(seq // lc,),
        in_specs=[pl.BlockSpec(memory_space=pl.ANY)] + [_resident_spec(a) for a in consts],
        out_specs=[pl.BlockSpec(memory_space=pl.ANY), st, st],
        out_shape=[jax.ShapeDtypeStruct((bsz, seq, S5_WIDTH), F32),
                   jax.ShapeDtypeStruct((bsz, S5_LANES), F32), jax.ShapeDtypeStruct((bsz, S5_LANES), F32)],
        scratch_shapes=[pltpu.VMEM((2, lc, bsz, S5_WIDTH), F32), pltpu.VMEM((2, lc, bsz, S5_WIDTH), F32),
                        pltpu.VMEM((lc * bsz, 2 * S5_LANES), F32), pltpu.VMEM((bsz, 2 * S5_LANES), F32),
                        pltpu.SemaphoreType.DMA((2, bsz)), pltpu.SemaphoreType.DMA((2, bsz))],
        compiler_params=pltpu.CompilerParams(dimension_semantics=("arbitrary",), vmem_limit_bytes=VMEM_LIMIT),
        name="s5_seq",
    )(u, *consts)


def _sample_post_body(yc_ref, xs_ref, z_ref, dexp_ref, snrm_ref, u_ref, hr_ref, hi_ref, wb_ref, abr_ref, abi_ref,
                      wcr_ref, wci_ref, d_ref, wglu_ref, bglu_ref, nrm_ref,
                      yssd_ref, ys5_ref, nre_ref, nim_ref):
    z = z_ref[...]
    y = yc_ref[...] + dexp_ref[...] * xs_ref[...]
    yssd_ref[...] = _rms(y * (z * jax.nn.sigmoid(z)), snrm_ref[...]).astype(yssd_ref.dtype)

    u = u_ref[...]
    nw = 16 * S5_STATE
    ar, ai = abr_ref[...], abi_ref[...]

    def store(jj, re, im):
        sl = slice(jj * nw, (jj + 1) * nw)
        h0r, h0i = hr_ref[:, sl], hi_ref[:, sl]
        nre_ref[:, sl] = ar[:, sl] * h0r - ai[:, sl] * h0i + re
        nim_ref[:, sl] = ar[:, sl] * h0i + ai[:, sl] * h0r + im

    _s5_project_in(u.astype(BF16), wb_ref, store)
    slab = lambda ref: (lambda jj: ref[:, jj * nw:(jj + 1) * nw])
    y5 = _s5_tail(slab(nre_ref), slab(nim_ref), u, wcr_ref, wci_ref, d_ref, wglu_ref, bglu_ref, nrm_ref)
    ys5_ref[...] = y5.astype(ys5_ref.dtype)


def _sample_post(yc, xs, z, dexp, snrm, u, h0r, h0i, wb, abr1, abi1, wcr, wci, d, wglu, bglu, nrm):
    n = yc.shape[0]
    args = (yc, xs, z, dexp, snrm, u, h0r, h0i, wb, abr1, abi1, wcr, wci, d, wglu, bglu, nrm)
    spec = lambda w: pl.BlockSpec((n, w), lambda: (0, 0))
    return pl.pallas_call(
        _sample_post_body,
        in_specs=[_full_spec(a) for a in args],
        out_specs=[spec(SSD_WIDTH), spec(S5_WIDTH), spec(S5_LANES), spec(S5_LANES)],
        out_shape=[jax.ShapeDtypeStruct((n, SSD_WIDTH), BF16), jax.ShapeDtypeStruct((n, S5_WIDTH), BF16),
                   jax.ShapeDtypeStruct((n, S5_LANES), F32), jax.ShapeDtypeStruct((n, S5_LANES), F32)],
        compiler_params=pltpu.CompilerParams(vmem_limit_bytes=VMEM_LIMIT),
        name="sample_post",
    )(*args)


def _mix_route_body(n_blocks, n_sorted, xp_ref, ysp_ref, y5p_ref, xs_ref, yss_ref, y5s_ref, *refs):
    consts = refs[:6]
    x1_ref, xn_hbm, rt_ref, pos_ref, meta_ref, carry, fields, xbuf, sems = refs[6:]
    i = pl.program_id(0)
    tm, n_sample = xp_ref.shape[0], xs_ref.shape[0]
    col0 = pl.multiple_of(i * tm, LANES)

    def xn_copy(step, rows, j):
        return pltpu.make_async_copy(xbuf.at[step % 2, pl.ds(0, rows), pl.ds(j * LANES, LANES)],
                                     xn_hbm.at[pl.ds(step * tm, rows), j, :], sems.at[step % 2, j])

    @pl.when(i == 0)
    def _init():
        carry[...] = jnp.zeros_like(carry)

    @pl.when(i < n_blocks)
    def _prompt_rows():
        _mix_route_compute(xp_ref, ysp_ref, y5p_ref, *consts, x1_ref, rt_ref, carry, xbuf.at[i % 2], fields, col0)
        for j in range(PACK_ROWS):
            xn_copy(i, tm, j).start()

    @pl.when(i == n_blocks)
    def _sample_rows():
        _mix_route_compute(xs_ref, yss_ref, y5s_ref, *consts, x1_ref, rt_ref, carry, xbuf.at[i % 2], fields, col0)
        for j in range(PACK_ROWS):
            xn_copy(i, n_sample, j).start()
        _route_layout(carry, fields, pos_ref, meta_ref, n_sorted)
        for j in range(PACK_ROWS):
            xn_copy(i, n_sample, j).wait()

    @pl.when(i > 0)
    def _wait_previous_rows():
        for j in range(PACK_ROWS):
            xn_copy(i - 1, tm, j).wait()


def _route_layout(carry, fields, pos_ref, meta_ref, n_sorted):
    counts = carry[...]
    tiles_per = jnp.floor((counts + (MOE_TILE - 1)) * (1.0 / MOE_TILE))
    upto = lax.broadcasted_iota(jnp.int32, (LANES, LANES), 0) <= lax.broadcasted_iota(jnp.int32, (LANES, LANES), 1)
    tile_end = _dot(tiles_per.astype(BF16), upto.astype(BF16))
    pstart = (tile_end - tiles_per) * MOE_TILE
    n_used = tile_end[:, MOE_EXPERTS - 1:MOE_EXPERTS]

    f = fields[...]
    first_row = jnp.zeros_like(f)
    tile_id = jnp.minimum(lax.broadcasted_iota(jnp.int32, meta_ref.shape, 1).astype(F32), n_used - 1.0)
    tile_expert = jnp.zeros(meta_ref.shape, F32)
    for e in range(MOE_EXPERTS):
        first_row = first_row + jnp.where(f == float(e), pstart[:, e:e + 1], 0.0)
        tile_expert = tile_expert + jnp.where(tile_end[:, e:e + 1] <= tile_id, 1.0, 0.0)
    pos = first_row + pltpu.roll(f, shift=4, axis=0)
    pos_ref[...] = jnp.clip(pos, 0.0, n_sorted - 1.0).astype(jnp.int32)
    is_row0 = lax.broadcasted_iota(jnp.int32, meta_ref.shape, 0) == 0
    meta_ref[...] = jnp.where(is_row0, tile_expert, n_used).astype(jnp.int32)


def _mix_route_compute(x_ref, ys_ref, y5_ref, wa_ref, wb_ref, nf_ref, wrh_ref, wrl_ref, br_ref,
                       x1_ref, rt_ref, carry, xn_buf, fields, col0):
    rows = x_ref.shape[0]
    x1 = x_ref[...] + _dot(ys_ref[...], wa_ref[...]) + _dot(y5_ref[...].astype(BF16), wb_ref[...])
    x1_ref[0:rows, :] = x1
    xn = _rms(x1, nf_ref[...])
    xn_buf[0:rows, :] = _pack_bf16_pairs(xn)

    xh = xn.astype(BF16)
    xl = (xn - xh.astype(F32)).astype(BF16)
    logits = _dot(xh, wrh_ref[...]) + _dot(xl, wrh_ref[...]) + _dot(xh, wrl_ref[...]) + br_ref[...]
    tm = logits.shape[0]
    lane = lax.broadcasted_iota(jnp.int32, logits.shape, 1).astype(F32)
    neg = -jnp.inf
    big = float(LANES)

    def first_max(v):
        m = jnp.max(v, axis=-1, keepdims=True)
        return m, jnp.min(jnp.where(v == m, lane, big), axis=-1, keepdims=True)

    coarse = lane < MOE_GROUPS
    mc, gsel = first_max(jnp.where(coarse, logits, neg))
    psel = 1.0 / jnp.sum(jnp.where(coarse, jnp.exp(logits - mc), 0.0), axis=-1, keepdims=True)
    lo = MOE_GROUPS + MOE_EPG * gsel
    lf = jnp.where((lane >= lo) & (lane < lo + MOE_EPG), logits, neg)
    m1, i1 = first_max(lf)
    m2, i2 = first_max(jnp.where(lane == i1, neg, lf))
    e2 = jnp.exp(m2 - m1)
    g1 = psel / (1.0 + e2)
    g2 = psel * e2 / (1.0 + e2)
    e_a, e_b = i1 - MOE_GROUPS, i2 - MOE_GROUPS

    pick_a, pick_b = lane == e_a, lane == e_b
    picks = jnp.where(pick_a | pick_b, 1.0, 0.0)
    earlier = lax.broadcasted_iota(jnp.int32, (tm, tm), 0) > lax.broadcasted_iota(jnp.int32, (tm, tm), 1)
    prior = _dot(earlier.astype(BF16), picks.astype(BF16)) + carry[...]
    rank_a = jnp.sum(jnp.where(pick_a, prior, 0.0), axis=-1, keepdims=True)
    rank_b = jnp.sum(jnp.where(pick_b, prior, 0.0), axis=-1, keepdims=True)
    carry[...] = prior[tm - 1:tm, :] + picks[tm - 1:tm, :]

    out = jnp.zeros_like(logits)
    for k, v in enumerate((e_a, e_b, g1, g2, rank_a, rank_b)):
        out = jnp.where(lane == float(k), v, out)
    rt_ref[0:rows, :] = out
    fields[:, pl.ds(col0, rows)] = out.T[0:SUBLANES, :]


def _mix_route(prompt, sample, consts, tm, n_tiles):
    n_prompt, n_sample = prompt[0].shape[0], sample[0].shape[0]
    assert n_prompt % tm == 0 and n_sample <= tm
    n_blocks = n_prompt // tm
    total_rows = n_prompt + n_sample
    row = lambda w: pl.BlockSpec((tm, w), lambda i: (jnp.minimum(i, n_blocks - 1), 0))
    out_row = lambda w: pl.BlockSpec((tm, w), lambda i: (i, 0))
    assert total_rows % LANES == 0 and n_tiles <= 2 * LANES
    whole = lambda shape: pl.BlockSpec(shape, lambda i: (0, 0))
    return pl.pallas_call(
        functools.partial(_mix_route_body, n_blocks, n_tiles * MOE_TILE),
        grid=(n_blocks + 1,),
        in_specs=([row(D_MODEL), row(SSD_WIDTH), row(S5_WIDTH)] + [_full_spec(a) for a in sample]
                  + [_full_spec(a) for a in consts]),
        out_specs=[out_row(D_MODEL), pl.BlockSpec(memory_space=pl.ANY), out_row(LANES),
                   whole((SUBLANES, total_rows)), whole((SUBLANES, 2 * LANES))],
        out_shape=[jax.ShapeDtypeStruct((total_rows, D_MODEL), F32),
                   jax.ShapeDtypeStruct((total_rows, PACK_ROWS, LANES), jnp.uint32),
                   jax.ShapeDtypeStruct((total_rows, LANES), F32),
                   jax.ShapeDtypeStruct((SUBLANES, total_rows), jnp.int32),
                   jax.ShapeDtypeStruct((SUBLANES, 2 * LANES), jnp.int32)],
        scratch_shapes=[pltpu.VMEM((1, LANES), F32), pltpu.VMEM((SUBLANES, total_rows), F32),
                        pltpu.VMEM((2, tm, D_MODEL // 2), jnp.uint32), pltpu.SemaphoreType.DMA((2, PACK_ROWS))],
        compiler_params=pltpu.CompilerParams(dimension_semantics=("arbitrary",), vmem_limit_bytes=VMEM_LIMIT),
        name="mix_route",
    )(*prompt, *sample, *consts)


def _sc_mesh():
    return plsc.VectorSubcoreMesh(core_axis_name="c", subcore_axis_name="s")


def _sc_worker():
    return lax.axis_index("s") * SC_CORES + lax.axis_index("c")


def _sc_dispatch(xn, pos_a, pos_b, n_rows):
    n_tok = xn.shape[0]
    ch = SC_DISPATCH_ROWS
    n_chunks = n_tok // ch
    assert n_tok % ch == 0 and n_chunks >= SC_WORKERS
    max_mine = -(-n_chunks // SC_WORKERS)
    row_shape, dtype = xn.shape[1:], xn.dtype
    stage = [pltpu.VMEM((ch,), jnp.int32), pltpu.VMEM((ch,), jnp.int32), pltpu.VMEM((ch,) + row_shape, dtype),
             pltpu.SemaphoreType.DMA]

    @functools.partial(
        pl.kernel, mesh=_sc_mesh(),
        out_type=jax.ShapeDtypeStruct((n_rows,) + row_shape, dtype),
        scratch_types=stage + stage + [pltpu.SemaphoreType.DMA])
    def push(xn_hbm, pa_hbm, pb_hbm, xs_hbm, ia0, ib0, rows0, lsem0, ia1, ib1, rows1, lsem1, ssem):
        wid = _sc_worker()
        mine = (n_chunks - wid + SC_WORKERS - 1) // SC_WORKERS
        bufs = ((ia0, ib0, rows0, lsem0), (ia1, ib1, rows1, lsem1))

        def loads(t, b):
            ia, ib, rows, sem = bufs[b]
            off = pl.multiple_of((wid + t * SC_WORKERS) * ch, ch)
            return (pltpu.make_async_copy(pa_hbm.at[pl.ds(off, ch)], ia, sem),
                    pltpu.make_async_copy(pb_hbm.at[pl.ds(off, ch)], ib, sem),
                    pltpu.make_async_copy(xn_hbm.at[pl.ds(off, ch)], rows, sem))

        def stage_in(t, b):
            for c in loads(t, b):
                c.start()

        def scatter(t, b):
            ia, ib, rows, _ = bufs[b]
            for c in loads(t, b):
                c.wait()
            first = pltpu.async_copy(rows, xs_hbm.at[ia], ssem)
            second = pltpu.async_copy(rows, xs_hbm.at[ib], ssem)
            first.wait()
            second.wait()

        stage_in(0, 0)

        @pl.loop(0, (max_mine + 1) // 2)
        def _(p):
            t = 2 * p

            @pl.when(t + 1 < mine)
            def _():
                stage_in(t + 1, 1)

            @pl.when(t < mine)
            def _():
                scatter(t, 0)

            @pl.when(t + 2 < mine)
            def _():
                stage_in(t + 2, 0)

            @pl.when(t + 1 < mine)
            def _():
                scatter(t + 1, 1)

    return push(xn, pos_a, pos_b)


def _sc_collect(ysorted, pos_flat, ch):
    n_pick = pos_flat.shape[0]
    per_worker = n_pick // SC_WORKERS
    n_chunks = per_worker // ch
    assert n_pick % SC_WORKERS == 0 and per_worker % ch == 0
    row_shape, dtype = ysorted.shape[1:], ysorted.dtype

    @functools.partial(
        pl.kernel, mesh=_sc_mesh(),
        out_type=jax.ShapeDtypeStruct((n_pick,) + row_shape, dtype),
        scratch_types=[pltpu.VMEM((ch,), jnp.int32), pltpu.VMEM((ch,), jnp.int32),
                       pltpu.VMEM((ch,) + row_shape, dtype), pltpu.VMEM((ch,) + row_shape, dtype),
                       pltpu.SemaphoreType.DMA, pltpu.SemaphoreType.DMA])
    def pull(ys_hbm, pos_hbm, out_hbm, idx0, idx1, rows0, rows1, sem0, sem1):
        base = _sc_worker() * per_worker
        bufs = ((idx0, rows0, sem0), (idx1, rows1, sem1))

        def offset(j):
            return pl.multiple_of(base + j * ch, SUBLANES)

        def fetch(j, b):
            idx, rows, sem = bufs[b]
            pltpu.sync_copy(pos_hbm.at[pl.ds(offset(j), ch)], idx)
            pltpu.async_copy(ys_hbm.at[idx], rows, sem)

        def flush(j, b):
            idx, rows, sem = bufs[b]
            pltpu.make_async_copy(ys_hbm.at[idx], rows, sem).wait()
            pltpu.sync_copy(rows, out_hbm.at[pl.ds(offset(j), ch)])

        fetch(0, 0)

        @pl.loop(0, n_chunks // 2)
        def _(p):
            j = 2 * p
            fetch(j + 1, 1)
            flush(j, 0)

            @pl.when(j + 2 < n_chunks)
            def _():
                fetch(j + 2, 0)

            flush(j + 1, 1)

        if n_chunks % 2:
            flush(n_chunks - 1, 0)

    return pull(ysorted, pos_flat)


def _moe_ffn_body(te_ref, nused_ref, x_ref, wg_hbm, wu_hbm, wd_hbm, y_ref,
                  wg_f32, wu_f32, wd_f32, wgb, wub, wdb, slot_ref, sems):
    i = pl.program_id(0)
    n_used = nused_ref[0]

    def fetch(expert, slot):
        return (pltpu.make_async_copy(wg_hbm.at[expert], wg_f32.at[slot], sems.at[slot, 0]),
                pltpu.make_async_copy(wu_hbm.at[expert], wu_f32.at[slot], sems.at[slot, 1]),
                pltpu.make_async_copy(wd_hbm.at[expert], wd_f32.at[slot], sems.at[slot, 2]))

    @pl.when(i >= n_used)
    def _unused_tile():
        y_ref[...] = jnp.zeros_like(y_ref)

    @pl.when(i < n_used)
    def _tile():
        expert = te_ref[i]

        @pl.when(i == 0)
        def _first_fetch():
            slot_ref[0] = 0
            for c in fetch(expert, 0):
                c.start()

        @pl.when((i == 0) | (expert != te_ref[jnp.maximum(i - 1, 0)]))
        def _new_expert():
            slot = slot_ref[0]
            nxt = lax.while_loop(lambda k: (k < n_used) & (te_ref[jnp.minimum(k, n_used - 1)] == expert),
                                 lambda k: k + 1, i + 1)

            @pl.when(nxt < n_used)
            def _prefetch():
                for c in fetch(te_ref[jnp.minimum(nxt, n_used - 1)], 1 - slot):
                    c.start()

            for c in fetch(expert, slot):
                c.wait()
            wgb[...] = wg_f32[slot].astype(BF16)
            wub[...] = wu_f32[slot].astype(BF16)
            wdb[...] = wd_f32[slot].astype(BF16)
            slot_ref[0] = 1 - slot

        x = _unpack_bf16_pairs(x_ref, MOE_TILE).astype(BF16)
        gate = _dot(x, wgb[...])
        hmid = (gate * jax.nn.sigmoid(gate)) * _dot(x, wub[...])
        y = _dot(hmid.astype(BF16), wdb[...])
        packed = _pack_bf16_pairs(y)
        for j in range(PACK_ROWS):
            y_ref[pl.ds(j, MOE_TILE, stride=PACK_ROWS), :] = packed[:, j * LANES:(j + 1) * LANES]


def _moe_ffn(tile_expert, n_used, xsorted, w_gate, w_up, w_down):
    n_tiles = tile_expert.shape[0]
    hbm = pl.BlockSpec(memory_space=pl.ANY)
    tile = lambda rows, imap: pl.BlockSpec((MOE_TILE * rows, LANES), imap)
    up_shape, down_shape = (D_MODEL, MOE_D_FF), (MOE_D_FF, D_MODEL)
    return pl.pallas_call(
        _moe_ffn_body,
        grid_spec=pltpu.PrefetchScalarGridSpec(
            num_scalar_prefetch=2,
            grid=(n_tiles,),
            in_specs=[tile(PACK_ROWS, lambda i, te, nu: (jnp.clip(i, 0, jnp.maximum(nu[0] - 1, 0)), 0)),
                      hbm, hbm, hbm],
            out_specs=tile(PACK_ROWS, lambda i, te, nu: (i, 0)),
            scratch_shapes=[pltpu.VMEM((2,) + up_shape, F32), pltpu.VMEM((2,) + up_shape, F32),
                            pltpu.VMEM((2,) + down_shape, F32),
                            pltpu.VMEM(up_shape, BF16), pltpu.VMEM(up_shape, BF16), pltpu.VMEM(down_shape, BF16),
                            pltpu.SMEM((1,), jnp.int32), pltpu.SemaphoreType.DMA((2, 3))]),
        out_shape=jax.ShapeDtypeStruct((n_tiles * MOE_TILE * PACK_ROWS, LANES), jnp.uint32),
        compiler_params=pltpu.CompilerParams(dimension_semantics=("arbitrary",), vmem_limit_bytes=VMEM_LIMIT),
        name="moe_ffn",
    )(tile_expert, n_used, xsorted, w_gate, w_up, w_down)


def _combine_body(x1_ref, rt_ref, ya_ref, yb_ref, nf_ref, *rest):
    out_ref = rest[-1]
    rt = rt_ref[...]
    x1 = x1_ref[...]
    tm = x1.shape[0]

    x2 = (x1 + rt[:, 2:3] * _unpack_bf16_pairs(ya_ref.at[0], tm)
          + rt[:, 3:4] * _unpack_bf16_pairs(yb_ref.at[0], tm))
    out_ref[...] = _rms(x2, nf_ref[...])


def _combine(x1, rt, y_picks, nf, tm, rows, x_block, y_block, out_rows, out_block, out_buf=None):
    row = lambda w: pl.BlockSpec((tm, w), lambda i: (i + x_block, 0))
    pick = lambda k: pl.BlockSpec((1, tm * PACK_ROWS, LANES), lambda i: (k, i + y_block, 0))
    in_specs = [row(D_MODEL), row(LANES), pick(0), pick(1), pl.BlockSpec((1, D_MODEL), lambda i: (0, 0))]
    args = [x1, rt, y_picks, y_picks, nf]
    aliases = {}
    if out_buf is not None:
        in_specs.append(pl.BlockSpec(memory_space=pl.ANY))
        aliases[len(args)] = 0
        args.append(out_buf)
    return pl.pallas_call(
        _combine_body,
        grid=(rows // tm,),
        in_specs=in_specs,
        out_specs=pl.BlockSpec((tm, D_MODEL), lambda i: (i + out_block, 0)),
        out_shape=jax.ShapeDtypeStruct((out_rows, D_MODEL), F32),
        input_output_aliases=aliases,
        compiler_params=pltpu.CompilerParams(dimension_semantics=("parallel",), vmem_limit_bytes=VMEM_LIMIT),
        name="moe_combine",
    )(*args)


def _s5_tables(a_re, a_im, log_dt, b_re, b_im, c_re, c_im):
    dt = jnp.exp(log_dt)[:, None]
    mag = jnp.exp(a_re * dt)
    ab_re = mag * jnp.cos(a_im * dt)
    ab_im = mag * jnp.sin(a_im * dt)
    den = a_re * a_re + a_im * a_im
    nr = ab_re - 1.0
    q_re = (nr * a_re + ab_im * a_im) / den
    q_im = (ab_im * a_re - nr * a_im) / den
    bb_re = q_re[..., None] * b_re - q_im[..., None] * b_im
    bb_im = q_re[..., None] * b_im + q_im[..., None] * b_re
    nblk = S5_GROUPS // 16
    kw, nw = 16 * S5_GROUP_CH, 16 * S5_STATE
    same_group = (jnp.arange(kw)[:, None] // S5_GROUP_CH) == (jnp.arange(nw)[None, :] // S5_STATE)

    def in_map(bb):
        rows = bb.reshape(nblk, 16, S5_STATE, S5_GROUP_CH).transpose(0, 1, 3, 2).reshape(nblk, kw, S5_STATE)
        return jnp.where(same_group, jnp.tile(rows, (1, 1, 16)), 0.0)

    def out_map(cc):
        cols = cc.reshape(nblk, 16, S5_GROUP_CH, S5_STATE).transpose(0, 3, 1, 2).reshape(nblk, S5_STATE, kw)
        return jnp.where(same_group.T, jnp.tile(cols, (1, 16, 1)), 0.0)

    wb = jnp.concatenate([in_map(bb_re), in_map(bb_im)], axis=-1).astype(BF16)
    return (wb, ab_re.reshape(1, S5_LANES), ab_im.reshape(1, S5_LANES),
            out_map(c_re).astype(BF16), out_map(-c_im).astype(BF16))


def kernel(x_prompt, x_sample, state_ssd_conv, state_ssd_ssm, state_s5_re, state_s5_im, meta_tokens, norm_mix, w_in, conv_w, conv_b, dt_bias, a_log, d_ssd, ssd_norm, s5_a_re, s5_a_im, s5_log_dt, s5_b_re, s5_b_im, s5_c_re, s5_c_im, s5_d, w_glu, b_glu, s5_norm, w_out, norm_ffn, router_coarse_w, router_coarse_b, router_fine_w, router_fine_b, w_gate, w_up, w_down, norm_final):
    bp, seq, _ = x_prompt.shape
    bs = x_sample.shape[0]
    n_prompt = bp * seq
    n_tok = n_prompt + bs
    row2 = lambda v: v.reshape(1, -1)
    pad_heads = lambda v: jnp.pad(v, (0, LANES - SSD_HEADS)).reshape(1, LANES)

    w = w_in[0]
    o1, o2, o3 = SSD_WIDTH, SSD_WIDTH + SSD_CONV_DIM, SSD_WIDTH + SSD_CONV_DIM + SSD_HEADS
    wz, wx, wu = w[:, :o1].astype(BF16), w[:, o1:o2].astype(BF16), w[:, o3:].astype(BF16)
    wdt = jnp.pad(w[:, o2:o3], ((0, 0), (0, LANES - SSD_HEADS))).astype(BF16)
    g_mix = row2(norm_mix[0])
    cw, cb = conv_w[0], row2(conv_b[0])
    dtb, alog = pad_heads(dt_bias[0]), pad_heads(a_log[0])
    dexp = row2(jnp.repeat(d_ssd[0], SSD_HEAD_DIM))
    snrm = row2(ssd_norm[0])
    eexp = (jnp.arange(LANES)[:, None] == (jnp.arange(SSD_WIDTH) // SSD_HEAD_DIM)[None, :]).astype(BF16)
    wb5, ab_re, ab_im, wcr, wci = _s5_tables(s5_a_re[0], s5_a_im[0], s5_log_dt[0], s5_b_re[0], s5_b_im[0],
                                             s5_c_re[0], s5_c_im[0])
    d5, wglu, bglu, nrm5 = row2(s5_d[0]), w_glu[0].astype(BF16), row2(b_glu[0]), row2(s5_norm[0])
    wo_a, wo_b = w_out[0][:SSD_WIDTH].astype(BF16), w_out[0][SSD_WIDTH:].astype(BF16)
    w_r = jnp.concatenate([router_coarse_w[0], router_fine_w[0].transpose(1, 0, 2).reshape(D_MODEL, MOE_EXPERTS)], axis=1)
    w_r = jnp.pad(w_r, ((0, 0), (0, LANES - w_r.shape[1])))
    wrh = w_r.astype(BF16)
    wrl = (w_r - wrh.astype(F32)).astype(BF16)
    b_r = jnp.concatenate([router_coarse_b[0], router_fine_b[0].reshape(-1)])
    b_r = jnp.pad(b_r, (0, LANES - b_r.shape[0])).reshape(1, LANES)

    zp, xbcp, dtp, up = _in_proj(x_prompt.reshape(n_prompt, D_MODEL), g_mix, wz, wx, wdt, wu, TOK_TILE, BF16, F32)
    xsm = jnp.concatenate([x_sample.reshape(bs, D_MODEL), meta_tokens], axis=0)
    zs, xbcs, dts, us = _in_proj(xsm, g_mix, wz, wx, wdt, wu, xsm.shape[0], F32, F32)

    front = SSD_CHUNK - N_META
    padf = lambda a: jnp.pad(a[bs:], ((front, 0), (0, 0)))[None]
    gw = SSD_HPG * SSD_HEAD_DIM
    ssd_consts = (cw, cb, dtb, alog, dexp, snrm, eexp)
    _, ctail_m, _, ht_m = _ssd_chunked(
        padf(xbcs).astype(BF16), padf(dts), jnp.zeros((1, SSD_CHUNK, SSD_WIDTH), F32),
        jnp.zeros((1, SUBLANES, SSD_CONV_DIM), F32), jnp.zeros((1, SSD_GROUPS, SSD_STATE, gw), F32),
        *ssd_consts, mask_rows=front)
    y_ssd_p, ctail_p, ssm_p, _ = _ssd_chunked(
        xbcp.reshape(bp, seq, SSD_CONV_DIM), dtp.reshape(bp, seq, LANES), zp.reshape(bp, seq, SSD_WIDTH),
        ctail_m, ht_m, *ssd_consts, mask_rows=0)

    abr8, abi8 = jnp.broadcast_to(ab_re, (bp, S5_LANES)), jnp.broadcast_to(ab_im, (bp, S5_LANES))
    um8 = jnp.repeat(us[bs:], bp, axis=0).astype(BF16)
    y_s5_p, s5re_p, s5im_p = _s5_seq(up.reshape(bp, seq, S5_WIDTH), um8, wb5, abr8, abi8,
                                     wcr, wci, d5, wglu, bglu, nrm5)

    cst = state_ssd_conv[0]
    xt_s, dt_s, dec_s, bc, xs_s = _ssd_step_prep(xbcs[:bs], cst[:, 0], cst[:, 1], cst[:, 2], dts[:bs],
                                                 cw, cb, dtb, alog)
    ssm_s, y_core = _ssd_step(dt_s[:, :SSD_HEADS].reshape(-1), dec_s[:, :SSD_HEADS].reshape(-1),
                              state_ssd_ssm[0], xt_s, bc)
    y_ssd_s, y_s5_s, s5re_s, s5im_s = _sample_post(
        y_core, xs_s, zs[:bs], dexp, snrm, us[:bs], state_s5_re[0].reshape(bs, S5_LANES),
        state_s5_im[0].reshape(bs, S5_LANES), wb5, ab_re, ab_im, wcr, wci, d5, wglu, bglu, nrm5)

    route_consts = (wo_a, wo_b, row2(norm_ffn[0]), wrh, wrl, b_r)
    n_tiles = -(-2 * n_tok // MOE_TILE) + MOE_EXPERTS
    x1, xn, rt, pos, meta = _mix_route(
        (x_prompt.reshape(n_prompt, D_MODEL), y_ssd_p.reshape(n_prompt, SSD_WIDTH), y_s5_p.reshape(n_prompt, S5_WIDTH)),
        (x_sample.reshape(bs, D_MODEL), y_ssd_s, y_s5_s), route_consts, TOK_TILE, n_tiles)

    pos_a, pos_b = pos[0], pos[1]
    tile_expert, n_used = meta[0, :n_tiles], meta[1, :1]
    xsorted = _sc_dispatch(xn, pos_a, pos_b, n_tiles * MOE_TILE)
    ysorted = _moe_ffn(tile_expert, n_used, xsorted.reshape(-1, LANES), w_gate[0], w_up[0], w_down[0])
    nfin = row2(norm_final)

    half = n_prompt // 2

    def collect(lo, hi, ch):
        picks = jnp.concatenate([pos_a[lo:hi], pos_b[lo:hi]])
        packed_rows = ysorted.reshape(-1, PACK_ROWS, LANES)
        return _sc_collect(packed_rows, picks, ch).reshape(2, (hi - lo) * PACK_ROWS, LANES)

    picks_1 = collect(0, half, SC_COLLECT_ROWS[0])
    picks_2 = collect(half, n_tok, SC_COLLECT_ROWS[1])
    blocks = half // MOE_TILE
    y_p = _combine(x1, rt, picks_1, nfin, MOE_TILE, half, 0, 0, n_prompt, 0)
    y_p = _combine(x1, rt, picks_2, nfin, MOE_TILE, half, blocks, 0, n_prompt, blocks, out_buf=y_p)
    y_s = _combine(x1, rt, picks_2, nfin, bs, bs, n_prompt // bs, half // bs, bs, 0)

    s5_state = lambda a, b: a.reshape(1, b, S5_GROUPS, S5_STATE)
    new_conv_s = jnp.stack([cst[:, 1], cst[:, 2], xbcs[:bs]], axis=1)[None]
    return (y_p.reshape(bp, seq, D_MODEL), y_s.reshape(bs, 1, D_MODEL),
            ctail_p[:, SUBLANES - (SSD_CONV - 1):][None], ssm_p[None], s5_state(s5re_p, bp), s5_state(s5im_p, bp),
            new_conv_s, ssm_s[None], s5_state(s5re_s, bs), s5_state(s5im_s, bs))
```

```python
import functools

import jax
import jax.numpy as jnp
from jax import lax
from jax.experimental import pallas as pl
from jax.experimental.pallas import tpu as pltpu
from jax.experimental.pallas import tpu_sc as plsc

F32, BF16 = jnp.float32, jnp.bfloat16

D_MODEL = 1024
N_META = 16
SSD_WIDTH = 1024
SSD_HEAD_DIM = 64
SSD_HEADS = 16
SSD_GROUPS = 2
SSD_HPG = SSD_HEADS // SSD_GROUPS
SSD_STATE = 128
SSD_CONV = 4
SSD_CHUNK = 128
SSD_CONV_DIM = SSD_WIDTH + 2 * SSD_GROUPS * SSD_STATE
S5_WIDTH = 1024
S5_GROUP_CH = 16
S5_GROUPS = 64
S5_STATE = 64
S5_LANES = S5_GROUPS * S5_STATE
MOE_GROUPS = 4
MOE_EPG = 8
MOE_EXPERTS = MOE_GROUPS * MOE_EPG
MOE_D_FF = 512
EPS = 1e-6

LANES = 128
SUBLANES = 8
VMEM_LIMIT = 56 * 1024 * 1024

SSD_CHUNKS_PER_STEP = 2
S5_TIME_TILE = 64
S5_SCAN_LANES = 512
MOE_TILE = 256
SLAB_ROWS = D_MODEL // LANES
PACK_ROWS = SLAB_ROWS // 2
SC_CORES = 2
SC_SUBCORES = 16
SC_WORKERS = SC_CORES * SC_SUBCORES
SC_DISPATCH_ROWS = 32
SC_COLLECT_ROWS = (32, 40)
TOK_TILE = 512


def _dot(a, b):
    return jnp.dot(a, b, preferred_element_type=F32)


def _rms(x, g):
    return x * lax.rsqrt(jnp.mean(x * x, axis=-1, keepdims=True) + EPS) * g


def _softplus(x):
    return jnp.maximum(x, 0.0) + jnp.log1p(jnp.exp(-jnp.abs(x)))


def _split3(x):
    hi = x.astype(BF16)
    r = x - hi.astype(F32)
    mid = r.astype(BF16)
    lo = (r - mid.astype(F32)).astype(BF16)
    return hi, mid, lo


def _dot3(x, w):
    hi, mid, lo = _split3(x)
    return _dot(hi, w) + _dot(mid, w) + _dot(lo, w)


def _dot3_left(w, x):
    hi, mid, lo = _split3(x)
    return _dot(w, hi) + _dot(w, mid) + _dot(w, lo)


def _pack_bf16_pairs(x):
    bits = pltpu.bitcast(x.astype(BF16).astype(F32), jnp.uint32)
    half = x.shape[1] // 2
    return (bits[:, :half] & jnp.uint32(0xFFFF0000)) | (bits[:, half:] >> jnp.uint32(16))


def _unpack_bf16_pairs(ref, rows):
    words = [ref[pl.ds(j, rows, stride=PACK_ROWS), :] for j in range(PACK_ROWS)]
    high = [pltpu.bitcast(w & jnp.uint32(0xFFFF0000), F32) for w in words]
    low = [pltpu.bitcast(w << jnp.uint32(16), F32) for w in words]
    return jnp.concatenate(high + low, axis=-1)


def _full_spec(a):
    nd = a.ndim
    return pl.BlockSpec(a.shape, lambda *_: (0,) * nd)


def _resident_spec(a):
    nd = a.ndim
    return pl.BlockSpec(a.shape, lambda *_: (0,) * nd, pipeline_mode=pl.Buffered(1))


def _in_proj_body(x_ref, g_ref, wz_ref, wx_ref, wdt_ref, wu_ref, z_ref, xbc_ref, dt_ref, u_ref):
    xb = _rms(x_ref[...], g_ref[...]).astype(BF16)
    z_ref[...] = _dot(xb, wz_ref[...]).astype(z_ref.dtype)
    xbc_ref[...] = _dot(xb, wx_ref[...]).astype(xbc_ref.dtype)
    dt_ref[...] = _dot(xb, wdt_ref[...])
    u_ref[...] = _dot(xb, wu_ref[...]).astype(u_ref.dtype)


def _in_proj(x2d, g, wz, wx, wdt, wu, tm, act_dtype, u_dtype):
    rows = x2d.shape[0]
    row = lambda w: pl.BlockSpec((tm, w), lambda i: (i, 0))
    return pl.pallas_call(
        _in_proj_body,
        grid=(rows // tm,),
        in_specs=[row(D_MODEL), _full_spec(g), _full_spec(wz), _full_spec(wx), _full_spec(wdt), _full_spec(wu)],
        out_specs=[row(SSD_WIDTH), row(SSD_CONV_DIM), row(LANES), row(S5_WIDTH)],
        out_shape=[jax.ShapeDtypeStruct((rows, SSD_WIDTH), act_dtype),
                   jax.ShapeDtypeStruct((rows, SSD_CONV_DIM), act_dtype),
                   jax.ShapeDtypeStruct((rows, LANES), F32),
                   jax.ShapeDtypeStruct((rows, S5_WIDTH), u_dtype)],
        compiler_params=pltpu.CompilerParams(dimension_semantics=("parallel",), vmem_limit_bytes=VMEM_LIMIT),
        name="in_proj",
    )(x2d, g, wz, wx, wdt, wu)


def _ssd_body(mask_rows, per_step, *refs):
    for k in range(per_step):
        _ssd_chunk(mask_rows, per_step, k, *refs)


def _ssd_chunk(mask_rows, per_step, k, xbc_ref, dt_ref, z_ref, cinit_ref, hinit_ref, cw_ref, cb_ref, dtb_ref,
               alog_ref, dexp_ref, nrm_ref, eexp_ref, y_ref, ctail_ref, st_ref, hto_ref, xwin, hT):
    L = SSD_CHUNK
    c = pl.program_id(1) * per_step + k
    n_chunks = pl.num_programs(1) * per_step
    window = pl.ds(k * L, L)
    xbc_ref, dt_ref, z_ref, y_ref = (r.at[:, window, :] for r in (xbc_ref, dt_ref, z_ref, y_ref))

    @pl.when(c == 0)
    def _init():
        xwin[...] = cinit_ref[0]
        hT[...] = hinit_ref[0]

    x_b = xbc_ref[0]
    x_f = x_b.astype(F32)
    taps = SSD_CONV - 1
    m_i = lax.broadcasted_iota(jnp.int32, (taps * L, L), 0)
    r_i = lax.broadcasted_iota(jnp.int32, (taps * L, L), 1)
    shift = (r_i + (taps - m_i // L) == m_i % L).astype(BF16)
    shifted = _dot(shift, x_b)
    acc = cb_ref[...] + x_f * cw_ref[taps:taps + 1, :]
    for k in range(taps):
        acc = acc + shifted[k * L:(k + 1) * L, :] * cw_ref[k:k + 1, :]
    joint = jnp.concatenate([xwin[...], x_f[0:SUBLANES, :]], axis=0)
    row8 = lax.broadcasted_iota(jnp.int32, (SUBLANES, 1), 0)
    head = acc[0:SUBLANES, :]
    for k in range(taps):
        d = taps - k
        head = head + jnp.where(row8 < d, joint[SUBLANES - d:2 * SUBLANES - d, :], 0.0) * cw_ref[k:k + 1, :]
    acc = jnp.concatenate([head, acc[SUBLANES:, :]], axis=0)
    tail = x_f[L - SUBLANES:, :]
    xwin[...] = tail
    ctail_ref[0] = tail

    xact = acc * jax.nn.sigmoid(acc)
    dt = _softplus(dt_ref[0] + dtb_ref[...])
    if mask_rows:
        valid = lax.broadcasted_iota(jnp.int32, (L, 1), 0) >= mask_rows
        xact = jnp.where(valid, xact, 0.0)
        dt = jnp.where(valid, dt, 0.0)

    a_neg = -jnp.exp(alog_ref[...])
    dA = dt * a_neg
    row_i = lax.broadcasted_iota(jnp.int32, (L, L), 0)
    col_i = lax.broadcasted_iota(jnp.int32, (L, L), 1)
    causal = row_i >= col_i
    tril = causal.astype(BF16)
    cs = _dot3_left(tril, dA)
    csT = cs.T
    dtT = dt.T
    ecs = jnp.exp(cs)
    wdec = jnp.exp(cs[L - 1:L, :] - cs) * dt
    eexp = eexp_ref[...]
    ecs_e = _dot3(ecs, eexp)
    wdec_e = _dot3(wdec, eexp)
    lane = lax.broadcasted_iota(jnp.int32, (L, LANES), 1)
    first_half = lane < SSD_HEAD_DIM

    gw = SSD_HPG * SSD_HEAD_DIM
    y_groups = []
    for g in range(SSD_GROUPS):
        b_g = xact[:, SSD_WIDTH + g * SSD_STATE: SSD_WIDTH + (g + 1) * SSD_STATE]
        c_g = xact[:, SSD_WIDTH + (SSD_GROUPS + g) * SSD_STATE: SSD_WIDTH + (SSD_GROUPS + g + 1) * SSD_STATE]
        b_b = b_g.astype(BF16)
        c_b = c_g.astype(BF16)
        cb = lax.dot_general(c_b, b_b, (((1,), (1,)), ((), ())), preferred_element_type=F32)
        xs_g = xact[:, g * gw:(g + 1) * gw]
        h_prev = hT[g]
        y_off = _dot(c_b, h_prev.astype(BF16)) * ecs_e[:, g * gw:(g + 1) * gw]
        xdec = (xs_g * wdec_e[:, g * gw:(g + 1) * gw]).astype(BF16)
        hT[g] = h_prev * ecs_e[L - 1:L, g * gw:(g + 1) * gw] + _dot(b_g.T.astype(BF16), xdec)
        pieces = []
        for j in range(SSD_HPG // 2):
            xs_pair = xs_g[:, j * LANES:(j + 1) * LANES]
            halves = (jnp.where(first_half, xs_pair, 0.0).astype(BF16),
                      jnp.where(first_half, 0.0, xs_pair).astype(BF16))
            yd = None
            for t in range(2):
                h = g * SSD_HPG + 2 * j + t
                seg = cs[:, h:h + 1] - csT[h:h + 1, :]
                lmat = jnp.exp(jnp.where(causal, seg, -jnp.inf))
                m = (cb * lmat * dtT[h:h + 1, :]).astype(BF16)
                part = _dot(m, halves[t])
                yd = part if yd is None else yd + part
            pieces.append(yd)
        y_groups.append(jnp.concatenate(pieces, axis=-1) + y_off + dexp_ref[:, g * gw:(g + 1) * gw] * xs_g)
    y = jnp.concatenate(y_groups, axis=-1)
    z = z_ref[0].astype(F32)
    y_ref[0] = _rms(y * (z * jax.nn.sigmoid(z)), nrm_ref[...]).astype(y_ref.dtype)

    @pl.when(c == n_chunks - 1)
    def _emit():
        hto_ref[0] = hT[...]
        for g in range(SSD_GROUPS):
            t = hT[g].T
            for k in range(SSD_HPG):
                st_ref[0, g * SSD_HPG + k] = t[k * SSD_HEAD_DIM:(k + 1) * SSD_HEAD_DIM, :]


def _ssd_chunked(xbc, dt, z, cinit, hinit, cw, cb, dtb, alog, dexp, nrm, eexp, mask_rows):
    bsz, seq, _ = xbc.shape
    nc = seq // SSD_CHUNK
    per_step = SSD_CHUNKS_PER_STEP if nc % SSD_CHUNKS_PER_STEP == 0 else 1
    gw = SSD_HPG * SSD_HEAD_DIM
    blk = lambda w: pl.BlockSpec((1, per_step * SSD_CHUNK, w), lambda b, c: (b, c, 0))
    return pl.pallas_call(
        functools.partial(_ssd_body, mask_rows, per_step),
        grid=(bsz, nc // per_step),
        in_specs=[blk(SSD_CONV_DIM), blk(LANES), blk(SSD_WIDTH),
                  pl.BlockSpec((1, SUBLANES, SSD_CONV_DIM), lambda b, c: (0, 0, 0)),
                  pl.BlockSpec((1, SSD_GROUPS, SSD_STATE, gw), lambda b, c: (0, 0, 0, 0)),
                  _full_spec(cw), _full_spec(cb), _full_spec(dtb), _full_spec(alog),
                  _full_spec(dexp), _full_spec(nrm), _full_spec(eexp)],
        out_specs=[blk(SSD_WIDTH),
                   pl.BlockSpec((1, SUBLANES, SSD_CONV_DIM), lambda b, c: (b, 0, 0)),
                   pl.BlockSpec((1, SSD_HEADS, SSD_HEAD_DIM, SSD_STATE), lambda b, c: (b, 0, 0, 0)),
                   pl.BlockSpec((1, SSD_GROUPS, SSD_STATE, gw), lambda b, c: (b, 0, 0, 0))],
        out_shape=[jax.ShapeDtypeStruct((bsz, seq, SSD_WIDTH), BF16),
                   jax.ShapeDtypeStruct((bsz, SUBLANES, SSD_CONV_DIM), F32),
                   jax.ShapeDtypeStruct((bsz, SSD_HEADS, SSD_HEAD_DIM, SSD_STATE), F32),
                   jax.ShapeDtypeStruct((bsz, SSD_GROUPS, SSD_STATE, gw), F32)],
        scratch_shapes=[pltpu.VMEM((SUBLANES, SSD_CONV_DIM), F32),
                        pltpu.VMEM((SSD_GROUPS, SSD_STATE, gw), F32)],
        compiler_params=pltpu.CompilerParams(dimension_semantics=("parallel", "arbitrary"),
                                             vmem_limit_bytes=VMEM_LIMIT),
        name="ssd_chunked",
    )(xbc, dt, z, cinit, hinit, cw, cb, dtb, alog, dexp, nrm, eexp)


def _ssd_step_prep_body(xbc_ref, c0_ref, c1_ref, c2_ref, dt_ref, cw_ref, cb_ref, dtb_ref, alog_ref,
                        xt_ref, dt_out_ref, dec_ref, bc_ref, xs_ref):
    acc = cb_ref[...]
    for k, r in enumerate((c0_ref, c1_ref, c2_ref, xbc_ref)):
        acc = acc + r[...] * cw_ref[k:k + 1, :]
    xact = acc * jax.nn.sigmoid(acc)
    xs = xact[:, :SSD_WIDTH]
    dt = _softplus(dt_ref[...] + dtb_ref[...])
    dt_out_ref[...] = dt
    dec_ref[...] = jnp.exp(dt * -jnp.exp(alog_ref[...]))
    bc_ref[...] = xact[:, SSD_WIDTH:]
    xs_ref[...] = xs
    xt_ref[...] = xs.T.astype(xt_ref.dtype)


def _ssd_step_prep(xbc, c0, c1, c2, dt, cw, cb, dtb, alog):
    n = xbc.shape[0]
    args = (xbc, c0, c1, c2, dt, cw, cb, dtb, alog)
    spec = lambda r, w: pl.BlockSpec((r, w), lambda: (0, 0))
    return pl.pallas_call(
        _ssd_step_prep_body,
        in_specs=[_full_spec(a) for a in args],
        out_specs=[spec(SSD_WIDTH, n), spec(n, LANES), spec(n, LANES), spec(n, 2 * SSD_GROUPS * SSD_STATE),
                   spec(n, SSD_WIDTH)],
        out_shape=[jax.ShapeDtypeStruct((SSD_WIDTH, n), BF16), jax.ShapeDtypeStruct((n, LANES), F32),
                   jax.ShapeDtypeStruct((n, LANES), F32),
                   jax.ShapeDtypeStruct((n, 2 * SSD_GROUPS * SSD_STATE), F32),
                   jax.ShapeDtypeStruct((n, SSD_WIDTH), F32)],
        compiler_params=pltpu.CompilerParams(vmem_limit_bytes=VMEM_LIMIT),
        name="ssd_step_prep",
    )(*args)


def _ssd_step_body(dt_ref, dec_ref, st_ref, xt_ref, bc_ref, so_ref, y_ref):
    n = xt_ref.shape[1]
    gw = SSD_HPG * SSD_HEAD_DIM
    blk = pl.program_id(0)
    seq_id = lax.broadcasted_iota(jnp.int32, (n, SSD_STATE), 0)
    sub_id = lax.broadcasted_iota(jnp.int32, (SUBLANES, gw), 0)
    base = pl.multiple_of(blk * SUBLANES, SUBLANES)
    y_acc = [jnp.zeros((SUBLANES, gw), F32) for _ in range(SSD_GROUPS)]
    for i in range(SUBLANES):
        s = blk * SUBLANES + i
        for g in range(SSD_GROUPS):
            b_all = bc_ref[:, g * SSD_STATE:(g + 1) * SSD_STATE]
            rhs = jnp.where(seq_id == s, b_all, 0.0).astype(BF16)
            outer = _dot(xt_ref[g * gw:(g + 1) * gw, :], rhs)
            news = []
            for k in range(SSD_HPG):
                h = g * SSD_HPG + k
                new = (dec_ref[s * SSD_HEADS + h] * st_ref[i, h]
                       + dt_ref[s * SSD_HEADS + h] * outer[k * SSD_HEAD_DIM:(k + 1) * SSD_HEAD_DIM, :])
                so_ref[i, h] = new
                news.append(new)
            new_g = jnp.concatenate(news, axis=0).astype(BF16)
            c_lo = (SSD_GROUPS + g) * SSD_STATE
            c_blk = bc_ref[pl.ds(base, SUBLANES), c_lo:c_lo + SSD_STATE].astype(BF16)
            r = lax.dot_general(c_blk, new_g, (((1,), (1,)), ((), ())), preferred_element_type=F32)
            y_acc[g] = y_acc[g] + jnp.where(sub_id == i, r, 0.0)
    y_ref[...] = jnp.concatenate(y_acc, axis=-1)


def _ssd_step(dt_flat, dec_flat, state, xt, bc):
    n = state.shape[0]
    st_spec = pl.BlockSpec((SUBLANES, SSD_HEADS, SSD_HEAD_DIM, SSD_STATE), lambda i, *_: (i, 0, 0, 0))
    return pl.pallas_call(
        _ssd_step_body,
        grid_spec=pltpu.PrefetchScalarGridSpec(
            num_scalar_prefetch=2,
            grid=(n // SUBLANES,),
            in_specs=[st_spec, pl.BlockSpec(xt.shape, lambda i, *_: (0, 0)),
                      pl.BlockSpec(bc.shape, lambda i, *_: (0, 0))],
            out_specs=[st_spec, pl.BlockSpec((SUBLANES, SSD_WIDTH), lambda i, *_: (i, 0))]),
        out_shape=[jax.ShapeDtypeStruct(state.shape, F32), jax.ShapeDtypeStruct((n, SSD_WIDTH), F32)],
        compiler_params=pltpu.CompilerParams(dimension_semantics=("parallel",), vmem_limit_bytes=VMEM_LIMIT),
        name="ssd_step",
    )(dt_flat, dec_flat, state, xt, bc)


def _s5_project_in(u_b16, wb_ref, store):
    kw = 16 * S5_GROUP_CH
    nw = 16 * S5_STATE
    for j in range(S5_WIDTH // kw):
        r = _dot(u_b16[:, j * kw:(j + 1) * kw], wb_ref[j])
        store(j, r[:, :nw], r[:, nw:])


def _s5_tail(hre_of, him_of, u_f32, wcr_ref, wci_ref, d_ref, wglu_ref, bglu_ref, nrm_ref):
    cols = []
    for j in range(wcr_ref.shape[0]):
        cols.append(_dot(hre_of(j).astype(BF16), wcr_ref[j]) + _dot(him_of(j).astype(BF16), wci_ref[j]))
    return _s5_finish(cols, u_f32, d_ref, wglu_ref, bglu_ref, nrm_ref)


def _s5_finish(cols, u_f32, d_ref, wglu_ref, bglu_ref, nrm_ref):
    y = jnp.concatenate(cols, axis=-1) + d_ref[...] * u_f32
    y = jax.nn.gelu(y)
    y = y * jax.nn.sigmoid(_dot(y.astype(BF16), wglu_ref[...]) + bglu_ref[...])
    return _rms(y, nrm_ref[...])


def _s5_seq_body(u_hbm, um_ref, wb_ref, abr_ref, abi_ref, wcr_ref, wci_ref, d_ref, wglu_ref, bglu_ref, nrm_ref,
                 y_hbm, sre_ref, sim_ref, ubuf, ybuf, bu, h, in_sems, out_sems):
    j = pl.program_id(0)
    last = pl.num_programs(0) - 1
    lc, bsz = ubuf.shape[1], ubuf.shape[2]
    rows = lc * bsz
    nw = 16 * S5_STATE

    def in_copy(step, b):
        return pltpu.make_async_copy(u_hbm.at[b, pl.ds(step * lc, lc), :], ubuf.at[step % 2, :, b, :],
                                     in_sems.at[step % 2, b])

    def out_copy(step, b):
        return pltpu.make_async_copy(ybuf.at[step % 2, :, b, :], y_hbm.at[b, pl.ds(step * lc, lc), :],
                                     out_sems.at[step % 2, b])

    def project_in(u_b16, nrows):
        def store(jj, re, im):
            bu[0:nrows, jj * nw:(jj + 1) * nw] = re
            bu[0:nrows, S5_LANES + jj * nw:S5_LANES + (jj + 1) * nw] = im
        _s5_project_in(u_b16, wb_ref, store)

    def scan(nsteps):
        for k in range(S5_LANES // S5_SCAN_LANES):
            sl_r = pl.ds(k * S5_SCAN_LANES, S5_SCAN_LANES)
            sl_i = pl.ds(S5_LANES + k * S5_SCAN_LANES, S5_SCAN_LANES)
            ar = abr_ref[:, sl_r]
            ai = abi_ref[:, sl_r]

            def step(l, carry):
                hr, hi = carry
                slab = pl.ds(pl.multiple_of(l * bsz, bsz), bsz)
                nr = ar * hr - ai * hi + bu[slab, sl_r]
                ni = ar * hi + ai * hr + bu[slab, sl_i]
                bu[slab, sl_r] = nr
                bu[slab, sl_i] = ni
                return nr, ni

            hr, hi = lax.fori_loop(0, nsteps, step, (h[:, sl_r], h[:, sl_i]))
            h[:, sl_r] = hr
            h[:, sl_i] = hi

    @pl.when(j == 0)
    def _first():
        for b in range(bsz):
            in_copy(0, b).start()
        h[...] = jnp.zeros_like(h)
        project_in(um_ref[...], N_META * bsz)
        scan(N_META)

    @pl.when(j < last)
    def _prefetch():
        for b in range(bsz):
            in_copy(j + 1, b).start()

    for b in range(bsz):
        in_copy(j, b).wait()
    u2 = ubuf[j % 2].reshape(rows, S5_WIDTH)
    u_b16 = u2.astype(BF16)
    kw = 16 * S5_GROUP_CH

    def project_block(jj):
        r = _dot(u_b16[:, jj * kw:(jj + 1) * kw], wb_ref[jj])
        bu[0:rows, jj * nw:(jj + 1) * nw] = r[:, :nw]
        bu[0:rows, S5_LANES + jj * nw:S5_LANES + (jj + 1) * nw] = r[:, nw:]

    def scan_block(jj):
        for k in range(nw // S5_SCAN_LANES):
            lo = jj * nw + k * S5_SCAN_LANES
            sl_r = slice(lo, lo + S5_SCAN_LANES)
            sl_i = slice(S5_LANES + lo, S5_LANES + lo + S5_SCAN_LANES)
            ar, ai = abr_ref[:, sl_r], abi_ref[:, sl_r]
            hr, hi = h[:, sl_r], h[:, sl_i]
            for l in range(lc):
                slab = slice(l * bsz, (l + 1) * bsz)
                hr, hi = (ar * hr - ai * hi + bu[slab, sl_r], ar * hi + ai * hr + bu[slab, sl_i])
                bu[slab, sl_r] = hr
                bu[slab, sl_i] = hi
            h[:, sl_r] = hr
            h[:, sl_i] = hi

    def readout_block(jj):
        return (_dot(bu[:, jj * nw:(jj + 1) * nw].astype(BF16), wcr_ref[jj])
                + _dot(bu[:, S5_LANES + jj * nw:S5_LANES + (jj + 1) * nw].astype(BF16), wci_ref[jj]))

    n_blocks = S5_WIDTH // kw
    project_block(0)
    cols = []
    for jj in range(n_blocks):
        if jj + 1 < n_blocks:
            project_block(jj + 1)
        scan_block(jj)
        cols.append(readout_block(jj))
    y = _s5_finish(cols, u2, d_ref, wglu_ref, bglu_ref, nrm_ref)
    ybuf[j % 2] = y.reshape(lc, bsz, S5_WIDTH)
    for b in range(bsz):
        out_copy(j, b).start()

    @pl.when(j > 0)
    def _wait_previous_out():
        for b in range(bsz):
            out_copy(j - 1, b).wait()

    @pl.when(j == last)
    def _emit():
        for b in range(bsz):
            out_copy(j, b).wait()
        sre_ref[...] = h[:, 0:S5_LANES]
        sim_ref[...] = h[:, S5_LANES:]


def _s5_seq(u, um, wb, abr, abi, wcr, wci, d, wglu, bglu, nrm):
    bsz, seq, _ = u.shape
    lc = S5_TIME_TILE
    consts = (um, wb, abr, abi, wcr, wci, d, wglu, bglu, nrm)
    st = pl.BlockSpec((bsz, S5_LANES), lambda j: (0, 0))
    return pl.pallas_call(
        _s5_seq_body,
        grid=(seq // lc,),
        in_specs=[pl.BlockSpec(memory_space=pl.ANY)] + [_resident_spec(a) for a in consts],
        out_specs=[pl.BlockSpec(memory_space=pl.ANY), st, st],
        out_shape=[jax.ShapeDtypeStruct((bsz, seq, S5_WIDTH), F32),
                   jax.ShapeDtypeStruct((bsz, S5_LANES), F32), jax.ShapeDtypeStruct((bsz, S5_LANES), F32)],
        scratch_shapes=[pltpu.VMEM((2, lc, bsz, S5_WIDTH), F32), pltpu.VMEM((2, lc, bsz, S5_WIDTH), F32),
                        pltpu.VMEM((lc * bsz, 2 * S5_LANES), F32), pltpu.VMEM((bsz, 2 * S5_LANES), F32),
                        pltpu.SemaphoreType.DMA((2, bsz)), pltpu.SemaphoreType.DMA((2, bsz))],
        compiler_params=pltpu.CompilerParams(dimension_semantics=("arbitrary",), vmem_limit_bytes=VMEM_LIMIT),
        name="s5_seq",
    )(u, *consts)


def _sample_post_body(yc_ref, xs_ref, z_ref, dexp_ref, snrm_ref, u_ref, hr_ref, hi_ref, wb_ref, abr_ref, abi_ref,
                      wcr_ref, wci_ref, d_ref, wglu_ref, bglu_ref, nrm_ref,
                      yssd_ref, ys5_ref, nre_ref, nim_ref):
    z = z_ref[...]
    y = yc_ref[...] + dexp_ref[...] * xs_ref[...]
    yssd_ref[...] = _rms(y * (z * jax.nn.sigmoid(z)), snrm_ref[...]).astype(yssd_ref.dtype)

    u = u_ref[...]
    nw = 16 * S5_STATE
    ar, ai = abr_ref[...], abi_ref[...]

    def store(jj, re, im):
        sl = slice(jj * nw, (jj + 1) * nw)
        h0r, h0i = hr_ref[:, sl], hi_ref[:, sl]
        nre_ref[:, sl] = ar[:, sl] * h0r - ai[:, sl] * h0i + re
        nim_ref[:, sl] = ar[:, sl] * h0i + ai[:, sl] * h0r + im

    _s5_project_in(u.astype(BF16), wb_ref, store)
    slab = lambda ref: (lambda jj: ref[:, jj * nw:(jj + 1) * nw])
    y5 = _s5_tail(slab(nre_ref), slab(nim_ref), u, wcr_ref, wci_ref, d_ref, wglu_ref, bglu_ref, nrm_ref)
    ys5_ref[...] = y5.astype(ys5_ref.dtype)


def _sample_post(yc, xs, z, dexp, snrm, u, h0r, h0i, wb, abr1, abi1, wcr, wci, d, wglu, bglu, nrm):
    n = yc.shape[0]
    args = (yc, xs, z, dexp, snrm, u, h0r, h0i, wb, abr1, abi1, wcr, wci, d, wglu, bglu, nrm)
    spec = lambda w: pl.BlockSpec((n, w), lambda: (0, 0))
    return pl.pallas_call(
        _sample_post_body,
        in_specs=[_full_spec(a) for a in args],
        out_specs=[spec(SSD_WIDTH), spec(S5_WIDTH), spec(S5_LANES), spec(S5_LANES)],
        out_shape=[jax.ShapeDtypeStruct((n, SSD_WIDTH), BF16), jax.ShapeDtypeStruct((n, S5_WIDTH), BF16),
                   jax.ShapeDtypeStruct((n, S5_LANES), F32), jax.ShapeDtypeStruct((n, S5_LANES), F32)],
        compiler_params=pltpu.CompilerParams(vmem_limit_bytes=VMEM_LIMIT),
        name="sample_post",
    )(*args)


def _mix_route_body(n_blocks, n_sorted, xp_ref, ysp_ref, y5p_ref, xs_ref, yss_ref, y5s_ref, *refs):
    consts = refs[:6]
    x1_ref, xn_hbm, rt_ref, pos_ref, meta_ref, carry, fields, xbuf, sems = refs[6:]
    i = pl.program_id(0)
    tm, n_sample = xp_ref.shape[0], xs_ref.shape[0]
    col0 = pl.multiple_of(i * tm, LANES)

    def xn_copy(step, rows, j):
        return pltpu.make_async_copy(xbuf.at[step % 2, pl.ds(0, rows), pl.ds(j * LANES, LANES)],
                                     xn_hbm.at[pl.ds(step * tm, rows), j, :], sems.at[step % 2, j])

    @pl.when(i == 0)
    def _init():
        carry[...] = jnp.zeros_like(carry)

    @pl.when(i < n_blocks)
    def _prompt_rows():
        _mix_route_compute(xp_ref, ysp_ref, y5p_ref, *consts, x1_ref, rt_ref, carry, xbuf.at[i % 2], fields, col0)
        for j in range(PACK_ROWS):
            xn_copy(i, tm, j).start()

    @pl.when(i == n_blocks)
    def _sample_rows():
        _mix_route_compute(xs_ref, yss_ref, y5s_ref, *consts, x1_ref, rt_ref, carry, xbuf.at[i % 2], fields, col0)
        for j in range(PACK_ROWS):
            xn_copy(i, n_sample, j).start()
        _route_layout(carry, fields, pos_ref, meta_ref, n_sorted)
        for j in range(PACK_ROWS):
            xn_copy(i, n_sample, j).wait()

    @pl.when(i > 0)
    def _wait_previous_rows():
        for j in range(PACK_ROWS):
            xn_copy(i - 1, tm, j).wait()


def _route_layout(carry, fields, pos_ref, meta_ref, n_sorted):
    counts = carry[...]
    tiles_per = jnp.floor((counts + (MOE_TILE - 1)) * (1.0 / MOE_TILE))
    upto = lax.broadcasted_iota(jnp.int32, (LANES, LANES), 0) <= lax.broadcasted_iota(jnp.int32, (LANES, LANES), 1)
    tile_end = _dot(tiles_per.astype(BF16), upto.astype(BF16))
    pstart = (tile_end - tiles_per) * MOE_TILE
    n_used = tile_end[:, MOE_EXPERTS - 1:MOE_EXPERTS]

    f = fields[...]
    first_row = jnp.zeros_like(f)
    tile_id = jnp.minimum(lax.broadcasted_iota(jnp.int32, meta_ref.shape, 1).astype(F32), n_used - 1.0)
    tile_expert = jnp.zeros(meta_ref.shape, F32)
    for e in range(MOE_EXPERTS):
        first_row = first_row + jnp.where(f == float(e), pstart[:, e:e + 1], 0.0)
        tile_expert = tile_expert + jnp.where(tile_end[:, e:e + 1] <= tile_id, 1.0, 0.0)
    pos = first_row + pltpu.roll(f, shift=4, axis=0)
    pos_ref[...] = jnp.clip(pos, 0.0, n_sorted - 1.0).astype(jnp.int32)
    is_row0 = lax.broadcasted_iota(jnp.int32, meta_ref.shape, 0) == 0
    meta_ref[...] = jnp.where(is_row0, tile_expert, n_used).astype(jnp.int32)


def _mix_route_compute(x_ref, ys_ref, y5_ref, wa_ref, wb_ref, nf_ref, wrh_ref, wrl_ref, br_ref,
                       x1_ref, rt_ref, carry, xn_buf, fields, col0):
    rows = x_ref.shape[0]
    x1 = x_ref[...] + _dot(ys_ref[...], wa_ref[...]) + _dot(y5_ref[...].astype(BF16), wb_ref[...])
    x1_ref[0:rows, :] = x1
    xn = _rms(x1, nf_ref[...])
    xn_buf[0:rows, :] = _pack_bf16_pairs(xn)

    xh = xn.astype(BF16)
    xl = (xn - xh.astype(F32)).astype(BF16)
    logits = _dot(xh, wrh_ref[...]) + _dot(xl, wrh_ref[...]) + _dot(xh, wrl_ref[...]) + br_ref[...]
    tm = logits.shape[0]
    lane = lax.broadcasted_iota(jnp.int32, logits.shape, 1).astype(F32)
    neg = -jnp.inf
    big = float(LANES)

    def first_max(v):
        m = jnp.max(v, axis=-1, keepdims=True)
        return m, jnp.min(jnp.where(v == m, lane, big), axis=-1, keepdims=True)

    coarse = lane < MOE_GROUPS
    mc, gsel = first_max(jnp.where(coarse, logits, neg))
    psel = 1.0 / jnp.sum(jnp.where(coarse, jnp.exp(logits - mc), 0.0), axis=-1, keepdims=True)
    lo = MOE_GROUPS + MOE_EPG * gsel
    lf = jnp.where((lane >= lo) & (lane < lo + MOE_EPG), logits, neg)
    m1, i1 = first_max(lf)
    m2, i2 = first_max(jnp.where(lane == i1, neg, lf))
    e2 = jnp.exp(m2 - m1)
    g1 = psel / (1.0 + e2)
    g2 = psel * e2 / (1.0 + e2)
    e_a, e_b = i1 - MOE_GROUPS, i2 - MOE_GROUPS

    pick_a, pick_b = lane == e_a, lane == e_b
    picks = jnp.where(pick_a | pick_b, 1.0, 0.0)
    earlier = lax.broadcasted_iota(jnp.int32, (tm, tm), 0) > lax.broadcasted_iota(jnp.int32, (tm, tm), 1)
    prior = _dot(earlier.astype(BF16), picks.astype(BF16)) + carry[...]
    rank_a = jnp.sum(jnp.where(pick_a, prior, 0.0), axis=-1, keepdims=True)
    rank_b = jnp.sum(jnp.where(pick_b, prior, 0.0), axis=-1, keepdims=True)
    carry[...] = prior[tm - 1:tm, :] + picks[tm - 1:tm, :]

    out = jnp.zeros_like(logits)
    for k, v in enumerate((e_a, e_b, g1, g2, rank_a, rank_b)):
        out = jnp.where(lane == float(k), v, out)
    rt_ref[0:rows, :] = out
    fields[:, pl.ds(col0, rows)] = out.T[0:SUBLANES, :]


def _mix_route(prompt, sample, consts, tm, n_tiles):
    n_prompt, n_sample = prompt[0].shape[0], sample[0].shape[0]
    assert n_prompt % tm == 0 and n_sample <= tm
    n_blocks = n_prompt // tm
    total_rows = n_prompt + n_sample
    row = lambda w: pl.BlockSpec((tm, w), lambda i: (jnp.minimum(i, n_blocks - 1), 0))
    out_row = lambda w: pl.BlockSpec((tm, w), lambda i: (i, 0))
    assert total_rows % LANES == 0 and n_tiles <= 2 * LANES
    whole = lambda shape: pl.BlockSpec(shape, lambda i: (0, 0))
    return pl.pallas_call(
        functools.partial(_mix_route_body, n_blocks, n_tiles * MOE_TILE),
        grid=(n_blocks + 1,),
        in_specs=([row(D_MODEL), row(SSD_WIDTH), row(S5_WIDTH)] + [_full_spec(a) for a in sample]
                  + [_full_spec(a) for a in consts]),
        out_specs=[out_row(D_MODEL), pl.BlockSpec(memory_space=pl.ANY), out_row(LANES),
                   whole((SUBLANES, total_rows)), whole((SUBLANES, 2 * LANES))],
        out_shape=[jax.ShapeDtypeStruct((total_rows, D_MODEL), F32),
                   jax.ShapeDtypeStruct((total_rows, PACK_ROWS, LANES), jnp.uint32),
                   jax.ShapeDtypeStruct((total_rows, LANES), F32),
                   jax.ShapeDtypeStruct((SUBLANES, total_rows), jnp.int32),
                   jax.ShapeDtypeStruct((SUBLANES, 2 * LANES), jnp.int32)],
        scratch_shapes=[pltpu.VMEM((1, LANES), F32), pltpu.VMEM((SUBLANES, total_rows), F32),
                        pltpu.VMEM((2, tm, D_MODEL // 2), jnp.uint32), pltpu.SemaphoreType.DMA((2, PACK_ROWS))],
        compiler_params=pltpu.CompilerParams(dimension_semantics=("arbitrary",), vmem_limit_bytes=VMEM_LIMIT),
        name="mix_route",
    )(*prompt, *sample, *consts)


def _sc_mesh():
    return plsc.VectorSubcoreMesh(core_axis_name="c", subcore_axis_name="s")


def _sc_worker():
    return lax.axis_index("s") * SC_CORES + lax.axis_index("c")


def _sc_dispatch(xn, pos_a, pos_b, n_rows):
    n_tok = xn.shape[0]
    ch = SC_DISPATCH_ROWS
    n_chunks = n_tok // ch
    assert n_tok % ch == 0 and n_chunks >= SC_WORKERS
    max_mine = -(-n_chunks // SC_WORKERS)
    row_shape, dtype = xn.shape[1:], xn.dtype
    stage = [pltpu.VMEM((ch,), jnp.int32), pltpu.VMEM((ch,), jnp.int32), pltpu.VMEM((ch,) + row_shape, dtype),
             pltpu.SemaphoreType.DMA]

    @functools.partial(
        pl.kernel, mesh=_sc_mesh(),
        out_type=jax.ShapeDtypeStruct((n_rows,) + row_shape, dtype),
        scratch_types=stage + stage + [pltpu.SemaphoreType.DMA])
    def push(xn_hbm, pa_hbm, pb_hbm, xs_hbm, ia0, ib0, rows0, lsem0, ia1, ib1, rows1, lsem1, ssem):
        wid = _sc_worker()
        mine = (n_chunks - wid + SC_WORKERS - 1) // SC_WORKERS
        bufs = ((ia0, ib0, rows0, lsem0), (ia1, ib1, rows1, lsem1))

        def loads(t, b):
            ia, ib, rows, sem = bufs[b]
            off = pl.multiple_of((wid + t * SC_WORKERS) * ch, ch)
            return (pltpu.make_async_copy(pa_hbm.at[pl.ds(off, ch)], ia, sem),
                    pltpu.make_async_copy(pb_hbm.at[pl.ds(off, ch)], ib, sem),
                    pltpu.make_async_copy(xn_hbm.at[pl.ds(off, ch)], rows, sem))

        def stage_in(t, b):
            for c in loads(t, b):
                c.start()

        def scatter(t, b):
            ia, ib, rows, _ = bufs[b]
            for c in loads(t, b):
                c.wait()
            first = pltpu.async_copy(rows, xs_hbm.at[ia], ssem)
            second = pltpu.async_copy(rows, xs_hbm.at[ib], ssem)
            first.wait()
            second.wait()

        stage_in(0, 0)

        @pl.loop(0, (max_mine + 1) // 2)
        def _(p):
            t = 2 * p

            @pl.when(t + 1 < mine)
            def _():
                stage_in(t + 1, 1)

            @pl.when(t < mine)
            def _():
                scatter(t, 0)

            @pl.when(t + 2 < mine)
            def _():
                stage_in(t + 2, 0)

            @pl.when(t + 1 < mine)
            def _():
                scatter(t + 1, 1)

    return push(xn, pos_a, pos_b)


def _sc_collect(ysorted, pos_flat, ch):
    n_pick = pos_flat.shape[0]
    per_worker = n_pick // SC_WORKERS
    n_chunks = per_worker // ch
    assert n_pick % SC_WORKERS == 0 and per_worker % ch == 0
    row_shape, dtype = ysorted.shape[1:], ysorted.dtype

    @functools.partial(
        pl.kernel, mesh=_sc_mesh(),
        out_type=jax.ShapeDtypeStruct((n_pick,) + row_shape, dtype),
        scratch_types=[pltpu.VMEM((ch,), jnp.int32), pltpu.VMEM((ch,), jnp.int32),
                       pltpu.VMEM((ch,) + row_shape, dtype), pltpu.VMEM((ch,) + row_shape, dtype),
                       pltpu.SemaphoreType.DMA, pltpu.SemaphoreType.DMA])
    def pull(ys_hbm, pos_hbm, out_hbm, idx0, idx1, rows0, rows1, sem0, sem1):
        base = _sc_worker() * per_worker
        bufs = ((idx0, rows0, sem0), (idx1, rows1, sem1))

        def offset(j):
            return pl.multiple_of(base + j * ch, SUBLANES)

        def fetch(j, b):
            idx, rows, sem = bufs[b]
            pltpu.sync_copy(pos_hbm.at[pl.ds(offset(j), ch)], idx)
            pltpu.async_copy(ys_hbm.at[idx], rows, sem)

        def flush(j, b):
            idx, rows, sem = bufs[b]
            pltpu.make_async_copy(ys_hbm.at[idx], rows, sem).wait()
            pltpu.sync_copy(rows, out_hbm.at[pl.ds(offset(j), ch)])

        fetch(0, 0)

        @pl.loop(0, n_chunks // 2)
        def _(p):
            j = 2 * p
            fetch(j + 1, 1)
            flush(j, 0)

            @pl.when(j + 2 < n_chunks)
            def _():
                fetch(j + 2, 0)

            flush(j + 1, 1)

        if n_chunks % 2:
            flush(n_chunks - 1, 0)

    return pull(ysorted, pos_flat)


def _moe_ffn_body(te_ref, nused_ref, x_ref, wg_hbm, wu_hbm, wd_hbm, y_ref,
                  wg_f32, wu_f32, wd_f32, wgb, wub, wdb, slot_ref, sems):
    i = pl.program_id(0)
    n_used = nused_ref[0]

    def fetch(expert, slot):
        return (pltpu.make_async_copy(wg_hbm.at[expert], wg_f32.at[slot], sems.at[slot, 0]),
                pltpu.make_async_copy(wu_hbm.at[expert], wu_f32.at[slot], sems.at[slot, 1]),
                pltpu.make_async_copy(wd_hbm.at[expert], wd_f32.at[slot], sems.at[slot, 2]))

    @pl.when(i >= n_used)
    def _unused_tile():
        y_ref[...] = jnp.zeros_like(y_ref)

    @pl.when(i < n_used)
    def _tile():
        expert = te_ref[i]

        @pl.when(i == 0)
        def _first_fetch():
            slot_ref[0] = 0
            for c in fetch(expert, 0):
                c.start()

        @pl.when((i == 0) | (expert != te_ref[jnp.maximum(i - 1, 0)]))
        def _new_expert():
            slot = slot_ref[0]
            nxt = lax.while_loop(lambda k: (k < n_used) & (te_ref[jnp.minimum(k, n_used - 1)] == expert),
                                 lambda k: k + 1, i + 1)

            @pl.when(nxt < n_used)
            def _prefetch():
                for c in fetch(te_ref[jnp.minimum(nxt, n_used - 1)], 1 - slot):
                    c.start()

            for c in fetch(expert, slot):
                c.wait()
            wgb[...] = wg_f32[slot].astype(BF16)
            wub[...] = wu_f32[slot].astype(BF16)
            wdb[...] = wd_f32[slot].astype(BF16)
            slot_ref[0] = 1 - slot

        x = _unpack_bf16_pairs(x_ref, MOE_TILE).astype(BF16)
        gate = _dot(x, wgb[...])
        hmid = (gate * jax.nn.sigmoid(gate)) * _dot(x, wub[...])
        y = _dot(hmid.astype(BF16), wdb[...])
        packed = _pack_bf16_pairs(y)
        for j in range(PACK_ROWS):
            y_ref[pl.ds(j, MOE_TILE, stride=PACK_ROWS), :] = packed[:, j * LANES:(j + 1) * LANES]


def _moe_ffn(tile_expert, n_used, xsorted, w_gate, w_up, w_down):
    n_tiles = tile_expert.shape[0]
    hbm = pl.BlockSpec(memory_space=pl.ANY)
    tile = lambda rows, imap: pl.BlockSpec((MOE_TILE * rows, LANES), imap)
    up_shape, down_shape = (D_MODEL, MOE_D_FF), (MOE_D_FF, D_MODEL)
    return pl.pallas_call(
        _moe_ffn_body,
        grid_spec=pltpu.PrefetchScalarGridSpec(
            num_scalar_prefetch=2,
            grid=(n_tiles,),
            in_specs=[tile(PACK_ROWS, lambda i, te, nu: (jnp.clip(i, 0, jnp.maximum(nu[0] - 1, 0)), 0)),
                      hbm, hbm, hbm],
            out_specs=tile(PACK_ROWS, lambda i, te, nu: (i, 0)),
            scratch_shapes=[pltpu.VMEM((2,) + up_shape, F32), pltpu.VMEM((2,) + up_shape, F32),
                            pltpu.VMEM((2,) + down_shape, F32),
                            pltpu.VMEM(up_shape, BF16), pltpu.VMEM(up_shape, BF16), pltpu.VMEM(down_shape, BF16),
                            pltpu.SMEM((1,), jnp.int32), pltpu.SemaphoreType.DMA((2, 3))]),
        out_shape=jax.ShapeDtypeStruct((n_tiles * MOE_TILE * PACK_ROWS, LANES), jnp.uint32),
        compiler_params=pltpu.CompilerParams(dimension_semantics=("arbitrary",), vmem_limit_bytes=VMEM_LIMIT),
        name="moe_ffn",
    )(tile_expert, n_used, xsorted, w_gate, w_up, w_down)


def _combine_body(x1_ref, rt_ref, ya_ref, yb_ref, nf_ref, *rest):
    out_ref = rest[-1]
    rt = rt_ref[...]
    x1 = x1_ref[...]
    tm = x1.shape[0]

    x2 = (x1 + rt[:, 2:3] * _unpack_bf16_pairs(ya_ref.at[0], tm)
          + rt[:, 3:4] * _unpack_bf16_pairs(yb_ref.at[0], tm))
    out_ref[...] = _rms(x2, nf_ref[...])


def _combine(x1, rt, y_picks, nf, tm, rows, x_block, y_block, out_rows, out_block, out_buf=None):
    row = lambda w: pl.BlockSpec((tm, w), lambda i: (i + x_block, 0))
    pick = lambda k: pl.BlockSpec((1, tm * PACK_ROWS, LANES), lambda i: (k, i + y_block, 0))
    in_specs = [row(D_MODEL), row(LANES), pick(0), pick(1), pl.BlockSpec((1, D_MODEL), lambda i: (0, 0))]
    args = [x1, rt, y_picks, y_picks, nf]
    aliases = {}
    if out_buf is not None:
        in_specs.append(pl.BlockSpec(memory_space=pl.ANY))
        aliases[len(args)] = 0
        args.append(out_buf)
    return pl.pallas_call(
        _combine_body,
        grid=(rows // tm,),
        in_specs=in_specs,
        out_specs=pl.BlockSpec((tm, D_MODEL), lambda i: (i + out_block, 0)),
        out_shape=jax.ShapeDtypeStruct((out_rows, D_MODEL), F32),
        input_output_aliases=aliases,
        compiler_params=pltpu.CompilerParams(dimension_semantics=("parallel",), vmem_limit_bytes=VMEM_LIMIT),
        name="moe_combine",
    )(*args)


def _s5_tables(a_re, a_im, log_dt, b_re, b_im, c_re, c_im):
    dt = jnp.exp(log_dt)[:, None]
    mag = jnp.exp(a_re * dt)
    ab_re = mag * jnp.cos(a_im * dt)
    ab_im = mag * jnp.sin(a_im * dt)
    den = a_re * a_re + a_im * a_im
    nr = ab_re - 1.0
    q_re = (nr * a_re + ab_im * a_im) / den
    q_im = (ab_im * a_re - nr * a_im) / den
    bb_re = q_re[..., None] * b_re - q_im[..., None] * b_im
    bb_im = q_re[..., None] * b_im + q_im[..., None] * b_re
    nblk = S5_GROUPS // 16
    kw, nw = 16 * S5_GROUP_CH, 16 * S5_STATE
    same_group = (jnp.arange(kw)[:, None] // S5_GROUP_CH) == (jnp.arange(nw)[None, :] // S5_STATE)

    def in_map(bb):
        rows = bb.reshape(nblk, 16, S5_STATE, S5_GROUP_CH).transpose(0, 1, 3, 2).reshape(nblk, kw, S5_STATE)
        return jnp.where(same_group, jnp.tile(rows, (1, 1, 16)), 0.0)

    def out_map(cc):
        cols = cc.reshape(nblk, 16, S5_GROUP_CH, S5_STATE).transpose(0, 3, 1, 2).reshape(nblk, S5_STATE, kw)
        return jnp.where(same_group.T, jnp.tile(cols, (1, 16, 1)), 0.0)

    wb = jnp.concatenate([in_map(bb_re), in_map(bb_im)], axis=-1).astype(BF16)
    return (wb, ab_re.reshape(1, S5_LANES), ab_im.reshape(1, S5_LANES),
            out_map(c_re).astype(BF16), out_map(-c_im).astype(BF16))


def kernel(x_prompt, x_sample, state_ssd_conv, state_ssd_ssm, state_s5_re, state_s5_im, meta_tokens, norm_mix, w_in, conv_w, conv_b, dt_bias, a_log, d_ssd, ssd_norm, s5_a_re, s5_a_im, s5_log_dt, s5_b_re, s5_b_im, s5_c_re, s5_c_im, s5_d, w_glu, b_glu, s5_norm, w_out, norm_ffn, router_coarse_w, router_coarse_b, router_fine_w, router_fine_b, w_gate, w_up, w_down, norm_final):
    bp, seq, _ = x_prompt.shape
    bs = x_sample.shape[0]
    n_prompt = bp * seq
    n_tok = n_prompt + bs
    row2 = lambda v: v.reshape(1, -1)
    pad_heads = lambda v: jnp.pad(v, (0, LANES - SSD_HEADS)).reshape(1, LANES)

    w = w_in[0]
    o1, o2, o3 = SSD_WIDTH, SSD_WIDTH + SSD_CONV_DIM, SSD_WIDTH + SSD_CONV_DIM + SSD_HEADS
    wz, wx, wu = w[:, :o1].astype(BF16), w[:, o1:o2].astype(BF16), w[:, o3:].astype(BF16)
    wdt = jnp.pad(w[:, o2:o3], ((0, 0), (0, LANES - SSD_HEADS))).astype(BF16)
    g_mix = row2(norm_mix[0])
    cw, cb = conv_w[0], row2(conv_b[0])
    dtb, alog = pad_heads(dt_bias[0]), pad_heads(a_log[0])
    dexp = row2(jnp.repeat(d_ssd[0], SSD_HEAD_DIM))
    snrm = row2(ssd_norm[0])
    eexp = (jnp.arange(LANES)[:, None] == (jnp.arange(SSD_WIDTH) // SSD_HEAD_DIM)[None, :]).astype(BF16)
    wb5, ab_re, ab_im, wcr, wci = _s5_tables(s5_a_re[0], s5_a_im[0], s5_log_dt[0], s5_b_re[0], s5_b_im[0],
                                             s5_c_re[0], s5_c_im[0])
    d5, wglu, bglu, nrm5 = row2(s5_d[0]), w_glu[0].astype(BF16), row2(b_glu[0]), row2(s5_norm[0])
    wo_a, wo_b = w_out[0][:SSD_WIDTH].astype(BF16), w_out[0][SSD_WIDTH:].astype(BF16)
    w_r = jnp.concatenate([router_coarse_w[0], router_fine_w[0].transpose(1, 0, 2).reshape(D_MODEL, MOE_EXPERTS)], axis=1)
    w_r = jnp.pad(w_r, ((0, 0), (0, LANES - w_r.shape[1])))
    wrh = w_r.astype(BF16)
    wrl = (w_r - wrh.astype(F32)).astype(BF16)
    b_r = jnp.concatenate([router_coarse_b[0], router_fine_b[0].reshape(-1)])
    b_r = jnp.pad(b_r, (0, LANES - b_r.shape[0])).reshape(1, LANES)

    zp, xbcp, dtp, up = _in_proj(x_prompt.reshape(n_prompt, D_MODEL), g_mix, wz, wx, wdt, wu, TOK_TILE, BF16, F32)
    xsm = jnp.concatenate([x_sample.reshape(bs, D_MODEL), meta_tokens], axis=0)
    zs, xbcs, dts, us = _in_proj(xsm, g_mix, wz, wx, wdt, wu, xsm.shape[0], F32, F32)

    front = SSD_CHUNK - N_META
    padf = lambda a: jnp.pad(a[bs:], ((front, 0), (0, 0)))[None]
    gw = SSD_HPG * SSD_HEAD_DIM
    ssd_consts = (cw, cb, dtb, alog, dexp, snrm, eexp)
    _, ctail_m, _, ht_m = _ssd_chunked(
        padf(xbcs).astype(BF16), padf(dts), jnp.zeros((1, SSD_CHUNK, SSD_WIDTH), F32),
        jnp.zeros((1, SUBLANES, SSD_CONV_DIM), F32), jnp.zeros((1, SSD_GROUPS, SSD_STATE, gw), F32),
        *ssd_consts, mask_rows=front)
    y_ssd_p, ctail_p, ssm_p, _ = _ssd_chunked(
        xbcp.reshape(bp, seq, SSD_CONV_DIM), dtp.reshape(bp, seq, LANES), zp.reshape(bp, seq, SSD_WIDTH),
        ctail_m, ht_m, *ssd_consts, mask_rows=0)

    abr8, abi8 = jnp.broadcast_to(ab_re, (bp, S5_LANES)), jnp.broadcast_to(ab_im, (bp, S5_LANES))
    um8 = jnp.repeat(us[bs:], bp, axis=0).astype(BF16)
    y_s5_p, s5re_p, s5im_p = _s5_seq(up.reshape(bp, seq, S5_WIDTH), um8, wb5, abr8, abi8,
                                     wcr, wci, d5, wglu, bglu, nrm5)

    cst = state_ssd_conv[0]
    xt_s, dt_s, dec_s, bc, xs_s = _ssd_step_prep(xbcs[:bs], cst[:, 0], cst[:, 1], cst[:, 2], dts[:bs],
                                                 cw, cb, dtb, alog)
    ssm_s, y_core = _ssd_step(dt_s[:, :SSD_HEADS].reshape(-1), dec_s[:, :SSD_HEADS].reshape(-1),
                              state_ssd_ssm[0], xt_s, bc)
    y_ssd_s, y_s5_s, s5re_s, s5im_s = _sample_post(
        y_core, xs_s, zs[:bs], dexp, snrm, us[:bs], state_s5_re[0].reshape(bs, S5_LANES),
        state_s5_im[0].reshape(bs, S5_LANES), wb5, ab_re, ab_im, wcr, wci, d5, wglu, bglu, nrm5)

    route_consts = (wo_a, wo_b, row2(norm_ffn[0]), wrh, wrl, b_r)
    n_tiles = -(-2 * n_tok // MOE_TILE) + MOE_EXPERTS
    x1, xn, rt, pos, meta = _mix_route(
        (x_prompt.reshape(n_prompt, D_MODEL), y_ssd_p.reshape(n_prompt, SSD_WIDTH), y_s5_p.reshape(n_prompt, S5_WIDTH)),
        (x_sample.reshape(bs, D_MODEL), y_ssd_s, y_s5_s), route_consts, TOK_TILE, n_tiles)

    pos_a, pos_b = pos[0], pos[1]
    tile_expert, n_used = meta[0, :n_tiles], meta[1, :1]
    xsorted = _sc_dispatch(xn, pos_a, pos_b, n_tiles * MOE_TILE)
    ysorted = _moe_ffn(tile_expert, n_used, xsorted.reshape(-1, LANES), w_gate[0], w_up[0], w_down[0])
    nfin = row2(norm_final)

    half = n_prompt // 2

    def collect(lo, hi, ch):
        picks = jnp.concatenate([pos_a[lo:hi], pos_b[lo:hi]])
        packed_rows = ysorted.reshape(-1, PACK_ROWS, LANES)
        return _sc_collect(packed_rows, picks, ch).reshape(2, (hi - lo) * PACK_ROWS, LANES)

    picks_1 = collect(0, half, SC_COLLECT_ROWS[0])
    picks_2 = collect(half, n_tok, SC_COLLECT_ROWS[1])
    blocks = half // MOE_TILE
    y_p = _combine(x1, rt, picks_1, nfin, MOE_TILE, half, 0, 0, n_prompt, 0)
    y_p = _combine(x1, rt, picks_2, nfin, MOE_TILE, half, blocks, 0, n_prompt, blocks, out_buf=y_p)
    y_s = _combine(x1, rt, picks_2, nfin, bs, bs, n_prompt // bs, half // bs, bs, 0)

    s5_state = lambda a, b: a.reshape(1, b, S5_GROUPS, S5_STATE)
    new_conv_s = jnp.stack([cst[:, 1], cst[:, 2], xbcs[:bs]], axis=1)[None]
    return (y_p.reshape(bp, seq, D_MODEL), y_s.reshape(bs, 1, D_MODEL),
            ctail_p[:, SUBLANES - (SSD_CONV - 1):][None], ssm_p[None], s5_state(s5re_p, bp), s5_state(s5im_p, bp),
            new_conv_s, ssm_s[None], s5_state(s5re_s, bs), s5_state(s5im_s, bs))
```

```python
import functools

import jax
import jax.numpy as jnp
from jax import lax
from jax.experimental import pallas as pl
from jax.experimental.pallas import tpu as pltpu
from jax.experimental.pallas import tpu_sc as plsc

F32, BF16 = jnp.float32, jnp.bfloat16

D_MODEL = 1024
N_META = 16
SSD_WIDTH = 1024
SSD_HEAD_DIM = 64
SSD_HEADS = 16
SSD_GROUPS = 2
SSD_HPG = SSD_HEADS // SSD_GROUPS
SSD_STATE = 128
SSD_CONV = 4
SSD_CHUNK = 128
SSD_CONV_DIM = SSD_WIDTH + 2 * SSD_GROUPS * SSD_STATE
S5_WIDTH = 1024
S5_GROUP_CH = 16
S5_GROUPS = 64
S5_STATE = 64
S5_LANES = S5_GROUPS * S5_STATE
MOE_GROUPS = 4
MOE_EPG = 8
MOE_EXPERTS = MOE_GROUPS * MOE_EPG
MOE_D_FF = 512
EPS = 1e-6

LANES = 128
SUBLANES = 8
VMEM_LIMIT = 56 * 1024 * 1024

SSD_CHUNKS_PER_STEP = 2
S5_TIME_TILE = 64
S5_SCAN_LANES = 512
MOE_TILE = 256
MOE_TILES_PER_STEP = 2
SLAB_ROWS = D_MODEL // LANES
PACK_ROWS = SLAB_ROWS // 2
SC_CORES = 2
SC_SUBCORES = 16
SC_WORKERS = SC_CORES * SC_SUBCORES
SC_DISPATCH_ROWS = 32
SC_COLLECT_ROWS = (32, 40)
TOK_TILE = 512


def _dot(a, b):
    return jnp.dot(a, b, preferred_element_type=F32)


def _rms(x, g):
    return x * lax.rsqrt(jnp.mean(x * x, axis=-1, keepdims=True) + EPS) * g


def _softplus(x):
    return jnp.maximum(x, 0.0) + jnp.log1p(jnp.exp(-jnp.abs(x)))


def _split3(x):
    hi = x.astype(BF16)
    r = x - hi.astype(F32)
    mid = r.astype(BF16)
    lo = (r - mid.astype(F32)).astype(BF16)
    return hi, mid, lo


def _dot3(x, w):
    hi, mid, lo = _split3(x)
    return _dot(hi, w) + _dot(mid, w) + _dot(lo, w)


def _dot3_left(w, x):
    hi, mid, lo = _split3(x)
    return _dot(w, hi) + _dot(w, mid) + _dot(w, lo)


def _pack_bf16_pairs(x):
    bits = pltpu.bitcast(x.astype(BF16).astype(F32), jnp.uint32)
    half = x.shape[1] // 2
    return (bits[:, :half] & jnp.uint32(0xFFFF0000)) | (bits[:, half:] >> jnp.uint32(16))


def _unpack_bf16_pairs(ref, rows):
    words = [ref[pl.ds(j, rows, stride=PACK_ROWS), :] for j in range(PACK_ROWS)]
    high = [pltpu.bitcast(w & jnp.uint32(0xFFFF0000), F32) for w in words]
    low = [pltpu.bitcast(w << jnp.uint32(16), F32) for w in words]
    return jnp.concatenate(high + low, axis=-1)


def _full_spec(a):
    nd = a.ndim
    return pl.BlockSpec(a.shape, lambda *_: (0,) * nd)


def _resident_spec(a):
    nd = a.ndim
    return pl.BlockSpec(a.shape, lambda *_: (0,) * nd, pipeline_mode=pl.Buffered(1))


def _in_proj_body(x_ref, g_ref, wz_ref, wx_ref, wdt_ref, wu_ref, z_ref, xbc_ref, dt_ref, u_ref):
    xb = _rms(x_ref[...], g_ref[...]).astype(BF16)
    z_ref[...] = _dot(xb, wz_ref[...]).astype(z_ref.dtype)
    xbc_ref[...] = _dot(xb, wx_ref[...]).astype(xbc_ref.dtype)
    dt_ref[...] = _dot(xb, wdt_ref[...])
    u_ref[...] = _dot(xb, wu_ref[...]).astype(u_ref.dtype)


def _in_proj(x2d, g, wz, wx, wdt, wu, tm, act_dtype, u_dtype):
    rows = x2d.shape[0]
    row = lambda w: pl.BlockSpec((tm, w), lambda i: (i, 0))
    return pl.pallas_call(
        _in_proj_body,
        grid=(rows // tm,),
        in_specs=[row(D_MODEL), _full_spec(g), _full_spec(wz), _full_spec(wx), _full_spec(wdt), _full_spec(wu)],
        out_specs=[row(SSD_WIDTH), row(SSD_CONV_DIM), row(LANES), row(S5_WIDTH)],
        out_shape=[jax.ShapeDtypeStruct((rows, SSD_WIDTH), act_dtype),
                   jax.ShapeDtypeStruct((rows, SSD_CONV_DIM), act_dtype),
                   jax.ShapeDtypeStruct((rows, LANES), F32),
                   jax.ShapeDtypeStruct((rows, S5_WIDTH), u_dtype)],
        compiler_params=pltpu.CompilerParams(dimension_semantics=("parallel",), vmem_limit_bytes=VMEM_LIMIT),
        name="in_proj",
    )(x2d, g, wz, wx, wdt, wu)


def _ssd_body(mask_rows, per_step, *refs):
    for k in range(per_step):
        _ssd_chunk(mask_rows, per_step, k, *refs)


def _ssd_chunk(mask_rows, per_step, k, xbc_ref, dt_ref, z_ref, cinit_ref, hinit_ref, cw_ref, cb_ref, dtb_ref,
               alog_ref, dexp_ref, nrm_ref, eexp_ref, y_ref, ctail_ref, st_ref, hto_ref, xwin, hT):
    L = SSD_CHUNK
    c = pl.program_id(1) * per_step + k
    n_chunks = pl.num_programs(1) * per_step
    window = pl.ds(k * L, L)
    xbc_ref, dt_ref, z_ref, y_ref = (r.at[:, window, :] for r in (xbc_ref, dt_ref, z_ref, y_ref))

    @pl.when(c == 0)
    def _init():
        xwin[...] = cinit_ref[0]
        hT[...] = hinit_ref[0]

    x_b = xbc_ref[0]
    x_f = x_b.astype(F32)
    taps = SSD_CONV - 1
    m_i = lax.broadcasted_iota(jnp.int32, (taps * L, L), 0)
    r_i = lax.broadcasted_iota(jnp.int32, (taps * L, L), 1)
    shift = (r_i + (taps - m_i // L) == m_i % L).astype(BF16)
    shifted = _dot(shift, x_b)
    acc = cb_ref[...] + x_f * cw_ref[taps:taps + 1, :]
    for k in range(taps):
        acc = acc + shifted[k * L:(k + 1) * L, :] * cw_ref[k:k + 1, :]
    joint = jnp.concatenate([xwin[...], x_f[0:SUBLANES, :]], axis=0)
    row8 = lax.broadcasted_iota(jnp.int32, (SUBLANES, 1), 0)
    head = acc[0:SUBLANES, :]
    for k in range(taps):
        d = taps - k
        head = head + jnp.where(row8 < d, joint[SUBLANES - d:2 * SUBLANES - d, :], 0.0) * cw_ref[k:k + 1, :]
    acc = jnp.concatenate([head, acc[SUBLANES:, :]], axis=0)
    tail = x_f[L - SUBLANES:, :]
    xwin[...] = tail
    ctail_ref[0] = tail

    xact = acc * jax.nn.sigmoid(acc)
    dt = _softplus(dt_ref[0] + dtb_ref[...])
    if mask_rows:
        valid = lax.broadcasted_iota(jnp.int32, (L, 1), 0) >= mask_rows
        xact = jnp.where(valid, xact, 0.0)
        dt = jnp.where(valid, dt, 0.0)

    a_neg = -jnp.exp(alog_ref[...])
    dA = dt * a_neg
    row_i = lax.broadcasted_iota(jnp.int32, (L, L), 0)
    col_i = lax.broadcasted_iota(jnp.int32, (L, L), 1)
    causal = row_i >= col_i
    tril = causal.astype(BF16)
    cs = _dot3_left(tril, dA)
    csT = cs.T
    dtT = dt.T
    ecs = jnp.exp(cs)
    wdec = jnp.exp(cs[L - 1:L, :] - cs) * dt
    eexp = eexp_ref[...]
    ecs_e = _dot3(ecs, eexp)
    wdec_e = _dot3(wdec, eexp)
    lane = lax.broadcasted_iota(jnp.int32, (L, LANES), 1)
    first_half = lane < SSD_HEAD_DIM

    gw = SSD_HPG * SSD_HEAD_DIM
    y_groups = []
    for g in range(SSD_GROUPS):
        b_g = xact[:, SSD_WIDTH + g * SSD_STATE: SSD_WIDTH + (g + 1) * SSD_STATE]
        c_g = xact[:, SSD_WIDTH + (SSD_GROUPS + g) * SSD_STATE: SSD_WIDTH + (SSD_GROUPS + g + 1) * SSD_STATE]
        b_b = b_g.astype(BF16)
        c_b = c_g.astype(BF16)
        cb = lax.dot_general(c_b, b_b, (((1,), (1,)), ((), ())), preferred_element_type=F32)
        xs_g = xact[:, g * gw:(g + 1) * gw]
        h_prev = hT[g]
        y_off = _dot(c_b, h_prev.astype(BF16)) * ecs_e[:, g * gw:(g + 1) * gw]
        xdec = (xs_g * wdec_e[:, g * gw:(g + 1) * gw]).astype(BF16)
        hT[g] = h_prev * ecs_e[L - 1:L, g * gw:(g + 1) * gw] + _dot(b_g.T.astype(BF16), xdec)
        pieces = []
        for j in range(SSD_HPG // 2):
            xs_pair = xs_g[:, j * LANES:(j + 1) * LANES]
            halves = (jnp.where(first_half, xs_pair, 0.0).astype(BF16),
                      jnp.where(first_half, 0.0, xs_pair).astype(BF16))
            yd = None
            for t in range(2):
                h = g * SSD_HPG + 2 * j + t
                seg = cs[:, h:h + 1] - csT[h:h + 1, :]
                lmat = jnp.exp(jnp.where(causal, seg, -jnp.inf))
                m = (cb * lmat * dtT[h:h + 1, :]).astype(BF16)
                part = _dot(m, halves[t])
                yd = part if yd is None else yd + part
            pieces.append(yd)
        y_groups.append(jnp.concatenate(pieces, axis=-1) + y_off + dexp_ref[:, g * gw:(g + 1) * gw] * xs_g)
    y = jnp.concatenate(y_groups, axis=-1)
    z = z_ref[0].astype(F32)
    y_ref[0] = _rms(y * (z * jax.nn.sigmoid(z)), nrm_ref[...]).astype(y_ref.dtype)

    @pl.when(c == n_chunks - 1)
    def _emit():
        hto_ref[0] = hT[...]
        for g in range(SSD_GROUPS):
            t = hT[g].T
            for k in range(SSD_HPG):
                st_ref[0, g * SSD_HPG + k] = t[k * SSD_HEAD_DIM:(k + 1) * SSD_HEAD_DIM, :]


def _ssd_chunked(xbc, dt, z, cinit, hinit, cw, cb, dtb, alog, dexp, nrm, eexp, mask_rows):
    bsz, seq, _ = xbc.shape
    nc = seq // SSD_CHUNK
    per_step = SSD_CHUNKS_PER_STEP if nc % SSD_CHUNKS_PER_STEP == 0 else 1
    gw = SSD_HPG * SSD_HEAD_DIM
    blk = lambda w: pl.BlockSpec((1, per_step * SSD_CHUNK, w), lambda b, c: (b, c, 0))
    return pl.pallas_call(
        functools.partial(_ssd_body, mask_rows, per_step),
        grid=(bsz, nc // per_step),
        in_specs=[blk(SSD_CONV_DIM), blk(LANES), blk(SSD_WIDTH),
                  pl.BlockSpec((1, SUBLANES, SSD_CONV_DIM), lambda b, c: (0, 0, 0)),
                  pl.BlockSpec((1, SSD_GROUPS, SSD_STATE, gw), lambda b, c: (0, 0, 0, 0)),
                  _full_spec(cw), _full_spec(cb), _full_spec(dtb), _full_spec(alog),
                  _full_spec(dexp), _full_spec(nrm), _full_spec(eexp)],
        out_specs=[blk(SSD_WIDTH),
                   pl.BlockSpec((1, SUBLANES, SSD_CONV_DIM), lambda b, c: (b, 0, 0)),
                   pl.BlockSpec((1, SSD_HEADS, SSD_HEAD_DIM, SSD_STATE), lambda b, c: (b, 0, 0, 0)),
                   pl.BlockSpec((1, SSD_GROUPS, SSD_STATE, gw), lambda b, c: (b, 0, 0, 0))],
        out_shape=[jax.ShapeDtypeStruct((bsz, seq, SSD_WIDTH), BF16),
                   jax.ShapeDtypeStruct((bsz, SUBLANES, SSD_CONV_DIM), F32),
                   jax.ShapeDtypeStruct((bsz, SSD_HEADS, SSD_HEAD_DIM, SSD_STATE), F32),
                   jax.ShapeDtypeStruct((bsz, SSD_GROUPS, SSD_STATE, gw), F32)],
        scratch_shapes=[pltpu.VMEM((SUBLANES, SSD_CONV_DIM), F32),
                        pltpu.VMEM((SSD_GROUPS, SSD_STATE, gw), F32)],
        compiler_params=pltpu.CompilerParams(dimension_semantics=("parallel", "arbitrary"),
                                             vmem_limit_bytes=VMEM_LIMIT),
        name="ssd_chunked",
    )(xbc, dt, z, cinit, hinit, cw, cb, dtb, alog, dexp, nrm, eexp)


def _ssd_step_prep_body(xbc_ref, c0_ref, c1_ref, c2_ref, dt_ref, cw_ref, cb_ref, dtb_ref, alog_ref,
                        xt_ref, dt_out_ref, dec_ref, bc_ref, xs_ref):
    acc = cb_ref[...]
    for k, r in enumerate((c0_ref, c1_ref, c2_ref, xbc_ref)):
        acc = acc + r[...] * cw_ref[k:k + 1, :]
    xact = acc * jax.nn.sigmoid(acc)
    xs = xact[:, :SSD_WIDTH]
    dt = _softplus(dt_ref[...] + dtb_ref[...])
    dt_out_ref[...] = dt
    dec_ref[...] = jnp.exp(dt * -jnp.exp(alog_ref[...]))
    bc_ref[...] = xact[:, SSD_WIDTH:]
    xs_ref[...] = xs
    xt_ref[...] = xs.T.astype(xt_ref.dtype)


def _ssd_step_prep(xbc, c0, c1, c2, dt, cw, cb, dtb, alog):
    n = xbc.shape[0]
    args = (xbc, c0, c1, c2, dt, cw, cb, dtb, alog)
    spec = lambda r, w: pl.BlockSpec((r, w), lambda: (0, 0))
    return pl.pallas_call(
        _ssd_step_prep_body,
        in_specs=[_full_spec(a) for a in args],
        out_specs=[spec(SSD_WIDTH, n), spec(n, LANES), spec(n, LANES), spec(n, 2 * SSD_GROUPS * SSD_STATE),
                   spec(n, SSD_WIDTH)],
        out_shape=[jax.ShapeDtypeStruct((SSD_WIDTH, n), BF16), jax.ShapeDtypeStruct((n, LANES), F32),
                   jax.ShapeDtypeStruct((n, LANES), F32),
                   jax.ShapeDtypeStruct((n, 2 * SSD_GROUPS * SSD_STATE), F32),
                   jax.ShapeDtypeStruct((n, SSD_WIDTH), F32)],
        compiler_params=pltpu.CompilerParams(vmem_limit_bytes=VMEM_LIMIT),
        name="ssd_step_prep",
    )(*args)


def _ssd_step_body(dt_ref, dec_ref, st_ref, xt_ref, bc_ref, so_ref, y_ref):
    n = xt_ref.shape[1]
    gw = SSD_HPG * SSD_HEAD_DIM
    blk = pl.program_id(0)
    seq_id = lax.broadcasted_iota(jnp.int32, (n, SSD_STATE), 0)
    sub_id = lax.broadcasted_iota(jnp.int32, (SUBLANES, gw), 0)
    base = pl.multiple_of(blk * SUBLANES, SUBLANES)
    y_acc = [jnp.zeros((SUBLANES, gw), F32) for _ in range(SSD_GROUPS)]
    for i in range(SUBLANES):
        s = blk * SUBLANES + i
        for g in range(SSD_GROUPS):
            b_all = bc_ref[:, g * SSD_STATE:(g + 1) * SSD_STATE]
            rhs = jnp.where(seq_id == s, b_all, 0.0).astype(BF16)
            outer = _dot(xt_ref[g * gw:(g + 1) * gw, :], rhs)
            news = []
            for k in range(SSD_HPG):
                h = g * SSD_HPG + k
                new = (dec_ref[s * SSD_HEADS + h] * st_ref[i, h]
                       + dt_ref[s * SSD_HEADS + h] * outer[k * SSD_HEAD_DIM:(k + 1) * SSD_HEAD_DIM, :])
                so_ref[i, h] = new
                news.append(new)
            new_g = jnp.concatenate(news, axis=0).astype(BF16)
            c_lo = (SSD_GROUPS + g) * SSD_STATE
            c_blk = bc_ref[pl.ds(base, SUBLANES), c_lo:c_lo + SSD_STATE].astype(BF16)
            r = lax.dot_general(c_blk, new_g, (((1,), (1,)), ((), ())), preferred_element_type=F32)
            y_acc[g] = y_acc[g] + jnp.where(sub_id == i, r, 0.0)
    y_ref[...] = jnp.concatenate(y_acc, axis=-1)


def _ssd_step(dt_flat, dec_flat, state, xt, bc):
    n = state.shape[0]
    st_spec = pl.BlockSpec((SUBLANES, SSD_HEADS, SSD_HEAD_DIM, SSD_STATE), lambda i, *_: (i, 0, 0, 0))
    return pl.pallas_call(
        _ssd_step_body,
        grid_spec=pltpu.PrefetchScalarGridSpec(
            num_scalar_prefetch=2,
            grid=(n // SUBLANES,),
            in_specs=[st_spec, pl.BlockSpec(xt.shape, lambda i, *_: (0, 0)),
                      pl.BlockSpec(bc.shape, lambda i, *_: (0, 0))],
            out_specs=[st_spec, pl.BlockSpec((SUBLANES, SSD_WIDTH), lambda i, *_: (i, 0))]),
        out_shape=[jax.ShapeDtypeStruct(state.shape, F32), jax.ShapeDtypeStruct((n, SSD_WIDTH), F32)],
        compiler_params=pltpu.CompilerParams(dimension_semantics=("parallel",), vmem_limit_bytes=VMEM_LIMIT),
        name="ssd_step",
    )(dt_flat, dec_flat, state, xt, bc)


def _s5_project_in(u_b16, wb_ref, store):
    kw = 16 * S5_GROUP_CH
    nw = 16 * S5_STATE
    for j in range(S5_WIDTH // kw):
        r = _dot(u_b16[:, j * kw:(j + 1) * kw], wb_ref[j])
        store(j, r[:, :nw], r[:, nw:])


def _s5_tail(hre_of, him_of, u_f32, wcr_ref, wci_ref, d_ref, wglu_ref, bglu_ref, nrm_ref):
    cols = []
    for j in range(wcr_ref.shape[0]):
        cols.append(_dot(hre_of(j).astype(BF16), wcr_ref[j]) + _dot(him_of(j).astype(BF16), wci_ref[j]))
    return _s5_finish(cols, u_f32, d_ref, wglu_ref, bglu_ref, nrm_ref)


def _s5_finish(cols, u_f32, d_ref, wglu_ref, bglu_ref, nrm_ref):
    y = jnp.concatenate(cols, axis=-1) + d_ref[...] * u_f32
    y = jax.nn.gelu(y)
    y = y * jax.nn.sigmoid(_dot(y.astype(BF16), wglu_ref[...]) + bglu_ref[...])
    return _rms(y, nrm_ref[...])


def _s5_seq_body(u_hbm, um_ref, wb_ref, abr_ref, abi_ref, wcr_ref, wci_ref, d_ref, wglu_ref, bglu_ref, nrm_ref,
                 y_hbm, sre_ref, sim_ref, ubuf, ybuf, bu, h, in_sems, out_sems):
    j = pl.program_id(0)
    last = pl.num_programs(0) - 1
    lc, bsz = ubuf.shape[1], ubuf.shape[2]
    rows = lc * bsz
    nw = 16 * S5_STATE

    def in_copy(step, b):
        return pltpu.make_async_copy(u_hbm.at[b, pl.ds(step * lc, lc), :], ubuf.at[step % 2, :, b, :],
                                     in_sems.at[step % 2, b])

    def out_copy(step, b):
        return pltpu.make_async_copy(ybuf.at[step % 2, :, b, :], y_hbm.at[b, pl.ds(step * lc, lc), :],
                                     out_sems.at[step % 2, b])

    def project_in(u_b16, nrows):
        def store(jj, re, im):
            bu[0:nrows, jj * nw:(jj + 1) * nw] = re
            bu[0:nrows, S5_LANES + jj * nw:S5_LANES + (jj + 1) * nw] = im
        _s5_project_in(u_b16, wb_ref, store)

    def scan(nsteps):
        for k in range(S5_LANES // S5_SCAN_LANES):
            sl_r = pl.ds(k * S5_SCAN_LANES, S5_SCAN_LANES)
            sl_i = pl.ds(S5_LANES + k * S5_SCAN_LANES, S5_SCAN_LANES)
            ar = abr_ref[:, sl_r]
            ai = abi_ref[:, sl_r]

            def step(l, carry):
                hr, hi = carry
                slab = pl.ds(pl.multiple_of(l * bsz, bsz), bsz)
                nr = ar * hr - ai * hi + bu[slab, sl_r]
                ni = ar * hi + ai * hr + bu[slab, sl_i]
                bu[slab, sl_r] = nr
                bu[slab, sl_i] = ni
                return nr, ni

            hr, hi = lax.fori_loop(0, nsteps, step, (h[:, sl_r], h[:, sl_i]))
            h[:, sl_r] = hr
            h[:, sl_i] = hi

    @pl.when(j == 0)
    def _first():
        for b in range(bsz):
            in_copy(0, b).start()
        h[...] = jnp.zeros_like(h)
        project_in(um_ref[...], N_META * bsz)
        scan(N_META)

    @pl.when(j < last)
    def _prefetch():
        for b in range(bsz):
            in_copy(j + 1, b).start()

    for b in range(bsz):
        in_copy(j, b).wait()
    u2 = ubuf[j % 2].reshape(rows, S5_WIDTH)
    u_b16 = u2.astype(BF16)
    kw = 16 * S5_GROUP_CH

    def project_block(jj):
        r = _dot(u_b16[:, jj * kw:(jj + 1) * kw], wb_ref[jj])
        bu[0:rows, jj * nw:(jj + 1) * nw] = r[:, :nw]
        bu[0:rows, S5_LANES + jj * nw:S5_LANES + (jj + 1) * nw] = r[:, nw:]

    def scan_block(jj):
        for k in range(nw // S5_SCAN_LANES):
            lo = jj * nw + k * S5_SCAN_LANES
            sl_r = slice(lo, lo + S5_SCAN_LANES)
            sl_i = slice(S5_LANES + lo, S5_LANES + lo + S5_SCAN_LANES)
            ar, ai = abr_ref[:, sl_r], abi_ref[:, sl_r]
            hr, hi = h[:, sl_r], h[:, sl_i]
            for l in range(lc):
                slab = slice(l * bsz, (l + 1) * bsz)
                hr, hi = (ar * hr - ai * hi + bu[slab, sl_r], ar * hi + ai * hr + bu[slab, sl_i])
                bu[slab, sl_r] = hr
                bu[slab, sl_i] = hi
            h[:, sl_r] = hr
            h[:, sl_i] = hi

    def readout_block(jj):
        return (_dot(bu[:, jj * nw:(jj + 1) * nw].astype(BF16), wcr_ref[jj])
                + _dot(bu[:, S5_LANES + jj * nw:S5_LANES + (jj + 1) * nw].astype(BF16), wci_ref[jj]))

    n_blocks = S5_WIDTH // kw
    project_block(0)
    cols = []
    for jj in range(n_blocks):
        if jj + 1 < n_blocks:
            project_block(jj + 1)
        scan_block(jj)
        cols.append(readout_block(jj))
    y = _s5_finish(cols, u2, d_ref, wglu_ref, bglu_ref, nrm_ref)
    ybuf[j % 2] = y.reshape(lc, bsz, S5_WIDTH)
    for b in range(bsz):
        out_copy(j, b).start()

    @pl.when(j > 0)
    def _wait_previous_out():
        for b in range(bsz):
            out_copy(j - 1, b).wait()

    @pl.when(j == last)
    def _emit():
        for b in range(bsz):
            out_copy(j, b).wait()
        sre_ref[...] = h[:, 0:S5_LANES]
        sim_ref[...] = h[:, S5_LANES:]


def _s5_seq(u, um, wb, abr, abi, wcr, wci, d, wglu, bglu, nrm):
    bsz, seq, _ = u.shape
    lc = S5_TIME_TILE
    consts = (um, wb, abr, abi, wcr, wci, d, wglu, bglu, nrm)
    st = pl.BlockSpec((bsz, S5_LANES), lambda j: (0, 0))
    return pl.pallas_call(
        _s5_seq_body,
        grid=(seq // lc,),
        in_specs=[pl.BlockSpec(memory_space=pl.ANY)] + [_resident_spec(a) for a in consts],
        out_specs=[pl.BlockSpec(memory_space=pl.ANY), st, st],
        out_shape=[jax.ShapeDtypeStruct((bsz, seq, S5_WIDTH), F32),
                   jax.ShapeDtypeStruct((bsz, S5_LANES), F32), jax.ShapeDtypeStruct((bsz, S5_LANES), F32)],
        scratch_shapes=[pltpu.VMEM((2, lc, bsz, S5_WIDTH), F32), pltpu.VMEM((2, lc, bsz, S5_WIDTH), F32),
                        pltpu.VMEM((lc * bsz, 2 * S5_LANES), F32), pltpu.VMEM((bsz, 2 * S5_LANES), F32),
                        pltpu.SemaphoreType.DMA((2, bsz)), pltpu.SemaphoreType.DMA((2, bsz))],
        compiler_params=pltpu.CompilerParams(dimension_semantics=("arbitrary",), vmem_limit_bytes=VMEM_LIMIT),
        name="s5_seq",
    )(u, *consts)


def _sample_post_body(yc_ref, xs_ref, z_ref, dexp_ref, snrm_ref, u_ref, hr_ref, hi_ref, wb_ref, abr_ref, abi_ref,
                      wcr_ref, wci_ref, d_ref, wglu_ref, bglu_ref, nrm_ref,
                      yssd_ref, ys5_ref, nre_ref, nim_ref):
    z = z_ref[...]
    y = yc_ref[...] + dexp_ref[...] * xs_ref[...]
    yssd_ref[...] = _rms(y * (z * jax.nn.sigmoid(z)), snrm_ref[...]).astype(yssd_ref.dtype)

    u = u_ref[...]
    nw = 16 * S5_STATE
    ar, ai = abr_ref[...], abi_ref[...]

    def store(jj, re, im):
        sl = slice(jj * nw, (jj + 1) * nw)
        h0r, h0i = hr_ref[:, sl], hi_ref[:, sl]
        nre_ref[:, sl] = ar[:, sl] * h0r - ai[:, sl] * h0i + re
        nim_ref[:, sl] = ar[:, sl] * h0i + ai[:, sl] * h0r + im

    _s5_project_in(u.astype(BF16), wb_ref, store)
    slab = lambda ref: (lambda jj: ref[:, jj * nw:(jj + 1) * nw])
    y5 = _s5_tail(slab(nre_ref), slab(nim_ref), u, wcr_ref, wci_ref, d_ref, wglu_ref, bglu_ref, nrm_ref)
    ys5_ref[...] = y5.astype(ys5_ref.dtype)


def _sample_post(yc, xs, z, dexp, snrm, u, h0r, h0i, wb, abr1, abi1, wcr, wci, d, wglu, bglu, nrm):
    n = yc.shape[0]
    args = (yc, xs, z, dexp, snrm, u, h0r, h0i, wb, abr1, abi1, wcr, wci, d, wglu, bglu, nrm)
    spec = lambda w: pl.BlockSpec((n, w), lambda: (0, 0))
    return pl.pallas_call(
        _sample_post_body,
        in_specs=[_full_spec(a) for a in args],
        out_specs=[spec(SSD_WIDTH), spec(S5_WIDTH), spec(S5_LANES), spec(S5_LANES)],
        out_shape=[jax.ShapeDtypeStruct((n, SSD_WIDTH), BF16), jax.ShapeDtypeStruct((n, S5_WIDTH), BF16),
                   jax.ShapeDtypeStruct((n, S5_LANES), F32), jax.ShapeDtypeStruct((n, S5_LANES), F32)],
        compiler_params=pltpu.CompilerParams(vmem_limit_bytes=VMEM_LIMIT),
        name="sample_post",
    )(*args)


def _mix_route_body(n_blocks, n_sorted, xp_ref, ysp_ref, y5p_ref, xs_ref, yss_ref, y5s_ref, *refs):
    consts = refs[:6]
    x1_ref, xn_hbm, rt_ref, pos_ref, meta_ref, carry, fields, xbuf, sems = refs[6:]
    i = pl.program_id(0)
    tm, n_sample = xp_ref.shape[0], xs_ref.shape[0]
    col0 = pl.multiple_of(i * tm, LANES)

    def xn_copy(step, rows, j):
        return pltpu.make_async_copy(xbuf.at[step % 2, pl.ds(0, rows), pl.ds(j * LANES, LANES)],
                                     xn_hbm.at[pl.ds(step * tm, rows), j, :], sems.at[step % 2, j])

    @pl.when(i == 0)
    def _init():
        carry[...] = jnp.zeros_like(carry)

    @pl.when(i < n_blocks)
    def _prompt_rows():
        _mix_route_compute(xp_ref, ysp_ref, y5p_ref, *consts, x1_ref, rt_ref, carry, xbuf.at[i % 2], fields, col0)
        for j in range(PACK_ROWS):
            xn_copy(i, tm, j).start()

    @pl.when(i == n_blocks)
    def _sample_rows():
        _mix_route_compute(xs_ref, yss_ref, y5s_ref, *consts, x1_ref, rt_ref, carry, xbuf.at[i % 2], fields, col0)
        for j in range(PACK_ROWS):
            xn_copy(i, n_sample, j).start()
        _route_layout(carry, fields, pos_ref, meta_ref, n_sorted)
        for j in range(PACK_ROWS):
            xn_copy(i, n_sample, j).wait()

    @pl.when(i > 0)
    def _wait_previous_rows():
        for j in range(PACK_ROWS):
            xn_copy(i - 1, tm, j).wait()


def _route_layout(carry, fields, pos_ref, meta_ref, n_sorted):
    counts = carry[...]
    tiles_per = jnp.floor((counts + (MOE_TILE - 1)) * (1.0 / MOE_TILE))
    upto = lax.broadcasted_iota(jnp.int32, (LANES, LANES), 0) <= lax.broadcasted_iota(jnp.int32, (LANES, LANES), 1)
    tile_end = _dot(tiles_per.astype(BF16), upto.astype(BF16))
    pstart = (tile_end - tiles_per) * MOE_TILE
    n_used = tile_end[:, MOE_EXPERTS - 1:MOE_EXPERTS]

    f = fields[...]
    first_row = jnp.zeros_like(f)
    tile_id = jnp.minimum(lax.broadcasted_iota(jnp.int32, meta_ref.shape, 1).astype(F32), n_used - 1.0)
    tile_expert = jnp.zeros(meta_ref.shape, F32)
    for e in range(MOE_EXPERTS):
        first_row = first_row + jnp.where(f == float(e), pstart[:, e:e + 1], 0.0)
        tile_expert = tile_expert + jnp.where(tile_end[:, e:e + 1] <= tile_id, 1.0, 0.0)
    pos = first_row + pltpu.roll(f, shift=4, axis=0)
    pos_ref[...] = jnp.clip(pos, 0.0, n_sorted - 1.0).astype(jnp.int32)
    is_row0 = lax.broadcasted_iota(jnp.int32, meta_ref.shape, 0) == 0
    meta_ref[...] = jnp.where(is_row0, tile_expert, n_used).astype(jnp.int32)


def _mix_route_compute(x_ref, ys_ref, y5_ref, wa_ref, wb_ref, nf_ref, wrh_ref, wrl_ref, br_ref,
                       x1_ref, rt_ref, carry, xn_buf, fields, col0):
    rows = x_ref.shape[0]
    x1 = x_ref[...] + _dot(ys_ref[...], wa_ref[...]) + _dot(y5_ref[...].astype(BF16), wb_ref[...])
    x1_ref[0:rows, :] = x1
    xn = _rms(x1, nf_ref[...])
    xn_buf[0:rows, :] = _pack_bf16_pairs(xn)

    xh = xn.astype(BF16)
    xl = (xn - xh.astype(F32)).astype(BF16)
    logits = _dot(xh, wrh_ref[...]) + _dot(xl, wrh_ref[...]) + _dot(xh, wrl_ref[...]) + br_ref[...]
    tm = logits.shape[0]
    lane = lax.broadcasted_iota(jnp.int32, logits.shape, 1).astype(F32)
    neg = -jnp.inf
    big = float(LANES)

    def first_max(v):
        m = jnp.max(v, axis=-1, keepdims=True)
        return m, jnp.min(jnp.where(v == m, lane, big), axis=-1, keepdims=True)

    coarse = lane < MOE_GROUPS
    mc, gsel = first_max(jnp.where(coarse, logits, neg))
    psel = 1.0 / jnp.sum(jnp.where(coarse, jnp.exp(logits - mc), 0.0), axis=-1, keepdims=True)
    lo = MOE_GROUPS + MOE_EPG * gsel
    lf = jnp.where((lane >= lo) & (lane < lo + MOE_EPG), logits, neg)
    m1, i1 = first_max(lf)
    m2, i2 = first_max(jnp.where(lane == i1, neg, lf))
    e2 = jnp.exp(m2 - m1)
    g1 = psel / (1.0 + e2)
    g2 = psel * e2 / (1.0 + e2)
    e_a, e_b = i1 - MOE_GROUPS, i2 - MOE_GROUPS

    pick_a, pick_b = lane == e_a, lane == e_b
    picks = jnp.where(pick_a | pick_b, 1.0, 0.0)
    earlier = lax.broadcasted_iota(jnp.int32, (tm, tm), 0) > lax.broadcasted_iota(jnp.int32, (tm, tm), 1)
    prior = _dot(earlier.astype(BF16), picks.astype(BF16)) + carry[...]
    rank_a = jnp.sum(jnp.where(pick_a, prior, 0.0), axis=-1, keepdims=True)
    rank_b = jnp.sum(jnp.where(pick_b, prior, 0.0), axis=-1, keepdims=True)
    carry[...] = prior[tm - 1:tm, :] + picks[tm - 1:tm, :]

    out = jnp.zeros_like(logits)
    for k, v in enumerate((e_a, e_b, g1, g2, rank_a, rank_b)):
        out = jnp.where(lane == float(k), v, out)
    rt_ref[0:rows, :] = out
    fields[:, pl.ds(col0, rows)] = out.T[0:SUBLANES, :]


def _mix_route(prompt, sample, consts, tm, n_tiles):
    n_prompt, n_sample = prompt[0].shape[0], sample[0].shape[0]
    assert n_prompt % tm == 0 and n_sample <= tm
    n_blocks = n_prompt // tm
    total_rows = n_prompt + n_sample
    row = lambda w: pl.BlockSpec((tm, w), lambda i: (jnp.minimum(i, n_blocks - 1), 0))
    out_row = lambda w: pl.BlockSpec((tm, w), lambda i: (i, 0))
    assert total_rows % LANES == 0 and n_tiles <= 2 * LANES
    whole = lambda shape: pl.BlockSpec(shape, lambda i: (0, 0))
    return pl.pallas_call(
        functools.partial(_mix_route_body, n_blocks, n_tiles * MOE_TILE),
        grid=(n_blocks + 1,),
        in_specs=([row(D_MODEL), row(SSD_WIDTH), row(S5_WIDTH)] + [_full_spec(a) for a in sample]
                  + [_full_spec(a) for a in consts]),
        out_specs=[out_row(D_MODEL), pl.BlockSpec(memory_space=pl.ANY), out_row(LANES),
                   whole((SUBLANES, total_rows)), whole((SUBLANES, 2 * LANES))],
        out_shape=[jax.ShapeDtypeStruct((total_rows, D_MODEL), F32),
                   jax.ShapeDtypeStruct((total_rows, PACK_ROWS, LANES), jnp.uint32),
                   jax.ShapeDtypeStruct((total_rows, LANES), F32),
                   jax.ShapeDtypeStruct((SUBLANES, total_rows), jnp.int32),
                   jax.ShapeDtypeStruct((SUBLANES, 2 * LANES), jnp.int32)],
        scratch_shapes=[pltpu.VMEM((1, LANES), F32), pltpu.VMEM((SUBLANES, total_rows), F32),
                        pltpu.VMEM((2, tm, D_MODEL // 2), jnp.uint32), pltpu.SemaphoreType.DMA((2, PACK_ROWS))],
        compiler_params=pltpu.CompilerParams(dimension_semantics=("arbitrary",), vmem_limit_bytes=VMEM_LIMIT),
        name="mix_route",
    )(*prompt, *sample, *consts)


def _sc_mesh():
    return plsc.VectorSubcoreMesh(core_axis_name="c", subcore_axis_name="s")


def _sc_worker():
    return lax.axis_index("s") * SC_CORES + lax.axis_index("c")


def _sc_dispatch(xn, pos_a, pos_b, n_rows):
    n_tok = xn.shape[0]
    ch = SC_DISPATCH_ROWS
    n_chunks = n_tok // ch
    assert n_tok % ch == 0 and n_chunks >= SC_WORKERS
    max_mine = -(-n_chunks // SC_WORKERS)
    row_shape, dtype = xn.shape[1:], xn.dtype
    stage = [pltpu.VMEM((ch,), jnp.int32), pltpu.VMEM((ch,), jnp.int32), pltpu.VMEM((ch,) + row_shape, dtype),
             pltpu.SemaphoreType.DMA]

    @functools.partial(
        pl.kernel, mesh=_sc_mesh(),
        out_type=jax.ShapeDtypeStruct((n_rows,) + row_shape, dtype),
        scratch_types=stage + stage + [pltpu.SemaphoreType.DMA])
    def push(xn_hbm, pa_hbm, pb_hbm, xs_hbm, ia0, ib0, rows0, lsem0, ia1, ib1, rows1, lsem1, ssem):
        wid = _sc_worker()
        mine = (n_chunks - wid + SC_WORKERS - 1) // SC_WORKERS
        bufs = ((ia0, ib0, rows0, lsem0), (ia1, ib1, rows1, lsem1))

        def loads(t, b):
            ia, ib, rows, sem = bufs[b]
            off = pl.multiple_of((wid + t * SC_WORKERS) * ch, ch)
            return (pltpu.make_async_copy(pa_hbm.at[pl.ds(off, ch)], ia, sem),
                    pltpu.make_async_copy(pb_hbm.at[pl.ds(off, ch)], ib, sem),
                    pltpu.make_async_copy(xn_hbm.at[pl.ds(off, ch)], rows, sem))

        def stage_in(t, b):
            for c in loads(t, b):
                c.start()

        def scatter(t, b):
            ia, ib, rows, _ = bufs[b]
            for c in loads(t, b):
                c.wait()
            first = pltpu.async_copy(rows, xs_hbm.at[ia], ssem)
            second = pltpu.async_copy(rows, xs_hbm.at[ib], ssem)
            first.wait()
            second.wait()

        stage_in(0, 0)

        @pl.loop(0, (max_mine + 1) // 2)
        def _(p):
            t = 2 * p

            @pl.when(t + 1 < mine)
            def _():
                stage_in(t + 1, 1)

            @pl.when(t < mine)
            def _():
                scatter(t, 0)

            @pl.when(t + 2 < mine)
            def _():
                stage_in(t + 2, 0)

            @pl.when(t + 1 < mine)
            def _():
                scatter(t + 1, 1)

    return push(xn, pos_a, pos_b)


def _sc_collect(ysorted, pos_flat, ch):
    n_pick = pos_flat.shape[0]
    per_worker = n_pick // SC_WORKERS
    n_chunks = per_worker // ch
    assert n_pick % SC_WORKERS == 0 and per_worker % ch == 0
    row_shape, dtype = ysorted.shape[1:], ysorted.dtype

    @functools.partial(
        pl.kernel, mesh=_sc_mesh(),
        out_type=jax.ShapeDtypeStruct((n_pick,) + row_shape, dtype),
        scratch_types=[pltpu.VMEM((ch,), jnp.int32), pltpu.VMEM((ch,), jnp.int32),
                       pltpu.VMEM((ch,) + row_shape, dtype), pltpu.VMEM((ch,) + row_shape, dtype),
                       pltpu.SemaphoreType.DMA, pltpu.SemaphoreType.DMA])
    def pull(ys_hbm, pos_hbm, out_hbm, idx0, idx1, rows0, rows1, sem0, sem1):
        base = _sc_worker() * per_worker
        bufs = ((idx0, rows0, sem0), (idx1, rows1, sem1))

        def offset(j):
            return pl.multiple_of(base + j * ch, SUBLANES)

        def fetch(j, b):
            idx, rows, sem = bufs[b]
            pltpu.sync_copy(pos_hbm.at[pl.ds(offset(j), ch)], idx)
            pltpu.async_copy(ys_hbm.at[idx], rows, sem)

        def flush(j, b):
            idx, rows, sem = bufs[b]
            pltpu.make_async_copy(ys_hbm.at[idx], rows, sem).wait()
            pltpu.sync_copy(rows, out_hbm.at[pl.ds(offset(j), ch)])

        fetch(0, 0)

        @pl.loop(0, n_chunks // 2)
        def _(p):
            j = 2 * p
            fetch(j + 1, 1)
            flush(j, 0)

            @pl.when(j + 2 < n_chunks)
            def _():
                fetch(j + 2, 0)

            flush(j + 1, 1)

        if n_chunks % 2:
            flush(n_chunks - 1, 0)

    return pull(ysorted, pos_flat)


def _moe_ffn_body(*refs):
    for k in range(MOE_TILES_PER_STEP):
        _moe_ffn_tile(k, *refs)


def _moe_ffn_tile(k, te_ref, nused_ref, x_ref, wg_hbm, wu_hbm, wd_hbm, y_ref,
                  wg_f32, wu_f32, wd_f32, wgb, wub, wdb, slot_ref, sems):
    i = pl.program_id(0) * MOE_TILES_PER_STEP + k
    n_used = nused_ref[0]
    window = pl.ds(k * MOE_TILE * PACK_ROWS, MOE_TILE * PACK_ROWS)
    x_ref, y_ref = x_ref.at[window, :], y_ref.at[window, :]

    def fetch(expert, slot):
        return (pltpu.make_async_copy(wg_hbm.at[expert], wg_f32.at[slot], sems.at[slot, 0]),
                pltpu.make_async_copy(wu_hbm.at[expert], wu_f32.at[slot], sems.at[slot, 1]),
                pltpu.make_async_copy(wd_hbm.at[expert], wd_f32.at[slot], sems.at[slot, 2]))

    @pl.when(i >= n_used)
    def _unused_tile():
        y_ref[...] = jnp.zeros_like(y_ref)

    @pl.when(i < n_used)
    def _tile():
        expert = te_ref[i]

        @pl.when(i == 0)
        def _first_fetch():
            slot_ref[0] = 0
            for c in fetch(expert, 0):
                c.start()

        @pl.when((i == 0) | (expert != te_ref[jnp.maximum(i - 1, 0)]))
        def _new_expert():
            slot = slot_ref[0]
            nxt = lax.while_loop(lambda k: (k < n_used) & (te_ref[jnp.minimum(k, n_used - 1)] == expert),
                                 lambda k: k + 1, i + 1)

            @pl.when(nxt < n_used)
            def _prefetch():
                for c in fetch(te_ref[jnp.minimum(nxt, n_used - 1)], 1 - slot):
                    c.start()

            for c in fetch(expert, slot):
                c.wait()
            wgb[...] = wg_f32[slot].astype(BF16)
            wub[...] = wu_f32[slot].astype(BF16)
            wdb[...] = wd_f32[slot].astype(BF16)
            slot_ref[0] = 1 - slot

        x = _unpack_bf16_pairs(x_ref, MOE_TILE).astype(BF16)
        gate = _dot(x, wgb[...])
        hmid = (gate * jax.nn.sigmoid(gate)) * _dot(x, wub[...])
        y = _dot(hmid.astype(BF16), wdb[...])
        packed = _pack_bf16_pairs(y)
        for j in range(PACK_ROWS):
            y_ref[pl.ds(j, MOE_TILE, stride=PACK_ROWS), :] = packed[:, j * LANES:(j + 1) * LANES]


def _moe_ffn(tile_expert, n_used, xsorted, w_gate, w_up, w_down):
    n_tiles = tile_expert.shape[0]
    per_step = MOE_TILES_PER_STEP
    assert n_tiles % per_step == 0
    hbm = pl.BlockSpec(memory_space=pl.ANY)
    tile = lambda imap: pl.BlockSpec((per_step * MOE_TILE * PACK_ROWS, LANES), imap)
    up_shape, down_shape = (D_MODEL, MOE_D_FF), (MOE_D_FF, D_MODEL)
    return pl.pallas_call(
        _moe_ffn_body,
        grid_spec=pltpu.PrefetchScalarGridSpec(
            num_scalar_prefetch=2,
            grid=(n_tiles // per_step,),
            in_specs=[tile(lambda i, te, nu: (jnp.clip(i, 0, jnp.maximum(nu[0] - 1, 0) // per_step), 0)),
                      hbm, hbm, hbm],
            out_specs=tile(lambda i, te, nu: (i, 0)),
            scratch_shapes=[pltpu.VMEM((2,) + up_shape, F32), pltpu.VMEM((2,) + up_shape, F32),
                            pltpu.VMEM((2,) + down_shape, F32),
                            pltpu.VMEM(up_shape, BF16), pltpu.VMEM(up_shape, BF16), pltpu.VMEM(down_shape, BF16),
                            pltpu.SMEM((1,), jnp.int32), pltpu.SemaphoreType.DMA((2, 3))]),
        out_shape=jax.ShapeDtypeStruct((n_tiles * MOE_TILE * PACK_ROWS, LANES), jnp.uint32),
        compiler_params=pltpu.CompilerParams(dimension_semantics=("arbitrary",), vmem_limit_bytes=VMEM_LIMIT),
        name="moe_ffn",
    )(tile_expert, n_used, xsorted, w_gate, w_up, w_down)


def _combine_body(x1_ref, rt_ref, ya_ref, yb_ref, nf_ref, *rest):
    out_ref = rest[-1]
    rt = rt_ref[...]
    x1 = x1_ref[...]
    tm = x1.shape[0]

    x2 = (x1 + rt[:, 2:3] * _unpack_bf16_pairs(ya_ref.at[0], tm)
          + rt[:, 3:4] * _unpack_bf16_pairs(yb_ref.at[0], tm))
    out_ref[...] = _rms(x2, nf_ref[...])


def _combine(x1, rt, y_picks, nf, tm, rows, x_block, y_block, out_rows, out_block, out_buf=None):
    row = lambda w: pl.BlockSpec((tm, w), lambda i: (i + x_block, 0))
    pick = lambda k: pl.BlockSpec((1, tm * PACK_ROWS, LANES), lambda i: (k, i + y_block, 0))
    in_specs = [row(D_MODEL), row(LANES), pick(0), pick(1), pl.BlockSpec((1, D_MODEL), lambda i: (0, 0))]
    args = [x1, rt, y_picks, y_picks, nf]
    aliases = {}
    if out_buf is not None:
        in_specs.append(pl.BlockSpec(memory_space=pl.ANY))
        aliases[len(args)] = 0
        args.append(out_buf)
    return pl.pallas_call(
        _combine_body,
        grid=(rows // tm,),
        in_specs=in_specs,
        out_specs=pl.BlockSpec((tm, D_MODEL), lambda i: (i + out_block, 0)),
        out_shape=jax.ShapeDtypeStruct((out_rows, D_MODEL), F32),
        input_output_aliases=aliases,
        compiler_params=pltpu.CompilerParams(dimension_semantics=("parallel",), vmem_limit_bytes=VMEM_LIMIT),
        name="moe_combine",
    )(*args)


def _s5_tables(a_re, a_im, log_dt, b_re, b_im, c_re, c_im):
    dt = jnp.exp(log_dt)[:, None]
    mag = jnp.exp(a_re * dt)
    ab_re = mag * jnp.cos(a_im * dt)
    ab_im = mag * jnp.sin(a_im * dt)
    den = a_re * a_re + a_im * a_im
    nr = ab_re - 1.0
    q_re = (nr * a_re + ab_im * a_im) / den
    q_im = (ab_im * a_re - nr * a_im) / den
    bb_re = q_re[..., None] * b_re - q_im[..., None] * b_im
    bb_im = q_re[..., None] * b_im + q_im[..., None] * b_re
    nblk = S5_GROUPS // 16
    kw, nw = 16 * S5_GROUP_CH, 16 * S5_STATE
    same_group = (jnp.arange(kw)[:, None] // S5_GROUP_CH) == (jnp.arange(nw)[None, :] // S5_STATE)

    def in_map(bb):
        rows = bb.reshape(nblk, 16, S5_STATE, S5_GROUP_CH).transpose(0, 1, 3, 2).reshape(nblk, kw, S5_STATE)
        return jnp.where(same_group, jnp.tile(rows, (1, 1, 16)), 0.0)

    def out_map(cc):
        cols = cc.reshape(nblk, 16, S5_GROUP_CH, S5_STATE).transpose(0, 3, 1, 2).reshape(nblk, S5_STATE, kw)
        return jnp.where(same_group.T, jnp.tile(cols, (1, 16, 1)), 0.0)

    wb = jnp.concatenate([in_map(bb_re), in_map(bb_im)], axis=-1).astype(BF16)
    return (wb, ab_re.reshape(1, S5_LANES), ab_im.reshape(1, S5_LANES),
            out_map(c_re).astype(BF16), out_map(-c_im).astype(BF16))


def kernel(x_prompt, x_sample, state_ssd_conv, state_ssd_ssm, state_s5_re, state_s5_im, meta_tokens, norm_mix, w_in, conv_w, conv_b, dt_bias, a_log, d_ssd, ssd_norm, s5_a_re, s5_a_im, s5_log_dt, s5_b_re, s5_b_im, s5_c_re, s5_c_im, s5_d, w_glu, b_glu, s5_norm, w_out, norm_ffn, router_coarse_w, router_coarse_b, router_fine_w, router_fine_b, w_gate, w_up, w_down, norm_final):
    bp, seq, _ = x_prompt.shape
    bs = x_sample.shape[0]
    n_prompt = bp * seq
    n_tok = n_prompt + bs
    row2 = lambda v: v.reshape(1, -1)
    pad_heads = lambda v: jnp.pad(v, (0, LANES - SSD_HEADS)).reshape(1, LANES)

    w = w_in[0]
    o1, o2, o3 = SSD_WIDTH, SSD_WIDTH + SSD_CONV_DIM, SSD_WIDTH + SSD_CONV_DIM + SSD_HEADS
    wz, wx, wu = w[:, :o1].astype(BF16), w[:, o1:o2].astype(BF16), w[:, o3:].astype(BF16)
    wdt = jnp.pad(w[:, o2:o3], ((0, 0), (0, LANES - SSD_HEADS))).astype(BF16)
    g_mix = row2(norm_mix[0])
    cw, cb = conv_w[0], row2(conv_b[0])
    dtb, alog = pad_heads(dt_bias[0]), pad_heads(a_log[0])
    dexp = row2(jnp.repeat(d_ssd[0], SSD_HEAD_DIM))
    snrm = row2(ssd_norm[0])
    eexp = (jnp.arange(LANES)[:, None] == (jnp.arange(SSD_WIDTH) // SSD_HEAD_DIM)[None, :]).astype(BF16)
    wb5, ab_re, ab_im, wcr, wci = _s5_tables(s5_a_re[0], s5_a_im[0], s5_log_dt[0], s5_b_re[0], s5_b_im[0],
                                             s5_c_re[0], s5_c_im[0])
    d5, wglu, bglu, nrm5 = row2(s5_d[0]), w_glu[0].astype(BF16), row2(b_glu[0]), row2(s5_norm[0])
    wo_a, wo_b = w_out[0][:SSD_WIDTH].astype(BF16), w_out[0][SSD_WIDTH:].astype(BF16)
    w_r = jnp.concatenate([router_coarse_w[0], router_fine_w[0].transpose(1, 0, 2).reshape(D_MODEL, MOE_EXPERTS)], axis=1)
    w_r = jnp.pad(w_r, ((0, 0), (0, LANES - w_r.shape[1])))
    wrh = w_r.astype(BF16)
    wrl = (w_r - wrh.astype(F32)).astype(BF16)
    b_r = jnp.concatenate([router_coarse_b[0], router_fine_b[0].reshape(-1)])
    b_r = jnp.pad(b_r, (0, LANES - b_r.shape[0])).reshape(1, LANES)

    zp, xbcp, dtp, up = _in_proj(x_prompt.reshape(n_prompt, D_MODEL), g_mix, wz, wx, wdt, wu, TOK_TILE, BF16, F32)
    xsm = jnp.concatenate([x_sample.reshape(bs, D_MODEL), meta_tokens], axis=0)
    zs, xbcs, dts, us = _in_proj(xsm, g_mix, wz, wx, wdt, wu, xsm.shape[0], F32, F32)

    front = SSD_CHUNK - N_META
    padf = lambda a: jnp.pad(a[bs:], ((front, 0), (0, 0)))[None]
    gw = SSD_HPG * SSD_HEAD_DIM
    ssd_consts = (cw, cb, dtb, alog, dexp, snrm, eexp)
    _, ctail_m, _, ht_m = _ssd_chunked(
        padf(xbcs).astype(BF16), padf(dts), jnp.zeros((1, SSD_CHUNK, SSD_WIDTH), F32),
        jnp.zeros((1, SUBLANES, SSD_CONV_DIM), F32), jnp.zeros((1, SSD_GROUPS, SSD_STATE, gw), F32),
        *ssd_consts, mask_rows=front)
    y_ssd_p, ctail_p, ssm_p, _ = _ssd_chunked(
        xbcp.reshape(bp, seq, SSD_CONV_DIM), dtp.reshape(bp, seq, LANES), zp.reshape(bp, seq, SSD_WIDTH),
        ctail_m, ht_m, *ssd_consts, mask_rows=0)

    abr8, abi8 = jnp.broadcast_to(ab_re, (bp, S5_LANES)), jnp.broadcast_to(ab_im, (bp, S5_LANES))
    um8 = jnp.repeat(us[bs:], bp, axis=0).astype(BF16)
    y_s5_p, s5re_p, s5im_p = _s5_seq(up.reshape(bp, seq, S5_WIDTH), um8, wb5, abr8, abi8,
                                     wcr, wci, d5, wglu, bglu, nrm5)

    cst = state_ssd_conv[0]
    xt_s, dt_s, dec_s, bc, xs_s = _ssd_step_prep(xbcs[:bs], cst[:, 0], cst[:, 1], cst[:, 2], dts[:bs],
                                                 cw, cb, dtb, alog)
    ssm_s, y_core = _ssd_step(dt_s[:, :SSD_HEADS].reshape(-1), dec_s[:, :SSD_HEADS].reshape(-1),
                              state_ssd_ssm[0], xt_s, bc)
    y_ssd_s, y_s5_s, s5re_s, s5im_s = _sample_post(
        y_core, xs_s, zs[:bs], dexp, snrm, us[:bs], state_s5_re[0].reshape(bs, S5_LANES),
        state_s5_im[0].reshape(bs, S5_LANES), wb5, ab_re, ab_im, wcr, wci, d5, wglu, bglu, nrm5)

    route_consts = (wo_a, wo_b, row2(norm_ffn[0]), wrh, wrl, b_r)
    n_tiles = -(-2 * n_tok // MOE_TILE) + MOE_EXPERTS
    n_tiles = -(-n_tiles // MOE_TILES_PER_STEP) * MOE_TILES_PER_STEP
    x1, xn, rt, pos, meta = _mix_route(
        (x_prompt.reshape(n_prompt, D_MODEL), y_ssd_p.reshape(n_prompt, SSD_WIDTH), y_s5_p.reshape(n_prompt, S5_WIDTH)),
        (x_sample.reshape(bs, D_MODEL), y_ssd_s, y_s5_s), route_consts, TOK_TILE, n_tiles)

    pos_a, pos_b = pos[0], pos[1]
    tile_expert, n_used = meta[0, :n_tiles], meta[1, :1]
    xsorted = _sc_dispatch(xn, pos_a, pos_b, n_tiles * MOE_TILE)
    ysorted = _moe_ffn(tile_expert, n_used, xsorted.reshape(-1, LANES), w_gate[0], w_up[0], w_down[0])
    nfin = row2(norm_final)

    half = n_prompt // 2

    def collect(lo, hi, ch):
        picks = jnp.concatenate([pos_a[lo:hi], pos_b[lo:hi]])
        packed_rows = ysorted.reshape(-1, PACK_ROWS, LANES)
        return _sc_collect(packed_rows, picks, ch).reshape(2, (hi - lo) * PACK_ROWS, LANES)

    picks_1 = collect(0, half, SC_COLLECT_ROWS[0])
    picks_2 = collect(half, n_tok, SC_COLLECT_ROWS[1])
    blocks = half // MOE_TILE
    y_p = _combine(x1, rt, picks_1, nfin, MOE_TILE, half, 0, 0, n_prompt, 0)
    y_p = _combine(x1, rt, picks_2, nfin, MOE_TILE, half, blocks, 0, n_prompt, blocks, out_buf=y_p)
    y_s = _combine(x1, rt, picks_2, nfin, bs, bs, n_prompt // bs, half // bs, bs, 0)

    s5_state = lambda a, b: a.reshape(1, b, S5_GROUPS, S5_STATE)
    new_conv_s = jnp.stack([cst[:, 1], cst[:, 2], xbcs[:bs]], axis=1)[None]
    return (y_p.reshape(bp, seq, D_MODEL), y_s.reshape(bs, 1, D_MODEL),
            ctail_p[:, SUBLANES - (SSD_CONV - 1):][None], ssm_p[None], s5_state(s5re_p, bp), s5_state(s5im_p, bp),
            new_conv_s, ssm_s[None], s5_state(s5re_s, bs), s5_state(s5im_s, bs))
```

```python
import functools

import jax
import jax.numpy as jnp
from jax import lax
from jax.experimental import pallas as pl
from jax.experimental.pallas import tpu as pltpu
from jax.experimental.pallas import tpu_sc as plsc

F32, BF16 = jnp.float32, jnp.bfloat16

D_MODEL = 1024
N_META = 16
SSD_WIDTH = 1024
SSD_HEAD_DIM = 64
SSD_HEADS = 16
SSD_GROUPS = 2
SSD_HPG = SSD_HEADS // SSD_GROUPS
SSD_STATE = 128
SSD_CONV = 4
SSD_CHUNK = 128
SSD_CONV_DIM = SSD_WIDTH + 2 * SSD_GROUPS * SSD_STATE
S5_WIDTH = 1024
S5_GROUP_CH = 16
S5_GROUPS = 64
S5_STATE = 64
S5_LANES = S5_GROUPS * S5_STATE
MOE_GROUPS = 4
MOE_EPG = 8
MOE_EXPERTS = MOE_GROUPS * MOE_EPG
MOE_D_FF = 512
EPS = 1e-6

LANES = 128
SUBLANES = 8
VMEM_LIMIT = 56 * 1024 * 1024

SSD_CHUNKS_PER_STEP = 4
S5_TIME_TILE = 64
S5_SCAN_LANES = 512
MOE_TILE = 256
MOE_TILES_PER_STEP = 4
SLAB_ROWS = D_MODEL // LANES
PACK_ROWS = SLAB_ROWS // 2
SC_CORES = 2
SC_SUBCORES = 16
SC_WORKERS = SC_CORES * SC_SUBCORES
SC_DISPATCH_ROWS = 32
SC_COLLECT_ROWS = (32, 40)
TOK_TILE = 512


def _dot(a, b):
    return jnp.dot(a, b, preferred_element_type=F32)


def _rms(x, g):
    return x * lax.rsqrt(jnp.mean(x * x, axis=-1, keepdims=True) + EPS) * g


def _softplus(x):
    return jnp.maximum(x, 0.0) + jnp.log1p(jnp.exp(-jnp.abs(x)))


def _split3(x):
    hi = x.astype(BF16)
    r = x - hi.astype(F32)
    mid = r.astype(BF16)
    lo = (r - mid.astype(F32)).astype(BF16)
    return hi, mid, lo


def _dot3(x, w):
    hi, mid, lo = _split3(x)
    return _dot(hi, w) + _dot(mid, w) + _dot(lo, w)


def _dot3_left(w, x):
    hi, mid, lo = _split3(x)
    return _dot(w, hi) + _dot(w, mid) + _dot(w, lo)


def _pack_bf16_pairs(x):
    bits = pltpu.bitcast(x.astype(BF16).astype(F32), jnp.uint32)
    half = x.shape[1] // 2
    return (bits[:, :half] & jnp.uint32(0xFFFF0000)) | (bits[:, half:] >> jnp.uint32(16))


def _unpack_bf16_pairs(ref, rows):
    words = [ref[pl.ds(j, rows, stride=PACK_ROWS), :] for j in range(PACK_ROWS)]
    high = [pltpu.bitcast(w & jnp.uint32(0xFFFF0000), F32) for w in words]
    low = [pltpu.bitcast(w << jnp.uint32(16), F32) for w in words]
    return jnp.concatenate(high + low, axis=-1)


def _full_spec(a):
    nd = a.ndim
    return pl.BlockSpec(a.shape, lambda *_: (0,) * nd)


def _resident_spec(a):
    nd = a.ndim
    return pl.BlockSpec(a.shape, lambda *_: (0,) * nd, pipeline_mode=pl.Buffered(1))


def _in_proj_body(x_ref, g_ref, wz_ref, wx_ref, wdt_ref, wu_ref, z_ref, xbc_ref, dt_ref, u_ref):
    xb = _rms(x_ref[...], g_ref[...]).astype(BF16)
    z_ref[...] = _dot(xb, wz_ref[...]).astype(z_ref.dtype)
    xbc_ref[...] = _dot(xb, wx_ref[...]).astype(xbc_ref.dtype)
    dt_ref[...] = _dot(xb, wdt_ref[...])
    u_ref[...] = _dot(xb, wu_ref[...]).astype(u_ref.dtype)


def _in_proj(x2d, g, wz, wx, wdt, wu, tm, act_dtype, u_dtype):
    rows = x2d.shape[0]
    row = lambda w: pl.BlockSpec((tm, w), lambda i: (i, 0))
    return pl.pallas_call(
        _in_proj_body,
        grid=(rows // tm,),
        in_specs=[row(D_MODEL), _full_spec(g), _full_spec(wz), _full_spec(wx), _full_spec(wdt), _full_spec(wu)],
        out_specs=[row(SSD_WIDTH), row(SSD_CONV_DIM), row(LANES), row(S5_WIDTH)],
        out_shape=[jax.ShapeDtypeStruct((rows, SSD_WIDTH), act_dtype),
                   jax.ShapeDtypeStruct((rows, SSD_CONV_DIM), act_dtype),
                   jax.ShapeDtypeStruct((rows, LANES), F32),
                   jax.ShapeDtypeStruct((rows, S5_WIDTH), u_dtype)],
        compiler_params=pltpu.CompilerParams(dimension_semantics=("parallel",), vmem_limit_bytes=VMEM_LIMIT),
        name="in_proj",
    )(x2d, g, wz, wx, wdt, wu)


def _ssd_body(mask_rows, per_step, *refs):
    for k in range(per_step):
        _ssd_chunk(mask_rows, per_step, k, *refs)


def _ssd_chunk(mask_rows, per_step, k, xbc_ref, dt_ref, z_ref, cinit_ref, hinit_ref, cw_ref, cb_ref, dtb_ref,
               alog_ref, dexp_ref, nrm_ref, eexp_ref, y_ref, ctail_ref, st_ref, hto_ref, xwin, hT):
    L = SSD_CHUNK
    c = pl.program_id(1) * per_step + k
    n_chunks = pl.num_programs(1) * per_step
    window = pl.ds(k * L, L)
    xbc_ref, dt_ref, z_ref, y_ref = (r.at[:, window, :] for r in (xbc_ref, dt_ref, z_ref, y_ref))

    @pl.when(c == 0)
    def _init():
        xwin[...] = cinit_ref[0]
        hT[...] = hinit_ref[0]

    x_b = xbc_ref[0]
    x_f = x_b.astype(F32)
    taps = SSD_CONV - 1
    m_i = lax.broadcasted_iota(jnp.int32, (taps * L, L), 0)
    r_i = lax.broadcasted_iota(jnp.int32, (taps * L, L), 1)
    shift = (r_i + (taps - m_i // L) == m_i % L).astype(BF16)
    shifted = _dot(shift, x_b)
    acc = cb_ref[...] + x_f * cw_ref[taps:taps + 1, :]
    for k in range(taps):
        acc = acc + shifted[k * L:(k + 1) * L, :] * cw_ref[k:k + 1, :]
    joint = jnp.concatenate([xwin[...], x_f[0:SUBLANES, :]], axis=0)
    row8 = lax.broadcasted_iota(jnp.int32, (SUBLANES, 1), 0)
    head = acc[0:SUBLANES, :]
    for k in range(taps):
        d = taps - k
        head = head + jnp.where(row8 < d, joint[SUBLANES - d:2 * SUBLANES - d, :], 0.0) * cw_ref[k:k + 1, :]
    acc = jnp.concatenate([head, acc[SUBLANES:, :]], axis=0)
    tail = x_f[L - SUBLANES:, :]
    xwin[...] = tail
    ctail_ref[0] = tail

    xact = acc * jax.nn.sigmoid(acc)
    dt = _softplus(dt_ref[0] + dtb_ref[...])
    if mask_rows:
        valid = lax.broadcasted_iota(jnp.int32, (L, 1), 0) >= mask_rows
        xact = jnp.where(valid, xact, 0.0)
        dt = jnp.where(valid, dt, 0.0)

    a_neg = -jnp.exp(alog_ref[...])
    dA = dt * a_neg
    row_i = lax.broadcasted_iota(jnp.int32, (L, L), 0)
    col_i = lax.broadcasted_iota(jnp.int32, (L, L), 1)
    causal = row_i >= col_i
    tril = causal.astype(BF16)
    cs = _dot3_left(tril, dA)
    csT = cs.T
    dtT = dt.T
    ecs = jnp.exp(cs)
    wdec = jnp.exp(cs[L - 1:L, :] - cs) * dt
    eexp = eexp_ref[...]
    ecs_e = _dot3(ecs, eexp)
    wdec_e = _dot3(wdec, eexp)
    lane = lax.broadcasted_iota(jnp.int32, (L, LANES), 1)
    first_half = lane < SSD_HEAD_DIM

    gw = SSD_HPG * SSD_HEAD_DIM
    y_groups = []
    for g in range(SSD_GROUPS):
        b_g = xact[:, SSD_WIDTH + g * SSD_STATE: SSD_WIDTH + (g + 1) * SSD_STATE]
        c_g = xact[:, SSD_WIDTH + (SSD_GROUPS + g) * SSD_STATE: SSD_WIDTH + (SSD_GROUPS + g + 1) * SSD_STATE]
        b_b = b_g.astype(BF16)
        c_b = c_g.astype(BF16)
        cb = lax.dot_general(c_b, b_b, (((1,), (1,)), ((), ())), preferred_element_type=F32)
        xs_g = xact[:, g * gw:(g + 1) * gw]
        h_prev = hT[g]
        y_off = _dot(c_b, h_prev.astype(BF16)) * ecs_e[:, g * gw:(g + 1) * gw]
        xdec = (xs_g * wdec_e[:, g * gw:(g + 1) * gw]).astype(BF16)
        hT[g] = h_prev * ecs_e[L - 1:L, g * gw:(g + 1) * gw] + _dot(b_g.T.astype(BF16), xdec)
        pieces = []
        for j in range(SSD_HPG // 2):
            xs_pair = xs_g[:, j * LANES:(j + 1) * LANES]
            halves = (jnp.where(first_half, xs_pair, 0.0).astype(BF16),
                      jnp.where(first_half, 0.0, xs_pair).astype(BF16))
            yd = None
            for t in range(2):
                h = g * SSD_HPG + 2 * j + t
                seg = cs[:, h:h + 1] - csT[h:h + 1, :]
                lmat = jnp.exp(jnp.where(causal, seg, -jnp.inf))
                m = (cb * lmat * dtT[h:h + 1, :]).astype(BF16)
                part = _dot(m, halves[t])
                yd = part if yd is None else yd + part
            pieces.append(yd)
        y_groups.append(jnp.concatenate(pieces, axis=-1) + y_off + dexp_ref[:, g * gw:(g + 1) * gw] * xs_g)
    y = jnp.concatenate(y_groups, axis=-1)
    z = z_ref[0].astype(F32)
    y_ref[0] = _rms(y * (z * jax.nn.sigmoid(z)), nrm_ref[...]).astype(y_ref.dtype)

    @pl.when(c == n_chunks - 1)
    def _emit():
        hto_ref[0] = hT[...]
        for g in range(SSD_GROUPS):
            t = hT[g].T
            for k in range(SSD_HPG):
                st_ref[0, g * SSD_HPG + k] = t[k * SSD_HEAD_DIM:(k + 1) * SSD_HEAD_DIM, :]


def _ssd_chunked(xbc, dt, z, cinit, hinit, cw, cb, dtb, alog, dexp, nrm, eexp, mask_rows):
    bsz, seq, _ = xbc.shape
    nc = seq // SSD_CHUNK
    per_step = SSD_CHUNKS_PER_STEP if nc % SSD_CHUNKS_PER_STEP == 0 else 1
    gw = SSD_HPG * SSD_HEAD_DIM
    blk = lambda w: pl.BlockSpec((1, per_step * SSD_CHUNK, w), lambda b, c: (b, c, 0))
    return pl.pallas_call(
        functools.partial(_ssd_body, mask_rows, per_step),
        grid=(bsz, nc // per_step),
        in_specs=[blk(SSD_CONV_DIM), blk(LANES), blk(SSD_WIDTH),
                  pl.BlockSpec((1, SUBLANES, SSD_CONV_DIM), lambda b, c: (0, 0, 0)),
                  pl.BlockSpec((1, SSD_GROUPS, SSD_STATE, gw), lambda b, c: (0, 0, 0, 0)),
                  _full_spec(cw), _full_spec(cb), _full_spec(dtb), _full_spec(alog),
                  _full_spec(dexp), _full_spec(nrm), _full_spec(eexp)],
        out_specs=[blk(SSD_WIDTH),
                   pl.BlockSpec((1, SUBLANES, SSD_CONV_DIM), lambda b, c: (b, 0, 0)),
                   pl.BlockSpec((1, SSD_HEADS, SSD_HEAD_DIM, SSD_STATE), lambda b, c: (b, 0, 0, 0)),
                   pl.BlockSpec((1, SSD_GROUPS, SSD_STATE, gw), lambda b, c: (b, 0, 0, 0))],
        out_shape=[jax.ShapeDtypeStruct((bsz, seq, SSD_WIDTH), BF16),
                   jax.ShapeDtypeStruct((bsz, SUBLANES, SSD_CONV_DIM), F32),
                   jax.ShapeDtypeStruct((bsz, SSD_HEADS, SSD_HEAD_DIM, SSD_STATE), F32),
                   jax.ShapeDtypeStruct((bsz, SSD_GROUPS, SSD_STATE, gw), F32)],
        scratch_shapes=[pltpu.VMEM((SUBLANES, SSD_CONV_DIM), F32),
                        pltpu.VMEM((SSD_GROUPS, SSD_STATE, gw), F32)],
        compiler_params=pltpu.CompilerParams(dimension_semantics=("parallel", "arbitrary"),
                                             vmem_limit_bytes=VMEM_LIMIT),
        name="ssd_chunked",
    )(xbc, dt, z, cinit, hinit, cw, cb, dtb, alog, dexp, nrm, eexp)


def _ssd_step_prep_body(xbc_ref, c0_ref, c1_ref, c2_ref, dt_ref, cw_ref, cb_ref, dtb_ref, alog_ref,
                        xt_ref, dt_out_ref, dec_ref, bc_ref, xs_ref):
    acc = cb_ref[...]
    for k, r in enumerate((c0_ref, c1_ref, c2_ref, xbc_ref)):
        acc = acc + r[...] * cw_ref[k:k + 1, :]
    xact = acc * jax.nn.sigmoid(acc)
    xs = xact[:, :SSD_WIDTH]
    dt = _softplus(dt_ref[...] + dtb_ref[...])
    dt_out_ref[...] = dt
    dec_ref[...] = jnp.exp(dt * -jnp.exp(alog_ref[...]))
    bc_ref[...] = xact[:, SSD_WIDTH:]
    xs_ref[...] = xs
    xt_ref[...] = xs.T.astype(xt_ref.dtype)


def _ssd_step_prep(xbc, c0, c1, c2, dt, cw, cb, dtb, alog):
    n = xbc.shape[0]
    args = (xbc, c0, c1, c2, dt, cw, cb, dtb, alog)
    spec = lambda r, w: pl.BlockSpec((r, w), lambda: (0, 0))
    return pl.pallas_call(
        _ssd_step_prep_body,
        in_specs=[_full_spec(a) for a in args],
        out_specs=[spec(SSD_WIDTH, n), spec(n, LANES), spec(n, LANES), spec(n, 2 * SSD_GROUPS * SSD_STATE),
                   spec(n, SSD_WIDTH)],
        out_shape=[jax.ShapeDtypeStruct((SSD_WIDTH, n), BF16), jax.ShapeDtypeStruct((n, LANES), F32),
                   jax.ShapeDtypeStruct((n, LANES), F32),
                   jax.ShapeDtypeStruct((n, 2 * SSD_GROUPS * SSD_STATE), F32),
                   jax.ShapeDtypeStruct((n, SSD_WIDTH), F32)],
        compiler_params=pltpu.CompilerParams(vmem_limit_bytes=VMEM_LIMIT),
        name="ssd_step_prep",
    )(*args)


def _ssd_step_body(dt_ref, dec_ref, st_ref, xt_ref, bc_ref, so_ref, y_ref):
    n = xt_ref.shape[1]
    gw = SSD_HPG * SSD_HEAD_DIM
    blk = pl.program_id(0)
    seq_id = lax.broadcasted_iota(jnp.int32, (n, SSD_STATE), 0)
    sub_id = lax.broadcasted_iota(jnp.int32, (SUBLANES, gw), 0)
    base = pl.multiple_of(blk * SUBLANES, SUBLANES)
    y_acc = [jnp.zeros((SUBLANES, gw), F32) for _ in range(SSD_GROUPS)]
    for i in range(SUBLANES):
        s = blk * SUBLANES + i
        for g in range(SSD_GROUPS):
            b_all = bc_ref[:, g * SSD_STATE:(g + 1) * SSD_STATE]
            rhs = jnp.where(seq_id == s, b_all, 0.0).astype(BF16)
            outer = _dot(xt_ref[g * gw:(g + 1) * gw, :], rhs)
            news = []
            for k in range(SSD_HPG):
                h = g * SSD_HPG + k
                new = (dec_ref[s * SSD_HEADS + h] * st_ref[i, h]
                       + dt_ref[s * SSD_HEADS + h] * outer[k * SSD_HEAD_DIM:(k + 1) * SSD_HEAD_DIM, :])
                so_ref[i, h] = new
                news.append(new)
            new_g = jnp.concatenate(news, axis=0).astype(BF16)
            c_lo = (SSD_GROUPS + g) * SSD_STATE
            c_blk = bc_ref[pl.ds(base, SUBLANES), c_lo:c_lo + SSD_STATE].astype(BF16)
            r = lax.dot_general(c_blk, new_g, (((1,), (1,)), ((), ())), preferred_element_type=F32)
            y_acc[g] = y_acc[g] + jnp.where(sub_id == i, r, 0.0)
    y_ref[...] = jnp.concatenate(y_acc, axis=-1)


def _ssd_step(dt_flat, dec_flat, state, xt, bc):
    n = state.shape[0]
    st_spec = pl.BlockSpec((SUBLANES, SSD_HEADS, SSD_HEAD_DIM, SSD_STATE), lambda i, *_: (i, 0, 0, 0))
    return pl.pallas_call(
        _ssd_step_body,
        grid_spec=pltpu.PrefetchScalarGridSpec(
            num_scalar_prefetch=2,
            grid=(n // SUBLANES,),
            in_specs=[st_spec, pl.BlockSpec(xt.shape, lambda i, *_: (0, 0)),
                      pl.BlockSpec(bc.shape, lambda i, *_: (0, 0))],
            out_specs=[st_spec, pl.BlockSpec((SUBLANES, SSD_WIDTH), lambda i, *_: (i, 0))]),
        out_shape=[jax.ShapeDtypeStruct(state.shape, F32), jax.ShapeDtypeStruct((n, SSD_WIDTH), F32)],
        compiler_params=pltpu.CompilerParams(dimension_semantics=("parallel",), vmem_limit_bytes=VMEM_LIMIT),
        name="ssd_step",
    )(dt_flat, dec_flat, state, xt, bc)


def _s5_project_in(u_b16, wb_ref, store):
    kw = 16 * S5_GROUP_CH
    nw = 16 * S5_STATE
    for j in range(S5_WIDTH // kw):
        r = _dot(u_b16[:, j * kw:(j + 1) * kw], wb_ref[j])
        store(j, r[:, :nw], r[:, nw:])


def _s5_tail(hre_of, him_of, u_f32, wcr_ref, wci_ref, d_ref, wglu_ref, bglu_ref, nrm_ref):
    cols = []
    for j in range(wcr_ref.shape[0]):
        cols.append(_dot(hre_of(j).astype(BF16), wcr_ref[j]) + _dot(him_of(j).astype(BF16), wci_ref[j]))
    return _s5_finish(cols, u_f32, d_ref, wglu_ref, bglu_ref, nrm_ref)


def _s5_finish(cols, u_f32, d_ref, wglu_ref, bglu_ref, nrm_ref):
    y = jnp.concatenate(cols, axis=-1) + d_ref[...] * u_f32
    y = jax.nn.gelu(y)
    y = y * jax.nn.sigmoid(_dot(y.astype(BF16), wglu_ref[...]) + bglu_ref[...])
    return _rms(y, nrm_ref[...])


def _s5_seq_body(u_hbm, um_ref, wb_ref, abr_ref, abi_ref, wcr_ref, wci_ref, d_ref, wglu_ref, bglu_ref, nrm_ref,
                 y_hbm, sre_ref, sim_ref, ubuf, ybuf, bu, h, in_sems, out_sems):
    j = pl.program_id(0)
    last = pl.num_programs(0) - 1
    lc, bsz = ubuf.shape[1], ubuf.shape[2]
    rows = lc * bsz
    nw = 16 * S5_STATE

    def in_copy(step, b):
        return pltpu.make_async_copy(u_hbm.at[b, pl.ds(step * lc, lc), :], ubuf.at[step % 2, :, b, :],
                                     in_sems.at[step % 2, b])

    def out_copy(step, b):
        return pltpu.make_async_copy(ybuf.at[step % 2, :, b, :], y_hbm.at[b, pl.ds(step * lc, lc), :],
                                     out_sems.at[step % 2, b])

    def project_in(u_b16, nrows):
        def store(jj, re, im):
            bu[0:nrows, jj * nw:(jj + 1) * nw] = re
            bu[0:nrows, S5_LANES + jj * nw:S5_LANES + (jj + 1) * nw] = im
        _s5_project_in(u_b16, wb_ref, store)

    def scan(nsteps):
        for k in range(S5_LANES // S5_SCAN_LANES):
            sl_r = pl.ds(k * S5_SCAN_LANES, S5_SCAN_LANES)
            sl_i = pl.ds(S5_LANES + k * S5_SCAN_LANES, S5_SCAN_LANES)
            ar = abr_ref[:, sl_r]
            ai = abi_ref[:, sl_r]

            def step(l, carry):
                hr, hi = carry
                slab = pl.ds(pl.multiple_of(l * bsz, bsz), bsz)
                nr = ar * hr - ai * hi + bu[slab, sl_r]
                ni = ar * hi + ai * hr + bu[slab, sl_i]
                bu[slab, sl_r] = nr
                bu[slab, sl_i] = ni
                return nr, ni

            hr, hi = lax.fori_loop(0, nsteps, step, (h[:, sl_r], h[:, sl_i]))
            h[:, sl_r] = hr
            h[:, sl_i] = hi

    @pl.when(j == 0)
    def _first():
        for b in range(bsz):
            in_copy(0, b).start()
        h[...] = jnp.zeros_like(h)
        project_in(um_ref[...], N_META * bsz)
        scan(N_META)

    @pl.when(j < last)
    def _prefetch():
        for b in range(bsz):
            in_copy(j + 1, b).start()

    for b in range(bsz):
        in_copy(j, b).wait()
    u2 = ubuf[j % 2].reshape(rows, S5_WIDTH)
    u_b16 = u2.astype(BF16)
    kw = 16 * S5_GROUP_CH

    def project_block(jj):
        r = _dot(u_b16[:, jj * kw:(jj + 1) * kw], wb_ref[jj])
        bu[0:rows, jj * nw:(jj + 1) * nw] = r[:, :nw]
        bu[0:rows, S5_LANES + jj * nw:S5_LANES + (jj + 1) * nw] = r[:, nw:]

    def scan_block(jj):
        for k in range(nw // S5_SCAN_LANES):
            lo = jj * nw + k * S5_SCAN_LANES
            sl_r = slice(lo, lo + S5_SCAN_LANES)
            sl_i = slice(S5_LANES + lo, S5_LANES + lo + S5_SCAN_LANES)
            ar, ai = abr_ref[:, sl_r], abi_ref[:, sl_r]
            hr, hi = h[:, sl_r], h[:, sl_i]
            for l in range(lc):
                slab = slice(l * bsz, (l + 1) * bsz)
                hr, hi = (ar * hr - ai * hi + bu[slab, sl_r], ar * hi + ai * hr + bu[slab, sl_i])
                bu[slab, sl_r] = hr
                bu[slab, sl_i] = hi
            h[:, sl_r] = hr
            h[:, sl_i] = hi

    def readout_block(jj):
        return (_dot(bu[:, jj * nw:(jj + 1) * nw].astype(BF16), wcr_ref[jj])
                + _dot(bu[:, S5_LANES + jj * nw:S5_LANES + (jj + 1) * nw].astype(BF16), wci_ref[jj]))

    n_blocks = S5_WIDTH // kw
    project_block(0)
    cols = []
    for jj in range(n_blocks):
        if jj + 1 < n_blocks:
            project_block(jj + 1)
        scan_block(jj)
        cols.append(readout_block(jj))
    y = _s5_finish(cols, u2, d_ref, wglu_ref, bglu_ref, nrm_ref)
    ybuf[j % 2] = y.reshape(lc, bsz, S5_WIDTH)
    for b in range(bsz):
        out_copy(j, b).start()

    @pl.when(j > 0)
    def _wait_previous_out():
        for b in range(bsz):
            out_copy(j - 1, b).wait()

    @pl.when(j == last)
    def _emit():
        for b in range(bsz):
            out_copy(j, b).wait()
        sre_ref[...] = h[:, 0:S5_LANES]
        sim_ref[...] = h[:, S5_LANES:]


def _s5_seq(u, um, wb, abr, abi, wcr, wci, d, wglu, bglu, nrm):
    bsz, seq, _ = u.shape
    lc = S5_TIME_TILE
    consts = (um, wb, abr, abi, wcr, wci, d, wglu, bglu, nrm)
    st = pl.BlockSpec((bsz, S5_LANES), lambda j: (0, 0))
    return pl.pallas_call(
        _s5_seq_body,
        grid=(seq // lc,),
        in_specs=[pl.BlockSpec(memory_space=pl.ANY)] + [_resident_spec(a) for a in consts],
        out_specs=[pl.BlockSpec(memory_space=pl.ANY), st, st],
        out_shape=[jax.ShapeDtypeStruct((bsz, seq, S5_WIDTH), F32),
                   jax.ShapeDtypeStruct((bsz, S5_LANES), F32), jax.ShapeDtypeStruct((bsz, S5_LANES), F32)],
        scratch_shapes=[pltpu.VMEM((2, lc, bsz, S5_WIDTH), F32), pltpu.VMEM((2, lc, bsz, S5_WIDTH), F32),
                        pltpu.VMEM((lc * bsz, 2 * S5_LANES), F32), pltpu.VMEM((bsz, 2 * S5_LANES), F32),
                        pltpu.SemaphoreType.DMA((2, bsz)), pltpu.SemaphoreType.DMA((2, bsz))],
        compiler_params=pltpu.CompilerParams(dimension_semantics=("arbitrary",), vmem_limit_bytes=VMEM_LIMIT),
        name="s5_seq",
    )(u, *consts)


def _sample_post_body(yc_ref, xs_ref, z_ref, dexp_ref, snrm_ref, u_ref, hr_ref, hi_ref, wb_ref, abr_ref, abi_ref,
                      wcr_ref, wci_ref, d_ref, wglu_ref, bglu_ref, nrm_ref,
                      yssd_ref, ys5_ref, nre_ref, nim_ref):
    z = z_ref[...]
    y = yc_ref[...] + dexp_ref[...] * xs_ref[...]
    yssd_ref[...] = _rms(y * (z * jax.nn.sigmoid(z)), snrm_ref[...]).astype(yssd_ref.dtype)

    u = u_ref[...]
    nw = 16 * S5_STATE
    ar, ai = abr_ref[...], abi_ref[...]

    def store(jj, re, im):
        sl = slice(jj * nw, (jj + 1) * nw)
        h0r, h0i = hr_ref[:, sl], hi_ref[:, sl]
        nre_ref[:, sl] = ar[:, sl] * h0r - ai[:, sl] * h0i + re
        nim_ref[:, sl] = ar[:, sl] * h0i + ai[:, sl] * h0r + im

    _s5_project_in(u.astype(BF16), wb_ref, store)
    slab = lambda ref: (lambda jj: ref[:, jj * nw:(jj + 1) * nw])
    y5 = _s5_tail(slab(nre_ref), slab(nim_ref), u, wcr_ref, wci_ref, d_ref, wglu_ref, bglu_ref, nrm_ref)
    ys5_ref[...] = y5.astype(ys5_ref.dtype)


def _sample_post(yc, xs, z, dexp, snrm, u, h0r, h0i, wb, abr1, abi1, wcr, wci, d, wglu, bglu, nrm):
    n = yc.shape[0]
    args = (yc, xs, z, dexp, snrm, u, h0r, h0i, wb, abr1, abi1, wcr, wci, d, wglu, bglu, nrm)
    spec = lambda w: pl.BlockSpec((n, w), lambda: (0, 0))
    return pl.pallas_call(
        _sample_post_body,
        in_specs=[_full_spec(a) for a in args],
        out_specs=[spec(SSD_WIDTH), spec(S5_WIDTH), spec(S5_LANES), spec(S5_LANES)],
        out_shape=[jax.ShapeDtypeStruct((n, SSD_WIDTH), BF16), jax.ShapeDtypeStruct((n, S5_WIDTH), BF16),
                   jax.ShapeDtypeStruct((n, S5_LANES), F32), jax.ShapeDtypeStruct((n, S5_LANES), F32)],
        compiler_params=pltpu.CompilerParams(vmem_limit_bytes=VMEM_LIMIT),
        name="sample_post",
    )(*args)


def _mix_route_body(n_blocks, n_sorted, xp_ref, ysp_ref, y5p_ref, xs_ref, yss_ref, y5s_ref, *refs):
    consts = refs[:6]
    x1_ref, xn_hbm, rt_ref, pos_ref, meta_ref, carry, fields, xbuf, sems = refs[6:]
    i = pl.program_id(0)
    tm, n_sample = xp_ref.shape[0], xs_ref.shape[0]
    col0 = pl.multiple_of(i * tm, LANES)

    def xn_copy(step, rows, j):
        return pltpu.make_async_copy(xbuf.at[step % 2, pl.ds(0, rows), pl.ds(j * LANES, LANES)],
                                     xn_hbm.at[pl.ds(step * tm, rows), j, :], sems.at[step % 2, j])

    @pl.when(i == 0)
    def _init():
        carry[...] = jnp.zeros_like(carry)

    @pl.when(i < n_blocks)
    def _prompt_rows():
        _mix_route_compute(xp_ref, ysp_ref, y5p_ref, *consts, x1_ref, rt_ref, carry, xbuf.at[i % 2], fields, col0)
        for j in range(PACK_ROWS):
            xn_copy(i, tm, j).start()

    @pl.when(i == n_blocks)
    def _sample_rows():
        _mix_route_compute(xs_ref, yss_ref, y5s_ref, *consts, x1_ref, rt_ref, carry, xbuf.at[i % 2], fields, col0)
        for j in range(PACK_ROWS):
            xn_copy(i, n_sample, j).start()
        _route_layout(carry, fields, pos_ref, meta_ref, n_sorted)
        for j in range(PACK_ROWS):
            xn_copy(i, n_sample, j).wait()

    @pl.when(i > 0)
    def _wait_previous_rows():
        for j in range(PACK_ROWS):
            xn_copy(i - 1, tm, j).wait()


def _route_layout(carry, fields, pos_ref, meta_ref, n_sorted):
    counts = carry[...]
    tiles_per = jnp.floor((counts + (MOE_TILE - 1)) * (1.0 / MOE_TILE))
    upto = lax.broadcasted_iota(jnp.int32, (LANES, LANES), 0) <= lax.broadcasted_iota(jnp.int32, (LANES, LANES), 1)
    tile_end = _dot(tiles_per.astype(BF16), upto.astype(BF16))
    pstart = (tile_end - tiles_per) * MOE_TILE
    n_used = tile_end[:, MOE_EXPERTS - 1:MOE_EXPERTS]

    f = fields[...]
    first_row = jnp.zeros_like(f)
    tile_id = jnp.minimum(lax.broadcasted_iota(jnp.int32, meta_ref.shape, 1).astype(F32), n_used - 1.0)
    tile_expert = jnp.zeros(meta_ref.shape, F32)
    for e in range(MOE_EXPERTS):
        first_row = first_row + jnp.where(f == float(e), pstart[:, e:e + 1], 0.0)
        tile_expert = tile_expert + jnp.where(tile_end[:, e:e + 1] <= tile_id, 1.0, 0.0)
    pos = first_row + pltpu.roll(f, shift=4, axis=0)
    pos_ref[...] = jnp.clip(pos, 0.0, n_sorted - 1.0).astype(jnp.int32)
    is_row0 = lax.broadcasted_iota(jnp.int32, meta_ref.shape, 0) == 0
    meta_ref[...] = jnp.where(is_row0, tile_expert, n_used).astype(jnp.int32)


def _mix_route_compute(x_ref, ys_ref, y5_ref, wa_ref, wb_ref, nf_ref, wrh_ref, wrl_ref, br_ref,
                       x1_ref, rt_ref, carry, xn_buf, fields, col0):
    rows = x_ref.shape[0]
    x1 = x_ref[...] + _dot(ys_ref[...], wa_ref[...]) + _dot(y5_ref[...].astype(BF16), wb_ref[...])
    x1_ref[0:rows, :] = x1
    xn = _rms(x1, nf_ref[...])
    xn_buf[0:rows, :] = _pack_bf16_pairs(xn)

    xh = xn.astype(BF16)
    xl = (xn - xh.astype(F32)).astype(BF16)
    logits = _dot(xh, wrh_ref[...]) + _dot(xl, wrh_ref[...]) + _dot(xh, wrl_ref[...]) + br_ref[...]
    tm = logits.shape[0]
    lane = lax.broadcasted_iota(jnp.int32, logits.shape, 1).astype(F32)
    neg = -jnp.inf
    big = float(LANES)

    def first_max(v):
        m = jnp.max(v, axis=-1, keepdims=True)
        return m, jnp.min(jnp.where(v == m, lane, big), axis=-1, keepdims=True)

    coarse = lane < MOE_GROUPS
    mc, gsel = first_max(jnp.where(coarse, logits, neg))
    psel = 1.0 / jnp.sum(jnp.where(coarse, jnp.exp(logits - mc), 0.0), axis=-1, keepdims=True)
    lo = MOE_GROUPS + MOE_EPG * gsel
    lf = jnp.where((lane >= lo) & (lane < lo + MOE_EPG), logits, neg)
    m1, i1 = first_max(lf)
    m2, i2 = first_max(jnp.where(lane == i1, neg, lf))
    e2 = jnp.exp(m2 - m1)
    g1 = psel / (1.0 + e2)
    g2 = psel * e2 / (1.0 + e2)
    e_a, e_b = i1 - MOE_GROUPS, i2 - MOE_GROUPS

    pick_a, pick_b = lane == e_a, lane == e_b
    picks = jnp.where(pick_a | pick_b, 1.0, 0.0)
    earlier = lax.broadcasted_iota(jnp.int32, (tm, tm), 0) > lax.broadcasted_iota(jnp.int32, (tm, tm), 1)
    prior = _dot(earlier.astype(BF16), picks.astype(BF16)) + carry[...]
    rank_a = jnp.sum(jnp.where(pick_a, prior, 0.0), axis=-1, keepdims=True)
    rank_b = jnp.sum(jnp.where(pick_b, prior, 0.0), axis=-1, keepdims=True)
    carry[...] = prior[tm - 1:tm, :] + picks[tm - 1:tm, :]

    out = jnp.zeros_like(logits)
    for k, v in enumerate((e_a, e_b, g1, g2, rank_a, rank_b)):
        out = jnp.where(lane == float(k), v, out)
    rt_ref[0:rows, :] = out
    fields[:, pl.ds(col0, rows)] = out.T[0:SUBLANES, :]


def _mix_route(prompt, sample, consts, tm, n_tiles):
    n_prompt, n_sample = prompt[0].shape[0], sample[0].shape[0]
    assert n_prompt % tm == 0 and n_sample <= tm
    n_blocks = n_prompt // tm
    total_rows = n_prompt + n_sample
    row = lambda w: pl.BlockSpec((tm, w), lambda i: (jnp.minimum(i, n_blocks - 1), 0))
    out_row = lambda w: pl.BlockSpec((tm, w), lambda i: (i, 0))
    assert total_rows % LANES == 0 and n_tiles <= 2 * LANES
    whole = lambda shape: pl.BlockSpec(shape, lambda i: (0, 0))
    return pl.pallas_call(
        functools.partial(_mix_route_body, n_blocks, n_tiles * MOE_TILE),
        grid=(n_blocks + 1,),
        in_specs=([row(D_MODEL), row(SSD_WIDTH), row(S5_WIDTH)] + [_full_spec(a) for a in sample]
                  + [_full_spec(a) for a in consts]),
        out_specs=[out_row(D_MODEL), pl.BlockSpec(memory_space=pl.ANY), out_row(LANES),
                   whole((SUBLANES, total_rows)), whole((SUBLANES, 2 * LANES))],
        out_shape=[jax.ShapeDtypeStruct((total_rows, D_MODEL), F32),
                   jax.ShapeDtypeStruct((total_rows, PACK_ROWS, LANES), jnp.uint32),
                   jax.ShapeDtypeStruct((total_rows, LANES), F32),
                   jax.ShapeDtypeStruct((SUBLANES, total_rows), jnp.int32),
                   jax.ShapeDtypeStruct((SUBLANES, 2 * LANES), jnp.int32)],
        scratch_shapes=[pltpu.VMEM((1, LANES), F32), pltpu.VMEM((SUBLANES, total_rows), F32),
                        pltpu.VMEM((2, tm, D_MODEL // 2), jnp.uint32), pltpu.SemaphoreType.DMA((2, PACK_ROWS))],
        compiler_params=pltpu.CompilerParams(dimension_semantics=("arbitrary",), vmem_limit_bytes=VMEM_LIMIT),
        name="mix_route",
    )(*prompt, *sample, *consts)


def _sc_mesh():
    return plsc.VectorSubcoreMesh(core_axis_name="c", subcore_axis_name="s")


def _sc_worker():
    return lax.axis_index("s") * SC_CORES + lax.axis_index("c")


def _sc_dispatch(xn, pos_a, pos_b, n_rows):
    n_tok = xn.shape[0]
    ch = SC_DISPATCH_ROWS
    n_chunks = n_tok // ch
    assert n_tok % ch == 0 and n_chunks >= SC_WORKERS
    max_mine = -(-n_chunks // SC_WORKERS)
    row_shape, dtype = xn.shape[1:], xn.dtype
    stage = [pltpu.VMEM((ch,), jnp.int32), pltpu.VMEM((ch,), jnp.int32), pltpu.VMEM((ch,) + row_shape, dtype),
             pltpu.SemaphoreType.DMA]

    @functools.partial(
        pl.kernel, mesh=_sc_mesh(),
        out_type=jax.ShapeDtypeStruct((n_rows,) + row_shape, dtype),
        scratch_types=stage + stage + [pltpu.SemaphoreType.DMA])
    def push(xn_hbm, pa_hbm, pb_hbm, xs_hbm, ia0, ib0, rows0, lsem0, ia1, ib1, rows1, lsem1, ssem):
        wid = _sc_worker()
        mine = (n_chunks - wid + SC_WORKERS - 1) // SC_WORKERS
        bufs = ((ia0, ib0, rows0, lsem0), (ia1, ib1, rows1, lsem1))

        def loads(t, b):
            ia, ib, rows, sem = bufs[b]
            off = pl.multiple_of((wid + t * SC_WORKERS) * ch, ch)
            return (pltpu.make_async_copy(pa_hbm.at[pl.ds(off, ch)], ia, sem),
                    pltpu.make_async_copy(pb_hbm.at[pl.ds(off, ch)], ib, sem),
                    pltpu.make_async_copy(xn_hbm.at[pl.ds(off, ch)], rows, sem))

        def stage_in(t, b):
            for c in loads(t, b):
                c.start()

        def scatter(t, b):
            ia, ib, rows, _ = bufs[b]
            for c in loads(t, b):
                c.wait()
            first = pltpu.async_copy(rows, xs_hbm.at[ia], ssem)
            second = pltpu.async_copy(rows, xs_hbm.at[ib], ssem)
            first.wait()
            second.wait()

        stage_in(0, 0)

        @pl.loop(0, (max_mine + 1) // 2)
        def _(p):
            t = 2 * p

            @pl.when(t + 1 < mine)
            def _():
                stage_in(t + 1, 1)

            @pl.when(t < mine)
            def _():
                scatter(t, 0)

            @pl.when(t + 2 < mine)
            def _():
                stage_in(t + 2, 0)

            @pl.when(t + 1 < mine)
            def _():
                scatter(t + 1, 1)

    return push(xn, pos_a, pos_b)


def _sc_collect(ysorted, pos_flat, ch):
    n_pick = pos_flat.shape[0]
    per_worker = n_pick // SC_WORKERS
    n_chunks = per_worker // ch
    assert n_pick % SC_WORKERS == 0 and per_worker % ch == 0
    row_shape, dtype = ysorted.shape[1:], ysorted.dtype

    @functools.partial(
        pl.kernel, mesh=_sc_mesh(),
        out_type=jax.ShapeDtypeStruct((n_pick,) + row_shape, dtype),
        scratch_types=[pltpu.VMEM((ch,), jnp.int32), pltpu.VMEM((ch,), jnp.int32),
                       pltpu.VMEM((ch,) + row_shape, dtype), pltpu.VMEM((ch,) + row_shape, dtype),
                       pltpu.SemaphoreType.DMA, pltpu.SemaphoreType.DMA])
    def pull(ys_hbm, pos_hbm, out_hbm, idx0, idx1, rows0, rows1, sem0, sem1):
        base = _sc_worker() * per_worker
        bufs = ((idx0, rows0, sem0), (idx1, rows1, sem1))

        def offset(j):
            return pl.multiple_of(base + j * ch, SUBLANES)

        def fetch(j, b):
            idx, rows, sem = bufs[b]
            pltpu.sync_copy(pos_hbm.at[pl.ds(offset(j), ch)], idx)
            pltpu.async_copy(ys_hbm.at[idx], rows, sem)

        def flush(j, b):
            idx, rows, sem = bufs[b]
            pltpu.make_async_copy(ys_hbm.at[idx], rows, sem).wait()
            pltpu.sync_copy(rows, out_hbm.at[pl.ds(offset(j), ch)])

        fetch(0, 0)

        @pl.loop(0, n_chunks // 2)
        def _(p):
            j = 2 * p
            fetch(j + 1, 1)
            flush(j, 0)

            @pl.when(j + 2 < n_chunks)
            def _():
                fetch(j + 2, 0)

            flush(j + 1, 1)

        if n_chunks % 2:
            flush(n_chunks - 1, 0)

    return pull(ysorted, pos_flat)


def _moe_ffn_body(*refs):
    for k in range(MOE_TILES_PER_STEP):
        _moe_ffn_tile(k, *refs)


def _moe_ffn_tile(k, te_ref, nused_ref, x_ref, wg_hbm, wu_hbm, wd_hbm, y_ref,
                  wg_f32, wu_f32, wd_f32, wgb, wub, wdb, slot_ref, sems):
    i = pl.program_id(0) * MOE_TILES_PER_STEP + k
    n_used = nused_ref[0]
    window = pl.ds(k * MOE_TILE * PACK_ROWS, MOE_TILE * PACK_ROWS)
    x_ref, y_ref = x_ref.at[window, :], y_ref.at[window, :]

    def fetch(expert, slot):
        return (pltpu.make_async_copy(wg_hbm.at[expert], wg_f32.at[slot], sems.at[slot, 0]),
                pltpu.make_async_copy(wu_hbm.at[expert], wu_f32.at[slot], sems.at[slot, 1]),
                pltpu.make_async_copy(wd_hbm.at[expert], wd_f32.at[slot], sems.at[slot, 2]))

    @pl.when(i >= n_used)
    def _unused_tile():
        y_ref[...] = jnp.zeros_like(y_ref)

    @pl.when(i < n_used)
    def _tile():
        expert = te_ref[i]

        @pl.when(i == 0)
        def _first_fetch():
            slot_ref[0] = 0
            for c in fetch(expert, 0):
                c.start()

        @pl.when((i == 0) | (expert != te_ref[jnp.maximum(i - 1, 0)]))
        def _new_expert():
            slot = slot_ref[0]
            nxt = lax.while_loop(lambda k: (k < n_used) & (te_ref[jnp.minimum(k, n_used - 1)] == expert),
                                 lambda k: k + 1, i + 1)

            @pl.when(nxt < n_used)
            def _prefetch():
                for c in fetch(te_ref[jnp.minimum(nxt, n_used - 1)], 1 - slot):
                    c.start()

            for c in fetch(expert, slot):
                c.wait()
            wgb[...] = wg_f32[slot].astype(BF16)
            wub[...] = wu_f32[slot].astype(BF16)
            wdb[...] = wd_f32[slot].astype(BF16)
            slot_ref[0] = 1 - slot

        x = _unpack_bf16_pairs(x_ref, MOE_TILE).astype(BF16)
        gate = _dot(x, wgb[...])
        hmid = (gate * jax.nn.sigmoid(gate)) * _dot(x, wub[...])
        y = _dot(hmid.astype(BF16), wdb[...])
        packed = _pack_bf16_pairs(y)
        for j in range(PACK_ROWS):
            y_ref[pl.ds(j, MOE_TILE, stride=PACK_ROWS), :] = packed[:, j * LANES:(j + 1) * LANES]


def _moe_ffn(tile_expert, n_used, xsorted, w_gate, w_up, w_down):
    n_tiles = tile_expert.shape[0]
    per_step = MOE_TILES_PER_STEP
    assert n_tiles % per_step == 0
    hbm = pl.BlockSpec(memory_space=pl.ANY)
    tile = lambda imap: pl.BlockSpec((per_step * MOE_TILE * PACK_ROWS, LANES), imap)
    up_shape, down_shape = (D_MODEL, MOE_D_FF), (MOE_D_FF, D_MODEL)
    return pl.pallas_call(
        _moe_ffn_body,
        grid_spec=pltpu.PrefetchScalarGridSpec(
            num_scalar_prefetch=2,
            grid=(n_tiles // per_step,),
            in_specs=[tile(lambda i, te, nu: (jnp.clip(i, 0, jnp.maximum(nu[0] - 1, 0) // per_step), 0)),
                      hbm, hbm, hbm],
            out_specs=tile(lambda i, te, nu: (i, 0)),
            scratch_shapes=[pltpu.VMEM((2,) + up_shape, F32), pltpu.VMEM((2,) + up_shape, F32),
                            pltpu.VMEM((2,) + down_shape, F32),
                            pltpu.VMEM(up_shape, BF16), pltpu.VMEM(up_shape, BF16), pltpu.VMEM(down_shape, BF16),
                            pltpu.SMEM((1,), jnp.int32), pltpu.SemaphoreType.DMA((2, 3))]),
        out_shape=jax.ShapeDtypeStruct((n_tiles * MOE_TILE * PACK_ROWS, LANES), jnp.uint32),
        compiler_params=pltpu.CompilerParams(dimension_semantics=("arbitrary",), vmem_limit_bytes=VMEM_LIMIT),
        name="moe_ffn",
    )(tile_expert, n_used, xsorted, w_gate, w_up, w_down)


def _combine_body(x1_ref, rt_ref, ya_ref, yb_ref, nf_ref, *rest):
    out_ref = rest[-1]
    rt = rt_ref[...]
    x1 = x1_ref[...]
    tm = x1.shape[0]

    x2 = (x1 + rt[:, 2:3] * _unpack_bf16_pairs(ya_ref.at[0], tm)
          + rt[:, 3:4] * _unpack_bf16_pairs(yb_ref.at[0], tm))
    out_ref[...] = _rms(x2, nf_ref[...])


def _combine(x1, rt, y_picks, nf, tm, rows, x_block, y_block, out_rows, out_block, out_buf=None):
    row = lambda w: pl.BlockSpec((tm, w), lambda i: (i + x_block, 0))
    pick = lambda k: pl.BlockSpec((1, tm * PACK_ROWS, LANES), lambda i: (k, i + y_block, 0))
    in_specs = [row(D_MODEL), row(LANES), pick(0), pick(1), pl.BlockSpec((1, D_MODEL), lambda i: (0, 0))]
    args = [x1, rt, y_picks, y_picks, nf]
    aliases = {}
    if out_buf is not None:
        in_specs.append(pl.BlockSpec(memory_space=pl.ANY))
        aliases[len(args)] = 0
        args.append(out_buf)
    return pl.pallas_call(
        _combine_body,
        grid=(rows // tm,),
        in_specs=in_specs,
        out_specs=pl.BlockSpec((tm, D_MODEL), lambda i: (i + out_block, 0)),
        out_shape=jax.ShapeDtypeStruct((out_rows, D_MODEL), F32),
        input_output_aliases=aliases,
        compiler_params=pltpu.CompilerParams(dimension_semantics=("parallel",), vmem_limit_bytes=VMEM_LIMIT),
        name="moe_combine",
    )(*args)


def _s5_tables(a_re, a_im, log_dt, b_re, b_im, c_re, c_im):
    dt = jnp.exp(log_dt)[:, None]
    mag = jnp.exp(a_re * dt)
    ab_re = mag * jnp.cos(a_im * dt)
    ab_im = mag * jnp.sin(a_im * dt)
    den = a_re * a_re + a_im * a_im
    nr = ab_re - 1.0
    q_re = (nr * a_re + ab_im * a_im) / den
    q_im = (ab_im * a_re - nr * a_im) / den
    bb_re = q_re[..., None] * b_re - q_im[..., None] * b_im
    bb_im = q_re[..., None] * b_im + q_im[..., None] * b_re
    nblk = S5_GROUPS // 16
    kw, nw = 16 * S5_GROUP_CH, 16 * S5_STATE
    same_group = (jnp.arange(kw)[:, None] // S5_GROUP_CH) == (jnp.arange(nw)[None, :] // S5_STATE)

    def in_map(bb):
        rows = bb.reshape(nblk, 16, S5_STATE, S5_GROUP_CH).transpose(0, 1, 3, 2).reshape(nblk, kw, S5_STATE)
        return jnp.where(same_group, jnp.tile(rows, (1, 1, 16)), 0.0)

    def out_map(cc):
        cols = cc.reshape(nblk, 16, S5_GROUP_CH, S5_STATE).transpose(0, 3, 1, 2).reshape(nblk, S5_STATE, kw)
        return jnp.where(same_group.T, jnp.tile(cols, (1, 16, 1)), 0.0)

    wb = jnp.concatenate([in_map(bb_re), in_map(bb_im)], axis=-1).astype(BF16)
    return (wb, ab_re.reshape(1, S5_LANES), ab_im.reshape(1, S5_LANES),
            out_map(c_re).astype(BF16), out_map(-c_im).astype(BF16))


def kernel(x_prompt, x_sample, state_ssd_conv, state_ssd_ssm, state_s5_re, state_s5_im, meta_tokens, norm_mix, w_in, conv_w, conv_b, dt_bias, a_log, d_ssd, ssd_norm, s5_a_re, s5_a_im, s5_log_dt, s5_b_re, s5_b_im, s5_c_re, s5_c_im, s5_d, w_glu, b_glu, s5_norm, w_out, norm_ffn, router_coarse_w, router_coarse_b, router_fine_w, router_fine_b, w_gate, w_up, w_down, norm_final):
    bp, seq, _ = x_prompt.shape
    bs = x_sample.shape[0]
    n_prompt = bp * seq
    n_tok = n_prompt + bs
    row2 = lambda v: v.reshape(1, -1)
    pad_heads = lambda v: jnp.pad(v, (0, LANES - SSD_HEADS)).reshape(1, LANES)

    w = w_in[0]
    o1, o2, o3 = SSD_WIDTH, SSD_WIDTH + SSD_CONV_DIM, SSD_WIDTH + SSD_CONV_DIM + SSD_HEADS
    wz, wx, wu = w[:, :o1].astype(BF16), w[:, o1:o2].astype(BF16), w[:, o3:].astype(BF16)
    wdt = jnp.pad(w[:, o2:o3], ((0, 0), (0, LANES - SSD_HEADS))).astype(BF16)
    g_mix = row2(norm_mix[0])
    cw, cb = conv_w[0], row2(conv_b[0])
    dtb, alog = pad_heads(dt_bias[0]), pad_heads(a_log[0])
    dexp = row2(jnp.repeat(d_ssd[0], SSD_HEAD_DIM))
    snrm = row2(ssd_norm[0])
    eexp = (jnp.arange(LANES)[:, None] == (jnp.arange(SSD_WIDTH) // SSD_HEAD_DIM)[None, :]).astype(BF16)
    wb5, ab_re, ab_im, wcr, wci = _s5_tables(s5_a_re[0], s5_a_im[0], s5_log_dt[0], s5_b_re[0], s5_b_im[0],
                                             s5_c_re[0], s5_c_im[0])
    d5, wglu, bglu, nrm5 = row2(s5_d[0]), w_glu[0].astype(BF16), row2(b_glu[0]), row2(s5_norm[0])
    wo_a, wo_b = w_out[0][:SSD_WIDTH].astype(BF16), w_out[0][SSD_WIDTH:].astype(BF16)
    w_r = jnp.concatenate([router_coarse_w[0], router_fine_w[0].transpose(1, 0, 2).reshape(D_MODEL, MOE_EXPERTS)], axis=1)
    w_r = jnp.pad(w_r, ((0, 0), (0, LANES - w_r.shape[1])))
    wrh = w_r.astype(BF16)
    wrl = (w_r - wrh.astype(F32)).astype(BF16)
    b_r = jnp.concatenate([router_coarse_b[0], router_fine_b[0].reshape(-1)])
    b_r = jnp.pad(b_r, (0, LANES - b_r.shape[0])).reshape(1, LANES)

    zp, xbcp, dtp, up = _in_proj(x_prompt.reshape(n_prompt, D_MODEL), g_mix, wz, wx, wdt, wu, TOK_TILE, BF16, F32)
    xsm = jnp.concatenate([x_sample.reshape(bs, D_MODEL), meta_tokens], axis=0)
    zs, xbcs, dts, us = _in_proj(xsm, g_mix, wz, wx, wdt, wu, xsm.shape[0], F32, F32)

    front = SSD_CHUNK - N_META
    padf = lambda a: jnp.pad(a[bs:], ((front, 0), (0, 0)))[None]
    gw = SSD_HPG * SSD_HEAD_DIM
    ssd_consts = (cw, cb, dtb, alog, dexp, snrm, eexp)
    _, ctail_m, _, ht_m = _ssd_chunked(
        padf(xbcs).astype(BF16), padf(dts), jnp.zeros((1, SSD_CHUNK, SSD_WIDTH), F32),
        jnp.zeros((1, SUBLANES, SSD_CONV_DIM), F32), jnp.zeros((1, SSD_GROUPS, SSD_STATE, gw), F32),
        *ssd_consts, mask_rows=front)
    y_ssd_p, ctail_p, ssm_p, _ = _ssd_chunked(
        xbcp.reshape(bp, seq, SSD_CONV_DIM), dtp.reshape(bp, seq, LANES), zp.reshape(bp, seq, SSD_WIDTH),
        ctail_m, ht_m, *ssd_consts, mask_rows=0)

    abr8, abi8 = jnp.broadcast_to(ab_re, (bp, S5_LANES)), jnp.broadcast_to(ab_im, (bp, S5_LANES))
    um8 = jnp.repeat(us[bs:], bp, axis=0).astype(BF16)
    y_s5_p, s5re_p, s5im_p = _s5_seq(up.reshape(bp, seq, S5_WIDTH), um8, wb5, abr8, abi8,
                                     wcr, wci, d5, wglu, bglu, nrm5)

    cst = state_ssd_conv[0]
    xt_s, dt_s, dec_s, bc, xs_s = _ssd_step_prep(xbcs[:bs], cst[:, 0], cst[:, 1], cst[:, 2], dts[:bs],
                                                 cw, cb, dtb, alog)
    ssm_s, y_core = _ssd_step(dt_s[:, :SSD_HEADS].reshape(-1), dec_s[:, :SSD_HEADS].reshape(-1),
                              state_ssd_ssm[0], xt_s, bc)
    y_ssd_s, y_s5_s, s5re_s, s5im_s = _sample_post(
        y_core, xs_s, zs[:bs], dexp, snrm, us[:bs], state_s5_re[0].reshape(bs, S5_LANES),
        state_s5_im[0].reshape(bs, S5_LANES), wb5, ab_re, ab_im, wcr, wci, d5, wglu, bglu, nrm5)

    route_consts = (wo_a, wo_b, row2(norm_ffn[0]), wrh, wrl, b_r)
    n_tiles = -(-2 * n_tok // MOE_TILE) + MOE_EXPERTS
    n_tiles = -(-n_tiles // MOE_TILES_PER_STEP) * MOE_TILES_PER_STEP
    x1, xn, rt, pos, meta = _mix_route(
        (x_prompt.reshape(n_prompt, D_MODEL), y_ssd_p.reshape(n_prompt, SSD_WIDTH), y_s5_p.reshape(n_prompt, S5_WIDTH)),
        (x_sample.reshape(bs, D_MODEL), y_ssd_s, y_s5_s), route_consts, TOK_TILE, n_tiles)

    pos_a, pos_b = pos[0], pos[1]
    tile_expert, n_used = meta[0, :n_tiles], meta[1, :1]
    xsorted = _sc_dispatch(xn, pos_a, pos_b, n_tiles * MOE_TILE)
    ysorted = _moe_ffn(tile_expert, n_used, xsorted.reshape(-1, LANES), w_gate[0], w_up[0], w_down[0])
    nfin = row2(norm_final)

    half = n_prompt // 2

    def collect(lo, hi, ch):
        picks = jnp.concatenate([pos_a[lo:hi], pos_b[lo:hi]])
        packed_rows = ysorted.reshape(-1, PACK_ROWS, LANES)
        return _sc_collect(packed_rows, picks, ch).reshape(2, (hi - lo) * PACK_ROWS, LANES)

    picks_1 = collect(0, half, SC_COLLECT_ROWS[0])
    picks_2 = collect(half, n_tok, SC_COLLECT_ROWS[1])
    blocks = half // MOE_TILE
    y_p = _combine(x1, rt, picks_1, nfin, MOE_TILE, half, 0, 0, n_prompt, 0)
    y_p = _combine(x1, rt, picks_2, nfin, MOE_TILE, half, blocks, 0, n_prompt, blocks, out_buf=y_p)
    y_s = _combine(x1, rt, picks_2, nfin, bs, bs, n_prompt // bs, half // bs, bs, 0)

    s5_state = lambda a, b: a.reshape(1, b, S5_GROUPS, S5_STATE)
    new_conv_s = jnp.stack([cst[:, 1], cst[:, 2], xbcs[:bs]], axis=1)[None]
    return (y_p.reshape(bp, seq, D_MODEL), y_s.reshape(bs, 1, D_MODEL),
            ctail_p[:, SUBLANES - (SSD_CONV - 1):][None], ssm_p[None], s5_state(s5re_p, bp), s5_state(s5im_p, bp),
            new_conv_s, ssm_s[None], s5_state(s5re_s, bs), s5_state(s5im_s, bs))
```

```python
import functools

import jax
import jax.numpy as jnp
from jax import lax
from jax.experimental import pallas as pl
from jax.experimental.pallas import tpu as pltpu
from jax.experimental.pallas import tpu_sc as plsc

F32, BF16 = jnp.float32, jnp.bfloat16

D_MODEL = 1024
N_META = 16
SSD_WIDTH = 1024
SSD_HEAD_DIM = 64
SSD_HEADS = 16
SSD_GROUPS = 2
SSD_HPG = SSD_HEADS // SSD_GROUPS
SSD_STATE = 128
SSD_CONV = 4
SSD_CHUNK = 128
SSD_CONV_DIM = SSD_WIDTH + 2 * SSD_GROUPS * SSD_STATE
S5_WIDTH = 1024
S5_GROUP_CH = 16
S5_GROUPS = 64
S5_STATE = 64
S5_LANES = S5_GROUPS * S5_STATE
MOE_GROUPS = 4
MOE_EPG = 8
MOE_EXPERTS = MOE_GROUPS * MOE_EPG
MOE_D_FF = 512
EPS = 1e-6

LANES = 128
SUBLANES = 8
VMEM_LIMIT = 56 * 1024 * 1024

SSD_CHUNKS_PER_STEP = 4
S5_TIME_TILE = 64
S5_SCAN_LANES = 512
MOE_TILE = 256
MOE_TILES_PER_STEP = 4
SLAB_ROWS = D_MODEL // LANES
PACK_ROWS = SLAB_ROWS // 2
SC_CORES = 2
SC_SUBCORES = 16
SC_WORKERS = SC_CORES * SC_SUBCORES
SC_DISPATCH_ROWS = 64
SC_COLLECT_ROWS = (64, 104)
TOK_TILE = 512


def _dot(a, b):
    return jnp.dot(a, b, preferred_element_type=F32)


def _rms(x, g):
    return x * lax.rsqrt(jnp.mean(x * x, axis=-1, keepdims=True) + EPS) * g


def _softplus(x):
    return jnp.maximum(x, 0.0) + jnp.log1p(jnp.exp(-jnp.abs(x)))


def _split3(x):
    hi = x.astype(BF16)
    r = x - hi.astype(F32)
    mid = r.astype(BF16)
    lo = (r - mid.astype(F32)).astype(BF16)
    return hi, mid, lo


def _dot3(x, w):
    hi, mid, lo = _split3(x)
    return _dot(hi, w) + _dot(mid, w) + _dot(lo, w)


def _dot3_left(w, x):
    hi, mid, lo = _split3(x)
    return _dot(w, hi) + _dot(w, mid) + _dot(w, lo)


def _pack_bf16_pairs(x):
    bits = pltpu.bitcast(x.astype(BF16).astype(F32), jnp.uint32)
    half = x.shape[1] // 2
    return (bits[:, :half] & jnp.uint32(0xFFFF0000)) | (bits[:, half:] >> jnp.uint32(16))


def _unpack_bf16_pairs(ref, rows):
    words = [ref[pl.ds(j, rows, stride=PACK_ROWS), :] for j in range(PACK_ROWS)]
    high = [pltpu.bitcast(w & jnp.uint32(0xFFFF0000), F32) for w in words]
    low = [pltpu.bitcast(w << jnp.uint32(16), F32) for w in words]
    return jnp.concatenate(high + low, axis=-1)


def _full_spec(a):
    nd = a.ndim
    return pl.BlockSpec(a.shape, lambda *_: (0,) * nd)


def _resident_spec(a):
    nd = a.ndim
    return pl.BlockSpec(a.shape, lambda *_: (0,) * nd, pipeline_mode=pl.Buffered(1))


def _in_proj_body(x_ref, g_ref, wz_ref, wx_ref, wdt_ref, wu_ref, z_ref, xbc_ref, dt_ref, u_ref):
    xb = _rms(x_ref[...], g_ref[...]).astype(BF16)
    z_ref[...] = _dot(xb, wz_ref[...]).astype(z_ref.dtype)
    xbc_ref[...] = _dot(xb, wx_ref[...]).astype(xbc_ref.dtype)
    dt_ref[...] = _dot(xb, wdt_ref[...])
    u_ref[...] = _dot(xb, wu_ref[...]).astype(u_ref.dtype)


def _in_proj(x2d, g, wz, wx, wdt, wu, tm, act_dtype, u_dtype):
    rows = x2d.shape[0]
    row = lambda w: pl.BlockSpec((tm, w), lambda i: (i, 0))
    return pl.pallas_call(
        _in_proj_body,
        grid=(rows // tm,),
        in_specs=[row(D_MODEL), _full_spec(g), _full_spec(wz), _full_spec(wx), _full_spec(wdt), _full_spec(wu)],
        out_specs=[row(SSD_WIDTH), row(SSD_CONV_DIM), row(LANES), row(S5_WIDTH)],
        out_shape=[jax.ShapeDtypeStruct((rows, SSD_WIDTH), act_dtype),
                   jax.ShapeDtypeStruct((rows, SSD_CONV_DIM), act_dtype),
                   jax.ShapeDtypeStruct((rows, LANES), F32),
                   jax.ShapeDtypeStruct((rows, S5_WIDTH), u_dtype)],
        compiler_params=pltpu.CompilerParams(dimension_semantics=("parallel",), vmem_limit_bytes=VMEM_LIMIT),
        name="in_proj",
    )(x2d, g, wz, wx, wdt, wu)


def _ssd_body(mask_rows, per_step, *refs):
    for k in range(per_step):
        _ssd_chunk(mask_rows, per_step, k, *refs)


def _ssd_chunk(mask_rows, per_step, k, xbc_ref, dt_ref, z_ref, cinit_ref, hinit_ref, cw_ref, cb_ref, dtb_ref,
               alog_ref, dexp_ref, nrm_ref, eexp_ref, y_ref, ctail_ref, st_ref, hto_ref, xwin, hT):
    L = SSD_CHUNK
    c = pl.program_id(1) * per_step + k
    n_chunks = pl.num_programs(1) * per_step
    window = pl.ds(k * L, L)
    xbc_ref, dt_ref, z_ref, y_ref = (r.at[:, window, :] for r in (xbc_ref, dt_ref, z_ref, y_ref))

    @pl.when(c == 0)
    def _init():
        xwin[...] = cinit_ref[0]
        hT[...] = hinit_ref[0]

    x_b = xbc_ref[0]
    x_f = x_b.astype(F32)
    taps = SSD_CONV - 1
    m_i = lax.broadcasted_iota(jnp.int32, (taps * L, L), 0)
    r_i = lax.broadcasted_iota(jnp.int32, (taps * L, L), 1)
    shift = (r_i + (taps - m_i // L) == m_i % L).astype(BF16)
    shifted = _dot(shift, x_b)
    acc = cb_ref[...] + x_f * cw_ref[taps:taps + 1, :]
    for k in range(taps):
        acc = acc + shifted[k * L:(k + 1) * L, :] * cw_ref[k:k + 1, :]
    joint = jnp.concatenate([xwin[...], x_f[0:SUBLANES, :]], axis=0)
    row8 = lax.broadcasted_iota(jnp.int32, (SUBLANES, 1), 0)
    head = acc[0:SUBLANES, :]
    for k in range(taps):
        d = taps - k
        head = head + jnp.where(row8 < d, joint[SUBLANES - d:2 * SUBLANES - d, :], 0.0) * cw_ref[k:k + 1, :]
    acc = jnp.concatenate([head, acc[SUBLANES:, :]], axis=0)
    tail = x_f[L - SUBLANES:, :]
    xwin[...] = tail
    ctail_ref[0] = tail

    xact = acc * jax.nn.sigmoid(acc)
    dt = _softplus(dt_ref[0] + dtb_ref[...])
    if mask_rows:
        valid = lax.broadcasted_iota(jnp.int32, (L, 1), 0) >= mask_rows
        xact = jnp.where(valid, xact, 0.0)
        dt = jnp.where(valid, dt, 0.0)

    a_neg = -jnp.exp(alog_ref[...])
    dA = dt * a_neg
    row_i = lax.broadcasted_iota(jnp.int32, (L, L), 0)
    col_i = lax.broadcasted_iota(jnp.int32, (L, L), 1)
    causal = row_i >= col_i
    tril = causal.astype(BF16)
    cs = _dot3_left(tril, dA)
    csT = cs.T
    dtT = dt.T
    ecs = jnp.exp(cs)
    wdec = jnp.exp(cs[L - 1:L, :] - cs) * dt
    eexp = eexp_ref[...]
    ecs_e = _dot3(ecs, eexp)
    wdec_e = _dot3(wdec, eexp)
    lane = lax.broadcasted_iota(jnp.int32, (L, LANES), 1)
    first_half = lane < SSD_HEAD_DIM

    gw = SSD_HPG * SSD_HEAD_DIM
    y_groups = []
    for g in range(SSD_GROUPS):
        b_g = xact[:, SSD_WIDTH + g * SSD_STATE: SSD_WIDTH + (g + 1) * SSD_STATE]
        c_g = xact[:, SSD_WIDTH + (SSD_GROUPS + g) * SSD_STATE: SSD_WIDTH + (SSD_GROUPS + g + 1) * SSD_STATE]
        b_b = b_g.astype(BF16)
        c_b = c_g.astype(BF16)
        cb = lax.dot_general(c_b, b_b, (((1,), (1,)), ((), ())), preferred_element_type=F32)
        xs_g = xact[:, g * gw:(g + 1) * gw]
        h_prev = hT[g]
        y_off = _dot(c_b, h_prev.astype(BF16)) * ecs_e[:, g * gw:(g + 1) * gw]
        xdec = (xs_g * wdec_e[:, g * gw:(g + 1) * gw]).astype(BF16)
        hT[g] = h_prev * ecs_e[L - 1:L, g * gw:(g + 1) * gw] + _dot(b_g.T.astype(BF16), xdec)
        pieces = []
        for j in range(SSD_HPG // 2):
            xs_pair = xs_g[:, j * LANES:(j + 1) * LANES]
            halves = (jnp.where(first_half, xs_pair, 0.0).astype(BF16),
                      jnp.where(first_half, 0.0, xs_pair).astype(BF16))
            yd = None
            for t in range(2):
                h = g * SSD_HPG + 2 * j + t
                seg = cs[:, h:h + 1] - csT[h:h + 1, :]
                lmat = jnp.exp(jnp.where(causal, seg, -jnp.inf))
                m = (cb * lmat * dtT[h:h + 1, :]).astype(BF16)
                part = _dot(m, halves[t])
                yd = part if yd is None else yd + part
            pieces.append(yd)
        y_groups.append(jnp.concatenate(pieces, axis=-1) + y_off + dexp_ref[:, g * gw:(g + 1) * gw] * xs_g)
    y = jnp.concatenate(y_groups, axis=-1)
    z = z_ref[0].astype(F32)
    y_ref[0] = _rms(y * (z * jax.nn.sigmoid(z)), nrm_ref[...]).astype(y_ref.dtype)

    @pl.when(c == n_chunks - 1)
    def _emit():
        hto_ref[0] = hT[...]
        for g in range(SSD_GROUPS):
            t = hT[g].T
            for k in range(SSD_HPG):
                st_ref[0, g * SSD_HPG + k] = t[k * SSD_HEAD_DIM:(k + 1) * SSD_HEAD_DIM, :]


def _ssd_chunked(xbc, dt, z, cinit, hinit, cw, cb, dtb, alog, dexp, nrm, eexp, mask_rows):
    bsz, seq, _ = xbc.shape
    nc = seq // SSD_CHUNK
    per_step = SSD_CHUNKS_PER_STEP if nc % SSD_CHUNKS_PER_STEP == 0 else 1
    gw = SSD_HPG * SSD_HEAD_DIM
    blk = lambda w: pl.BlockSpec((1, per_step * SSD_CHUNK, w), lambda b, c: (b, c, 0))
    return pl.pallas_call(
        functools.partial(_ssd_body, mask_rows, per_step),
        grid=(bsz, nc // per_step),
        in_specs=[blk(SSD_CONV_DIM), blk(LANES), blk(SSD_WIDTH),
                  pl.BlockSpec((1, SUBLANES, SSD_CONV_DIM), lambda b, c: (0, 0, 0)),
                  pl.BlockSpec((1, SSD_GROUPS, SSD_STATE, gw), lambda b, c: (0, 0, 0, 0)),
                  _full_spec(cw), _full_spec(cb), _full_spec(dtb), _full_spec(alog),
                  _full_spec(dexp), _full_spec(nrm), _full_spec(eexp)],
        out_specs=[blk(SSD_WIDTH),
                   pl.BlockSpec((1, SUBLANES, SSD_CONV_DIM), lambda b, c: (b, 0, 0)),
                   pl.BlockSpec((1, SSD_HEADS, SSD_HEAD_DIM, SSD_STATE), lambda b, c: (b, 0, 0, 0)),
                   pl.BlockSpec((1, SSD_GROUPS, SSD_STATE, gw), lambda b, c: (b, 0, 0, 0))],
        out_shape=[jax.ShapeDtypeStruct((bsz, seq, SSD_WIDTH), BF16),
                   jax.ShapeDtypeStruct((bsz, SUBLANES, SSD_CONV_DIM), F32),
                   jax.ShapeDtypeStruct((bsz, SSD_HEADS, SSD_HEAD_DIM, SSD_STATE), F32),
                   jax.ShapeDtypeStruct((bsz, SSD_GROUPS, SSD_STATE, gw), F32)],
        scratch_shapes=[pltpu.VMEM((SUBLANES, SSD_CONV_DIM), F32),
                        pltpu.VMEM((SSD_GROUPS, SSD_STATE, gw), F32)],
        compiler_params=pltpu.CompilerParams(dimension_semantics=("parallel", "arbitrary"),
                                             vmem_limit_bytes=VMEM_LIMIT),
        name="ssd_chunked",
    )(xbc, dt, z, cinit, hinit, cw, cb, dtb, alog, dexp, nrm, eexp)


def _ssd_step_prep_body(xbc_ref, c0_ref, c1_ref, c2_ref, dt_ref, cw_ref, cb_ref, dtb_ref, alog_ref,
                        xt_ref, dt_out_ref, dec_ref, bc_ref, xs_ref):
    acc = cb_ref[...]
    for k, r in enumerate((c0_ref, c1_ref, c2_ref, xbc_ref)):
        acc = acc + r[...] * cw_ref[k:k + 1, :]
    xact = acc * jax.nn.sigmoid(acc)
    xs = xact[:, :SSD_WIDTH]
    dt = _softplus(dt_ref[...] + dtb_ref[...])
    dt_out_ref[...] = dt
    dec_ref[...] = jnp.exp(dt * -jnp.exp(alog_ref[...]))
    bc_ref[...] = xact[:, SSD_WIDTH:]
    xs_ref[...] = xs
    xt_ref[...] = xs.T.astype(xt_ref.dtype)


def _ssd_step_prep(xbc, c0, c1, c2, dt, cw, cb, dtb, alog):
    n = xbc.shape[0]
    args = (xbc, c0, c1, c2, dt, cw, cb, dtb, alog)
    spec = lambda r, w: pl.BlockSpec((r, w), lambda: (0, 0))
    return pl.pallas_call(
        _ssd_step_prep_body,
        in_specs=[_full_spec(a) for a in args],
        out_specs=[spec(SSD_WIDTH, n), spec(n, LANES), spec(n, LANES), spec(n, 2 * SSD_GROUPS * SSD_STATE),
                   spec(n, SSD_WIDTH)],
        out_shape=[jax.ShapeDtypeStruct((SSD_WIDTH, n), BF16), jax.ShapeDtypeStruct((n, LANES), F32),
                   jax.ShapeDtypeStruct((n, LANES), F32),
                   jax.ShapeDtypeStruct((n, 2 * SSD_GROUPS * SSD_STATE), F32),
                   jax.ShapeDtypeStruct((n, SSD_WIDTH), F32)],
        compiler_params=pltpu.CompilerParams(vmem_limit_bytes=VMEM_LIMIT),
        name="ssd_step_prep",
    )(*args)


def _ssd_step_body(dt_ref, dec_ref, st_ref, xt_ref, bc_ref, so_ref, y_ref):
    n = xt_ref.shape[1]
    gw = SSD_HPG * SSD_HEAD_DIM
    blk = pl.program_id(0)
    seq_id = lax.broadcasted_iota(jnp.int32, (n, SSD_STATE), 0)
    sub_id = lax.broadcasted_iota(jnp.int32, (SUBLANES, gw), 0)
    base = pl.multiple_of(blk * SUBLANES, SUBLANES)
    y_acc = [jnp.zeros((SUBLANES, gw), F32) for _ in range(SSD_GROUPS)]
    for i in range(SUBLANES):
        s = blk * SUBLANES + i
        for g in range(SSD_GROUPS):
            b_all = bc_ref[:, g * SSD_STATE:(g + 1) * SSD_STATE]
            rhs = jnp.where(seq_id == s, b_all, 0.0).astype(BF16)
            outer = _dot(xt_ref[g * gw:(g + 1) * gw, :], rhs)
            news = []
            for k in range(SSD_HPG):
                h = g * SSD_HPG + k
                new = (dec_ref[s * SSD_HEADS + h] * st_ref[i, h]
                       + dt_ref[s * SSD_HEADS + h] * outer[k * SSD_HEAD_DIM:(k + 1) * SSD_HEAD_DIM, :])
                so_ref[i, h] = new
                news.append(new)
            new_g = jnp.concatenate(news, axis=0).astype(BF16)
            c_lo = (SSD_GROUPS + g) * SSD_STATE
            c_blk = bc_ref[pl.ds(base, SUBLANES), c_lo:c_lo + SSD_STATE].astype(BF16)
            r = lax.dot_general(c_blk, new_g, (((1,), (1,)), ((), ())), preferred_element_type=F32)
            y_acc[g] = y_acc[g] + jnp.where(sub_id == i, r, 0.0)
    y_ref[...] = jnp.concatenate(y_acc, axis=-1)


def _ssd_step(dt_flat, dec_flat, state, xt, bc):
    n = state.shape[0]
    st_spec = pl.BlockSpec((SUBLANES, SSD_HEADS, SSD_HEAD_DIM, SSD_STATE), lambda i, *_: (i, 0, 0, 0))
    return pl.pallas_call(
        _ssd_step_body,
        grid_spec=pltpu.PrefetchScalarGridSpec(
            num_scalar_prefetch=2,
            grid=(n // SUBLANES,),
            in_specs=[st_spec, pl.BlockSpec(xt.shape, lambda i, *_: (0, 0)),
                      pl.BlockSpec(bc.shape, lambda i, *_: (0, 0))],
            out_specs=[st_spec, pl.BlockSpec((SUBLANES, SSD_WIDTH), lambda i, *_: (i, 0))]),
        out_shape=[jax.ShapeDtypeStruct(state.shape, F32), jax.ShapeDtypeStruct((n, SSD_WIDTH), F32)],
        compiler_params=pltpu.CompilerParams(dimension_semantics=("parallel",), vmem_limit_bytes=VMEM_LIMIT),
        name="ssd_step",
    )(dt_flat, dec_flat, state, xt, bc)


def _s5_project_in(u_b16, wb_ref, store):
    kw = 16 * S5_GROUP_CH
    nw = 16 * S5_STATE
    for j in range(S5_WIDTH // kw):
        r = _dot(u_b16[:, j * kw:(j + 1) * kw], wb_ref[j])
        store(j, r[:, :nw], r[:, nw:])


def _s5_tail(hre_of, him_of, u_f32, wcr_ref, wci_ref, d_ref, wglu_ref, bglu_ref, nrm_ref):
    cols = []
    for j in range(wcr_ref.shape[0]):
        cols.append(_dot(hre_of(j).astype(BF16), wcr_ref[j]) + _dot(him_of(j).astype(BF16), wci_ref[j]))
    return _s5_finish(cols, u_f32, d_ref, wglu_ref, bglu_ref, nrm_ref)


def _s5_finish(cols, u_f32, d_ref, wglu_ref, bglu_ref, nrm_ref):
    y = jnp.concatenate(cols, axis=-1) + d_ref[...] * u_f32
    y = jax.nn.gelu(y)
    y = y * jax.nn.sigmoid(_dot(y.astype(BF16), wglu_ref[...]) + bglu_ref[...])
    return _rms(y, nrm_ref[...])


def _s5_seq_body(u_hbm, um_ref, wb_ref, abr_ref, abi_ref, wcr_ref, wci_ref, d_ref, wglu_ref, bglu_ref, nrm_ref,
                 y_hbm, sre_ref, sim_ref, ubuf, ybuf, bu, h, in_sems, out_sems):
    j = pl.program_id(0)
    last = pl.num_programs(0) - 1
    lc, bsz = ubuf.shape[1], ubuf.shape[2]
    rows = lc * bsz
    nw = 16 * S5_STATE

    def in_copy(step, b):
        return pltpu.make_async_copy(u_hbm.at[b, pl.ds(step * lc, lc), :], ubuf.at[step % 2, :, b, :],
                                     in_sems.at[step % 2, b])

    def out_copy(step, b):
        return pltpu.make_async_copy(ybuf.at[step % 2, :, b, :], y_hbm.at[b, pl.ds(step * lc, lc), :],
                                     out_sems.at[step % 2, b])

    def project_in(u_b16, nrows):
        def store(jj, re, im):
            bu[0:nrows, jj * nw:(jj + 1) * nw] = re
            bu[0:nrows, S5_LANES + jj * nw:S5_LANES + (jj + 1) * nw] = im
        _s5_project_in(u_b16, wb_ref, store)

    def scan(nsteps):
        for k in range(S5_LANES // S5_SCAN_LANES):
            sl_r = pl.ds(k * S5_SCAN_LANES, S5_SCAN_LANES)
            sl_i = pl.ds(S5_LANES + k * S5_SCAN_LANES, S5_SCAN_LANES)
            ar = abr_ref[:, sl_r]
            ai = abi_ref[:, sl_r]

            def step(l, carry):
                hr, hi = carry
                slab = pl.ds(pl.multiple_of(l * bsz, bsz), bsz)
                nr = ar * hr - ai * hi + bu[slab, sl_r]
                ni = ar * hi + ai * hr + bu[slab, sl_i]
                bu[slab, sl_r] = nr
                bu[slab, sl_i] = ni
                return nr, ni

            hr, hi = lax.fori_loop(0, nsteps, step, (h[:, sl_r], h[:, sl_i]))
            h[:, sl_r] = hr
            h[:, sl_i] = hi

    @pl.when(j == 0)
    def _first():
        for b in range(bsz):
            in_copy(0, b).start()
        h[...] = jnp.zeros_like(h)
        project_in(um_ref[...], N_META * bsz)
        scan(N_META)

    @pl.when(j < last)
    def _prefetch():
        for b in range(bsz):
            in_copy(j + 1, b).start()

    for b in range(bsz):
        in_copy(j, b).wait()
    u2 = ubuf[j % 2].reshape(rows, S5_WIDTH)
    u_b16 = u2.astype(BF16)
    kw = 16 * S5_GROUP_CH

    def project_block(jj):
        r = _dot(u_b16[:, jj * kw:(jj + 1) * kw], wb_ref[jj])
        bu[0:rows, jj * nw:(jj + 1) * nw] = r[:, :nw]
        bu[0:rows, S5_LANES + jj * nw:S5_LANES + (jj + 1) * nw] = r[:, nw:]

    def scan_block(jj):
        for k in range(nw // S5_SCAN_LANES):
            lo = jj * nw + k * S5_SCAN_LANES
            sl_r = slice(lo, lo + S5_SCAN_LANES)
            sl_i = slice(S5_LANES + lo, S5_LANES + lo + S5_SCAN_LANES)
            ar, ai = abr_ref[:, sl_r], abi_ref[:, sl_r]
            hr, hi = h[:, sl_r], h[:, sl_i]
            for l in range(lc):
                slab = slice(l * bsz, (l + 1) * bsz)
                hr, hi = (ar * hr - ai * hi + bu[slab, sl_r], ar * hi + ai * hr + bu[slab, sl_i])
                bu[slab, sl_r] = hr
                bu[slab, sl_i] = hi
            h[:, sl_r] = hr
            h[:, sl_i] = hi

    def readout_block(jj):
        return (_dot(bu[:, jj * nw:(jj + 1) * nw].astype(BF16), wcr_ref[jj])
                + _dot(bu[:, S5_LANES + jj * nw:S5_LANES + (jj + 1) * nw].astype(BF16), wci_ref[jj]))

    n_blocks = S5_WIDTH // kw
    project_block(0)
    cols = []
    for jj in range(n_blocks):
        if jj + 1 < n_blocks:
            project_block(jj + 1)
        scan_block(jj)
        cols.append(readout_block(jj))
    y = _s5_finish(cols, u2, d_ref, wglu_ref, bglu_ref, nrm_ref)
    ybuf[j % 2] = y.reshape(lc, bsz, S5_WIDTH)
    for b in range(bsz):
        out_copy(j, b).start()

    @pl.when(j > 0)
    def _wait_previous_out():
        for b in range(bsz):
            out_copy(j - 1, b).wait()

    @pl.when(j == last)
    def _emit():
        for b in range(bsz):
            out_copy(j, b).wait()
        sre_ref[...] = h[:, 0:S5_LANES]
        sim_ref[...] = h[:, S5_LANES:]


def _s5_seq(u, um, wb, abr, abi, wcr, wci, d, wglu, bglu, nrm):
    bsz, seq, _ = u.shape
    lc = S5_TIME_TILE
    consts = (um, wb, abr, abi, wcr, wci, d, wglu, bglu, nrm)
    st = pl.BlockSpec((bsz, S5_LANES), lambda j: (0, 0))
    return pl.pallas_call(
        _s5_seq_body,
        grid=(seq // lc,),
        in_specs=[pl.BlockSpec(memory_space=pl.ANY)] + [_resident_spec(a) for a in consts],
        out_specs=[pl.BlockSpec(memory_space=pl.ANY), st, st],
        out_shape=[jax.ShapeDtypeStruct((bsz, seq, S5_WIDTH), F32),
                   jax.ShapeDtypeStruct((bsz, S5_LANES), F32), jax.ShapeDtypeStruct((bsz, S5_LANES), F32)],
        scratch_shapes=[pltpu.VMEM((2, lc, bsz, S5_WIDTH), F32), pltpu.VMEM((2, lc, bsz, S5_WIDTH), F32),
                        pltpu.VMEM((lc * bsz, 2 * S5_LANES), F32), pltpu.VMEM((bsz, 2 * S5_LANES), F32),
                        pltpu.SemaphoreType.DMA((2, bsz)), pltpu.SemaphoreType.DMA((2, bsz))],
        compiler_params=pltpu.CompilerParams(dimension_semantics=("arbitrary",), vmem_limit_bytes=VMEM_LIMIT),
        name="s5_seq",
    )(u, *consts)


def _sample_post_body(yc_ref, xs_ref, z_ref, dexp_ref, snrm_ref, u_ref, hr_ref, hi_ref, wb_ref, abr_ref, abi_ref,
                      wcr_ref, wci_ref, d_ref, wglu_ref, bglu_ref, nrm_ref,
                      yssd_ref, ys5_ref, nre_ref, nim_ref):
    z = z_ref[...]
    y = yc_ref[...] + dexp_ref[...] * xs_ref[...]
    yssd_ref[...] = _rms(y * (z * jax.nn.sigmoid(z)), snrm_ref[...]).astype(yssd_ref.dtype)

    u = u_ref[...]
    nw = 16 * S5_STATE
    ar, ai = abr_ref[...], abi_ref[...]

    def store(jj, re, im):
        sl = slice(jj * nw, (jj + 1) * nw)
        h0r, h0i = hr_ref[:, sl], hi_ref[:, sl]
        nre_ref[:, sl] = ar[:, sl] * h0r - ai[:, sl] * h0i + re
        nim_ref[:, sl] = ar[:, sl] * h0i + ai[:, sl] * h0r + im

    _s5_project_in(u.astype(BF16), wb_ref, store)
    slab = lambda ref: (lambda jj: ref[:, jj * nw:(jj + 1) * nw])
    y5 = _s5_tail(slab(nre_ref), slab(nim_ref), u, wcr_ref, wci_ref, d_ref, wglu_ref, bglu_ref, nrm_ref)
    ys5_ref[...] = y5.astype(ys5_ref.dtype)


def _sample_post(yc, xs, z, dexp, snrm, u, h0r, h0i, wb, abr1, abi1, wcr, wci, d, wglu, bglu, nrm):
    n = yc.shape[0]
    args = (yc, xs, z, dexp, snrm, u, h0r, h0i, wb, abr1, abi1, wcr, wci, d, wglu, bglu, nrm)
    spec = lambda w: pl.BlockSpec((n, w), lambda: (0, 0))
    return pl.pallas_call(
        _sample_post_body,
        in_specs=[_full_spec(a) for a in args],
        out_specs=[spec(SSD_WIDTH), spec(S5_WIDTH), spec(S5_LANES), spec(S5_LANES)],
        out_shape=[jax.ShapeDtypeStruct((n, SSD_WIDTH), BF16), jax.ShapeDtypeStruct((n, S5_WIDTH), BF16),
                   jax.ShapeDtypeStruct((n, S5_LANES), F32), jax.ShapeDtypeStruct((n, S5_LANES), F32)],
        compiler_params=pltpu.CompilerParams(vmem_limit_bytes=VMEM_LIMIT),
        name="sample_post",
    )(*args)


def _mix_route_body(n_blocks, n_sorted, xp_ref, ysp_ref, y5p_ref, xs_ref, yss_ref, y5s_ref, *refs):
    consts = refs[:6]
    x1_ref, xn_hbm, rt_ref, pos_ref, meta_ref, carry, fields, xbuf, sems = refs[6:]
    i = pl.program_id(0)
    tm, n_sample = xp_ref.shape[0], xs_ref.shape[0]
    col0 = pl.multiple_of(i * tm, LANES)

    def xn_copy(step, rows, j):
        return pltpu.make_async_copy(xbuf.at[step % 2, pl.ds(0, rows), pl.ds(j * LANES, LANES)],
                                     xn_hbm.at[pl.ds(step * tm, rows), j, :], sems.at[step % 2, j])

    @pl.when(i == 0)
    def _init():
        carry[...] = jnp.zeros_like(carry)

    @pl.when(i < n_blocks)
    def _prompt_rows():
        _mix_route_compute(xp_ref, ysp_ref, y5p_ref, *consts, x1_ref, rt_ref, carry, xbuf.at[i % 2], fields, col0)
        for j in range(PACK_ROWS):
            xn_copy(i, tm, j).start()

    @pl.when(i == n_blocks)
    def _sample_rows():
        _mix_route_compute(xs_ref, yss_ref, y5s_ref, *consts, x1_ref, rt_ref, carry, xbuf.at[i % 2], fields, col0)
        for j in range(PACK_ROWS):
            xn_copy(i, n_sample, j).start()
        _route_layout(carry, fields, pos_ref, meta_ref, n_sorted)
        for j in range(PACK_ROWS):
            xn_copy(i, n_sample, j).wait()

    @pl.when(i > 0)
    def _wait_previous_rows():
        for j in range(PACK_ROWS):
            xn_copy(i - 1, tm, j).wait()


def _route_layout(carry, fields, pos_ref, meta_ref, n_sorted):
    counts = carry[...]
    tiles_per = jnp.floor((counts + (MOE_TILE - 1)) * (1.0 / MOE_TILE))
    upto = lax.broadcasted_iota(jnp.int32, (LANES, LANES), 0) <= lax.broadcasted_iota(jnp.int32, (LANES, LANES), 1)
    tile_end = _dot(tiles_per.astype(BF16), upto.astype(BF16))
    pstart = (tile_end - tiles_per) * MOE_TILE
    n_used = tile_end[:, MOE_EXPERTS - 1:MOE_EXPERTS]

    f = fields[...]
    first_row = jnp.zeros_like(f)
    tile_id = jnp.minimum(lax.broadcasted_iota(jnp.int32, meta_ref.shape, 1).astype(F32), n_used - 1.0)
    tile_expert = jnp.zeros(meta_ref.shape, F32)
    for e in range(MOE_EXPERTS):
        first_row = first_row + jnp.where(f == float(e), pstart[:, e:e + 1], 0.0)
        tile_expert = tile_expert + jnp.where(tile_end[:, e:e + 1] <= tile_id, 1.0, 0.0)
    pos = first_row + pltpu.roll(f, shift=4, axis=0)
    pos_ref[...] = jnp.clip(pos, 0.0, n_sorted - 1.0).astype(jnp.int32)
    is_row0 = lax.broadcasted_iota(jnp.int32, meta_ref.shape, 0) == 0
    meta_ref[...] = jnp.where(is_row0, tile_expert, n_used).astype(jnp.int32)


def _mix_route_compute(x_ref, ys_ref, y5_ref, wa_ref, wb_ref, nf_ref, wrh_ref, wrl_ref, br_ref,
                       x1_ref, rt_ref, carry, xn_buf, fields, col0):
    rows = x_ref.shape[0]
    x1 = x_ref[...] + _dot(ys_ref[...], wa_ref[...]) + _dot(y5_ref[...].astype(BF16), wb_ref[...])
    x1_ref[0:rows, :] = x1
    xn = _rms(x1, nf_ref[...])
    xn_buf[0:rows, :] = _pack_bf16_pairs(xn)

    xh = xn.astype(BF16)
    xl = (xn - xh.astype(F32)).astype(BF16)
    logits = _dot(xh, wrh_ref[...]) + _dot(xl, wrh_ref[...]) + _dot(xh, wrl_ref[...]) + br_ref[...]
    tm = logits.shape[0]
    lane = lax.broadcasted_iota(jnp.int32, logits.shape, 1).astype(F32)
    neg = -jnp.inf
    big = float(LANES)

    def first_max(v):
        m = jnp.max(v, axis=-1, keepdims=True)
        return m, jnp.min(jnp.where(v == m, lane, big), axis=-1, keepdims=True)

    coarse = lane < MOE_GROUPS
    mc, gsel = first_max(jnp.where(coarse, logits, neg))
    psel = 1.0 / jnp.sum(jnp.where(coarse, jnp.exp(logits - mc), 0.0), axis=-1, keepdims=True)
    lo = MOE_GROUPS + MOE_EPG * gsel
    lf = jnp.where((lane >= lo) & (lane < lo + MOE_EPG), logits, neg)
    m1, i1 = first_max(lf)
    m2, i2 = first_max(jnp.where(lane == i1, neg, lf))
    e2 = jnp.exp(m2 - m1)
    g1 = psel / (1.0 + e2)
    g2 = psel * e2 / (1.0 + e2)
    e_a, e_b = i1 - MOE_GROUPS, i2 - MOE_GROUPS

    pick_a, pick_b = lane == e_a, lane == e_b
    picks = jnp.where(pick_a | pick_b, 1.0, 0.0)
    earlier = lax.broadcasted_iota(jnp.int32, (tm, tm), 0) > lax.broadcasted_iota(jnp.int32, (tm, tm), 1)
    prior = _dot(earlier.astype(BF16), picks.astype(BF16)) + carry[...]
    rank_a = jnp.sum(jnp.where(pick_a, prior, 0.0), axis=-1, keepdims=True)
    rank_b = jnp.sum(jnp.where(pick_b, prior, 0.0), axis=-1, keepdims=True)
    carry[...] = prior[tm - 1:tm, :] + picks[tm - 1:tm, :]

    out = jnp.zeros_like(logits)
    for k, v in enumerate((e_a, e_b, g1, g2, rank_a, rank_b)):
        out = jnp.where(lane == float(k), v, out)
    rt_ref[0:rows, :] = out
    fields[:, pl.ds(col0, rows)] = out.T[0:SUBLANES, :]


def _mix_route(prompt, sample, consts, tm, n_tiles):
    n_prompt, n_sample = prompt[0].shape[0], sample[0].shape[0]
    assert n_prompt % tm == 0 and n_sample <= tm
    n_blocks = n_prompt // tm
    total_rows = n_prompt + n_sample
    row = lambda w: pl.BlockSpec((tm, w), lambda i: (jnp.minimum(i, n_blocks - 1), 0))
    out_row = lambda w: pl.BlockSpec((tm, w), lambda i: (i, 0))
    assert total_rows % LANES == 0 and n_tiles <= 2 * LANES
    whole = lambda shape: pl.BlockSpec(shape, lambda i: (0, 0))
    return pl.pallas_call(
        functools.partial(_mix_route_body, n_blocks, n_tiles * MOE_TILE),
        grid=(n_blocks + 1,),
        in_specs=([row(D_MODEL), row(SSD_WIDTH), row(S5_WIDTH)] + [_full_spec(a) for a in sample]
                  + [_full_spec(a) for a in consts]),
        out_specs=[out_row(D_MODEL), pl.BlockSpec(memory_space=pl.ANY), out_row(LANES),
                   whole((SUBLANES, total_rows)), whole((SUBLANES, 2 * LANES))],
        out_shape=[jax.ShapeDtypeStruct((total_rows, D_MODEL), F32),
                   jax.ShapeDtypeStruct((total_rows, PACK_ROWS, LANES), jnp.uint32),
                   jax.ShapeDtypeStruct((total_rows, LANES), F32),
                   jax.ShapeDtypeStruct((SUBLANES, total_rows), jnp.int32),
                   jax.ShapeDtypeStruct((SUBLANES, 2 * LANES), jnp.int32)],
        scratch_shapes=[pltpu.VMEM((1, LANES), F32), pltpu.VMEM((SUBLANES, total_rows), F32),
                        pltpu.VMEM((2, tm, D_MODEL // 2), jnp.uint32), pltpu.SemaphoreType.DMA((2, PACK_ROWS))],
        compiler_params=pltpu.CompilerParams(dimension_semantics=("arbitrary",), vmem_limit_bytes=VMEM_LIMIT),
        name="mix_route",
    )(*prompt, *sample, *consts)


def _sc_mesh():
    return plsc.VectorSubcoreMesh(core_axis_name="c", subcore_axis_name="s")


def _sc_worker():
    return lax.axis_index("s") * SC_CORES + lax.axis_index("c")


def _sc_dispatch(xn, pos_a, pos_b, n_rows):
    n_tok = xn.shape[0]
    ch = SC_DISPATCH_ROWS
    n_chunks = n_tok // ch
    assert n_tok % ch == 0 and n_chunks >= SC_WORKERS
    max_mine = -(-n_chunks // SC_WORKERS)
    row_shape, dtype = xn.shape[1:], xn.dtype
    stage = [pltpu.VMEM((ch,), jnp.int32), pltpu.VMEM((ch,), jnp.int32), pltpu.VMEM((ch,) + row_shape, dtype),
             pltpu.SemaphoreType.DMA]

    @functools.partial(
        pl.kernel, mesh=_sc_mesh(),
        out_type=jax.ShapeDtypeStruct((n_rows,) + row_shape, dtype),
        scratch_types=stage + stage + [pltpu.SemaphoreType.DMA])
    def push(xn_hbm, pa_hbm, pb_hbm, xs_hbm, ia0, ib0, rows0, lsem0, ia1, ib1, rows1, lsem1, ssem):
        wid = _sc_worker()
        mine = (n_chunks - wid + SC_WORKERS - 1) // SC_WORKERS
        bufs = ((ia0, ib0, rows0, lsem0), (ia1, ib1, rows1, lsem1))

        def loads(t, b):
            ia, ib, rows, sem = bufs[b]
            off = pl.multiple_of((wid + t * SC_WORKERS) * ch, ch)
            return (pltpu.make_async_copy(pa_hbm.at[pl.ds(off, ch)], ia, sem),
                    pltpu.make_async_copy(pb_hbm.at[pl.ds(off, ch)], ib, sem),
                    pltpu.make_async_copy(xn_hbm.at[pl.ds(off, ch)], rows, sem))

        def stage_in(t, b):
            for c in loads(t, b):
                c.start()

        def scatter(t, b):
            ia, ib, rows, _ = bufs[b]
            for c in loads(t, b):
                c.wait()
            first = pltpu.async_copy(rows, xs_hbm.at[ia], ssem)
            second = pltpu.async_copy(rows, xs_hbm.at[ib], ssem)
            first.wait()
            second.wait()

        stage_in(0, 0)

        @pl.loop(0, (max_mine + 1) // 2)
        def _(p):
            t = 2 * p

            @pl.when(t + 1 < mine)
            def _():
                stage_in(t + 1, 1)

            @pl.when(t < mine)
            def _():
                scatter(t, 0)

            @pl.when(t + 2 < mine)
            def _():
                stage_in(t + 2, 0)

            @pl.when(t + 1 < mine)
            def _():
                scatter(t + 1, 1)

    return push(xn, pos_a, pos_b)


def _sc_collect(ysorted, pos_flat, ch):
    n_pick = pos_flat.shape[0]
    per_worker = n_pick // SC_WORKERS
    n_chunks = per_worker // ch
    assert n_pick % SC_WORKERS == 0 and per_worker % ch == 0
    row_shape, dtype = ysorted.shape[1:], ysorted.dtype

    @functools.partial(
        pl.kernel, mesh=_sc_mesh(),
        out_type=jax.ShapeDtypeStruct((n_pick,) + row_shape, dtype),
        scratch_types=[pltpu.VMEM((ch,), jnp.int32), pltpu.VMEM((ch,), jnp.int32),
                       pltpu.VMEM((ch,) + row_shape, dtype), pltpu.VMEM((ch,) + row_shape, dtype),
                       pltpu.SemaphoreType.DMA, pltpu.SemaphoreType.DMA])
    def pull(ys_hbm, pos_hbm, out_hbm, idx0, idx1, rows0, rows1, sem0, sem1):
        base = _sc_worker() * per_worker
        bufs = ((idx0, rows0, sem0), (idx1, rows1, sem1))

        def offset(j):
            return pl.multiple_of(base + j * ch, SUBLANES)

        def fetch(j, b):
            idx, rows, sem = bufs[b]
            pltpu.sync_copy(pos_hbm.at[pl.ds(offset(j), ch)], idx)
            pltpu.async_copy(ys_hbm.at[idx], rows, sem)

        def flush(j, b):
            idx, rows, sem = bufs[b]
            pltpu.make_async_copy(ys_hbm.at[idx], rows, sem).wait()
            pltpu.sync_copy(rows, out_hbm.at[pl.ds(offset(j), ch)])

        fetch(0, 0)

        @pl.loop(0, n_chunks // 2)
        def _(p):
            j = 2 * p
            fetch(j + 1, 1)
            flush(j, 0)

            @pl.when(j + 2 < n_chunks)
            def _():
                fetch(j + 2, 0)

            flush(j + 1, 1)

        if n_chunks % 2:
            flush(n_chunks - 1, 0)

    return pull(ysorted, pos_flat)


def _moe_ffn_body(*refs):
    for k in range(MOE_TILES_PER_STEP):
        _moe_ffn_tile(k, *refs)


def _moe_ffn_tile(k, te_ref, nused_ref, x_ref, wg_hbm, wu_hbm, wd_hbm, y_ref,
                  wg_f32, wu_f32, wd_f32, wgb, wub, wdb, slot_ref, sems):
    i = pl.program_id(0) * MOE_TILES_PER_STEP + k
    n_used = nused_ref[0]
    window = pl.ds(k * MOE_TILE * PACK_ROWS, MOE_TILE * PACK_ROWS)
    x_ref, y_ref = x_ref.at[window, :], y_ref.at[window, :]

    def fetch(expert, slot):
        return (pltpu.make_async_copy(wg_hbm.at[expert], wg_f32.at[slot], sems.at[slot, 0]),
                pltpu.make_async_copy(wu_hbm.at[expert], wu_f32.at[slot], sems.at[slot, 1]),
                pltpu.make_async_copy(wd_hbm.at[expert], wd_f32.at[slot], sems.at[slot, 2]))

    @pl.when(i >= n_used)
    def _unused_tile():
        y_ref[...] = jnp.zeros_like(y_ref)

    @pl.when(i < n_used)
    def _tile():
        expert = te_ref[i]

        @pl.when(i == 0)
        def _first_fetch():
            slot_ref[0] = 0
            for c in fetch(expert, 0):
                c.start()

        @pl.when((i == 0) | (expert != te_ref[jnp.maximum(i - 1, 0)]))
        def _new_expert():
            slot = slot_ref[0]
            nxt = lax.while_loop(lambda k: (k < n_used) & (te_ref[jnp.minimum(k, n_used - 1)] == expert),
                                 lambda k: k + 1, i + 1)

            @pl.when(nxt < n_used)
            def _prefetch():
                for c in fetch(te_ref[jnp.minimum(nxt, n_used - 1)], 1 - slot):
                    c.start()

            for c in fetch(expert, slot):
                c.wait()
            wgb[...] = wg_f32[slot].astype(BF16)
            wub[...] = wu_f32[slot].astype(BF16)
            wdb[...] = wd_f32[slot].astype(BF16)
            slot_ref[0] = 1 - slot

        x = _unpack_bf16_pairs(x_ref, MOE_TILE).astype(BF16)
        gate = _dot(x, wgb[...])
        hmid = (gate * jax.nn.sigmoid(gate)) * _dot(x, wub[...])
        y = _dot(hmid.astype(BF16), wdb[...])
        packed = _pack_bf16_pairs(y)
        for j in range(PACK_ROWS):
            y_ref[pl.ds(j, MOE_TILE, stride=PACK_ROWS), :] = packed[:, j * LANES:(j + 1) * LANES]


def _moe_ffn(tile_expert, n_used, xsorted, w_gate, w_up, w_down):
    n_tiles = tile_expert.shape[0]
    per_step = MOE_TILES_PER_STEP
    assert n_tiles % per_step == 0
    hbm = pl.BlockSpec(memory_space=pl.ANY)
    tile = lambda imap: pl.BlockSpec((per_step * MOE_TILE * PACK_ROWS, LANES), imap)
    up_shape, down_shape = (D_MODEL, MOE_D_FF), (MOE_D_FF, D_MODEL)
    return pl.pallas_call(
        _moe_ffn_body,
        grid_spec=pltpu.PrefetchScalarGridSpec(
            num_scalar_prefetch=2,
            grid=(n_tiles // per_step,),
            in_specs=[tile(lambda i, te, nu: (jnp.clip(i, 0, jnp.maximum(nu[0] - 1, 0) // per_step), 0)),
                      hbm, hbm, hbm],
            out_specs=tile(lambda i, te, nu: (i, 0)),
            scratch_shapes=[pltpu.VMEM((2,) + up_shape, F32), pltpu.VMEM((2,) + up_shape, F32),
                            pltpu.VMEM((2,) + down_shape, F32),
                            pltpu.VMEM(up_shape, BF16), pltpu.VMEM(up_shape, BF16), pltpu.VMEM(down_shape, BF16),
                            pltpu.SMEM((1,), jnp.int32), pltpu.SemaphoreType.DMA((2, 3))]),
        out_shape=jax.ShapeDtypeStruct((n_tiles * MOE_TILE * PACK_ROWS, LANES), jnp.uint32),
        compiler_params=pltpu.CompilerParams(dimension_semantics=("arbitrary",), vmem_limit_bytes=VMEM_LIMIT),
        name="moe_ffn",
    )(tile_expert, n_used, xsorted, w_gate, w_up, w_down)


def _combine_body(x1_ref, rt_ref, ya_ref, yb_ref, nf_ref, *rest):
    out_ref = rest[-1]
    rt = rt_ref[...]
    x1 = x1_ref[...]
    tm = x1.shape[0]

    x2 = (x1 + rt[:, 2:3] * _unpack_bf16_pairs(ya_ref.at[0], tm)
          + rt[:, 3:4] * _unpack_bf16_pairs(yb_ref.at[0], tm))
    out_ref[...] = _rms(x2, nf_ref[...])


def _combine(x1, rt, y_picks, nf, tm, rows, x_block, y_block, out_rows, out_block, out_buf=None):
    row = lambda w: pl.BlockSpec((tm, w), lambda i: (i + x_block, 0))
    pick = lambda k: pl.BlockSpec((1, tm * PACK_ROWS, LANES), lambda i: (k, i + y_block, 0))
    in_specs = [row(D_MODEL), row(LANES), pick(0), pick(1), pl.BlockSpec((1, D_MODEL), lambda i: (0, 0))]
    args = [x1, rt, y_picks, y_picks, nf]
    aliases = {}
    if out_buf is not None:
        in_specs.append(pl.BlockSpec(memory_space=pl.ANY))
        aliases[len(args)] = 0
        args.append(out_buf)
    return pl.pallas_call(
        _combine_body,
        grid=(rows // tm,),
        in_specs=in_specs,
        out_specs=pl.BlockSpec((tm, D_MODEL), lambda i: (i + out_block, 0)),
        out_shape=jax.ShapeDtypeStruct((out_rows, D_MODEL), F32),
        input_output_aliases=aliases,
        compiler_params=pltpu.CompilerParams(dimension_semantics=("parallel",), vmem_limit_bytes=VMEM_LIMIT),
        name="moe_combine",
    )(*args)


def _s5_tables(a_re, a_im, log_dt, b_re, b_im, c_re, c_im):
    dt = jnp.exp(log_dt)[:, None]
    mag = jnp.exp(a_re * dt)
    ab_re = mag * jnp.cos(a_im * dt)
    ab_im = mag * jnp.sin(a_im * dt)
    den = a_re * a_re + a_im * a_im
    nr = ab_re - 1.0
    q_re = (nr * a_re + ab_im * a_im) / den
    q_im = (ab_im * a_re - nr * a_im) / den
    bb_re = q_re[..., None] * b_re - q_im[..., None] * b_im
    bb_im = q_re[..., None] * b_im + q_im[..., None] * b_re
    nblk = S5_GROUPS // 16
    kw, nw = 16 * S5_GROUP_CH, 16 * S5_STATE
    same_group = (jnp.arange(kw)[:, None] // S5_GROUP_CH) == (jnp.arange(nw)[None, :] // S5_STATE)

    def in_map(bb):
        rows = bb.reshape(nblk, 16, S5_STATE, S5_GROUP_CH).transpose(0, 1, 3, 2).reshape(nblk, kw, S5_STATE)
        return jnp.where(same_group, jnp.tile(rows, (1, 1, 16)), 0.0)

    def out_map(cc):
        cols = cc.reshape(nblk, 16, S5_GROUP_CH, S5_STATE).transpose(0, 3, 1, 2).reshape(nblk, S5_STATE, kw)
        return jnp.where(same_group.T, jnp.tile(cols, (1, 16, 1)), 0.0)

    wb = jnp.concatenate([in_map(bb_re), in_map(bb_im)], axis=-1).astype(BF16)
    return (wb, ab_re.reshape(1, S5_LANES), ab_im.reshape(1, S5_LANES),
            out_map(c_re).astype(BF16), out_map(-c_im).astype(BF16))


def kernel(x_prompt, x_sample, state_ssd_conv, state_ssd_ssm, state_s5_re, state_s5_im, meta_tokens, norm_mix, w_in, conv_w, conv_b, dt_bias, a_log, d_ssd, ssd_norm, s5_a_re, s5_a_im, s5_log_dt, s5_b_re, s5_b_im, s5_c_re, s5_c_im, s5_d, w_glu, b_glu, s5_norm, w_out, norm_ffn, router_coarse_w, router_coarse_b, router_fine_w, router_fine_b, w_gate, w_up, w_down, norm_final):
    bp, seq, _ = x_prompt.shape
    bs = x_sample.shape[0]
    n_prompt = bp * seq
    n_tok = n_prompt + bs
    row2 = lambda v: v.reshape(1, -1)
    pad_heads = lambda v: jnp.pad(v, (0, LANES - SSD_HEADS)).reshape(1, LANES)

    w = w_in[0]
    o1, o2, o3 = SSD_WIDTH, SSD_WIDTH + SSD_CONV_DIM, SSD_WIDTH + SSD_CONV_DIM + SSD_HEADS
    wz, wx, wu = w[:, :o1].astype(BF16), w[:, o1:o2].astype(BF16), w[:, o3:].astype(BF16)
    wdt = jnp.pad(w[:, o2:o3], ((0, 0), (0, LANES - SSD_HEADS))).astype(BF16)
    g_mix = row2(norm_mix[0])
    cw, cb = conv_w[0], row2(conv_b[0])
    dtb, alog = pad_heads(dt_bias[0]), pad_heads(a_log[0])
    dexp = row2(jnp.repeat(d_ssd[0], SSD_HEAD_DIM))
    snrm = row2(ssd_norm[0])
    eexp = (jnp.arange(LANES)[:, None] == (jnp.arange(SSD_WIDTH) // SSD_HEAD_DIM)[None, :]).astype(BF16)
    wb5, ab_re, ab_im, wcr, wci = _s5_tables(s5_a_re[0], s5_a_im[0], s5_log_dt[0], s5_b_re[0], s5_b_im[0],
                                             s5_c_re[0], s5_c_im[0])
    d5, wglu, bglu, nrm5 = row2(s5_d[0]), w_glu[0].astype(BF16), row2(b_glu[0]), row2(s5_norm[0])
    wo_a, wo_b = w_out[0][:SSD_WIDTH].astype(BF16), w_out[0][SSD_WIDTH:].astype(BF16)
    w_r = jnp.concatenate([router_coarse_w[0], router_fine_w[0].transpose(1, 0, 2).reshape(D_MODEL, MOE_EXPERTS)], axis=1)
    w_r = jnp.pad(w_r, ((0, 0), (0, LANES - w_r.shape[1])))
    wrh = w_r.astype(BF16)
    wrl = (w_r - wrh.astype(F32)).astype(BF16)
    b_r = jnp.concatenate([router_coarse_b[0], router_fine_b[0].reshape(-1)])
    b_r = jnp.pad(b_r, (0, LANES - b_r.shape[0])).reshape(1, LANES)

    zp, xbcp, dtp, up = _in_proj(x_prompt.reshape(n_prompt, D_MODEL), g_mix, wz, wx, wdt, wu, TOK_TILE, BF16, F32)
    xsm = jnp.concatenate([x_sample.reshape(bs, D_MODEL), meta_tokens], axis=0)
    zs, xbcs, dts, us = _in_proj(xsm, g_mix, wz, wx, wdt, wu, xsm.shape[0], F32, F32)

    front = SSD_CHUNK - N_META
    padf = lambda a: jnp.pad(a[bs:], ((front, 0), (0, 0)))[None]
    gw = SSD_HPG * SSD_HEAD_DIM
    ssd_consts = (cw, cb, dtb, alog, dexp, snrm, eexp)
    _, ctail_m, _, ht_m = _ssd_chunked(
        padf(xbcs).astype(BF16), padf(dts), jnp.zeros((1, SSD_CHUNK, SSD_WIDTH), F32),
        jnp.zeros((1, SUBLANES, SSD_CONV_DIM), F32), jnp.zeros((1, SSD_GROUPS, SSD_STATE, gw), F32),
        *ssd_consts, mask_rows=front)
    y_ssd_p, ctail_p, ssm_p, _ = _ssd_chunked(
        xbcp.reshape(bp, seq, SSD_CONV_DIM), dtp.reshape(bp, seq, LANES), zp.reshape(bp, seq, SSD_WIDTH),
        ctail_m, ht_m, *ssd_consts, mask_rows=0)

    abr8, abi8 = jnp.broadcast_to(ab_re, (bp, S5_LANES)), jnp.broadcast_to(ab_im, (bp, S5_LANES))
    um8 = jnp.repeat(us[bs:], bp, axis=0).astype(BF16)
    y_s5_p, s5re_p, s5im_p = _s5_seq(up.reshape(bp, seq, S5_WIDTH), um8, wb5, abr8, abi8,
                                     wcr, wci, d5, wglu, bglu, nrm5)

    cst = state_ssd_conv[0]
    xt_s, dt_s, dec_s, bc, xs_s = _ssd_step_prep(xbcs[:bs], cst[:, 0], cst[:, 1], cst[:, 2], dts[:bs],
                                                 cw, cb, dtb, alog)
    ssm_s, y_core = _ssd_step(dt_s[:, :SSD_HEADS].reshape(-1), dec_s[:, :SSD_HEADS].reshape(-1),
                              state_ssd_ssm[0], xt_s, bc)
    y_ssd_s, y_s5_s, s5re_s, s5im_s = _sample_post(
        y_core, xs_s, zs[:bs], dexp, snrm, us[:bs], state_s5_re[0].reshape(bs, S5_LANES),
        state_s5_im[0].reshape(bs, S5_LANES), wb5, ab_re, ab_im, wcr, wci, d5, wglu, bglu, nrm5)

    route_consts = (wo_a, wo_b, row2(norm_ffn[0]), wrh, wrl, b_r)
    n_tiles = -(-2 * n_tok // MOE_TILE) + MOE_EXPERTS
    n_tiles = -(-n_tiles // MOE_TILES_PER_STEP) * MOE_TILES_PER_STEP
    x1, xn, rt, pos, meta = _mix_route(
        (x_prompt.reshape(n_prompt, D_MODEL), y_ssd_p.reshape(n_prompt, SSD_WIDTH), y_s5_p.reshape(n_prompt, S5_WIDTH)),
        (x_sample.reshape(bs, D_MODEL), y_ssd_s, y_s5_s), route_consts, TOK_TILE, n_tiles)

    pos_a, pos_b = pos[0], pos[1]
    tile_expert, n_used = meta[0, :n_tiles], meta[1, :1]
    xsorted = _sc_dispatch(xn, pos_a, pos_b, n_tiles * MOE_TILE)
    ysorted = _moe_ffn(tile_expert, n_used, xsorted.reshape(-1, LANES), w_gate[0], w_up[0], w_down[0])
    nfin = row2(norm_final)

    half = n_prompt // 2

    def collect(lo, hi, ch):
        picks = jnp.concatenate([pos_a[lo:hi], pos_b[lo:hi]])
        packed_rows = ysorted.reshape(-1, PACK_ROWS, LANES)
        return _sc_collect(packed_rows, picks, ch).reshape(2, (hi - lo) * PACK_ROWS, LANES)

    picks_1 = collect(0, half, SC_COLLECT_ROWS[0])
    picks_2 = collect(half, n_tok, SC_COLLECT_ROWS[1])
    blocks = half // TOK_TILE
    y_p = _combine(x1, rt, picks_1, nfin, TOK_TILE, half, 0, 0, n_prompt, 0)
    y_p = _combine(x1, rt, picks_2, nfin, TOK_TILE, half, blocks, 0, n_prompt, blocks, out_buf=y_p)
    y_s = _combine(x1, rt, picks_2, nfin, bs, bs, n_prompt // bs, half // bs, bs, 0)

    s5_state = lambda a, b: a.reshape(1, b, S5_GROUPS, S5_STATE)
    new_conv_s = jnp.stack([cst[:, 1], cst[:, 2], xbcs[:bs]], axis=1)[None]
    return (y_p.reshape(bp, seq, D_MODEL), y_s.reshape(bs, 1, D_MODEL),
            ctail_p[:, SUBLANES - (SSD_CONV - 1):][None], ssm_p[None], s5_state(s5re_p, bp), s5_state(s5im_p, bp),
            new_conv_s, ssm_s[None], s5_state(s5re_s, bs), s5_state(s5im_s, bs))
```

```python
import functools

import jax
import jax.numpy as jnp
from jax import lax
from jax.experimental import pallas as pl
from jax.experimental.pallas import tpu as pltpu
from jax.experimental.pallas import tpu_sc as plsc

F32, BF16 = jnp.float32, jnp.bfloat16

D_MODEL = 1024
N_META = 16
SSD_WIDTH = 1024
SSD_HEAD_DIM = 64
SSD_HEADS = 16
SSD_GROUPS = 2
SSD_HPG = SSD_HEADS // SSD_GROUPS
SSD_STATE = 128
SSD_CONV = 4
SSD_CHUNK = 128
SSD_CONV_DIM = SSD_WIDTH + 2 * SSD_GROUPS * SSD_STATE
S5_WIDTH = 1024
S5_GROUP_CH = 16
S5_GROUPS = 64
S5_STATE = 64
S5_LANES = S5_GROUPS * S5_STATE
MOE_GROUPS = 4
MOE_EPG = 8
MOE_EXPERTS = MOE_GROUPS * MOE_EPG
MOE_D_FF = 512
EPS = 1e-6

LANES = 128
SUBLANES = 8
VMEM_LIMIT = 56 * 1024 * 1024

SSD_CHUNKS_PER_STEP = 4
S5_TIME_TILE = 64
S5_SCAN_LANES = 512
MOE_TILE = 256
MOE_TILES_PER_STEP = 4
SLAB_ROWS = D_MODEL // LANES
PACK_ROWS = SLAB_ROWS // 2
SC_CORES = 2
SC_SUBCORES = 16
SC_WORKERS = SC_CORES * SC_SUBCORES
SC_DISPATCH_ROWS = 64
SC_COLLECT_ROWS = (64, 104)
TOK_TILE = 512
IN_PROJ_TILE = 1024


def _dot(a, b):
    return jnp.dot(a, b, preferred_element_type=F32)


def _rms(x, g):
    return x * lax.rsqrt(jnp.mean(x * x, axis=-1, keepdims=True) + EPS) * g


def _softplus(x):
    return jnp.maximum(x, 0.0) + jnp.log1p(jnp.exp(-jnp.abs(x)))


def _split3(x):
    hi = x.astype(BF16)
    r = x - hi.astype(F32)
    mid = r.astype(BF16)
    lo = (r - mid.astype(F32)).astype(BF16)
    return hi, mid, lo


def _dot3(x, w):
    hi, mid, lo = _split3(x)
    return _dot(hi, w) + _dot(mid, w) + _dot(lo, w)


def _dot3_left(w, x):
    hi, mid, lo = _split3(x)
    return _dot(w, hi) + _dot(w, mid) + _dot(w, lo)


def _pack_bf16_pairs(x):
    bits = pltpu.bitcast(x.astype(BF16).astype(F32), jnp.uint32)
    half = x.shape[1] // 2
    return (bits[:, :half] & jnp.uint32(0xFFFF0000)) | (bits[:, half:] >> jnp.uint32(16))


def _unpack_bf16_pairs(ref, rows):
    words = [ref[pl.ds(j, rows, stride=PACK_ROWS), :] for j in range(PACK_ROWS)]
    high = [pltpu.bitcast(w & jnp.uint32(0xFFFF0000), F32) for w in words]
    low = [pltpu.bitcast(w << jnp.uint32(16), F32) for w in words]
    return jnp.concatenate(high + low, axis=-1)


def _full_spec(a):
    nd = a.ndim
    return pl.BlockSpec(a.shape, lambda *_: (0,) * nd)


def _resident_spec(a):
    nd = a.ndim
    return pl.BlockSpec(a.shape, lambda *_: (0,) * nd, pipeline_mode=pl.Buffered(1))


def _in_proj_body(x_ref, g_ref, wz_ref, wx_ref, wdt_ref, wu_ref, z_ref, xbc_ref, dt_ref, u_ref):
    xb = _rms(x_ref[...], g_ref[...]).astype(BF16)
    z_ref[...] = _dot(xb, wz_ref[...]).astype(z_ref.dtype)
    xbc_ref[...] = _dot(xb, wx_ref[...]).astype(xbc_ref.dtype)
    dt_ref[...] = _dot(xb, wdt_ref[...])
    u_ref[...] = _dot(xb, wu_ref[...]).astype(u_ref.dtype)


def _in_proj(x2d, g, wz, wx, wdt, wu, tm, act_dtype, u_dtype):
    rows = x2d.shape[0]
    row = lambda w: pl.BlockSpec((tm, w), lambda i: (i, 0))
    return pl.pallas_call(
        _in_proj_body,
        grid=(rows // tm,),
        in_specs=[row(D_MODEL)] + [_resident_spec(a) for a in (g, wz, wx, wdt, wu)],
        out_specs=[row(SSD_WIDTH), row(SSD_CONV_DIM), row(LANES), row(S5_WIDTH)],
        out_shape=[jax.ShapeDtypeStruct((rows, SSD_WIDTH), act_dtype),
                   jax.ShapeDtypeStruct((rows, SSD_CONV_DIM), act_dtype),
                   jax.ShapeDtypeStruct((rows, LANES), F32),
                   jax.ShapeDtypeStruct((rows, S5_WIDTH), u_dtype)],
        compiler_params=pltpu.CompilerParams(dimension_semantics=("parallel",), vmem_limit_bytes=VMEM_LIMIT),
        name="in_proj",
    )(x2d, g, wz, wx, wdt, wu)


def _ssd_body(mask_rows, per_step, *refs):
    for k in range(per_step):
        _ssd_chunk(mask_rows, per_step, k, *refs)


def _ssd_chunk(mask_rows, per_step, k, xbc_ref, dt_ref, z_ref, cinit_ref, hinit_ref, cw_ref, cb_ref, dtb_ref,
               alog_ref, dexp_ref, nrm_ref, eexp_ref, y_ref, ctail_ref, st_ref, hto_ref, xwin, hT):
    L = SSD_CHUNK
    c = pl.program_id(1) * per_step + k
    n_chunks = pl.num_programs(1) * per_step
    window = pl.ds(k * L, L)
    xbc_ref, dt_ref, z_ref, y_ref = (r.at[:, window, :] for r in (xbc_ref, dt_ref, z_ref, y_ref))

    @pl.when(c == 0)
    def _init():
        xwin[...] = cinit_ref[0]
        hT[...] = hinit_ref[0]

    x_b = xbc_ref[0]
    x_f = x_b.astype(F32)
    taps = SSD_CONV - 1
    m_i = lax.broadcasted_iota(jnp.int32, (taps * L, L), 0)
    r_i = lax.broadcasted_iota(jnp.int32, (taps * L, L), 1)
    shift = (r_i + (taps - m_i // L) == m_i % L).astype(BF16)
    shifted = _dot(shift, x_b)
    acc = cb_ref[...] + x_f * cw_ref[taps:taps + 1, :]
    for k in range(taps):
        acc = acc + shifted[k * L:(k + 1) * L, :] * cw_ref[k:k + 1, :]
    joint = jnp.concatenate([xwin[...], x_f[0:SUBLANES, :]], axis=0)
    row8 = lax.broadcasted_iota(jnp.int32, (SUBLANES, 1), 0)
    head = acc[0:SUBLANES, :]
    for k in range(taps):
        d = taps - k
        head = head + jnp.where(row8 < d, joint[SUBLANES - d:2 * SUBLANES - d, :], 0.0) * cw_ref[k:k + 1, :]
    acc = jnp.concatenate([head, acc[SUBLANES:, :]], axis=0)
    tail = x_f[L - SUBLANES:, :]
    xwin[...] = tail
    ctail_ref[0] = tail

    xact = acc * jax.nn.sigmoid(acc)
    dt = _softplus(dt_ref[0] + dtb_ref[...])
    if mask_rows:
        valid = lax.broadcasted_iota(jnp.int32, (L, 1), 0) >= mask_rows
        xact = jnp.where(valid, xact, 0.0)
        dt = jnp.where(valid, dt, 0.0)

    a_neg = -jnp.exp(alog_ref[...])
    dA = dt * a_neg
    row_i = lax.broadcasted_iota(jnp.int32, (L, L), 0)
    col_i = lax.broadcasted_iota(jnp.int32, (L, L), 1)
    causal = row_i >= col_i
    tril = causal.astype(BF16)
    cs = _dot3_left(tril, dA)
    csT = cs.T
    dtT = dt.T
    ecs = jnp.exp(cs)
    wdec = jnp.exp(cs[L - 1:L, :] - cs) * dt
    eexp = eexp_ref[...]
    ecs_e = _dot3(ecs, eexp)
    wdec_e = _dot3(wdec, eexp)
    lane = lax.broadcasted_iota(jnp.int32, (L, LANES), 1)
    first_half = lane < SSD_HEAD_DIM

    gw = SSD_HPG * SSD_HEAD_DIM
    y_groups = []
    for g in range(SSD_GROUPS):
        b_g = xact[:, SSD_WIDTH + g * SSD_STATE: SSD_WIDTH + (g + 1) * SSD_STATE]
        c_g = xact[:, SSD_WIDTH + (SSD_GROUPS + g) * SSD_STATE: SSD_WIDTH + (SSD_GROUPS + g + 1) * SSD_STATE]
        b_b = b_g.astype(BF16)
        c_b = c_g.astype(BF16)
        cb = lax.dot_general(c_b, b_b, (((1,), (1,)), ((), ())), preferred_element_type=F32)
        xs_g = xact[:, g * gw:(g + 1) * gw]
        h_prev = hT[g]
        y_off = _dot(c_b, h_prev.astype(BF16)) * ecs_e[:, g * gw:(g + 1) * gw]
        xdec = (xs_g * wdec_e[:, g * gw:(g + 1) * gw]).astype(BF16)
        hT[g] = h_prev * ecs_e[L - 1:L, g * gw:(g + 1) * gw] + _dot(b_g.T.astype(BF16), xdec)
        pieces = []
        for j in range(SSD_HPG // 2):
            xs_pair = xs_g[:, j * LANES:(j + 1) * LANES]
            halves = (jnp.where(first_half, xs_pair, 0.0).astype(BF16),
                      jnp.where(first_half, 0.0, xs_pair).astype(BF16))
            yd = None
            for t in range(2):
                h = g * SSD_HPG + 2 * j + t
                seg = cs[:, h:h + 1] - csT[h:h + 1, :]
                lmat = jnp.exp(jnp.where(causal, seg, -jnp.inf))
                m = (cb * lmat * dtT[h:h + 1, :]).astype(BF16)
                part = _dot(m, halves[t])
                yd = part if yd is None else yd + part
            pieces.append(yd)
        y_groups.append(jnp.concatenate(pieces, axis=-1) + y_off + dexp_ref[:, g * gw:(g + 1) * gw] * xs_g)
    y = jnp.concatenate(y_groups, axis=-1)
    z = z_ref[0].astype(F32)
    y_ref[0] = _rms(y * (z * jax.nn.sigmoid(z)), nrm_ref[...]).astype(y_ref.dtype)

    @pl.when(c == n_chunks - 1)
    def _emit():
        hto_ref[0] = hT[...]
        for g in range(SSD_GROUPS):
            t = hT[g].T
            for k in range(SSD_HPG):
                st_ref[0, g * SSD_HPG + k] = t[k * SSD_HEAD_DIM:(k + 1) * SSD_HEAD_DIM, :]


def _ssd_chunked(xbc, dt, z, cinit, hinit, cw, cb, dtb, alog, dexp, nrm, eexp, mask_rows):
    bsz, seq, _ = xbc.shape
    nc = seq // SSD_CHUNK
    per_step = SSD_CHUNKS_PER_STEP if nc % SSD_CHUNKS_PER_STEP == 0 else 1
    gw = SSD_HPG * SSD_HEAD_DIM
    blk = lambda w: pl.BlockSpec((1, per_step * SSD_CHUNK, w), lambda b, c: (b, c, 0))
    return pl.pallas_call(
        functools.partial(_ssd_body, mask_rows, per_step),
        grid=(bsz, nc // per_step),
        in_specs=[blk(SSD_CONV_DIM), blk(LANES), blk(SSD_WIDTH),
                  pl.BlockSpec((1, SUBLANES, SSD_CONV_DIM), lambda b, c: (0, 0, 0)),
                  pl.BlockSpec((1, SSD_GROUPS, SSD_STATE, gw), lambda b, c: (0, 0, 0, 0)),
                  _full_spec(cw), _full_spec(cb), _full_spec(dtb), _full_spec(alog),
                  _full_spec(dexp), _full_spec(nrm), _full_spec(eexp)],
        out_specs=[blk(SSD_WIDTH),
                   pl.BlockSpec((1, SUBLANES, SSD_CONV_DIM), lambda b, c: (b, 0, 0)),
                   pl.BlockSpec((1, SSD_HEADS, SSD_HEAD_DIM, SSD_STATE), lambda b, c: (b, 0, 0, 0)),
                   pl.BlockSpec((1, SSD_GROUPS, SSD_STATE, gw), lambda b, c: (b, 0, 0, 0))],
        out_shape=[jax.ShapeDtypeStruct((bsz, seq, SSD_WIDTH), BF16),
                   jax.ShapeDtypeStruct((bsz, SUBLANES, SSD_CONV_DIM), F32),
                   jax.ShapeDtypeStruct((bsz, SSD_HEADS, SSD_HEAD_DIM, SSD_STATE), F32),
                   jax.ShapeDtypeStruct((bsz, SSD_GROUPS, SSD_STATE, gw), F32)],
        scratch_shapes=[pltpu.VMEM((SUBLANES, SSD_CONV_DIM), F32),
                        pltpu.VMEM((SSD_GROUPS, SSD_STATE, gw), F32)],
        compiler_params=pltpu.CompilerParams(dimension_semantics=("parallel", "arbitrary"),
                                             vmem_limit_bytes=VMEM_LIMIT),
        name="ssd_chunked",
    )(xbc, dt, z, cinit, hinit, cw, cb, dtb, alog, dexp, nrm, eexp)


def _ssd_step_prep_body(xbc_ref, c0_ref, c1_ref, c2_ref, dt_ref, cw_ref, cb_ref, dtb_ref, alog_ref,
                        xt_ref, dt_out_ref, dec_ref, bc_ref, xs_ref):
    acc = cb_ref[...]
    for k, r in enumerate((c0_ref, c1_ref, c2_ref, xbc_ref)):
        acc = acc + r[...] * cw_ref[k:k + 1, :]
    xact = acc * jax.nn.sigmoid(acc)
    xs = xact[:, :SSD_WIDTH]
    dt = _softplus(dt_ref[...] + dtb_ref[...])
    dt_out_ref[...] = dt
    dec_ref[...] = jnp.exp(dt * -jnp.exp(alog_ref[...]))
    bc_ref[...] = xact[:, SSD_WIDTH:]
    xs_ref[...] = xs
    xt_ref[...] = xs.T.astype(xt_ref.dtype)


def _ssd_step_prep(xbc, c0, c1, c2, dt, cw, cb, dtb, alog):
    n = xbc.shape[0]
    args = (xbc, c0, c1, c2, dt, cw, cb, dtb, alog)
    spec = lambda r, w: pl.BlockSpec((r, w), lambda: (0, 0))
    return pl.pallas_call(
        _ssd_step_prep_body,
        in_specs=[_full_spec(a) for a in args],
        out_specs=[spec(SSD_WIDTH, n), spec(n, LANES), spec(n, LANES), spec(n, 2 * SSD_GROUPS * SSD_STATE),
                   spec(n, SSD_WIDTH)],
        out_shape=[jax.ShapeDtypeStruct((SSD_WIDTH, n), BF16), jax.ShapeDtypeStruct((n, LANES), F32),
                   jax.ShapeDtypeStruct((n, LANES), F32),
                   jax.ShapeDtypeStruct((n, 2 * SSD_GROUPS * SSD_STATE), F32),
                   jax.ShapeDtypeStruct((n, SSD_WIDTH), F32)],
        compiler_params=pltpu.CompilerParams(vmem_limit_bytes=VMEM_LIMIT),
        name="ssd_step_prep",
    )(*args)


def _ssd_step_body(dt_ref, dec_ref, st_ref, xt_ref, bc_ref, so_ref, y_ref):
    n = xt_ref.shape[1]
    gw = SSD_HPG * SSD_HEAD_DIM
    blk = pl.program_id(0)
    seq_id = lax.broadcasted_iota(jnp.int32, (n, SSD_STATE), 0)
    sub_id = lax.broadcasted_iota(jnp.int32, (SUBLANES, gw), 0)
    base = pl.multiple_of(blk * SUBLANES, SUBLANES)
    y_acc = [jnp.zeros((SUBLANES, gw), F32) for _ in range(SSD_GROUPS)]
    for i in range(SUBLANES):
        s = blk * SUBLANES + i
        for g in range(SSD_GROUPS):
            b_all = bc_ref[:, g * SSD_STATE:(g + 1) * SSD_STATE]
            rhs = jnp.where(seq_id == s, b_all, 0.0).astype(BF16)
            outer = _dot(xt_ref[g * gw:(g + 1) * gw, :], rhs)
            news = []
            for k in range(SSD_HPG):
                h = g * SSD_HPG + k
                new = (dec_ref[s * SSD_HEADS + h] * st_ref[i, h]
                       + dt_ref[s * SSD_HEADS + h] * outer[k * SSD_HEAD_DIM:(k + 1) * SSD_HEAD_DIM, :])
                so_ref[i, h] = new
                news.append(new)
            new_g = jnp.concatenate(news, axis=0).astype(BF16)
            c_lo = (SSD_GROUPS + g) * SSD_STATE
            c_blk = bc_ref[pl.ds(base, SUBLANES), c_lo:c_lo + SSD_STATE].astype(BF16)
            r = lax.dot_general(c_blk, new_g, (((1,), (1,)), ((), ())), preferred_element_type=F32)
            y_acc[g] = y_acc[g] + jnp.where(sub_id == i, r, 0.0)
    y_ref[...] = jnp.concatenate(y_acc, axis=-1)


def _ssd_step(dt_flat, dec_flat, state, xt, bc):
    n = state.shape[0]
    st_spec = pl.BlockSpec((SUBLANES, SSD_HEADS, SSD_HEAD_DIM, SSD_STATE), lambda i, *_: (i, 0, 0, 0))
    return pl.pallas_call(
        _ssd_step_body,
        grid_spec=pltpu.PrefetchScalarGridSpec(
            num_scalar_prefetch=2,
            grid=(n // SUBLANES,),
            in_specs=[st_spec, pl.BlockSpec(xt.shape, lambda i, *_: (0, 0)),
                      pl.BlockSpec(bc.shape, lambda i, *_: (0, 0))],
            out_specs=[st_spec, pl.BlockSpec((SUBLANES, SSD_WIDTH), lambda i, *_: (i, 0))]),
        out_shape=[jax.ShapeDtypeStruct(state.shape, F32), jax.ShapeDtypeStruct((n, SSD_WIDTH), F32)],
        compiler_params=pltpu.CompilerParams(dimension_semantics=("parallel",), vmem_limit_bytes=VMEM_LIMIT),
        name="ssd_step",
    )(dt_flat, dec_flat, state, xt, bc)


def _s5_project_in(u_b16, wb_ref, store):
    kw = 16 * S5_GROUP_CH
    nw = 16 * S5_STATE
    for j in range(S5_WIDTH // kw):
        r = _dot(u_b16[:, j * kw:(j + 1) * kw], wb_ref[j])
        store(j, r[:, :nw], r[:, nw:])


def _s5_tail(hre_of, him_of, u_f32, wcr_ref, wci_ref, d_ref, wglu_ref, bglu_ref, nrm_ref):
    cols = []
    for j in range(wcr_ref.shape[0]):
        cols.append(_dot(hre_of(j).astype(BF16), wcr_ref[j]) + _dot(him_of(j).astype(BF16), wci_ref[j]))
    return _s5_finish(cols, u_f32, d_ref, wglu_ref, bglu_ref, nrm_ref)


def _s5_finish(cols, u_f32, d_ref, wglu_ref, bglu_ref, nrm_ref):
    y = jnp.concatenate(cols, axis=-1) + d_ref[...] * u_f32
    y = jax.nn.gelu(y)
    y = y * jax.nn.sigmoid(_dot(y.astype(BF16), wglu_ref[...]) + bglu_ref[...])
    return _rms(y, nrm_ref[...])


def _s5_seq_body(u_hbm, um_ref, wb_ref, abr_ref, abi_ref, wcr_ref, wci_ref, d_ref, wglu_ref, bglu_ref, nrm_ref,
                 y_hbm, sre_ref, sim_ref, ubuf, ybuf, bu, h, in_sems, out_sems):
    j = pl.program_id(0)
    last = pl.num_programs(0) - 1
    lc, bsz = ubuf.shape[1], ubuf.shape[2]
    rows = lc * bsz
    nw = 16 * S5_STATE

    def in_copy(step, b):
        return pltpu.make_async_copy(u_hbm.at[b, pl.ds(step * lc, lc), :], ubuf.at[step % 2, :, b, :],
                                     in_sems.at[step % 2, b])

    def out_copy(step, b):
        return pltpu.make_async_copy(ybuf.at[step % 2, :, b, :], y_hbm.at[b, pl.ds(step * lc, lc), :],
                                     out_sems.at[step % 2, b])

    def project_in(u_b16, nrows):
        def store(jj, re, im):
            bu[0:nrows, jj * nw:(jj + 1) * nw] = re
            bu[0:nrows, S5_LANES + jj * nw:S5_LANES + (jj + 1) * nw] = im
        _s5_project_in(u_b16, wb_ref, store)

    def scan(nsteps):
        for k in range(S5_LANES // S5_SCAN_LANES):
            sl_r = pl.ds(k * S5_SCAN_LANES, S5_SCAN_LANES)
            sl_i = pl.ds(S5_LANES + k * S5_SCAN_LANES, S5_SCAN_LANES)
            ar = abr_ref[:, sl_r]
            ai = abi_ref[:, sl_r]

            def step(l, carry):
                hr, hi = carry
                slab = pl.ds(pl.multiple_of(l * bsz, bsz), bsz)
                nr = ar * hr - ai * hi + bu[slab, sl_r]
                ni = ar * hi + ai * hr + bu[slab, sl_i]
                bu[slab, sl_r] = nr
                bu[slab, sl_i] = ni
                return nr, ni

            hr, hi = lax.fori_loop(0, nsteps, step, (h[:, sl_r], h[:, sl_i]))
            h[:, sl_r] = hr
            h[:, sl_i] = hi

    @pl.when(j == 0)
    def _first():
        for b in range(bsz):
            in_copy(0, b).start()
        h[...] = jnp.zeros_like(h)
        project_in(um_ref[...], N_META * bsz)
        scan(N_META)

    @pl.when(j < last)
    def _prefetch():
        for b in range(bsz):
            in_copy(j + 1, b).start()

    for b in range(bsz):
        in_copy(j, b).wait()
    u2 = ubuf[j % 2].reshape(rows, S5_WIDTH)
    u_b16 = u2.astype(BF16)
    kw = 16 * S5_GROUP_CH

    def project_block(jj):
        r = _dot(u_b16[:, jj * kw:(jj + 1) * kw], wb_ref[jj])
        bu[0:rows, jj * nw:(jj + 1) * nw] = r[:, :nw]
        bu[0:rows, S5_LANES + jj * nw:S5_LANES + (jj + 1) * nw] = r[:, nw:]

    def scan_block(jj):
        for k in range(nw // S5_SCAN_LANES):
            lo = jj * nw + k * S5_SCAN_LANES
            sl_r = slice(lo, lo + S5_SCAN_LANES)
            sl_i = slice(S5_LANES + lo, S5_LANES + lo + S5_SCAN_LANES)
            ar, ai = abr_ref[:, sl_r], abi_ref[:, sl_r]
            hr, hi = h[:, sl_r], h[:, sl_i]
            for l in range(lc):
                slab = slice(l * bsz, (l + 1) * bsz)
                hr, hi = (ar * hr - ai * hi + bu[slab, sl_r], ar * hi + ai * hr + bu[slab, sl_i])
                bu[slab, sl_r] = hr
                bu[slab, sl_i] = hi
            h[:, sl_r] = hr
            h[:, sl_i] = hi

    def readout_block(jj):
        return (_dot(bu[:, jj * nw:(jj + 1) * nw].astype(BF16), wcr_ref[jj])
                + _dot(bu[:, S5_LANES + jj * nw:S5_LANES + (jj + 1) * nw].astype(BF16), wci_ref[jj]))

    n_blocks = S5_WIDTH // kw
    project_block(0)
    cols = []
    for jj in range(n_blocks):
        if jj + 1 < n_blocks:
            project_block(jj + 1)
        scan_block(jj)
        cols.append(readout_block(jj))
    y = _s5_finish(cols, u2, d_ref, wglu_ref, bglu_ref, nrm_ref)
    ybuf[j % 2] = y.reshape(lc, bsz, S5_WIDTH)
    for b in range(bsz):
        out_copy(j, b).start()

    @pl.when(j > 0)
    def _wait_previous_out():
        for b in range(bsz):
            out_copy(j - 1, b).wait()

    @pl.when(j == last)
    def _emit():
        for b in range(bsz):
            out_copy(j, b).wait()
        sre_ref[...] = h[:, 0:S5_LANES]
        sim_ref[...] = h[:, S5_LANES:]


def _s5_seq(u, um, wb, abr, abi, wcr, wci, d, wglu, bglu, nrm):
    bsz, seq, _ = u.shape
    lc = S5_TIME_TILE
    consts = (um, wb, abr, abi, wcr, wci, d, wglu, bglu, nrm)
    st = pl.BlockSpec((bsz, S5_LANES), lambda j: (0, 0))
    return pl.pallas_call(
        _s5_seq_body,
        grid=(seq // lc,),
        in_specs=[pl.BlockSpec(memory_space=pl.ANY)] + [_resident_spec(a) for a in consts],
        out_specs=[pl.BlockSpec(memory_space=pl.ANY), st, st],
        out_shape=[jax.ShapeDtypeStruct((bsz, seq, S5_WIDTH), F32),
                   jax.ShapeDtypeStruct((bsz, S5_LANES), F32), jax.ShapeDtypeStruct((bsz, S5_LANES), F32)],
        scratch_shapes=[pltpu.VMEM((2, lc, bsz, S5_WIDTH), F32), pltpu.VMEM((2, lc, bsz, S5_WIDTH), F32),
                        pltpu.VMEM((lc * bsz, 2 * S5_LANES), F32), pltpu.VMEM((bsz, 2 * S5_LANES), F32),
                        pltpu.SemaphoreType.DMA((2, bsz)), pltpu.SemaphoreType.DMA((2, bsz))],
        compiler_params=pltpu.CompilerParams(dimension_semantics=("arbitrary",), vmem_limit_bytes=VMEM_LIMIT),
        name="s5_seq",
    )(u, *consts)


def _sample_post_body(yc_ref, xs_ref, z_ref, dexp_ref, snrm_ref, u_ref, hr_ref, hi_ref, wb_ref, abr_ref, abi_ref,
                      wcr_ref, wci_ref, d_ref, wglu_ref, bglu_ref, nrm_ref,
                      yssd_ref, ys5_ref, nre_ref, nim_ref):
    z = z_ref[...]
    y = yc_ref[...] + dexp_ref[...] * xs_ref[...]
    yssd_ref[...] = _rms(y * (z * jax.nn.sigmoid(z)), snrm_ref[...]).astype(yssd_ref.dtype)

    u = u_ref[...]
    nw = 16 * S5_STATE
    ar, ai = abr_ref[...], abi_ref[...]

    def store(jj, re, im):
        sl = slice(jj * nw, (jj + 1) * nw)
        h0r, h0i = hr_ref[:, sl], hi_ref[:, sl]
        nre_ref[:, sl] = ar[:, sl] * h0r - ai[:, sl] * h0i + re
        nim_ref[:, sl] = ar[:, sl] * h0i + ai[:, sl] * h0r + im

    _s5_project_in(u.astype(BF16), wb_ref, store)
    slab = lambda ref: (lambda jj: ref[:, jj * nw:(jj + 1) * nw])
    y5 = _s5_tail(slab(nre_ref), slab(nim_ref), u, wcr_ref, wci_ref, d_ref, wglu_ref, bglu_ref, nrm_ref)
    ys5_ref[...] = y5.astype(ys5_ref.dtype)


def _sample_post(yc, xs, z, dexp, snrm, u, h0r, h0i, wb, abr1, abi1, wcr, wci, d, wglu, bglu, nrm):
    n = yc.shape[0]
    args = (yc, xs, z, dexp, snrm, u, h0r, h0i, wb, abr1, abi1, wcr, wci, d, wglu, bglu, nrm)
    spec = lambda w: pl.BlockSpec((n, w), lambda: (0, 0))
    return pl.pallas_call(
        _sample_post_body,
        in_specs=[_full_spec(a) for a in args],
        out_specs=[spec(SSD_WIDTH), spec(S5_WIDTH), spec(S5_LANES), spec(S5_LANES)],
        out_shape=[jax.ShapeDtypeStruct((n, SSD_WIDTH), BF16), jax.ShapeDtypeStruct((n, S5_WIDTH), BF16),
                   jax.ShapeDtypeStruct((n, S5_LANES), F32), jax.ShapeDtypeStruct((n, S5_LANES), F32)],
        compiler_params=pltpu.CompilerParams(vmem_limit_bytes=VMEM_LIMIT),
        name="sample_post",
    )(*args)


def _mix_route_body(n_blocks, n_sorted, xp_ref, ysp_ref, y5p_ref, xs_ref, yss_ref, y5s_ref, *refs):
    consts = refs[:6]
    x1_ref, xn_hbm, rt_ref, pos_ref, meta_ref, carry, fields, xbuf, sems = refs[6:]
    i = pl.program_id(0)
    tm, n_sample = xp_ref.shape[0], xs_ref.shape[0]
    col0 = pl.multiple_of(i * tm, LANES)

    def xn_copy(step, rows, j):
        return pltpu.make_async_copy(xbuf.at[step % 2, pl.ds(0, rows), pl.ds(j * LANES, LANES)],
                                     xn_hbm.at[pl.ds(step * tm, rows), j, :], sems.at[step % 2, j])

    @pl.when(i == 0)
    def _init():
        carry[...] = jnp.zeros_like(carry)

    @pl.when(i < n_blocks)
    def _prompt_rows():
        _mix_route_compute(xp_ref, ysp_ref, y5p_ref, *consts, x1_ref, rt_ref, carry, xbuf.at[i % 2], fields, col0)
        for j in range(PACK_ROWS):
            xn_copy(i, tm, j).start()

    @pl.when(i == n_blocks)
    def _sample_rows():
        _mix_route_compute(xs_ref, yss_ref, y5s_ref, *consts, x1_ref, rt_ref, carry, xbuf.at[i % 2], fields, col0)
        for j in range(PACK_ROWS):
            xn_copy(i, n_sample, j).start()
        _route_layout(carry, fields, pos_ref, meta_ref, n_sorted)
        for j in range(PACK_ROWS):
            xn_copy(i, n_sample, j).wait()

    @pl.when(i > 0)
    def _wait_previous_rows():
        for j in range(PACK_ROWS):
            xn_copy(i - 1, tm, j).wait()


def _route_layout(carry, fields, pos_ref, meta_ref, n_sorted):
    counts = carry[...]
    tiles_per = jnp.floor((counts + (MOE_TILE - 1)) * (1.0 / MOE_TILE))
    upto = lax.broadcasted_iota(jnp.int32, (LANES, LANES), 0) <= lax.broadcasted_iota(jnp.int32, (LANES, LANES), 1)
    tile_end = _dot(tiles_per.astype(BF16), upto.astype(BF16))
    pstart = (tile_end - tiles_per) * MOE_TILE
    n_used = tile_end[:, MOE_EXPERTS - 1:MOE_EXPERTS]

    f = fields[...]
    first_row = jnp.zeros_like(f)
    tile_id = jnp.minimum(lax.broadcasted_iota(jnp.int32, meta_ref.shape, 1).astype(F32), n_used - 1.0)
    tile_expert = jnp.zeros(meta_ref.shape, F32)
    for e in range(MOE_EXPERTS):
        first_row = first_row + jnp.where(f == float(e), pstart[:, e:e + 1], 0.0)
        tile_expert = tile_expert + jnp.where(tile_end[:, e:e + 1] <= tile_id, 1.0, 0.0)
    pos = first_row + pltpu.roll(f, shift=4, axis=0)
    pos_ref[...] = jnp.clip(pos, 0.0, n_sorted - 1.0).astype(jnp.int32)
    is_row0 = lax.broadcasted_iota(jnp.int32, meta_ref.shape, 0) == 0
    meta_ref[...] = jnp.where(is_row0, tile_expert, n_used).astype(jnp.int32)


def _mix_route_compute(x_ref, ys_ref, y5_ref, wa_ref, wb_ref, nf_ref, wrh_ref, wrl_ref, br_ref,
                       x1_ref, rt_ref, carry, xn_buf, fields, col0):
    rows = x_ref.shape[0]
    x1 = x_ref[...] + _dot(ys_ref[...], wa_ref[...]) + _dot(y5_ref[...].astype(BF16), wb_ref[...])
    x1_ref[0:rows, :] = x1
    xn = _rms(x1, nf_ref[...])
    xn_buf[0:rows, :] = _pack_bf16_pairs(xn)

    xh = xn.astype(BF16)
    xl = (xn - xh.astype(F32)).astype(BF16)
    logits = _dot(xh, wrh_ref[...]) + _dot(xl, wrh_ref[...]) + _dot(xh, wrl_ref[...]) + br_ref[...]
    tm = logits.shape[0]
    lane = lax.broadcasted_iota(jnp.int32, logits.shape, 1).astype(F32)
    neg = -jnp.inf
    big = float(LANES)

    def first_max(v):
        m = jnp.max(v, axis=-1, keepdims=True)
        return m, jnp.min(jnp.where(v == m, lane, big), axis=-1, keepdims=True)

    coarse = lane < MOE_GROUPS
    mc, gsel = first_max(jnp.where(coarse, logits, neg))
    psel = 1.0 / jnp.sum(jnp.where(coarse, jnp.exp(logits - mc), 0.0), axis=-1, keepdims=True)
    lo = MOE_GROUPS + MOE_EPG * gsel
    lf = jnp.where((lane >= lo) & (lane < lo + MOE_EPG), logits, neg)
    m1, i1 = first_max(lf)
    m2, i2 = first_max(jnp.where(lane == i1, neg, lf))
    e2 = jnp.exp(m2 - m1)
    g1 = psel / (1.0 + e2)
    g2 = psel * e2 / (1.0 + e2)
    e_a, e_b = i1 - MOE_GROUPS, i2 - MOE_GROUPS

    pick_a, pick_b = lane == e_a, lane == e_b
    picks = jnp.where(pick_a | pick_b, 1.0, 0.0)
    earlier = lax.broadcasted_iota(jnp.int32, (tm, tm), 0) > lax.broadcasted_iota(jnp.int32, (tm, tm), 1)
    prior = _dot(earlier.astype(BF16), picks.astype(BF16)) + carry[...]
    rank_a = jnp.sum(jnp.where(pick_a, prior, 0.0), axis=-1, keepdims=True)
    rank_b = jnp.sum(jnp.where(pick_b, prior, 0.0), axis=-1, keepdims=True)
    carry[...] = prior[tm - 1:tm, :] + picks[tm - 1:tm, :]

    out = jnp.zeros_like(logits)
    for k, v in enumerate((e_a, e_b, g1, g2, rank_a, rank_b)):
        out = jnp.where(lane == float(k), v, out)
    rt_ref[0:rows, :] = out
    fields[:, pl.ds(col0, rows)] = out.T[0:SUBLANES, :]


def _mix_route(prompt, sample, consts, tm, n_tiles):
    n_prompt, n_sample = prompt[0].shape[0], sample[0].shape[0]
    assert n_prompt % tm == 0 and n_sample <= tm
    n_blocks = n_prompt // tm
    total_rows = n_prompt + n_sample
    row = lambda w: pl.BlockSpec((tm, w), lambda i: (jnp.minimum(i, n_blocks - 1), 0))
    out_row = lambda w: pl.BlockSpec((tm, w), lambda i: (i, 0))
    assert total_rows % LANES == 0 and n_tiles <= 2 * LANES
    whole = lambda shape: pl.BlockSpec(shape, lambda i: (0, 0))
    return pl.pallas_call(
        functools.partial(_mix_route_body, n_blocks, n_tiles * MOE_TILE),
        grid=(n_blocks + 1,),
        in_specs=([row(D_MODEL), row(SSD_WIDTH), row(S5_WIDTH)] + [_full_spec(a) for a in sample]
                  + [_full_spec(a) for a in consts]),
        out_specs=[out_row(D_MODEL), pl.BlockSpec(memory_space=pl.ANY), out_row(LANES),
                   whole((SUBLANES, total_rows)), whole((SUBLANES, 2 * LANES))],
        out_shape=[jax.ShapeDtypeStruct((total_rows, D_MODEL), F32),
                   jax.ShapeDtypeStruct((total_rows, PACK_ROWS, LANES), jnp.uint32),
                   jax.ShapeDtypeStruct((total_rows, LANES), F32),
                   jax.ShapeDtypeStruct((SUBLANES, total_rows), jnp.int32),
                   jax.ShapeDtypeStruct((SUBLANES, 2 * LANES), jnp.int32)],
        scratch_shapes=[pltpu.VMEM((1, LANES), F32), pltpu.VMEM((SUBLANES, total_rows), F32),
                        pltpu.VMEM((2, tm, D_MODEL // 2), jnp.uint32), pltpu.SemaphoreType.DMA((2, PACK_ROWS))],
        compiler_params=pltpu.CompilerParams(dimension_semantics=("arbitrary",), vmem_limit_bytes=VMEM_LIMIT),
        name="mix_route",
    )(*prompt, *sample, *consts)


def _sc_mesh():
    return plsc.VectorSubcoreMesh(core_axis_name="c", subcore_axis_name="s")


def _sc_worker():
    return lax.axis_index("s") * SC_CORES + lax.axis_index("c")


def _sc_dispatch(xn, pos_a, pos_b, n_rows):
    n_tok = xn.shape[0]
    ch = SC_DISPATCH_ROWS
    n_chunks = n_tok // ch
    assert n_tok % ch == 0 and n_chunks >= SC_WORKERS
    max_mine = -(-n_chunks // SC_WORKERS)
    row_shape, dtype = xn.shape[1:], xn.dtype
    stage = [pltpu.VMEM((ch,), jnp.int32), pltpu.VMEM((ch,), jnp.int32), pltpu.VMEM((ch,) + row_shape, dtype),
             pltpu.SemaphoreType.DMA]

    @functools.partial(
        pl.kernel, mesh=_sc_mesh(),
        out_type=jax.ShapeDtypeStruct((n_rows,) + row_shape, dtype),
        scratch_types=stage + stage + [pltpu.SemaphoreType.DMA])
    def push(xn_hbm, pa_hbm, pb_hbm, xs_hbm, ia0, ib0, rows0, lsem0, ia1, ib1, rows1, lsem1, ssem):
        wid = _sc_worker()
        mine = (n_chunks - wid + SC_WORKERS - 1) // SC_WORKERS
        bufs = ((ia0, ib0, rows0, lsem0), (ia1, ib1, rows1, lsem1))

        def loads(t, b):
            ia, ib, rows, sem = bufs[b]
            off = pl.multiple_of((wid + t * SC_WORKERS) * ch, ch)
            return (pltpu.make_async_copy(pa_hbm.at[pl.ds(off, ch)], ia, sem),
                    pltpu.make_async_copy(pb_hbm.at[pl.ds(off, ch)], ib, sem),
                    pltpu.make_async_copy(xn_hbm.at[pl.ds(off, ch)], rows, sem))

        def stage_in(t, b):
            for c in loads(t, b):
                c.start()

        def scatter(t, b):
            ia, ib, rows, _ = bufs[b]
            for c in loads(t, b):
                c.wait()
            first = pltpu.async_copy(rows, xs_hbm.at[ia], ssem)
            second = pltpu.async_copy(rows, xs_hbm.at[ib], ssem)
            first.wait()
            second.wait()

        stage_in(0, 0)

        @pl.loop(0, (max_mine + 1) // 2)
        def _(p):
            t = 2 * p

            @pl.when(t + 1 < mine)
            def _():
                stage_in(t + 1, 1)

            @pl.when(t < mine)
            def _():
                scatter(t, 0)

            @pl.when(t + 2 < mine)
            def _():
                stage_in(t + 2, 0)

            @pl.when(t + 1 < mine)
            def _():
                scatter(t + 1, 1)

    return push(xn, pos_a, pos_b)


def _sc_collect(ysorted, pos_flat, ch):
    n_pick = pos_flat.shape[0]
    per_worker = n_pick // SC_WORKERS
    n_chunks = per_worker // ch
    assert n_pick % SC_WORKERS == 0 and per_worker % ch == 0
    row_shape, dtype = ysorted.shape[1:], ysorted.dtype

    @functools.partial(
        pl.kernel, mesh=_sc_mesh(),
        out_type=jax.ShapeDtypeStruct((n_pick,) + row_shape, dtype),
        scratch_types=[pltpu.VMEM((ch,), jnp.int32), pltpu.VMEM((ch,), jnp.int32),
                       pltpu.VMEM((ch,) + row_shape, dtype), pltpu.VMEM((ch,) + row_shape, dtype),
                       pltpu.SemaphoreType.DMA, pltpu.SemaphoreType.DMA])
    def pull(ys_hbm, pos_hbm, out_hbm, idx0, idx1, rows0, rows1, sem0, sem1):
        base = _sc_worker() * per_worker
        bufs = ((idx0, rows0, sem0), (idx1, rows1, sem1))

        def offset(j):
            return pl.multiple_of(base + j * ch, SUBLANES)

        def fetch(j, b):
            idx, rows, sem = bufs[b]
            pltpu.sync_copy(pos_hbm.at[pl.ds(offset(j), ch)], idx)
            pltpu.async_copy(ys_hbm.at[idx], rows, sem)

        def flush(j, b):
            idx, rows, sem = bufs[b]
            pltpu.make_async_copy(ys_hbm.at[idx], rows, sem).wait()
            pltpu.sync_copy(rows, out_hbm.at[pl.ds(offset(j), ch)])

        fetch(0, 0)

        @pl.loop(0, n_chunks // 2)
        def _(p):
            j = 2 * p
            fetch(j + 1, 1)
            flush(j, 0)

            @pl.when(j + 2 < n_chunks)
            def _():
                fetch(j + 2, 0)

            flush(j + 1, 1)

        if n_chunks % 2:
            flush(n_chunks - 1, 0)

    return pull(ysorted, pos_flat)


def _moe_ffn_body(*refs):
    for k in range(MOE_TILES_PER_STEP):
        _moe_ffn_tile(k, *refs)


def _moe_ffn_tile(k, te_ref, nused_ref, x_ref, wg_hbm, wu_hbm, wd_hbm, y_ref,
                  wg_f32, wu_f32, wd_f32, wgb, wub, wdb, slot_ref, sems):
    i = pl.program_id(0) * MOE_TILES_PER_STEP + k
    n_used = nused_ref[0]
    window = pl.ds(k * MOE_TILE * PACK_ROWS, MOE_TILE * PACK_ROWS)
    x_ref, y_ref = x_ref.at[window, :], y_ref.at[window, :]

    def fetch(expert, slot):
        return (pltpu.make_async_copy(wg_hbm.at[expert], wg_f32.at[slot], sems.at[slot, 0]),
                pltpu.make_async_copy(wu_hbm.at[expert], wu_f32.at[slot], sems.at[slot, 1]),
                pltpu.make_async_copy(wd_hbm.at[expert], wd_f32.at[slot], sems.at[slot, 2]))

    @pl.when(i >= n_used)
    def _unused_tile():
        y_ref[...] = jnp.zeros_like(y_ref)

    @pl.when(i < n_used)
    def _tile():
        expert = te_ref[i]

        @pl.when(i == 0)
        def _first_fetch():
            slot_ref[0] = 0
            for c in fetch(expert, 0):
                c.start()

        @pl.when((i == 0) | (expert != te_ref[jnp.maximum(i - 1, 0)]))
        def _new_expert():
            slot = slot_ref[0]
            nxt = lax.while_loop(lambda k: (k < n_used) & (te_ref[jnp.minimum(k, n_used - 1)] == expert),
                                 lambda k: k + 1, i + 1)

            @pl.when(nxt < n_used)
            def _prefetch():
                for c in fetch(te_ref[jnp.minimum(nxt, n_used - 1)], 1 - slot):
                    c.start()

            for c in fetch(expert, slot):
                c.wait()
            wgb[...] = wg_f32[slot].astype(BF16)
            wub[...] = wu_f32[slot].astype(BF16)
            wdb[...] = wd_f32[slot].astype(BF16)
            slot_ref[0] = 1 - slot

        x = _unpack_bf16_pairs(x_ref, MOE_TILE).astype(BF16)
        gate = _dot(x, wgb[...])
        hmid = (gate * jax.nn.sigmoid(gate)) * _dot(x, wub[...])
        y = _dot(hmid.astype(BF16), wdb[...])
        packed = _pack_bf16_pairs(y)
        for j in range(PACK_ROWS):
            y_ref[pl.ds(j, MOE_TILE, stride=PACK_ROWS), :] = packed[:, j * LANES:(j + 1) * LANES]


def _moe_ffn(tile_expert, n_used, xsorted, w_gate, w_up, w_down):
    n_tiles = tile_expert.shape[0]
    per_step = MOE_TILES_PER_STEP
    assert n_tiles % per_step == 0
    hbm = pl.BlockSpec(memory_space=pl.ANY)
    tile = lambda imap: pl.BlockSpec((per_step * MOE_TILE * PACK_ROWS, LANES), imap)
    up_shape, down_shape = (D_MODEL, MOE_D_FF), (MOE_D_FF, D_MODEL)
    return pl.pallas_call(
        _moe_ffn_body,
        grid_spec=pltpu.PrefetchScalarGridSpec(
            num_scalar_prefetch=2,
            grid=(n_tiles // per_step,),
            in_specs=[tile(lambda i, te, nu: (jnp.clip(i, 0, jnp.maximum(nu[0] - 1, 0) // per_step), 0)),
                      hbm, hbm, hbm],
            out_specs=tile(lambda i, te, nu: (i, 0)),
            scratch_shapes=[pltpu.VMEM((2,) + up_shape, F32), pltpu.VMEM((2,) + up_shape, F32),
                            pltpu.VMEM((2,) + down_shape, F32),
                            pltpu.VMEM(up_shape, BF16), pltpu.VMEM(up_shape, BF16), pltpu.VMEM(down_shape, BF16),
                            pltpu.SMEM((1,), jnp.int32), pltpu.SemaphoreType.DMA((2, 3))]),
        out_shape=jax.ShapeDtypeStruct((n_tiles * MOE_TILE * PACK_ROWS, LANES), jnp.uint32),
        compiler_params=pltpu.CompilerParams(dimension_semantics=("arbitrary",), vmem_limit_bytes=VMEM_LIMIT),
        name="moe_ffn",
    )(tile_expert, n_used, xsorted, w_gate, w_up, w_down)


def _combine_body(x1_ref, rt_ref, ya_ref, yb_ref, nf_ref, *rest):
    out_ref = rest[-1]
    rt = rt_ref[...]
    x1 = x1_ref[...]
    tm = x1.shape[0]

    x2 = (x1 + rt[:, 2:3] * _unpack_bf16_pairs(ya_ref.at[0], tm)
          + rt[:, 3:4] * _unpack_bf16_pairs(yb_ref.at[0], tm))
    out_ref[...] = _rms(x2, nf_ref[...])


def _combine(x1, rt, y_picks, nf, tm, rows, x_block, y_block, out_rows, out_block, out_buf=None):
    row = lambda w: pl.BlockSpec((tm, w), lambda i: (i + x_block, 0))
    pick = lambda k: pl.BlockSpec((1, tm * PACK_ROWS, LANES), lambda i: (k, i + y_block, 0))
    in_specs = [row(D_MODEL), row(LANES), pick(0), pick(1), pl.BlockSpec((1, D_MODEL), lambda i: (0, 0))]
    args = [x1, rt, y_picks, y_picks, nf]
    aliases = {}
    if out_buf is not None:
        in_specs.append(pl.BlockSpec(memory_space=pl.ANY))
        aliases[len(args)] = 0
        args.append(out_buf)
    return pl.pallas_call(
        _combine_body,
        grid=(rows // tm,),
        in_specs=in_specs,
        out_specs=pl.BlockSpec((tm, D_MODEL), lambda i: (i + out_block, 0)),
        out_shape=jax.ShapeDtypeStruct((out_rows, D_MODEL), F32),
        input_output_aliases=aliases,
        compiler_params=pltpu.CompilerParams(dimension_semantics=("parallel",), vmem_limit_bytes=VMEM_LIMIT),
        name="moe_combine",
    )(*args)


def _s5_tables(a_re, a_im, log_dt, b_re, b_im, c_re, c_im):
    dt = jnp.exp(log_dt)[:, None]
    mag = jnp.exp(a_re * dt)
    ab_re = mag * jnp.cos(a_im * dt)
    ab_im = mag * jnp.sin(a_im * dt)
    den = a_re * a_re + a_im * a_im
    nr = ab_re - 1.0
    q_re = (nr * a_re + ab_im * a_im) / den
    q_im = (ab_im * a_re - nr * a_im) / den
    bb_re = q_re[..., None] * b_re - q_im[..., None] * b_im
    bb_im = q_re[..., None] * b_im + q_im[..., None] * b_re
    nblk = S5_GROUPS // 16
    kw, nw = 16 * S5_GROUP_CH, 16 * S5_STATE
    same_group = (jnp.arange(kw)[:, None] // S5_GROUP_CH) == (jnp.arange(nw)[None, :] // S5_STATE)

    def in_map(bb):
        rows = bb.reshape(nblk, 16, S5_STATE, S5_GROUP_CH).transpose(0, 1, 3, 2).reshape(nblk, kw, S5_STATE)
        return jnp.where(same_group, jnp.tile(rows, (1, 1, 16)), 0.0)

    def out_map(cc):
        cols = cc.reshape(nblk, 16, S5_GROUP_CH, S5_STATE).transpose(0, 3, 1, 2).reshape(nblk, S5_STATE, kw)
        return jnp.where(same_group.T, jnp.tile(cols, (1, 16, 1)), 0.0)

    wb = jnp.concatenate([in_map(bb_re), in_map(bb_im)], axis=-1).astype(BF16)
    return (wb, ab_re.reshape(1, S5_LANES), ab_im.reshape(1, S5_LANES),
            out_map(c_re).astype(BF16), out_map(-c_im).astype(BF16))


def kernel(x_prompt, x_sample, state_ssd_conv, state_ssd_ssm, state_s5_re, state_s5_im, meta_tokens, norm_mix, w_in, conv_w, conv_b, dt_bias, a_log, d_ssd, ssd_norm, s5_a_re, s5_a_im, s5_log_dt, s5_b_re, s5_b_im, s5_c_re, s5_c_im, s5_d, w_glu, b_glu, s5_norm, w_out, norm_ffn, router_coarse_w, router_coarse_b, router_fine_w, router_fine_b, w_gate, w_up, w_down, norm_final):
    bp, seq, _ = x_prompt.shape
    bs = x_sample.shape[0]
    n_prompt = bp * seq
    n_tok = n_prompt + bs
    row2 = lambda v: v.reshape(1, -1)
    pad_heads = lambda v: jnp.pad(v, (0, LANES - SSD_HEADS)).reshape(1, LANES)

    w = w_in[0]
    o1, o2, o3 = SSD_WIDTH, SSD_WIDTH + SSD_CONV_DIM, SSD_WIDTH + SSD_CONV_DIM + SSD_HEADS
    wz, wx, wu = w[:, :o1].astype(BF16), w[:, o1:o2].astype(BF16), w[:, o3:].astype(BF16)
    wdt = jnp.pad(w[:, o2:o3], ((0, 0), (0, LANES - SSD_HEADS))).astype(BF16)
    g_mix = row2(norm_mix[0])
    cw, cb = conv_w[0], row2(conv_b[0])
    dtb, alog = pad_heads(dt_bias[0]), pad_heads(a_log[0])
    dexp = row2(jnp.repeat(d_ssd[0], SSD_HEAD_DIM))
    snrm = row2(ssd_norm[0])
    eexp = (jnp.arange(LANES)[:, None] == (jnp.arange(SSD_WIDTH) // SSD_HEAD_DIM)[None, :]).astype(BF16)
    wb5, ab_re, ab_im, wcr, wci = _s5_tables(s5_a_re[0], s5_a_im[0], s5_log_dt[0], s5_b_re[0], s5_b_im[0],
                                             s5_c_re[0], s5_c_im[0])
    d5, wglu, bglu, nrm5 = row2(s5_d[0]), w_glu[0].astype(BF16), row2(b_glu[0]), row2(s5_norm[0])
    wo_a, wo_b = w_out[0][:SSD_WIDTH].astype(BF16), w_out[0][SSD_WIDTH:].astype(BF16)
    w_r = jnp.concatenate([router_coarse_w[0], router_fine_w[0].transpose(1, 0, 2).reshape(D_MODEL, MOE_EXPERTS)], axis=1)
    w_r = jnp.pad(w_r, ((0, 0), (0, LANES - w_r.shape[1])))
    wrh = w_r.astype(BF16)
    wrl = (w_r - wrh.astype(F32)).astype(BF16)
    b_r = jnp.concatenate([router_coarse_b[0], router_fine_b[0].reshape(-1)])
    b_r = jnp.pad(b_r, (0, LANES - b_r.shape[0])).reshape(1, LANES)

    zp, xbcp, dtp, up = _in_proj(x_prompt.reshape(n_prompt, D_MODEL), g_mix, wz, wx, wdt, wu, IN_PROJ_TILE, BF16, F32)
    xsm = jnp.concatenate([x_sample.reshape(bs, D_MODEL), meta_tokens], axis=0)
    zs, xbcs, dts, us = _in_proj(xsm, g_mix, wz, wx, wdt, wu, xsm.shape[0], F32, F32)

    front = SSD_CHUNK - N_META
    padf = lambda a: jnp.pad(a[bs:], ((front, 0), (0, 0)))[None]
    gw = SSD_HPG * SSD_HEAD_DIM
    ssd_consts = (cw, cb, dtb, alog, dexp, snrm, eexp)
    _, ctail_m, _, ht_m = _ssd_chunked(
        padf(xbcs).astype(BF16), padf(dts), jnp.zeros((1, SSD_CHUNK, SSD_WIDTH), F32),
        jnp.zeros((1, SUBLANES, SSD_CONV_DIM), F32), jnp.zeros((1, SSD_GROUPS, SSD_STATE, gw), F32),
        *ssd_consts, mask_rows=front)
    y_ssd_p, ctail_p, ssm_p, _ = _ssd_chunked(
        xbcp.reshape(bp, seq, SSD_CONV_DIM), dtp.reshape(bp, seq, LANES), zp.reshape(bp, seq, SSD_WIDTH),
        ctail_m, ht_m, *ssd_consts, mask_rows=0)

    abr8, abi8 = jnp.broadcast_to(ab_re, (bp, S5_LANES)), jnp.broadcast_to(ab_im, (bp, S5_LANES))
    um8 = jnp.repeat(us[bs:], bp, axis=0).astype(BF16)
    y_s5_p, s5re_p, s5im_p = _s5_seq(up.reshape(bp, seq, S5_WIDTH), um8, wb5, abr8, abi8,
                                     wcr, wci, d5, wglu, bglu, nrm5)

    cst = state_ssd_conv[0]
    xt_s, dt_s, dec_s, bc, xs_s = _ssd_step_prep(xbcs[:bs], cst[:, 0], cst[:, 1], cst[:, 2], dts[:bs],
                                                 cw, cb, dtb, alog)
    ssm_s, y_core = _ssd_step(dt_s[:, :SSD_HEADS].reshape(-1), dec_s[:, :SSD_HEADS].reshape(-1),
                              state_ssd_ssm[0], xt_s, bc)
    y_ssd_s, y_s5_s, s5re_s, s5im_s = _sample_post(
        y_core, xs_s, zs[:bs], dexp, snrm, us[:bs], state_s5_re[0].reshape(bs, S5_LANES),
        state_s5_im[0].reshape(bs, S5_LANES), wb5, ab_re, ab_im, wcr, wci, d5, wglu, bglu, nrm5)

    route_consts = (wo_a, wo_b, row2(norm_ffn[0]), wrh, wrl, b_r)
    n_tiles = -(-2 * n_tok // MOE_TILE) + MOE_EXPERTS
    n_tiles = -(-n_tiles // MOE_TILES_PER_STEP) * MOE_TILES_PER_STEP
    x1, xn, rt, pos, meta = _mix_route(
        (x_prompt.reshape(n_prompt, D_MODEL), y_ssd_p.reshape(n_prompt, SSD_WIDTH), y_s5_p.reshape(n_prompt, S5_WIDTH)),
        (x_sample.reshape(bs, D_MODEL), y_ssd_s, y_s5_s), route_consts, TOK_TILE, n_tiles)

    pos_a, pos_b = pos[0], pos[1]
    tile_expert, n_used = meta[0, :n_tiles], meta[1, :1]
    xsorted = _sc_dispatch(xn, pos_a, pos_b, n_tiles * MOE_TILE)
    ysorted = _moe_ffn(tile_expert, n_used, xsorted.reshape(-1, LANES), w_gate[0], w_up[0], w_down[0])
    nfin = row2(norm_final)

    half = n_prompt // 2

    def collect(lo, hi, ch):
        picks = jnp.concatenate([pos_a[lo:hi], pos_b[lo:hi]])
        packed_rows = ysorted.reshape(-1, PACK_ROWS, LANES)
        return _sc_collect(packed_rows, picks, ch).reshape(2, (hi - lo) * PACK_ROWS, LANES)

    picks_1 = collect(0, half, SC_COLLECT_ROWS[0])
    picks_2 = collect(half, n_tok, SC_COLLECT_ROWS[1])
    blocks = half // TOK_TILE
    y_p = _combine(x1, rt, picks_1, nfin, TOK_TILE, half, 0, 0, n_prompt, 0)
    y_p = _combine(x1, rt, picks_2, nfin, TOK_TILE, half, blocks, 0, n_prompt, blocks, out_buf=y_p)
    y_s = _combine(x1, rt, picks_2, nfin, bs, bs, n_prompt // bs, half // bs, bs, 0)

    s5_state = lambda a, b: a.reshape(1, b, S5_GROUPS, S5_STATE)
    new_conv_s = jnp.stack([cst[:, 1], cst[:, 2], xbcs[:bs]], axis=1)[None]
    return (y_p.reshape(bp, seq, D_MODEL), y_s.reshape(bs, 1, D_MODEL),
            ctail_p[:, SUBLANES - (SSD_CONV - 1):][None], ssm_p[None], s5_state(s5re_p, bp), s5_state(s5im_p, bp),
            new_conv_s, ssm_s[None], s5_state(s5re_s, bs), s5_state(s5im_s, bs))
```

```python
import functools

import jax
import jax.numpy as jnp
from jax import lax
from jax.experimental import pallas as pl
from jax.experimental.pallas import tpu as pltpu
from jax.experimental.pallas import tpu_sc as plsc

F32, BF16 = jnp.float32, jnp.bfloat16

D_MODEL = 1024
N_META = 16
SSD_WIDTH = 1024
SSD_HEAD_DIM = 64
SSD_HEADS = 16
SSD_GROUPS = 2
SSD_HPG = SSD_HEADS // SSD_GROUPS
SSD_STATE = 128
SSD_CONV = 4
SSD_CHUNK = 128
SSD_CONV_DIM = SSD_WIDTH + 2 * SSD_GROUPS * SSD_STATE
S5_WIDTH = 1024
S5_GROUP_CH = 16
S5_GROUPS = 64
S5_STATE = 64
S5_LANES = S5_GROUPS * S5_STATE
MOE_GROUPS = 4
MOE_EPG = 8
MOE_EXPERTS = MOE_GROUPS * MOE_EPG
MOE_D_FF = 512
EPS = 1e-6

LANES = 128
SUBLANES = 8
VMEM_LIMIT = 56 * 1024 * 1024

SSD_CHUNKS_PER_STEP = 4
S5_TIME_TILE = 64
S5_SCAN_LANES = 512
MOE_TILE = 256
MOE_TILES_PER_STEP = 4
SLAB_ROWS = D_MODEL // LANES
PACK_ROWS = SLAB_ROWS // 2
SC_CORES = 2
SC_SUBCORES = 16
SC_WORKERS = SC_CORES * SC_SUBCORES
SC_DISPATCH_ROWS = 64
SC_COLLECT_ROWS = (64, 104)
TOK_TILE = 512
IN_PROJ_TILE = 1024


def _dot(a, b):
    return jnp.dot(a, b, preferred_element_type=F32)


def _rms(x, g):
    return x * lax.rsqrt(jnp.mean(x * x, axis=-1, keepdims=True) + EPS) * g


def _softplus(x):
    return jnp.maximum(x, 0.0) + jnp.log1p(jnp.exp(-jnp.abs(x)))


def _split3(x):
    hi = x.astype(BF16)
    r = x - hi.astype(F32)
    mid = r.astype(BF16)
    lo = (r - mid.astype(F32)).astype(BF16)
    return hi, mid, lo


def _dot3(x, w):
    hi, mid, lo = _split3(x)
    return _dot(hi, w) + _dot(mid, w) + _dot(lo, w)


def _dot3_left(w, x):
    hi, mid, lo = _split3(x)
    return _dot(w, hi) + _dot(w, mid) + _dot(w, lo)


def _pack_bf16_pairs(x):
    bits = pltpu.bitcast(x.astype(BF16).astype(F32), jnp.uint32)
    half = x.shape[1] // 2
    return (bits[:, :half] & jnp.uint32(0xFFFF0000)) | (bits[:, half:] >> jnp.uint32(16))


def _unpack_bf16_pairs(ref, rows):
    words = [ref[pl.ds(j, rows, stride=PACK_ROWS), :] for j in range(PACK_ROWS)]
    high = [pltpu.bitcast(w & jnp.uint32(0xFFFF0000), F32) for w in words]
    low = [pltpu.bitcast(w << jnp.uint32(16), F32) for w in words]
    return jnp.concatenate(high + low, axis=-1)


def _full_spec(a):
    nd = a.ndim
    return pl.BlockSpec(a.shape, lambda *_: (0,) * nd)


def _resident_spec(a):
    nd = a.ndim
    return pl.BlockSpec(a.shape, lambda *_: (0,) * nd, pipeline_mode=pl.Buffered(1))


def _in_proj_body(x_ref, g_ref, wz_ref, wx_ref, wdt_ref, wu_ref, z_ref, xbc_ref, dt_ref, u_ref):
    xb = _rms(x_ref[...], g_ref[...]).astype(BF16)
    z_ref[...] = _dot(xb, wz_ref[...]).astype(z_ref.dtype)
    xbc_ref[...] = _dot(xb, wx_ref[...]).astype(xbc_ref.dtype)
    dt_ref[...] = _dot(xb, wdt_ref[...])
    u_ref[...] = _dot(xb, wu_ref[...]).astype(u_ref.dtype)


def _in_proj(x2d, g, wz, wx, wdt, wu, tm, act_dtype, u_dtype):
    rows = x2d.shape[0]
    row = lambda w: pl.BlockSpec((tm, w), lambda i: (i, 0))
    return pl.pallas_call(
        _in_proj_body,
        grid=(rows // tm,),
        in_specs=[row(D_MODEL)] + [_resident_spec(a) for a in (g, wz, wx, wdt, wu)],
        out_specs=[row(SSD_WIDTH), row(SSD_CONV_DIM), row(LANES), row(S5_WIDTH)],
        out_shape=[jax.ShapeDtypeStruct((rows, SSD_WIDTH), act_dtype),
                   jax.ShapeDtypeStruct((rows, SSD_CONV_DIM), act_dtype),
                   jax.ShapeDtypeStruct((rows, LANES), F32),
                   jax.ShapeDtypeStruct((rows, S5_WIDTH), u_dtype)],
        compiler_params=pltpu.CompilerParams(dimension_semantics=("parallel",), vmem_limit_bytes=VMEM_LIMIT),
        name="in_proj",
    )(x2d, g, wz, wx, wdt, wu)


def _ssd_body(mask_rows, per_step, *refs):
    for k in range(per_step):
        _ssd_chunk(mask_rows, per_step, k, *refs)


def _ssd_chunk(mask_rows, per_step, k, xbc_ref, dt_ref, z_ref, cinit_ref, hinit_ref, cw_ref, cb_ref, dtb_ref,
               alog_ref, dexp_ref, nrm_ref, eexp_ref, y_ref, ctail_ref, st_ref, hto_ref, xwin, hT):
    L = SSD_CHUNK
    c = pl.program_id(1) * per_step + k
    n_chunks = pl.num_programs(1) * per_step
    window = pl.ds(k * L, L)
    xbc_ref, dt_ref, z_ref, y_ref = (r.at[:, window, :] for r in (xbc_ref, dt_ref, z_ref, y_ref))

    @pl.when(c == 0)
    def _init():
        xwin[...] = cinit_ref[0]
        hT[...] = hinit_ref[0]

    x_b = xbc_ref[0]
    x_f = x_b.astype(F32)
    taps = SSD_CONV - 1
    m_i = lax.broadcasted_iota(jnp.int32, (taps * L, L), 0)
    r_i = lax.broadcasted_iota(jnp.int32, (taps * L, L), 1)
    shift = (r_i + (taps - m_i // L) == m_i % L).astype(BF16)
    shifted = _dot(shift, x_b)
    acc = cb_ref[...] + x_f * cw_ref[taps:taps + 1, :]
    for k in range(taps):
        acc = acc + shifted[k * L:(k + 1) * L, :] * cw_ref[k:k + 1, :]
    joint = jnp.concatenate([xwin[...], x_f[0:SUBLANES, :]], axis=0)
    row8 = lax.broadcasted_iota(jnp.int32, (SUBLANES, 1), 0)
    head = acc[0:SUBLANES, :]
    for k in range(taps):
        d = taps - k
        head = head + jnp.where(row8 < d, joint[SUBLANES - d:2 * SUBLANES - d, :], 0.0) * cw_ref[k:k + 1, :]
    acc = jnp.concatenate([head, acc[SUBLANES:, :]], axis=0)
    tail = x_f[L - SUBLANES:, :]
    xwin[...] = tail
    ctail_ref[0] = tail

    xact = acc * jax.nn.sigmoid(acc)
    dt = _softplus(dt_ref[0] + dtb_ref[...])
    if mask_rows:
        valid = lax.broadcasted_iota(jnp.int32, (L, 1), 0) >= mask_rows
        xact = jnp.where(valid, xact, 0.0)
        dt = jnp.where(valid, dt, 0.0)

    a_neg = -jnp.exp(alog_ref[...])
    dA = dt * a_neg
    row_i = lax.broadcasted_iota(jnp.int32, (L, L), 0)
    col_i = lax.broadcasted_iota(jnp.int32, (L, L), 1)
    causal = row_i >= col_i
    tril = causal.astype(BF16)
    cs = _dot3_left(tril, dA)
    csT = cs.T
    dtT = dt.T
    ecs = jnp.exp(cs)
    wdec = jnp.exp(cs[L - 1:L, :] - cs) * dt
    eexp = eexp_ref[...]
    ecs_e = _dot3(ecs, eexp)
    wdec_e = _dot3(wdec, eexp)
    lane = lax.broadcasted_iota(jnp.int32, (L, LANES), 1)
    first_half = lane < SSD_HEAD_DIM

    gw = SSD_HPG * SSD_HEAD_DIM
    y_groups = []
    for g in range(SSD_GROUPS):
        b_g = xact[:, SSD_WIDTH + g * SSD_STATE: SSD_WIDTH + (g + 1) * SSD_STATE]
        c_g = xact[:, SSD_WIDTH + (SSD_GROUPS + g) * SSD_STATE: SSD_WIDTH + (SSD_GROUPS + g + 1) * SSD_STATE]
        b_b = b_g.astype(BF16)
        c_b = c_g.astype(BF16)
        cb = lax.dot_general(c_b, b_b, (((1,), (1,)), ((), ())), preferred_element_type=F32)
        xs_g = xact[:, g * gw:(g + 1) * gw]
        h_prev = hT[g]
        y_off = _dot(c_b, h_prev.astype(BF16)) * ecs_e[:, g * gw:(g + 1) * gw]
        xdec = (xs_g * wdec_e[:, g * gw:(g + 1) * gw]).astype(BF16)
        hT[g] = h_prev * ecs_e[L - 1:L, g * gw:(g + 1) * gw] + _dot(b_g.T.astype(BF16), xdec)
        pieces = []
        for j in range(SSD_HPG // 2):
            xs_pair = xs_g[:, j * LANES:(j + 1) * LANES]
            halves = (jnp.where(first_half, xs_pair, 0.0).astype(BF16),
                      jnp.where(first_half, 0.0, xs_pair).astype(BF16))
            yd = None
            for t in range(2):
                h = g * SSD_HPG + 2 * j + t
                seg = cs[:, h:h + 1] - csT[h:h + 1, :]
                lmat = jnp.exp(jnp.where(causal, seg, -jnp.inf))
                m = (cb * lmat * dtT[h:h + 1, :]).astype(BF16)
                part = _dot(m, halves[t])
                yd = part if yd is None else yd + part
            pieces.append(yd)
        y_groups.append(jnp.concatenate(pieces, axis=-1) + y_off + dexp_ref[:, g * gw:(g + 1) * gw] * xs_g)
    y = jnp.concatenate(y_groups, axis=-1)
    z = z_ref[0].astype(F32)
    y_ref[0] = _rms(y * (z * jax.nn.sigmoid(z)), nrm_ref[...]).astype(y_ref.dtype)

    @pl.when(c == n_chunks - 1)
    def _emit():
        hto_ref[0] = hT[...]
        for g in range(SSD_GROUPS):
            t = hT[g].T
            for k in range(SSD_HPG):
                st_ref[0, g * SSD_HPG + k] = t[k * SSD_HEAD_DIM:(k + 1) * SSD_HEAD_DIM, :]


def _ssd_chunked(xbc, dt, z, cinit, hinit, cw, cb, dtb, alog, dexp, nrm, eexp, mask_rows):
    bsz, seq, _ = xbc.shape
    nc = seq // SSD_CHUNK
    per_step = SSD_CHUNKS_PER_STEP if nc % SSD_CHUNKS_PER_STEP == 0 else 1
    gw = SSD_HPG * SSD_HEAD_DIM
    blk = lambda w: pl.BlockSpec((1, per_step * SSD_CHUNK, w), lambda b, c: (b, c, 0))
    return pl.pallas_call(
        functools.partial(_ssd_body, mask_rows, per_step),
        grid=(bsz, nc // per_step),
        in_specs=[blk(SSD_CONV_DIM), blk(LANES), blk(SSD_WIDTH),
                  pl.BlockSpec((1, SUBLANES, SSD_CONV_DIM), lambda b, c: (0, 0, 0)),
                  pl.BlockSpec((1, SSD_GROUPS, SSD_STATE, gw), lambda b, c: (0, 0, 0, 0)),
                  _full_spec(cw), _full_spec(cb), _full_spec(dtb), _full_spec(alog),
                  _full_spec(dexp), _full_spec(nrm), _full_spec(eexp)],
        out_specs=[blk(SSD_WIDTH),
                   pl.BlockSpec((1, SUBLANES, SSD_CONV_DIM), lambda b, c: (b, 0, 0)),
                   pl.BlockSpec((1, SSD_HEADS, SSD_HEAD_DIM, SSD_STATE), lambda b, c: (b, 0, 0, 0)),
                   pl.BlockSpec((1, SSD_GROUPS, SSD_STATE, gw), lambda b, c: (b, 0, 0, 0))],
        out_shape=[jax.ShapeDtypeStruct((bsz, seq, SSD_WIDTH), BF16),
                   jax.ShapeDtypeStruct((bsz, SUBLANES, SSD_CONV_DIM), F32),
                   jax.ShapeDtypeStruct((bsz, SSD_HEADS, SSD_HEAD_DIM, SSD_STATE), F32),
                   jax.ShapeDtypeStruct((bsz, SSD_GROUPS, SSD_STATE, gw), F32)],
        scratch_shapes=[pltpu.VMEM((SUBLANES, SSD_CONV_DIM), F32),
                        pltpu.VMEM((SSD_GROUPS, SSD_STATE, gw), F32)],
        compiler_params=pltpu.CompilerParams(dimension_semantics=("parallel", "arbitrary"),
                                             vmem_limit_bytes=VMEM_LIMIT),
        name="ssd_chunked",
    )(xbc, dt, z, cinit, hinit, cw, cb, dtb, alog, dexp, nrm, eexp)


def _ssd_step_prep_body(xbc_ref, c0_ref, c1_ref, c2_ref, dt_ref, cw_ref, cb_ref, dtb_ref, alog_ref,
                        xt_ref, dt_out_ref, dec_ref, bc_ref, xs_ref):
    acc = cb_ref[...]
    for k, r in enumerate((c0_ref, c1_ref, c2_ref, xbc_ref)):
        acc = acc + r[...] * cw_ref[k:k + 1, :]
    xact = acc * jax.nn.sigmoid(acc)
    xs = xact[:, :SSD_WIDTH]
    dt = _softplus(dt_ref[...] + dtb_ref[...])
    dt_out_ref[...] = dt
    dec_ref[...] = jnp.exp(dt * -jnp.exp(alog_ref[...]))
    bc_ref[...] = xact[:, SSD_WIDTH:]
    xs_ref[...] = xs
    xt_ref[...] = xs.T.astype(xt_ref.dtype)


def _ssd_step_prep(xbc, c0, c1, c2, dt, cw, cb, dtb, alog):
    n = xbc.shape[0]
    args = (xbc, c0, c1, c2, dt, cw, cb, dtb, alog)
    spec = lambda r, w: pl.BlockSpec((r, w), lambda: (0, 0))
    return pl.pallas_call(
        _ssd_step_prep_body,
        in_specs=[_full_spec(a) for a in args],
        out_specs=[spec(SSD_WIDTH, n), spec(n, LANES), spec(n, LANES), spec(n, 2 * SSD_GROUPS * SSD_STATE),
                   spec(n, SSD_WIDTH)],
        out_shape=[jax.ShapeDtypeStruct((SSD_WIDTH, n), BF16), jax.ShapeDtypeStruct((n, LANES), F32),
                   jax.ShapeDtypeStruct((n, LANES), F32),
                   jax.ShapeDtypeStruct((n, 2 * SSD_GROUPS * SSD_STATE), F32),
                   jax.ShapeDtypeStruct((n, SSD_WIDTH), F32)],
        compiler_params=pltpu.CompilerParams(vmem_limit_bytes=VMEM_LIMIT),
        name="ssd_step_prep",
    )(*args)


def _ssd_step_body(dt_ref, dec_ref, st_ref, xt_ref, bc_ref, so_ref, y_ref):
    n = xt_ref.shape[1]
    gw = SSD_HPG * SSD_HEAD_DIM
    blk = pl.program_id(0)
    seq_id = lax.broadcasted_iota(jnp.int32, (n, SSD_STATE), 0)
    sub_id = lax.broadcasted_iota(jnp.int32, (SUBLANES, gw), 0)
    base = pl.multiple_of(blk * SUBLANES, SUBLANES)
    y_acc = [jnp.zeros((SUBLANES, gw), F32) for _ in range(SSD_GROUPS)]
    for i in range(SUBLANES):
        s = blk * SUBLANES + i
        for g in range(SSD_GROUPS):
            b_all = bc_ref[:, g * SSD_STATE:(g + 1) * SSD_STATE]
            rhs = jnp.where(seq_id == s, b_all, 0.0).astype(BF16)
            outer = _dot(xt_ref[g * gw:(g + 1) * gw, :], rhs)
            news = []
            for k in range(SSD_HPG):
                h = g * SSD_HPG + k
                new = (dec_ref[s * SSD_HEADS + h] * st_ref[i, h]
                       + dt_ref[s * SSD_HEADS + h] * outer[k * SSD_HEAD_DIM:(k + 1) * SSD_HEAD_DIM, :])
                so_ref[i, h] = new
                news.append(new)
            new_g = jnp.concatenate(news, axis=0).astype(BF16)
            c_lo = (SSD_GROUPS + g) * SSD_STATE
            c_blk = bc_ref[pl.ds(base, SUBLANES), c_lo:c_lo + SSD_STATE].astype(BF16)
            r = lax.dot_general(c_blk, new_g, (((1,), (1,)), ((), ())), preferred_element_type=F32)
            y_acc[g] = y_acc[g] + jnp.where(sub_id == i, r, 0.0)
    y_ref[...] = jnp.concatenate(y_acc, axis=-1)


def _ssd_step(dt_flat, dec_flat, state, xt, bc):
    n = state.shape[0]
    st_spec = pl.BlockSpec((SUBLANES, SSD_HEADS, SSD_HEAD_DIM, SSD_STATE), lambda i, *_: (i, 0, 0, 0))
    return pl.pallas_call(
        _ssd_step_body,
        grid_spec=pltpu.PrefetchScalarGridSpec(
            num_scalar_prefetch=2,
            grid=(n // SUBLANES,),
            in_specs=[st_spec, pl.BlockSpec(xt.shape, lambda i, *_: (0, 0)),
                      pl.BlockSpec(bc.shape, lambda i, *_: (0, 0))],
            out_specs=[st_spec, pl.BlockSpec((SUBLANES, SSD_WIDTH), lambda i, *_: (i, 0))]),
        out_shape=[jax.ShapeDtypeStruct(state.shape, F32), jax.ShapeDtypeStruct((n, SSD_WIDTH), F32)],
        compiler_params=pltpu.CompilerParams(dimension_semantics=("parallel",), vmem_limit_bytes=VMEM_LIMIT),
        name="ssd_step",
    )(dt_flat, dec_flat, state, xt, bc)


def _s5_project_in(u_b16, wb_ref, store):
    kw = 16 * S5_GROUP_CH
    nw = 16 * S5_STATE
    for j in range(S5_WIDTH // kw):
        r = _dot(u_b16[:, j * kw:(j + 1) * kw], wb_ref[j])
        store(j, r[:, :nw], r[:, nw:])


def _s5_tail(hre_of, him_of, u_f32, wcr_ref, wci_ref, d_ref, wglu_ref, bglu_ref, nrm_ref):
    cols = []
    for j in range(wcr_ref.shape[0]):
        cols.append(_dot(hre_of(j).astype(BF16), wcr_ref[j]) + _dot(him_of(j).astype(BF16), wci_ref[j]))
    return _s5_finish(cols, u_f32, d_ref, wglu_ref, bglu_ref, nrm_ref)


def _s5_finish(cols, u_f32, d_ref, wglu_ref, bglu_ref, nrm_ref):
    y = jnp.concatenate(cols, axis=-1) + d_ref[...] * u_f32
    y = jax.nn.gelu(y)
    y = y * jax.nn.sigmoid(_dot(y.astype(BF16), wglu_ref[...]) + bglu_ref[...])
    return _rms(y, nrm_ref[...])


def _s5_seq_body(u_hbm, um_ref, wb_ref, abr_ref, abi_ref, wcr_ref, wci_ref, d_ref, wglu_ref, bglu_ref, nrm_ref,
                 y_hbm, sre_ref, sim_ref, ubuf, ybuf, bu, h, in_sems, out_sems):
    j = pl.program_id(0)
    last = pl.num_programs(0) - 1
    lc, bsz = ubuf.shape[1], ubuf.shape[2]
    rows = lc * bsz
    nw = 16 * S5_STATE

    def in_copy(step, b):
        return pltpu.make_async_copy(u_hbm.at[b, pl.ds(step * lc, lc), :], ubuf.at[step % 2, :, b, :],
                                     in_sems.at[step % 2, b])

    def out_copy(step, b):
        return pltpu.make_async_copy(ybuf.at[step % 2, :, b, :], y_hbm.at[b, pl.ds(step * lc, lc), :],
                                     out_sems.at[step % 2, b])

    def project_in(u_b16, nrows):
        def store(jj, re, im):
            bu[0:nrows, jj * nw:(jj + 1) * nw] = re
            bu[0:nrows, S5_LANES + jj * nw:S5_LANES + (jj + 1) * nw] = im
        _s5_project_in(u_b16, wb_ref, store)

    def scan(nsteps):
        for k in range(S5_LANES // S5_SCAN_LANES):
            sl_r = pl.ds(k * S5_SCAN_LANES, S5_SCAN_LANES)
            sl_i = pl.ds(S5_LANES + k * S5_SCAN_LANES, S5_SCAN_LANES)
            ar = abr_ref[:, sl_r]
            ai = abi_ref[:, sl_r]

            def step(l, carry):
                hr, hi = carry
                slab = pl.ds(pl.multiple_of(l * bsz, bsz), bsz)
                nr = ar * hr - ai * hi + bu[slab, sl_r]
                ni = ar * hi + ai * hr + bu[slab, sl_i]
                bu[slab, sl_r] = nr
                bu[slab, sl_i] = ni
                return nr, ni

            hr, hi = lax.fori_loop(0, nsteps, step, (h[:, sl_r], h[:, sl_i]))
            h[:, sl_r] = hr
            h[:, sl_i] = hi

    @pl.when(j == 0)
    def _first():
        for b in range(bsz):
            in_copy(0, b).start()
        h[...] = jnp.zeros_like(h)
        project_in(um_ref[...], N_META * bsz)
        scan(N_META)

    @pl.when(j < last)
    def _prefetch():
        for b in range(bsz):
            in_copy(j + 1, b).start()

    for b in range(bsz):
        in_copy(j, b).wait()
    u2 = ubuf[j % 2].reshape(rows, S5_WIDTH)
    u_b16 = u2.astype(BF16)
    kw = 16 * S5_GROUP_CH

    def project_block(jj):
        r = _dot(u_b16[:, jj * kw:(jj + 1) * kw], wb_ref[jj])
        bu[0:rows, jj * nw:(jj + 1) * nw] = r[:, :nw]
        bu[0:rows, S5_LANES + jj * nw:S5_LANES + (jj + 1) * nw] = r[:, nw:]

    def scan_block(jj):
        for k in range(nw // S5_SCAN_LANES):
            lo = jj * nw + k * S5_SCAN_LANES
            sl_r = slice(lo, lo + S5_SCAN_LANES)
            sl_i = slice(S5_LANES + lo, S5_LANES + lo + S5_SCAN_LANES)
            ar, ai = abr_ref[:, sl_r], abi_ref[:, sl_r]
            hr, hi = h[:, sl_r], h[:, sl_i]
            for l in range(lc):
                slab = slice(l * bsz, (l + 1) * bsz)
                hr, hi = (ar * hr - ai * hi + bu[slab, sl_r], ar * hi + ai * hr + bu[slab, sl_i])
                bu[slab, sl_r] = hr
                bu[slab, sl_i] = hi
            h[:, sl_r] = hr
            h[:, sl_i] = hi

    def readout_block(jj):
        return (_dot(bu[:, jj * nw:(jj + 1) * nw].astype(BF16), wcr_ref[jj])
                + _dot(bu[:, S5_LANES + jj * nw:S5_LANES + (jj + 1) * nw].astype(BF16), wci_ref[jj]))

    n_blocks = S5_WIDTH // kw
    project_block(0)
    cols = []
    for jj in range(n_blocks):
        if jj + 1 < n_blocks:
            project_block(jj + 1)
        scan_block(jj)
        cols.append(readout_block(jj))
    y = _s5_finish(cols, u2, d_ref, wglu_ref, bglu_ref, nrm_ref)
    ybuf[j % 2] = y.reshape(lc, bsz, S5_WIDTH)
    for b in range(bsz):
        out_copy(j, b).start()

    @pl.when(j > 0)
    def _wait_previous_out():
        for b in range(bsz):
            out_copy(j - 1, b).wait()

    @pl.when(j == last)
    def _emit():
        for b in range(bsz):
            out_copy(j, b).wait()
        sre_ref[...] = h[:, 0:S5_LANES]
        sim_ref[...] = h[:, S5_LANES:]


def _s5_seq(u, um, wb, abr, abi, wcr, wci, d, wglu, bglu, nrm):
    bsz, seq, _ = u.shape
    lc = S5_TIME_TILE
    consts = (um, wb, abr, abi, wcr, wci, d, wglu, bglu, nrm)
    st = pl.BlockSpec((bsz, S5_LANES), lambda j: (0, 0))
    return pl.pallas_call(
        _s5_seq_body,
        grid=(seq // lc,),
        in_specs=[pl.BlockSpec(memory_space=pl.ANY)] + [_resident_spec(a) for a in consts],
        out_specs=[pl.BlockSpec(memory_space=pl.ANY), st, st],
        out_shape=[jax.ShapeDtypeStruct((bsz, seq, S5_WIDTH), F32),
                   jax.ShapeDtypeStruct((bsz, S5_LANES), F32), jax.ShapeDtypeStruct((bsz, S5_LANES), F32)],
        scratch_shapes=[pltpu.VMEM((2, lc, bsz, S5_WIDTH), F32), pltpu.VMEM((2, lc, bsz, S5_WIDTH), F32),
                        pltpu.VMEM((lc * bsz, 2 * S5_LANES), F32), pltpu.VMEM((bsz, 2 * S5_LANES), F32),
                        pltpu.SemaphoreType.DMA((2, bsz)), pltpu.SemaphoreType.DMA((2, bsz))],
        compiler_params=pltpu.CompilerParams(dimension_semantics=("arbitrary",), vmem_limit_bytes=VMEM_LIMIT),
        name="s5_seq",
    )(u, *consts)


def _sample_post_body(yc_ref, xs_ref, z_ref, dexp_ref, snrm_ref, u_ref, hr_ref, hi_ref, wb_ref, abr_ref, abi_ref,
                      wcr_ref, wci_ref, d_ref, wglu_ref, bglu_ref, nrm_ref,
                      yssd_ref, ys5_ref, nre_ref, nim_ref):
    z = z_ref[...]
    y = yc_ref[...] + dexp_ref[...] * xs_ref[...]
    yssd_ref[...] = _rms(y * (z * jax.nn.sigmoid(z)), snrm_ref[...]).astype(yssd_ref.dtype)

    u = u_ref[...]
    nw = 16 * S5_STATE
    ar, ai = abr_ref[...], abi_ref[...]

    def store(jj, re, im):
        sl = slice(jj * nw, (jj + 1) * nw)
        h0r, h0i = hr_ref[:, sl], hi_ref[:, sl]
        nre_ref[:, sl] = ar[:, sl] * h0r - ai[:, sl] * h0i + re
        nim_ref[:, sl] = ar[:, sl] * h0i + ai[:, sl] * h0r + im

    _s5_project_in(u.astype(BF16), wb_ref, store)
    slab = lambda ref: (lambda jj: ref[:, jj * nw:(jj + 1) * nw])
    y5 = _s5_tail(slab(nre_ref), slab(nim_ref), u, wcr_ref, wci_ref, d_ref, wglu_ref, bglu_ref, nrm_ref)
    ys5_ref[...] = y5.astype(ys5_ref.dtype)


def _sample_post(yc, xs, z, dexp, snrm, u, h0r, h0i, wb, abr1, abi1, wcr, wci, d, wglu, bglu, nrm):
    n = yc.shape[0]
    args = (yc, xs, z, dexp, snrm, u, h0r, h0i, wb, abr1, abi1, wcr, wci, d, wglu, bglu, nrm)
    spec = lambda w: pl.BlockSpec((n, w), lambda: (0, 0))
    return pl.pallas_call(
        _sample_post_body,
        in_specs=[_full_spec(a) for a in args],
        out_specs=[spec(SSD_WIDTH), spec(S5_WIDTH), spec(S5_LANES), spec(S5_LANES)],
        out_shape=[jax.ShapeDtypeStruct((n, SSD_WIDTH), BF16), jax.ShapeDtypeStruct((n, S5_WIDTH), BF16),
                   jax.ShapeDtypeStruct((n, S5_LANES), F32), jax.ShapeDtypeStruct((n, S5_LANES), F32)],
        compiler_params=pltpu.CompilerParams(vmem_limit_bytes=VMEM_LIMIT),
        name="sample_post",
    )(*args)


def _mix_route_body(n_blocks, n_sorted, xp_ref, ysp_ref, y5p_ref, xs_ref, yss_ref, y5s_ref, *refs):
    consts = refs[:6]
    x1_ref, xn_hbm, rt_ref, pos_ref, meta_ref, carry, fields, xbuf, sems = refs[6:]
    i = pl.program_id(0)
    tm, n_sample = xp_ref.shape[0], xs_ref.shape[0]
    col0 = pl.multiple_of(i * tm, LANES)

    def xn_copy(step, rows, j):
        return pltpu.make_async_copy(xbuf.at[step % 2, pl.ds(0, rows), pl.ds(j * LANES, LANES)],
                                     xn_hbm.at[pl.ds(step * tm, rows), j, :], sems.at[step % 2, j])

    @pl.when(i == 0)
    def _init():
        carry[...] = jnp.zeros_like(carry)

    @pl.when(i < n_blocks)
    def _prompt_rows():
        _mix_route_compute(xp_ref, ysp_ref, y5p_ref, *consts, x1_ref, rt_ref, carry, xbuf.at[i % 2], fields, col0)
        for j in range(PACK_ROWS):
            xn_copy(i, tm, j).start()

    @pl.when(i == n_blocks)
    def _sample_rows():
        _mix_route_compute(xs_ref, yss_ref, y5s_ref, *consts, x1_ref, rt_ref, carry, xbuf.at[i % 2], fields, col0)
        for j in range(PACK_ROWS):
            xn_copy(i, n_sample, j).start()
        _route_layout(carry, fields, pos_ref, meta_ref, n_sorted)
        for j in range(PACK_ROWS):
            xn_copy(i, n_sample, j).wait()

    @pl.when(i > 0)
    def _wait_previous_rows():
        for j in range(PACK_ROWS):
            xn_copy(i - 1, tm, j).wait()


def _route_layout(carry, fields, pos_ref, meta_ref, n_sorted):
    counts = carry[...]
    tiles_per = jnp.floor((counts + (MOE_TILE - 1)) * (1.0 / MOE_TILE))
    upto = lax.broadcasted_iota(jnp.int32, (LANES, LANES), 0) <= lax.broadcasted_iota(jnp.int32, (LANES, LANES), 1)
    tile_end = _dot(tiles_per.astype(BF16), upto.astype(BF16))
    pstart = (tile_end - tiles_per) * MOE_TILE
    n_used = tile_end[:, MOE_EXPERTS - 1:MOE_EXPERTS]

    f = fields[...]
    first_row = jnp.zeros_like(f)
    tile_id = jnp.minimum(lax.broadcasted_iota(jnp.int32, meta_ref.shape, 1).astype(F32), n_used - 1.0)
    tile_expert = jnp.zeros(meta_ref.shape, F32)
    for e in range(MOE_EXPERTS):
        first_row = first_row + jnp.where(f == float(e), pstart[:, e:e + 1], 0.0)
        tile_expert = tile_expert + jnp.where(tile_end[:, e:e + 1] <= tile_id, 1.0, 0.0)
    pos = first_row + pltpu.roll(f, shift=4, axis=0)
    pos_ref[...] = jnp.clip(pos, 0.0, n_sorted - 1.0).astype(jnp.int32)
    is_row0 = lax.broadcasted_iota(jnp.int32, meta_ref.shape, 0) == 0
    meta_ref[...] = jnp.where(is_row0, tile_expert, n_used).astype(jnp.int32)


def _mix_route_compute(x_ref, ys_ref, y5_ref, wa_ref, wb_ref, nf_ref, wrh_ref, wrl_ref, br_ref,
                       x1_ref, rt_ref, carry, xn_buf, fields, col0):
    rows = x_ref.shape[0]
    x1 = x_ref[...] + _dot(ys_ref[...], wa_ref[...]) + _dot(y5_ref[...].astype(BF16), wb_ref[...])
    x1_ref[0:rows, :] = x1
    xn = _rms(x1, nf_ref[...])
    xn_buf[0:rows, :] = _pack_bf16_pairs(xn)

    xh = xn.astype(BF16)
    xl = (xn - xh.astype(F32)).astype(BF16)
    logits = _dot(xh, wrh_ref[...]) + _dot(xl, wrh_ref[...]) + _dot(xh, wrl_ref[...]) + br_ref[...]
    tm = logits.shape[0]
    lane = lax.broadcasted_iota(jnp.int32, logits.shape, 1).astype(F32)
    neg = -jnp.inf
    big = float(LANES)

    def first_max(v):
        m = jnp.max(v, axis=-1, keepdims=True)
        return m, jnp.min(jnp.where(v == m, lane, big), axis=-1, keepdims=True)

    coarse = lane < MOE_GROUPS
    mc, gsel = first_max(jnp.where(coarse, logits, neg))
    psel = 1.0 / jnp.sum(jnp.where(coarse, jnp.exp(logits - mc), 0.0), axis=-1, keepdims=True)
    lo = MOE_GROUPS + MOE_EPG * gsel
    lf = jnp.where((lane >= lo) & (lane < lo + MOE_EPG), logits, neg)
    m1, i1 = first_max(lf)
    m2, i2 = first_max(jnp.where(lane == i1, neg, lf))
    e2 = jnp.exp(m2 - m1)
    g1 = psel / (1.0 + e2)
    g2 = psel * e2 / (1.0 + e2)
    e_a, e_b = i1 - MOE_GROUPS, i2 - MOE_GROUPS

    pick_a, pick_b = lane == e_a, lane == e_b
    picks = jnp.where(pick_a | pick_b, 1.0, 0.0)
    earlier = lax.broadcasted_iota(jnp.int32, (tm, tm), 0) > lax.broadcasted_iota(jnp.int32, (tm, tm), 1)
    prior = _dot(earlier.astype(BF16), picks.astype(BF16)) + carry[...]
    rank_a = jnp.sum(jnp.where(pick_a, prior, 0.0), axis=-1, keepdims=True)
    rank_b = jnp.sum(jnp.where(pick_b, prior, 0.0), axis=-1, keepdims=True)
    carry[...] = prior[tm - 1:tm, :] + picks[tm - 1:tm, :]

    out = jnp.zeros_like(logits)
    for k, v in enumerate((e_a, e_b, g1, g2, rank_a, rank_b)):
        out = jnp.where(lane == float(k), v, out)
    rt_ref[0:rows, :] = out
    fields[:, pl.ds(col0, rows)] = out.T[0:SUBLANES, :]


def _mix_route(prompt, sample, consts, tm, n_tiles):
    n_prompt, n_sample = prompt[0].shape[0], sample[0].shape[0]
    assert n_prompt % tm == 0 and n_sample <= tm
    n_blocks = n_prompt // tm
    total_rows = n_prompt + n_sample
    row = lambda w: pl.BlockSpec((tm, w), lambda i: (jnp.minimum(i, n_blocks - 1), 0))
    out_row = lambda w: pl.BlockSpec((tm, w), lambda i: (i, 0))
    assert total_rows % LANES == 0 and n_tiles <= 2 * LANES
    whole = lambda shape: pl.BlockSpec(shape, lambda i: (0, 0))
    return pl.pallas_call(
        functools.partial(_mix_route_body, n_blocks, n_tiles * MOE_TILE),
        grid=(n_blocks + 1,),
        in_specs=([row(D_MODEL), row(SSD_WIDTH), row(S5_WIDTH)] + [_full_spec(a) for a in sample]
                  + [_resident_spec(a) for a in consts]),
        out_specs=[out_row(D_MODEL), pl.BlockSpec(memory_space=pl.ANY), out_row(LANES),
                   whole((SUBLANES, total_rows)), whole((SUBLANES, 2 * LANES))],
        out_shape=[jax.ShapeDtypeStruct((total_rows, D_MODEL), F32),
                   jax.ShapeDtypeStruct((total_rows, PACK_ROWS, LANES), jnp.uint32),
                   jax.ShapeDtypeStruct((total_rows, LANES), F32),
                   jax.ShapeDtypeStruct((SUBLANES, total_rows), jnp.int32),
                   jax.ShapeDtypeStruct((SUBLANES, 2 * LANES), jnp.int32)],
        scratch_shapes=[pltpu.VMEM((1, LANES), F32), pltpu.VMEM((SUBLANES, total_rows), F32),
                        pltpu.VMEM((2, tm, D_MODEL // 2), jnp.uint32), pltpu.SemaphoreType.DMA((2, PACK_ROWS))],
        compiler_params=pltpu.CompilerParams(dimension_semantics=("arbitrary",), vmem_limit_bytes=VMEM_LIMIT),
        name="mix_route",
    )(*prompt, *sample, *consts)


def _sc_mesh():
    return plsc.VectorSubcoreMesh(core_axis_name="c", subcore_axis_name="s")


def _sc_worker():
    return lax.axis_index("s") * SC_CORES + lax.axis_index("c")


def _sc_dispatch(xn, pos_a, pos_b, n_rows):
    n_tok = xn.shape[0]
    ch = SC_DISPATCH_ROWS
    n_chunks = n_tok // ch
    assert n_tok % ch == 0 and n_chunks >= SC_WORKERS
    max_mine = -(-n_chunks // SC_WORKERS)
    row_shape, dtype = xn.shape[1:], xn.dtype
    stage = [pltpu.VMEM((ch,), jnp.int32), pltpu.VMEM((ch,), jnp.int32), pltpu.VMEM((ch,) + row_shape, dtype),
             pltpu.SemaphoreType.DMA]

    @functools.partial(
        pl.kernel, mesh=_sc_mesh(),
        out_type=jax.ShapeDtypeStruct((n_rows,) + row_shape, dtype),
        scratch_types=stage + stage + [pltpu.SemaphoreType.DMA])
    def push(xn_hbm, pa_hbm, pb_hbm, xs_hbm, ia0, ib0, rows0, lsem0, ia1, ib1, rows1, lsem1, ssem):
        wid = _sc_worker()
        mine = (n_chunks - wid + SC_WORKERS - 1) // SC_WORKERS
        bufs = ((ia0, ib0, rows0, lsem0), (ia1, ib1, rows1, lsem1))

        def loads(t, b):
            ia, ib, rows, sem = bufs[b]
            off = pl.multiple_of((wid + t * SC_WORKERS) * ch, ch)
            return (pltpu.make_async_copy(pa_hbm.at[pl.ds(off, ch)], ia, sem),
                    pltpu.make_async_copy(pb_hbm.at[pl.ds(off, ch)], ib, sem),
                    pltpu.make_async_copy(xn_hbm.at[pl.ds(off, ch)], rows, sem))

        def stage_in(t, b):
            for c in loads(t, b):
                c.start()

        def scatter(t, b):
            ia, ib, rows, _ = bufs[b]
            for c in loads(t, b):
                c.wait()
            first = pltpu.async_copy(rows, xs_hbm.at[ia], ssem)
            second = pltpu.async_copy(rows, xs_hbm.at[ib], ssem)
            first.wait()
            second.wait()

        stage_in(0, 0)

        @pl.loop(0, (max_mine + 1) // 2)
        def _(p):
            t = 2 * p

            @pl.when(t + 1 < mine)
            def _():
                stage_in(t + 1, 1)

            @pl.when(t < mine)
            def _():
                scatter(t, 0)

            @pl.when(t + 2 < mine)
            def _():
                stage_in(t + 2, 0)

            @pl.when(t + 1 < mine)
            def _():
                scatter(t + 1, 1)

    return push(xn, pos_a, pos_b)


def _sc_collect(ysorted, pos_flat, ch):
    n_pick = pos_flat.shape[0]
    per_worker = n_pick // SC_WORKERS
    n_chunks = per_worker // ch
    assert n_pick % SC_WORKERS == 0 and per_worker % ch == 0
    row_shape, dtype = ysorted.shape[1:], ysorted.dtype

    @functools.partial(
        pl.kernel, mesh=_sc_mesh(),
        out_type=jax.ShapeDtypeStruct((n_pick,) + row_shape, dtype),
        scratch_types=[pltpu.VMEM((ch,), jnp.int32), pltpu.VMEM((ch,), jnp.int32),
                       pltpu.VMEM((ch,) + row_shape, dtype), pltpu.VMEM((ch,) + row_shape, dtype),
                       pltpu.SemaphoreType.DMA, pltpu.SemaphoreType.DMA])
    def pull(ys_hbm, pos_hbm, out_hbm, idx0, idx1, rows0, rows1, sem0, sem1):
        base = _sc_worker() * per_worker
        bufs = ((idx0, rows0, sem0), (idx1, rows1, sem1))

        def offset(j):
            return pl.multiple_of(base + j * ch, SUBLANES)

        def fetch(j, b):
            idx, rows, sem = bufs[b]
            pltpu.sync_copy(pos_hbm.at[pl.ds(offset(j), ch)], idx)
            pltpu.async_copy(ys_hbm.at[idx], rows, sem)

        def flush(j, b):
            idx, rows, sem = bufs[b]
            pltpu.make_async_copy(ys_hbm.at[idx], rows, sem).wait()
            pltpu.sync_copy(rows, out_hbm.at[pl.ds(offset(j), ch)])

        fetch(0, 0)

        @pl.loop(0, n_chunks // 2)
        def _(p):
            j = 2 * p
            fetch(j + 1, 1)
            flush(j, 0)

            @pl.when(j + 2 < n_chunks)
            def _():
                fetch(j + 2, 0)

            flush(j + 1, 1)

        if n_chunks % 2:
            flush(n_chunks - 1, 0)

    return pull(ysorted, pos_flat)


def _moe_ffn_body(*refs):
    for k in range(MOE_TILES_PER_STEP):
        _moe_ffn_tile(k, *refs)


def _moe_ffn_tile(k, te_ref, nused_ref, x_ref, wg_hbm, wu_hbm, wd_hbm, y_ref,
                  wg_f32, wu_f32, wd_f32, wgb, wub, wdb, slot_ref, sems):
    i = pl.program_id(0) * MOE_TILES_PER_STEP + k
    n_used = nused_ref[0]
    window = pl.ds(k * MOE_TILE * PACK_ROWS, MOE_TILE * PACK_ROWS)
    x_ref, y_ref = x_ref.at[window, :], y_ref.at[window, :]

    def fetch(expert, slot):
        return (pltpu.make_async_copy(wg_hbm.at[expert], wg_f32.at[slot], sems.at[slot, 0]),
                pltpu.make_async_copy(wu_hbm.at[expert], wu_f32.at[slot], sems.at[slot, 1]),
                pltpu.make_async_copy(wd_hbm.at[expert], wd_f32.at[slot], sems.at[slot, 2]))

    @pl.when(i >= n_used)
    def _unused_tile():
        y_ref[...] = jnp.zeros_like(y_ref)

    @pl.when(i < n_used)
    def _tile():
        expert = te_ref[i]

        @pl.when(i == 0)
        def _first_fetch():
            slot_ref[0] = 0
            for c in fetch(expert, 0):
                c.start()

        @pl.when((i == 0) | (expert != te_ref[jnp.maximum(i - 1, 0)]))
        def _new_expert():
            slot = slot_ref[0]
            nxt = lax.while_loop(lambda k: (k < n_used) & (te_ref[jnp.minimum(k, n_used - 1)] == expert),
                                 lambda k: k + 1, i + 1)

            @pl.when(nxt < n_used)
            def _prefetch():
                for c in fetch(te_ref[jnp.minimum(nxt, n_used - 1)], 1 - slot):
                    c.start()

            for c in fetch(expert, slot):
                c.wait()
            wgb[...] = wg_f32[slot].astype(BF16)
            wub[...] = wu_f32[slot].astype(BF16)
            wdb[...] = wd_f32[slot].astype(BF16)
            slot_ref[0] = 1 - slot

        x = _unpack_bf16_pairs(x_ref, MOE_TILE).astype(BF16)
        gate = _dot(x, wgb[...])
        hmid = (gate * jax.nn.sigmoid(gate)) * _dot(x, wub[...])
        y = _dot(hmid.astype(BF16), wdb[...])
        packed = _pack_bf16_pairs(y)
        for j in range(PACK_ROWS):
            y_ref[pl.ds(j, MOE_TILE, stride=PACK_ROWS), :] = packed[:, j * LANES:(j + 1) * LANES]


def _moe_ffn(tile_expert, n_used, xsorted, w_gate, w_up, w_down):
    n_tiles = tile_expert.shape[0]
    per_step = MOE_TILES_PER_STEP
    assert n_tiles % per_step == 0
    hbm = pl.BlockSpec(memory_space=pl.ANY)
    tile = lambda imap: pl.BlockSpec((per_step * MOE_TILE * PACK_ROWS, LANES), imap)
    up_shape, down_shape = (D_MODEL, MOE_D_FF), (MOE_D_FF, D_MODEL)
    return pl.pallas_call(
        _moe_ffn_body,
        grid_spec=pltpu.PrefetchScalarGridSpec(
            num_scalar_prefetch=2,
            grid=(n_tiles // per_step,),
            in_specs=[tile(lambda i, te, nu: (jnp.clip(i, 0, jnp.maximum(nu[0] - 1, 0) // per_step), 0)),
                      hbm, hbm, hbm],
            out_specs=tile(lambda i, te, nu: (i, 0)),
            scratch_shapes=[pltpu.VMEM((2,) + up_shape, F32), pltpu.VMEM((2,) + up_shape, F32),
                            pltpu.VMEM((2,) + down_shape, F32),
                            pltpu.VMEM(up_shape, BF16), pltpu.VMEM(up_shape, BF16), pltpu.VMEM(down_shape, BF16),
                            pltpu.SMEM((1,), jnp.int32), pltpu.SemaphoreType.DMA((2, 3))]),
        out_shape=jax.ShapeDtypeStruct((n_tiles * MOE_TILE * PACK_ROWS, LANES), jnp.uint32),
        compiler_params=pltpu.CompilerParams(dimension_semantics=("arbitrary",), vmem_limit_bytes=VMEM_LIMIT),
        name="moe_ffn",
    )(tile_expert, n_used, xsorted, w_gate, w_up, w_down)


def _combine_body(x1_ref, rt_ref, ya_ref, yb_ref, nf_ref, *rest):
    out_ref = rest[-1]
    rt = rt_ref[...]
    x1 = x1_ref[...]
    tm = x1.shape[0]

    x2 = (x1 + rt[:, 2:3] * _unpack_bf16_pairs(ya_ref.at[0], tm)
          + rt[:, 3:4] * _unpack_bf16_pairs(yb_ref.at[0], tm))
    out_ref[...] = _rms(x2, nf_ref[...])


def _combine(x1, rt, y_picks, nf, tm, rows, x_block, y_block, out_rows, out_block, out_buf=None):
    row = lambda w: pl.BlockSpec((tm, w), lambda i: (i + x_block, 0))
    pick = lambda k: pl.BlockSpec((1, tm * PACK_ROWS, LANES), lambda i: (k, i + y_block, 0))
    in_specs = [row(D_MODEL), row(LANES), pick(0), pick(1), pl.BlockSpec((1, D_MODEL), lambda i: (0, 0))]
    args = [x1, rt, y_picks, y_picks, nf]
    aliases = {}
    if out_buf is not None:
        in_specs.append(pl.BlockSpec(memory_space=pl.ANY))
        aliases[len(args)] = 0
        args.append(out_buf)
    return pl.pallas_call(
        _combine_body,
        grid=(rows // tm,),
        in_specs=in_specs,
        out_specs=pl.BlockSpec((tm, D_MODEL), lambda i: (i + out_block, 0)),
        out_shape=jax.ShapeDtypeStruct((out_rows, D_MODEL), F32),
        input_output_aliases=aliases,
        compiler_params=pltpu.CompilerParams(dimension_semantics=("parallel",), vmem_limit_bytes=VMEM_LIMIT),
        name="moe_combine",
    )(*args)


def _s5_tables(a_re, a_im, log_dt, b_re, b_im, c_re, c_im):
    dt = jnp.exp(log_dt)[:, None]
    mag = jnp.exp(a_re * dt)
    ab_re = mag * jnp.cos(a_im * dt)
    ab_im = mag * jnp.sin(a_im * dt)
    den = a_re * a_re + a_im * a_im
    nr = ab_re - 1.0
    q_re = (nr * a_re + ab_im * a_im) / den
    q_im = (ab_im * a_re - nr * a_im) / den
    bb_re = q_re[..., None] * b_re - q_im[..., None] * b_im
    bb_im = q_re[..., None] * b_im + q_im[..., None] * b_re
    nblk = S5_GROUPS // 16
    kw, nw = 16 * S5_GROUP_CH, 16 * S5_STATE
    same_group = (jnp.arange(kw)[:, None] // S5_GROUP_CH) == (jnp.arange(nw)[None, :] // S5_STATE)

    def in_map(bb):
        rows = bb.reshape(nblk, 16, S5_STATE, S5_GROUP_CH).transpose(0, 1, 3, 2).reshape(nblk, kw, S5_STATE)
        return jnp.where(same_group, jnp.tile(rows, (1, 1, 16)), 0.0)

    def out_map(cc):
        cols = cc.reshape(nblk, 16, S5_GROUP_CH, S5_STATE).transpose(0, 3, 1, 2).reshape(nblk, S5_STATE, kw)
        return jnp.where(same_group.T, jnp.tile(cols, (1, 16, 1)), 0.0)

    wb = jnp.concatenate([in_map(bb_re), in_map(bb_im)], axis=-1).astype(BF16)
    return (wb, ab_re.reshape(1, S5_LANES), ab_im.reshape(1, S5_LANES),
            out_map(c_re).astype(BF16), out_map(-c_im).astype(BF16))


def kernel(x_prompt, x_sample, state_ssd_conv, state_ssd_ssm, state_s5_re, state_s5_im, meta_tokens, norm_mix, w_in, conv_w, conv_b, dt_bias, a_log, d_ssd, ssd_norm, s5_a_re, s5_a_im, s5_log_dt, s5_b_re, s5_b_im, s5_c_re, s5_c_im, s5_d, w_glu, b_glu, s5_norm, w_out, norm_ffn, router_coarse_w, router_coarse_b, router_fine_w, router_fine_b, w_gate, w_up, w_down, norm_final):
    bp, seq, _ = x_prompt.shape
    bs = x_sample.shape[0]
    n_prompt = bp * seq
    n_tok = n_prompt + bs
    row2 = lambda v: v.reshape(1, -1)
    pad_heads = lambda v: jnp.pad(v, (0, LANES - SSD_HEADS)).reshape(1, LANES)

    w = w_in[0]
    o1, o2, o3 = SSD_WIDTH, SSD_WIDTH + SSD_CONV_DIM, SSD_WIDTH + SSD_CONV_DIM + SSD_HEADS
    wz, wx, wu = w[:, :o1].astype(BF16), w[:, o1:o2].astype(BF16), w[:, o3:].astype(BF16)
    wdt = jnp.pad(w[:, o2:o3], ((0, 0), (0, LANES - SSD_HEADS))).astype(BF16)
    g_mix = row2(norm_mix[0])
    cw, cb = conv_w[0], row2(conv_b[0])
    dtb, alog = pad_heads(dt_bias[0]), pad_heads(a_log[0])
    dexp = row2(jnp.repeat(d_ssd[0], SSD_HEAD_DIM))
    snrm = row2(ssd_norm[0])
    eexp = (jnp.arange(LANES)[:, None] == (jnp.arange(SSD_WIDTH) // SSD_HEAD_DIM)[None, :]).astype(BF16)
    wb5, ab_re, ab_im, wcr, wci = _s5_tables(s5_a_re[0], s5_a_im[0], s5_log_dt[0], s5_b_re[0], s5_b_im[0],
                                             s5_c_re[0], s5_c_im[0])
    d5, wglu, bglu, nrm5 = row2(s5_d[0]), w_glu[0].astype(BF16), row2(b_glu[0]), row2(s5_norm[0])
    wo_a, wo_b = w_out[0][:SSD_WIDTH].astype(BF16), w_out[0][SSD_WIDTH:].astype(BF16)
    w_r = jnp.concatenate([router_coarse_w[0], router_fine_w[0].transpose(1, 0, 2).reshape(D_MODEL, MOE_EXPERTS)], axis=1)
    w_r = jnp.pad(w_r, ((0, 0), (0, LANES - w_r.shape[1])))
    wrh = w_r.astype(BF16)
    wrl = (w_r - wrh.astype(F32)).astype(BF16)
    b_r = jnp.concatenate([router_coarse_b[0], router_fine_b[0].reshape(-1)])
    b_r = jnp.pad(b_r, (0, LANES - b_r.shape[0])).reshape(1, LANES)

    zp, xbcp, dtp, up = _in_proj(x_prompt.reshape(n_prompt, D_MODEL), g_mix, wz, wx, wdt, wu, IN_PROJ_TILE, BF16, F32)
    xsm = jnp.concatenate([x_sample.reshape(bs, D_MODEL), meta_tokens], axis=0)
    zs, xbcs, dts, us = _in_proj(xsm, g_mix, wz, wx, wdt, wu, xsm.shape[0], F32, F32)

    front = SSD_CHUNK - N_META
    padf = lambda a: jnp.pad(a[bs:], ((front, 0), (0, 0)))[None]
    gw = SSD_HPG * SSD_HEAD_DIM
    ssd_consts = (cw, cb, dtb, alog, dexp, snrm, eexp)
    _, ctail_m, _, ht_m = _ssd_chunked(
        padf(xbcs).astype(BF16), padf(dts), jnp.zeros((1, SSD_CHUNK, SSD_WIDTH), F32),
        jnp.zeros((1, SUBLANES, SSD_CONV_DIM), F32), jnp.zeros((1, SSD_GROUPS, SSD_STATE, gw), F32),
        *ssd_consts, mask_rows=front)
    y_ssd_p, ctail_p, ssm_p, _ = _ssd_chunked(
        xbcp.reshape(bp, seq, SSD_CONV_DIM), dtp.reshape(bp, seq, LANES), zp.reshape(bp, seq, SSD_WIDTH),
        ctail_m, ht_m, *ssd_consts, mask_rows=0)

    abr8, abi8 = jnp.broadcast_to(ab_re, (bp, S5_LANES)), jnp.broadcast_to(ab_im, (bp, S5_LANES))
    um8 = jnp.repeat(us[bs:], bp, axis=0).astype(BF16)
    y_s5_p, s5re_p, s5im_p = _s5_seq(up.reshape(bp, seq, S5_WIDTH), um8, wb5, abr8, abi8,
                                     wcr, wci, d5, wglu, bglu, nrm5)

    cst = state_ssd_conv[0]
    xt_s, dt_s, dec_s, bc, xs_s = _ssd_step_prep(xbcs[:bs], cst[:, 0], cst[:, 1], cst[:, 2], dts[:bs],
                                                 cw, cb, dtb, alog)
    ssm_s, y_core = _ssd_step(dt_s[:, :SSD_HEADS].reshape(-1), dec_s[:, :SSD_HEADS].reshape(-1),
                              state_ssd_ssm[0], xt_s, bc)
    y_ssd_s, y_s5_s, s5re_s, s5im_s = _sample_post(
        y_core, xs_s, zs[:bs], dexp, snrm, us[:bs], state_s5_re[0].reshape(bs, S5_LANES),
        state_s5_im[0].reshape(bs, S5_LANES), wb5, ab_re, ab_im, wcr, wci, d5, wglu, bglu, nrm5)

    route_consts = (wo_a, wo_b, row2(norm_ffn[0]), wrh, wrl, b_r)
    n_tiles = -(-2 * n_tok // MOE_TILE) + MOE_EXPERTS
    n_tiles = -(-n_tiles // MOE_TILES_PER_STEP) * MOE_TILES_PER_STEP
    x1, xn, rt, pos, meta = _mix_route(
        (x_prompt.reshape(n_prompt, D_MODEL), y_ssd_p.reshape(n_prompt, SSD_WIDTH), y_s5_p.reshape(n_prompt, S5_WIDTH)),
        (x_sample.reshape(bs, D_MODEL), y_ssd_s, y_s5_s), route_consts, IN_PROJ_TILE, n_tiles)

    pos_a, pos_b = pos[0], pos[1]
    tile_expert, n_used = meta[0, :n_tiles], meta[1, :1]
    xsorted = _sc_dispatch(xn, pos_a, pos_b, n_tiles * MOE_TILE)
    ysorted = _moe_ffn(tile_expert, n_used, xsorted.reshape(-1, LANES), w_gate[0], w_up[0], w_down[0])
    nfin = row2(norm_final)

    half = n_prompt // 2

    def collect(lo, hi, ch):
        picks = jnp.concatenate([pos_a[lo:hi], pos_b[lo:hi]])
        packed_rows = ysorted.reshape(-1, PACK_ROWS, LANES)
        return _sc_collect(packed_rows, picks, ch).reshape(2, (hi - lo) * PACK_ROWS, LANES)

    picks_1 = collect(0, half, SC_COLLECT_ROWS[0])
    picks_2 = collect(half, n_tok, SC_COLLECT_ROWS[1])
    blocks = half // TOK_TILE
    y_p = _combine(x1, rt, picks_1, nfin, TOK_TILE, half, 0, 0, n_prompt, 0)
    y_p = _combine(x1, rt, picks_2, nfin, TOK_TILE, half, blocks, 0, n_prompt, blocks, out_buf=y_p)
    y_s = _combine(x1, rt, picks_2, nfin, bs, bs, n_prompt // bs, half // bs, bs, 0)

    s5_state = lambda a, b: a.reshape(1, b, S5_GROUPS, S5_STATE)
    new_conv_s = jnp.stack([cst[:, 1], cst[:, 2], xbcs[:bs]], axis=1)[None]
    return (y_p.reshape(bp, seq, D_MODEL), y_s.reshape(bs, 1, D_MODEL),
            ctail_p[:, SUBLANES - (SSD_CONV - 1):][None], ssm_p[None], s5_state(s5re_p, bp), s5_state(s5im_p, bp),
            new_conv_s, ssm_s[None], s5_state(s5re_s, bs), s5_state(s5im_s, bs))
```

```python
import functools

import jax
import jax.numpy as jnp
from jax import lax
from jax.experimental import pallas as pl
from jax.experimental.pallas import tpu as pltpu
from jax.experimental.pallas import tpu_sc as plsc

F32, BF16 = jnp.float32, jnp.bfloat16

D_MODEL = 1024
N_META = 16
SSD_WIDTH = 1024
SSD_HEAD_DIM = 64
SSD_HEADS = 16
SSD_GROUPS = 2
SSD_HPG = SSD_HEADS // SSD_GROUPS
SSD_STATE = 128
SSD_CONV = 4
SSD_CHUNK = 128
SSD_CONV_DIM = SSD_WIDTH + 2 * SSD_GROUPS * SSD_STATE
S5_WIDTH = 1024
S5_GROUP_CH = 16
S5_GROUPS = 64
S5_STATE = 64
S5_LANES = S5_GROUPS * S5_STATE
MOE_GROUPS = 4
MOE_EPG = 8
MOE_EXPERTS = MOE_GROUPS * MOE_EPG
MOE_D_FF = 512
EPS = 1e-6

LANES = 128
SUBLANES = 8
VMEM_LIMIT = 56 * 1024 * 1024

SSD_CHUNKS_PER_STEP = 4
S5_TIME_TILE = 64
S5_SCAN_LANES = 512
MOE_TILE = 512
MOE_TILES_PER_STEP = 2
SLAB_ROWS = D_MODEL // LANES
PACK_ROWS = SLAB_ROWS // 2
SC_CORES = 2
SC_SUBCORES = 16
SC_WORKERS = SC_CORES * SC_SUBCORES
SC_DISPATCH_ROWS = 64
SC_COLLECT_ROWS = (64, 104)
TOK_TILE = 512
IN_PROJ_TILE = 1024


def _dot(a, b):
    return jnp.dot(a, b, preferred_element_type=F32)


def _rms(x, g):
    return x * lax.rsqrt(jnp.mean(x * x, axis=-1, keepdims=True) + EPS) * g


def _softplus(x):
    return jnp.maximum(x, 0.0) + jnp.log1p(jnp.exp(-jnp.abs(x)))


def _split3(x):
    hi = x.astype(BF16)
    r = x - hi.astype(F32)
    mid = r.astype(BF16)
    lo = (r - mid.astype(F32)).astype(BF16)
    return hi, mid, lo


def _dot3(x, w):
    hi, mid, lo = _split3(x)
    return _dot(hi, w) + _dot(mid, w) + _dot(lo, w)


def _dot3_left(w, x):
    hi, mid, lo = _split3(x)
    return _dot(w, hi) + _dot(w, mid) + _dot(w, lo)


def _pack_bf16_pairs(x):
    bits = pltpu.bitcast(x.astype(BF16).astype(F32), jnp.uint32)
    half = x.shape[1] // 2
    return (bits[:, :half] & jnp.uint32(0xFFFF0000)) | (bits[:, half:] >> jnp.uint32(16))


def _unpack_bf16_pairs(ref, rows):
    words = [ref[pl.ds(j, rows, stride=PACK_ROWS), :] for j in range(PACK_ROWS)]
    high = [pltpu.bitcast(w & jnp.uint32(0xFFFF0000), F32) for w in words]
    low = [pltpu.bitcast(w << jnp.uint32(16), F32) for w in words]
    return jnp.concatenate(high + low, axis=-1)


def _full_spec(a):
    nd = a.ndim
    return pl.BlockSpec(a.shape, lambda *_: (0,) * nd)


def _resident_spec(a):
    nd = a.ndim
    return pl.BlockSpec(a.shape, lambda *_: (0,) * nd, pipeline_mode=pl.Buffered(1))


def _in_proj_body(x_ref, g_ref, wz_ref, wx_ref, wdt_ref, wu_ref, z_ref, xbc_ref, dt_ref, u_ref):
    xb = _rms(x_ref[...], g_ref[...]).astype(BF16)
    z_ref[...] = _dot(xb, wz_ref[...]).astype(z_ref.dtype)
    xbc_ref[...] = _dot(xb, wx_ref[...]).astype(xbc_ref.dtype)
    dt_ref[...] = _dot(xb, wdt_ref[...])
    u_ref[...] = _dot(xb, wu_ref[...]).astype(u_ref.dtype)


def _in_proj(x2d, g, wz, wx, wdt, wu, tm, act_dtype, u_dtype):
    rows = x2d.shape[0]
    row = lambda w: pl.BlockSpec((tm, w), lambda i: (i, 0))
    return pl.pallas_call(
        _in_proj_body,
        grid=(rows // tm,),
        in_specs=[row(D_MODEL)] + [_resident_spec(a) for a in (g, wz, wx, wdt, wu)],
        out_specs=[row(SSD_WIDTH), row(SSD_CONV_DIM), row(LANES), row(S5_WIDTH)],
        out_shape=[jax.ShapeDtypeStruct((rows, SSD_WIDTH), act_dtype),
                   jax.ShapeDtypeStruct((rows, SSD_CONV_DIM), act_dtype),
                   jax.ShapeDtypeStruct((rows, LANES), F32),
                   jax.ShapeDtypeStruct((rows, S5_WIDTH), u_dtype)],
        compiler_params=pltpu.CompilerParams(dimension_semantics=("parallel",), vmem_limit_bytes=VMEM_LIMIT),
        name="in_proj",
    )(x2d, g, wz, wx, wdt, wu)


def _ssd_body(mask_rows, per_step, *refs):
    for k in range(per_step):
        _ssd_chunk(mask_rows, per_step, k, *refs)


def _ssd_chunk(mask_rows, per_step, k, xbc_ref, dt_ref, z_ref, cinit_ref, hinit_ref, cw_ref, cb_ref, dtb_ref,
               alog_ref, dexp_ref, nrm_ref, eexp_ref, y_ref, ctail_ref, st_ref, hto_ref, xwin, hT):
    L = SSD_CHUNK
    c = pl.program_id(1) * per_step + k
    n_chunks = pl.num_programs(1) * per_step
    window = pl.ds(k * L, L)
    xbc_ref, dt_ref, z_ref, y_ref = (r.at[:, window, :] for r in (xbc_ref, dt_ref, z_ref, y_ref))

    @pl.when(c == 0)
    def _init():
        xwin[...] = cinit_ref[0]
        hT[...] = hinit_ref[0]

    x_b = xbc_ref[0]
    x_f = x_b.astype(F32)
    taps = SSD_CONV - 1
    m_i = lax.broadcasted_iota(jnp.int32, (taps * L, L), 0)
    r_i = lax.broadcasted_iota(jnp.int32, (taps * L, L), 1)
    shift = (r_i + (taps - m_i // L) == m_i % L).astype(BF16)
    shifted = _dot(shift, x_b)
    acc = cb_ref[...] + x_f * cw_ref[taps:taps + 1, :]
    for k in range(taps):
        acc = acc + shifted[k * L:(k + 1) * L, :] * cw_ref[k:k + 1, :]
    joint = jnp.concatenate([xwin[...], x_f[0:SUBLANES, :]], axis=0)
    row8 = lax.broadcasted_iota(jnp.int32, (SUBLANES, 1), 0)
    head = acc[0:SUBLANES, :]
    for k in range(taps):
        d = taps - k
        head = head + jnp.where(row8 < d, joint[SUBLANES - d:2 * SUBLANES - d, :], 0.0) * cw_ref[k:k + 1, :]
    acc = jnp.concatenate([head, acc[SUBLANES:, :]], axis=0)
    tail = x_f[L - SUBLANES:, :]
    xwin[...] = tail
    ctail_ref[0] = tail

    xact = acc * jax.nn.sigmoid(acc)
    dt = _softplus(dt_ref[0] + dtb_ref[...])
    if mask_rows:
        valid = lax.broadcasted_iota(jnp.int32, (L, 1), 0) >= mask_rows
        xact = jnp.where(valid, xact, 0.0)
        dt = jnp.where(valid, dt, 0.0)

    a_neg = -jnp.exp(alog_ref[...])
    dA = dt * a_neg
    row_i = lax.broadcasted_iota(jnp.int32, (L, L), 0)
    col_i = lax.broadcasted_iota(jnp.int32, (L, L), 1)
    causal = row_i >= col_i
    tril = causal.astype(BF16)
    cs = _dot3_left(tril, dA)
    csT = cs.T
    dtT = dt.T
    ecs = jnp.exp(cs)
    wdec = jnp.exp(cs[L - 1:L, :] - cs) * dt
    eexp = eexp_ref[...]
    ecs_e = _dot3(ecs, eexp)
    wdec_e = _dot3(wdec, eexp)
    lane = lax.broadcasted_iota(jnp.int32, (L, LANES), 1)
    first_half = lane < SSD_HEAD_DIM

    gw = SSD_HPG * SSD_HEAD_DIM
    y_groups = []
    for g in range(SSD_GROUPS):
        b_g = xact[:, SSD_WIDTH + g * SSD_STATE: SSD_WIDTH + (g + 1) * SSD_STATE]
        c_g = xact[:, SSD_WIDTH + (SSD_GROUPS + g) * SSD_STATE: SSD_WIDTH + (SSD_GROUPS + g + 1) * SSD_STATE]
        b_b = b_g.astype(BF16)
        c_b = c_g.astype(BF16)
        cb = lax.dot_general(c_b, b_b, (((1,), (1,)), ((), ())), preferred_element_type=F32)
        xs_g = xact[:, g * gw:(g + 1) * gw]
        h_prev = hT[g]
        y_off = _dot(c_b, h_prev.astype(BF16)) * ecs_e[:, g * gw:(g + 1) * gw]
        xdec = (xs_g * wdec_e[:, g * gw:(g + 1) * gw]).astype(BF16)
        hT[g] = h_prev * ecs_e[L - 1:L, g * gw:(g + 1) * gw] + _dot(b_g.T.astype(BF16), xdec)
        pieces = []
        for j in range(SSD_HPG // 2):
            xs_pair = xs_g[:, j * LANES:(j + 1) * LANES]
            halves = (jnp.where(first_half, xs_pair, 0.0).astype(BF16),
                      jnp.where(first_half, 0.0, xs_pair).astype(BF16))
            yd = None
            for t in range(2):
                h = g * SSD_HPG + 2 * j + t
                seg = cs[:, h:h + 1] - csT[h:h + 1, :]
                lmat = jnp.exp(jnp.where(causal, seg, -jnp.inf))
                m = (cb * lmat * dtT[h:h + 1, :]).astype(BF16)
                part = _dot(m, halves[t])
                yd = part if yd is None else yd + part
            pieces.append(yd)
        y_groups.append(jnp.concatenate(pieces, axis=-1) + y_off + dexp_ref[:, g * gw:(g + 1) * gw] * xs_g)
    y = jnp.concatenate(y_groups, axis=-1)
    z = z_ref[0].astype(F32)
    y_ref[0] = _rms(y * (z * jax.nn.sigmoid(z)), nrm_ref[...]).astype(y_ref.dtype)

    @pl.when(c == n_chunks - 1)
    def _emit():
        hto_ref[0] = hT[...]
        for g in range(SSD_GROUPS):
            t = hT[g].T
            for k in range(SSD_HPG):
                st_ref[0, g * SSD_HPG + k] = t[k * SSD_HEAD_DIM:(k + 1) * SSD_HEAD_DIM, :]


def _ssd_chunked(xbc, dt, z, cinit, hinit, cw, cb, dtb, alog, dexp, nrm, eexp, mask_rows):
    bsz, seq, _ = xbc.shape
    nc = seq // SSD_CHUNK
    per_step = SSD_CHUNKS_PER_STEP if nc % SSD_CHUNKS_PER_STEP == 0 else 1
    gw = SSD_HPG * SSD_HEAD_DIM
    blk = lambda w: pl.BlockSpec((1, per_step * SSD_CHUNK, w), lambda b, c: (b, c, 0))
    return pl.pallas_call(
        functools.partial(_ssd_body, mask_rows, per_step),
        grid=(bsz, nc // per_step),
        in_specs=[blk(SSD_CONV_DIM), blk(LANES), blk(SSD_WIDTH),
                  pl.BlockSpec((1, SUBLANES, SSD_CONV_DIM), lambda b, c: (0, 0, 0)),
                  pl.BlockSpec((1, SSD_GROUPS, SSD_STATE, gw), lambda b, c: (0, 0, 0, 0)),
                  _full_spec(cw), _full_spec(cb), _full_spec(dtb), _full_spec(alog),
                  _full_spec(dexp), _full_spec(nrm), _full_spec(eexp)],
        out_specs=[blk(SSD_WIDTH),
                   pl.BlockSpec((1, SUBLANES, SSD_CONV_DIM), lambda b, c: (b, 0, 0)),
                   pl.BlockSpec((1, SSD_HEADS, SSD_HEAD_DIM, SSD_STATE), lambda b, c: (b, 0, 0, 0)),
                   pl.BlockSpec((1, SSD_GROUPS, SSD_STATE, gw), lambda b, c: (b, 0, 0, 0))],
        out_shape=[jax.ShapeDtypeStruct((bsz, seq, SSD_WIDTH), BF16),
                   jax.ShapeDtypeStruct((bsz, SUBLANES, SSD_CONV_DIM), F32),
                   jax.ShapeDtypeStruct((bsz, SSD_HEADS, SSD_HEAD_DIM, SSD_STATE), F32),
                   jax.ShapeDtypeStruct((bsz, SSD_GROUPS, SSD_STATE, gw), F32)],
        scratch_shapes=[pltpu.VMEM((SUBLANES, SSD_CONV_DIM), F32),
                        pltpu.VMEM((SSD_GROUPS, SSD_STATE, gw), F32)],
        compiler_params=pltpu.CompilerParams(dimension_semantics=("parallel", "arbitrary"),
                                             vmem_limit_bytes=VMEM_LIMIT),
        name="ssd_chunked",
    )(xbc, dt, z, cinit, hinit, cw, cb, dtb, alog, dexp, nrm, eexp)


def _ssd_step_prep_body(xbc_ref, c0_ref, c1_ref, c2_ref, dt_ref, cw_ref, cb_ref, dtb_ref, alog_ref,
                        xt_ref, dt_out_ref, dec_ref, bc_ref, xs_ref):
    acc = cb_ref[...]
    for k, r in enumerate((c0_ref, c1_ref, c2_ref, xbc_ref)):
        acc = acc + r[...] * cw_ref[k:k + 1, :]
    xact = acc * jax.nn.sigmoid(acc)
    xs = xact[:, :SSD_WIDTH]
    dt = _softplus(dt_ref[...] + dtb_ref[...])
    dt_out_ref[...] = dt
    dec_ref[...] = jnp.exp(dt * -jnp.exp(alog_ref[...]))
    bc_ref[...] = xact[:, SSD_WIDTH:]
    xs_ref[...] = xs
    xt_ref[...] = xs.T.astype(xt_ref.dtype)


def _ssd_step_prep(xbc, c0, c1, c2, dt, cw, cb, dtb, alog):
    n = xbc.shape[0]
    args = (xbc, c0, c1, c2, dt, cw, cb, dtb, alog)
    spec = lambda r, w: pl.BlockSpec((r, w), lambda: (0, 0))
    return pl.pallas_call(
        _ssd_step_prep_body,
        in_specs=[_full_spec(a) for a in args],
        out_specs=[spec(SSD_WIDTH, n), spec(n, LANES), spec(n, LANES), spec(n, 2 * SSD_GROUPS * SSD_STATE),
                   spec(n, SSD_WIDTH)],
        out_shape=[jax.ShapeDtypeStruct((SSD_WIDTH, n), BF16), jax.ShapeDtypeStruct((n, LANES), F32),
                   jax.ShapeDtypeStruct((n, LANES), F32),
                   jax.ShapeDtypeStruct((n, 2 * SSD_GROUPS * SSD_STATE), F32),
                   jax.ShapeDtypeStruct((n, SSD_WIDTH), F32)],
        compiler_params=pltpu.CompilerParams(vmem_limit_bytes=VMEM_LIMIT),
        name="ssd_step_prep",
    )(*args)


def _ssd_step_body(dt_ref, dec_ref, st_ref, xt_ref, bc_ref, so_ref, y_ref):
    n = xt_ref.shape[1]
    gw = SSD_HPG * SSD_HEAD_DIM
    blk = pl.program_id(0)
    seq_id = lax.broadcasted_iota(jnp.int32, (n, SSD_STATE), 0)
    sub_id = lax.broadcasted_iota(jnp.int32, (SUBLANES, gw), 0)
    base = pl.multiple_of(blk * SUBLANES, SUBLANES)
    y_acc = [jnp.zeros((SUBLANES, gw), F32) for _ in range(SSD_GROUPS)]
    for i in range(SUBLANES):
        s = blk * SUBLANES + i
        for g in range(SSD_GROUPS):
            b_all = bc_ref[:, g * SSD_STATE:(g + 1) * SSD_STATE]
            rhs = jnp.where(seq_id == s, b_all, 0.0).astype(BF16)
            outer = _dot(xt_ref[g * gw:(g + 1) * gw, :], rhs)
            news = []
            for k in range(SSD_HPG):
                h = g * SSD_HPG + k
                new = (dec_ref[s * SSD_HEADS + h] * st_ref[i, h]
                       + dt_ref[s * SSD_HEADS + h] * outer[k * SSD_HEAD_DIM:(k + 1) * SSD_HEAD_DIM, :])
                so_ref[i, h] = new
                news.append(new)
            new_g = jnp.concatenate(news, axis=0).astype(BF16)
            c_lo = (SSD_GROUPS + g) * SSD_STATE
            c_blk = bc_ref[pl.ds(base, SUBLANES), c_lo:c_lo + SSD_STATE].astype(BF16)
            r = lax.dot_general(c_blk, new_g, (((1,), (1,)), ((), ())), preferred_element_type=F32)
            y_acc[g] = y_acc[g] + jnp.where(sub_id == i, r, 0.0)
    y_ref[...] = jnp.concatenate(y_acc, axis=-1)


def _ssd_step(dt_flat, dec_flat, state, xt, bc):
    n = state.shape[0]
    st_spec = pl.BlockSpec((SUBLANES, SSD_HEADS, SSD_HEAD_DIM, SSD_STATE), lambda i, *_: (i, 0, 0, 0))
    return pl.pallas_call(
        _ssd_step_body,
        grid_spec=pltpu.PrefetchScalarGridSpec(
            num_scalar_prefetch=2,
            grid=(n // SUBLANES,),
            in_specs=[st_spec, pl.BlockSpec(xt.shape, lambda i, *_: (0, 0)),
                      pl.BlockSpec(bc.shape, lambda i, *_: (0, 0))],
            out_specs=[st_spec, pl.BlockSpec((SUBLANES, SSD_WIDTH), lambda i, *_: (i, 0))]),
        out_shape=[jax.ShapeDtypeStruct(state.shape, F32), jax.ShapeDtypeStruct((n, SSD_WIDTH), F32)],
        compiler_params=pltpu.CompilerParams(dimension_semantics=("parallel",), vmem_limit_bytes=VMEM_LIMIT),
        name="ssd_step",
    )(dt_flat, dec_flat, state, xt, bc)


def _s5_project_in(u_b16, wb_ref, store):
    kw = 16 * S5_GROUP_CH
    nw = 16 * S5_STATE
    for j in range(S5_WIDTH // kw):
        r = _dot(u_b16[:, j * kw:(j + 1) * kw], wb_ref[j])
        store(j, r[:, :nw], r[:, nw:])


def _s5_tail(hre_of, him_of, u_f32, wcr_ref, wci_ref, d_ref, wglu_ref, bglu_ref, nrm_ref):
    cols = []
    for j in range(wcr_ref.shape[0]):
        cols.append(_dot(hre_of(j).astype(BF16), wcr_ref[j]) + _dot(him_of(j).astype(BF16), wci_ref[j]))
    return _s5_finish(cols, u_f32, d_ref, wglu_ref, bglu_ref, nrm_ref)


def _s5_finish(cols, u_f32, d_ref, wglu_ref, bglu_ref, nrm_ref):
    y = jnp.concatenate(cols, axis=-1) + d_ref[...] * u_f32
    y = jax.nn.gelu(y)
    y = y * jax.nn.sigmoid(_dot(y.astype(BF16), wglu_ref[...]) + bglu_ref[...])
    return _rms(y, nrm_ref[...])


def _s5_seq_body(u_hbm, um_ref, wb_ref, abr_ref, abi_ref, wcr_ref, wci_ref, d_ref, wglu_ref, bglu_ref, nrm_ref,
                 y_hbm, sre_ref, sim_ref, ubuf, ybuf, bu, h, in_sems, out_sems):
    j = pl.program_id(0)
    last = pl.num_programs(0) - 1
    lc, bsz = ubuf.shape[1], ubuf.shape[2]
    rows = lc * bsz
    nw = 16 * S5_STATE

    def in_copy(step, b):
        return pltpu.make_async_copy(u_hbm.at[b, pl.ds(step * lc, lc), :], ubuf.at[step % 2, :, b, :],
                                     in_sems.at[step % 2, b])

    def out_copy(step, b):
        return pltpu.make_async_copy(ybuf.at[step % 2, :, b, :], y_hbm.at[b, pl.ds(step * lc, lc), :],
                                     out_sems.at[step % 2, b])

    def project_in(u_b16, nrows):
        def store(jj, re, im):
            bu[0:nrows, jj * nw:(jj + 1) * nw] = re
            bu[0:nrows, S5_LANES + jj * nw:S5_LANES + (jj + 1) * nw] = im
        _s5_project_in(u_b16, wb_ref, store)

    def scan(nsteps):
        for k in range(S5_LANES // S5_SCAN_LANES):
            sl_r = pl.ds(k * S5_SCAN_LANES, S5_SCAN_LANES)
            sl_i = pl.ds(S5_LANES + k * S5_SCAN_LANES, S5_SCAN_LANES)
            ar = abr_ref[:, sl_r]
            ai = abi_ref[:, sl_r]

            def step(l, carry):
                hr, hi = carry
                slab = pl.ds(pl.multiple_of(l * bsz, bsz), bsz)
                nr = ar * hr - ai * hi + bu[slab, sl_r]
                ni = ar * hi + ai * hr + bu[slab, sl_i]
                bu[slab, sl_r] = nr
                bu[slab, sl_i] = ni
                return nr, ni

            hr, hi = lax.fori_loop(0, nsteps, step, (h[:, sl_r], h[:, sl_i]))
            h[:, sl_r] = hr
            h[:, sl_i] = hi

    @pl.when(j == 0)
    def _first():
        for b in range(bsz):
            in_copy(0, b).start()
        h[...] = jnp.zeros_like(h)
        project_in(um_ref[...], N_META * bsz)
        scan(N_META)

    @pl.when(j < last)
    def _prefetch():
        for b in range(bsz):
            in_copy(j + 1, b).start()

    for b in range(bsz):
        in_copy(j, b).wait()
    u2 = ubuf[j % 2].reshape(rows, S5_WIDTH)
    u_b16 = u2.astype(BF16)
    kw = 16 * S5_GROUP_CH

    def project_block(jj):
        r = _dot(u_b16[:, jj * kw:(jj + 1) * kw], wb_ref[jj])
        bu[0:rows, jj * nw:(jj + 1) * nw] = r[:, :nw]
        bu[0:rows, S5_LANES + jj * nw:S5_LANES + (jj + 1) * nw] = r[:, nw:]

    def scan_block(jj):
        for k in range(nw // S5_SCAN_LANES):
            lo = jj * nw + k * S5_SCAN_LANES
            sl_r = slice(lo, lo + S5_SCAN_LANES)
            sl_i = slice(S5_LANES + lo, S5_LANES + lo + S5_SCAN_LANES)
            ar, ai = abr_ref[:, sl_r], abi_ref[:, sl_r]
            hr, hi = h[:, sl_r], h[:, sl_i]
            for l in range(lc):
                slab = slice(l * bsz, (l + 1) * bsz)
                hr, hi = (ar * hr - ai * hi + bu[slab, sl_r], ar * hi + ai * hr + bu[slab, sl_i])
                bu[slab, sl_r] = hr
                bu[slab, sl_i] = hi
            h[:, sl_r] = hr
            h[:, sl_i] = hi

    def readout_block(jj):
        return (_dot(bu[:, jj * nw:(jj + 1) * nw].astype(BF16), wcr_ref[jj])
                + _dot(bu[:, S5_LANES + jj * nw:S5_LANES + (jj + 1) * nw].astype(BF16), wci_ref[jj]))

    n_blocks = S5_WIDTH // kw
    project_block(0)
    cols = []
    for jj in range(n_blocks):
        if jj + 1 < n_blocks:
            project_block(jj + 1)
        scan_block(jj)
        cols.append(readout_block(jj))
    y = _s5_finish(cols, u2, d_ref, wglu_ref, bglu_ref, nrm_ref)
    ybuf[j % 2] = y.reshape(lc, bsz, S5_WIDTH)
    for b in range(bsz):
        out_copy(j, b).start()

    @pl.when(j > 0)
    def _wait_previous_out():
        for b in range(bsz):
            out_copy(j - 1, b).wait()

    @pl.when(j == last)
    def _emit():
        for b in range(bsz):
            out_copy(j, b).wait()
        sre_ref[...] = h[:, 0:S5_LANES]
        sim_ref[...] = h[:, S5_LANES:]


def _s5_seq(u, um, wb, abr, abi, wcr, wci, d, wglu, bglu, nrm):
    bsz, seq, _ = u.shape
    lc = S5_TIME_TILE
    consts = (um, wb, abr, abi, wcr, wci, d, wglu, bglu, nrm)
    st = pl.BlockSpec((bsz, S5_LANES), lambda j: (0, 0))
    return pl.pallas_call(
        _s5_seq_body,
        grid=(seq // lc,),
        in_specs=[pl.BlockSpec(memory_space=pl.ANY)] + [_resident_spec(a) for a in consts],
        out_specs=[pl.BlockSpec(memory_space=pl.ANY), st, st],
        out_shape=[jax.ShapeDtypeStruct((bsz, seq, S5_WIDTH), F32),
                   jax.ShapeDtypeStruct((bsz, S5_LANES), F32), jax.ShapeDtypeStruct((bsz, S5_LANES), F32)],
        scratch_shapes=[pltpu.VMEM((2, lc, bsz, S5_WIDTH), F32), pltpu.VMEM((2, lc, bsz, S5_WIDTH), F32),
                        pltpu.VMEM((lc * bsz, 2 * S5_LANES), F32), pltpu.VMEM((bsz, 2 * S5_LANES), F32),
                        pltpu.SemaphoreType.DMA((2, bsz)), pltpu.SemaphoreType.DMA((2, bsz))],
        compiler_params=pltpu.CompilerParams(dimension_semantics=("arbitrary",), vmem_limit_bytes=VMEM_LIMIT),
        name="s5_seq",
    )(u, *consts)


def _sample_post_body(yc_ref, xs_ref, z_ref, dexp_ref, snrm_ref, u_ref, hr_ref, hi_ref, wb_ref, abr_ref, abi_ref,
                      wcr_ref, wci_ref, d_ref, wglu_ref, bglu_ref, nrm_ref,
                      yssd_ref, ys5_ref, nre_ref, nim_ref):
    z = z_ref[...]
    y = yc_ref[...] + dexp_ref[...] * xs_ref[...]
    yssd_ref[...] = _rms(y * (z * jax.nn.sigmoid(z)), snrm_ref[...]).astype(yssd_ref.dtype)

    u = u_ref[...]
    nw = 16 * S5_STATE
    ar, ai = abr_ref[...], abi_ref[...]

    def store(jj, re, im):
        sl = slice(jj * nw, (jj + 1) * nw)
        h0r, h0i = hr_ref[:, sl], hi_ref[:, sl]
        nre_ref[:, sl] = ar[:, sl] * h0r - ai[:, sl] * h0i + re
        nim_ref[:, sl] = ar[:, sl] * h0i + ai[:, sl] * h0r + im

    _s5_project_in(u.astype(BF16), wb_ref, store)
    slab = lambda ref: (lambda jj: ref[:, jj * nw:(jj + 1) * nw])
    y5 = _s5_tail(slab(nre_ref), slab(nim_ref), u, wcr_ref, wci_ref, d_ref, wglu_ref, bglu_ref, nrm_ref)
    ys5_ref[...] = y5.astype(ys5_ref.dtype)


def _sample_post(yc, xs, z, dexp, snrm, u, h0r, h0i, wb, abr1, abi1, wcr, wci, d, wglu, bglu, nrm):
    n = yc.shape[0]
    args = (yc, xs, z, dexp, snrm, u, h0r, h0i, wb, abr1, abi1, wcr, wci, d, wglu, bglu, nrm)
    spec = lambda w: pl.BlockSpec((n, w), lambda: (0, 0))
    return pl.pallas_call(
        _sample_post_body,
        in_specs=[_full_spec(a) for a in args],
        out_specs=[spec(SSD_WIDTH), spec(S5_WIDTH), spec(S5_LANES), spec(S5_LANES)],
        out_shape=[jax.ShapeDtypeStruct((n, SSD_WIDTH), BF16), jax.ShapeDtypeStruct((n, S5_WIDTH), BF16),
                   jax.ShapeDtypeStruct((n, S5_LANES), F32), jax.ShapeDtypeStruct((n, S5_LANES), F32)],
        compiler_params=pltpu.CompilerParams(vmem_limit_bytes=VMEM_LIMIT),
        name="sample_post",
    )(*args)


def _mix_route_body(n_blocks, n_sorted, xp_ref, ysp_ref, y5p_ref, xs_ref, yss_ref, y5s_ref, *refs):
    consts = refs[:6]
    x1_ref, xn_hbm, rt_ref, pos_ref, meta_ref, carry, fields, xbuf, sems = refs[6:]
    i = pl.program_id(0)
    tm, n_sample = xp_ref.shape[0], xs_ref.shape[0]
    col0 = pl.multiple_of(i * tm, LANES)

    def xn_copy(step, rows, j):
        return pltpu.make_async_copy(xbuf.at[step % 2, pl.ds(0, rows), pl.ds(j * LANES, LANES)],
                                     xn_hbm.at[pl.ds(step * tm, rows), j, :], sems.at[step % 2, j])

    @pl.when(i == 0)
    def _init():
        carry[...] = jnp.zeros_like(carry)

    @pl.when(i < n_blocks)
    def _prompt_rows():
        _mix_route_compute(xp_ref, ysp_ref, y5p_ref, *consts, x1_ref, rt_ref, carry, xbuf.at[i % 2], fields, col0)
        for j in range(PACK_ROWS):
            xn_copy(i, tm, j).start()

    @pl.when(i == n_blocks)
    def _sample_rows():
        _mix_route_compute(xs_ref, yss_ref, y5s_ref, *consts, x1_ref, rt_ref, carry, xbuf.at[i % 2], fields, col0)
        for j in range(PACK_ROWS):
            xn_copy(i, n_sample, j).start()
        _route_layout(carry, fields, pos_ref, meta_ref, n_sorted)
        for j in range(PACK_ROWS):
            xn_copy(i, n_sample, j).wait()

    @pl.when(i > 0)
    def _wait_previous_rows():
        for j in range(PACK_ROWS):
            xn_copy(i - 1, tm, j).wait()


def _route_layout(carry, fields, pos_ref, meta_ref, n_sorted):
    counts = carry[...]
    tiles_per = jnp.floor((counts + (MOE_TILE - 1)) * (1.0 / MOE_TILE))
    upto = lax.broadcasted_iota(jnp.int32, (LANES, LANES), 0) <= lax.broadcasted_iota(jnp.int32, (LANES, LANES), 1)
    tile_end = _dot(tiles_per.astype(BF16), upto.astype(BF16))
    pstart = (tile_end - tiles_per) * MOE_TILE
    n_used = tile_end[:, MOE_EXPERTS - 1:MOE_EXPERTS]

    f = fields[...]
    first_row = jnp.zeros_like(f)
    tile_id = jnp.minimum(lax.broadcasted_iota(jnp.int32, meta_ref.shape, 1).astype(F32), n_used - 1.0)
    tile_expert = jnp.zeros(meta_ref.shape, F32)
    for e in range(MOE_EXPERTS):
        first_row = first_row + jnp.where(f == float(e), pstart[:, e:e + 1], 0.0)
        tile_expert = tile_expert + jnp.where(tile_end[:, e:e + 1] <= tile_id, 1.0, 0.0)
    pos = first_row + pltpu.roll(f, shift=4, axis=0)
    pos_ref[...] = jnp.clip(pos, 0.0, n_sorted - 1.0).astype(jnp.int32)
    is_row0 = lax.broadcasted_iota(jnp.int32, meta_ref.shape, 0) == 0
    meta_ref[...] = jnp.where(is_row0, tile_expert, n_used).astype(jnp.int32)


def _mix_route_compute(x_ref, ys_ref, y5_ref, wa_ref, wb_ref, nf_ref, wrh_ref, wrl_ref, br_ref,
                       x1_ref, rt_ref, carry, xn_buf, fields, col0):
    rows = x_ref.shape[0]
    x1 = x_ref[...] + _dot(ys_ref[...], wa_ref[...]) + _dot(y5_ref[...].astype(BF16), wb_ref[...])
    x1_ref[0:rows, :] = x1
    xn = _rms(x1, nf_ref[...])
    xn_buf[0:rows, :] = _pack_bf16_pairs(xn)

    xh = xn.astype(BF16)
    xl = (xn - xh.astype(F32)).astype(BF16)
    logits = _dot(xh, wrh_ref[...]) + _dot(xl, wrh_ref[...]) + _dot(xh, wrl_ref[...]) + br_ref[...]
    tm = logits.shape[0]
    lane = lax.broadcasted_iota(jnp.int32, logits.shape, 1).astype(F32)
    neg = -jnp.inf
    big = float(LANES)

    def first_max(v):
        m = jnp.max(v, axis=-1, keepdims=True)
        return m, jnp.min(jnp.where(v == m, lane, big), axis=-1, keepdims=True)

    coarse = lane < MOE_GROUPS
    mc, gsel = first_max(jnp.where(coarse, logits, neg))
    psel = 1.0 / jnp.sum(jnp.where(coarse, jnp.exp(logits - mc), 0.0), axis=-1, keepdims=True)
    lo = MOE_GROUPS + MOE_EPG * gsel
    lf = jnp.where((lane >= lo) & (lane < lo + MOE_EPG), logits, neg)
    m1, i1 = first_max(lf)
    m2, i2 = first_max(jnp.where(lane == i1, neg, lf))
    e2 = jnp.exp(m2 - m1)
    g1 = psel / (1.0 + e2)
    g2 = psel * e2 / (1.0 + e2)
    e_a, e_b = i1 - MOE_GROUPS, i2 - MOE_GROUPS

    pick_a, pick_b = lane == e_a, lane == e_b
    picks = jnp.where(pick_a | pick_b, 1.0, 0.0)
    earlier = lax.broadcasted_iota(jnp.int32, (tm, tm), 0) > lax.broadcasted_iota(jnp.int32, (tm, tm), 1)
    prior = _dot(earlier.astype(BF16), picks.astype(BF16)) + carry[...]
    rank_a = jnp.sum(jnp.where(pick_a, prior, 0.0), axis=-1, keepdims=True)
    rank_b = jnp.sum(jnp.where(pick_b, prior, 0.0), axis=-1, keepdims=True)
    carry[...] = prior[tm - 1:tm, :] + picks[tm - 1:tm, :]

    out = jnp.zeros_like(logits)
    for k, v in enumerate((e_a, e_b, g1, g2, rank_a, rank_b)):
        out = jnp.where(lane == float(k), v, out)
    rt_ref[0:rows, :] = out
    fields[:, pl.ds(col0, rows)] = out.T[0:SUBLANES, :]


def _mix_route(prompt, sample, consts, tm, n_tiles):
    n_prompt, n_sample = prompt[0].shape[0], sample[0].shape[0]
    assert n_prompt % tm == 0 and n_sample <= tm
    n_blocks = n_prompt // tm
    total_rows = n_prompt + n_sample
    row = lambda w: pl.BlockSpec((tm, w), lambda i: (jnp.minimum(i, n_blocks - 1), 0))
    out_row = lambda w: pl.BlockSpec((tm, w), lambda i: (i, 0))
    assert total_rows % LANES == 0 and n_tiles <= 2 * LANES
    whole = lambda shape: pl.BlockSpec(shape, lambda i: (0, 0))
    return pl.pallas_call(
        functools.partial(_mix_route_body, n_blocks, n_tiles * MOE_TILE),
        grid=(n_blocks + 1,),
        in_specs=([row(D_MODEL), row(SSD_WIDTH), row(S5_WIDTH)] + [_full_spec(a) for a in sample]
                  + [_resident_spec(a) for a in consts]),
        out_specs=[out_row(D_MODEL), pl.BlockSpec(memory_space=pl.ANY), out_row(LANES),
                   whole((SUBLANES, total_rows)), whole((SUBLANES, 2 * LANES))],
        out_shape=[jax.ShapeDtypeStruct((total_rows, D_MODEL), F32),
                   jax.ShapeDtypeStruct((total_rows, PACK_ROWS, LANES), jnp.uint32),
                   jax.ShapeDtypeStruct((total_rows, LANES), F32),
                   jax.ShapeDtypeStruct((SUBLANES, total_rows), jnp.int32),
                   jax.ShapeDtypeStruct((SUBLANES, 2 * LANES), jnp.int32)],
        scratch_shapes=[pltpu.VMEM((1, LANES), F32), pltpu.VMEM((SUBLANES, total_rows), F32),
                        pltpu.VMEM((2, tm, D_MODEL // 2), jnp.uint32), pltpu.SemaphoreType.DMA((2, PACK_ROWS))],
        compiler_params=pltpu.CompilerParams(dimension_semantics=("arbitrary",), vmem_limit_bytes=VMEM_LIMIT),
        name="mix_route",
    )(*prompt, *sample, *consts)


def _sc_mesh():
    return plsc.VectorSubcoreMesh(core_axis_name="c", subcore_axis_name="s")


def _sc_worker():
    return lax.axis_index("s") * SC_CORES + lax.axis_index("c")


def _sc_dispatch(xn, pos_a, pos_b, n_rows):
    n_tok = xn.shape[0]
    ch = SC_DISPATCH_ROWS
    n_chunks = n_tok // ch
    assert n_tok % ch == 0 and n_chunks >= SC_WORKERS
    max_mine = -(-n_chunks // SC_WORKERS)
    row_shape, dtype = xn.shape[1:], xn.dtype
    stage = [pltpu.VMEM((ch,), jnp.int32), pltpu.VMEM((ch,), jnp.int32), pltpu.VMEM((ch,) + row_shape, dtype),
             pltpu.SemaphoreType.DMA]

    @functools.partial(
        pl.kernel, mesh=_sc_mesh(),
        out_type=jax.ShapeDtypeStruct((n_rows,) + row_shape, dtype),
        scratch_types=stage + stage + [pltpu.SemaphoreType.DMA])
    def push(xn_hbm, pa_hbm, pb_hbm, xs_hbm, ia0, ib0, rows0, lsem0, ia1, ib1, rows1, lsem1, ssem):
        wid = _sc_worker()
        mine = (n_chunks - wid + SC_WORKERS - 1) // SC_WORKERS
        bufs = ((ia0, ib0, rows0, lsem0), (ia1, ib1, rows1, lsem1))

        def loads(t, b):
            ia, ib, rows, sem = bufs[b]
            off = pl.multiple_of((wid + t * SC_WORKERS) * ch, ch)
            return (pltpu.make_async_copy(pa_hbm.at[pl.ds(off, ch)], ia, sem),
                    pltpu.make_async_copy(pb_hbm.at[pl.ds(off, ch)], ib, sem),
                    pltpu.make_async_copy(xn_hbm.at[pl.ds(off, ch)], rows, sem))

        def stage_in(t, b):
            for c in loads(t, b):
                c.start()

        def scatter(t, b):
            ia, ib, rows, _ = bufs[b]
            for c in loads(t, b):
                c.wait()
            first = pltpu.async_copy(rows, xs_hbm.at[ia], ssem)
            second = pltpu.async_copy(rows, xs_hbm.at[ib], ssem)
            first.wait()
            second.wait()

        stage_in(0, 0)

        @pl.loop(0, (max_mine + 1) // 2)
        def _(p):
            t = 2 * p

            @pl.when(t + 1 < mine)
            def _():
                stage_in(t + 1, 1)

            @pl.when(t < mine)
            def _():
                scatter(t, 0)

            @pl.when(t + 2 < mine)
            def _():
                stage_in(t + 2, 0)

            @pl.when(t + 1 < mine)
            def _():
                scatter(t + 1, 1)

    return push(xn, pos_a, pos_b)


def _sc_collect(ysorted, pos_flat, ch):
    n_pick = pos_flat.shape[0]
    per_worker = n_pick // SC_WORKERS
    n_chunks = per_worker // ch
    assert n_pick % SC_WORKERS == 0 and per_worker % ch == 0
    row_shape, dtype = ysorted.shape[1:], ysorted.dtype

    @functools.partial(
        pl.kernel, mesh=_sc_mesh(),
        out_type=jax.ShapeDtypeStruct((n_pick,) + row_shape, dtype),
        scratch_types=[pltpu.VMEM((ch,), jnp.int32), pltpu.VMEM((ch,), jnp.int32),
                       pltpu.VMEM((ch,) + row_shape, dtype), pltpu.VMEM((ch,) + row_shape, dtype),
                       pltpu.SemaphoreType.DMA, pltpu.SemaphoreType.DMA])
    def pull(ys_hbm, pos_hbm, out_hbm, idx0, idx1, rows0, rows1, sem0, sem1):
        base = _sc_worker() * per_worker
        bufs = ((idx0, rows0, sem0), (idx1, rows1, sem1))

        def offset(j):
            return pl.multiple_of(base + j * ch, SUBLANES)

        def fetch(j, b):
            idx, rows, sem = bufs[b]
            pltpu.sync_copy(pos_hbm.at[pl.ds(offset(j), ch)], idx)
            pltpu.async_copy(ys_hbm.at[idx], rows, sem)

        def flush(j, b):
            idx, rows, sem = bufs[b]
            pltpu.make_async_copy(ys_hbm.at[idx], rows, sem).wait()
            pltpu.sync_copy(rows, out_hbm.at[pl.ds(offset(j), ch)])

        fetch(0, 0)

        @pl.loop(0, n_chunks // 2)
        def _(p):
            j = 2 * p
            fetch(j + 1, 1)
            flush(j, 0)

            @pl.when(j + 2 < n_chunks)
            def _():
                fetch(j + 2, 0)

            flush(j + 1, 1)

        if n_chunks % 2:
            flush(n_chunks - 1, 0)

    return pull(ysorted, pos_flat)


def _moe_ffn_body(*refs):
    for k in range(MOE_TILES_PER_STEP):
        _moe_ffn_tile(k, *refs)


def _moe_ffn_tile(k, te_ref, nused_ref, x_ref, wg_hbm, wu_hbm, wd_hbm, y_ref,
                  wg_f32, wu_f32, wd_f32, wgb, wub, wdb, slot_ref, sems):
    i = pl.program_id(0) * MOE_TILES_PER_STEP + k
    n_used = nused_ref[0]
    window = pl.ds(k * MOE_TILE * PACK_ROWS, MOE_TILE * PACK_ROWS)
    x_ref, y_ref = x_ref.at[window, :], y_ref.at[window, :]

    def fetch(expert, slot):
        return (pltpu.make_async_copy(wg_hbm.at[expert], wg_f32.at[slot], sems.at[slot, 0]),
                pltpu.make_async_copy(wu_hbm.at[expert], wu_f32.at[slot], sems.at[slot, 1]),
                pltpu.make_async_copy(wd_hbm.at[expert], wd_f32.at[slot], sems.at[slot, 2]))

    @pl.when(i >= n_used)
    def _unused_tile():
        y_ref[...] = jnp.zeros_like(y_ref)

    @pl.when(i < n_used)
    def _tile():
        expert = te_ref[i]

        @pl.when(i == 0)
        def _first_fetch():
            slot_ref[0] = 0
            for c in fetch(expert, 0):
                c.start()

        @pl.when((i == 0) | (expert != te_ref[jnp.maximum(i - 1, 0)]))
        def _new_expert():
            slot = slot_ref[0]
            nxt = lax.while_loop(lambda k: (k < n_used) & (te_ref[jnp.minimum(k, n_used - 1)] == expert),
                                 lambda k: k + 1, i + 1)

            @pl.when(nxt < n_used)
            def _prefetch():
                for c in fetch(te_ref[jnp.minimum(nxt, n_used - 1)], 1 - slot):
                    c.start()

            for c in fetch(expert, slot):
                c.wait()
            wgb[...] = wg_f32[slot].astype(BF16)
            wub[...] = wu_f32[slot].astype(BF16)
            wdb[...] = wd_f32[slot].astype(BF16)
            slot_ref[0] = 1 - slot

        x = _unpack_bf16_pairs(x_ref, MOE_TILE).astype(BF16)
        gate = _dot(x, wgb[...])
        hmid = (gate * jax.nn.sigmoid(gate)) * _dot(x, wub[...])
        y = _dot(hmid.astype(BF16), wdb[...])
        packed = _pack_bf16_pairs(y)
        for j in range(PACK_ROWS):
            y_ref[pl.ds(j, MOE_TILE, stride=PACK_ROWS), :] = packed[:, j * LANES:(j + 1) * LANES]


def _moe_ffn(tile_expert, n_used, xsorted, w_gate, w_up, w_down):
    n_tiles = tile_expert.shape[0]
    per_step = MOE_TILES_PER_STEP
    assert n_tiles % per_step == 0
    hbm = pl.BlockSpec(memory_space=pl.ANY)
    tile = lambda imap: pl.BlockSpec((per_step * MOE_TILE * PACK_ROWS, LANES), imap)
    up_shape, down_shape = (D_MODEL, MOE_D_FF), (MOE_D_FF, D_MODEL)
    return pl.pallas_call(
        _moe_ffn_body,
        grid_spec=pltpu.PrefetchScalarGridSpec(
            num_scalar_prefetch=2,
            grid=(n_tiles // per_step,),
            in_specs=[tile(lambda i, te, nu: (jnp.clip(i, 0, jnp.maximum(nu[0] - 1, 0) // per_step), 0)),
                      hbm, hbm, hbm],
            out_specs=tile(lambda i, te, nu: (i, 0)),
            scratch_shapes=[pltpu.VMEM((2,) + up_shape, F32), pltpu.VMEM((2,) + up_shape, F32),
                            pltpu.VMEM((2,) + down_shape, F32),
                            pltpu.VMEM(up_shape, BF16), pltpu.VMEM(up_shape, BF16), pltpu.VMEM(down_shape, BF16),
                            pltpu.SMEM((1,), jnp.int32), pltpu.SemaphoreType.DMA((2, 3))]),
        out_shape=jax.ShapeDtypeStruct((n_tiles * MOE_TILE * PACK_ROWS, LANES), jnp.uint32),
        compiler_params=pltpu.CompilerParams(dimension_semantics=("arbitrary",), vmem_limit_bytes=VMEM_LIMIT),
        name="moe_ffn",
    )(tile_expert, n_used, xsorted, w_gate, w_up, w_down)


def _combine_body(x1_ref, rt_ref, ya_ref, yb_ref, nf_ref, *rest):
    out_ref = rest[-1]
    rt = rt_ref[...]
    x1 = x1_ref[...]
    tm = x1.shape[0]

    x2 = (x1 + rt[:, 2:3] * _unpack_bf16_pairs(ya_ref.at[0], tm)
          + rt[:, 3:4] * _unpack_bf16_pairs(yb_ref.at[0], tm))
    out_ref[...] = _rms(x2, nf_ref[...])


def _combine(x1, rt, y_picks, nf, tm, rows, x_block, y_block, out_rows, out_block, out_buf=None):
    row = lambda w: pl.BlockSpec((tm, w), lambda i: (i + x_block, 0))
    pick = lambda k: pl.BlockSpec((1, tm * PACK_ROWS, LANES), lambda i: (k, i + y_block, 0))
    in_specs = [row(D_MODEL), row(LANES), pick(0), pick(1), pl.BlockSpec((1, D_MODEL), lambda i: (0, 0))]
    args = [x1, rt, y_picks, y_picks, nf]
    aliases = {}
    if out_buf is not None:
        in_specs.append(pl.BlockSpec(memory_space=pl.ANY))
        aliases[len(args)] = 0
        args.append(out_buf)
    return pl.pallas_call(
        _combine_body,
        grid=(rows // tm,),
        in_specs=in_specs,
        out_specs=pl.BlockSpec((tm, D_MODEL), lambda i: (i + out_block, 0)),
        out_shape=jax.ShapeDtypeStruct((out_rows, D_MODEL), F32),
        input_output_aliases=aliases,
        compiler_params=pltpu.CompilerParams(dimension_semantics=("parallel",), vmem_limit_bytes=VMEM_LIMIT),
        name="moe_combine",
    )(*args)


def _s5_tables(a_re, a_im, log_dt, b_re, b_im, c_re, c_im):
    dt = jnp.exp(log_dt)[:, None]
    mag = jnp.exp(a_re * dt)
    ab_re = mag * jnp.cos(a_im * dt)
    ab_im = mag * jnp.sin(a_im * dt)
    den = a_re * a_re + a_im * a_im
    nr = ab_re - 1.0
    q_re = (nr * a_re + ab_im * a_im) / den
    q_im = (ab_im * a_re - nr * a_im) / den
    bb_re = q_re[..., None] * b_re - q_im[..., None] * b_im
    bb_im = q_re[..., None] * b_im + q_im[..., None] * b_re
    nblk = S5_GROUPS // 16
    kw, nw = 16 * S5_GROUP_CH, 16 * S5_STATE
    same_group = (jnp.arange(kw)[:, None] // S5_GROUP_CH) == (jnp.arange(nw)[None, :] // S5_STATE)

    def in_map(bb):
        rows = bb.reshape(nblk, 16, S5_STATE, S5_GROUP_CH).transpose(0, 1, 3, 2).reshape(nblk, kw, S5_STATE)
        return jnp.where(same_group, jnp.tile(rows, (1, 1, 16)), 0.0)

    def out_map(cc):
        cols = cc.reshape(nblk, 16, S5_GROUP_CH, S5_STATE).transpose(0, 3, 1, 2).reshape(nblk, S5_STATE, kw)
        return jnp.where(same_group.T, jnp.tile(cols, (1, 16, 1)), 0.0)

    wb = jnp.concatenate([in_map(bb_re), in_map(bb_im)], axis=-1).astype(BF16)
    return (wb, ab_re.reshape(1, S5_LANES), ab_im.reshape(1, S5_LANES),
            out_map(c_re).astype(BF16), out_map(-c_im).astype(BF16))


def kernel(x_prompt, x_sample, state_ssd_conv, state_ssd_ssm, state_s5_re, state_s5_im, meta_tokens, norm_mix, w_in, conv_w, conv_b, dt_bias, a_log, d_ssd, ssd_norm, s5_a_re, s5_a_im, s5_log_dt, s5_b_re, s5_b_im, s5_c_re, s5_c_im, s5_d, w_glu, b_glu, s5_norm, w_out, norm_ffn, router_coarse_w, router_coarse_b, router_fine_w, router_fine_b, w_gate, w_up, w_down, norm_final):
    bp, seq, _ = x_prompt.shape
    bs = x_sample.shape[0]
    n_prompt = bp * seq
    n_tok = n_prompt + bs
    row2 = lambda v: v.reshape(1, -1)
    pad_heads = lambda v: jnp.pad(v, (0, LANES - SSD_HEADS)).reshape(1, LANES)

    w = w_in[0]
    o1, o2, o3 = SSD_WIDTH, SSD_WIDTH + SSD_CONV_DIM, SSD_WIDTH + SSD_CONV_DIM + SSD_HEADS
    wz, wx, wu = w[:, :o1].astype(BF16), w[:, o1:o2].astype(BF16), w[:, o3:].astype(BF16)
    wdt = jnp.pad(w[:, o2:o3], ((0, 0), (0, LANES - SSD_HEADS))).astype(BF16)
    g_mix = row2(norm_mix[0])
    cw, cb = conv_w[0], row2(conv_b[0])
    dtb, alog = pad_heads(dt_bias[0]), pad_heads(a_log[0])
    dexp = row2(jnp.repeat(d_ssd[0], SSD_HEAD_DIM))
    snrm = row2(ssd_norm[0])
    eexp = (jnp.arange(LANES)[:, None] == (jnp.arange(SSD_WIDTH) // SSD_HEAD_DIM)[None, :]).astype(BF16)
    wb5, ab_re, ab_im, wcr, wci = _s5_tables(s5_a_re[0], s5_a_im[0], s5_log_dt[0], s5_b_re[0], s5_b_im[0],
                                             s5_c_re[0], s5_c_im[0])
    d5, wglu, bglu, nrm5 = row2(s5_d[0]), w_glu[0].astype(BF16), row2(b_glu[0]), row2(s5_norm[0])
    wo_a, wo_b = w_out[0][:SSD_WIDTH].astype(BF16), w_out[0][SSD_WIDTH:].astype(BF16)
    w_r = jnp.concatenate([router_coarse_w[0], router_fine_w[0].transpose(1, 0, 2).reshape(D_MODEL, MOE_EXPERTS)], axis=1)
    w_r = jnp.pad(w_r, ((0, 0), (0, LANES - w_r.shape[1])))
    wrh = w_r.astype(BF16)
    wrl = (w_r - wrh.astype(F32)).astype(BF16)
    b_r = jnp.concatenate([router_coarse_b[0], router_fine_b[0].reshape(-1)])
    b_r = jnp.pad(b_r, (0, LANES - b_r.shape[0])).reshape(1, LANES)

    zp, xbcp, dtp, up = _in_proj(x_prompt.reshape(n_prompt, D_MODEL), g_mix, wz, wx, wdt, wu, IN_PROJ_TILE, BF16, F32)
    xsm = jnp.concatenate([x_sample.reshape(bs, D_MODEL), meta_tokens], axis=0)
    zs, xbcs, dts, us = _in_proj(xsm, g_mix, wz, wx, wdt, wu, xsm.shape[0], F32, F32)

    front = SSD_CHUNK - N_META
    padf = lambda a: jnp.pad(a[bs:], ((front, 0), (0, 0)))[None]
    gw = SSD_HPG * SSD_HEAD_DIM
    ssd_consts = (cw, cb, dtb, alog, dexp, snrm, eexp)
    _, ctail_m, _, ht_m = _ssd_chunked(
        padf(xbcs).astype(BF16), padf(dts), jnp.zeros((1, SSD_CHUNK, SSD_WIDTH), F32),
        jnp.zeros((1, SUBLANES, SSD_CONV_DIM), F32), jnp.zeros((1, SSD_GROUPS, SSD_STATE, gw), F32),
        *ssd_consts, mask_rows=front)
    y_ssd_p, ctail_p, ssm_p, _ = _ssd_chunked(
        xbcp.reshape(bp, seq, SSD_CONV_DIM), dtp.reshape(bp, seq, LANES), zp.reshape(bp, seq, SSD_WIDTH),
        ctail_m, ht_m, *ssd_consts, mask_rows=0)

    abr8, abi8 = jnp.broadcast_to(ab_re, (bp, S5_LANES)), jnp.broadcast_to(ab_im, (bp, S5_LANES))
    um8 = jnp.repeat(us[bs:], bp, axis=0).astype(BF16)
    y_s5_p, s5re_p, s5im_p = _s5_seq(up.reshape(bp, seq, S5_WIDTH), um8, wb5, abr8, abi8,
                                     wcr, wci, d5, wglu, bglu, nrm5)

    cst = state_ssd_conv[0]
    xt_s, dt_s, dec_s, bc, xs_s = _ssd_step_prep(xbcs[:bs], cst[:, 0], cst[:, 1], cst[:, 2], dts[:bs],
                                                 cw, cb, dtb, alog)
    ssm_s, y_core = _ssd_step(dt_s[:, :SSD_HEADS].reshape(-1), dec_s[:, :SSD_HEADS].reshape(-1),
                              state_ssd_ssm[0], xt_s, bc)
    y_ssd_s, y_s5_s, s5re_s, s5im_s = _sample_post(
        y_core, xs_s, zs[:bs], dexp, snrm, us[:bs], state_s5_re[0].reshape(bs, S5_LANES),
        state_s5_im[0].reshape(bs, S5_LANES), wb5, ab_re, ab_im, wcr, wci, d5, wglu, bglu, nrm5)

    route_consts = (wo_a, wo_b, row2(norm_ffn[0]), wrh, wrl, b_r)
    n_tiles = -(-2 * n_tok // MOE_TILE) + MOE_EXPERTS
    n_tiles = -(-n_tiles // MOE_TILES_PER_STEP) * MOE_TILES_PER_STEP
    x1, xn, rt, pos, meta = _mix_route(
        (x_prompt.reshape(n_prompt, D_MODEL), y_ssd_p.reshape(n_prompt, SSD_WIDTH), y_s5_p.reshape(n_prompt, S5_WIDTH)),
        (x_sample.reshape(bs, D_MODEL), y_ssd_s, y_s5_s), route_consts, IN_PROJ_TILE, n_tiles)

    pos_a, pos_b = pos[0], pos[1]
    tile_expert, n_used = meta[0, :n_tiles], meta[1, :1]
    xsorted = _sc_dispatch(xn, pos_a, pos_b, n_tiles * MOE_TILE)
    ysorted = _moe_ffn(tile_expert, n_used, xsorted.reshape(-1, LANES), w_gate[0], w_up[0], w_down[0])
    nfin = row2(norm_final)

    half = n_prompt // 2

    def collect(lo, hi, ch):
        picks = jnp.concatenate([pos_a[lo:hi], pos_b[lo:hi]])
        packed_rows = ysorted.reshape(-1, PACK_ROWS, LANES)
        return _sc_collect(packed_rows, picks, ch).reshape(2, (hi - lo) * PACK_ROWS, LANES)

    picks_1 = collect(0, half, SC_COLLECT_ROWS[0])
    picks_2 = collect(half, n_tok, SC_COLLECT_ROWS[1])
    blocks = half // TOK_TILE
    y_p = _combine(x1, rt, picks_1, nfin, TOK_TILE, half, 0, 0, n_prompt, 0)
    y_p = _combine(x1, rt, picks_2, nfin, TOK_TILE, half, blocks, 0, n_prompt, blocks, out_buf=y_p)
    y_s = _combine(x1, rt, picks_2, nfin, bs, bs, n_prompt // bs, half // bs, bs, 0)

    s5_state = lambda a, b: a.reshape(1, b, S5_GROUPS, S5_STATE)
    new_conv_s = jnp.stack([cst[:, 1], cst[:, 2], xbcs[:bs]], axis=1)[None]
    return (y_p.reshape(bp, seq, D_MODEL), y_s.reshape(bs, 1, D_MODEL),
            ctail_p[:, SUBLANES - (SSD_CONV - 1):][None], ssm_p[None], s5_state(s5re_p, bp), s5_state(s5im_p, bp),
            new_conv_s, ssm_s[None], s5_state(s5re_s, bs), s5_state(s5im_s, bs))
```

```python
import functools

import jax
import jax.numpy as jnp
from jax import lax
from jax.experimental import pallas as pl
from jax.experimental.pallas import tpu as pltpu
from jax.experimental.pallas import tpu_sc as plsc

F32, BF16 = jnp.float32, jnp.bfloat16

D_MODEL = 1024
N_META = 16
SSD_WIDTH = 1024
SSD_HEAD_DIM = 64
SSD_HEADS = 16
SSD_GROUPS = 2
SSD_HPG = SSD_HEADS // SSD_GROUPS
SSD_STATE = 128
SSD_CONV = 4
SSD_CHUNK = 128
SSD_CONV_DIM = SSD_WIDTH + 2 * SSD_GROUPS * SSD_STATE
S5_WIDTH = 1024
S5_GROUP_CH = 16
S5_GROUPS = 64
S5_STATE = 64
S5_LANES = S5_GROUPS * S5_STATE
MOE_GROUPS = 4
MOE_EPG = 8
MOE_EXPERTS = MOE_GROUPS * MOE_EPG
MOE_D_FF = 512
EPS = 1e-6

LANES = 128
SUBLANES = 8
VMEM_LIMIT = 56 * 1024 * 1024

SSD_CHUNKS_PER_STEP = 4
S5_TIME_TILE = 64
S5_SCAN_LANES = 512
MOE_TILE = 512
MOE_TILES_PER_STEP = 4
SLAB_ROWS = D_MODEL // LANES
PACK_ROWS = SLAB_ROWS // 2
SC_CORES = 2
SC_SUBCORES = 16
SC_WORKERS = SC_CORES * SC_SUBCORES
SC_DISPATCH_ROWS = 64
SC_COLLECT_ROWS = (64, 104)
TOK_TILE = 512
IN_PROJ_TILE = 1024


def _dot(a, b):
    return jnp.dot(a, b, preferred_element_type=F32)


def _rms(x, g):
    return x * lax.rsqrt(jnp.mean(x * x, axis=-1, keepdims=True) + EPS) * g


def _softplus(x):
    return jnp.maximum(x, 0.0) + jnp.log1p(jnp.exp(-jnp.abs(x)))


def _split3(x):
    hi = x.astype(BF16)
    r = x - hi.astype(F32)
    mid = r.astype(BF16)
    lo = (r - mid.astype(F32)).astype(BF16)
    return hi, mid, lo


def _dot3(x, w):
    hi, mid, lo = _split3(x)
    return _dot(hi, w) + _dot(mid, w) + _dot(lo, w)


def _dot3_left(w, x):
    hi, mid, lo = _split3(x)
    return _dot(w, hi) + _dot(w, mid) + _dot(w, lo)


def _pack_bf16_pairs(x):
    bits = pltpu.bitcast(x.astype(BF16).astype(F32), jnp.uint32)
    half = x.shape[1] // 2
    return (bits[:, :half] & jnp.uint32(0xFFFF0000)) | (bits[:, half:] >> jnp.uint32(16))


def _unpack_bf16_pairs(ref, rows):
    words = [ref[pl.ds(j, rows, stride=PACK_ROWS), :] for j in range(PACK_ROWS)]
    high = [pltpu.bitcast(w & jnp.uint32(0xFFFF0000), F32) for w in words]
    low = [pltpu.bitcast(w << jnp.uint32(16), F32) for w in words]
    return jnp.concatenate(high + low, axis=-1)


def _full_spec(a):
    nd = a.ndim
    return pl.BlockSpec(a.shape, lambda *_: (0,) * nd)


def _resident_spec(a):
    nd = a.ndim
    return pl.BlockSpec(a.shape, lambda *_: (0,) * nd, pipeline_mode=pl.Buffered(1))


def _in_proj_body(x_ref, g_ref, wz_ref, wx_ref, wdt_ref, wu_ref, z_ref, xbc_ref, dt_ref, u_ref):
    xb = _rms(x_ref[...], g_ref[...]).astype(BF16)
    z_ref[...] = _dot(xb, wz_ref[...]).astype(z_ref.dtype)
    xbc_ref[...] = _dot(xb, wx_ref[...]).astype(xbc_ref.dtype)
    dt_ref[...] = _dot(xb, wdt_ref[...])
    u_ref[...] = _dot(xb, wu_ref[...]).astype(u_ref.dtype)


def _in_proj(x2d, g, wz, wx, wdt, wu, tm, act_dtype, u_dtype):
    rows = x2d.shape[0]
    row = lambda w: pl.BlockSpec((tm, w), lambda i: (i, 0))
    return pl.pallas_call(
        _in_proj_body,
        grid=(rows // tm,),
        in_specs=[row(D_MODEL)] + [_resident_spec(a) for a in (g, wz, wx, wdt, wu)],
        out_specs=[row(SSD_WIDTH), row(SSD_CONV_DIM), row(LANES), row(S5_WIDTH)],
        out_shape=[jax.ShapeDtypeStruct((rows, SSD_WIDTH), act_dtype),
                   jax.ShapeDtypeStruct((rows, SSD_CONV_DIM), act_dtype),
                   jax.ShapeDtypeStruct((rows, LANES), F32),
                   jax.ShapeDtypeStruct((rows, S5_WIDTH), u_dtype)],
        compiler_params=pltpu.CompilerParams(dimension_semantics=("parallel",), vmem_limit_bytes=VMEM_LIMIT),
        name="in_proj",
    )(x2d, g, wz, wx, wdt, wu)


def _ssd_body(mask_rows, per_step, *refs):
    for k in range(per_step):
        _ssd_chunk(mask_rows, per_step, k, *refs)


def _ssd_chunk(mask_rows, per_step, k, xbc_ref, dt_ref, z_ref, cinit_ref, hinit_ref, cw_ref, cb_ref, dtb_ref,
               alog_ref, dexp_ref, nrm_ref, eexp_ref, y_ref, ctail_ref, st_ref, hto_ref, xwin, hT):
    L = SSD_CHUNK
    c = pl.program_id(1) * per_step + k
    n_chunks = pl.num_programs(1) * per_step
    window = pl.ds(k * L, L)
    xbc_ref, dt_ref, z_ref, y_ref = (r.at[:, window, :] for r in (xbc_ref, dt_ref, z_ref, y_ref))

    @pl.when(c == 0)
    def _init():
        xwin[...] = cinit_ref[0]
        hT[...] = hinit_ref[0]

    x_b = xbc_ref[0]
    x_f = x_b.astype(F32)
    taps = SSD_CONV - 1
    m_i = lax.broadcasted_iota(jnp.int32, (taps * L, L), 0)
    r_i = lax.broadcasted_iota(jnp.int32, (taps * L, L), 1)
    shift = (r_i + (taps - m_i // L) == m_i % L).astype(BF16)
    shifted = _dot(shift, x_b)
    acc = cb_ref[...] + x_f * cw_ref[taps:taps + 1, :]
    for k in range(taps):
        acc = acc + shifted[k * L:(k + 1) * L, :] * cw_ref[k:k + 1, :]
    joint = jnp.concatenate([xwin[...], x_f[0:SUBLANES, :]], axis=0)
    row8 = lax.broadcasted_iota(jnp.int32, (SUBLANES, 1), 0)
    head = acc[0:SUBLANES, :]
    for k in range(taps):
        d = taps - k
        head = head + jnp.where(row8 < d, joint[SUBLANES - d:2 * SUBLANES - d, :], 0.0) * cw_ref[k:k + 1, :]
    acc = jnp.concatenate([head, acc[SUBLANES:, :]], axis=0)
    tail = x_f[L - SUBLANES:, :]
    xwin[...] = tail
    ctail_ref[0] = tail

    xact = acc * jax.nn.sigmoid(acc)
    dt = _softplus(dt_ref[0] + dtb_ref[...])
    if mask_rows:
        valid = lax.broadcasted_iota(jnp.int32, (L, 1), 0) >= mask_rows
        xact = jnp.where(valid, xact, 0.0)
        dt = jnp.where(valid, dt, 0.0)

    a_neg = -jnp.exp(alog_ref[...])
    dA = dt * a_neg
    row_i = lax.broadcasted_iota(jnp.int32, (L, L), 0)
    col_i = lax.broadcasted_iota(jnp.int32, (L, L), 1)
    causal = row_i >= col_i
    tril = causal.astype(BF16)
    cs = _dot3_left(tril, dA)
    csT = cs.T
    dtT = dt.T
    ecs = jnp.exp(cs)
    wdec = jnp.exp(cs[L - 1:L, :] - cs) * dt
    eexp = eexp_ref[...]
    ecs_e = _dot3(ecs, eexp)
    wdec_e = _dot3(wdec, eexp)
    lane = lax.broadcasted_iota(jnp.int32, (L, LANES), 1)
    first_half = lane < SSD_HEAD_DIM

    gw = SSD_HPG * SSD_HEAD_DIM
    y_groups = []
    for g in range(SSD_GROUPS):
        b_g = xact[:, SSD_WIDTH + g * SSD_STATE: SSD_WIDTH + (g + 1) * SSD_STATE]
        c_g = xact[:, SSD_WIDTH + (SSD_GROUPS + g) * SSD_STATE: SSD_WIDTH + (SSD_GROUPS + g + 1) * SSD_STATE]
        b_b = b_g.astype(BF16)
        c_b = c_g.astype(BF16)
        cb = lax.dot_general(c_b, b_b, (((1,), (1,)), ((), ())), preferred_element_type=F32)
        xs_g = xact[:, g * gw:(g + 1) * gw]
        h_prev = hT[g]
        y_off = _dot(c_b, h_prev.astype(BF16)) * ecs_e[:, g * gw:(g + 1) * gw]
        xdec = (xs_g * wdec_e[:, g * gw:(g + 1) * gw]).astype(BF16)
        hT[g] = h_prev * ecs_e[L - 1:L, g * gw:(g + 1) * gw] + _dot(b_g.T.astype(BF16), xdec)
        pieces = []
        for j in range(SSD_HPG // 2):
            xs_pair = xs_g[:, j * LANES:(j + 1) * LANES]
            halves = (jnp.where(first_half, xs_pair, 0.0).astype(BF16),
                      jnp.where(first_half, 0.0, xs_pair).astype(BF16))
            yd = None
            for t in range(2):
                h = g * SSD_HPG + 2 * j + t
                seg = cs[:, h:h + 1] - csT[h:h + 1, :]
                lmat = jnp.exp(jnp.where(causal, seg, -jnp.inf))
                m = (cb * lmat * dtT[h:h + 1, :]).astype(BF16)
                part = _dot(m, halves[t])
                yd = part if yd is None else yd + part
            pieces.append(yd)
        y_groups.append(jnp.concatenate(pieces, axis=-1) + y_off + dexp_ref[:, g * gw:(g + 1) * gw] * xs_g)
    y = jnp.concatenate(y_groups, axis=-1)
    z = z_ref[0].astype(F32)
    y_ref[0] = _rms(y * (z * jax.nn.sigmoid(z)), nrm_ref[...]).astype(y_ref.dtype)

    @pl.when(c == n_chunks - 1)
    def _emit():
        hto_ref[0] = hT[...]
        for g in range(SSD_GROUPS):
            t = hT[g].T
            for k in range(SSD_HPG):
                st_ref[0, g * SSD_HPG + k] = t[k * SSD_HEAD_DIM:(k + 1) * SSD_HEAD_DIM, :]


def _ssd_chunked(xbc, dt, z, cinit, hinit, cw, cb, dtb, alog, dexp, nrm, eexp, mask_rows):
    bsz, seq, _ = xbc.shape
    nc = seq // SSD_CHUNK
    per_step = SSD_CHUNKS_PER_STEP if nc % SSD_CHUNKS_PER_STEP == 0 else 1
    gw = SSD_HPG * SSD_HEAD_DIM
    blk = lambda w: pl.BlockSpec((1, per_step * SSD_CHUNK, w), lambda b, c: (b, c, 0))
    return pl.pallas_call(
        functools.partial(_ssd_body, mask_rows, per_step),
        grid=(bsz, nc // per_step),
        in_specs=[blk(SSD_CONV_DIM), blk(LANES), blk(SSD_WIDTH),
                  pl.BlockSpec((1, SUBLANES, SSD_CONV_DIM), lambda b, c: (0, 0, 0)),
                  pl.BlockSpec((1, SSD_GROUPS, SSD_STATE, gw), lambda b, c: (0, 0, 0, 0)),
                  _full_spec(cw), _full_spec(cb), _full_spec(dtb), _full_spec(alog),
                  _full_spec(dexp), _full_spec(nrm), _full_spec(eexp)],
        out_specs=[blk(SSD_WIDTH),
                   pl.BlockSpec((1, SUBLANES, SSD_CONV_DIM), lambda b, c: (b, 0, 0)),
                   pl.BlockSpec((1, SSD_HEADS, SSD_HEAD_DIM, SSD_STATE), lambda b, c: (b, 0, 0, 0)),
                   pl.BlockSpec((1, SSD_GROUPS, SSD_STATE, gw), lambda b, c: (b, 0, 0, 0))],
        out_shape=[jax.ShapeDtypeStruct((bsz, seq, SSD_WIDTH), BF16),
                   jax.ShapeDtypeStruct((bsz, SUBLANES, SSD_CONV_DIM), F32),
                   jax.ShapeDtypeStruct((bsz, SSD_HEADS, SSD_HEAD_DIM, SSD_STATE), F32),
                   jax.ShapeDtypeStruct((bsz, SSD_GROUPS, SSD_STATE, gw), F32)],
        scratch_shapes=[pltpu.VMEM((SUBLANES, SSD_CONV_DIM), F32),
                        pltpu.VMEM((SSD_GROUPS, SSD_STATE, gw), F32)],
        compiler_params=pltpu.CompilerParams(dimension_semantics=("parallel", "arbitrary"),
                                             vmem_limit_bytes=VMEM_LIMIT),
        name="ssd_chunked",
    )(xbc, dt, z, cinit, hinit, cw, cb, dtb, alog, dexp, nrm, eexp)


def _ssd_step_prep_body(xbc_ref, c0_ref, c1_ref, c2_ref, dt_ref, cw_ref, cb_ref, dtb_ref, alog_ref,
                        xt_ref, dt_out_ref, dec_ref, bc_ref, xs_ref):
    acc = cb_ref[...]
    for k, r in enumerate((c0_ref, c1_ref, c2_ref, xbc_ref)):
        acc = acc + r[...] * cw_ref[k:k + 1, :]
    xact = acc * jax.nn.sigmoid(acc)
    xs = xact[:, :SSD_WIDTH]
    dt = _softplus(dt_ref[...] + dtb_ref[...])
    dt_out_ref[...] = dt
    dec_ref[...] = jnp.exp(dt * -jnp.exp(alog_ref[...]))
    bc_ref[...] = xact[:, SSD_WIDTH:]
    xs_ref[...] = xs
    xt_ref[...] = xs.T.astype(xt_ref.dtype)


def _ssd_step_prep(xbc, c0, c1, c2, dt, cw, cb, dtb, alog):
    n = xbc.shape[0]
    args = (xbc, c0, c1, c2, dt, cw, cb, dtb, alog)
    spec = lambda r, w: pl.BlockSpec((r, w), lambda: (0, 0))
    return pl.pallas_call(
        _ssd_step_prep_body,
        in_specs=[_full_spec(a) for a in args],
        out_specs=[spec(SSD_WIDTH, n), spec(n, LANES), spec(n, LANES), spec(n, 2 * SSD_GROUPS * SSD_STATE),
                   spec(n, SSD_WIDTH)],
        out_shape=[jax.ShapeDtypeStruct((SSD_WIDTH, n), BF16), jax.ShapeDtypeStruct((n, LANES), F32),
                   jax.ShapeDtypeStruct((n, LANES), F32),
                   jax.ShapeDtypeStruct((n, 2 * SSD_GROUPS * SSD_STATE), F32),
                   jax.ShapeDtypeStruct((n, SSD_WIDTH), F32)],
        compiler_params=pltpu.CompilerParams(vmem_limit_bytes=VMEM_LIMIT),
        name="ssd_step_prep",
    )(*args)


def _ssd_step_body(dt_ref, dec_ref, st_ref, xt_ref, bc_ref, so_ref, y_ref):
    n = xt_ref.shape[1]
    gw = SSD_HPG * SSD_HEAD_DIM
    blk = pl.program_id(0)
    seq_id = lax.broadcasted_iota(jnp.int32, (n, SSD_STATE), 0)
    sub_id = lax.broadcasted_iota(jnp.int32, (SUBLANES, gw), 0)
    base = pl.multiple_of(blk * SUBLANES, SUBLANES)
    y_acc = [jnp.zeros((SUBLANES, gw), F32) for _ in range(SSD_GROUPS)]
    for i in range(SUBLANES):
        s = blk * SUBLANES + i
        for g in range(SSD_GROUPS):
            b_all = bc_ref[:, g * SSD_STATE:(g + 1) * SSD_STATE]
            rhs = jnp.where(seq_id == s, b_all, 0.0).astype(BF16)
            outer = _dot(xt_ref[g * gw:(g + 1) * gw, :], rhs)
            news = []
            for k in range(SSD_HPG):
                h = g * SSD_HPG + k
                new = (dec_ref[s * SSD_HEADS + h] * st_ref[i, h]
                       + dt_ref[s * SSD_HEADS + h] * outer[k * SSD_HEAD_DIM:(k + 1) * SSD_HEAD_DIM, :])
                so_ref[i, h] = new
                news.append(new)
            new_g = jnp.concatenate(news, axis=0).astype(BF16)
            c_lo = (SSD_GROUPS + g) * SSD_STATE
            c_blk = bc_ref[pl.ds(base, SUBLANES), c_lo:c_lo + SSD_STATE].astype(BF16)
            r = lax.dot_general(c_blk, new_g, (((1,), (1,)), ((), ())), preferred_element_type=F32)
            y_acc[g] = y_acc[g] + jnp.where(sub_id == i, r, 0.0)
    y_ref[...] = jnp.concatenate(y_acc, axis=-1)


def _ssd_step(dt_flat, dec_flat, state, xt, bc):
    n = state.shape[0]
    st_spec = pl.BlockSpec((SUBLANES, SSD_HEADS, SSD_HEAD_DIM, SSD_STATE), lambda i, *_: (i, 0, 0, 0))
    return pl.pallas_call(
        _ssd_step_body,
        grid_spec=pltpu.PrefetchScalarGridSpec(
            num_scalar_prefetch=2,
            grid=(n // SUBLANES,),
            in_specs=[st_spec, pl.BlockSpec(xt.shape, lambda i, *_: (0, 0)),
                      pl.BlockSpec(bc.shape, lambda i, *_: (0, 0))],
            out_specs=[st_spec, pl.BlockSpec((SUBLANES, SSD_WIDTH), lambda i, *_: (i, 0))]),
        out_shape=[jax.ShapeDtypeStruct(state.shape, F32), jax.ShapeDtypeStruct((n, SSD_WIDTH), F32)],
        compiler_params=pltpu.CompilerParams(dimension_semantics=("parallel",), vmem_limit_bytes=VMEM_LIMIT),
        name="ssd_step",
    )(dt_flat, dec_flat, state, xt, bc)


def _s5_project_in(u_b16, wb_ref, store):
    kw = 16 * S5_GROUP_CH
    nw = 16 * S5_STATE
    for j in range(S5_WIDTH // kw):
        r = _dot(u_b16[:, j * kw:(j + 1) * kw], wb_ref[j])
        store(j, r[:, :nw], r[:, nw:])


def _s5_tail(hre_of, him_of, u_f32, wcr_ref, wci_ref, d_ref, wglu_ref, bglu_ref, nrm_ref):
    cols = []
    for j in range(wcr_ref.shape[0]):
        cols.append(_dot(hre_of(j).astype(BF16), wcr_ref[j]) + _dot(him_of(j).astype(BF16), wci_ref[j]))
    return _s5_finish(cols, u_f32, d_ref, wglu_ref, bglu_ref, nrm_ref)


def _s5_finish(cols, u_f32, d_ref, wglu_ref, bglu_ref, nrm_ref):
    y = jnp.concatenate(cols, axis=-1) + d_ref[...] * u_f32
    y = jax.nn.gelu(y)
    y = y * jax.nn.sigmoid(_dot(y.astype(BF16), wglu_ref[...]) + bglu_ref[...])
    return _rms(y, nrm_ref[...])


def _s5_seq_body(u_hbm, um_ref, wb_ref, abr_ref, abi_ref, wcr_ref, wci_ref, d_ref, wglu_ref, bglu_ref, nrm_ref,
                 y_hbm, sre_ref, sim_ref, ubuf, ybuf, bu, h, in_sems, out_sems):
    j = pl.program_id(0)
    last = pl.num_programs(0) - 1
    lc, bsz = ubuf.shape[1], ubuf.shape[2]
    rows = lc * bsz
    nw = 16 * S5_STATE

    def in_copy(step, b):
        return pltpu.make_async_copy(u_hbm.at[b, pl.ds(step * lc, lc), :], ubuf.at[step % 2, :, b, :],
                                     in_sems.at[step % 2, b])

    def out_copy(step, b):
        return pltpu.make_async_copy(ybuf.at[step % 2, :, b, :], y_hbm.at[b, pl.ds(step * lc, lc), :],
                                     out_sems.at[step % 2, b])

    def project_in(u_b16, nrows):
        def store(jj, re, im):
            bu[0:nrows, jj * nw:(jj + 1) * nw] = re
            bu[0:nrows, S5_LANES + jj * nw:S5_LANES + (jj + 1) * nw] = im
        _s5_project_in(u_b16, wb_ref, store)

    def scan(nsteps):
        for k in range(S5_LANES // S5_SCAN_LANES):
            sl_r = pl.ds(k * S5_SCAN_LANES, S5_SCAN_LANES)
            sl_i = pl.ds(S5_LANES + k * S5_SCAN_LANES, S5_SCAN_LANES)
            ar = abr_ref[:, sl_r]
            ai = abi_ref[:, sl_r]

            def step(l, carry):
                hr, hi = carry
                slab = pl.ds(pl.multiple_of(l * bsz, bsz), bsz)
                nr = ar * hr - ai * hi + bu[slab, sl_r]
                ni = ar * hi + ai * hr + bu[slab, sl_i]
                bu[slab, sl_r] = nr
                bu[slab, sl_i] = ni
                return nr, ni

            hr, hi = lax.fori_loop(0, nsteps, step, (h[:, sl_r], h[:, sl_i]))
            h[:, sl_r] = hr
            h[:, sl_i] = hi

    @pl.when(j == 0)
    def _first():
        for b in range(bsz):
            in_copy(0, b).start()
        h[...] = jnp.zeros_like(h)
        project_in(um_ref[...], N_META * bsz)
        scan(N_META)

    @pl.when(j < last)
    def _prefetch():
        for b in range(bsz):
            in_copy(j + 1, b).start()

    for b in range(bsz):
        in_copy(j, b).wait()
    u2 = ubuf[j % 2].reshape(rows, S5_WIDTH)
    u_b16 = u2.astype(BF16)
    kw = 16 * S5_GROUP_CH

    def project_block(jj):
        r = _dot(u_b16[:, jj * kw:(jj + 1) * kw], wb_ref[jj])
        bu[0:rows, jj * nw:(jj + 1) * nw] = r[:, :nw]
        bu[0:rows, S5_LANES + jj * nw:S5_LANES + (jj + 1) * nw] = r[:, nw:]

    def scan_block(jj):
        for k in range(nw // S5_SCAN_LANES):
            lo = jj * nw + k * S5_SCAN_LANES
            sl_r = slice(lo, lo + S5_SCAN_LANES)
            sl_i = slice(S5_LANES + lo, S5_LANES + lo + S5_SCAN_LANES)
            ar, ai = abr_ref[:, sl_r], abi_ref[:, sl_r]
            hr, hi = h[:, sl_r], h[:, sl_i]
            for l in range(lc):
                slab = slice(l * bsz, (l + 1) * bsz)
                hr, hi = (ar * hr - ai * hi + bu[slab, sl_r], ar * hi + ai * hr + bu[slab, sl_i])
                bu[slab, sl_r] = hr
                bu[slab, sl_i] = hi
            h[:, sl_r] = hr
            h[:, sl_i] = hi

    def readout_block(jj):
        return (_dot(bu[:, jj * nw:(jj + 1) * nw].astype(BF16), wcr_ref[jj])
                + _dot(bu[:, S5_LANES + jj * nw:S5_LANES + (jj + 1) * nw].astype(BF16), wci_ref[jj]))

    n_blocks = S5_WIDTH // kw
    project_block(0)
    cols = []
    for jj in range(n_blocks):
        if jj + 1 < n_blocks:
            project_block(jj + 1)
        scan_block(jj)
        cols.append(readout_block(jj))
    y = _s5_finish(cols, u2, d_ref, wglu_ref, bglu_ref, nrm_ref)
    ybuf[j % 2] = y.reshape(lc, bsz, S5_WIDTH)
    for b in range(bsz):
        out_copy(j, b).start()

    @pl.when(j > 0)
    def _wait_previous_out():
        for b in range(bsz):
            out_copy(j - 1, b).wait()

    @pl.when(j == last)
    def _emit():
        for b in range(bsz):
            out_copy(j, b).wait()
        sre_ref[...] = h[:, 0:S5_LANES]
        sim_ref[...] = h[:, S5_LANES:]


def _s5_seq(u, um, wb, abr, abi, wcr, wci, d, wglu, bglu, nrm):
    bsz, seq, _ = u.shape
    lc = S5_TIME_TILE
    consts = (um, wb, abr, abi, wcr, wci, d, wglu, bglu, nrm)
    st = pl.BlockSpec((bsz, S5_LANES), lambda j: (0, 0))
    return pl.pallas_call(
        _s5_seq_body,
        grid=(seq // lc,),
        in_specs=[pl.BlockSpec(memory_space=pl.ANY)] + [_resident_spec(a) for a in consts],
        out_specs=[pl.BlockSpec(memory_space=pl.ANY), st, st],
        out_shape=[jax.ShapeDtypeStruct((bsz, seq, S5_WIDTH), F32),
                   jax.ShapeDtypeStruct((bsz, S5_LANES), F32), jax.ShapeDtypeStruct((bsz, S5_LANES), F32)],
        scratch_shapes=[pltpu.VMEM((2, lc, bsz, S5_WIDTH), F32), pltpu.VMEM((2, lc, bsz, S5_WIDTH), F32),
                        pltpu.VMEM((lc * bsz, 2 * S5_LANES), F32), pltpu.VMEM((bsz, 2 * S5_LANES), F32),
                        pltpu.SemaphoreType.DMA((2, bsz)), pltpu.SemaphoreType.DMA((2, bsz))],
        compiler_params=pltpu.CompilerParams(dimension_semantics=("arbitrary",), vmem_limit_bytes=VMEM_LIMIT),
        name="s5_seq",
    )(u, *consts)


def _sample_post_body(yc_ref, xs_ref, z_ref, dexp_ref, snrm_ref, u_ref, hr_ref, hi_ref, wb_ref, abr_ref, abi_ref,
                      wcr_ref, wci_ref, d_ref, wglu_ref, bglu_ref, nrm_ref,
                      yssd_ref, ys5_ref, nre_ref, nim_ref):
    z = z_ref[...]
    y = yc_ref[...] + dexp_ref[...] * xs_ref[...]
    yssd_ref[...] = _rms(y * (z * jax.nn.sigmoid(z)), snrm_ref[...]).astype(yssd_ref.dtype)

    u = u_ref[...]
    nw = 16 * S5_STATE
    ar, ai = abr_ref[...], abi_ref[...]

    def store(jj, re, im):
        sl = slice(jj * nw, (jj + 1) * nw)
        h0r, h0i = hr_ref[:, sl], hi_ref[:, sl]
        nre_ref[:, sl] = ar[:, sl] * h0r - ai[:, sl] * h0i + re
        nim_ref[:, sl] = ar[:, sl] * h0i + ai[:, sl] * h0r + im

    _s5_project_in(u.astype(BF16), wb_ref, store)
    slab = lambda ref: (lambda jj: ref[:, jj * nw:(jj + 1) * nw])
    y5 = _s5_tail(slab(nre_ref), slab(nim_ref), u, wcr_ref, wci_ref, d_ref, wglu_ref, bglu_ref, nrm_ref)
    ys5_ref[...] = y5.astype(ys5_ref.dtype)


def _sample_post(yc, xs, z, dexp, snrm, u, h0r, h0i, wb, abr1, abi1, wcr, wci, d, wglu, bglu, nrm):
    n = yc.shape[0]
    args = (yc, xs, z, dexp, snrm, u, h0r, h0i, wb, abr1, abi1, wcr, wci, d, wglu, bglu, nrm)
    spec = lambda w: pl.BlockSpec((n, w), lambda: (0, 0))
    return pl.pallas_call(
        _sample_post_body,
        in_specs=[_full_spec(a) for a in args],
        out_specs=[spec(SSD_WIDTH), spec(S5_WIDTH), spec(S5_LANES), spec(S5_LANES)],
        out_shape=[jax.ShapeDtypeStruct((n, SSD_WIDTH), BF16), jax.ShapeDtypeStruct((n, S5_WIDTH), BF16),
                   jax.ShapeDtypeStruct((n, S5_LANES), F32), jax.ShapeDtypeStruct((n, S5_LANES), F32)],
        compiler_params=pltpu.CompilerParams(vmem_limit_bytes=VMEM_LIMIT),
        name="sample_post",
    )(*args)


def _mix_route_body(n_blocks, n_sorted, xp_ref, ysp_ref, y5p_ref, xs_ref, yss_ref, y5s_ref, *refs):
    consts = refs[:6]
    x1_ref, xn_hbm, rt_ref, pos_ref, meta_ref, carry, fields, xbuf, sems = refs[6:]
    i = pl.program_id(0)
    tm, n_sample = xp_ref.shape[0], xs_ref.shape[0]
    col0 = pl.multiple_of(i * tm, LANES)

    def xn_copy(step, rows, j):
        return pltpu.make_async_copy(xbuf.at[step % 2, pl.ds(0, rows), pl.ds(j * LANES, LANES)],
                                     xn_hbm.at[pl.ds(step * tm, rows), j, :], sems.at[step % 2, j])

    @pl.when(i == 0)
    def _init():
        carry[...] = jnp.zeros_like(carry)

    @pl.when(i < n_blocks)
    def _prompt_rows():
        _mix_route_compute(xp_ref, ysp_ref, y5p_ref, *consts, x1_ref, rt_ref, carry, xbuf.at[i % 2], fields, col0)
        for j in range(PACK_ROWS):
            xn_copy(i, tm, j).start()

    @pl.when(i == n_blocks)
    def _sample_rows():
        _mix_route_compute(xs_ref, yss_ref, y5s_ref, *consts, x1_ref, rt_ref, carry, xbuf.at[i % 2], fields, col0)
        for j in range(PACK_ROWS):
            xn_copy(i, n_sample, j).start()
        _route_layout(carry, fields, pos_ref, meta_ref, n_sorted)
        for j in range(PACK_ROWS):
            xn_copy(i, n_sample, j).wait()

    @pl.when(i > 0)
    def _wait_previous_rows():
        for j in range(PACK_ROWS):
            xn_copy(i - 1, tm, j).wait()


def _route_layout(carry, fields, pos_ref, meta_ref, n_sorted):
    counts = carry[...]
    tiles_per = jnp.floor((counts + (MOE_TILE - 1)) * (1.0 / MOE_TILE))
    upto = lax.broadcasted_iota(jnp.int32, (LANES, LANES), 0) <= lax.broadcasted_iota(jnp.int32, (LANES, LANES), 1)
    tile_end = _dot(tiles_per.astype(BF16), upto.astype(BF16))
    pstart = (tile_end - tiles_per) * MOE_TILE
    n_used = tile_end[:, MOE_EXPERTS - 1:MOE_EXPERTS]

    f = fields[...]
    first_row = jnp.zeros_like(f)
    tile_id = jnp.minimum(lax.broadcasted_iota(jnp.int32, meta_ref.shape, 1).astype(F32), n_used - 1.0)
    tile_expert = jnp.zeros(meta_ref.shape, F32)
    for e in range(MOE_EXPERTS):
        first_row = first_row + jnp.where(f == float(e), pstart[:, e:e + 1], 0.0)
        tile_expert = tile_expert + jnp.where(tile_end[:, e:e + 1] <= tile_id, 1.0, 0.0)
    pos = first_row + pltpu.roll(f, shift=4, axis=0)
    pos_ref[...] = jnp.clip(pos, 0.0, n_sorted - 1.0).astype(jnp.int32)
    is_row0 = lax.broadcasted_iota(jnp.int32, meta_ref.shape, 0) == 0
    meta_ref[...] = jnp.where(is_row0, tile_expert, n_used).astype(jnp.int32)


def _mix_route_compute(x_ref, ys_ref, y5_ref, wa_ref, wb_ref, nf_ref, wrh_ref, wrl_ref, br_ref,
                       x1_ref, rt_ref, carry, xn_buf, fields, col0):
    rows = x_ref.shape[0]
    x1 = x_ref[...] + _dot(ys_ref[...], wa_ref[...]) + _dot(y5_ref[...].astype(BF16), wb_ref[...])
    x1_ref[0:rows, :] = x1
    xn = _rms(x1, nf_ref[...])
    xn_buf[0:rows, :] = _pack_bf16_pairs(xn)

    xh = xn.astype(BF16)
    xl = (xn - xh.astype(F32)).astype(BF16)
    logits = _dot(xh, wrh_ref[...]) + _dot(xl, wrh_ref[...]) + _dot(xh, wrl_ref[...]) + br_ref[...]
    tm = logits.shape[0]
    lane = lax.broadcasted_iota(jnp.int32, logits.shape, 1).astype(F32)
    neg = -jnp.inf
    big = float(LANES)

    def first_max(v):
        m = jnp.max(v, axis=-1, keepdims=True)
        return m, jnp.min(jnp.where(v == m, lane, big), axis=-1, keepdims=True)

    coarse = lane < MOE_GROUPS
    mc, gsel = first_max(jnp.where(coarse, logits, neg))
    psel = 1.0 / jnp.sum(jnp.where(coarse, jnp.exp(logits - mc), 0.0), axis=-1, keepdims=True)
    lo = MOE_GROUPS + MOE_EPG * gsel
    lf = jnp.where((lane >= lo) & (lane < lo + MOE_EPG), logits, neg)
    m1, i1 = first_max(lf)
    m2, i2 = first_max(jnp.where(lane == i1, neg, lf))
    e2 = jnp.exp(m2 - m1)
    g1 = psel / (1.0 + e2)
    g2 = psel * e2 / (1.0 + e2)
    e_a, e_b = i1 - MOE_GROUPS, i2 - MOE_GROUPS

    pick_a, pick_b = lane == e_a, lane == e_b
    picks = jnp.where(pick_a | pick_b, 1.0, 0.0)
    earlier = lax.broadcasted_iota(jnp.int32, (tm, tm), 0) > lax.broadcasted_iota(jnp.int32, (tm, tm), 1)
    prior = _dot(earlier.astype(BF16), picks.astype(BF16)) + carry[...]
    rank_a = jnp.sum(jnp.where(pick_a, prior, 0.0), axis=-1, keepdims=True)
    rank_b = jnp.sum(jnp.where(pick_b, prior, 0.0), axis=-1, keepdims=True)
    carry[...] = prior[tm - 1:tm, :] + picks[tm - 1:tm, :]

    out = jnp.zeros_like(logits)
    for k, v in enumerate((e_a, e_b, g1, g2, rank_a, rank_b)):
        out = jnp.where(lane == float(k), v, out)
    rt_ref[0:rows, :] = out
    fields[:, pl.ds(col0, rows)] = out.T[0:SUBLANES, :]


def _mix_route(prompt, sample, consts, tm, n_tiles):
    n_prompt, n_sample = prompt[0].shape[0], sample[0].shape[0]
    assert n_prompt % tm == 0 and n_sample <= tm
    n_blocks = n_prompt // tm
    total_rows = n_prompt + n_sample
    row = lambda w: pl.BlockSpec((tm, w), lambda i: (jnp.minimum(i, n_blocks - 1), 0))
    out_row = lambda w: pl.BlockSpec((tm, w), lambda i: (i, 0))
    assert total_rows % LANES == 0 and n_tiles <= 2 * LANES
    whole = lambda shape: pl.BlockSpec(shape, lambda i: (0, 0))
    return pl.pallas_call(
        functools.partial(_mix_route_body, n_blocks, n_tiles * MOE_TILE),
        grid=(n_blocks + 1,),
        in_specs=([row(D_MODEL), row(SSD_WIDTH), row(S5_WIDTH)] + [_full_spec(a) for a in sample]
                  + [_resident_spec(a) for a in consts]),
        out_specs=[out_row(D_MODEL), pl.BlockSpec(memory_space=pl.ANY), out_row(LANES),
                   whole((SUBLANES, total_rows)), whole((SUBLANES, 2 * LANES))],
        out_shape=[jax.ShapeDtypeStruct((total_rows, D_MODEL), F32),
                   jax.ShapeDtypeStruct((total_rows, PACK_ROWS, LANES), jnp.uint32),
                   jax.ShapeDtypeStruct((total_rows, LANES), F32),
                   jax.ShapeDtypeStruct((SUBLANES, total_rows), jnp.int32),
                   jax.ShapeDtypeStruct((SUBLANES, 2 * LANES), jnp.int32)],
        scratch_shapes=[pltpu.VMEM((1, LANES), F32), pltpu.VMEM((SUBLANES, total_rows), F32),
                        pltpu.VMEM((2, tm, D_MODEL // 2), jnp.uint32), pltpu.SemaphoreType.DMA((2, PACK_ROWS))],
        compiler_params=pltpu.CompilerParams(dimension_semantics=("arbitrary",), vmem_limit_bytes=VMEM_LIMIT),
        name="mix_route",
    )(*prompt, *sample, *consts)


def _sc_mesh():
    return plsc.VectorSubcoreMesh(core_axis_name="c", subcore_axis_name="s")


def _sc_worker():
    return lax.axis_index("s") * SC_CORES + lax.axis_index("c")


def _sc_dispatch(xn, pos_a, pos_b, n_rows):
    n_tok = xn.shape[0]
    ch = SC_DISPATCH_ROWS
    n_chunks = n_tok // ch
    assert n_tok % ch == 0 and n_chunks >= SC_WORKERS
    max_mine = -(-n_chunks // SC_WORKERS)
    row_shape, dtype = xn.shape[1:], xn.dtype
    stage = [pltpu.VMEM((ch,), jnp.int32), pltpu.VMEM((ch,), jnp.int32), pltpu.VMEM((ch,) + row_shape, dtype),
             pltpu.SemaphoreType.DMA]

    @functools.partial(
        pl.kernel, mesh=_sc_mesh(),
        out_type=jax.ShapeDtypeStruct((n_rows,) + row_shape, dtype),
        scratch_types=stage + stage + [pltpu.SemaphoreType.DMA])
    def push(xn_hbm, pa_hbm, pb_hbm, xs_hbm, ia0, ib0, rows0, lsem0, ia1, ib1, rows1, lsem1, ssem):
        wid = _sc_worker()
        mine = (n_chunks - wid + SC_WORKERS - 1) // SC_WORKERS
        bufs = ((ia0, ib0, rows0, lsem0), (ia1, ib1, rows1, lsem1))

        def loads(t, b):
            ia, ib, rows, sem = bufs[b]
            off = pl.multiple_of((wid + t * SC_WORKERS) * ch, ch)
            return (pltpu.make_async_copy(pa_hbm.at[pl.ds(off, ch)], ia, sem),
                    pltpu.make_async_copy(pb_hbm.at[pl.ds(off, ch)], ib, sem),
                    pltpu.make_async_copy(xn_hbm.at[pl.ds(off, ch)], rows, sem))

        def stage_in(t, b):
            for c in loads(t, b):
                c.start()

        def scatter(t, b):
            ia, ib, rows, _ = bufs[b]
            for c in loads(t, b):
                c.wait()
            first = pltpu.async_copy(rows, xs_hbm.at[ia], ssem)
            second = pltpu.async_copy(rows, xs_hbm.at[ib], ssem)
            first.wait()
            second.wait()

        stage_in(0, 0)

        @pl.loop(0, (max_mine + 1) // 2)
        def _(p):
            t = 2 * p

            @pl.when(t + 1 < mine)
            def _():
                stage_in(t + 1, 1)

            @pl.when(t < mine)
            def _():
                scatter(t, 0)

            @pl.when(t + 2 < mine)
            def _():
                stage_in(t + 2, 0)

            @pl.when(t + 1 < mine)
            def _():
                scatter(t + 1, 1)

    return push(xn, pos_a, pos_b)


def _sc_collect(ysorted, pos_flat, ch):
    n_pick = pos_flat.shape[0]
    per_worker = n_pick // SC_WORKERS
    n_chunks = per_worker // ch
    assert n_pick % SC_WORKERS == 0 and per_worker % ch == 0
    row_shape, dtype = ysorted.shape[1:], ysorted.dtype

    @functools.partial(
        pl.kernel, mesh=_sc_mesh(),
        out_type=jax.ShapeDtypeStruct((n_pick,) + row_shape, dtype),
        scratch_types=[pltpu.VMEM((ch,), jnp.int32), pltpu.VMEM((ch,), jnp.int32),
                       pltpu.VMEM((ch,) + row_shape, dtype), pltpu.VMEM((ch,) + row_shape, dtype),
                       pltpu.SemaphoreType.DMA, pltpu.SemaphoreType.DMA])
    def pull(ys_hbm, pos_hbm, out_hbm, idx0, idx1, rows0, rows1, sem0, sem1):
        base = _sc_worker() * per_worker
        bufs = ((idx0, rows0, sem0), (idx1, rows1, sem1))

        def offset(j):
            return pl.multiple_of(base + j * ch, SUBLANES)

        def fetch(j, b):
            idx, rows, sem = bufs[b]
            pltpu.sync_copy(pos_hbm.at[pl.ds(offset(j), ch)], idx)
            pltpu.async_copy(ys_hbm.at[idx], rows, sem)

        def flush(j, b):
            idx, rows, sem = bufs[b]
            pltpu.make_async_copy(ys_hbm.at[idx], rows, sem).wait()
            pltpu.sync_copy(rows, out_hbm.at[pl.ds(offset(j), ch)])

        fetch(0, 0)

        @pl.loop(0, n_chunks // 2)
        def _(p):
            j = 2 * p
            fetch(j + 1, 1)
            flush(j, 0)

            @pl.when(j + 2 < n_chunks)
            def _():
                fetch(j + 2, 0)

            flush(j + 1, 1)

        if n_chunks % 2:
            flush(n_chunks - 1, 0)

    return pull(ysorted, pos_flat)


def _moe_ffn_body(*refs):
    for k in range(MOE_TILES_PER_STEP):
        _moe_ffn_tile(k, *refs)


def _moe_ffn_tile(k, te_ref, nused_ref, x_ref, wg_hbm, wu_hbm, wd_hbm, y_ref,
                  wg_f32, wu_f32, wd_f32, wgb, wub, wdb, slot_ref, sems):
    i = pl.program_id(0) * MOE_TILES_PER_STEP + k
    n_used = nused_ref[0]
    window = pl.ds(k * MOE_TILE * PACK_ROWS, MOE_TILE * PACK_ROWS)
    x_ref, y_ref = x_ref.at[window, :], y_ref.at[window, :]

    def fetch(expert, slot):
        return (pltpu.make_async_copy(wg_hbm.at[expert], wg_f32.at[slot], sems.at[slot, 0]),
                pltpu.make_async_copy(wu_hbm.at[expert], wu_f32.at[slot], sems.at[slot, 1]),
                pltpu.make_async_copy(wd_hbm.at[expert], wd_f32.at[slot], sems.at[slot, 2]))

    @pl.when(i >= n_used)
    def _unused_tile():
        y_ref[...] = jnp.zeros_like(y_ref)

    @pl.when(i < n_used)
    def _tile():
        expert = te_ref[i]

        @pl.when(i == 0)
        def _first_fetch():
            slot_ref[0] = 0
            for c in fetch(expert, 0):
                c.start()

        @pl.when((i == 0) | (expert != te_ref[jnp.maximum(i - 1, 0)]))
        def _new_expert():
            slot = slot_ref[0]
            nxt = lax.while_loop(lambda k: (k < n_used) & (te_ref[jnp.minimum(k, n_used - 1)] == expert),
                                 lambda k: k + 1, i + 1)

            @pl.when(nxt < n_used)
            def _prefetch():
                for c in fetch(te_ref[jnp.minimum(nxt, n_used - 1)], 1 - slot):
                    c.start()

            for c in fetch(expert, slot):
                c.wait()
            wgb[...] = wg_f32[slot].astype(BF16)
            wub[...] = wu_f32[slot].astype(BF16)
            wdb[...] = wd_f32[slot].astype(BF16)
            slot_ref[0] = 1 - slot

        x = _unpack_bf16_pairs(x_ref, MOE_TILE).astype(BF16)
        gate = _dot(x, wgb[...])
        hmid = (gate * jax.nn.sigmoid(gate)) * _dot(x, wub[...])
        y = _dot(hmid.astype(BF16), wdb[...])
        packed = _pack_bf16_pairs(y)
        for j in range(PACK_ROWS):
            y_ref[pl.ds(j, MOE_TILE, stride=PACK_ROWS), :] = packed[:, j * LANES:(j + 1) * LANES]


def _moe_ffn(tile_expert, n_used, xsorted, w_gate, w_up, w_down):
    n_tiles = tile_expert.shape[0]
    per_step = MOE_TILES_PER_STEP
    assert n_tiles % per_step == 0
    hbm = pl.BlockSpec(memory_space=pl.ANY)
    tile = lambda imap: pl.BlockSpec((per_step * MOE_TILE * PACK_ROWS, LANES), imap)
    up_shape, down_shape = (D_MODEL, MOE_D_FF), (MOE_D_FF, D_MODEL)
    return pl.pallas_call(
        _moe_ffn_body,
        grid_spec=pltpu.PrefetchScalarGridSpec(
            num_scalar_prefetch=2,
            grid=(n_tiles // per_step,),
            in_specs=[tile(lambda i, te, nu: (jnp.clip(i, 0, jnp.maximum(nu[0] - 1, 0) // per_step), 0)),
                      hbm, hbm, hbm],
            out_specs=tile(lambda i, te, nu: (i, 0)),
            scratch_shapes=[pltpu.VMEM((2,) + up_shape, F32), pltpu.VMEM((2,) + up_shape, F32),
                            pltpu.VMEM((2,) + down_shape, F32),
                            pltpu.VMEM(up_shape, BF16), pltpu.VMEM(up_shape, BF16), pltpu.VMEM(down_shape, BF16),
                            pltpu.SMEM((1,), jnp.int32), pltpu.SemaphoreType.DMA((2, 3))]),
        out_shape=jax.ShapeDtypeStruct((n_tiles * MOE_TILE * PACK_ROWS, LANES), jnp.uint32),
        compiler_params=pltpu.CompilerParams(dimension_semantics=("arbitrary",), vmem_limit_bytes=VMEM_LIMIT),
        name="moe_ffn",
    )(tile_expert, n_used, xsorted, w_gate, w_up, w_down)


def _combine_body(x1_ref, rt_ref, ya_ref, yb_ref, nf_ref, *rest):
    out_ref = rest[-1]
    rt = rt_ref[...]
    x1 = x1_ref[...]
    tm = x1.shape[0]

    x2 = (x1 + rt[:, 2:3] * _unpack_bf16_pairs(ya_ref.at[0], tm)
          + rt[:, 3:4] * _unpack_bf16_pairs(yb_ref.at[0], tm))
    out_ref[...] = _rms(x2, nf_ref[...])


def _combine(x1, rt, y_picks, nf, tm, rows, x_block, y_block, out_rows, out_block, out_buf=None):
    row = lambda w: pl.BlockSpec((tm, w), lambda i: (i + x_block, 0))
    pick = lambda k: pl.BlockSpec((1, tm * PACK_ROWS, LANES), lambda i: (k, i + y_block, 0))
    in_specs = [row(D_MODEL), row(LANES), pick(0), pick(1), pl.BlockSpec((1, D_MODEL), lambda i: (0, 0))]
    args = [x1, rt, y_picks, y_picks, nf]
    aliases = {}
    if out_buf is not None:
        in_specs.append(pl.BlockSpec(memory_space=pl.ANY))
        aliases[len(args)] = 0
        args.append(out_buf)
    return pl.pallas_call(
        _combine_body,
        grid=(rows // tm,),
        in_specs=in_specs,
        out_specs=pl.BlockSpec((tm, D_MODEL), lambda i: (i + out_block, 0)),
        out_shape=jax.ShapeDtypeStruct((out_rows, D_MODEL), F32),
        input_output_aliases=aliases,
        compiler_params=pltpu.CompilerParams(dimension_semantics=("parallel",), vmem_limit_bytes=VMEM_LIMIT),
        name="moe_combine",
    )(*args)


def _s5_tables(a_re, a_im, log_dt, b_re, b_im, c_re, c_im):
    dt = jnp.exp(log_dt)[:, None]
    mag = jnp.exp(a_re * dt)
    ab_re = mag * jnp.cos(a_im * dt)
    ab_im = mag * jnp.sin(a_im * dt)
    den = a_re * a_re + a_im * a_im
    nr = ab_re - 1.0
    q_re = (nr * a_re + ab_im * a_im) / den
    q_im = (ab_im * a_re - nr * a_im) / den
    bb_re = q_re[..., None] * b_re - q_im[..., None] * b_im
    bb_im = q_re[..., None] * b_im + q_im[..., None] * b_re
    nblk = S5_GROUPS // 16
    kw, nw = 16 * S5_GROUP_CH, 16 * S5_STATE
    same_group = (jnp.arange(kw)[:, None] // S5_GROUP_CH) == (jnp.arange(nw)[None, :] // S5_STATE)

    def in_map(bb):
        rows = bb.reshape(nblk, 16, S5_STATE, S5_GROUP_CH).transpose(0, 1, 3, 2).reshape(nblk, kw, S5_STATE)
        return jnp.where(same_group, jnp.tile(rows, (1, 1, 16)), 0.0)

    def out_map(cc):
        cols = cc.reshape(nblk, 16, S5_GROUP_CH, S5_STATE).transpose(0, 3, 1, 2).reshape(nblk, S5_STATE, kw)
        return jnp.where(same_group.T, jnp.tile(cols, (1, 16, 1)), 0.0)

    wb = jnp.concatenate([in_map(bb_re), in_map(bb_im)], axis=-1).astype(BF16)
    return (wb, ab_re.reshape(1, S5_LANES), ab_im.reshape(1, S5_LANES),
            out_map(c_re).astype(BF16), out_map(-c_im).astype(BF16))


def kernel(x_prompt, x_sample, state_ssd_conv, state_ssd_ssm, state_s5_re, state_s5_im, meta_tokens, norm_mix, w_in, conv_w, conv_b, dt_bias, a_log, d_ssd, ssd_norm, s5_a_re, s5_a_im, s5_log_dt, s5_b_re, s5_b_im, s5_c_re, s5_c_im, s5_d, w_glu, b_glu, s5_norm, w_out, norm_ffn, router_coarse_w, router_coarse_b, router_fine_w, router_fine_b, w_gate, w_up, w_down, norm_final):
    bp, seq, _ = x_prompt.shape
    bs = x_sample.shape[0]
    n_prompt = bp * seq
    n_tok = n_prompt + bs
    row2 = lambda v: v.reshape(1, -1)
    pad_heads = lambda v: jnp.pad(v, (0, LANES - SSD_HEADS)).reshape(1, LANES)

    w = w_in[0]
    o1, o2, o3 = SSD_WIDTH, SSD_WIDTH + SSD_CONV_DIM, SSD_WIDTH + SSD_CONV_DIM + SSD_HEADS
    wz, wx, wu = w[:, :o1].astype(BF16), w[:, o1:o2].astype(BF16), w[:, o3:].astype(BF16)
    wdt = jnp.pad(w[:, o2:o3], ((0, 0), (0, LANES - SSD_HEADS))).astype(BF16)
    g_mix = row2(norm_mix[0])
    cw, cb = conv_w[0], row2(conv_b[0])
    dtb, alog = pad_heads(dt_bias[0]), pad_heads(a_log[0])
    dexp = row2(jnp.repeat(d_ssd[0], SSD_HEAD_DIM))
    snrm = row2(ssd_norm[0])
    eexp = (jnp.arange(LANES)[:, None] == (jnp.arange(SSD_WIDTH) // SSD_HEAD_DIM)[None, :]).astype(BF16)
    wb5, ab_re, ab_im, wcr, wci = _s5_tables(s5_a_re[0], s5_a_im[0], s5_log_dt[0], s5_b_re[0], s5_b_im[0],
                                             s5_c_re[0], s5_c_im[0])
    d5, wglu, bglu, nrm5 = row2(s5_d[0]), w_glu[0].astype(BF16), row2(b_glu[0]), row2(s5_norm[0])
    wo_a, wo_b = w_out[0][:SSD_WIDTH].astype(BF16), w_out[0][SSD_WIDTH:].astype(BF16)
    w_r = jnp.concatenate([router_coarse_w[0], router_fine_w[0].transpose(1, 0, 2).reshape(D_MODEL, MOE_EXPERTS)], axis=1)
    w_r = jnp.pad(w_r, ((0, 0), (0, LANES - w_r.shape[1])))
    wrh = w_r.astype(BF16)
    wrl = (w_r - wrh.astype(F32)).astype(BF16)
    b_r = jnp.concatenate([router_coarse_b[0], router_fine_b[0].reshape(-1)])
    b_r = jnp.pad(b_r, (0, LANES - b_r.shape[0])).reshape(1, LANES)

    zp, xbcp, dtp, up = _in_proj(x_prompt.reshape(n_prompt, D_MODEL), g_mix, wz, wx, wdt, wu, IN_PROJ_TILE, BF16, F32)
    xsm = jnp.concatenate([x_sample.reshape(bs, D_MODEL), meta_tokens], axis=0)
    zs, xbcs, dts, us = _in_proj(xsm, g_mix, wz, wx, wdt, wu, xsm.shape[0], F32, F32)

    front = SSD_CHUNK - N_META
    padf = lambda a: jnp.pad(a[bs:], ((front, 0), (0, 0)))[None]
    gw = SSD_HPG * SSD_HEAD_DIM
    ssd_consts = (cw, cb, dtb, alog, dexp, snrm, eexp)
    _, ctail_m, _, ht_m = _ssd_chunked(
        padf(xbcs).astype(BF16), padf(dts), jnp.zeros((1, SSD_CHUNK, SSD_WIDTH), F32),
        jnp.zeros((1, SUBLANES, SSD_CONV_DIM), F32), jnp.zeros((1, SSD_GROUPS, SSD_STATE, gw), F32),
        *ssd_consts, mask_rows=front)
    y_ssd_p, ctail_p, ssm_p, _ = _ssd_chunked(
        xbcp.reshape(bp, seq, SSD_CONV_DIM), dtp.reshape(bp, seq, LANES), zp.reshape(bp, seq, SSD_WIDTH),
        ctail_m, ht_m, *ssd_consts, mask_rows=0)

    abr8, abi8 = jnp.broadcast_to(ab_re, (bp, S5_LANES)), jnp.broadcast_to(ab_im, (bp, S5_LANES))
    um8 = jnp.repeat(us[bs:], bp, axis=0).astype(BF16)
    y_s5_p, s5re_p, s5im_p = _s5_seq(up.reshape(bp, seq, S5_WIDTH), um8, wb5, abr8, abi8,
                                     wcr, wci, d5, wglu, bglu, nrm5)

    cst = state_ssd_conv[0]
    xt_s, dt_s, dec_s, bc, xs_s = _ssd_step_prep(xbcs[:bs], cst[:, 0], cst[:, 1], cst[:, 2], dts[:bs],
                                                 cw, cb, dtb, alog)
    ssm_s, y_core = _ssd_step(dt_s[:, :SSD_HEADS].reshape(-1), dec_s[:, :SSD_HEADS].reshape(-1),
                              state_ssd_ssm[0], xt_s, bc)
    y_ssd_s, y_s5_s, s5re_s, s5im_s = _sample_post(
        y_core, xs_s, zs[:bs], dexp, snrm, us[:bs], state_s5_re[0].reshape(bs, S5_LANES),
        state_s5_im[0].reshape(bs, S5_LANES), wb5, ab_re, ab_im, wcr, wci, d5, wglu, bglu, nrm5)

    route_consts = (wo_a, wo_b, row2(norm_ffn[0]), wrh, wrl, b_r)
    n_tiles = -(-2 * n_tok // MOE_TILE) + MOE_EXPERTS
    n_tiles = -(-n_tiles // MOE_TILES_PER_STEP) * MOE_TILES_PER_STEP
    x1, xn, rt, pos, meta = _mix_route(
        (x_prompt.reshape(n_prompt, D_MODEL), y_ssd_p.reshape(n_prompt, SSD_WIDTH), y_s5_p.reshape(n_prompt, S5_WIDTH)),
        (x_sample.reshape(bs, D_MODEL), y_ssd_s, y_s5_s), route_consts, IN_PROJ_TILE, n_tiles)

    pos_a, pos_b = pos[0], pos[1]
    tile_expert, n_used = meta[0, :n_tiles], meta[1, :1]
    xsorted = _sc_dispatch(xn, pos_a, pos_b, n_tiles * MOE_TILE)
    ysorted = _moe_ffn(tile_expert, n_used, xsorted.reshape(-1, LANES), w_gate[0], w_up[0], w_down[0])
    nfin = row2(norm_final)

    half = n_prompt // 2

    def collect(lo, hi, ch):
        picks = jnp.concatenate([pos_a[lo:hi], pos_b[lo:hi]])
        packed_rows = ysorted.reshape(-1, PACK_ROWS, LANES)
        return _sc_collect(packed_rows, picks, ch).reshape(2, (hi - lo) * PACK_ROWS, LANES)

    picks_1 = collect(0, half, SC_COLLECT_ROWS[0])
    picks_2 = collect(half, n_tok, SC_COLLECT_ROWS[1])
    blocks = half // TOK_TILE
    y_p = _combine(x1, rt, picks_1, nfin, TOK_TILE, half, 0, 0, n_prompt, 0)
    y_p = _combine(x1, rt, picks_2, nfin, TOK_TILE, half, blocks, 0, n_prompt, blocks, out_buf=y_p)
    y_s = _combine(x1, rt, picks_2, nfin, bs, bs, n_prompt // bs, half // bs, bs, 0)

    s5_state = lambda a, b: a.reshape(1, b, S5_GROUPS, S5_STATE)
    new_conv_s = jnp.stack([cst[:, 1], cst[:, 2], xbcs[:bs]], axis=1)[None]
    return (y_p.reshape(bp, seq, D_MODEL), y_s.reshape(bs, 1, D_MODEL),
            ctail_p[:, SUBLANES - (SSD_CONV - 1):][None], ssm_p[None], s5_state(s5re_p, bp), s5_state(s5im_p, bp),
            new_conv_s, ssm_s[None], s5_state(s5re_s, bs), s5_state(s5im_s, bs))
```
